```python
import math
import jax, jax.numpy as jnp
from jax import lax
import numpy as np

D_MODEL = 2048
BATCH = 8
SEQ = 8192
DEPTH = 1

CHUNK = 64
Q_BLOCK = 128
FOX_HEADS = 8
FOX_HEAD_DIM = 128
FOX_WIDTH = FOX_HEADS * FOX_HEAD_DIM
GDN_HEADS = 8
GDN_HEAD_DIM = 128
GDN_WIDTH = GDN_HEADS * GDN_HEAD_DIM
MIX_WIDTH = FOX_WIDTH + GDN_WIDTH
CONV_WIDTH = 4
EPS = 1e-6
IN_SIZES = (FOX_WIDTH, FOX_WIDTH, FOX_WIDTH, FOX_WIDTH, FOX_HEADS,
            GDN_WIDTH, GDN_WIDTH, GDN_WIDTH, GDN_WIDTH, GDN_HEADS, GDN_HEADS)
IN_WIDTH = 4 * FOX_WIDTH + FOX_HEADS + 4 * GDN_WIDTH + 2 * GDN_HEADS

kernel_name = "hybrid_fox_gdn_adaln_block"


def rmsnorm(x, g):
    xf = x.astype(jnp.float32)
    y = xf * lax.rsqrt(jnp.mean(xf * xf, axis=-1, keepdims=True) + EPS)
    return (y * g.astype(jnp.float32)).astype(x.dtype)


def l2norm(x):
    xf = x.astype(jnp.float32)
    return xf * lax.rsqrt(jnp.sum(xf * xf, axis=-1, keepdims=True) + EPS)


def to_heads(t, n_heads):
    b, s, w = t.shape
    return t.reshape(b, s, n_heads, w // n_heads).transpose(0, 2, 1, 3)


def from_heads(t):
    b, n, s, d = t.shape
    return t.transpose(0, 2, 1, 3).reshape(b, s, n * d)


def split_cols(p):
    idx = np.cumsum(np.array(IN_SIZES))[:-1].tolist()
    return jnp.split(p, idx, axis=-1)


def forgetting_attention(q, k, v, f_logit, b_f, qn_g, kn_g):
    b, s, _ = q.shape
    qh = rmsnorm(to_heads(q, FOX_HEADS), qn_g).astype(jnp.float32)
    kh = rmsnorm(to_heads(k, FOX_HEADS), kn_g).astype(jnp.float32)
    vh = to_heads(v, FOX_HEADS).astype(jnp.float32)
    log_f = jax.nn.log_sigmoid(f_logit.astype(jnp.float32) + b_f.astype(jnp.float32))
    F = jnp.cumsum(log_f, axis=1).transpose(0, 2, 1)
    nq = s // Q_BLOCK
    qb = qh.reshape(b, FOX_HEADS, nq, Q_BLOCK, FOX_HEAD_DIM).transpose(2, 0, 1, 3, 4)
    Fb = F.reshape(b, FOX_HEADS, nq, Q_BLOCK).transpose(2, 0, 1, 3)
    tb = jnp.arange(s, dtype=jnp.int32).reshape(nq, Q_BLOCK)
    s_pos = jnp.arange(s, dtype=jnp.int32)
    scale = FOX_HEAD_DIM ** -0.5

    def block(args):
        q_i, F_i, t_i = args
        logits = (jnp.einsum('bhqd,bhkd->bhqk', q_i, kh) * scale
                  + (F_i[..., :, None] - F[..., None, :]))
        mask = s_pos[None, :] <= t_i[:, None]
        logits = jnp.where(mask, logits, -jnp.inf)
        p = jax.nn.softmax(logits, axis=-1)
        return jnp.einsum('bhqk,bhkd->bhqd', p, vh)

    o = lax.map(block, (qb, Fb, tb))
    return o.transpose(1, 0, 3, 2, 4).reshape(b, s, FOX_WIDTH)


def causal_depthwise_conv(x, w):
    c = x.shape[-1]
    return lax.conv_general_dilated(
        x, w[:, None, :], window_strides=(1,), padding=[(CONV_WIDTH - 1, 0)],
        dimension_numbers=('NWC', 'WIO', 'NWC'), feature_group_count=c)


def chunk_gated_delta_rule(q, k, v, g, beta):
    b, h, s, dk = q.shape
    dv = v.shape[-1]
    n = s // CHUNK
    q = q.reshape(b, h, n, CHUNK, dk)
    k = k.reshape(b, h, n, CHUNK, dk)
    v = v.reshape(b, h, n, CHUNK, dv)
    g = g.reshape(b, h, n, CHUNK)
    beta = beta.reshape(b, h, n, CHUNK)
    gc = jnp.cumsum(g, axis=-1)
    idx = jnp.arange(CHUNK)
    lower = idx[:, None] >= idx[None, :]
    strict = idx[:, None] > idx[None, :]
    decay = jnp.exp(jnp.where(lower, gc[..., :, None] - gc[..., None, :], -jnp.inf))
    kb = k * beta[..., None]
    vb = v * beta[..., None]
    M = jnp.where(strict, jnp.einsum('bhncd,bhnsd->bhncs', kb, k) * decay, 0.0)
    A = M + jnp.eye(CHUNK, dtype=jnp.float32)
    rhs = jnp.concatenate([vb, kb * jnp.exp(gc)[..., None]], axis=-1)
    sol = lax.linalg.triangular_solve(A, rhs, left_side=True, lower=True, unit_diagonal=True)
    u, w = sol[..., :dv], sol[..., dv:]
    attn = jnp.where(lower, jnp.einsum('bhncd,bhnsd->bhncs', q, k) * decay, 0.0)

    def step(state, xs):
        q_i, k_i, u_i, w_i, gc_i, attn_i = xs
        v_new = u_i - jnp.einsum('bhcd,bhde->bhce', w_i, state)
        o_i = (jnp.einsum('bhcd,bhde->bhce', q_i * jnp.exp(gc_i)[..., None], state)
               + jnp.einsum('bhcs,bhse->bhce', attn_i, v_new))
        g_last = gc_i[..., -1]
        k_dec = k_i * jnp.exp(g_last[..., None] - gc_i)[..., None]
        new_state = state * jnp.exp(g_last)[..., None, None] + jnp.einsum('bhcd,bhce->bhde', k_dec, v_new)
        return new_state, o_i

    mv = lambda t: jnp.moveaxis(t, 2, 0)
    state0 = jnp.zeros((b, h, dk, dv), jnp.float32)
    _, o = lax.scan(step, state0, (mv(q), mv(k), mv(u), mv(w), mv(gc), mv(attn)))
    return o.transpose(1, 2, 0, 3, 4).reshape(b, h, s, dv)


def gated_deltanet(q, k, v, a, bt, conv_w, A_log, dt_bias, norm_g):
    qkv = jnp.concatenate([q, k, v], axis=-1).astype(jnp.float32)
    qkv = jax.nn.silu(causal_depthwise_conv(qkv, conv_w.astype(jnp.float32)))
    q, k, v = jnp.split(qkv, 3, axis=-1)
    qh = l2norm(to_heads(q, GDN_HEADS)) * (GDN_HEAD_DIM ** -0.5)
    kh = l2norm(to_heads(k, GDN_HEADS))
    vh = to_heads(v, GDN_HEADS)
    beta = jax.nn.sigmoid(bt.astype(jnp.float32)).transpose(0, 2, 1)
    g = (-jnp.exp(A_log.astype(jnp.float32))
         * jax.nn.softplus(a.astype(jnp.float32) + dt_bias.astype(jnp.float32))).transpose(0, 2, 1)
    o = chunk_gated_delta_rule(qh, kh, vh, g, beta)
    o = rmsnorm(o, norm_g)
    return from_heads(o)


def _fwd_setup_inputs(seed: int = 0) -> dict:
    key = jax.random.key(seed)
    ks = jax.random.split(key, 16)
    f32 = jnp.float32
    x = jax.random.normal(ks[0], (BATCH, SEQ, D_MODEL), f32)
    c = jax.random.normal(ks[1], (BATCH, D_MODEL), f32)
    norm_g = 1.0 + 0.02 * jax.random.normal(ks[2], (DEPTH, D_MODEL), f32)
    w_ada = 0.5 * D_MODEL ** -0.5 * jax.random.normal(ks[3], (DEPTH, D_MODEL, 3 * D_MODEL), f32)
    b_ada = 0.01 * jax.random.normal(ks[4], (DEPTH, 3 * D_MODEL), f32)
    w_in = D_MODEL ** -0.5 * jax.random.normal(ks[5], (DEPTH, D_MODEL, IN_WIDTH), f32)
    b_fgate = jax.random.uniform(ks[6], (DEPTH, FOX_HEADS), f32, 1.0, 4.0)
    fox_qn_g = 1.0 + 0.02 * jax.random.normal(ks[7], (DEPTH, FOX_HEAD_DIM), f32)
    fox_kn_g = 1.0 + 0.02 * jax.random.normal(ks[8], (DEPTH, FOX_HEAD_DIM), f32)
    gdn_conv_w = CONV_WIDTH ** -0.5 * jax.random.normal(ks[9], (DEPTH, CONV_WIDTH, 3 * GDN_WIDTH), f32)
    gdn_A_log = jnp.log(jax.random.uniform(ks[10], (DEPTH, GDN_HEADS), f32, 1.0, 16.0))
    dt = jnp.exp(jax.random.uniform(ks[11], (DEPTH, GDN_HEADS), f32, math.log(1e-3), math.log(1e-1)))
    gdn_dt_bias = dt + jnp.log(-jnp.expm1(-dt))
    gdn_norm_g = 1.0 + 0.02 * jax.random.normal(ks[12], (DEPTH, GDN_HEAD_DIM), f32)
    w_out = MIX_WIDTH ** -0.5 * jax.random.normal(ks[13], (DEPTH, MIX_WIDTH, D_MODEL), f32)
    final_g = 1.0 + 0.02 * jax.random.normal(ks[14], (D_MODEL,), f32)
    return {"x": x, "c": c, "norm_g": norm_g, "w_ada": w_ada, "b_ada": b_ada,
            "w_in": w_in, "b_fgate": b_fgate, "fox_qn_g": fox_qn_g, "fox_kn_g": fox_kn_g,
            "gdn_conv_w": gdn_conv_w, "gdn_A_log": gdn_A_log, "gdn_dt_bias": gdn_dt_bias,
            "gdn_norm_g": gdn_norm_g, "w_out": w_out, "final_g": final_g}


def _fwd_reference(x, c, norm_g, w_ada, b_ada, w_in, b_fgate, fox_qn_g, fox_kn_g,
              gdn_conv_w, gdn_A_log, gdn_dt_bias, gdn_norm_g, w_out, final_g):
    c_act = jax.nn.silu(c)
    for l in range(DEPTH):
        mod = c_act @ w_ada[l] + b_ada[l]
        shift, scale, gate = jnp.split(mod, 3, axis=-1)
        h = rmsnorm(x, norm_g[l]) * (1.0 + scale[:, None, :]) + shift[:, None, :]
        p = h @ w_in[l]
        fq, fk, fv, fz, ff, gq, gk, gv, gz, ga, gb = split_cols(p)
        fox_o = forgetting_attention(fq, fk, fv, ff, b_fgate[l], fox_qn_g[l], fox_kn_g[l])
        fox_o = fox_o * jax.nn.silu(fz.astype(jnp.float32))
        gdn_o = gated_deltanet(gq, gk, gv, ga, gb, gdn_conv_w[l], gdn_A_log[l],
                               gdn_dt_bias[l], gdn_norm_g[l])
        gdn_o = gdn_o * jax.nn.silu(gz.astype(jnp.float32))
        mixed = jnp.concatenate([fox_o, gdn_o], axis=-1).astype(x.dtype)
        x = x + gate[:, None, :] * (mixed @ w_out[l])
    return rmsnorm(x, final_g)


import jax as _jax
import jax.numpy as _jnp

TWIN_FORMAT = 'train_step'
FWD_PARAMS = ['x', 'c', 'norm_g', 'w_ada', 'b_ada', 'w_in', 'b_fgate', 'fox_qn_g', 'fox_kn_g', 'gdn_conv_w', 'gdn_A_log', 'gdn_dt_bias', 'gdn_norm_g', 'w_out', 'final_g']
TWIN_WEIGHTS = ['norm_g', 'w_ada', 'b_ada', 'w_in', 'b_fgate', 'fox_qn_g', 'fox_kn_g', 'gdn_conv_w', 'gdn_A_log', 'gdn_dt_bias', 'gdn_norm_g', 'w_out', 'final_g']
TWIN_DIFF_INPUT = 'x'
TWIN_INPUTS = ['x', 'c', 'norm_g', 'w_ada', 'b_ada', 'w_in', 'b_fgate', 'fox_qn_g', 'fox_kn_g', 'gdn_conv_w', 'gdn_A_log', 'gdn_dt_bias', 'gdn_norm_g', 'w_out', 'final_g', 'loss_target', 'm_norm_g', 'm_w_ada', 'm_b_ada', 'm_w_in', 'm_b_fgate', 'm_fox_qn_g', 'm_fox_kn_g', 'm_gdn_conv_w', 'm_gdn_A_log', 'm_gdn_dt_bias', 'm_gdn_norm_g', 'm_w_out', 'm_final_g', 'v_norm_g', 'v_w_ada', 'v_b_ada', 'v_w_in', 'v_b_fgate', 'v_fox_qn_g', 'v_fox_kn_g', 'v_gdn_conv_w', 'v_gdn_A_log', 'v_gdn_dt_bias', 'v_gdn_norm_g', 'v_w_out', 'v_final_g']
TWIN_OUTPUTS = ['loss', 'grad_x', 'grad_norm_g', 'grad_w_ada', 'grad_b_ada', 'grad_w_in', 'grad_b_fgate', 'grad_fox_qn_g', 'grad_fox_kn_g', 'grad_gdn_conv_w', 'grad_gdn_A_log', 'grad_gdn_dt_bias', 'grad_gdn_norm_g', 'grad_w_out', 'grad_final_g', 'delta_norm_g', 'delta_w_ada', 'delta_b_ada', 'delta_w_in', 'delta_b_fgate', 'delta_fox_qn_g', 'delta_fox_kn_g', 'delta_gdn_conv_w', 'delta_gdn_A_log', 'delta_gdn_dt_bias', 'delta_gdn_norm_g', 'delta_w_out', 'delta_final_g', 'new_m_norm_g', 'new_m_w_ada', 'new_m_b_ada', 'new_m_w_in', 'new_m_b_fgate', 'new_m_fox_qn_g', 'new_m_fox_kn_g', 'new_m_gdn_conv_w', 'new_m_gdn_A_log', 'new_m_gdn_dt_bias', 'new_m_gdn_norm_g', 'new_m_w_out', 'new_m_final_g', 'new_v_norm_g', 'new_v_w_ada', 'new_v_b_ada', 'new_v_w_in', 'new_v_b_fgate', 'new_v_fox_qn_g', 'new_v_fox_kn_g', 'new_v_gdn_conv_w', 'new_v_gdn_A_log', 'new_v_gdn_dt_bias', 'new_v_gdn_norm_g', 'new_v_w_out', 'new_v_final_g']
TWIN_LEAF_KINDS = {'loss': 'loss', 'grad_x': 'grad_x', 'grad_norm_g': 'grad_w', 'grad_w_ada': 'grad_w', 'grad_b_ada': 'grad_w', 'grad_w_in': 'grad_w', 'grad_b_fgate': 'grad_w', 'grad_fox_qn_g': 'grad_w', 'grad_fox_kn_g': 'grad_w', 'grad_gdn_conv_w': 'grad_w', 'grad_gdn_A_log': 'grad_w', 'grad_gdn_dt_bias': 'grad_w', 'grad_gdn_norm_g': 'grad_w', 'grad_w_out': 'grad_w', 'grad_final_g': 'grad_w', 'delta_norm_g': 'delta_w', 'delta_w_ada': 'delta_w', 'delta_b_ada': 'delta_w', 'delta_w_in': 'delta_w', 'delta_b_fgate': 'delta_w', 'delta_fox_qn_g': 'delta_w', 'delta_fox_kn_g': 'delta_w', 'delta_gdn_conv_w': 'delta_w', 'delta_gdn_A_log': 'delta_w', 'delta_gdn_dt_bias': 'delta_w', 'delta_gdn_norm_g': 'delta_w', 'delta_w_out': 'delta_w', 'delta_final_g': 'delta_w', 'new_m_norm_g': 'new_m', 'new_m_w_ada': 'new_m', 'new_m_b_ada': 'new_m', 'new_m_w_in': 'new_m', 'new_m_b_fgate': 'new_m', 'new_m_fox_qn_g': 'new_m', 'new_m_fox_kn_g': 'new_m', 'new_m_gdn_conv_w': 'new_m', 'new_m_gdn_A_log': 'new_m', 'new_m_gdn_dt_bias': 'new_m', 'new_m_gdn_norm_g': 'new_m', 'new_m_w_out': 'new_m', 'new_m_final_g': 'new_m', 'new_v_norm_g': 'new_v', 'new_v_w_ada': 'new_v', 'new_v_b_ada': 'new_v', 'new_v_w_in': 'new_v', 'new_v_b_fgate': 'new_v', 'new_v_fox_qn_g': 'new_v', 'new_v_fox_kn_g': 'new_v', 'new_v_gdn_conv_w': 'new_v', 'new_v_gdn_A_log': 'new_v', 'new_v_gdn_dt_bias': 'new_v', 'new_v_gdn_norm_g': 'new_v', 'new_v_w_out': 'new_v', 'new_v_final_g': 'new_v'}


def _forward(args):
    return _fwd_reference(*[args[k] for k in FWD_PARAMS])


def _output_shape():
    def fwd():
        inp = _fwd_setup_inputs(0)
        return _fwd_reference(*[inp[k] for k in FWD_PARAMS])
    out = _jax.eval_shape(fwd)
    return out.shape, out.dtype

N_MICROBATCH = 1
ADAM_LR = 0.001
ADAM_B1 = 0.9
ADAM_B2 = 0.999
ADAM_EPS = 1e-08
ADAM_WD = 0.01
ADAM_STEP = 10
PER_EXAMPLE_BATCH_AXIS = {'x': 0, 'c': 0, 'loss_target': 0}
SHARED_INPUTS = []
_WEIGHT_DTYPES = {'norm_g': _jnp.float32, 'w_ada': _jnp.float32, 'b_ada': _jnp.float32, 'w_in': _jnp.float32, 'b_fgate': _jnp.float32, 'fox_qn_g': _jnp.float32, 'fox_kn_g': _jnp.float32, 'gdn_conv_w': _jnp.float32, 'gdn_A_log': _jnp.float32, 'gdn_dt_bias': _jnp.float32, 'gdn_norm_g': _jnp.float32, 'w_out': _jnp.float32, 'final_g': _jnp.float32}
MOMENT_SCALE = {'norm_g': 2.942522e-02, 'w_ada': 2.759204e-02, 'b_ada': 4.690230e-02, 'w_in': 1.573817e-02, 'b_fgate': 7.036533e-02, 'fox_qn_g': 1.898502e-02, 'fox_kn_g': 1.873995e-02, 'gdn_conv_w': 1.834932e-02, 'gdn_A_log': 1.324951e-01, 'gdn_dt_bias': 1.304455e-01, 'gdn_norm_g': 8.818055e-02, 'w_out': 1.892702e-02, 'final_g': 3.194716e+01}


def _to_microbatches(a, axis):
    t = _jnp.moveaxis(a, axis, 0)
    t = t.reshape((N_MICROBATCH, t.shape[0] // N_MICROBATCH) + t.shape[1:])
    return _jnp.moveaxis(t, 1, axis + 1)


def setup_inputs(seed: int = 0) -> dict:
    inp = _fwd_setup_inputs(seed)
    key = _jax.random.fold_in(_jax.random.key(seed), 7919)
    shape, _ = _output_shape()
    out = dict(inp)
    out["loss_target"] = _jax.random.normal(_jax.random.fold_in(key, 0), shape, _jnp.float32)
    for i, name in enumerate(TWIN_WEIGHTS):
        w = inp[name].astype(_jnp.float32)
        if MOMENT_SCALE is None:
            s = _jnp.sqrt(_jnp.mean(_jnp.square(w)) + 1e-30)
        else:
            s = MOMENT_SCALE[name]
        km, kv = _jax.random.split(_jax.random.fold_in(key, i + 1))
        out[name] = w
        out["m_" + name] = s * _jax.random.normal(km, w.shape, _jnp.float32)
        out["v_" + name] = (s * s) * _jax.random.uniform(kv, w.shape, _jnp.float32, 0.5, 1.5)
    if N_MICROBATCH > 1:
        for name, axis in PER_EXAMPLE_BATCH_AXIS.items():
            out[name] = _to_microbatches(out[name], axis)
    return {'x': out['x'], 'c': out['c'], 'norm_g': out['norm_g'], 'w_ada': out['w_ada'], 'b_ada': out['b_ada'], 'w_in': out['w_in'], 'b_fgate': out['b_fgate'], 'fox_qn_g': out['fox_qn_g'], 'fox_kn_g': out['fox_kn_g'], 'gdn_conv_w': out['gdn_conv_w'], 'gdn_A_log': out['gdn_A_log'], 'gdn_dt_bias': out['gdn_dt_bias'], 'gdn_norm_g': out['gdn_norm_g'], 'w_out': out['w_out'], 'final_g': out['final_g'], 'loss_target': out['loss_target'], 'm_norm_g': out['m_norm_g'], 'm_w_ada': out['m_w_ada'], 'm_b_ada': out['m_b_ada'], 'm_w_in': out['m_w_in'], 'm_b_fgate': out['m_b_fgate'], 'm_fox_qn_g': out['m_fox_qn_g'], 'm_fox_kn_g': out['m_fox_kn_g'], 'm_gdn_conv_w': out['m_gdn_conv_w'], 'm_gdn_A_log': out['m_gdn_A_log'], 'm_gdn_dt_bias': out['m_gdn_dt_bias'], 'm_gdn_norm_g': out['m_gdn_norm_g'], 'm_w_out': out['m_w_out'], 'm_final_g': out['m_final_g'], 'v_norm_g': out['v_norm_g'], 'v_w_ada': out['v_w_ada'], 'v_b_ada': out['v_b_ada'], 'v_w_in': out['v_w_in'], 'v_b_fgate': out['v_b_fgate'], 'v_fox_qn_g': out['v_fox_qn_g'], 'v_fox_kn_g': out['v_fox_kn_g'], 'v_gdn_conv_w': out['v_gdn_conv_w'], 'v_gdn_A_log': out['v_gdn_A_log'], 'v_gdn_dt_bias': out['v_gdn_dt_bias'], 'v_gdn_norm_g': out['v_gdn_norm_g'], 'v_w_out': out['v_w_out'], 'v_final_g': out['v_final_g']}


def _loss(weights, diff, rest, loss_target):
    with _jax.named_scope("forward"):
        args = {**rest, TWIN_DIFF_INPUT: diff, **{k: w.astype(_WEIGHT_DTYPES[k]) for k, w in weights.items()}}
        y = _forward(args)
    with _jax.named_scope("loss_head"):
        err = _jnp.square(y.astype(_jnp.float32) - loss_target)
        return 0.5 * _jnp.sum(_jnp.mean(err, axis=-1)) if err.ndim else 0.5 * err


def _adamw(w, g, m, v):
    m = ADAM_B1 * m + (1.0 - ADAM_B1) * g
    v = ADAM_B2 * v + (1.0 - ADAM_B2) * _jnp.square(g)
    m_hat = m / (1.0 - ADAM_B1 ** ADAM_STEP)
    v_hat = v / (1.0 - ADAM_B2 ** ADAM_STEP)
    delta = -ADAM_LR * (m_hat / (_jnp.sqrt(v_hat) + ADAM_EPS) + ADAM_WD * w)
    return delta, m, v


def reference(x, c, norm_g, w_ada, b_ada, w_in, b_fgate, fox_qn_g, fox_kn_g, gdn_conv_w, gdn_A_log, gdn_dt_bias, gdn_norm_g, w_out, final_g, loss_target, m_norm_g, m_w_ada, m_b_ada, m_w_in, m_b_fgate, m_fox_qn_g, m_fox_kn_g, m_gdn_conv_w, m_gdn_A_log, m_gdn_dt_bias, m_gdn_norm_g, m_w_out, m_final_g, v_norm_g, v_w_ada, v_b_ada, v_w_in, v_b_fgate, v_fox_qn_g, v_fox_kn_g, v_gdn_conv_w, v_gdn_A_log, v_gdn_dt_bias, v_gdn_norm_g, v_w_out, v_final_g):
    given = dict(x=x, c=c, norm_g=norm_g, w_ada=w_ada, b_ada=b_ada, w_in=w_in, b_fgate=b_fgate, fox_qn_g=fox_qn_g, fox_kn_g=fox_kn_g, gdn_conv_w=gdn_conv_w, gdn_A_log=gdn_A_log, gdn_dt_bias=gdn_dt_bias, gdn_norm_g=gdn_norm_g, w_out=w_out, final_g=final_g, loss_target=loss_target, m_norm_g=m_norm_g, m_w_ada=m_w_ada, m_b_ada=m_b_ada, m_w_in=m_w_in, m_b_fgate=m_b_fgate, m_fox_qn_g=m_fox_qn_g, m_fox_kn_g=m_fox_kn_g, m_gdn_conv_w=m_gdn_conv_w, m_gdn_A_log=m_gdn_A_log, m_gdn_dt_bias=m_gdn_dt_bias, m_gdn_norm_g=m_gdn_norm_g, m_w_out=m_w_out, m_final_g=m_final_g, v_norm_g=v_norm_g, v_w_ada=v_w_ada, v_b_ada=v_b_ada, v_w_in=v_w_in, v_b_fgate=v_b_fgate, v_fox_qn_g=v_fox_qn_g, v_fox_kn_g=v_fox_kn_g, v_gdn_conv_w=v_gdn_conv_w, v_gdn_A_log=v_gdn_A_log, v_gdn_dt_bias=v_gdn_dt_bias, v_gdn_norm_g=v_gdn_norm_g, v_w_out=v_w_out, v_final_g=v_final_g)
    weights = {n: given[n] for n in TWIN_WEIGHTS}
    shared = {n: given[n] for n in SHARED_INPUTS}
    per_example = {n: given[n] for n in ['x', 'c']}
    grad_fn = _jax.value_and_grad(_loss, argnums=(0, 1))

    def one_microbatch(ex, loss_target):
        ex = dict(ex)
        diff = ex.pop(TWIN_DIFF_INPUT)
        return grad_fn(weights, diff, {**shared, **ex}, loss_target)

    if N_MICROBATCH == 1:
        loss, (grad_w, grad_x) = one_microbatch(per_example, given["loss_target"])
    else:
        def body(carry, xs):
            loss_sum, grad_sum = carry
            l_k, (gw_k, gx_k) = one_microbatch(xs[0], xs[1])
            with _jax.named_scope("update"):
                return (loss_sum + l_k, _jax.tree.map(_jnp.add, grad_sum, gw_k)), gx_k

        init = (_jnp.zeros((), _jnp.float32), _jax.tree.map(_jnp.zeros_like, weights))
        (loss, grad_w), grad_x = _jax.lax.scan(body, init, (per_example, given["loss_target"]))
    with _jax.named_scope("update"):
        delta_w, new_m, new_v = {}, {}, {}
        for n in TWIN_WEIGHTS:
            delta_w[n], new_m[n], new_v[n] = _adamw(weights[n], grad_w[n], given["m_" + n], given["v_" + n])
    return (loss, grad_x, *[grad_w[n] for n in TWIN_WEIGHTS], *[delta_w[n] for n in TWIN_WEIGHTS],
            *[new_m[n] for n in TWIN_WEIGHTS], *[new_v[n] for n in TWIN_WEIGHTS])
```

```python
import functools

import jax
import jax.numpy as jnp
from jax import lax
from jax.experimental import pallas as pl
from jax.experimental.pallas import tpu as pltpu

F32 = jnp.float32
MXU_DTYPE = jnp.bfloat16
HIGHEST = lax.Precision.HIGHEST
MESH_ID = pl.DeviceIdType.MESH

D_MODEL = 2048
N_HEADS = 8
D_HEAD = 128
WIDTH = N_HEADS * D_HEAD
CHUNK = 64
CONV_K = 4
EPS = 1e-6
NEG = -1e30
ATT_SCALE = D_HEAD ** -0.5
N_SMALL = 3 * N_HEADS
P_COLS = 8 * WIDTH + 128
CB_FQ, CB_FK, CB_FV, CB_FZ, CB_GQ, CB_GK, CB_GV, CB_GZ, CB_SMALL = 0, 8, 16, 24, 32, 40, 48, 56, 64
IN_WIDTH = 8 * WIDTH + N_SMALL
N_CHIPS = 4
N_DEV = 8

ADAM_LR, ADAM_B1, ADAM_B2, ADAM_EPS, ADAM_WD, ADAM_STEP = 0.001, 0.9, 0.999, 1e-08, 0.01, 10


def _dotg(a, b, dims):
    return lax.dot_general(a.astype(MXU_DTYPE), b.astype(MXU_DTYPE), (dims, ((), ())), preferred_element_type=F32)


@jax.custom_vjp
def mm_nn(a, b):
    return _dotg(a, b, ((1,), (0,)))


@jax.custom_vjp
def mm_nt(a, b):
    return _dotg(a, b, ((1,), (1,)))


@jax.custom_vjp
def mm_tn(a, b):
    return _dotg(a, b, ((0,), (0,)))


mm_nn.defvjp(lambda a, b: (mm_nn(a, b), (a, b)), lambda r, g: (mm_nt(g, r[1]), mm_tn(r[0], g)))
mm_nt.defvjp(lambda a, b: (mm_nt(a, b), (a, b)), lambda r, g: (mm_nn(g, r[1]), mm_tn(g, r[0])))
mm_tn.defvjp(lambda a, b: (mm_tn(a, b), (a, b)), lambda r, g: (mm_nt(r[1], g), mm_nn(r[0], g)))


def hdot(a, b):
    return lax.dot_general(a, b, (((1,), (0,)), ((), ())), precision=HIGHEST, preferred_element_type=F32)


def _sigmoid(x):
    return 0.5 * (jnp.tanh(0.5 * x) + 1.0)


def _silu(x):
    return x * _sigmoid(x)


def _softplus(x):
    return jnp.maximum(x, 0.0) + jnp.log(1.0 + jnp.exp(-jnp.abs(x)))


def _log_sigmoid(x):
    return jnp.minimum(x, 0.0) - jnp.log(1.0 + jnp.exp(-jnp.abs(x)))


def _rms(x, g):
    return x * lax.rsqrt(jnp.mean(x * x, axis=-1, keepdims=True) + EPS) * g


def _l2n(x):
    return x * lax.rsqrt(jnp.sum(x * x, axis=-1, keepdims=True) + EPS)


def _tile(n, pref):
    t = min(n, pref)
    assert n % t == 0, (n, pref)
    return t


def _bspec(kind, w, col0, ts):
    if kind == "row":
        return pl.BlockSpec((ts, w), lambda i, h: (i, col0))
    if kind == "head":
        return pl.BlockSpec((ts, w), lambda i, h: (i, col0 + h))
    assert kind == "head3"
    return pl.BlockSpec((1, ts, w), lambda i, h: (h, i, 0))


def _ld(ref, kind):
    return ref[0] if kind == "head3" else ref[...]


def _st(ref, kind, val):
    if kind == "head3":
        ref[0] = val.astype(ref.dtype)
    else:
        ref[...] = val.astype(ref.dtype)


def _full_spec(a):
    nd = a.ndim
    return pl.BlockSpec(a.shape, lambda i, h: (0,) * nd)


def _out_struct(kind, w, S, width_total, dtype, nh):
    if kind == "head3":
        return jax.ShapeDtypeStruct((nh, S, w), dtype)
    return jax.ShapeDtypeStruct((S, width_total), dtype)


def rowwise_fwd(name, f, S, ts, nh, ins, params, outs):
    n_in, n_p = len(ins), len(params)

    def body(*refs):
        vals = [_ld(r, s[1]).astype(F32) for r, s in zip(refs[:n_in], ins)]
        pv = [r[...] for r in refs[n_in:n_in + n_p]]
        res = f(*vals, *pv)
        for r, s, o in zip(refs[n_in + n_p:], outs, res):
            _st(r, s[0], o)

    return pl.pallas_call(
        body, name=name, grid=(S // ts, nh),
        in_specs=[_bspec(k, w, c0, ts) for (_, k, w, c0) in ins] + [_full_spec(p) for p in params],
        out_specs=[_bspec(k, w, c0, ts) for (k, w, c0, _, _) in outs],
        out_shape=[_out_struct(k, w, S, tw, dt, nh) for (k, w, c0, tw, dt) in outs],
    )(*[a for (a, _, _, _) in ins], *params)


def rowwise_bwd(name, f, S, ts, nh, ins, params, cts, row_grads, extra=None, extra_outs=(), primal_sum=False):
    n_in, n_p, n_ct = len(ins), len(params), len(cts)
    n_rg, n_ex = len(row_grads), len(extra_outs)

    def body(*refs):
        first = (pl.program_id(0) == 0) & (pl.program_id(1) == 0)
        in_refs = refs[:n_in]
        p_refs = refs[n_in:n_in + n_p]
        ct_refs = refs[n_in + n_p:n_in + n_p + n_ct]
        o = n_in + n_p + n_ct
        rg_refs = refs[o:o + n_rg]
        ex_refs = refs[o + n_rg:o + n_rg + n_ex]
        acc_refs = refs[o + n_rg + n_ex:]
        vals = [_ld(r, s[1]).astype(F32) for r, s in zip(in_refs, ins)]
        pv = [r[...] for r in p_refs]
        res, vjp = jax.vjp(f, *vals, *pv)
        if primal_sum:
            cv = [jnp.ones_like(res[0])] + [_ld(r, s[1]).astype(F32) for r, s in zip(ct_refs, cts)]
        else:
            cv = [_ld(r, s[1]).astype(F32) for r, s in zip(ct_refs, cts)]
        grads = vjp(tuple(cv))
        for r, s in zip(rg_refs, row_grads):
            _st(r, s[1], grads[s[0]])
        if extra is not None:
            for r, s, e in zip(ex_refs, extra_outs, extra(vals, cv, grads)):
                _st(r, s[0], e)

        @pl.when(first)
        def _():
            for r in acc_refs:
                r[...] = jnp.zeros_like(r)

        for r, g in zip(acc_refs[:n_p], grads[n_in:]):
            r[...] += g
        if primal_sum:
            acc_refs[n_p][...] += jnp.sum(res[0])

    acc_shapes = [jax.ShapeDtypeStruct(p.shape, F32) for p in params]
    acc_specs = [_full_spec(p) for p in params]
    if primal_sum:
        acc_shapes.append(jax.ShapeDtypeStruct((1, 128), F32))
        acc_specs.append(pl.BlockSpec((1, 128), lambda i, h: (0, 0)))
    return pl.pallas_call(
        body, name=name, grid=(S // ts, nh),
        in_specs=[_bspec(k, w, c0, ts) for (_, k, w, c0) in ins] + [_full_spec(p) for p in params]
        + [_bspec(k, w, c0, ts) for (_, k, w, c0) in cts],
        out_specs=[_bspec(k, w, c0, ts) for (_, k, w, c0, _, _) in row_grads]
        + [_bspec(k, w, c0, ts) for (k, w, c0, _, _) in extra_outs] + acc_specs,
        out_shape=[_out_struct(k, w, S, tw, dt, nh) for (_, k, w, c0, tw, dt) in row_grads]
        + [_out_struct(k, w, S, tw, dt, nh) for (k, w, c0, tw, dt) in extra_outs] + acc_shapes,
    )(*[a for (a, _, _, _) in ins], *params, *[a for (a, _, _, _) in cts])


def f_prologue(x, ng, sc, sh):
    return (_rms(x, ng) * (1.0 + sc) + sh,)


def f_prologue_res(x, ng, sc, sh):
    return (_rms(x, ng) * (1.0 + sc) + sh, x)


def f_loss(x, mo, tgt, gate, fg):
    y = _rms(x + gate * mo, fg)
    return (0.5 * jnp.mean(jnp.square(y - tgt), axis=-1, keepdims=True),)


def f_foxpre(fq, fk, fv, qg, kg):
    return (_rms(fq, qg), _rms(fk, kg), fv)


def f_foxnorm(fq, fk, qg, kg):
    return (_rms(fq, qg), _rms(fk, kg))


def f_postf(o, z):
    return (o * _silu(z),)


def f_postg(o, z, ng):
    return (_rms(o, ng) * _silu(z),)


def f_gdnpre(yq, yk, yv):
    return (_l2n(_silu(yq)) * ATT_SCALE, _l2n(_silu(yk)), _silu(yv))


def matmul(name, a, b, out_dtype, tm=512, tn=512, tk=2048):
    M, K = a.shape
    K2, N = b.shape
    assert K == K2
    tm, tn, tk = _tile(M, tm), _tile(N, tn), _tile(K, tk)
    nk = K // tk

    def body(a_ref, b_ref, o_ref, *scr):
        part = jnp.dot(a_ref[...], b_ref[...], preferred_element_type=F32)
        if nk == 1:
            o_ref[...] = part.astype(o_ref.dtype)
            return
        acc = scr[0]
        k = pl.program_id(2)

        @pl.when(k == 0)
        def _():
            acc[...] = part

        @pl.when(k > 0)
        def _():
            acc[...] += part

        @pl.when(k == nk - 1)
        def _():
            o_ref[...] = acc[...].astype(o_ref.dtype)

    return pl.pallas_call(
        body, name=name, grid=(M // tm, N // tn, nk),
        in_specs=[pl.BlockSpec((tm, tk), lambda i, j, k: (i, k)), pl.BlockSpec((tk, tn), lambda i, j, k: (k, j))],
        out_specs=pl.BlockSpec((tm, tn), lambda i, j, k: (i, j)),
        out_shape=jax.ShapeDtypeStruct((M, N), out_dtype),
        scratch_shapes=[] if nk == 1 else [pltpu.VMEM((tm, tn), F32)],
        compiler_params=pltpu.CompilerParams(dimension_semantics=("parallel", "parallel", "arbitrary")),
    )(a, b)


def _small_f(z, pv, av):
    lane = lax.broadcasted_iota(jnp.int32, z.shape, 1)
    zz = z + pv
    logf = _log_sigmoid(zz)
    gval = -jnp.exp(av) * _softplus(zz)
    beta = _sigmoid(zz)
    return jnp.where(lane < 8, logf, jnp.where(lane < 16, gval, jnp.where(lane < 24, beta, 0.0)))


def _tri(ts, upper):
    r = lax.broadcasted_iota(jnp.int32, (ts, ts), 0)
    c = lax.broadcasted_iota(jnp.int32, (ts, ts), 1)
    keep = (r <= c) if upper else (r >= c)
    same_chunk = (r // CHUNK) == (c // CHUNK)
    return keep.astype(F32), (keep & same_chunk).astype(F32)


def smalls_fwd(p_all, pv, av, S, ts):
    nt = S // ts

    def body(z_ref, pv_ref, av_ref, o_ref, carry):
        i = pl.program_id(0)

        @pl.when(i == 0)
        def _():
            carry[...] = jnp.zeros_like(carry)

        e = _small_f(z_ref[...], pv_ref[...], av_ref[...])
        lane = lax.broadcasted_iota(jnp.int32, e.shape, 1)
        l_all, l_chunk = _tri(ts, upper=False)
        cum_all = hdot(l_all, e) + carry[...]
        cum_chunk = hdot(l_chunk, e)
        carry[...] = cum_all[ts - 1:ts, :]
        o_ref[...] = jnp.where(lane < 8, cum_all, jnp.where(lane < 16, cum_chunk, e))

    return pl.pallas_call(
        body, name="smalls_fwd", grid=(nt,),
        in_specs=[pl.BlockSpec((ts, 128), lambda i: (i, CB_SMALL)), pl.BlockSpec((1, 128), lambda i: (0, 0)),
                  pl.BlockSpec((1, 128), lambda i: (0, 0))],
        out_specs=pl.BlockSpec((ts, 128), lambda i: (i, 0)),
        out_shape=jax.ShapeDtypeStruct((S, 128), F32),
        scratch_shapes=[pltpu.VMEM((1, 128), F32)],
    )(p_all, pv, av)


def smalls_bwd(p_all, pv, av, dsm, S, ts):
    nt = S // ts

    def body(z_ref, pv_ref, av_ref, d_ref, dz_ref, dpv_ref, dav_ref, carry):
        i = pl.program_id(0)

        @pl.when(i == 0)
        def _():
            carry[...] = jnp.zeros_like(carry)
            dpv_ref[...] = jnp.zeros_like(dpv_ref)
            dav_ref[...] = jnp.zeros_like(dav_ref)

        d = d_ref[...]
        lane = lax.broadcasted_iota(jnp.int32, d.shape, 1)
        u_all, u_chunk = _tri(ts, upper=True)
        rev_all = hdot(u_all, d) + carry[...]
        rev_chunk = hdot(u_chunk, d)
        carry[...] = rev_all[0:1, :]
        de = jnp.where(lane < 8, rev_all, jnp.where(lane < 16, rev_chunk, d))
        _, vjp = jax.vjp(_small_f, z_ref[...], pv_ref[...], av_ref[...])
        dz, dpv, dav = vjp(de)
        dz_ref[...] = dz
        dpv_ref[...] += dpv
        dav_ref[...] += dav

    rev = lambda i: (nt - 1 - i, 0)
    return pl.pallas_call(
        body, name="smalls_bwd", grid=(nt,),
        in_specs=[pl.BlockSpec((ts, 128), lambda i: (nt - 1 - i, CB_SMALL)), pl.BlockSpec((1, 128), lambda i: (0, 0)),
                  pl.BlockSpec((1, 128), lambda i: (0, 0)), pl.BlockSpec((ts, 128), rev)],
        out_specs=[pl.BlockSpec((ts, 128), rev), pl.BlockSpec((1, 128), lambda i: (0, 0)),
                   pl.BlockSpec((1, 128), lambda i: (0, 0))],
        out_shape=[jax.ShapeDtypeStruct((S, 128), F32), jax.ShapeDtypeStruct((1, 128), F32),
                   jax.ShapeDtypeStruct((1, 128), F32)],
        scratch_shapes=[pltpu.VMEM((1, 128), F32)],
    )(p_all, pv, av, dsm)


def flash_fwd(qn, kn, vb, fcol, frow, S, tq):
    nq = S // tq
    tk = tq

    def body(q_ref, k_ref, v_ref, fc_ref, fr_ref, o_ref, lse_ref):
        i = pl.program_id(1)
        q = q_ref[...]
        fq = fc_ref[0]
        rowid = i * tq + lax.broadcasted_iota(jnp.int32, (tq, tk), 0)
        colid0 = lax.broadcasted_iota(jnp.int32, (tq, tk), 1)

        def step(j, carry):
            m, l, acc = carry
            off = pl.multiple_of(j * tk, tk)
            k = k_ref[pl.ds(off, tk), :]
            v = v_ref[pl.ds(off, tk), :]
            s = mm_nt(q, k) * ATT_SCALE + (fq - fr_ref[0, j])
            s = jnp.where(colid0 + j * tk <= rowid, s, NEG)
            m_new = jnp.maximum(m, jnp.max(s, axis=-1, keepdims=True))
            a = jnp.exp(m - m_new)
            p = jnp.exp(s - m_new)
            l = a * l + jnp.sum(p, axis=-1, keepdims=True)
            acc = a * acc + mm_nn(p, v)
            return m_new, l, acc

        m, l, acc = lax.fori_loop(0, i + 1, step, (jnp.full((tq, 1), NEG, F32), jnp.zeros((tq, 1), F32),
                                                   jnp.zeros((tq, D_HEAD), F32)))
        o_ref[...] = acc / l
        lse_ref[0] = m + jnp.log(l)

    return pl.pallas_call(
        body, name="flash_fwd", grid=(N_HEADS, nq),
        in_specs=[pl.BlockSpec((tq, D_HEAD), lambda h, i: (i, h)), pl.BlockSpec((S, D_HEAD), lambda h, i: (0, h)),
                  pl.BlockSpec((S, D_HEAD), lambda h, i: (0, h)), pl.BlockSpec((1, tq, 1), lambda h, i: (h, i, 0)),
                  pl.BlockSpec((1, nq, 1, tk), lambda h, i: (h, 0, 0, 0))],
        out_specs=[pl.BlockSpec((tq, D_HEAD), lambda h, i: (i, h)), pl.BlockSpec((1, tq, 1), lambda h, i: (h, i, 0))],
        out_shape=[jax.ShapeDtypeStruct((S, WIDTH), F32), jax.ShapeDtypeStruct((N_HEADS, S, 1), F32)],
    )(qn, kn, vb, fcol, frow)


def flash_bwd_dq(qn, kn, vb, do, fcol, frow, lse, delta, S, tq):
    nq = S // tq
    tk = tq

    def body(q_ref, k_ref, v_ref, do_ref, fc_ref, fr_ref, lse_ref, dl_ref, dq_ref, dfq_ref):
        i = pl.program_id(1)
        q = q_ref[...]
        do_ = do_ref[...]
        fq = fc_ref[0]
        lse_ = lse_ref[0]
        dl = dl_ref[0]
        rowid = i * tq + lax.broadcasted_iota(jnp.int32, (tq, tk), 0)
        colid0 = lax.broadcasted_iota(jnp.int32, (tq, tk), 1)

        def step(j, carry):
            dq, dfq = carry
            off = pl.multiple_of(j * tk, tk)
            k = k_ref[pl.ds(off, tk), :]
            v = v_ref[pl.ds(off, tk), :]
            s = mm_nt(q, k) * ATT_SCALE + (fq - fr_ref[0, j])
            s = jnp.where(colid0 + j * tk <= rowid, s, NEG)
            p = jnp.exp(s - lse_)
            ds = p * (mm_nt(do_, v) - dl)
            return dq + mm_nn(ds, k), dfq + jnp.sum(ds, axis=-1, keepdims=True)

        dq, dfq = lax.fori_loop(0, i + 1, step, (jnp.zeros((tq, D_HEAD), F32), jnp.zeros((tq, 1), F32)))
        dq_ref[...] = dq * ATT_SCALE
        dfq_ref[0] = dfq

    col = pl.BlockSpec((1, tq, 1), lambda h, i: (h, i, 0))
    blk = pl.BlockSpec((tq, D_HEAD), lambda h, i: (i, h))
    full = pl.BlockSpec((S, D_HEAD), lambda h, i: (0, h))
    return pl.pallas_call(
        body, name="flash_bwd_dq", grid=(N_HEADS, nq),
        in_specs=[blk, full, full, blk, col, pl.BlockSpec((1, nq, 1, tk), lambda h, i: (h, 0, 0, 0)), col, col],
        out_specs=[blk, col],
        out_shape=[jax.ShapeDtypeStruct((S, WIDTH), F32), jax.ShapeDtypeStruct((N_HEADS, S, 1), F32)],
    )(qn, kn, vb, do, fcol, frow, lse, delta)


def flash_bwd_dkv(qn, kn, vb, do, fcol, frow, lse_row, delta_row, S, tq):
    nq = S // tq
    tk = tq

    def body(q_ref, k_ref, v_ref, do_ref, fc_ref, fr_ref, lse_ref, dl_ref, dk_ref, dv_ref, dfk_ref):
        j = pl.program_id(1)
        k = k_ref[...]
        v = v_ref[...]
        fk = fc_ref[0]
        keyid = j * tk + lax.broadcasted_iota(jnp.int32, (tk, tq), 0)
        qid0 = lax.broadcasted_iota(jnp.int32, (tk, tq), 1)

        def step(i, carry):
            dk, dv, dfk = carry
            off = pl.multiple_of(i * tq, tq)
            q = q_ref[pl.ds(off, tq), :]
            do_ = do_ref[pl.ds(off, tq), :]
            st = mm_nt(k, q) * ATT_SCALE + (fr_ref[0, i] - fk)
            st = jnp.where(keyid <= qid0 + i * tq, st, NEG)
            pt = jnp.exp(st - lse_ref[0, i])
            dst = pt * (mm_nt(v, do_) - dl_ref[0, i])
            return dk + mm_nn(dst, q), dv + mm_nn(pt, do_), dfk - jnp.sum(dst, axis=-1, keepdims=True)

        z = jnp.zeros((tk, D_HEAD), F32)
        dk, dv, dfk = lax.fori_loop(j, nq, step, (z, z, jnp.zeros((tk, 1), F32)))
        dk_ref[...] = dk * ATT_SCALE
        dv_ref[...] = dv
        dfk_ref[0] = dfk

    col = pl.BlockSpec((1, tk, 1), lambda h, j: (h, j, 0))
    blk = pl.BlockSpec((tk, D_HEAD), lambda h, j: (j, h))
    full = pl.BlockSpec((S, D_HEAD), lambda h, j: (0, h))
    rows = pl.BlockSpec((1, nq, 1, tq), lambda h, j: (h, 0, 0, 0))
    return pl.pallas_call(
        body, name="flash_bwd_dkv", grid=(N_HEADS, nq),
        in_specs=[full, blk, blk, full, col, rows, rows, rows],
        out_specs=[blk, blk, col],
        out_shape=[jax.ShapeDtypeStruct((S, WIDTH), F32), jax.ShapeDtypeStruct((S, WIDTH), F32),
                   jax.ShapeDtypeStruct((N_HEADS, S, 1), F32)],
    )(qn, kn, vb, do, fcol, frow, lse_row, delta_row)


CONV_COLS = 3 * WIDTH
CONV_TC = 512


def conv_fwd(p_all, conv_w, S, ts):
    nt, tc = S // ts, CONV_TC
    cb0 = CB_GQ * 128 // tc
    r8 = ts // 8

    def body(x_ref, halo_ref, w_ref, y_ref, scr):
        i = pl.program_id(1)
        scr[0:8, :] = jnp.where(i > 0, halo_ref[...], 0.0)
        scr[8:8 + ts, :] = x_ref[...]
        w = w_ref[...]
        y = w[3:4, :] * x_ref[...]
        for j in range(CONV_K - 1):
            y = y + w[j:j + 1, :] * scr[5 + j:5 + j + ts, :]
        y_ref[...] = y

    return pl.pallas_call(
        body, name="conv_fwd", grid=(CONV_COLS // tc, nt),
        in_specs=[pl.BlockSpec((ts, tc), lambda c, i: (i, cb0 + c)),
                  pl.BlockSpec((8, tc), lambda c, i: (jnp.maximum(i * r8 - 1, 0), cb0 + c)),
                  pl.BlockSpec((CONV_K, tc), lambda c, i: (0, c))],
        out_specs=pl.BlockSpec((ts, tc), lambda c, i: (i, c)),
        out_shape=jax.ShapeDtypeStruct((S, CONV_COLS), F32),
        scratch_shapes=[pltpu.VMEM((ts + 8, tc), F32)],
    )(p_all, p_all, conv_w)


def conv_bwd(p_all, dy, conv_w, S, ts):
    nt, tc = S // ts, CONV_TC
    cb0 = CB_GQ * 128 // tc
    r8 = ts // 8

    def body(x_ref, xh_ref, dy_ref, dyh_ref, w_ref, dx_ref, dw_ref, xs, ds):
        i = pl.program_id(1)
        xs[0:8, :] = jnp.where(i > 0, xh_ref[...], 0.0)
        xs[8:8 + ts, :] = x_ref[...]
        ds[0:ts, :] = dy_ref[...]
        ds[ts:ts + 8, :] = jnp.where(i < nt - 1, dyh_ref[...], 0.0)
        w = w_ref[...]
        d = dy_ref[...]
        dx = w[3:4, :] * d
        for j in range(CONV_K - 1):
            dx = dx + w[j:j + 1, :] * ds[3 - j:3 - j + ts, :]
        dx_ref[...] = dx

        @pl.when(i == 0)
        def _():
            dw_ref[...] = jnp.zeros_like(dw_ref)

        rows = [jnp.sum(d * xs[5 + j:5 + j + ts, :], axis=0, keepdims=True) for j in range(CONV_K - 1)]
        rows.append(jnp.sum(d * x_ref[...], axis=0, keepdims=True))
        dw_ref[...] += jnp.concatenate(rows, axis=0)

    return pl.pallas_call(
        body, name="conv_bwd", grid=(CONV_COLS // tc, nt),
        in_specs=[pl.BlockSpec((ts, tc), lambda c, i: (i, cb0 + c)),
                  pl.BlockSpec((8, tc), lambda c, i: (jnp.maximum(i * r8 - 1, 0), cb0 + c)),
                  pl.BlockSpec((ts, tc), lambda c, i: (i, c)),
                  pl.BlockSpec((8, tc), lambda c, i: (jnp.minimum((i + 1) * r8, nt * r8 - 1), c)),
                  pl.BlockSpec((CONV_K, tc), lambda c, i: (0, c))],
        out_specs=[pl.BlockSpec((ts, tc), lambda c, i: (i, c)), pl.BlockSpec((CONV_K, tc), lambda c, i: (0, c))],
        out_shape=[jax.ShapeDtypeStruct((S, CONV_COLS), F32), jax.ShapeDtypeStruct((CONV_K, CONV_COLS), F32)],
        scratch_shapes=[pltpu.VMEM((ts + 8, tc), F32), pltpu.VMEM((ts + 8, tc), F32)],
    )(p_all, p_all, dy, dy, conv_w)


def chunk_fn(q, k, v, gc_col, gc_row, beta, state):
    ii = lax.broadcasted_iota(jnp.int32, (CHUNK, CHUNK), 0)
    jj = lax.broadcasted_iota(jnp.int32, (CHUNK, CHUNK), 1)
    lower = ii >= jj
    strict = ii > jj
    eye = (ii == jj).astype(F32)
    decay = jnp.exp(jnp.where(lower, gc_col - gc_row, NEG))
    kb = k * beta
    vb = v * beta
    m = jnp.where(strict, mm_nt(kb, k) * decay, 0.0)
    t = eye - m
    pw = hdot(m, m)
    for _ in range(4):
        t = t + hdot(t, pw)
        pw = hdot(pw, pw)
    t = t + hdot(t, pw)
    u = hdot(t, vb)
    w = hdot(t, kb * jnp.exp(gc_col))
    attn = jnp.where(lower, mm_nt(q, k) * decay, 0.0)
    v_new = u - mm_nn(w, state)
    o = mm_nn(q * jnp.exp(gc_col), state) + mm_nn(attn, v_new)
    g_last = gc_col[CHUNK - 1:CHUNK, :]
    k_dec = k * jnp.exp(g_last - gc_col)
    new_state = state * jnp.exp(g_last) + mm_tn(k_dec, v_new)
    return o, new_state


def gdn_fwd(q, k, v, smalls, gc_row, S):
    n = S // CHUNK

    def body(q_ref, k_ref, v_ref, sm_ref, gr_ref, o_ref, st_ref, state):
        @pl.when(pl.program_id(0) == 0)
        def _():
            state[...] = jnp.zeros_like(state)

        for h in range(N_HEADS):
            sl = slice(h * D_HEAD, (h + 1) * D_HEAD)
            s0 = state[h]
            st_ref[0, h] = s0
            o, s1 = chunk_fn(q_ref[:, sl], k_ref[:, sl], v_ref[:, sl], sm_ref[:, 8 + h:9 + h], gr_ref[0, h:h + 1, :],
                             sm_ref[:, 16 + h:17 + h], s0)
            o_ref[:, sl] = o
            state[h] = s1

    return pl.pallas_call(
        body, name="gdn_chunk_fwd", grid=(n,),
        in_specs=[pl.BlockSpec((CHUNK, WIDTH), lambda i: (i, 0))] * 3
        + [pl.BlockSpec((CHUNK, 128), lambda i: (i, 0)), pl.BlockSpec((1, N_HEADS, CHUNK), lambda i: (i, 0, 0))],
        out_specs=[pl.BlockSpec((CHUNK, WIDTH), lambda i: (i, 0)),
                   pl.BlockSpec((1, N_HEADS, D_HEAD, D_HEAD), lambda i: (i, 0, 0, 0))],
        out_shape=[jax.ShapeDtypeStruct((S, WIDTH), F32), jax.ShapeDtypeStruct((n, N_HEADS, D_HEAD, D_HEAD), F32)],
        scratch_shapes=[pltpu.VMEM((N_HEADS, D_HEAD, D_HEAD), F32)],
    )(q, k, v, smalls, gc_row)


def gdn_bwd(q, k, v, smalls, gc_row, states, do, S):
    n = S // CHUNK

    def body(q_ref, k_ref, v_ref, sm_ref, gr_ref, st_ref, do_ref, dqkv_ref, dsm_ref, dgr_ref, dstate):
        @pl.when(pl.program_id(0) == 0)
        def _():
            dstate[...] = jnp.zeros_like(dstate)

        dgc, dgr, dbt = [], [], []
        for h in range(N_HEADS):
            sl = slice(h * D_HEAD, (h + 1) * D_HEAD)
            _, vjp = jax.vjp(chunk_fn, q_ref[:, sl], k_ref[:, sl], v_ref[:, sl], sm_ref[:, 8 + h:9 + h],
                             gr_ref[0, h:h + 1, :], sm_ref[:, 16 + h:17 + h], st_ref[0, h])
            dq, dk, dv, dgc_h, dgr_h, dbt_h, ds = vjp((do_ref[:, sl], dstate[h]))
            dqkv_ref[:, h * D_HEAD:(h + 1) * D_HEAD] = dq
            dqkv_ref[:, WIDTH + h * D_HEAD:WIDTH + (h + 1) * D_HEAD] = dk
            dqkv_ref[:, 2 * WIDTH + h * D_HEAD:2 * WIDTH + (h + 1) * D_HEAD] = dv
            dstate[h] = ds
            dgc.append(dgc_h)
            dgr.append(dgr_h)
            dbt.append(dbt_h)
        z8 = jnp.zeros((CHUNK, 8), F32)
        dsm_ref[...] = jnp.concatenate([z8] + dgc + dbt + [jnp.zeros((CHUNK, 128 - 24), F32)], axis=1)
        dgr_ref[0] = jnp.concatenate(dgr, axis=0)

    rev = lambda c: (lambda i: (n - 1 - i, c))
    return pl.pallas_call(
        body, name="gdn_chunk_bwd", grid=(n,),
        in_specs=[pl.BlockSpec((CHUNK, WIDTH), rev(0))] * 3
        + [pl.BlockSpec((CHUNK, 128), rev(0)), pl.BlockSpec((1, N_HEADS, CHUNK), lambda i: (n - 1 - i, 0, 0)),
                  pl.BlockSpec((1, N_HEADS, D_HEAD, D_HEAD), lambda i: (n - 1 - i, 0, 0, 0)),
                  pl.BlockSpec((CHUNK, WIDTH), rev(0))],
        out_specs=[pl.BlockSpec((CHUNK, 3 * WIDTH), rev(0)), pl.BlockSpec((CHUNK, 128), rev(0)),
                   pl.BlockSpec((1, N_HEADS, CHUNK), lambda i: (n - 1 - i, 0, 0))],
        out_shape=[jax.ShapeDtypeStruct((S, 3 * WIDTH), F32)] + [jax.ShapeDtypeStruct((S, 128), F32),
                                                                  jax.ShapeDtypeStruct((n, N_HEADS, CHUNK), F32)],
        scratch_shapes=[pltpu.VMEM((N_HEADS, D_HEAD, D_HEAD), F32)],
    )(q, k, v, smalls, gc_row, states, do)


def _flip(a, bit):
    return 1 - a if bit else a


def allgather8(name, buf):
    R, W = buf.shape

    def body(x_ref, out_ref, send_sems, recv_sems, local_sem):
        x, y, c = lax.axis_index("x"), lax.axis_index("y"), lax.axis_index("c")

        def slot(px, py, pc):
            return out_ref.at[4 * px + 2 * py + pc]

        mine = pltpu.make_async_copy(x_ref, slot(x, y, c), local_sem)
        mine.start()
        sends = []
        for r in range(1, N_DEV):
            peer = (_flip(x, r & 4), _flip(y, r & 2), _flip(c, r & 1))
            cp = pltpu.make_async_remote_copy(src_ref=x_ref, dst_ref=slot(x, y, c), send_sem=send_sems.at[r - 1],
                                              recv_sem=recv_sems.at[r - 1], device_id=peer, device_id_type=MESH_ID)
            cp.start()
            sends.append(cp)
        for r in range(1, N_DEV):
            peer = (_flip(x, r & 4), _flip(y, r & 2), _flip(c, r & 1))
            pltpu.make_async_remote_copy(src_ref=x_ref, dst_ref=slot(*peer), send_sem=send_sems.at[r - 1],
                                         recv_sem=recv_sems.at[r - 1], device_id=peer,
                                         device_id_type=MESH_ID).wait_recv()
        for cp in sends:
            cp.wait_send()
        mine.wait()

    return pl.pallas_call(
        body, name=name,
        out_shape=jax.ShapeDtypeStruct((N_DEV, R, W), buf.dtype),
        in_specs=[pl.BlockSpec(memory_space=pltpu.VMEM)],
        out_specs=pl.BlockSpec(memory_space=pltpu.VMEM),
        scratch_shapes=[pltpu.SemaphoreType.DMA((N_DEV - 1,)), pltpu.SemaphoreType.DMA((N_DEV - 1,)),
                        pltpu.SemaphoreType.DMA],
    )(buf)


def allgather_chips(name, blk):
    def body(x_ref, out_ref, send_sems, recv_sems, local_sem):
        x, y, c = lax.axis_index("x"), lax.axis_index("y"), lax.axis_index("c")
        mine = pltpu.make_async_copy(x_ref, out_ref.at[2 * x + y], local_sem)
        mine.start()
        sends = []
        for r in range(1, N_CHIPS):
            peer = (_flip(x, r & 2), _flip(y, r & 1), c)
            cp = pltpu.make_async_remote_copy(src_ref=x_ref, dst_ref=out_ref.at[2 * x + y], send_sem=send_sems.at[r - 1],
                                              recv_sem=recv_sems.at[r - 1], device_id=peer, device_id_type=MESH_ID)
            cp.start()
            sends.append(cp)
        for r in range(1, N_CHIPS):
            peer = (_flip(x, r & 2), _flip(y, r & 1), c)
            pltpu.make_async_remote_copy(src_ref=x_ref, dst_ref=out_ref.at[2 * peer[0] + peer[1]],
                                         send_sem=send_sems.at[r - 1], recv_sem=recv_sems.at[r - 1], device_id=peer,
                                         device_id_type=MESH_ID).wait_recv()
        for cp in sends:
            cp.wait_send()
        mine.wait()

    return pl.pallas_call(
        body, name=name,
        out_shape=jax.ShapeDtypeStruct((N_CHIPS,) + blk.shape, blk.dtype),
        in_specs=[pl.BlockSpec(memory_space=pl.ANY)],
        out_specs=pl.BlockSpec(memory_space=pl.ANY),
        scratch_shapes=[pltpu.SemaphoreType.DMA((N_CHIPS - 1,)), pltpu.SemaphoreType.DMA((N_CHIPS - 1,)),
                        pltpu.SemaphoreType.DMA],
    )(blk)


def scatter_chips(name, parts):
    def body(x_ref, out_ref, send_sems, recv_sems):
        x, y, c = lax.axis_index("x"), lax.axis_index("y"), lax.axis_index("c")
        sends = []
        for r in range(1, N_CHIPS):
            peer = (_flip(x, r & 2), _flip(y, r & 1), c)
            cp = pltpu.make_async_remote_copy(src_ref=x_ref.at[2 * peer[0] + peer[1]], dst_ref=out_ref.at[r - 1],
                                              send_sem=send_sems.at[r - 1], recv_sem=recv_sems.at[r - 1],
                                              device_id=peer, device_id_type=MESH_ID)
            cp.start()
            sends.append(cp)
        for cp in sends:
            cp.wait_recv()
        for cp in sends:
            cp.wait_send()

    return pl.pallas_call(
        body, name=name,
        out_shape=jax.ShapeDtypeStruct((N_CHIPS - 1,) + parts.shape[1:], parts.dtype),
        in_specs=[pl.BlockSpec(memory_space=pl.ANY)],
        out_specs=pl.BlockSpec(memory_space=pl.ANY),
        scratch_shapes=[pltpu.SemaphoreType.DMA((N_CHIPS - 1,)), pltpu.SemaphoreType.DMA((N_CHIPS - 1,))],
    )(parts)


def sibling_swap(name, buf):
    def body(x_ref, out_ref, send_sem, recv_sem):
        x, y, c = lax.axis_index("x"), lax.axis_index("y"), lax.axis_index("c")
        cp = pltpu.make_async_remote_copy(src_ref=x_ref, dst_ref=out_ref, send_sem=send_sem, recv_sem=recv_sem,
                                          device_id=(x, y, 1 - c), device_id_type=MESH_ID)
        cp.start()
        cp.wait_recv()
        cp.wait_send()

    return pl.pallas_call(
        body, name=name,
        out_shape=jax.ShapeDtypeStruct(buf.shape, buf.dtype),
        in_specs=[pl.BlockSpec(memory_space=pl.ANY)],
        out_specs=pl.BlockSpec(memory_space=pl.ANY),
        scratch_shapes=[pltpu.SemaphoreType.DMA, pltpu.SemaphoreType.DMA],
    )(buf)


def sum_parts(name, own, recv, tr):
    R, W = own.shape
    tr = _tile(R, tr)

    def body(o_ref, r_ref, out_ref):
        acc = o_ref[...]
        for k in range(recv.shape[0]):
            acc = acc + r_ref[k].astype(F32)
        out_ref[...] = acc

    return pl.pallas_call(
        body, name=name, grid=(R // tr,),
        in_specs=[pl.BlockSpec((tr, W), lambda i: (i, 0)), pl.BlockSpec((recv.shape[0], tr, W), lambda i: (0, i, 0))],
        out_specs=pl.BlockSpec((tr, W), lambda i: (i, 0)),
        out_shape=jax.ShapeDtypeStruct((R, W), F32),
    )(own, recv)


def sum_devices(name, g):
    def body(g_ref, o_ref):
        acc = g_ref[0]
        for d in range(1, N_DEV):
            acc = acc + g_ref[d]
        o_ref[...] = acc

    return pl.pallas_call(body, name=name, out_shape=jax.ShapeDtypeStruct(g.shape[1:], F32))(g)


def ada_fwd(c_all, w_ada):
    K, N = w_ada.shape
    tn = _tile(N, 512)

    def body(c_ref, w_ref, o_ref):
        o_ref[...] = mm_nn(_silu(c_ref[...]), w_ref[...])

    return pl.pallas_call(
        body, name="ada_fwd", grid=(N // tn,),
        in_specs=[pl.BlockSpec((N_DEV, K), lambda j: (0, 0)), pl.BlockSpec((K, tn), lambda j: (0, j))],
        out_specs=pl.BlockSpec((N_DEV, tn), lambda j: (0, j)),
        out_shape=jax.ShapeDtypeStruct((N_DEV, N), F32),
    )(c_all, w_ada)


def ada_grad(c_all, dmod):
    K = c_all.shape[1]
    N = dmod.shape[1]
    tn = _tile(N, 512)

    def body(c_ref, d_ref, o_ref):
        o_ref[...] = mm_tn(_silu(c_ref[...]), d_ref[...])

    return pl.pallas_call(
        body, name="ada_grad", grid=(N // tn,),
        in_specs=[pl.BlockSpec((N_DEV, K), lambda j: (0, 0)), pl.BlockSpec((N_DEV, tn), lambda j: (0, j))],
        out_specs=pl.BlockSpec((K, tn), lambda j: (0, j)),
        out_shape=jax.ShapeDtypeStruct((K, N), F32),
    )(c_all, dmod)


def adamw(name, w, gparts, m, v, tr):
    R, W = w.shape
    tr = _tile(R, tr)
    ng = len(gparts)

    def body(w_ref, *refs):
        g_refs, (m_ref, v_ref, g_out, d_out, m_out, v_out) = refs[:ng], refs[ng:]
        g = g_refs[0][...]
        for r in g_refs[1:]:
            g = g + r[...]
        m1 = ADAM_B1 * m_ref[...] + (1.0 - ADAM_B1) * g
        v1 = ADAM_B2 * v_ref[...] + (1.0 - ADAM_B2) * jnp.square(g)
        m_hat = m1 / (1.0 - ADAM_B1 ** ADAM_STEP)
        v_hat = v1 / (1.0 - ADAM_B2 ** ADAM_STEP)
        g_out[...] = g
        d_out[...] = -ADAM_LR * (m_hat / (jnp.sqrt(v_hat) + ADAM_EPS) + ADAM_WD * w_ref[...])
        m_out[...] = m1
        v_out[...] = v1

    spec = pl.BlockSpec((tr, W), lambda i: (i, 0))
    return pl.pallas_call(
        body, name=name, grid=(R // tr,),
        in_specs=[spec] * (3 + ng), out_specs=[spec] * 4,
        out_shape=[jax.ShapeDtypeStruct((R, W), F32)] * 4,
    )(w, *gparts, m, v)


def _to_internal_cols(w):
    pad = jnp.zeros(w.shape[:-1] + (128 - N_SMALL,), w.dtype)
    a = 4 * WIDTH
    return jnp.concatenate([w[..., :a], w[..., a + 8:2 * a + 8], w[..., a:a + 8], w[..., 2 * a + 8:], pad], axis=-1)


def _from_internal_cols(g):
    a = 4 * WIDTH
    return jnp.concatenate([g[..., :a], g[..., 2 * a:2 * a + 8], g[..., a:2 * a], g[..., 2 * a + 8:2 * a + N_SMALL]],
                           axis=-1)


def _pad_lanes(v, n=128):
    return jnp.pad(v, ((0, 0), (0, n - v.shape[1])))


def kernel(x, c, norm_g, w_ada, b_ada, w_in, b_fgate, fox_qn_g, fox_kn_g, gdn_conv_w, gdn_A_log, gdn_dt_bias, gdn_norm_g, w_out, final_g, loss_target, m_norm_g, m_w_ada, m_b_ada, m_w_in, m_b_fgate, m_fox_qn_g, m_fox_kn_g, m_gdn_conv_w, m_gdn_A_log, m_gdn_dt_bias, m_gdn_norm_g, m_w_out, m_final_g, v_norm_g, v_w_ada, v_b_ada, v_w_in, v_b_fgate, v_fox_qn_g, v_fox_kn_g, v_gdn_conv_w, v_gdn_A_log, v_gdn_dt_bias, v_gdn_norm_g, v_w_out, v_final_g):
    S = x.shape[1]
    H = N_HEADS
    ix, iy, ic = lax.axis_index("x"), lax.axis_index("y"), lax.axis_index("c")
    chip = 2 * ix + iy
    me = 2 * chip + ic
    x2, tgt = x[0], loss_target[0]
    ts_wide = _tile(S, 256)
    ts = _tile(S, 512)
    tq = _tile(S, 512)
    nq = S // tq
    nch = S // CHUNK
    conv_cols_chip = CONV_COLS // N_CHIPS

    pay = jnp.concatenate([c, _pad_lanes(gdn_conv_w[0], D_MODEL), jnp.zeros((3, D_MODEL), F32)], axis=0)
    g1 = allgather8("ag_c", pay)
    c_all = g1[:, 0, :]
    conv_w = jnp.concatenate([g1[2 * j, 1:1 + CONV_K, :conv_cols_chip] for j in range(N_CHIPS)], axis=1)
    g2 = allgather8("ag_mod", ada_fwd(c_all, w_ada[0]))
    mod_rows = jnp.concatenate([g2[2 * j] for j in range(N_CHIPS)], axis=1)
    mod = lax.dynamic_slice_in_dim(mod_rows, me, 1, axis=0) + b_ada
    shift, scale, gate = mod[:, :D_MODEL], mod[:, D_MODEL:2 * D_MODEL], mod[:, 2 * D_MODEL:]

    win4 = allgather_chips("ag_w_in", w_in[0].astype(MXU_DTYPE))
    w_all = _to_internal_cols(jnp.transpose(win4, (1, 0, 2)).reshape(D_MODEL, IN_WIDTH))
    wout4 = allgather_chips("ag_w_out", w_out[0].astype(MXU_DTYPE))
    w_out_full = wout4.reshape(2 * WIDTH, D_MODEL)

    (h_b,) = rowwise_fwd("prologue", f_prologue, S, ts_wide, 1, [(x2, "row", D_MODEL, 0)], [norm_g, scale, shift],
                         [("row", D_MODEL, 0, D_MODEL, MXU_DTYPE)])
    p_all = matmul("in_proj", h_b, w_all, F32, tm=512, tn=640, tk=2048)
    pv = jnp.concatenate([b_fgate, gdn_dt_bias, jnp.zeros((1, 128 - 16), F32)], axis=1)
    av = jnp.concatenate([jnp.zeros((1, 8), F32), gdn_A_log, jnp.zeros((1, 128 - 16), F32)], axis=1)
    smalls = smalls_fwd(p_all, pv, av, S, ts)
    f_t = jnp.transpose(smalls[:, :H])
    fcol = f_t[:, :, None]
    frow = f_t.reshape(H, nq, 1, tq)
    gc_row = jnp.transpose(smalls[:, 8:16].reshape(nch, CHUNK, H), (0, 2, 1))
    head_b = ("head", D_HEAD, 0, WIDTH, MXU_DTYPE)
    head_f = ("head", D_HEAD, 0, WIDTH, F32)
    pcol = lambda cb: (p_all, "head", D_HEAD, cb)
    qn, kn, vb = rowwise_fwd("fox_pre", f_foxpre, S, ts, H, [pcol(CB_FQ), pcol(CB_FK), pcol(CB_FV)],
                             [fox_qn_g, fox_kn_g], [head_b] * 3)
    o_fox, lse = flash_fwd(qn, kn, vb, fcol, frow, S, tq)
    y_conv = conv_fwd(p_all, conv_w, S, ts)
    ycol = lambda cb: (y_conv, "head", D_HEAD, cb)
    gq, gk, gv = rowwise_fwd("gdn_pre", f_gdnpre, S, ts, H, [ycol(0), ycol(8), ycol(16)], [], [head_f] * 3)
    o_gdn, states = gdn_fwd(gq, gk, gv, smalls, gc_row, S)
    (mix_f,) = rowwise_fwd("post_fox", f_postf, S, ts, H, [(o_fox, "head", D_HEAD, 0), pcol(CB_FZ)], [], [head_b])
    (mix_g,) = rowwise_fwd("post_gdn", f_postg, S, ts, H, [(o_gdn, "head", D_HEAD, 0), pcol(CB_GZ)], [gdn_norm_g],
                           [head_b])
    mixed = jnp.concatenate([mix_f, mix_g], axis=1)
    mo = matmul("out_proj", mixed, w_out_full, F32)

    wide = lambda a: (a, "row", D_MODEL, 0)
    dx_res, dmo, d_gate, d_final_g, loss_acc = rowwise_bwd(
        "loss", f_loss, S, ts_wide, 1, [wide(x2), wide(mo), wide(tgt)], [gate, final_g[None, :]], [],
        [(0, "row", D_MODEL, 0, D_MODEL, F32), (1, "row", D_MODEL, 0, D_MODEL, MXU_DTYPE)], primal_sum=True)

    d_mixed = matmul("d_mixed", dmo, jnp.transpose(w_out_full), F32)
    g_w_out = matmul("g_w_out", jnp.transpose(mixed), dmo, F32)
    do_fox, d_fz, delta = rowwise_bwd(
        "post_fox_bwd", f_postf, S, ts, H, [(o_fox, "head", D_HEAD, 0), pcol(CB_FZ)], [],
        [(d_mixed, "head", D_HEAD, 0)], [(0,) + head_b, (1,) + head_f],
        extra=lambda vals, cv, grads: (jnp.sum(grads[0] * vals[0], axis=-1, keepdims=True),),
        extra_outs=[("head3", 1, 0, None, F32)])
    do_gdn, d_gz, d_gdn_norm_g = rowwise_bwd(
        "post_gdn_bwd", f_postg, S, ts, H, [(o_gdn, "head", D_HEAD, 0), pcol(CB_GZ)], [gdn_norm_g],
        [(d_mixed, "head", D_HEAD, 8)], [(0,) + head_f, (1,) + head_f])
    d_qn, dfq_col = flash_bwd_dq(qn, kn, vb, do_fox, fcol, frow, lse, delta, S, tq)
    d_kn, d_fv, dfk_col = flash_bwd_dkv(qn, kn, vb, do_fox, fcol, frow, lse.reshape(H, nq, 1, tq),
                                        delta.reshape(H, nq, 1, tq), S, tq)
    d_fq, d_fk, d_qn_g, d_kn_g = rowwise_bwd(
        "fox_pre_bwd", f_foxnorm, S, ts, H, [pcol(CB_FQ), pcol(CB_FK)], [fox_qn_g, fox_kn_g],
        [(d_qn, "head", D_HEAD, 0), (d_kn, "head", D_HEAD, 0)], [(0,) + head_f, (1,) + head_f])
    d_qkv, dsm_chunk, dgc_row = gdn_bwd(gq, gk, gv, smalls, gc_row, states, do_gdn, S)
    d_yq, d_yk, d_yv = rowwise_bwd(
        "gdn_pre_bwd", f_gdnpre, S, ts, H, [ycol(0), ycol(8), ycol(16)], [],
        [(d_qkv, "head", D_HEAD, 0), (d_qkv, "head", D_HEAD, 8), (d_qkv, "head", D_HEAD, 16)],
        [(0,) + head_f, (1,) + head_f, (2,) + head_f])
    d_xc, d_conv_w = conv_bwd(p_all, jnp.concatenate([d_yq, d_yk, d_yv], axis=1), conv_w, S, ts)
    d_f = jnp.transpose((dfq_col + dfk_col)[:, :, 0])
    d_gc = jnp.transpose(dgc_row, (0, 2, 1)).reshape(S, H)
    dsm = dsm_chunk + jnp.concatenate([d_f, d_gc, jnp.zeros((S, 128 - 16), F32)], axis=1)
    d_small, d_pv, d_av = smalls_bwd(p_all, pv, av, dsm, S, ts)
    dp_all = jnp.concatenate([d_fq, d_fk, d_fv, d_fz, d_xc, d_gz, d_small], axis=1).astype(MXU_DTYPE)
    d_h = matmul("d_h", dp_all, jnp.transpose(w_all), F32, tm=512, tn=1024, tk=1664)
    g_w_all = matmul("g_w_in", jnp.transpose(h_b), dp_all, F32, tm=512, tn=640, tk=2048)
    d_x, d_norm_g, d_scale, d_shift = rowwise_bwd(
        "prologue_bwd", f_prologue_res, S, ts_wide, 1, [wide(x2)], [norm_g, scale, shift],
        [wide(d_h), wide(dx_res)], [(0, "row", D_MODEL, 0, D_MODEL, F32)])

    parts = [d_shift, d_scale, d_gate, d_norm_g, d_final_g, d_conv_w.reshape(1, CONV_K * CONV_COLS), d_qn_g, d_kn_g,
             d_gdn_norm_g, d_pv, d_av, loss_acc]
    n_pay = sum(p.shape[1] for p in parts)
    pay_w = -(-n_pay // 1024) * 1024
    pay2 = jnp.concatenate(parts + [jnp.zeros((1, pay_w - n_pay), F32)], axis=1).reshape(8, pay_w // 8)
    g3 = allgather8("ag_small", pay2)
    tot = sum_devices("sum_small", g3).reshape(1, pay_w)
    dmod_all = g3.reshape(N_DEV, pay_w)[:, :3 * D_MODEL]
    o = 3 * D_MODEL
    g_b_ada = tot[:, :o]
    g_norm_g, g_final_g = tot[:, o:o + D_MODEL], tot[:, o + D_MODEL:o + 2 * D_MODEL]
    o += 2 * D_MODEL
    g_conv_full = tot[:, o:o + CONV_K * CONV_COLS].reshape(CONV_K, CONV_COLS)
    g_conv = lax.dynamic_slice_in_dim(g_conv_full, chip * conv_cols_chip, conv_cols_chip, axis=1)
    o += CONV_K * CONV_COLS
    g_qn_g, g_kn_g, g_gdn_ng = tot[:, o:o + 128], tot[:, o + 128:o + 256], tot[:, o + 256:o + 384]
    g_pv, g_av = tot[:, o + 384:o + 512], tot[:, o + 512:o + 640]
    loss = tot[0, o + 640]
    g_b_fgate, g_dt_bias, g_a_log = g_pv[:, :8], g_pv[:, 8:16], g_av[:, 8:16]

    def small_vec(vals):
        flat = [norm for norm in vals[:3]] + [vals[3].reshape(1, CONV_K * conv_cols_chip)] + list(vals[4:7]) \
            + [_pad_lanes(s) for s in vals[7:]]
        n = sum(f.shape[1] for f in flat)
        nw = -(-n // 1024) * 1024
        return jnp.concatenate(flat + [jnp.zeros((1, nw - n), F32)], axis=1).reshape(8, nw // 8)

    sw = small_vec([norm_g, b_ada, final_g[None, :], gdn_conv_w[0], fox_qn_g, fox_kn_g, gdn_norm_g, b_fgate, gdn_A_log,
                    gdn_dt_bias])
    sm = small_vec([m_norm_g, m_b_ada, m_final_g[None, :], m_gdn_conv_w[0], m_fox_qn_g, m_fox_kn_g, m_gdn_norm_g,
                    m_b_fgate, m_gdn_A_log, m_gdn_dt_bias])
    sv = small_vec([v_norm_g, v_b_ada, v_final_g[None, :], v_gdn_conv_w[0], v_fox_qn_g, v_fox_kn_g, v_gdn_norm_g,
                    v_b_fgate, v_gdn_A_log, v_gdn_dt_bias])
    sg = small_vec([g_norm_g, g_b_ada, g_final_g, g_conv, g_qn_g, g_kn_g, g_gdn_ng, g_b_fgate, g_a_log, g_dt_bias])
    small_out = adamw("adamw_small", sw, [sg], sm, sv, 8)

    def split_small(a):
        a = a.reshape(1, -1)
        out, o = [], 0
        for n, shp in ((D_MODEL, (1, D_MODEL)), (3 * D_MODEL, (1, 3 * D_MODEL)), (D_MODEL, (D_MODEL,)),
                       (CONV_K * conv_cols_chip, (1, CONV_K, conv_cols_chip)), (128, (1, 128)), (128, (1, 128)),
                       (128, (1, 128))):
            out.append(a[:, o:o + n].reshape(shp))
            o += n
        for _ in range(3):
            out.append(a[:, o:o + 8])
            o += 128
        return out

    n_ada = w_ada.shape[2]
    g_w_ada = ada_grad(c_all, lax.dynamic_slice_in_dim(dmod_all, chip * n_ada, n_ada, axis=1))
    ada_out = adamw("adamw_w_ada", w_ada[0], [g_w_ada], m_w_ada[0], v_w_ada[0], 128)

    def reduce_scatter(tag, g4, tr):
        own = lax.dynamic_index_in_dim(g4, chip, axis=0, keepdims=False)
        recv = scatter_chips("rs_" + tag, g4.astype(MXU_DTYPE))
        part = sum_parts("sum_" + tag, own, recv, tr)
        return part, sibling_swap("swap_" + tag, part)

    n_in = w_in.shape[2]
    g4_in = jnp.transpose(_from_internal_cols(g_w_all).reshape(D_MODEL, N_CHIPS, n_in), (1, 0, 2))
    in_out = adamw("adamw_w_in", w_in[0], list(reduce_scatter("w_in", g4_in, 128)), m_w_in[0], v_w_in[0], 128)
    g4_out = g_w_out.reshape(N_CHIPS, w_out.shape[1], D_MODEL)
    out_out = adamw("adamw_w_out", w_out[0], list(reduce_scatter("w_out", g4_out, 128)), m_w_out[0], v_w_out[0], 128)

    res = [loss, d_x[None]]
    for k in range(4):
        s = split_small(small_out[k])
        big = [ada_out[k][None], in_out[k][None], out_out[k][None]]
        res += [s[0], big[0], s[1], big[1], s[7], s[4], s[5], s[3], s[8], s[9], s[6], big[2], s[2]]
    return tuple(res)
```

```python
import functools

import jax
import jax.numpy as jnp
from jax import lax
from jax.experimental import pallas as pl
from jax.experimental.pallas import tpu as pltpu

F32 = jnp.float32
MXU_DTYPE = jnp.bfloat16
HIGHEST = lax.Precision.HIGHEST
SOLVE_PRECISION = lax.Precision.HIGH
MESH_ID = pl.DeviceIdType.MESH

D_MODEL = 2048
N_HEADS = 8
D_HEAD = 128
WIDTH = N_HEADS * D_HEAD
CHUNK = 64
CONV_K = 4
EPS = 1e-6
NEG = -1e30
ATT_SCALE = D_HEAD ** -0.5
N_SMALL = 3 * N_HEADS
P_COLS = 8 * WIDTH + 128
CB_FQ, CB_FK, CB_FV, CB_FZ, CB_GQ, CB_GK, CB_GV, CB_GZ, CB_SMALL = 0, 8, 16, 24, 32, 40, 48, 56, 64
IN_WIDTH = 8 * WIDTH + N_SMALL
N_CHIPS = 4
N_DEV = 8

ADAM_LR, ADAM_B1, ADAM_B2, ADAM_EPS, ADAM_WD, ADAM_STEP = 0.001, 0.9, 0.999, 1e-08, 0.01, 10


def _dotg(a, b, dims):
    return lax.dot_general(a.astype(MXU_DTYPE), b.astype(MXU_DTYPE), (dims, ((), ())), preferred_element_type=F32)


@jax.custom_vjp
def mm_nn(a, b):
    return _dotg(a, b, ((1,), (0,)))


@jax.custom_vjp
def mm_nt(a, b):
    return _dotg(a, b, ((1,), (1,)))


@jax.custom_vjp
def mm_tn(a, b):
    return _dotg(a, b, ((0,), (0,)))


mm_nn.defvjp(lambda a, b: (mm_nn(a, b), (a, b)), lambda r, g: (mm_nt(g, r[1]), mm_tn(r[0], g)))
mm_nt.defvjp(lambda a, b: (mm_nt(a, b), (a, b)), lambda r, g: (mm_nn(g, r[1]), mm_tn(g, r[0])))
mm_tn.defvjp(lambda a, b: (mm_tn(a, b), (a, b)), lambda r, g: (mm_nt(r[1], g), mm_nn(r[0], g)))


def hdot(a, b):
    return lax.dot_general(a, b, (((1,), (0,)), ((), ())), precision=HIGHEST, preferred_element_type=F32)


def _sigmoid(x):
    return 0.5 * (jnp.tanh(0.5 * x) + 1.0)


def _silu(x):
    return x * _sigmoid(x)


def _softplus(x):
    return jnp.maximum(x, 0.0) + jnp.log(1.0 + jnp.exp(-jnp.abs(x)))


def _log_sigmoid(x):
    return jnp.minimum(x, 0.0) - jnp.log(1.0 + jnp.exp(-jnp.abs(x)))


def _rms(x, g):
    return x * lax.rsqrt(jnp.mean(x * x, axis=-1, keepdims=True) + EPS) * g


def _l2n(x):
    return x * lax.rsqrt(jnp.sum(x * x, axis=-1, keepdims=True) + EPS)


def _tile(n, pref):
    t = min(n, pref)
    assert n % t == 0, (n, pref)
    return t


def _bspec(kind, w, col0, ts):
    if kind == "row":
        return pl.BlockSpec((ts, w), lambda i, h: (i, col0))
    if kind == "head":
        return pl.BlockSpec((ts, w), lambda i, h: (i, col0 + h))
    assert kind == "head3"
    return pl.BlockSpec((1, ts, w), lambda i, h: (h, i, 0))


def _ld(ref, kind):
    return ref[0] if kind == "head3" else ref[...]


def _st(ref, kind, val):
    if kind == "head3":
        ref[0] = val.astype(ref.dtype)
    else:
        ref[...] = val.astype(ref.dtype)


def _full_spec(a):
    nd = a.ndim
    return pl.BlockSpec(a.shape, lambda i, h: (0,) * nd)


def _out_struct(kind, w, S, width_total, dtype, nh):
    if kind == "head3":
        return jax.ShapeDtypeStruct((nh, S, w), dtype)
    return jax.ShapeDtypeStruct((S, width_total), dtype)


def rowwise_fwd(name, f, S, ts, nh, ins, params, outs):
    n_in, n_p = len(ins), len(params)

    def body(*refs):
        vals = [_ld(r, s[1]).astype(F32) for r, s in zip(refs[:n_in], ins)]
        pv = [r[...] for r in refs[n_in:n_in + n_p]]
        res = f(*vals, *pv)
        for r, s, o in zip(refs[n_in + n_p:], outs, res):
            _st(r, s[0], o)

    return pl.pallas_call(
        body, name=name, grid=(S // ts, nh),
        in_specs=[_bspec(k, w, c0, ts) for (_, k, w, c0) in ins] + [_full_spec(p) for p in params],
        out_specs=[_bspec(k, w, c0, ts) for (k, w, c0, _, _) in outs],
        out_shape=[_out_struct(k, w, S, tw, dt, nh) for (k, w, c0, tw, dt) in outs],
    )(*[a for (a, _, _, _) in ins], *params)


def rowwise_bwd(name, f, S, ts, nh, ins, params, cts, row_grads, extra=None, extra_outs=(), primal_sum=False):
    n_in, n_p, n_ct = len(ins), len(params), len(cts)
    n_rg, n_ex = len(row_grads), len(extra_outs)

    def body(*refs):
        first = (pl.program_id(0) == 0) & (pl.program_id(1) == 0)
        in_refs = refs[:n_in]
        p_refs = refs[n_in:n_in + n_p]
        ct_refs = refs[n_in + n_p:n_in + n_p + n_ct]
        o = n_in + n_p + n_ct
        rg_refs = refs[o:o + n_rg]
        ex_refs = refs[o + n_rg:o + n_rg + n_ex]
        acc_refs = refs[o + n_rg + n_ex:]
        vals = [_ld(r, s[1]).astype(F32) for r, s in zip(in_refs, ins)]
        pv = [r[...] for r in p_refs]
        res, vjp = jax.vjp(f, *vals, *pv)
        if primal_sum:
            cv = [jnp.ones_like(res[0])] + [_ld(r, s[1]).astype(F32) for r, s in zip(ct_refs, cts)]
        else:
            cv = [_ld(r, s[1]).astype(F32) for r, s in zip(ct_refs, cts)]
        grads = vjp(tuple(cv))
        for r, s in zip(rg_refs, row_grads):
            _st(r, s[1], grads[s[0]])
        if extra is not None:
            for r, s, e in zip(ex_refs, extra_outs, extra(vals, cv, grads)):
                _st(r, s[0], e)

        @pl.when(first)
        def _():
            for r in acc_refs:
                r[...] = jnp.zeros_like(r)

        for r, g in zip(acc_refs[:n_p], grads[n_in:]):
            r[...] += g
        if primal_sum:
            acc_refs[n_p][...] += jnp.sum(res[0])

    acc_shapes = [jax.ShapeDtypeStruct(p.shape, F32) for p in params]
    acc_specs = [_full_spec(p) for p in params]
    if primal_sum:
        acc_shapes.append(jax.ShapeDtypeStruct((1, 128), F32))
        acc_specs.append(pl.BlockSpec((1, 128), lambda i, h: (0, 0)))
    return pl.pallas_call(
        body, name=name, grid=(S // ts, nh),
        in_specs=[_bspec(k, w, c0, ts) for (_, k, w, c0) in ins] + [_full_spec(p) for p in params]
        + [_bspec(k, w, c0, ts) for (_, k, w, c0) in cts],
        out_specs=[_bspec(k, w, c0, ts) for (_, k, w, c0, _, _) in row_grads]
        + [_bspec(k, w, c0, ts) for (k, w, c0, _, _) in extra_outs] + acc_specs,
        out_shape=[_out_struct(k, w, S, tw, dt, nh) for (_, k, w, c0, tw, dt) in row_grads]
        + [_out_struct(k, w, S, tw, dt, nh) for (k, w, c0, tw, dt) in extra_outs] + acc_shapes,
    )(*[a for (a, _, _, _) in ins], *params, *[a for (a, _, _, _) in cts])


def f_prologue(x, ng, sc, sh):
    return (_rms(x, ng) * (1.0 + sc) + sh,)


def f_prologue_res(x, ng, sc, sh):
    return (_rms(x, ng) * (1.0 + sc) + sh, x)


def f_loss(x, mo, tgt, gate, fg):
    y = _rms(x + gate * mo, fg)
    return (0.5 * jnp.mean(jnp.square(y - tgt), axis=-1, keepdims=True),)


def f_foxpre(fq, fk, fv, qg, kg):
    return (_rms(fq, qg) * ATT_SCALE, _rms(fk, kg), fv)


def f_foxnorm(fq, fk, qg, kg):
    return (_rms(fq, qg), _rms(fk, kg))


def f_postf(o, z):
    return (o * _silu(z),)


def f_postg(o, z, ng):
    return (_rms(o, ng) * _silu(z),)


def f_gdnpre(yq, yk, yv):
    return (_l2n(_silu(yq)) * ATT_SCALE, _l2n(_silu(yk)), _silu(yv))


def matmul(name, a, b, out_dtype, tm=512, tn=512, tk=2048, trans_b=False):
    M, K = a.shape
    N, K2 = b.shape if trans_b else b.shape[::-1]
    assert K == K2
    tm, tn, tk = _tile(M, tm), _tile(N, tn), _tile(K, tk)
    nk = K // tk
    b_spec = pl.BlockSpec((tn, tk), lambda i, j, k: (j, k)) if trans_b else pl.BlockSpec((tk, tn), lambda i, j, k: (k, j))

    def body(a_ref, b_ref, o_ref, *scr):
        part = lax.dot_general(a_ref[...], b_ref[...], (((1,), (1 if trans_b else 0,)), ((), ())),
                               preferred_element_type=F32)
        if nk == 1:
            o_ref[...] = part.astype(o_ref.dtype)
            return
        acc = scr[0]
        k = pl.program_id(2)

        @pl.when(k == 0)
        def _():
            acc[...] = part

        @pl.when(k > 0)
        def _():
            acc[...] += part

        @pl.when(k == nk - 1)
        def _():
            o_ref[...] = acc[...].astype(o_ref.dtype)

    return pl.pallas_call(
        body, name=name, grid=(M // tm, N // tn, nk),
        in_specs=[pl.BlockSpec((tm, tk), lambda i, j, k: (i, k)), b_spec],
        out_specs=pl.BlockSpec((tm, tn), lambda i, j, k: (i, j)),
        out_shape=jax.ShapeDtypeStruct((M, N), out_dtype),
        scratch_shapes=[] if nk == 1 else [pltpu.VMEM((tm, tn), F32)],
        compiler_params=pltpu.CompilerParams(dimension_semantics=("parallel", "parallel", "arbitrary")),
    )(a, b)


def _small_f(z, pv, av):
    lane = lax.broadcasted_iota(jnp.int32, z.shape, 1)
    zz = z + pv
    logf = _log_sigmoid(zz)
    gval = -jnp.exp(av) * _softplus(zz)
    beta = _sigmoid(zz)
    return jnp.where(lane < 8, logf, jnp.where(lane < 16, gval, jnp.where(lane < 24, beta, 0.0)))


def _tri(ts, upper):
    r = lax.broadcasted_iota(jnp.int32, (ts, ts), 0)
    c = lax.broadcasted_iota(jnp.int32, (ts, ts), 1)
    keep = (r <= c) if upper else (r >= c)
    same_chunk = (r // CHUNK) == (c // CHUNK)
    return keep.astype(F32), (keep & same_chunk).astype(F32)


def smalls_fwd(p_all, pv, av, S, ts):
    nt = S // ts

    def body(z_ref, pv_ref, av_ref, o_ref, carry):
        i = pl.program_id(0)

        @pl.when(i == 0)
        def _():
            carry[...] = jnp.zeros_like(carry)

        e = _small_f(z_ref[...], pv_ref[...], av_ref[...])
        lane = lax.broadcasted_iota(jnp.int32, e.shape, 1)
        l_all, l_chunk = _tri(ts, upper=False)
        cum_all = hdot(l_all, e) + carry[...]
        cum_chunk = hdot(l_chunk, e)
        carry[...] = cum_all[ts - 1:ts, :]
        o_ref[...] = jnp.where(lane < 8, cum_all, jnp.where(lane < 16, cum_chunk, e))

    return pl.pallas_call(
        body, name="smalls_fwd", grid=(nt,),
        in_specs=[pl.BlockSpec((ts, 128), lambda i: (i, CB_SMALL)), pl.BlockSpec((1, 128), lambda i: (0, 0)),
                  pl.BlockSpec((1, 128), lambda i: (0, 0))],
        out_specs=pl.BlockSpec((ts, 128), lambda i: (i, 0)),
        out_shape=jax.ShapeDtypeStruct((S, 128), F32),
        scratch_shapes=[pltpu.VMEM((1, 128), F32)],
    )(p_all, pv, av)


def smalls_bwd(p_all, pv, av, dsm, S, ts):
    nt = S // ts

    def body(z_ref, pv_ref, av_ref, d_ref, dz_ref, dpv_ref, dav_ref, carry):
        i = pl.program_id(0)

        @pl.when(i == 0)
        def _():
            carry[...] = jnp.zeros_like(carry)
            dpv_ref[...] = jnp.zeros_like(dpv_ref)
            dav_ref[...] = jnp.zeros_like(dav_ref)

        d = d_ref[...]
        lane = lax.broadcasted_iota(jnp.int32, d.shape, 1)
        u_all, u_chunk = _tri(ts, upper=True)
        rev_all = hdot(u_all, d) + carry[...]
        rev_chunk = hdot(u_chunk, d)
        carry[...] = rev_all[0:1, :]
        de = jnp.where(lane < 8, rev_all, jnp.where(lane < 16, rev_chunk, d))
        _, vjp = jax.vjp(_small_f, z_ref[...], pv_ref[...], av_ref[...])
        dz, dpv, dav = vjp(de)
        dz_ref[...] = dz
        dpv_ref[...] += dpv
        dav_ref[...] += dav

    rev = lambda i: (nt - 1 - i, 0)
    return pl.pallas_call(
        body, name="smalls_bwd", grid=(nt,),
        in_specs=[pl.BlockSpec((ts, 128), lambda i: (nt - 1 - i, CB_SMALL)), pl.BlockSpec((1, 128), lambda i: (0, 0)),
                  pl.BlockSpec((1, 128), lambda i: (0, 0)), pl.BlockSpec((ts, 128), rev)],
        out_specs=[pl.BlockSpec((ts, 128), rev), pl.BlockSpec((1, 128), lambda i: (0, 0)),
                   pl.BlockSpec((1, 128), lambda i: (0, 0))],
        out_shape=[jax.ShapeDtypeStruct((S, 128), F32), jax.ShapeDtypeStruct((1, 128), F32),
                   jax.ShapeDtypeStruct((1, 128), F32)],
        scratch_shapes=[pltpu.VMEM((1, 128), F32)],
    )(p_all, pv, av, dsm)


def _causal_mask(s, keys_first):
    r = lax.broadcasted_iota(jnp.int32, s.shape, 0)
    c = lax.broadcasted_iota(jnp.int32, s.shape, 1)
    return jnp.where((r <= c) if keys_first else (c <= r), s, NEG)


def flash_fwd(qs, kn, vb, frow, S, tq):
    nq = S // tq
    tk = tq

    def body(q_ref, k_ref, v_ref, fr_ref, o_ref, lse_ref):
        i = pl.program_id(1)
        q = q_ref[...]
        f_base = fr_ref[0, i][:, 0:1]

        def tile(j, carry, diagonal):
            m, l, acc = carry
            off = pl.multiple_of(j * tk, tk)
            k = k_ref[pl.ds(off, tk), :]
            v = v_ref[pl.ds(off, tk), :]
            s = mm_nt(q, k) - (fr_ref[0, j] - f_base)
            if diagonal:
                s = _causal_mask(s, keys_first=False)
            m_new = jnp.maximum(m, jnp.max(s, axis=-1, keepdims=True))
            a = jnp.exp(m - m_new)
            p = jnp.exp(s - m_new)
            l = a * l + jnp.sum(p, axis=-1, keepdims=True)
            acc = a * acc + mm_nn(p, v)
            return m_new, l, acc

        init = (jnp.full((tq, 1), NEG, F32), jnp.zeros((tq, 1), F32), jnp.zeros((tq, D_HEAD), F32))
        carry = lax.fori_loop(0, i, lambda j, c: tile(j, c, False), init)
        m, l, acc = tile(i, carry, True)
        o_ref[...] = acc / l
        lse_ref[0] = m + jnp.log(l)

    return pl.pallas_call(
        body, name="flash_fwd", grid=(N_HEADS, nq),
        in_specs=[pl.BlockSpec((tq, D_HEAD), lambda h, i: (i, h)), pl.BlockSpec((S, D_HEAD), lambda h, i: (0, h)),
                  pl.BlockSpec((S, D_HEAD), lambda h, i: (0, h)),
                  pl.BlockSpec((1, nq, 1, tk), lambda h, i: (h, 0, 0, 0))],
        out_specs=[pl.BlockSpec((tq, D_HEAD), lambda h, i: (i, h)), pl.BlockSpec((1, tq, 1), lambda h, i: (h, i, 0))],
        out_shape=[jax.ShapeDtypeStruct((S, WIDTH), F32), jax.ShapeDtypeStruct((N_HEADS, S, 1), F32)],
    )(qs, kn, vb, frow)


def flash_bwd_dq(qs, kn, vb, do, frow, lse, delta, S, tq):
    nq = S // tq
    tk = tq

    def body(q_ref, k_ref, v_ref, do_ref, fr_ref, lse_ref, dl_ref, dq_ref, dfq_ref):
        i = pl.program_id(1)
        q = q_ref[...]
        do_ = do_ref[...]
        lse_ = lse_ref[0]
        dl = dl_ref[0]
        f_base = fr_ref[0, i][:, 0:1]

        def tile(j, carry, diagonal):
            dq, dfq = carry
            off = pl.multiple_of(j * tk, tk)
            k = k_ref[pl.ds(off, tk), :]
            v = v_ref[pl.ds(off, tk), :]
            s = mm_nt(q, k) - (fr_ref[0, j] - f_base)
            if diagonal:
                s = _causal_mask(s, keys_first=False)
            p = jnp.exp(s - lse_)
            ds = p * (mm_nt(do_, v) - dl)
            return dq + mm_nn(ds, k), dfq + jnp.sum(ds, axis=-1, keepdims=True)

        carry = lax.fori_loop(0, i, lambda j, c: tile(j, c, False),
                              (jnp.zeros((tq, D_HEAD), F32), jnp.zeros((tq, 1), F32)))
        dq, dfq = tile(i, carry, True)
        dq_ref[...] = dq * ATT_SCALE
        dfq_ref[0] = dfq

    col = pl.BlockSpec((1, tq, 1), lambda h, i: (h, i, 0))
    blk = pl.BlockSpec((tq, D_HEAD), lambda h, i: (i, h))
    full = pl.BlockSpec((S, D_HEAD), lambda h, i: (0, h))
    return pl.pallas_call(
        body, name="flash_bwd_dq", grid=(N_HEADS, nq),
        in_specs=[blk, full, full, blk, pl.BlockSpec((1, nq, 1, tk), lambda h, i: (h, 0, 0, 0)), col, col],
        out_specs=[blk, col],
        out_shape=[jax.ShapeDtypeStruct((S, WIDTH), F32), jax.ShapeDtypeStruct((N_HEADS, S, 1), F32)],
    )(qs, kn, vb, do, frow, lse, delta)


def flash_bwd_dkv(qs, kn, vb, do, fcol, frow, lse_row, delta_row, S, tq):
    nq = S // tq
    tk = tq

    def body(q_ref, k_ref, v_ref, do_ref, fc_ref, fr_ref, lse_ref, dl_ref, dk_ref, dv_ref, dfk_ref):
        j = pl.program_id(1)
        k = k_ref[...]
        v = v_ref[...]
        fk = fc_ref[0]

        def tile(i, carry, diagonal):
            dk, dv, dfk = carry
            off = pl.multiple_of(i * tq, tq)
            q = q_ref[pl.ds(off, tq), :]
            do_ = do_ref[pl.ds(off, tq), :]
            st = mm_nt(k, q) - (fk - fr_ref[0, i][:, 0:1])
            if diagonal:
                st = _causal_mask(st, keys_first=True)
            pt = jnp.exp(st - lse_ref[0, i])
            dst = pt * (mm_nt(v, do_) - dl_ref[0, i])
            return dk + mm_nn(dst, q), dv + mm_nn(pt, do_), dfk - jnp.sum(dst, axis=-1, keepdims=True)

        z = jnp.zeros((tk, D_HEAD), F32)
        carry = tile(j, (z, z, jnp.zeros((tk, 1), F32)), True)
        dk, dv, dfk = lax.fori_loop(j + 1, nq, lambda i, c: tile(i, c, False), carry)
        dk_ref[...] = dk
        dv_ref[...] = dv
        dfk_ref[0] = dfk

    col = pl.BlockSpec((1, tk, 1), lambda h, j: (h, j, 0))
    blk = pl.BlockSpec((tk, D_HEAD), lambda h, j: (j, h))
    full = pl.BlockSpec((S, D_HEAD), lambda h, j: (0, h))
    rows = pl.BlockSpec((1, nq, 1, tq), lambda h, j: (h, 0, 0, 0))
    return pl.pallas_call(
        body, name="flash_bwd_dkv", grid=(N_HEADS, nq),
        in_specs=[full, blk, blk, full, col, rows, rows, rows],
        out_specs=[blk, blk, col],
        out_shape=[jax.ShapeDtypeStruct((S, WIDTH), F32), jax.ShapeDtypeStruct((S, WIDTH), F32),
                   jax.ShapeDtypeStruct((N_HEADS, S, 1), F32)],
    )(qs, kn, vb, do, fcol, frow, lse_row, delta_row)


CONV_COLS = 3 * WIDTH
CONV_TC = 512


def conv_fwd(p_all, conv_w, S, ts):
    nt, tc = S // ts, CONV_TC
    cb0 = CB_GQ * 128 // tc
    r8 = ts // 8

    def body(x_ref, halo_ref, w_ref, y_ref, scr):
        i = pl.program_id(1)
        scr[0:8, :] = jnp.where(i > 0, halo_ref[...], 0.0)
        scr[8:8 + ts, :] = x_ref[...]
        w = w_ref[...]
        y = w[3:4, :] * x_ref[...]
        for j in range(CONV_K - 1):
            y = y + w[j:j + 1, :] * scr[5 + j:5 + j + ts, :]
        y_ref[...] = y

    return pl.pallas_call(
        body, name="conv_fwd", grid=(CONV_COLS // tc, nt),
        in_specs=[pl.BlockSpec((ts, tc), lambda c, i: (i, cb0 + c)),
                  pl.BlockSpec((8, tc), lambda c, i: (jnp.maximum(i * r8 - 1, 0), cb0 + c)),
                  pl.BlockSpec((CONV_K, tc), lambda c, i: (0, c))],
        out_specs=pl.BlockSpec((ts, tc), lambda c, i: (i, c)),
        out_shape=jax.ShapeDtypeStruct((S, CONV_COLS), F32),
        scratch_shapes=[pltpu.VMEM((ts + 8, tc), F32)],
    )(p_all, p_all, conv_w)


def conv_bwd(p_all, dy, conv_w, S, ts):
    nt, tc = S // ts, CONV_TC
    cb0 = CB_GQ * 128 // tc
    r8 = ts // 8

    def body(x_ref, xh_ref, dy_ref, dyh_ref, w_ref, dx_ref, dw_ref, xs, ds):
        i = pl.program_id(1)
        xs[0:8, :] = jnp.where(i > 0, xh_ref[...], 0.0)
        xs[8:8 + ts, :] = x_ref[...]
        ds[0:ts, :] = dy_ref[...]
        ds[ts:ts + 8, :] = jnp.where(i < nt - 1, dyh_ref[...], 0.0)
        w = w_ref[...]
        d = dy_ref[...]
        dx = w[3:4, :] * d
        for j in range(CONV_K - 1):
            dx = dx + w[j:j + 1, :] * ds[3 - j:3 - j + ts, :]
        dx_ref[...] = dx

        @pl.when(i == 0)
        def _():
            dw_ref[...] = jnp.zeros_like(dw_ref)

        rows = [jnp.sum(d * xs[5 + j:5 + j + ts, :], axis=0, keepdims=True) for j in range(CONV_K - 1)]
        rows.append(jnp.sum(d * x_ref[...], axis=0, keepdims=True))
        dw_ref[...] += jnp.concatenate(rows, axis=0)

    return pl.pallas_call(
        body, name="conv_bwd", grid=(CONV_COLS // tc, nt),
        in_specs=[pl.BlockSpec((ts, tc), lambda c, i: (i, cb0 + c)),
                  pl.BlockSpec((8, tc), lambda c, i: (jnp.maximum(i * r8 - 1, 0), cb0 + c)),
                  pl.BlockSpec((ts, tc), lambda c, i: (i, c)),
                  pl.BlockSpec((8, tc), lambda c, i: (jnp.minimum((i + 1) * r8, nt * r8 - 1), c)),
                  pl.BlockSpec((CONV_K, tc), lambda c, i: (0, c))],
        out_specs=[pl.BlockSpec((ts, tc), lambda c, i: (i, c)), pl.BlockSpec((CONV_K, tc), lambda c, i: (0, c))],
        out_shape=[jax.ShapeDtypeStruct((S, CONV_COLS), F32), jax.ShapeDtypeStruct((CONV_K, CONV_COLS), F32)],
        scratch_shapes=[pltpu.VMEM((ts + 8, tc), F32), pltpu.VMEM((ts + 8, tc), F32)],
    )(p_all, p_all, dy, dy, conv_w)


def _bdot(a, b, ca, cb):
    return lax.dot_general(a.astype(MXU_DTYPE), b.astype(MXU_DTYPE), (((ca,), (cb,)), ((0,), (0,))),
                           preferred_element_type=F32)


@jax.custom_vjp
def bmm_nn(a, b):
    return _bdot(a, b, 2, 1)


@jax.custom_vjp
def bmm_nt(a, b):
    return _bdot(a, b, 2, 2)


@jax.custom_vjp
def bmm_tn(a, b):
    return _bdot(a, b, 1, 1)


bmm_nn.defvjp(lambda a, b: (bmm_nn(a, b), (a, b)), lambda r, g: (bmm_nt(g, r[1]), bmm_tn(r[0], g)))
bmm_nt.defvjp(lambda a, b: (bmm_nt(a, b), (a, b)), lambda r, g: (bmm_nn(g, r[1]), bmm_tn(g, r[0])))
bmm_tn.defvjp(lambda a, b: (bmm_tn(a, b), (a, b)), lambda r, g: (bmm_nt(r[1], g), bmm_nn(r[0], g)))


def bdot_hi(a, b):
    return lax.dot_general(a, b, (((2,), (1,)), ((0,), (0,))), precision=SOLVE_PRECISION, preferred_element_type=F32)


def chunk_fn(q, k, v, gc_col, gc_row, beta, state):
    shape = (N_HEADS, CHUNK, CHUNK)
    ii = lax.broadcasted_iota(jnp.int32, shape, 1)
    jj = lax.broadcasted_iota(jnp.int32, shape, 2)
    lower = ii >= jj
    strict = ii > jj
    eye = (ii == jj).astype(F32)
    decay = jnp.exp(jnp.where(lower, gc_col - gc_row, NEG))
    kb = k * beta
    vb = v * beta
    m = jnp.where(strict, bmm_nt(kb, k) * decay, 0.0)
    t = eye - m
    pw = bdot_hi(m, m)
    for _ in range(4):
        t = t + bdot_hi(t, pw)
        pw = bdot_hi(pw, pw)
    t = t + bdot_hi(t, pw)
    u = bdot_hi(t, vb)
    w = bdot_hi(t, kb * jnp.exp(gc_col))
    attn = jnp.where(lower, bmm_nt(q, k) * decay, 0.0)
    v_new = u - bmm_nn(w, state)
    o = bmm_nn(q * jnp.exp(gc_col), state) + bmm_nn(attn, v_new)
    g_last = gc_col[:, CHUNK - 1:CHUNK, :]
    k_dec = k * jnp.exp(g_last - gc_col)
    new_state = state * jnp.exp(g_last) + bmm_tn(k_dec, v_new)
    return o, new_state


def _heads(ref):
    return jnp.stack([ref[:, h * D_HEAD:(h + 1) * D_HEAD] for h in range(N_HEADS)], axis=0)


def _head_cols(ref, lane0):
    return jnp.stack([ref[:, lane0 + h:lane0 + h + 1] for h in range(N_HEADS)], axis=0)


def gdn_fwd(q, k, v, smalls, gc_row, S):
    n = S // CHUNK

    def body(q_ref, k_ref, v_ref, sm_ref, gr_ref, o_ref, st_ref, state):
        @pl.when(pl.program_id(0) == 0)
        def _():
            state[...] = jnp.zeros_like(state)

        s0 = state[...]
        st_ref[0] = s0
        o, s1 = chunk_fn(_heads(q_ref), _heads(k_ref), _heads(v_ref), _head_cols(sm_ref, 8), gr_ref[0][:, None, :],
                         _head_cols(sm_ref, 16), s0)
        for h in range(N_HEADS):
            o_ref[:, h * D_HEAD:(h + 1) * D_HEAD] = o[h]
        state[...] = s1

    return pl.pallas_call(
        body, name="gdn_chunk_fwd", grid=(n,),
        in_specs=[pl.BlockSpec((CHUNK, WIDTH), lambda i: (i, 0))] * 3
        + [pl.BlockSpec((CHUNK, 128), lambda i: (i, 0)), pl.BlockSpec((1, N_HEADS, CHUNK), lambda i: (i, 0, 0))],
        out_specs=[pl.BlockSpec((CHUNK, WIDTH), lambda i: (i, 0)),
                   pl.BlockSpec((1, N_HEADS, D_HEAD, D_HEAD), lambda i: (i, 0, 0, 0))],
        out_shape=[jax.ShapeDtypeStruct((S, WIDTH), F32), jax.ShapeDtypeStruct((n, N_HEADS, D_HEAD, D_HEAD), F32)],
        scratch_shapes=[pltpu.VMEM((N_HEADS, D_HEAD, D_HEAD), F32)],
    )(q, k, v, smalls, gc_row)


def gdn_bwd(q, k, v, smalls, gc_row, states, do, S):
    n = S // CHUNK

    def body(q_ref, k_ref, v_ref, sm_ref, gr_ref, st_ref, do_ref, dqkv_ref, dsm_ref, dgr_ref, dstate):
        @pl.when(pl.program_id(0) == 0)
        def _():
            dstate[...] = jnp.zeros_like(dstate)

        _, vjp = jax.vjp(chunk_fn, _heads(q_ref), _heads(k_ref), _heads(v_ref), _head_cols(sm_ref, 8),
                         gr_ref[0][:, None, :], _head_cols(sm_ref, 16), st_ref[0])
        dq, dk, dv, dgc, dgr, dbt, ds = vjp((_heads(do_ref), dstate[...]))
        for h in range(N_HEADS):
            dqkv_ref[:, h * D_HEAD:(h + 1) * D_HEAD] = dq[h]
            dqkv_ref[:, WIDTH + h * D_HEAD:WIDTH + (h + 1) * D_HEAD] = dk[h]
            dqkv_ref[:, 2 * WIDTH + h * D_HEAD:2 * WIDTH + (h + 1) * D_HEAD] = dv[h]
        dstate[...] = ds
        cols = [dgc[h] for h in range(N_HEADS)] + [dbt[h] for h in range(N_HEADS)]
        dsm_ref[...] = jnp.concatenate([jnp.zeros((CHUNK, 8), F32)] + cols + [jnp.zeros((CHUNK, 128 - 24), F32)],
                                       axis=1)
        dgr_ref[0] = jnp.concatenate([dgr[h] for h in range(N_HEADS)], axis=0)

    rev = lambda c: (lambda i: (n - 1 - i, c))
    return pl.pallas_call(
        body, name="gdn_chunk_bwd", grid=(n,),
        in_specs=[pl.BlockSpec((CHUNK, WIDTH), rev(0))] * 3
        + [pl.BlockSpec((CHUNK, 128), rev(0)), pl.BlockSpec((1, N_HEADS, CHUNK), lambda i: (n - 1 - i, 0, 0)),
                  pl.BlockSpec((1, N_HEADS, D_HEAD, D_HEAD), lambda i: (n - 1 - i, 0, 0, 0)),
                  pl.BlockSpec((CHUNK, WIDTH), rev(0))],
        out_specs=[pl.BlockSpec((CHUNK, 3 * WIDTH), rev(0)), pl.BlockSpec((CHUNK, 128), rev(0)),
                   pl.BlockSpec((1, N_HEADS, CHUNK), lambda i: (n - 1 - i, 0, 0))],
        out_shape=[jax.ShapeDtypeStruct((S, 3 * WIDTH), F32)] + [jax.ShapeDtypeStruct((S, 128), F32),
                                                                  jax.ShapeDtypeStruct((n, N_HEADS, CHUNK), F32)],
        scratch_shapes=[pltpu.VMEM((N_HEADS, D_HEAD, D_HEAD), F32)],
    )(q, k, v, smalls, gc_row, states, do)


def _flip(a, bit):
    return 1 - a if bit else a


def allgather8(name, buf):
    R, W = buf.shape

    def body(x_ref, out_ref, send_sems, recv_sems, local_sem):
        x, y, c = lax.axis_index("x"), lax.axis_index("y"), lax.axis_index("c")

        def slot(px, py, pc):
            return out_ref.at[4 * px + 2 * py + pc]

        mine = pltpu.make_async_copy(x_ref, slot(x, y, c), local_sem)
        mine.start()
        sends = []
        for r in range(1, N_DEV):
            peer = (_flip(x, r & 4), _flip(y, r & 2), _flip(c, r & 1))
            cp = pltpu.make_async_remote_copy(src_ref=x_ref, dst_ref=slot(x, y, c), send_sem=send_sems.at[r - 1],
                                              recv_sem=recv_sems.at[r - 1], device_id=peer, device_id_type=MESH_ID)
            cp.start()
            sends.append(cp)
        for r in range(1, N_DEV):
            peer = (_flip(x, r & 4), _flip(y, r & 2), _flip(c, r & 1))
            pltpu.make_async_remote_copy(src_ref=x_ref, dst_ref=slot(*peer), send_sem=send_sems.at[r - 1],
                                         recv_sem=recv_sems.at[r - 1], device_id=peer,
                                         device_id_type=MESH_ID).wait_recv()
        for cp in sends:
            cp.wait_send()
        mine.wait()

    return pl.pallas_call(
        body, name=name,
        out_shape=jax.ShapeDtypeStruct((N_DEV, R, W), buf.dtype),
        in_specs=[pl.BlockSpec(memory_space=pltpu.VMEM)],
        out_specs=pl.BlockSpec(memory_space=pltpu.VMEM),
        scratch_shapes=[pltpu.SemaphoreType.DMA((N_DEV - 1,)), pltpu.SemaphoreType.DMA((N_DEV - 1,)),
                        pltpu.SemaphoreType.DMA],
    )(buf)


def allgather_chips(name, blk):
    def body(x_ref, out_ref, send_sems, recv_sems, local_sem):
        x, y, c = lax.axis_index("x"), lax.axis_index("y"), lax.axis_index("c")
        mine = pltpu.make_async_copy(x_ref, out_ref.at[2 * x + y], local_sem)
        mine.start()
        sends = []
        for r in range(1, N_CHIPS):
            peer = (_flip(x, r & 2), _flip(y, r & 1), c)
            cp = pltpu.make_async_remote_copy(src_ref=x_ref, dst_ref=out_ref.at[2 * x + y], send_sem=send_sems.at[r - 1],
                                              recv_sem=recv_sems.at[r - 1], device_id=peer, device_id_type=MESH_ID)
            cp.start()
            sends.append(cp)
        for r in range(1, N_CHIPS):
            peer = (_flip(x, r & 2), _flip(y, r & 1), c)
            pltpu.make_async_remote_copy(src_ref=x_ref, dst_ref=out_ref.at[2 * peer[0] + peer[1]],
                                         send_sem=send_sems.at[r - 1], recv_sem=recv_sems.at[r - 1], device_id=peer,
                                         device_id_type=MESH_ID).wait_recv()
        for cp in sends:
            cp.wait_send()
        mine.wait()

    return pl.pallas_call(
        body, name=name,
        out_shape=jax.ShapeDtypeStruct((N_CHIPS,) + blk.shape, blk.dtype),
        in_specs=[pl.BlockSpec(memory_space=pl.ANY)],
        out_specs=pl.BlockSpec(memory_space=pl.ANY),
        scratch_shapes=[pltpu.SemaphoreType.DMA((N_CHIPS - 1,)), pltpu.SemaphoreType.DMA((N_CHIPS - 1,)),
                        pltpu.SemaphoreType.DMA],
    )(blk)


def scatter_chips(name, parts):
    def body(x_ref, out_ref, send_sems, recv_sems):
        x, y, c = lax.axis_index("x"), lax.axis_index("y"), lax.axis_index("c")
        sends = []
        for r in range(1, N_CHIPS):
            peer = (_flip(x, r & 2), _flip(y, r & 1), c)
            cp = pltpu.make_async_remote_copy(src_ref=x_ref.at[2 * peer[0] + peer[1]], dst_ref=out_ref.at[r - 1],
                                              send_sem=send_sems.at[r - 1], recv_sem=recv_sems.at[r - 1],
                                              device_id=peer, device_id_type=MESH_ID)
            cp.start()
            sends.append(cp)
        for cp in sends:
            cp.wait_recv()
        for cp in sends:
            cp.wait_send()

    return pl.pallas_call(
        body, name=name,
        out_shape=jax.ShapeDtypeStruct((N_CHIPS - 1,) + parts.shape[1:], parts.dtype),
        in_specs=[pl.BlockSpec(memory_space=pl.ANY)],
        out_specs=pl.BlockSpec(memory_space=pl.ANY),
        scratch_shapes=[pltpu.SemaphoreType.DMA((N_CHIPS - 1,)), pltpu.SemaphoreType.DMA((N_CHIPS - 1,))],
    )(parts)


def sibling_swap(name, buf):
    def body(x_ref, out_ref, send_sem, recv_sem):
        x, y, c = lax.axis_index("x"), lax.axis_index("y"), lax.axis_index("c")
        cp = pltpu.make_async_remote_copy(src_ref=x_ref, dst_ref=out_ref, send_sem=send_sem, recv_sem=recv_sem,
                                          device_id=(x, y, 1 - c), device_id_type=MESH_ID)
        cp.start()
        cp.wait_recv()
        cp.wait_send()

    return pl.pallas_call(
        body, name=name,
        out_shape=jax.ShapeDtypeStruct(buf.shape, buf.dtype),
        in_specs=[pl.BlockSpec(memory_space=pl.ANY)],
        out_specs=pl.BlockSpec(memory_space=pl.ANY),
        scratch_shapes=[pltpu.SemaphoreType.DMA, pltpu.SemaphoreType.DMA],
    )(buf)


def sum_parts(name, own, recv, tr):
    R, W = own.shape
    tr = _tile(R, tr)

    def body(o_ref, r_ref, out_ref):
        acc = o_ref[...]
        for k in range(recv.shape[0]):
            acc = acc + r_ref[k].astype(F32)
        out_ref[...] = acc

    return pl.pallas_call(
        body, name=name, grid=(R // tr,),
        in_specs=[pl.BlockSpec((tr, W), lambda i: (i, 0)), pl.BlockSpec((recv.shape[0], tr, W), lambda i: (0, i, 0))],
        out_specs=pl.BlockSpec((tr, W), lambda i: (i, 0)),
        out_shape=jax.ShapeDtypeStruct((R, W), F32),
    )(own, recv)


def sum_devices(name, g):
    def body(g_ref, o_ref):
        acc = g_ref[0]
        for d in range(1, N_DEV):
            acc = acc + g_ref[d]
        o_ref[...] = acc

    return pl.pallas_call(body, name=name, out_shape=jax.ShapeDtypeStruct(g.shape[1:], F32))(g)


def ada_fwd(c_all, w_ada):
    K, N = w_ada.shape
    tn = _tile(N, 512)

    def body(c_ref, w_ref, o_ref):
        o_ref[...] = mm_nn(_silu(c_ref[...]), w_ref[...])

    return pl.pallas_call(
        body, name="ada_fwd", grid=(N // tn,),
        in_specs=[pl.BlockSpec((N_DEV, K), lambda j: (0, 0)), pl.BlockSpec((K, tn), lambda j: (0, j))],
        out_specs=pl.BlockSpec((N_DEV, tn), lambda j: (0, j)),
        out_shape=jax.ShapeDtypeStruct((N_DEV, N), F32),
    )(c_all, w_ada)


def ada_grad(c_all, dmod):
    K = c_all.shape[1]
    N = dmod.shape[1]
    tn = _tile(N, 512)

    def body(c_ref, d_ref, o_ref):
        o_ref[...] = mm_tn(_silu(c_ref[...]), d_ref[...])

    return pl.pallas_call(
        body, name="ada_grad", grid=(N // tn,),
        in_specs=[pl.BlockSpec((N_DEV, K), lambda j: (0, 0)), pl.BlockSpec((N_DEV, tn), lambda j: (0, j))],
        out_specs=pl.BlockSpec((K, tn), lambda j: (0, j)),
        out_shape=jax.ShapeDtypeStruct((K, N), F32),
    )(c_all, dmod)


def adamw(name, w, gparts, m, v, tr):
    R, W = w.shape
    tr = _tile(R, tr)
    ng = len(gparts)

    def body(w_ref, *refs):
        g_refs, (m_ref, v_ref, g_out, d_out, m_out, v_out) = refs[:ng], refs[ng:]
        g = g_refs[0][...]
        for r in g_refs[1:]:
            g = g + r[...]
        m1 = ADAM_B1 * m_ref[...] + (1.0 - ADAM_B1) * g
        v1 = ADAM_B2 * v_ref[...] + (1.0 - ADAM_B2) * jnp.square(g)
        m_hat = m1 / (1.0 - ADAM_B1 ** ADAM_STEP)
        v_hat = v1 / (1.0 - ADAM_B2 ** ADAM_STEP)
        g_out[...] = g
        d_out[...] = -ADAM_LR * (m_hat / (jnp.sqrt(v_hat) + ADAM_EPS) + ADAM_WD * w_ref[...])
        m_out[...] = m1
        v_out[...] = v1

    spec = pl.BlockSpec((tr, W), lambda i: (i, 0))
    return pl.pallas_call(
        body, name=name, grid=(R // tr,),
        in_specs=[spec] * (3 + ng), out_specs=[spec] * 4,
        out_shape=[jax.ShapeDtypeStruct((R, W), F32)] * 4,
    )(w, *gparts, m, v)


def _to_internal_cols(w):
    pad = jnp.zeros(w.shape[:-1] + (128 - N_SMALL,), w.dtype)
    a = 4 * WIDTH
    return jnp.concatenate([w[..., :a], w[..., a + 8:2 * a + 8], w[..., a:a + 8], w[..., 2 * a + 8:], pad], axis=-1)


def _from_internal_cols(g):
    a = 4 * WIDTH
    return jnp.concatenate([g[..., :a], g[..., 2 * a:2 * a + 8], g[..., a:2 * a], g[..., 2 * a + 8:2 * a + N_SMALL]],
                           axis=-1)


def _pad_lanes(v, n=128):
    return jnp.pad(v, ((0, 0), (0, n - v.shape[1])))


def kernel(x, c, norm_g, w_ada, b_ada, w_in, b_fgate, fox_qn_g, fox_kn_g, gdn_conv_w, gdn_A_log, gdn_dt_bias, gdn_norm_g, w_out, final_g, loss_target, m_norm_g, m_w_ada, m_b_ada, m_w_in, m_b_fgate, m_fox_qn_g, m_fox_kn_g, m_gdn_conv_w, m_gdn_A_log, m_gdn_dt_bias, m_gdn_norm_g, m_w_out, m_final_g, v_norm_g, v_w_ada, v_b_ada, v_w_in, v_b_fgate, v_fox_qn_g, v_fox_kn_g, v_gdn_conv_w, v_gdn_A_log, v_gdn_dt_bias, v_gdn_norm_g, v_w_out, v_final_g):
    S = x.shape[1]
    H = N_HEADS
    ix, iy, ic = lax.axis_index("x"), lax.axis_index("y"), lax.axis_index("c")
    chip = 2 * ix + iy
    me = 2 * chip + ic
    x2, tgt = x[0], loss_target[0]
    ts_wide = _tile(S, 256)
    ts = _tile(S, 512)
    tq = _tile(S, 512)
    nq = S // tq
    nch = S // CHUNK
    conv_cols_chip = CONV_COLS // N_CHIPS

    pay = jnp.concatenate([c, _pad_lanes(gdn_conv_w[0], D_MODEL), jnp.zeros((3, D_MODEL), F32)], axis=0)
    g1 = allgather8("ag_c", pay)
    c_all = g1[:, 0, :]
    conv_w = jnp.concatenate([g1[2 * j, 1:1 + CONV_K, :conv_cols_chip] for j in range(N_CHIPS)], axis=1)
    g2 = allgather8("ag_mod", ada_fwd(c_all, w_ada[0]))
    mod_rows = jnp.concatenate([g2[2 * j] for j in range(N_CHIPS)], axis=1)
    mod = lax.dynamic_slice_in_dim(mod_rows, me, 1, axis=0) + b_ada
    shift, scale, gate = mod[:, :D_MODEL], mod[:, D_MODEL:2 * D_MODEL], mod[:, 2 * D_MODEL:]

    win4 = allgather_chips("ag_w_in", w_in[0].astype(MXU_DTYPE))
    w_all = _to_internal_cols(jnp.transpose(win4, (1, 0, 2)).reshape(D_MODEL, IN_WIDTH))
    wout4 = allgather_chips("ag_w_out", w_out[0].astype(MXU_DTYPE))
    w_out_full = wout4.reshape(2 * WIDTH, D_MODEL)

    (h_b,) = rowwise_fwd("prologue", f_prologue, S, ts_wide, 1, [(x2, "row", D_MODEL, 0)], [norm_g, scale, shift],
                         [("row", D_MODEL, 0, D_MODEL, MXU_DTYPE)])
    p_all = matmul("in_proj", h_b, w_all, F32, tm=512, tn=640, tk=2048)
    pv = jnp.concatenate([b_fgate, gdn_dt_bias, jnp.zeros((1, 128 - 16), F32)], axis=1)
    av = jnp.concatenate([jnp.zeros((1, 8), F32), gdn_A_log, jnp.zeros((1, 128 - 16), F32)], axis=1)
    smalls = smalls_fwd(p_all, pv, av, S, ts)
    f_t = jnp.transpose(smalls[:, :H])
    fcol = f_t[:, :, None]
    frow = f_t.reshape(H, nq, 1, tq)
    gc_row = jnp.transpose(smalls[:, 8:16].reshape(nch, CHUNK, H), (0, 2, 1))
    head_b = ("head", D_HEAD, 0, WIDTH, MXU_DTYPE)
    head_f = ("head", D_HEAD, 0, WIDTH, F32)
    pcol = lambda cb: (p_all, "head", D_HEAD, cb)
    qn, kn, vb = rowwise_fwd("fox_pre", f_foxpre, S, ts, H, [pcol(CB_FQ), pcol(CB_FK), pcol(CB_FV)],
                             [fox_qn_g, fox_kn_g], [head_b] * 3)
    o_fox, lse = flash_fwd(qn, kn, vb, frow, S, tq)
    y_conv = conv_fwd(p_all, conv_w, S, ts)
    ycol = lambda cb: (y_conv, "head", D_HEAD, cb)
    gq, gk, gv = rowwise_fwd("gdn_pre", f_gdnpre, S, ts, H, [ycol(0), ycol(8), ycol(16)], [], [head_f] * 3)
    o_gdn, states = gdn_fwd(gq, gk, gv, smalls, gc_row, S)
    (mix_f,) = rowwise_fwd("post_fox", f_postf, S, ts, H, [(o_fox, "head", D_HEAD, 0), pcol(CB_FZ)], [], [head_b])
    (mix_g,) = rowwise_fwd("post_gdn", f_postg, S, ts, H, [(o_gdn, "head", D_HEAD, 0), pcol(CB_GZ)], [gdn_norm_g],
                           [head_b])
    mixed = jnp.concatenate([mix_f, mix_g], axis=1)
    mo = matmul("out_proj", mixed, w_out_full, F32)

    wide = lambda a: (a, "row", D_MODEL, 0)
    dx_res, dmo, d_gate, d_final_g, loss_acc = rowwise_bwd(
        "loss", f_loss, S, ts_wide, 1, [wide(x2), wide(mo), wide(tgt)], [gate, final_g[None, :]], [],
        [(0, "row", D_MODEL, 0, D_MODEL, F32), (1, "row", D_MODEL, 0, D_MODEL, MXU_DTYPE)], primal_sum=True)

    d_mixed = matmul("d_mixed", dmo, w_out_full, F32, trans_b=True)
    g_w_out = matmul("g_w_out", jnp.transpose(mixed), dmo, F32)
    do_fox, d_fz, delta = rowwise_bwd(
        "post_fox_bwd", f_postf, S, ts, H, [(o_fox, "head", D_HEAD, 0), pcol(CB_FZ)], [],
        [(d_mixed, "head", D_HEAD, 0)], [(0,) + head_b, (1,) + head_f],
        extra=lambda vals, cv, grads: (jnp.sum(grads[0] * vals[0], axis=-1, keepdims=True),),
        extra_outs=[("head3", 1, 0, None, F32)])
    do_gdn, d_gz, d_gdn_norm_g = rowwise_bwd(
        "post_gdn_bwd", f_postg, S, ts, H, [(o_gdn, "head", D_HEAD, 0), pcol(CB_GZ)], [gdn_norm_g],
        [(d_mixed, "head", D_HEAD, 8)], [(0,) + head_f, (1,) + head_f])
    d_qn, dfq_col = flash_bwd_dq(qn, kn, vb, do_fox, frow, lse, delta, S, tq)
    d_kn, d_fv, dfk_col = flash_bwd_dkv(qn, kn, vb, do_fox, fcol, frow, lse.reshape(H, nq, 1, tq),
                                        delta.reshape(H, nq, 1, tq), S, tq)
    d_fq, d_fk, d_qn_g, d_kn_g = rowwise_bwd(
        "fox_pre_bwd", f_foxnorm, S, ts, H, [pcol(CB_FQ), pcol(CB_FK)], [fox_qn_g, fox_kn_g],
        [(d_qn, "head", D_HEAD, 0), (d_kn, "head", D_HEAD, 0)], [(0,) + head_f, (1,) + head_f])
    d_qkv, dsm_chunk, dgc_row = gdn_bwd(gq, gk, gv, smalls, gc_row, states, do_gdn, S)
    d_yq, d_yk, d_yv = rowwise_bwd(
        "gdn_pre_bwd", f_gdnpre, S, ts, H, [ycol(0), ycol(8), ycol(16)], [],
        [(d_qkv, "head", D_HEAD, 0), (d_qkv, "head", D_HEAD, 8), (d_qkv, "head", D_HEAD, 16)],
        [(0,) + head_f, (1,) + head_f, (2,) + head_f])
    d_xc, d_conv_w = conv_bwd(p_all, jnp.concatenate([d_yq, d_yk, d_yv], axis=1), conv_w, S, ts)
    d_f = jnp.transpose((dfq_col + dfk_col)[:, :, 0])
    d_gc = jnp.transpose(dgc_row, (0, 2, 1)).reshape(S, H)
    dsm = dsm_chunk + jnp.concatenate([d_f, d_gc, jnp.zeros((S, 128 - 16), F32)], axis=1)
    d_small, d_pv, d_av = smalls_bwd(p_all, pv, av, dsm, S, ts)
    dp_all = jnp.concatenate([d_fq, d_fk, d_fv, d_fz, d_xc, d_gz, d_small], axis=1).astype(MXU_DTYPE)
    d_h = matmul("d_h", dp_all, w_all, F32, tm=512, tn=1024, tk=1664, trans_b=True)
    g_w_all = matmul("g_w_in", jnp.transpose(h_b), dp_all, F32, tm=512, tn=640, tk=2048)
    d_x, d_norm_g, d_scale, d_shift = rowwise_bwd(
        "prologue_bwd", f_prologue_res, S, ts_wide, 1, [wide(x2)], [norm_g, scale, shift],
        [wide(d_h), wide(dx_res)], [(0, "row", D_MODEL, 0, D_MODEL, F32)])

    parts = [d_shift, d_scale, d_gate, d_norm_g, d_final_g, d_conv_w.reshape(1, CONV_K * CONV_COLS), d_qn_g, d_kn_g,
             d_gdn_norm_g, d_pv, d_av, loss_acc]
    n_pay = sum(p.shape[1] for p in parts)
    pay_w = -(-n_pay // 1024) * 1024
    pay2 = jnp.concatenate(parts + [jnp.zeros((1, pay_w - n_pay), F32)], axis=1).reshape(8, pay_w // 8)
    g3 = allgather8("ag_small", pay2)
    tot = sum_devices("sum_small", g3).reshape(1, pay_w)
    dmod_all = g3.reshape(N_DEV, pay_w)[:, :3 * D_MODEL]
    o = 3 * D_MODEL
    g_b_ada = tot[:, :o]
    g_norm_g, g_final_g = tot[:, o:o + D_MODEL], tot[:, o + D_MODEL:o + 2 * D_MODEL]
    o += 2 * D_MODEL
    g_conv_full = tot[:, o:o + CONV_K * CONV_COLS].reshape(CONV_K, CONV_COLS)
    g_conv = lax.dynamic_slice_in_dim(g_conv_full, chip * conv_cols_chip, conv_cols_chip, axis=1)
    o += CONV_K * CONV_COLS
    g_qn_g, g_kn_g, g_gdn_ng = tot[:, o:o + 128], tot[:, o + 128:o + 256], tot[:, o + 256:o + 384]
    g_pv, g_av = tot[:, o + 384:o + 512], tot[:, o + 512:o + 640]
    loss = tot[0, o + 640]
    g_b_fgate, g_dt_bias, g_a_log = g_pv[:, :8], g_pv[:, 8:16], g_av[:, 8:16]

    def small_vec(vals):
        flat = [norm for norm in vals[:3]] + [vals[3].reshape(1, CONV_K * conv_cols_chip)] + list(vals[4:7]) \
            + [_pad_lanes(s) for s in vals[7:]]
        n = sum(f.shape[1] for f in flat)
        nw = -(-n // 1024) * 1024
        return jnp.concatenate(flat + [jnp.zeros((1, nw - n), F32)], axis=1).reshape(8, nw // 8)

    sw = small_vec([norm_g, b_ada, final_g[None, :], gdn_conv_w[0], fox_qn_g, fox_kn_g, gdn_norm_g, b_fgate, gdn_A_log,
                    gdn_dt_bias])
    sm = small_vec([m_norm_g, m_b_ada, m_final_g[None, :], m_gdn_conv_w[0], m_fox_qn_g, m_fox_kn_g, m_gdn_norm_g,
                    m_b_fgate, m_gdn_A_log, m_gdn_dt_bias])
    sv = small_vec([v_norm_g, v_b_ada, v_final_g[None, :], v_gdn_conv_w[0], v_fox_qn_g, v_fox_kn_g, v_gdn_norm_g,
                    v_b_fgate, v_gdn_A_log, v_gdn_dt_bias])
    sg = small_vec([g_norm_g, g_b_ada, g_final_g, g_conv, g_qn_g, g_kn_g, g_gdn_ng, g_b_fgate, g_a_log, g_dt_bias])
    small_out = adamw("adamw_small", sw, [sg], sm, sv, 8)

    def split_small(a):
        a = a.reshape(1, -1)
        out, o = [], 0
        for n, shp in ((D_MODEL, (1, D_MODEL)), (3 * D_MODEL, (1, 3 * D_MODEL)), (D_MODEL, (D_MODEL,)),
                       (CONV_K * conv_cols_chip, (1, CONV_K, conv_cols_chip)), (128, (1, 128)), (128, (1, 128)),
                       (128, (1, 128))):
            out.append(a[:, o:o + n].reshape(shp))
            o += n
        for _ in range(3):
            out.append(a[:, o:o + 8])
            o += 128
        return out

    n_ada = w_ada.shape[2]
    g_w_ada = ada_grad(c_all, lax.dynamic_slice_in_dim(dmod_all, chip * n_ada, n_ada, axis=1))
    ada_out = adamw("adamw_w_ada", w_ada[0], [g_w_ada], m_w_ada[0], v_w_ada[0], 128)

    def reduce_scatter(tag, g4, tr):
        own = lax.dynamic_index_in_dim(g4, chip, axis=0, keepdims=False)
        recv = scatter_chips("rs_" + tag, g4.astype(MXU_DTYPE))
        part = sum_parts("sum_" + tag, own, recv, tr)
        return part, sibling_swap("swap_" + tag, part)

    n_in = w_in.shape[2]
    g4_in = jnp.transpose(_from_internal_cols(g_w_all).reshape(D_MODEL, N_CHIPS, n_in), (1, 0, 2))
    in_out = adamw("adamw_w_in", w_in[0], list(reduce_scatter("w_in", g4_in, 128)), m_w_in[0], v_w_in[0], 128)
    g4_out = g_w_out.reshape(N_CHIPS, w_out.shape[1], D_MODEL)
    out_out = adamw("adamw_w_out", w_out[0], list(reduce_scatter("w_out", g4_out, 128)), m_w_out[0], v_w_out[0], 128)

    res = [loss, d_x[None]]
    for k in range(4):
        s = split_small(small_out[k])
        big = [ada_out[k][None], in_out[k][None], out_out[k][None]]
        res += [s[0], big[0], s[1], big[1], s[7], s[4], s[5], s[3], s[8], s[9], s[6], big[2], s[2]]
    return tuple(res)
```

```python
import functools

import jax
import jax.numpy as jnp
from jax import lax
from jax.experimental import pallas as pl
from jax.experimental.pallas import tpu as pltpu

F32 = jnp.float32
MXU_DTYPE = jnp.bfloat16
HIGHEST = lax.Precision.HIGHEST
SOLVE_PRECISION = lax.Precision.HIGH
MESH_ID = pl.DeviceIdType.MESH

D_MODEL = 2048
N_HEADS = 8
D_HEAD = 128
WIDTH = N_HEADS * D_HEAD
CHUNK = 64
CONV_K = 4
EPS = 1e-6
NEG = -1e30
ATT_SCALE = D_HEAD ** -0.5
N_SMALL = 3 * N_HEADS
P_COLS = 8 * WIDTH + 128
CB_FQ, CB_FK, CB_FV, CB_FZ, CB_GQ, CB_GK, CB_GV, CB_GZ, CB_SMALL = 0, 8, 16, 24, 32, 40, 48, 56, 64
IN_WIDTH = 8 * WIDTH + N_SMALL
N_CHIPS = 4
N_DEV = 8

ADAM_LR, ADAM_B1, ADAM_B2, ADAM_EPS, ADAM_WD, ADAM_STEP = 0.001, 0.9, 0.999, 1e-08, 0.01, 10


def _dotg(a, b, dims):
    return lax.dot_general(a.astype(MXU_DTYPE), b.astype(MXU_DTYPE), (dims, ((), ())), preferred_element_type=F32)


@jax.custom_vjp
def mm_nn(a, b):
    return _dotg(a, b, ((1,), (0,)))


@jax.custom_vjp
def mm_nt(a, b):
    return _dotg(a, b, ((1,), (1,)))


@jax.custom_vjp
def mm_tn(a, b):
    return _dotg(a, b, ((0,), (0,)))


mm_nn.defvjp(lambda a, b: (mm_nn(a, b), (a, b)), lambda r, g: (mm_nt(g, r[1]), mm_tn(r[0], g)))
mm_nt.defvjp(lambda a, b: (mm_nt(a, b), (a, b)), lambda r, g: (mm_nn(g, r[1]), mm_tn(g, r[0])))
mm_tn.defvjp(lambda a, b: (mm_tn(a, b), (a, b)), lambda r, g: (mm_nt(r[1], g), mm_nn(r[0], g)))


def hdot(a, b):
    return lax.dot_general(a, b, (((1,), (0,)), ((), ())), precision=HIGHEST, preferred_element_type=F32)


def _sigmoid(x):
    return 0.5 * (jnp.tanh(0.5 * x) + 1.0)


def _silu(x):
    return x * _sigmoid(x)


def _softplus(x):
    return jnp.maximum(x, 0.0) + jnp.log(1.0 + jnp.exp(-jnp.abs(x)))


def _log_sigmoid(x):
    return jnp.minimum(x, 0.0) - jnp.log(1.0 + jnp.exp(-jnp.abs(x)))


def _rms(x, g):
    return x * lax.rsqrt(jnp.mean(x * x, axis=-1, keepdims=True) + EPS) * g


def _l2n(x):
    return x * lax.rsqrt(jnp.sum(x * x, axis=-1, keepdims=True) + EPS)


def _tile(n, pref):
    t = min(n, pref)
    assert n % t == 0, (n, pref)
    return t


def _bspec(kind, w, col0, ts):
    if kind == "row":
        return pl.BlockSpec((ts, w), lambda i, h: (i, col0))
    if kind == "head":
        return pl.BlockSpec((ts, w), lambda i, h: (i, col0 + h))
    assert kind == "head3"
    return pl.BlockSpec((1, ts, w), lambda i, h: (h, i, 0))


def _ld(ref, kind):
    return ref[0] if kind == "head3" else ref[...]


def _st(ref, kind, val):
    if kind == "head3":
        ref[0] = val.astype(ref.dtype)
    else:
        ref[...] = val.astype(ref.dtype)


def _full_spec(a):
    nd = a.ndim
    return pl.BlockSpec(a.shape, lambda i, h: (0,) * nd)


def _out_struct(kind, w, S, width_total, dtype, nh):
    if kind == "head3":
        return jax.ShapeDtypeStruct((nh, S, w), dtype)
    return jax.ShapeDtypeStruct((S, width_total), dtype)


def rowwise_fwd(name, f, S, ts, nh, ins, params, outs):
    n_in, n_p = len(ins), len(params)

    def body(*refs):
        vals = [_ld(r, s[1]).astype(F32) for r, s in zip(refs[:n_in], ins)]
        pv = [r[...] for r in refs[n_in:n_in + n_p]]
        res = f(*vals, *pv)
        for r, s, o in zip(refs[n_in + n_p:], outs, res):
            _st(r, s[0], o)

    return pl.pallas_call(
        body, name=name, grid=(S // ts, nh),
        in_specs=[_bspec(k, w, c0, ts) for (_, k, w, c0) in ins] + [_full_spec(p) for p in params],
        out_specs=[_bspec(k, w, c0, ts) for (k, w, c0, _, _) in outs],
        out_shape=[_out_struct(k, w, S, tw, dt, nh) for (k, w, c0, tw, dt) in outs],
    )(*[a for (a, _, _, _) in ins], *params)


def rowwise_bwd(name, f, S, ts, nh, ins, params, cts, row_grads, extra=None, extra_outs=(), primal_sum=False):
    n_in, n_p, n_ct = len(ins), len(params), len(cts)
    n_rg, n_ex = len(row_grads), len(extra_outs)
    into = {j: rg[6] for j, rg in enumerate(row_grads) if len(rg) > 6}
    row_grads = [rg[:6] for rg in row_grads]
    n_al = len(into)

    def body(*refs):
        first = (pl.program_id(0) == 0) & (pl.program_id(1) == 0)
        in_refs = refs[:n_in]
        p_refs = refs[n_in:n_in + n_p]
        ct_refs = refs[n_in + n_p:n_in + n_p + n_ct]
        o = n_in + n_p + n_ct + n_al
        rg_refs = refs[o:o + n_rg]
        ex_refs = refs[o + n_rg:o + n_rg + n_ex]
        acc_refs = refs[o + n_rg + n_ex:]
        vals = [_ld(r, s[1]).astype(F32) for r, s in zip(in_refs, ins)]
        pv = [r[...] for r in p_refs]
        res, vjp = jax.vjp(f, *vals, *pv)
        if primal_sum:
            cv = [jnp.ones_like(res[0])] + [_ld(r, s[1]).astype(F32) for r, s in zip(ct_refs, cts)]
        else:
            cv = [_ld(r, s[1]).astype(F32) for r, s in zip(ct_refs, cts)]
        grads = vjp(tuple(cv))
        for r, s in zip(rg_refs, row_grads):
            _st(r, s[1], grads[s[0]])
        if extra is not None:
            for r, s, e in zip(ex_refs, extra_outs, extra(vals, cv, grads)):
                _st(r, s[0], e)

        @pl.when(first)
        def _():
            for r in acc_refs:
                r[...] = jnp.zeros_like(r)

        for r, g in zip(acc_refs[:n_p], grads[n_in:]):
            r[...] += g
        if primal_sum:
            acc_refs[n_p][...] += jnp.sum(res[0])

    acc_shapes = [jax.ShapeDtypeStruct(p.shape, F32) for p in params]
    acc_specs = [_full_spec(p) for p in params]
    if primal_sum:
        acc_shapes.append(jax.ShapeDtypeStruct((1, 128), F32))
        acc_specs.append(pl.BlockSpec((1, 128), lambda i, h: (0, 0)))
    rg_shapes = [jax.ShapeDtypeStruct(into[j].shape, into[j].dtype) if j in into else _out_struct(k, w, S, tw, dt, nh)
                 for j, (_, k, w, c0, tw, dt) in enumerate(row_grads)]
    first_alias = n_in + n_p + n_ct
    return pl.pallas_call(
        body, name=name, grid=(S // ts, nh),
        in_specs=[_bspec(k, w, c0, ts) for (_, k, w, c0) in ins] + [_full_spec(p) for p in params]
        + [_bspec(k, w, c0, ts) for (_, k, w, c0) in cts] + [pl.BlockSpec(memory_space=pl.ANY)] * n_al,
        out_specs=[_bspec(k, w, c0, ts) for (_, k, w, c0, _, _) in row_grads]
        + [_bspec(k, w, c0, ts) for (k, w, c0, _, _) in extra_outs] + acc_specs,
        out_shape=rg_shapes + [_out_struct(k, w, S, tw, dt, nh) for (k, w, c0, tw, dt) in extra_outs] + acc_shapes,
        input_output_aliases={first_alias + n: j for n, j in enumerate(sorted(into))},
    )(*[a for (a, _, _, _) in ins], *params, *[a for (a, _, _, _) in cts], *[into[j] for j in sorted(into)])


def f_prologue(x, ng, sc, sh):
    return (_rms(x, ng) * (1.0 + sc) + sh,)


def f_prologue_res(x, ng, sc, sh):
    return (_rms(x, ng) * (1.0 + sc) + sh, x)


def f_loss(x, mo, tgt, gate, fg):
    y = _rms(x + gate * mo, fg)
    return (0.5 * jnp.mean(jnp.square(y - tgt), axis=-1, keepdims=True),)


def f_foxpre(fq, fk, fv, qg, kg):
    return (_rms(fq, qg) * ATT_SCALE, _rms(fk, kg), fv)


def f_rms1(x, g):
    return (_rms(x, g),)


def f_postf(o, z):
    return (o * _silu(z),)


def f_postg(o, z, ng):
    return (_rms(o, ng) * _silu(z),)


def f_gdnpre(yq, yk, yv):
    return (_l2n(_silu(yq)) * ATT_SCALE, _l2n(_silu(yk)), _silu(yv))


def matmul(name, a, b, out_dtype, tm=512, tn=512, tk=2048, trans_b=False):
    M, K = a.shape
    N, K2 = b.shape if trans_b else b.shape[::-1]
    assert K == K2
    tm, tn, tk = _tile(M, tm), _tile(N, tn), _tile(K, tk)
    nk = K // tk
    b_spec = pl.BlockSpec((tn, tk), lambda i, j, k: (j, k)) if trans_b else pl.BlockSpec((tk, tn), lambda i, j, k: (k, j))

    def body(a_ref, b_ref, o_ref, *scr):
        part = lax.dot_general(a_ref[...], b_ref[...], (((1,), (1 if trans_b else 0,)), ((), ())),
                               preferred_element_type=F32)
        if nk == 1:
            o_ref[...] = part.astype(o_ref.dtype)
            return
        acc = scr[0]
        k = pl.program_id(2)

        @pl.when(k == 0)
        def _():
            acc[...] = part

        @pl.when(k > 0)
        def _():
            acc[...] += part

        @pl.when(k == nk - 1)
        def _():
            o_ref[...] = acc[...].astype(o_ref.dtype)

    return pl.pallas_call(
        body, name=name, grid=(M // tm, N // tn, nk),
        in_specs=[pl.BlockSpec((tm, tk), lambda i, j, k: (i, k)), b_spec],
        out_specs=pl.BlockSpec((tm, tn), lambda i, j, k: (i, j)),
        out_shape=jax.ShapeDtypeStruct((M, N), out_dtype),
        scratch_shapes=[] if nk == 1 else [pltpu.VMEM((tm, tn), F32)],
        compiler_params=pltpu.CompilerParams(dimension_semantics=("parallel", "parallel", "arbitrary")),
    )(a, b)


def _small_f(z, pv, av):
    lane = lax.broadcasted_iota(jnp.int32, z.shape, 1)
    zz = z + pv
    logf = _log_sigmoid(zz)
    gval = -jnp.exp(av) * _softplus(zz)
    beta = _sigmoid(zz)
    return jnp.where(lane < 8, logf, jnp.where(lane < 16, gval, jnp.where(lane < 24, beta, 0.0)))


def _tri(ts, upper):
    r = lax.broadcasted_iota(jnp.int32, (ts, ts), 0)
    c = lax.broadcasted_iota(jnp.int32, (ts, ts), 1)
    keep = (r <= c) if upper else (r >= c)
    same_chunk = (r // CHUNK) == (c // CHUNK)
    return keep.astype(F32), (keep & same_chunk).astype(F32)


def smalls_fwd(p_all, pv, av, S, ts):
    nt = S // ts

    def body(z_ref, pv_ref, av_ref, o_ref, carry):
        i = pl.program_id(0)

        @pl.when(i == 0)
        def _():
            carry[...] = jnp.zeros_like(carry)

        e = _small_f(z_ref[...], pv_ref[...], av_ref[...])
        lane = lax.broadcasted_iota(jnp.int32, e.shape, 1)
        l_all, l_chunk = _tri(ts, upper=False)
        cum_all = hdot(l_all, e) + carry[...]
        cum_chunk = hdot(l_chunk, e)
        carry[...] = cum_all[ts - 1:ts, :]
        o_ref[...] = jnp.where(lane < 8, cum_all, jnp.where(lane < 16, cum_chunk, e))

    return pl.pallas_call(
        body, name="smalls_fwd", grid=(nt,),
        in_specs=[pl.BlockSpec((ts, 128), lambda i: (i, CB_SMALL)), pl.BlockSpec((1, 128), lambda i: (0, 0)),
                  pl.BlockSpec((1, 128), lambda i: (0, 0))],
        out_specs=pl.BlockSpec((ts, 128), lambda i: (i, 0)),
        out_shape=jax.ShapeDtypeStruct((S, 128), F32),
        scratch_shapes=[pltpu.VMEM((1, 128), F32)],
    )(p_all, pv, av)


def smalls_bwd(p_all, pv, av, dsm, dp, S, ts):
    nt = S // ts

    def body(z_ref, pv_ref, av_ref, d_ref, dp_in, dz_ref, dpv_ref, dav_ref, carry):
        i = pl.program_id(0)

        @pl.when(i == 0)
        def _():
            carry[...] = jnp.zeros_like(carry)
            dpv_ref[...] = jnp.zeros_like(dpv_ref)
            dav_ref[...] = jnp.zeros_like(dav_ref)

        d = d_ref[...]
        lane = lax.broadcasted_iota(jnp.int32, d.shape, 1)
        u_all, u_chunk = _tri(ts, upper=True)
        rev_all = hdot(u_all, d) + carry[...]
        rev_chunk = hdot(u_chunk, d)
        carry[...] = rev_all[0:1, :]
        de = jnp.where(lane < 8, rev_all, jnp.where(lane < 16, rev_chunk, d))
        _, vjp = jax.vjp(_small_f, z_ref[...], pv_ref[...], av_ref[...])
        dz, dpv, dav = vjp(de)
        dz_ref[...] = dz.astype(dz_ref.dtype)
        dpv_ref[...] += dpv
        dav_ref[...] += dav

    rev = lambda i: (nt - 1 - i, 0)
    small_cols = pl.BlockSpec((ts, 128), lambda i: (nt - 1 - i, CB_SMALL))
    return pl.pallas_call(
        body, name="smalls_bwd", grid=(nt,),
        in_specs=[small_cols, pl.BlockSpec((1, 128), lambda i: (0, 0)), pl.BlockSpec((1, 128), lambda i: (0, 0)),
                  pl.BlockSpec((ts, 128), rev), pl.BlockSpec(memory_space=pl.ANY)],
        out_specs=[small_cols, pl.BlockSpec((1, 128), lambda i: (0, 0)), pl.BlockSpec((1, 128), lambda i: (0, 0))],
        out_shape=[jax.ShapeDtypeStruct(dp.shape, dp.dtype), jax.ShapeDtypeStruct((1, 128), F32),
                   jax.ShapeDtypeStruct((1, 128), F32)],
        scratch_shapes=[pltpu.VMEM((1, 128), F32)],
        input_output_aliases={4: 0},
    )(p_all, pv, av, dsm, dp)


def _causal_mask(s, keys_first):
    r = lax.broadcasted_iota(jnp.int32, s.shape, 0)
    c = lax.broadcasted_iota(jnp.int32, s.shape, 1)
    return jnp.where((r <= c) if keys_first else (c <= r), s, NEG)


def flash_fwd(qs, kn, vb, frow, S, tq):
    nq = S // tq
    tk = tq

    def body(q_ref, k_ref, v_ref, fr_ref, o_ref, lse_ref):
        i = pl.program_id(1)
        q = q_ref[...]
        f_base = fr_ref[0, i][:, 0:1]

        def tile(j, carry, diagonal):
            m, l, acc = carry
            off = pl.multiple_of(j * tk, tk)
            k = k_ref[pl.ds(off, tk), :]
            v = v_ref[pl.ds(off, tk), :]
            s = mm_nt(q, k) - (fr_ref[0, j] - f_base)
            if diagonal:
                s = _causal_mask(s, keys_first=False)
            m_new = jnp.maximum(m, jnp.max(s, axis=-1, keepdims=True))
            a = jnp.exp(m - m_new)
            p = jnp.exp(s - m_new)
            l = a * l + jnp.sum(p, axis=-1, keepdims=True)
            acc = a * acc + mm_nn(p, v)
            return m_new, l, acc

        init = (jnp.full((tq, 1), NEG, F32), jnp.zeros((tq, 1), F32), jnp.zeros((tq, D_HEAD), F32))
        carry = lax.fori_loop(0, i, lambda j, c: tile(j, c, False), init)
        m, l, acc = tile(i, carry, True)
        o_ref[...] = acc / l
        lse_ref[0] = m + jnp.log(l)

    return pl.pallas_call(
        body, name="flash_fwd", grid=(N_HEADS, nq),
        in_specs=[pl.BlockSpec((tq, D_HEAD), lambda h, i: (i, h)), pl.BlockSpec((S, D_HEAD), lambda h, i: (0, h)),
                  pl.BlockSpec((S, D_HEAD), lambda h, i: (0, h)),
                  pl.BlockSpec((1, nq, 1, tk), lambda h, i: (h, 0, 0, 0))],
        out_specs=[pl.BlockSpec((tq, D_HEAD), lambda h, i: (i, h)), pl.BlockSpec((1, tq, 1), lambda h, i: (h, i, 0))],
        out_shape=[jax.ShapeDtypeStruct((S, WIDTH), F32), jax.ShapeDtypeStruct((N_HEADS, S, 1), F32)],
    )(qs, kn, vb, frow)


def flash_bwd_dq(qs, kn, vb, do, frow, lse, delta, S, tq):
    nq = S // tq
    tk = tq

    def body(q_ref, k_ref, v_ref, do_ref, fr_ref, lse_ref, dl_ref, dq_ref, dfq_ref):
        i = pl.program_id(1)
        q = q_ref[...]
        do_ = do_ref[...]
        lse_ = lse_ref[0]
        dl = dl_ref[0]
        f_base = fr_ref[0, i][:, 0:1]

        def tile(j, carry, diagonal):
            dq, dfq = carry
            off = pl.multiple_of(j * tk, tk)
            k = k_ref[pl.ds(off, tk), :]
            v = v_ref[pl.ds(off, tk), :]
            s = mm_nt(q, k) - (fr_ref[0, j] - f_base)
            if diagonal:
                s = _causal_mask(s, keys_first=False)
            p = jnp.exp(s - lse_)
            ds = p * (mm_nt(do_, v) - dl)
            return dq + mm_nn(ds, k), dfq + jnp.sum(ds, axis=-1, keepdims=True)

        carry = lax.fori_loop(0, i, lambda j, c: tile(j, c, False),
                              (jnp.zeros((tq, D_HEAD), F32), jnp.zeros((tq, 1), F32)))
        dq, dfq = tile(i, carry, True)
        dq_ref[...] = dq * ATT_SCALE
        dfq_ref[0] = dfq

    col = pl.BlockSpec((1, tq, 1), lambda h, i: (h, i, 0))
    blk = pl.BlockSpec((tq, D_HEAD), lambda h, i: (i, h))
    full = pl.BlockSpec((S, D_HEAD), lambda h, i: (0, h))
    return pl.pallas_call(
        body, name="flash_bwd_dq", grid=(N_HEADS, nq),
        in_specs=[blk, full, full, blk, pl.BlockSpec((1, nq, 1, tk), lambda h, i: (h, 0, 0, 0)), col, col],
        out_specs=[blk, col],
        out_shape=[jax.ShapeDtypeStruct((S, WIDTH), F32), jax.ShapeDtypeStruct((N_HEADS, S, 1), F32)],
    )(qs, kn, vb, do, frow, lse, delta)


def flash_bwd_dkv(qs, kn, vb, do, fcol, frow, lse_row, delta_row, dp, S, tq):
    nq = S // tq
    tk = tq

    def body(q_ref, k_ref, v_ref, do_ref, fc_ref, fr_ref, lse_ref, dl_ref, dp_in, dk_ref, dv_ref, dfk_ref):
        j = pl.program_id(1)
        k = k_ref[...]
        v = v_ref[...]
        fk = fc_ref[0]

        def tile(i, carry, diagonal):
            dk, dv, dfk = carry
            off = pl.multiple_of(i * tq, tq)
            q = q_ref[pl.ds(off, tq), :]
            do_ = do_ref[pl.ds(off, tq), :]
            st = mm_nt(k, q) - (fk - fr_ref[0, i][:, 0:1])
            if diagonal:
                st = _causal_mask(st, keys_first=True)
            pt = jnp.exp(st - lse_ref[0, i])
            dst = pt * (mm_nt(v, do_) - dl_ref[0, i])
            return dk + mm_nn(dst, q), dv + mm_nn(pt, do_), dfk - jnp.sum(dst, axis=-1, keepdims=True)

        z = jnp.zeros((tk, D_HEAD), F32)
        carry = tile(j, (z, z, jnp.zeros((tk, 1), F32)), True)
        dk, dv, dfk = lax.fori_loop(j + 1, nq, lambda i, c: tile(i, c, False), carry)
        dk_ref[...] = dk
        dv_ref[...] = dv.astype(dv_ref.dtype)
        dfk_ref[0] = dfk

    col = pl.BlockSpec((1, tk, 1), lambda h, j: (h, j, 0))
    blk = pl.BlockSpec((tk, D_HEAD), lambda h, j: (j, h))
    full = pl.BlockSpec((S, D_HEAD), lambda h, j: (0, h))
    rows = pl.BlockSpec((1, nq, 1, tq), lambda h, j: (h, 0, 0, 0))
    return pl.pallas_call(
        body, name="flash_bwd_dkv", grid=(N_HEADS, nq),
        in_specs=[full, blk, blk, full, col, rows, rows, rows, pl.BlockSpec(memory_space=pl.ANY)],
        out_specs=[blk, pl.BlockSpec((tk, D_HEAD), lambda h, j: (j, CB_FV + h)), col],
        out_shape=[jax.ShapeDtypeStruct((S, WIDTH), F32), jax.ShapeDtypeStruct(dp.shape, dp.dtype),
                   jax.ShapeDtypeStruct((N_HEADS, S, 1), F32)],
        input_output_aliases={8: 1},
    )(qs, kn, vb, do, fcol, frow, lse_row, delta_row, dp)


CONV_COLS = 3 * WIDTH
CONV_TC = 512


def conv_fwd(p_all, conv_w, S, ts):
    nt, tc = S // ts, CONV_TC
    cb0 = CB_GQ * 128 // tc
    r8 = ts // 8

    def body(x_ref, halo_ref, w_ref, y_ref, scr):
        i = pl.program_id(1)
        scr[0:8, :] = jnp.where(i > 0, halo_ref[...], 0.0)
        scr[8:8 + ts, :] = x_ref[...]
        w = w_ref[...]
        y = w[3:4, :] * x_ref[...]
        for j in range(CONV_K - 1):
            y = y + w[j:j + 1, :] * scr[5 + j:5 + j + ts, :]
        y_ref[...] = y

    return pl.pallas_call(
        body, name="conv_fwd", grid=(CONV_COLS // tc, nt),
        in_specs=[pl.BlockSpec((ts, tc), lambda c, i: (i, cb0 + c)),
                  pl.BlockSpec((8, tc), lambda c, i: (jnp.maximum(i * r8 - 1, 0), cb0 + c)),
                  pl.BlockSpec((CONV_K, tc), lambda c, i: (0, c))],
        out_specs=pl.BlockSpec((ts, tc), lambda c, i: (i, c)),
        out_shape=jax.ShapeDtypeStruct((S, CONV_COLS), F32),
        scratch_shapes=[pltpu.VMEM((ts + 8, tc), F32)],
    )(p_all, p_all, conv_w)


def conv_bwd(p_all, dy, conv_w, dp, S, ts):
    nt, tc = S // ts, CONV_TC
    cb0 = CB_GQ * 128 // tc
    r8 = ts // 8

    def body(x_ref, xh_ref, dy_ref, dyh_ref, w_ref, dp_in, dx_ref, dw_ref, xs, ds):
        i = pl.program_id(1)
        xs[0:8, :] = jnp.where(i > 0, xh_ref[...], 0.0)
        xs[8:8 + ts, :] = x_ref[...]
        ds[0:ts, :] = dy_ref[...]
        ds[ts:ts + 8, :] = jnp.where(i < nt - 1, dyh_ref[...], 0.0)
        w = w_ref[...]
        d = dy_ref[...]
        dx = w[3:4, :] * d
        for j in range(CONV_K - 1):
            dx = dx + w[j:j + 1, :] * ds[3 - j:3 - j + ts, :]
        dx_ref[...] = dx.astype(dx_ref.dtype)

        @pl.when(i == 0)
        def _():
            dw_ref[...] = jnp.zeros_like(dw_ref)

        rows = [jnp.sum(d * xs[5 + j:5 + j + ts, :], axis=0, keepdims=True) for j in range(CONV_K - 1)]
        rows.append(jnp.sum(d * x_ref[...], axis=0, keepdims=True))
        dw_ref[...] += jnp.concatenate(rows, axis=0)

    return pl.pallas_call(
        body, name="conv_bwd", grid=(CONV_COLS // tc, nt),
        in_specs=[pl.BlockSpec((ts, tc), lambda c, i: (i, cb0 + c)),
                  pl.BlockSpec((8, tc), lambda c, i: (jnp.maximum(i * r8 - 1, 0), cb0 + c)),
                  pl.BlockSpec((ts, tc), lambda c, i: (i, c)),
                  pl.BlockSpec((8, tc), lambda c, i: (jnp.minimum((i + 1) * r8, nt * r8 - 1), c)),
                  pl.BlockSpec((CONV_K, tc), lambda c, i: (0, c)), pl.BlockSpec(memory_space=pl.ANY)],
        out_specs=[pl.BlockSpec((ts, tc), lambda c, i: (i, cb0 + c)), pl.BlockSpec((CONV_K, tc), lambda c, i: (0, c))],
        out_shape=[jax.ShapeDtypeStruct(dp.shape, dp.dtype), jax.ShapeDtypeStruct((CONV_K, CONV_COLS), F32)],
        scratch_shapes=[pltpu.VMEM((ts + 8, tc), F32), pltpu.VMEM((ts + 8, tc), F32)],
        input_output_aliases={5: 0},
    )(p_all, p_all, dy, dy, conv_w, dp)


def _bdot(a, b, ca, cb):
    return lax.dot_general(a.astype(MXU_DTYPE), b.astype(MXU_DTYPE), (((ca,), (cb,)), ((0,), (0,))),
                           preferred_element_type=F32)


@jax.custom_vjp
def bmm_nn(a, b):
    return _bdot(a, b, 2, 1)


@jax.custom_vjp
def bmm_nt(a, b):
    return _bdot(a, b, 2, 2)


@jax.custom_vjp
def bmm_tn(a, b):
    return _bdot(a, b, 1, 1)


bmm_nn.defvjp(lambda a, b: (bmm_nn(a, b), (a, b)), lambda r, g: (bmm_nt(g, r[1]), bmm_tn(r[0], g)))
bmm_nt.defvjp(lambda a, b: (bmm_nt(a, b), (a, b)), lambda r, g: (bmm_nn(g, r[1]), bmm_tn(g, r[0])))
bmm_tn.defvjp(lambda a, b: (bmm_tn(a, b), (a, b)), lambda r, g: (bmm_nt(r[1], g), bmm_nn(r[0], g)))


def bdot_hi(a, b, ca=2, cb=1):
    return lax.dot_general(a, b, (((ca,), (cb,)), ((0,), (0,))), precision=SOLVE_PRECISION,
                           preferred_element_type=F32)


def _tri_inv_product(m):
    ii = lax.broadcasted_iota(jnp.int32, m.shape, 1)
    jj = lax.broadcasted_iota(jnp.int32, m.shape, 2)
    t = (ii == jj).astype(F32) - m
    pw = bdot_hi(m, m)
    for _ in range(4):
        t = t + bdot_hi(t, pw)
        pw = bdot_hi(pw, pw)
    return t + bdot_hi(t, pw)


def _tri_inv_bwd(t, g):
    return -bdot_hi(bdot_hi(t, g, 1, 1), t, 2, 2)


@jax.custom_vjp
def tri_inv(m):
    return _tri_inv_product(m)


@jax.custom_vjp
def tri_inv_known(m, t):
    return t


tri_inv.defvjp(lambda m: (lambda t: (t, t))(_tri_inv_product(m)), lambda t, g: (_tri_inv_bwd(t, g),))
tri_inv_known.defvjp(lambda m, t: (t, t), lambda t, g: (_tri_inv_bwd(t, g), jnp.zeros_like(t)))


def chunk_fn(q, k, v, gc_col, gc_row, beta, state, t_known=None):
    shape = (N_HEADS, CHUNK, CHUNK)
    ii = lax.broadcasted_iota(jnp.int32, shape, 1)
    jj = lax.broadcasted_iota(jnp.int32, shape, 2)
    lower = ii >= jj
    strict = ii > jj
    decay = jnp.exp(jnp.where(lower, gc_col - gc_row, NEG))
    kb = k * beta
    vb = v * beta
    m = jnp.where(strict, bmm_nt(kb, k) * decay, 0.0)
    t = tri_inv(m) if t_known is None else tri_inv_known(m, t_known)
    u = bdot_hi(t, vb)
    w = bdot_hi(t, kb * jnp.exp(gc_col))
    attn = jnp.where(lower, bmm_nt(q, k) * decay, 0.0)
    v_new = u - bmm_nn(w, state)
    o = bmm_nn(q * jnp.exp(gc_col), state) + bmm_nn(attn, v_new)
    g_last = gc_col[:, CHUNK - 1:CHUNK, :]
    k_dec = k * jnp.exp(g_last - gc_col)
    new_state = state * jnp.exp(g_last) + bmm_tn(k_dec, v_new)
    return (o, new_state, t) if t_known is None else (o, new_state)


def _heads(ref):
    return jnp.stack([ref[:, h * D_HEAD:(h + 1) * D_HEAD] for h in range(N_HEADS)], axis=0)


def _head_cols(ref, lane0):
    return jnp.stack([ref[:, lane0 + h:lane0 + h + 1] for h in range(N_HEADS)], axis=0)


def _qkv_head_cols(h):
    return [slice(g * WIDTH + h * D_HEAD, g * WIDTH + (h + 1) * D_HEAD) for g in range(3)]


def gdn_pre_fwd(y, S, ts):
    def body(y_ref, o_ref):
        for h in range(N_HEADS):
            cols = _qkv_head_cols(h)
            for c, o in zip(cols, f_gdnpre(*[y_ref[:, c] for c in cols])):
                o_ref[:, c] = o

    spec = pl.BlockSpec((ts, CONV_COLS), lambda i: (i, 0))
    return pl.pallas_call(body, name="gdn_pre", grid=(S // ts,), in_specs=[spec], out_specs=spec,
                          out_shape=jax.ShapeDtypeStruct((S, CONV_COLS), F32))(y)


def gdn_pre_bwd(y, dqkv, S, ts):
    def body(y_ref, d_ref, o_ref):
        for h in range(N_HEADS):
            cols = _qkv_head_cols(h)
            _, vjp = jax.vjp(f_gdnpre, *[y_ref[:, c] for c in cols])
            for c, g in zip(cols, vjp(tuple(d_ref[:, c] for c in cols))):
                o_ref[:, c] = g

    spec = pl.BlockSpec((ts, CONV_COLS), lambda i: (i, 0))
    return pl.pallas_call(body, name="gdn_pre_bwd", grid=(S // ts,), in_specs=[spec, spec], out_specs=spec,
                          out_shape=jax.ShapeDtypeStruct((S, CONV_COLS), F32))(y, dqkv)


def gdn_fwd(qkv, smalls, gc_row, S):
    n = S // CHUNK

    def body(q_ref, k_ref, v_ref, sm_ref, gr_ref, o_ref, st_ref, t_ref, state):
        @pl.when(pl.program_id(0) == 0)
        def _():
            state[...] = jnp.zeros_like(state)

        s0 = state[...]
        st_ref[0] = s0
        o, s1, t = chunk_fn(_heads(q_ref), _heads(k_ref), _heads(v_ref), _head_cols(sm_ref, 8),
                            gr_ref[0][:, None, :], _head_cols(sm_ref, 16), s0)
        for h in range(N_HEADS):
            o_ref[:, h * D_HEAD:(h + 1) * D_HEAD] = o[h]
        state[...] = s1
        t_ref[0] = t

    return pl.pallas_call(
        body, name="gdn_chunk_fwd", grid=(n,),
        in_specs=[pl.BlockSpec((CHUNK, WIDTH), lambda i, g=g: (i, g)) for g in range(3)]
        + [pl.BlockSpec((CHUNK, 128), lambda i: (i, 0)), pl.BlockSpec((1, N_HEADS, CHUNK), lambda i: (i, 0, 0))],
        out_specs=[pl.BlockSpec((CHUNK, WIDTH), lambda i: (i, 0)),
                   pl.BlockSpec((1, N_HEADS, D_HEAD, D_HEAD), lambda i: (i, 0, 0, 0)),
                   pl.BlockSpec((1, N_HEADS, CHUNK, CHUNK), lambda i: (i, 0, 0, 0))],
        out_shape=[jax.ShapeDtypeStruct((S, WIDTH), F32), jax.ShapeDtypeStruct((n, N_HEADS, D_HEAD, D_HEAD), F32),
                   jax.ShapeDtypeStruct((n, N_HEADS, CHUNK, CHUNK), F32)],
        scratch_shapes=[pltpu.VMEM((N_HEADS, D_HEAD, D_HEAD), F32)],
    )(qkv, qkv, qkv, smalls, gc_row)


def gdn_bwd(qkv, smalls, gc_row, states, tinv, do, S):
    n = S // CHUNK

    def body(q_ref, k_ref, v_ref, sm_ref, gr_ref, st_ref, t_ref, do_ref, dqkv_ref, dsm_ref, dgr_ref, dstate):
        @pl.when(pl.program_id(0) == 0)
        def _():
            dstate[...] = jnp.zeros_like(dstate)

        t_saved = t_ref[0]
        _, vjp = jax.vjp(lambda *a: chunk_fn(*a, t_known=t_saved), _heads(q_ref), _heads(k_ref), _heads(v_ref),
                         _head_cols(sm_ref, 8), gr_ref[0][:, None, :], _head_cols(sm_ref, 16), st_ref[0])
        dq, dk, dv, dgc, dgr, dbt, ds = vjp((_heads(do_ref), dstate[...]))
        for h in range(N_HEADS):
            dqkv_ref[:, h * D_HEAD:(h + 1) * D_HEAD] = dq[h]
            dqkv_ref[:, WIDTH + h * D_HEAD:WIDTH + (h + 1) * D_HEAD] = dk[h]
            dqkv_ref[:, 2 * WIDTH + h * D_HEAD:2 * WIDTH + (h + 1) * D_HEAD] = dv[h]
        dstate[...] = ds
        cols = [dgc[h] for h in range(N_HEADS)] + [dbt[h] for h in range(N_HEADS)]
        dsm_ref[...] = jnp.concatenate([jnp.zeros((CHUNK, 8), F32)] + cols + [jnp.zeros((CHUNK, 128 - 24), F32)],
                                       axis=1)
        dgr_ref[0] = jnp.concatenate([dgr[h] for h in range(N_HEADS)], axis=0)

    rev = lambda c: (lambda i: (n - 1 - i, c))
    return pl.pallas_call(
        body, name="gdn_chunk_bwd", grid=(n,),
        in_specs=[pl.BlockSpec((CHUNK, WIDTH), rev(g)) for g in range(3)]
        + [pl.BlockSpec((CHUNK, 128), rev(0)), pl.BlockSpec((1, N_HEADS, CHUNK), lambda i: (n - 1 - i, 0, 0)),
           pl.BlockSpec((1, N_HEADS, D_HEAD, D_HEAD), lambda i: (n - 1 - i, 0, 0, 0)),
           pl.BlockSpec((1, N_HEADS, CHUNK, CHUNK), lambda i: (n - 1 - i, 0, 0, 0)),
           pl.BlockSpec((CHUNK, WIDTH), rev(0))],
        out_specs=[pl.BlockSpec((CHUNK, 3 * WIDTH), rev(0)), pl.BlockSpec((CHUNK, 128), rev(0)),
                   pl.BlockSpec((1, N_HEADS, CHUNK), lambda i: (n - 1 - i, 0, 0))],
        out_shape=[jax.ShapeDtypeStruct((S, 3 * WIDTH), F32)] + [jax.ShapeDtypeStruct((S, 128), F32),
                                                                  jax.ShapeDtypeStruct((n, N_HEADS, CHUNK), F32)],
        scratch_shapes=[pltpu.VMEM((N_HEADS, D_HEAD, D_HEAD), F32)],
    )(qkv, qkv, qkv, smalls, gc_row, states, tinv, do)


def _flip(a, bit):
    return 1 - a if bit else a


def allgather8(name, buf):
    R, W = buf.shape

    def body(x_ref, out_ref, send_sems, recv_sems, local_sem):
        x, y, c = lax.axis_index("x"), lax.axis_index("y"), lax.axis_index("c")

        def slot(px, py, pc):
            return out_ref.at[4 * px + 2 * py + pc]

        mine = pltpu.make_async_copy(x_ref, slot(x, y, c), local_sem)
        mine.start()
        sends = []
        for r in range(1, N_DEV):
            peer = (_flip(x, r & 4), _flip(y, r & 2), _flip(c, r & 1))
            cp = pltpu.make_async_remote_copy(src_ref=x_ref, dst_ref=slot(x, y, c), send_sem=send_sems.at[r - 1],
                                              recv_sem=recv_sems.at[r - 1], device_id=peer, device_id_type=MESH_ID)
            cp.start()
            sends.append(cp)
        for r in range(1, N_DEV):
            peer = (_flip(x, r & 4), _flip(y, r & 2), _flip(c, r & 1))
            pltpu.make_async_remote_copy(src_ref=x_ref, dst_ref=slot(*peer), send_sem=send_sems.at[r - 1],
                                         recv_sem=recv_sems.at[r - 1], device_id=peer,
                                         device_id_type=MESH_ID).wait_recv()
        for cp in sends:
            cp.wait_send()
        mine.wait()

    return pl.pallas_call(
        body, name=name,
        out_shape=jax.ShapeDtypeStruct((N_DEV, R, W), buf.dtype),
        in_specs=[pl.BlockSpec(memory_space=pltpu.VMEM)],
        out_specs=pl.BlockSpec(memory_space=pltpu.VMEM),
        scratch_shapes=[pltpu.SemaphoreType.DMA((N_DEV - 1,)), pltpu.SemaphoreType.DMA((N_DEV - 1,)),
                        pltpu.SemaphoreType.DMA],
    )(buf)


def allgather_chips(name, blk):
    def body(x_ref, out_ref, send_sems, recv_sems, local_sem):
        x, y, c = lax.axis_index("x"), lax.axis_index("y"), lax.axis_index("c")
        mine = pltpu.make_async_copy(x_ref, out_ref.at[2 * x + y], local_sem)
        mine.start()
        sends = []
        for r in range(1, N_CHIPS):
            peer = (_flip(x, r & 2), _flip(y, r & 1), c)
            cp = pltpu.make_async_remote_copy(src_ref=x_ref, dst_ref=out_ref.at[2 * x + y], send_sem=send_sems.at[r - 1],
                                              recv_sem=recv_sems.at[r - 1], device_id=peer, device_id_type=MESH_ID)
            cp.start()
            sends.append(cp)
        for r in range(1, N_CHIPS):
            peer = (_flip(x, r & 2), _flip(y, r & 1), c)
            pltpu.make_async_remote_copy(src_ref=x_ref, dst_ref=out_ref.at[2 * peer[0] + peer[1]],
                                         send_sem=send_sems.at[r - 1], recv_sem=recv_sems.at[r - 1], device_id=peer,
                                         device_id_type=MESH_ID).wait_recv()
        for cp in sends:
            cp.wait_send()
        mine.wait()

    return pl.pallas_call(
        body, name=name,
        out_shape=jax.ShapeDtypeStruct((N_CHIPS,) + blk.shape, blk.dtype),
        in_specs=[pl.BlockSpec(memory_space=pl.ANY)],
        out_specs=pl.BlockSpec(memory_space=pl.ANY),
        scratch_shapes=[pltpu.SemaphoreType.DMA((N_CHIPS - 1,)), pltpu.SemaphoreType.DMA((N_CHIPS - 1,)),
                        pltpu.SemaphoreType.DMA],
    )(blk)


def scatter_chips(name, parts):
    def body(x_ref, out_ref, send_sems, recv_sems):
        x, y, c = lax.axis_index("x"), lax.axis_index("y"), lax.axis_index("c")
        sends = []
        for r in range(1, N_CHIPS):
            peer = (_flip(x, r & 2), _flip(y, r & 1), c)
            cp = pltpu.make_async_remote_copy(src_ref=x_ref.at[2 * peer[0] + peer[1]], dst_ref=out_ref.at[r - 1],
                                              send_sem=send_sems.at[r - 1], recv_sem=recv_sems.at[r - 1],
                                              device_id=peer, device_id_type=MESH_ID)
            cp.start()
            sends.append(cp)
        for cp in sends:
            cp.wait_recv()
        for cp in sends:
            cp.wait_send()

    return pl.pallas_call(
        body, name=name,
        out_shape=jax.ShapeDtypeStruct((N_CHIPS - 1,) + parts.shape[1:], parts.dtype),
        in_specs=[pl.BlockSpec(memory_space=pl.ANY)],
        out_specs=pl.BlockSpec(memory_space=pl.ANY),
        scratch_shapes=[pltpu.SemaphoreType.DMA((N_CHIPS - 1,)), pltpu.SemaphoreType.DMA((N_CHIPS - 1,))],
    )(parts)


def sibling_swap(name, buf):
    def body(x_ref, out_ref, send_sem, recv_sem):
        x, y, c = lax.axis_index("x"), lax.axis_index("y"), lax.axis_index("c")
        cp = pltpu.make_async_remote_copy(src_ref=x_ref, dst_ref=out_ref, send_sem=send_sem, recv_sem=recv_sem,
                                          device_id=(x, y, 1 - c), device_id_type=MESH_ID)
        cp.start()
        cp.wait_recv()
        cp.wait_send()

    return pl.pallas_call(
        body, name=name,
        out_shape=jax.ShapeDtypeStruct(buf.shape, buf.dtype),
        in_specs=[pl.BlockSpec(memory_space=pl.ANY)],
        out_specs=pl.BlockSpec(memory_space=pl.ANY),
        scratch_shapes=[pltpu.SemaphoreType.DMA, pltpu.SemaphoreType.DMA],
    )(buf)


def sum_parts(name, own, recv, tr):
    R, W = own.shape
    tr = _tile(R, tr)

    def body(o_ref, r_ref, out_ref):
        acc = o_ref[...]
        for k in range(recv.shape[0]):
            acc = acc + r_ref[k].astype(F32)
        out_ref[...] = acc

    return pl.pallas_call(
        body, name=name, grid=(R // tr,),
        in_specs=[pl.BlockSpec((tr, W), lambda i: (i, 0)), pl.BlockSpec((recv.shape[0], tr, W), lambda i: (0, i, 0))],
        out_specs=pl.BlockSpec((tr, W), lambda i: (i, 0)),
        out_shape=jax.ShapeDtypeStruct((R, W), F32),
    )(own, recv)


def sum_devices(name, g):
    def body(g_ref, o_ref):
        acc = g_ref[0]
        for d in range(1, N_DEV):
            acc = acc + g_ref[d]
        o_ref[...] = acc

    return pl.pallas_call(body, name=name, out_shape=jax.ShapeDtypeStruct(g.shape[1:], F32))(g)


def ada_fwd(c_all, w_ada):
    K, N = w_ada.shape
    tn = _tile(N, 512)

    def body(c_ref, w_ref, o_ref):
        o_ref[...] = mm_nn(_silu(c_ref[...]), w_ref[...])

    return pl.pallas_call(
        body, name="ada_fwd", grid=(N // tn,),
        in_specs=[pl.BlockSpec((N_DEV, K), lambda j: (0, 0)), pl.BlockSpec((K, tn), lambda j: (0, j))],
        out_specs=pl.BlockSpec((N_DEV, tn), lambda j: (0, j)),
        out_shape=jax.ShapeDtypeStruct((N_DEV, N), F32),
    )(c_all, w_ada)


def ada_grad(c_all, dmod):
    K = c_all.shape[1]
    N = dmod.shape[1]
    tn = _tile(N, 512)

    def body(c_ref, d_ref, o_ref):
        o_ref[...] = mm_tn(_silu(c_ref[...]), d_ref[...])

    return pl.pallas_call(
        body, name="ada_grad", grid=(N // tn,),
        in_specs=[pl.BlockSpec((N_DEV, K), lambda j: (0, 0)), pl.BlockSpec((N_DEV, tn), lambda j: (0, j))],
        out_specs=pl.BlockSpec((K, tn), lambda j: (0, j)),
        out_shape=jax.ShapeDtypeStruct((K, N), F32),
    )(c_all, dmod)


def adamw(name, w, gparts, m, v, tr):
    R, W = w.shape
    tr = _tile(R, tr)
    ng = len(gparts)

    def body(w_ref, *refs):
        g_refs, (m_ref, v_ref, g_out, d_out, m_out, v_out) = refs[:ng], refs[ng:]
        g = g_refs[0][...]
        for r in g_refs[1:]:
            g = g + r[...]
        m1 = ADAM_B1 * m_ref[...] + (1.0 - ADAM_B1) * g
        v1 = ADAM_B2 * v_ref[...] + (1.0 - ADAM_B2) * jnp.square(g)
        m_hat = m1 / (1.0 - ADAM_B1 ** ADAM_STEP)
        v_hat = v1 / (1.0 - ADAM_B2 ** ADAM_STEP)
        g_out[...] = g
        d_out[...] = -ADAM_LR * (m_hat / (jnp.sqrt(v_hat) + ADAM_EPS) + ADAM_WD * w_ref[...])
        m_out[...] = m1
        v_out[...] = v1

    spec = pl.BlockSpec((tr, W), lambda i: (i, 0))
    return pl.pallas_call(
        body, name=name, grid=(R // tr,),
        in_specs=[spec] * (3 + ng), out_specs=[spec] * 4,
        out_shape=[jax.ShapeDtypeStruct((R, W), F32)] * 4,
    )(w, *gparts, m, v)


_REF_SEGMENTS = ((0, 4 * WIDTH, 0), (4 * WIDTH, 4 * WIDTH + 8, 4 * WIDTH), (4 * WIDTH + 8, 8 * WIDTH + 8, -8),
                 (8 * WIDTH + 8, IN_WIDTH, 0))


def _internal_from_blocks(blocks, n):
    a = 4 * WIDTH
    pieces = []
    for lo, hi in ((0, a), (a + 8, 2 * a + 8), (a, a + 8), (2 * a + 8, IN_WIDTH)):
        for j in range(N_CHIPS):
            s, e = max(lo, j * n), min(hi, (j + 1) * n)
            if s < e:
                pieces.append(blocks[j][:, s - j * n:e - j * n])
    pieces.append(jnp.zeros(blocks.shape[1:2] + (128 - N_SMALL,), blocks.dtype))
    return jnp.concatenate(pieces, axis=1)


def _block_from_internal(g, j, n):
    pieces = []
    for lo, hi, shift in _REF_SEGMENTS:
        s, e = max(lo, j * n), min(hi, (j + 1) * n)
        if s < e:
            pieces.append(g[:, s + shift:e + shift])
    return jnp.concatenate(pieces, axis=1)


def _pad_lanes(v, n=128):
    return jnp.pad(v, ((0, 0), (0, n - v.shape[1])))


def kernel(x, c, norm_g, w_ada, b_ada, w_in, b_fgate, fox_qn_g, fox_kn_g, gdn_conv_w, gdn_A_log, gdn_dt_bias, gdn_norm_g, w_out, final_g, loss_target, m_norm_g, m_w_ada, m_b_ada, m_w_in, m_b_fgate, m_fox_qn_g, m_fox_kn_g, m_gdn_conv_w, m_gdn_A_log, m_gdn_dt_bias, m_gdn_norm_g, m_w_out, m_final_g, v_norm_g, v_w_ada, v_b_ada, v_w_in, v_b_fgate, v_fox_qn_g, v_fox_kn_g, v_gdn_conv_w, v_gdn_A_log, v_gdn_dt_bias, v_gdn_norm_g, v_w_out, v_final_g):
    S = x.shape[1]
    H = N_HEADS
    ix, iy, ic = lax.axis_index("x"), lax.axis_index("y"), lax.axis_index("c")
    chip = 2 * ix + iy
    me = 2 * chip + ic
    x2, tgt = x[0], loss_target[0]
    ts_wide = _tile(S, 256)
    ts = _tile(S, 512)
    tq = _tile(S, 512)
    nq = S // tq
    nch = S // CHUNK
    conv_cols_chip = CONV_COLS // N_CHIPS

    pay = jnp.concatenate([c, _pad_lanes(gdn_conv_w[0], D_MODEL), jnp.zeros((3, D_MODEL), F32)], axis=0)
    g1 = allgather8("ag_c", pay)
    c_all = g1[:, 0, :]
    conv_w = jnp.concatenate([g1[2 * j, 1:1 + CONV_K, :conv_cols_chip] for j in range(N_CHIPS)], axis=1)
    g2 = allgather8("ag_mod", ada_fwd(c_all, w_ada[0]))
    mod_rows = jnp.concatenate([g2[2 * j] for j in range(N_CHIPS)], axis=1)
    mod = lax.dynamic_slice_in_dim(mod_rows, me, 1, axis=0) + b_ada
    shift, scale, gate = mod[:, :D_MODEL], mod[:, D_MODEL:2 * D_MODEL], mod[:, 2 * D_MODEL:]

    n_in = w_in.shape[2]
    win4 = allgather_chips("ag_w_in", w_in[0].astype(MXU_DTYPE))
    w_all = _internal_from_blocks(win4, n_in)
    wout4 = allgather_chips("ag_w_out", w_out[0].astype(MXU_DTYPE))
    w_out_full = wout4.reshape(2 * WIDTH, D_MODEL)

    (h_b,) = rowwise_fwd("prologue", f_prologue, S, ts_wide, 1, [(x2, "row", D_MODEL, 0)], [norm_g, scale, shift],
                         [("row", D_MODEL, 0, D_MODEL, MXU_DTYPE)])
    p_all = matmul("in_proj", h_b, w_all, F32, tm=512, tn=640, tk=2048)
    pv = jnp.concatenate([b_fgate, gdn_dt_bias, jnp.zeros((1, 128 - 16), F32)], axis=1)
    av = jnp.concatenate([jnp.zeros((1, 8), F32), gdn_A_log, jnp.zeros((1, 128 - 16), F32)], axis=1)
    smalls = smalls_fwd(p_all, pv, av, S, ts)
    f_t = jnp.transpose(smalls[:, :H])
    fcol = f_t[:, :, None]
    frow = f_t.reshape(H, nq, 1, tq)
    gc_row = jnp.transpose(smalls[:, 8:16].reshape(nch, CHUNK, H), (0, 2, 1))
    head_b = ("head", D_HEAD, 0, WIDTH, MXU_DTYPE)
    head_f = ("head", D_HEAD, 0, WIDTH, F32)
    pcol = lambda cb: (p_all, "head", D_HEAD, cb)
    qn, kn, vb = rowwise_fwd("fox_pre", f_foxpre, S, ts, H, [pcol(CB_FQ), pcol(CB_FK), pcol(CB_FV)],
                             [fox_qn_g, fox_kn_g], [head_b] * 3)
    o_fox, lse = flash_fwd(qn, kn, vb, frow, S, tq)
    y_conv = conv_fwd(p_all, conv_w, S, ts)
    qkv = gdn_pre_fwd(y_conv, S, ts_wide)
    o_gdn, states, tinv = gdn_fwd(qkv, smalls, gc_row, S)
    (mix_f,) = rowwise_fwd("post_fox", f_postf, S, ts, H, [(o_fox, "head", D_HEAD, 0), pcol(CB_FZ)], [], [head_b])
    (mix_g,) = rowwise_fwd("post_gdn", f_postg, S, ts, H, [(o_gdn, "head", D_HEAD, 0), pcol(CB_GZ)], [gdn_norm_g],
                           [head_b])
    mixed = jnp.concatenate([mix_f, mix_g], axis=1)
    mo = matmul("out_proj", mixed, w_out_full, F32)

    wide = lambda a: (a, "row", D_MODEL, 0)
    dx_res, dmo, d_gate, d_final_g, loss_acc = rowwise_bwd(
        "loss", f_loss, S, ts_wide, 1, [wide(x2), wide(mo), wide(tgt)], [gate, final_g[None, :]], [],
        [(0, "row", D_MODEL, 0, D_MODEL, F32), (1, "row", D_MODEL, 0, D_MODEL, MXU_DTYPE)], primal_sum=True)

    d_mixed = matmul("d_mixed", dmo, w_out_full, F32, trans_b=True)
    g_w_out = matmul("g_w_out", jnp.transpose(mixed), dmo, F32)
    dp = lax.empty((S, P_COLS), MXU_DTYPE)
    into_dp = lambda idx, cb: (idx, "head", D_HEAD, cb, P_COLS, MXU_DTYPE, dp)
    do_fox, dp, delta = rowwise_bwd(
        "post_fox_bwd", f_postf, S, ts, H, [(o_fox, "head", D_HEAD, 0), pcol(CB_FZ)], [],
        [(d_mixed, "head", D_HEAD, 0)], [(0,) + head_b, into_dp(1, CB_FZ)],
        extra=lambda vals, cv, grads: (jnp.sum(grads[0] * vals[0], axis=-1, keepdims=True),),
        extra_outs=[("head3", 1, 0, None, F32)])
    do_gdn, dp, d_gdn_norm_g = rowwise_bwd(
        "post_gdn_bwd", f_postg, S, ts, H, [(o_gdn, "head", D_HEAD, 0), pcol(CB_GZ)], [gdn_norm_g],
        [(d_mixed, "head", D_HEAD, 8)], [(0,) + head_f, into_dp(1, CB_GZ)])
    d_qn, dfq_col = flash_bwd_dq(qn, kn, vb, do_fox, frow, lse, delta, S, tq)
    d_kn, dp, dfk_col = flash_bwd_dkv(qn, kn, vb, do_fox, fcol, frow, lse.reshape(H, nq, 1, tq),
                                      delta.reshape(H, nq, 1, tq), dp, S, tq)
    dp, d_qn_g = rowwise_bwd("fox_q_bwd", f_rms1, S, ts, H, [pcol(CB_FQ)], [fox_qn_g], [(d_qn, "head", D_HEAD, 0)],
                             [into_dp(0, CB_FQ)])
    dp, d_kn_g = rowwise_bwd("fox_k_bwd", f_rms1, S, ts, H, [pcol(CB_FK)], [fox_kn_g], [(d_kn, "head", D_HEAD, 0)],
                             [into_dp(0, CB_FK)])
    d_qkv, dsm_chunk, dgc_row = gdn_bwd(qkv, smalls, gc_row, states, tinv, do_gdn, S)
    dp, d_conv_w = conv_bwd(p_all, gdn_pre_bwd(y_conv, d_qkv, S, ts_wide), conv_w, dp, S, ts)
    d_f = jnp.transpose((dfq_col + dfk_col)[:, :, 0])
    d_gc = jnp.transpose(dgc_row, (0, 2, 1)).reshape(S, H)
    dsm = dsm_chunk + jnp.concatenate([d_f, d_gc, jnp.zeros((S, 128 - 16), F32)], axis=1)
    dp, d_pv, d_av = smalls_bwd(p_all, pv, av, dsm, dp, S, ts)
    d_h = matmul("d_h", dp, w_all, F32, tm=512, tn=1024, tk=1664, trans_b=True)
    g_w_all = matmul("g_w_in", jnp.transpose(h_b), dp, F32, tm=512, tn=640, tk=2048)
    d_x, d_norm_g, d_scale, d_shift = rowwise_bwd(
        "prologue_bwd", f_prologue_res, S, ts_wide, 1, [wide(x2)], [norm_g, scale, shift],
        [wide(d_h), wide(dx_res)], [(0, "row", D_MODEL, 0, D_MODEL, F32)])

    parts = [d_shift, d_scale, d_gate, d_norm_g, d_final_g, d_conv_w.reshape(1, CONV_K * CONV_COLS), d_qn_g, d_kn_g,
             d_gdn_norm_g, d_pv, d_av, loss_acc]
    n_pay = sum(p.shape[1] for p in parts)
    pay_w = -(-n_pay // 1024) * 1024
    pay2 = jnp.concatenate(parts + [jnp.zeros((1, pay_w - n_pay), F32)], axis=1).reshape(8, pay_w // 8)
    g3 = allgather8("ag_small", pay2)
    tot = sum_devices("sum_small", g3).reshape(1, pay_w)
    dmod_all = g3.reshape(N_DEV, pay_w)[:, :3 * D_MODEL]
    o = 3 * D_MODEL
    g_b_ada = tot[:, :o]
    g_norm_g, g_final_g = tot[:, o:o + D_MODEL], tot[:, o + D_MODEL:o + 2 * D_MODEL]
    o += 2 * D_MODEL
    g_conv_full = tot[:, o:o + CONV_K * CONV_COLS].reshape(CONV_K, CONV_COLS)
    g_conv = lax.dynamic_slice_in_dim(g_conv_full, chip * conv_cols_chip, conv_cols_chip, axis=1)
    o += CONV_K * CONV_COLS
    g_qn_g, g_kn_g, g_gdn_ng = tot[:, o:o + 128], tot[:, o + 128:o + 256], tot[:, o + 256:o + 384]
    g_pv, g_av = tot[:, o + 384:o + 512], tot[:, o + 512:o + 640]
    loss = tot[0, o + 640]
    g_b_fgate, g_dt_bias, g_a_log = g_pv[:, :8], g_pv[:, 8:16], g_av[:, 8:16]

    def small_vec(vals):
        flat = [norm for norm in vals[:3]] + [vals[3].reshape(1, CONV_K * conv_cols_chip)] + list(vals[4:7]) \
            + [_pad_lanes(s) for s in vals[7:]]
        n = sum(f.shape[1] for f in flat)
        nw = -(-n // 1024) * 1024
        return jnp.concatenate(flat + [jnp.zeros((1, nw - n), F32)], axis=1).reshape(8, nw // 8)

    sw = small_vec([norm_g, b_ada, final_g[None, :], gdn_conv_w[0], fox_qn_g, fox_kn_g, gdn_norm_g, b_fgate, gdn_A_log,
                    gdn_dt_bias])
    sm = small_vec([m_norm_g, m_b_ada, m_final_g[None, :], m_gdn_conv_w[0], m_fox_qn_g, m_fox_kn_g, m_gdn_norm_g,
                    m_b_fgate, m_gdn_A_log, m_gdn_dt_bias])
    sv = small_vec([v_norm_g, v_b_ada, v_final_g[None, :], v_gdn_conv_w[0], v_fox_qn_g, v_fox_kn_g, v_gdn_norm_g,
                    v_b_fgate, v_gdn_A_log, v_gdn_dt_bias])
    sg = small_vec([g_norm_g, g_b_ada, g_final_g, g_conv, g_qn_g, g_kn_g, g_gdn_ng, g_b_fgate, g_a_log, g_dt_bias])
    small_out = adamw("adamw_small", sw, [sg], sm, sv, 8)

    def split_small(a):
        a = a.reshape(1, -1)
        out, o = [], 0
        for n, shp in ((D_MODEL, (1, D_MODEL)), (3 * D_MODEL, (1, 3 * D_MODEL)), (D_MODEL, (D_MODEL,)),
                       (CONV_K * conv_cols_chip, (1, CONV_K, conv_cols_chip)), (128, (1, 128)), (128, (1, 128)),
                       (128, (1, 128))):
            out.append(a[:, o:o + n].reshape(shp))
            o += n
        for _ in range(3):
            out.append(a[:, o:o + 8])
            o += 128
        return out

    n_ada = w_ada.shape[2]
    g_w_ada = ada_grad(c_all, lax.dynamic_slice_in_dim(dmod_all, chip * n_ada, n_ada, axis=1))
    ada_out = adamw("adamw_w_ada", w_ada[0], [g_w_ada], m_w_ada[0], v_w_ada[0], 128)

    def reduce_scatter(tag, g4, tr):
        own = lax.dynamic_index_in_dim(g4, chip, axis=0, keepdims=False)
        recv = scatter_chips("rs_" + tag, g4.astype(MXU_DTYPE))
        part = sum_parts("sum_" + tag, own, recv, tr)
        return part, sibling_swap("swap_" + tag, part)

    g4_in = jnp.stack([_block_from_internal(g_w_all, j, n_in) for j in range(N_CHIPS)])
    in_out = adamw("adamw_w_in", w_in[0], list(reduce_scatter("w_in", g4_in, 128)), m_w_in[0], v_w_in[0], 128)
    g4_out = g_w_out.reshape(N_CHIPS, w_out.shape[1], D_MODEL)
    out_out = adamw("adamw_w_out", w_out[0], list(reduce_scatter("w_out", g4_out, 128)), m_w_out[0], v_w_out[0], 128)

    res = [loss, d_x[None]]
    for k in range(4):
        s = split_small(small_out[k])
        big = [ada_out[k][None], in_out[k][None], out_out[k][None]]
        res += [s[0], big[0], s[1], big[1], s[7], s[4], s[5], s[3], s[8], s[9], s[6], big[2], s[2]]
    return tuple(res)
```

```python
import functools

import jax
import jax.numpy as jnp
from jax import lax
from jax.experimental import pallas as pl
from jax.experimental.pallas import tpu as pltpu

F32 = jnp.float32
MXU_DTYPE = jnp.bfloat16
HIGHEST = lax.Precision.HIGHEST
SOLVE_PRECISION = lax.Precision.HIGH
MESH_ID = pl.DeviceIdType.MESH

D_MODEL = 2048
N_HEADS = 8
D_HEAD = 128
WIDTH = N_HEADS * D_HEAD
CHUNK = 64
CONV_K = 4
EPS = 1e-6
NEG = -1e30
ATT_SCALE = D_HEAD ** -0.5
N_SMALL = 3 * N_HEADS
P_COLS = 8 * WIDTH + 128
CB_FQ, CB_FK, CB_FV, CB_FZ, CB_GQ, CB_GK, CB_GV, CB_GZ, CB_SMALL = 0, 8, 16, 24, 32, 40, 48, 56, 64
IN_WIDTH = 8 * WIDTH + N_SMALL
N_CHIPS = 4
N_DEV = 8

ADAM_LR, ADAM_B1, ADAM_B2, ADAM_EPS, ADAM_WD, ADAM_STEP = 0.001, 0.9, 0.999, 1e-08, 0.01, 10


def _dotg(a, b, dims):
    return lax.dot_general(a.astype(MXU_DTYPE), b.astype(MXU_DTYPE), (dims, ((), ())), preferred_element_type=F32)


@jax.custom_vjp
def mm_nn(a, b):
    return _dotg(a, b, ((1,), (0,)))


@jax.custom_vjp
def mm_nt(a, b):
    return _dotg(a, b, ((1,), (1,)))


@jax.custom_vjp
def mm_tn(a, b):
    return _dotg(a, b, ((0,), (0,)))


mm_nn.defvjp(lambda a, b: (mm_nn(a, b), (a, b)), lambda r, g: (mm_nt(g, r[1]), mm_tn(r[0], g)))
mm_nt.defvjp(lambda a, b: (mm_nt(a, b), (a, b)), lambda r, g: (mm_nn(g, r[1]), mm_tn(g, r[0])))
mm_tn.defvjp(lambda a, b: (mm_tn(a, b), (a, b)), lambda r, g: (mm_nt(r[1], g), mm_nn(r[0], g)))


def hdot(a, b):
    return lax.dot_general(a, b, (((1,), (0,)), ((), ())), precision=HIGHEST, preferred_element_type=F32)


def _sigmoid(x):
    return 0.5 * (jnp.tanh(0.5 * x) + 1.0)


def _silu(x):
    return x * _sigmoid(x)


def _softplus(x):
    return jnp.maximum(x, 0.0) + jnp.log(1.0 + jnp.exp(-jnp.abs(x)))


def _log_sigmoid(x):
    return jnp.minimum(x, 0.0) - jnp.log(1.0 + jnp.exp(-jnp.abs(x)))


def _rms(x, g):
    return x * lax.rsqrt(jnp.mean(x * x, axis=-1, keepdims=True) + EPS) * g


def _l2n(x):
    return x * lax.rsqrt(jnp.sum(x * x, axis=-1, keepdims=True) + EPS)


def _tile(n, pref):
    t = min(n, pref)
    assert n % t == 0, (n, pref)
    return t


def _bspec(kind, w, col0, ts):
    if kind == "row":
        return pl.BlockSpec((ts, w), lambda i, h: (i, col0))
    if kind == "head":
        return pl.BlockSpec((ts, w), lambda i, h: (i, col0 + h))
    assert kind == "head3"
    return pl.BlockSpec((1, ts, w), lambda i, h: (h, i, 0))


def _ld(ref, kind):
    return ref[0] if kind == "head3" else ref[...]


def _st(ref, kind, val):
    if kind == "head3":
        ref[0] = val.astype(ref.dtype)
    else:
        ref[...] = val.astype(ref.dtype)


def _full_spec(a):
    nd = a.ndim
    return pl.BlockSpec(a.shape, lambda i, h: (0,) * nd)


def _out_struct(kind, w, S, width_total, dtype, nh):
    if kind == "head3":
        return jax.ShapeDtypeStruct((nh, S, w), dtype)
    return jax.ShapeDtypeStruct((S, width_total), dtype)


def rowwise_fwd(name, f, S, ts, nh, ins, params, outs):
    n_in, n_p = len(ins), len(params)

    def body(*refs):
        vals = [_ld(r, s[1]).astype(F32) for r, s in zip(refs[:n_in], ins)]
        pv = [r[...] for r in refs[n_in:n_in + n_p]]
        res = f(*vals, *pv)
        for r, s, o in zip(refs[n_in + n_p:], outs, res):
            _st(r, s[0], o)

    return pl.pallas_call(
        body, name=name, grid=(S // ts, nh),
        in_specs=[_bspec(k, w, c0, ts) for (_, k, w, c0) in ins] + [_full_spec(p) for p in params],
        out_specs=[_bspec(k, w, c0, ts) for (k, w, c0, _, _) in outs],
        out_shape=[_out_struct(k, w, S, tw, dt, nh) for (k, w, c0, tw, dt) in outs],
    )(*[a for (a, _, _, _) in ins], *params)


def rowwise_bwd(name, f, S, ts, nh, ins, params, cts, row_grads, extra=None, extra_outs=(), primal_sum=False):
    n_in, n_p, n_ct = len(ins), len(params), len(cts)
    n_rg, n_ex = len(row_grads), len(extra_outs)
    into = {j: rg[6] for j, rg in enumerate(row_grads) if len(rg) > 6}
    row_grads = [rg[:6] for rg in row_grads]
    n_al = len(into)

    def body(*refs):
        first = (pl.program_id(0) == 0) & (pl.program_id(1) == 0)
        in_refs = refs[:n_in]
        p_refs = refs[n_in:n_in + n_p]
        ct_refs = refs[n_in + n_p:n_in + n_p + n_ct]
        o = n_in + n_p + n_ct + n_al
        rg_refs = refs[o:o + n_rg]
        ex_refs = refs[o + n_rg:o + n_rg + n_ex]
        acc_refs = refs[o + n_rg + n_ex:]
        vals = [_ld(r, s[1]).astype(F32) for r, s in zip(in_refs, ins)]
        pv = [r[...] for r in p_refs]
        res, vjp = jax.vjp(f, *vals, *pv)
        if primal_sum:
            cv = [jnp.ones_like(res[0])] + [_ld(r, s[1]).astype(F32) for r, s in zip(ct_refs, cts)]
        else:
            cv = [_ld(r, s[1]).astype(F32) for r, s in zip(ct_refs, cts)]
        grads = vjp(tuple(cv))
        for r, s in zip(rg_refs, row_grads):
            _st(r, s[1], grads[s[0]])
        if extra is not None:
            for r, s, e in zip(ex_refs, extra_outs, extra(vals, cv, grads)):
                _st(r, s[0], e)

        @pl.when(first)
        def _():
            for r in acc_refs:
                r[...] = jnp.zeros_like(r)

        for r, g in zip(acc_refs[:n_p], grads[n_in:]):
            r[...] += g
        if primal_sum:
            acc_refs[n_p][...] += jnp.sum(res[0])

    acc_shapes = [jax.ShapeDtypeStruct(p.shape, F32) for p in params]
    acc_specs = [_full_spec(p) for p in params]
    if primal_sum:
        acc_shapes.append(jax.ShapeDtypeStruct((1, 128), F32))
        acc_specs.append(pl.BlockSpec((1, 128), lambda i, h: (0, 0)))
    rg_shapes = [jax.ShapeDtypeStruct(into[j].shape, into[j].dtype) if j in into else _out_struct(k, w, S, tw, dt, nh)
                 for j, (_, k, w, c0, tw, dt) in enumerate(row_grads)]
    first_alias = n_in + n_p + n_ct
    return pl.pallas_call(
        body, name=name, grid=(S // ts, nh),
        in_specs=[_bspec(k, w, c0, ts) for (_, k, w, c0) in ins] + [_full_spec(p) for p in params]
        + [_bspec(k, w, c0, ts) for (_, k, w, c0) in cts] + [pl.BlockSpec(memory_space=pl.ANY)] * n_al,
        out_specs=[_bspec(k, w, c0, ts) for (_, k, w, c0, _, _) in row_grads]
        + [_bspec(k, w, c0, ts) for (k, w, c0, _, _) in extra_outs] + acc_specs,
        out_shape=rg_shapes + [_out_struct(k, w, S, tw, dt, nh) for (k, w, c0, tw, dt) in extra_outs] + acc_shapes,
        input_output_aliases={first_alias + n: j for n, j in enumerate(sorted(into))},
    )(*[a for (a, _, _, _) in ins], *params, *[a for (a, _, _, _) in cts], *[into[j] for j in sorted(into)])


def f_prologue(x, ng, sc, sh):
    return (_rms(x, ng) * (1.0 + sc) + sh,)


def f_prologue_res(x, ng, sc, sh):
    return (_rms(x, ng) * (1.0 + sc) + sh, x)


def f_loss(x, mo, tgt, gate, fg):
    y = _rms(x + gate * mo, fg)
    return (0.5 * jnp.mean(jnp.square(y - tgt), axis=-1, keepdims=True),)


def f_foxpre(fq, fk, fv, qg, kg):
    return (_rms(fq, qg) * ATT_SCALE, _rms(fk, kg), fv)


def f_rms1(x, g):
    return (_rms(x, g),)


def f_postf(o, z):
    return (o * _silu(z),)


def f_postg(o, z, ng):
    return (_rms(o, ng) * _silu(z),)


def f_gdnpre(yq, yk, yv):
    return (_l2n(_silu(yq)) * ATT_SCALE, _l2n(_silu(yk)), _silu(yv))


def matmul(name, a, b, out_dtype, tm=512, tn=512, tk=2048, trans_b=False):
    M, K = a.shape
    N, K2 = b.shape if trans_b else b.shape[::-1]
    assert K == K2
    tm, tn, tk = _tile(M, tm), _tile(N, tn), _tile(K, tk)
    nk = K // tk
    b_spec = pl.BlockSpec((tn, tk), lambda i, j, k: (j, k)) if trans_b else pl.BlockSpec((tk, tn), lambda i, j, k: (k, j))

    def body(a_ref, b_ref, o_ref, *scr):
        part = lax.dot_general(a_ref[...], b_ref[...], (((1,), (1 if trans_b else 0,)), ((), ())),
                               preferred_element_type=F32)
        if nk == 1:
            o_ref[...] = part.astype(o_ref.dtype)
            return
        acc = scr[0]
        k = pl.program_id(2)

        @pl.when(k == 0)
        def _():
            acc[...] = part

        @pl.when(k > 0)
        def _():
            acc[...] += part

        @pl.when(k == nk - 1)
        def _():
            o_ref[...] = acc[...].astype(o_ref.dtype)

    return pl.pallas_call(
        body, name=name, grid=(M // tm, N // tn, nk),
        in_specs=[pl.BlockSpec((tm, tk), lambda i, j, k: (i, k)), b_spec],
        out_specs=pl.BlockSpec((tm, tn), lambda i, j, k: (i, j)),
        out_shape=jax.ShapeDtypeStruct((M, N), out_dtype),
        scratch_shapes=[] if nk == 1 else [pltpu.VMEM((tm, tn), F32)],
        compiler_params=pltpu.CompilerParams(dimension_semantics=("parallel", "parallel", "arbitrary")),
    )(a, b)


def _small_f(z, pv, av):
    lane = lax.broadcasted_iota(jnp.int32, z.shape, 1)
    zz = z + pv
    logf = _log_sigmoid(zz)
    gval = -jnp.exp(av) * _softplus(zz)
    beta = _sigmoid(zz)
    return jnp.where(lane < 8, logf, jnp.where(lane < 16, gval, jnp.where(lane < 24, beta, 0.0)))


def _tri(ts, upper):
    r = lax.broadcasted_iota(jnp.int32, (ts, ts), 0)
    c = lax.broadcasted_iota(jnp.int32, (ts, ts), 1)
    keep = (r <= c) if upper else (r >= c)
    same_chunk = (r // CHUNK) == (c // CHUNK)
    return keep.astype(F32), (keep & same_chunk).astype(F32)


def smalls_fwd(p_all, pv, av, S, ts):
    nt = S // ts

    def body(z_ref, pv_ref, av_ref, o_ref, carry):
        i = pl.program_id(0)

        @pl.when(i == 0)
        def _():
            carry[...] = jnp.zeros_like(carry)

        e = _small_f(z_ref[...], pv_ref[...], av_ref[...])
        lane = lax.broadcasted_iota(jnp.int32, e.shape, 1)
        l_all, l_chunk = _tri(ts, upper=False)
        cum_all = hdot(l_all, e) + carry[...]
        cum_chunk = hdot(l_chunk, e)
        carry[...] = cum_all[ts - 1:ts, :]
        o_ref[...] = jnp.where(lane < 8, cum_all, jnp.where(lane < 16, cum_chunk, e))

    return pl.pallas_call(
        body, name="smalls_fwd", grid=(nt,),
        in_specs=[pl.BlockSpec((ts, 128), lambda i: (i, CB_SMALL)), pl.BlockSpec((1, 128), lambda i: (0, 0)),
                  pl.BlockSpec((1, 128), lambda i: (0, 0))],
        out_specs=pl.BlockSpec((ts, 128), lambda i: (i, 0)),
        out_shape=jax.ShapeDtypeStruct((S, 128), F32),
        scratch_shapes=[pltpu.VMEM((1, 128), F32)],
    )(p_all, pv, av)


def smalls_bwd(p_all, pv, av, dsm, dp, S, ts):
    nt = S // ts

    def body(z_ref, pv_ref, av_ref, d_ref, dp_in, dz_ref, dpv_ref, dav_ref, carry):
        i = pl.program_id(0)

        @pl.when(i == 0)
        def _():
            carry[...] = jnp.zeros_like(carry)
            dpv_ref[...] = jnp.zeros_like(dpv_ref)
            dav_ref[...] = jnp.zeros_like(dav_ref)

        d = d_ref[...]
        lane = lax.broadcasted_iota(jnp.int32, d.shape, 1)
        u_all, u_chunk = _tri(ts, upper=True)
        rev_all = hdot(u_all, d) + carry[...]
        rev_chunk = hdot(u_chunk, d)
        carry[...] = rev_all[0:1, :]
        de = jnp.where(lane < 8, rev_all, jnp.where(lane < 16, rev_chunk, d))
        _, vjp = jax.vjp(_small_f, z_ref[...], pv_ref[...], av_ref[...])
        dz, dpv, dav = vjp(de)
        dz_ref[...] = dz.astype(dz_ref.dtype)
        dpv_ref[...] += dpv
        dav_ref[...] += dav

    rev = lambda i: (nt - 1 - i, 0)
    small_cols = pl.BlockSpec((ts, 128), lambda i: (nt - 1 - i, CB_SMALL))
    return pl.pallas_call(
        body, name="smalls_bwd", grid=(nt,),
        in_specs=[small_cols, pl.BlockSpec((1, 128), lambda i: (0, 0)), pl.BlockSpec((1, 128), lambda i: (0, 0)),
                  pl.BlockSpec((ts, 128), rev), pl.BlockSpec(memory_space=pl.ANY)],
        out_specs=[small_cols, pl.BlockSpec((1, 128), lambda i: (0, 0)), pl.BlockSpec((1, 128), lambda i: (0, 0))],
        out_shape=[jax.ShapeDtypeStruct(dp.shape, dp.dtype), jax.ShapeDtypeStruct((1, 128), F32),
                   jax.ShapeDtypeStruct((1, 128), F32)],
        scratch_shapes=[pltpu.VMEM((1, 128), F32)],
        input_output_aliases={4: 0},
    )(p_all, pv, av, dsm, dp)


def _col_to_row(col):
    return jnp.transpose(jnp.broadcast_to(col, (col.shape[0], 128)))[0:1, :]


def _row_to_col(row):
    return jnp.transpose(jnp.broadcast_to(row, (128, row.shape[1])))[:, 0:1]


def _causal_mask(s, keys_first):
    r = lax.broadcasted_iota(jnp.int32, s.shape, 0)
    c = lax.broadcasted_iota(jnp.int32, s.shape, 1)
    return jnp.where((r <= c) if keys_first else (c <= r), s, NEG)


def flash_fwd(qs, kn, vb, frow, S, tq):
    nq = S // tq
    tk = tq

    def body(q_ref, k_ref, v_ref, fr_ref, o_ref, lse_ref):
        i = pl.program_id(1)
        q = q_ref[...]
        f_base = fr_ref[0, i][:, 0:1]

        def tile(j, carry, diagonal):
            m, l, acc = carry
            off = pl.multiple_of(j * tk, tk)
            k = k_ref[pl.ds(off, tk), :]
            v = v_ref[pl.ds(off, tk), :]
            s = mm_nt(q, k) - (fr_ref[0, j] - f_base)
            if diagonal:
                s = _causal_mask(s, keys_first=False)
            m_new = jnp.maximum(m, jnp.max(s, axis=-1, keepdims=True))
            a = jnp.exp(m - m_new)
            p = jnp.exp(s - m_new)
            l = a * l + jnp.sum(p, axis=-1, keepdims=True)
            acc = a * acc + mm_nn(p, v)
            return m_new, l, acc

        init = (jnp.full((tq, 1), NEG, F32), jnp.zeros((tq, 1), F32), jnp.zeros((tq, D_HEAD), F32))
        carry = lax.fori_loop(0, i, lambda j, c: tile(j, c, False), init)
        m, l, acc = tile(i, carry, True)
        o_ref[...] = acc / l
        lse_ref[0, 0] = _col_to_row(m + jnp.log(l))

    return pl.pallas_call(
        body, name="flash_fwd", grid=(N_HEADS, nq),
        in_specs=[pl.BlockSpec((tq, D_HEAD), lambda h, i: (i, h)), pl.BlockSpec((S, D_HEAD), lambda h, i: (0, h)),
                  pl.BlockSpec((S, D_HEAD), lambda h, i: (0, h)),
                  pl.BlockSpec((1, nq, 1, tk), lambda h, i: (h, 0, 0, 0))],
        out_specs=[pl.BlockSpec((tq, D_HEAD), lambda h, i: (i, h)),
                   pl.BlockSpec((1, 1, 1, tq), lambda h, i: (h, i, 0, 0))],
        out_shape=[jax.ShapeDtypeStruct((S, WIDTH), F32), jax.ShapeDtypeStruct((N_HEADS, nq, 1, tq), F32)],
    )(qs, kn, vb, frow)


def flash_bwd_dq(qs, kn, vb, do, o, frow, lse_row, S, tq):
    nq = S // tq
    tk = tq

    def body(q_ref, k_ref, v_ref, do_ref, o_ref, fr_ref, lse_ref, dq_ref, dfq_ref, dl_ref):
        i = pl.program_id(1)
        q = q_ref[...]
        do_ = do_ref[...]
        lse_ = _row_to_col(lse_ref[0, 0])
        dl = jnp.sum(do_.astype(F32) * o_ref[...], axis=-1, keepdims=True)
        dl_ref[0, 0] = _col_to_row(dl)
        f_base = fr_ref[0, i][:, 0:1]

        def tile(j, carry, diagonal):
            dq, dfq = carry
            off = pl.multiple_of(j * tk, tk)
            k = k_ref[pl.ds(off, tk), :]
            v = v_ref[pl.ds(off, tk), :]
            s = mm_nt(q, k) - (fr_ref[0, j] - f_base)
            if diagonal:
                s = _causal_mask(s, keys_first=False)
            p = jnp.exp(s - lse_)
            ds = p * (mm_nt(do_, v) - dl)
            return dq + mm_nn(ds, k), dfq + jnp.sum(ds, axis=-1, keepdims=True)

        carry = lax.fori_loop(0, i, lambda j, c: tile(j, c, False),
                              (jnp.zeros((tq, D_HEAD), F32), jnp.zeros((tq, 1), F32)))
        dq, dfq = tile(i, carry, True)
        dq_ref[...] = dq * ATT_SCALE
        dfq_ref[0, 0] = _col_to_row(dfq)

    row = pl.BlockSpec((1, 1, 1, tq), lambda h, i: (h, i, 0, 0))
    blk = pl.BlockSpec((tq, D_HEAD), lambda h, i: (i, h))
    full = pl.BlockSpec((S, D_HEAD), lambda h, i: (0, h))
    stat = jax.ShapeDtypeStruct((N_HEADS, nq, 1, tq), F32)
    return pl.pallas_call(
        body, name="flash_bwd_dq", grid=(N_HEADS, nq),
        in_specs=[blk, full, full, blk, blk, pl.BlockSpec((1, nq, 1, tk), lambda h, i: (h, 0, 0, 0)), row],
        out_specs=[blk, row, row],
        out_shape=[jax.ShapeDtypeStruct((S, WIDTH), F32), stat, stat],
    )(qs, kn, vb, do, o, frow, lse_row)


def flash_bwd_dkv(qs, kn, vb, do, frow, lse_row, delta_row, dp, S, tq):
    nq = S // tq
    tk = tq

    def body(q_ref, k_ref, v_ref, do_ref, fr_ref, lse_ref, dl_ref, dp_in, dk_ref, dv_ref, dfk_ref):
        j = pl.program_id(1)
        k = k_ref[...]
        v = v_ref[...]
        fk = _row_to_col(fr_ref[0, j])

        def tile(i, carry, diagonal):
            dk, dv, dfk = carry
            off = pl.multiple_of(i * tq, tq)
            q = q_ref[pl.ds(off, tq), :]
            do_ = do_ref[pl.ds(off, tq), :]
            st = mm_nt(k, q) - (fk - fr_ref[0, i][:, 0:1])
            if diagonal:
                st = _causal_mask(st, keys_first=True)
            pt = jnp.exp(st - lse_ref[0, i])
            dst = pt * (mm_nt(v, do_) - dl_ref[0, i])
            return dk + mm_nn(dst, q), dv + mm_nn(pt, do_), dfk - jnp.sum(dst, axis=-1, keepdims=True)

        z = jnp.zeros((tk, D_HEAD), F32)
        carry = tile(j, (z, z, jnp.zeros((tk, 1), F32)), True)
        dk, dv, dfk = lax.fori_loop(j + 1, nq, lambda i, c: tile(i, c, False), carry)
        dk_ref[...] = dk
        dv_ref[...] = dv.astype(dv_ref.dtype)
        dfk_ref[0, 0] = _col_to_row(dfk)

    blk = pl.BlockSpec((tk, D_HEAD), lambda h, j: (j, h))
    full = pl.BlockSpec((S, D_HEAD), lambda h, j: (0, h))
    rows = pl.BlockSpec((1, nq, 1, tq), lambda h, j: (h, 0, 0, 0))
    return pl.pallas_call(
        body, name="flash_bwd_dkv", grid=(N_HEADS, nq),
        in_specs=[full, blk, blk, full, rows, rows, rows, pl.BlockSpec(memory_space=pl.ANY)],
        out_specs=[blk, pl.BlockSpec((tk, D_HEAD), lambda h, j: (j, CB_FV + h)),
                   pl.BlockSpec((1, 1, 1, tk), lambda h, j: (h, j, 0, 0))],
        out_shape=[jax.ShapeDtypeStruct((S, WIDTH), F32), jax.ShapeDtypeStruct(dp.shape, dp.dtype),
                   jax.ShapeDtypeStruct((N_HEADS, nq, 1, tk), F32)],
        input_output_aliases={7: 1},
    )(qs, kn, vb, do, frow, lse_row, delta_row, dp)


CONV_COLS = 3 * WIDTH
CONV_TC = 512


def conv_fwd(p_all, conv_w, S, ts):
    nt, tc = S // ts, CONV_TC
    cb0 = CB_GQ * 128 // tc
    r8 = ts // 8

    def body(x_ref, halo_ref, w_ref, y_ref, scr):
        i = pl.program_id(1)
        scr[0:8, :] = jnp.where(i > 0, halo_ref[...], 0.0)
        scr[8:8 + ts, :] = x_ref[...]
        w = w_ref[...]
        y = w[3:4, :] * x_ref[...]
        for j in range(CONV_K - 1):
            y = y + w[j:j + 1, :] * scr[5 + j:5 + j + ts, :]
        y_ref[...] = y

    return pl.pallas_call(
        body, name="conv_fwd", grid=(CONV_COLS // tc, nt),
        in_specs=[pl.BlockSpec((ts, tc), lambda c, i: (i, cb0 + c)),
                  pl.BlockSpec((8, tc), lambda c, i: (jnp.maximum(i * r8 - 1, 0), cb0 + c)),
                  pl.BlockSpec((CONV_K, tc), lambda c, i: (0, c))],
        out_specs=pl.BlockSpec((ts, tc), lambda c, i: (i, c)),
        out_shape=jax.ShapeDtypeStruct((S, CONV_COLS), F32),
        scratch_shapes=[pltpu.VMEM((ts + 8, tc), F32)],
    )(p_all, p_all, conv_w)


def conv_bwd(p_all, dy, conv_w, dp, S, ts):
    nt, tc = S // ts, CONV_TC
    cb0 = CB_GQ * 128 // tc
    r8 = ts // 8

    def body(x_ref, xh_ref, dy_ref, dyh_ref, w_ref, dp_in, dx_ref, dw_ref, xs, ds):
        i = pl.program_id(1)
        xs[0:8, :] = jnp.where(i > 0, xh_ref[...], 0.0)
        xs[8:8 + ts, :] = x_ref[...]
        ds[0:ts, :] = dy_ref[...]
        ds[ts:ts + 8, :] = jnp.where(i < nt - 1, dyh_ref[...], 0.0)
        w = w_ref[...]
        d = dy_ref[...]
        dx = w[3:4, :] * d
        for j in range(CONV_K - 1):
            dx = dx + w[j:j + 1, :] * ds[3 - j:3 - j + ts, :]
        dx_ref[...] = dx.astype(dx_ref.dtype)

        @pl.when(i == 0)
        def _():
            dw_ref[...] = jnp.zeros_like(dw_ref)

        rows = [jnp.sum(d * xs[5 + j:5 + j + ts, :], axis=0, keepdims=True) for j in range(CONV_K - 1)]
        rows.append(jnp.sum(d * x_ref[...], axis=0, keepdims=True))
        dw_ref[...] += jnp.concatenate(rows, axis=0)

    return pl.pallas_call(
        body, name="conv_bwd", grid=(CONV_COLS // tc, nt),
        in_specs=[pl.BlockSpec((ts, tc), lambda c, i: (i, cb0 + c)),
                  pl.BlockSpec((8, tc), lambda c, i: (jnp.maximum(i * r8 - 1, 0), cb0 + c)),
                  pl.BlockSpec((ts, tc), lambda c, i: (i, c)),
                  pl.BlockSpec((8, tc), lambda c, i: (jnp.minimum((i + 1) * r8, nt * r8 - 1), c)),
                  pl.BlockSpec((CONV_K, tc), lambda c, i: (0, c)), pl.BlockSpec(memory_space=pl.ANY)],
        out_specs=[pl.BlockSpec((ts, tc), lambda c, i: (i, cb0 + c)), pl.BlockSpec((CONV_K, tc), lambda c, i: (0, c))],
        out_shape=[jax.ShapeDtypeStruct(dp.shape, dp.dtype), jax.ShapeDtypeStruct((CONV_K, CONV_COLS), F32)],
        scratch_shapes=[pltpu.VMEM((ts + 8, tc), F32), pltpu.VMEM((ts + 8, tc), F32)],
        input_output_aliases={5: 0},
    )(p_all, p_all, dy, dy, conv_w, dp)


def _bdot(a, b, ca, cb):
    return lax.dot_general(a.astype(MXU_DTYPE), b.astype(MXU_DTYPE), (((ca,), (cb,)), ((0,), (0,))),
                           preferred_element_type=F32)


@jax.custom_vjp
def bmm_nn(a, b):
    return _bdot(a, b, 2, 1)


@jax.custom_vjp
def bmm_nt(a, b):
    return _bdot(a, b, 2, 2)


@jax.custom_vjp
def bmm_tn(a, b):
    return _bdot(a, b, 1, 1)


bmm_nn.defvjp(lambda a, b: (bmm_nn(a, b), (a, b)), lambda r, g: (bmm_nt(g, r[1]), bmm_tn(r[0], g)))
bmm_nt.defvjp(lambda a, b: (bmm_nt(a, b), (a, b)), lambda r, g: (bmm_nn(g, r[1]), bmm_tn(g, r[0])))
bmm_tn.defvjp(lambda a, b: (bmm_tn(a, b), (a, b)), lambda r, g: (bmm_nt(r[1], g), bmm_nn(r[0], g)))


def bdot_hi(a, b, ca=2, cb=1):
    return lax.dot_general(a, b, (((ca,), (cb,)), ((0,), (0,))), precision=SOLVE_PRECISION,
                           preferred_element_type=F32)


def _tri_inv_product(m):
    ii = lax.broadcasted_iota(jnp.int32, m.shape, 1)
    jj = lax.broadcasted_iota(jnp.int32, m.shape, 2)
    t = (ii == jj).astype(F32) - m
    pw = bdot_hi(m, m)
    for _ in range(4):
        t = t + bdot_hi(t, pw)
        pw = bdot_hi(pw, pw)
    return t + bdot_hi(t, pw)


def _tri_inv_bwd(t, g):
    return -bdot_hi(bdot_hi(t, g, 1, 1), t, 2, 2)


@jax.custom_vjp
def tri_inv(m):
    return _tri_inv_product(m)


@jax.custom_vjp
def tri_inv_known(m, t):
    return t


tri_inv.defvjp(lambda m: (lambda t: (t, t))(_tri_inv_product(m)), lambda t, g: (_tri_inv_bwd(t, g),))
tri_inv_known.defvjp(lambda m, t: (t, t), lambda t, g: (_tri_inv_bwd(t, g), jnp.zeros_like(t)))


def chunk_fn(q, k, v, gc_col, gc_row, beta, state, t_known=None):
    shape = (N_HEADS, CHUNK, CHUNK)
    ii = lax.broadcasted_iota(jnp.int32, shape, 1)
    jj = lax.broadcasted_iota(jnp.int32, shape, 2)
    lower = ii >= jj
    strict = ii > jj
    decay = jnp.exp(jnp.where(lower, gc_col - gc_row, NEG))
    kb = k * beta
    vb = v * beta
    m = jnp.where(strict, bmm_nt(kb, k) * decay, 0.0)
    t = tri_inv(m) if t_known is None else tri_inv_known(m, t_known)
    u = bdot_hi(t, vb)
    w = bdot_hi(t, kb * jnp.exp(gc_col))
    attn = jnp.where(lower, bmm_nt(q, k) * decay, 0.0)
    v_new = u - bmm_nn(w, state)
    o = bmm_nn(q * jnp.exp(gc_col), state) + bmm_nn(attn, v_new)
    g_last = gc_col[:, CHUNK - 1:CHUNK, :]
    k_dec = k * jnp.exp(g_last - gc_col)
    new_state = state * jnp.exp(g_last) + bmm_tn(k_dec, v_new)
    return (o, new_state, t) if t_known is None else (o, new_state)


def _heads(ref):
    return jnp.stack([ref[:, h * D_HEAD:(h + 1) * D_HEAD] for h in range(N_HEADS)], axis=0)


def _head_cols(ref, lane0):
    return jnp.stack([ref[:, lane0 + h:lane0 + h + 1] for h in range(N_HEADS)], axis=0)


def _qkv_head_cols(h):
    return [slice(g * WIDTH + h * D_HEAD, g * WIDTH + (h + 1) * D_HEAD) for g in range(3)]


def gdn_pre_fwd(y, S, ts):
    def body(y_ref, o_ref):
        for h in range(N_HEADS):
            cols = _qkv_head_cols(h)
            for c, o in zip(cols, f_gdnpre(*[y_ref[:, c] for c in cols])):
                o_ref[:, c] = o

    spec = pl.BlockSpec((ts, CONV_COLS), lambda i: (i, 0))
    return pl.pallas_call(body, name="gdn_pre", grid=(S // ts,), in_specs=[spec], out_specs=spec,
                          out_shape=jax.ShapeDtypeStruct((S, CONV_COLS), F32))(y)


def gdn_pre_bwd(y, dqkv, S, ts):
    def body(y_ref, d_ref, o_ref):
        for h in range(N_HEADS):
            cols = _qkv_head_cols(h)
            _, vjp = jax.vjp(f_gdnpre, *[y_ref[:, c] for c in cols])
            for c, g in zip(cols, vjp(tuple(d_ref[:, c] for c in cols))):
                o_ref[:, c] = g

    spec = pl.BlockSpec((ts, CONV_COLS), lambda i: (i, 0))
    return pl.pallas_call(body, name="gdn_pre_bwd", grid=(S // ts,), in_specs=[spec, spec], out_specs=spec,
                          out_shape=jax.ShapeDtypeStruct((S, CONV_COLS), F32))(y, dqkv)


def gdn_fwd(qkv, smalls, gc_row, S):
    n = S // CHUNK

    def body(q_ref, k_ref, v_ref, sm_ref, gr_ref, o_ref, st_ref, t_ref, state):
        @pl.when(pl.program_id(0) == 0)
        def _():
            state[...] = jnp.zeros_like(state)

        s0 = state[...]
        st_ref[0] = s0
        o, s1, t = chunk_fn(_heads(q_ref), _heads(k_ref), _heads(v_ref), _head_cols(sm_ref, 8),
                            gr_ref[0][:, None, :], _head_cols(sm_ref, 16), s0)
        for h in range(N_HEADS):
            o_ref[:, h * D_HEAD:(h + 1) * D_HEAD] = o[h]
        state[...] = s1
        t_ref[0] = t

    return pl.pallas_call(
        body, name="gdn_chunk_fwd", grid=(n,),
        in_specs=[pl.BlockSpec((CHUNK, WIDTH), lambda i, g=g: (i, g)) for g in range(3)]
        + [pl.BlockSpec((CHUNK, 128), lambda i: (i, 0)), pl.BlockSpec((1, N_HEADS, CHUNK), lambda i: (i, 0, 0))],
        out_specs=[pl.BlockSpec((CHUNK, WIDTH), lambda i: (i, 0)),
                   pl.BlockSpec((1, N_HEADS, D_HEAD, D_HEAD), lambda i: (i, 0, 0, 0)),
                   pl.BlockSpec((1, N_HEADS, CHUNK, CHUNK), lambda i: (i, 0, 0, 0))],
        out_shape=[jax.ShapeDtypeStruct((S, WIDTH), F32), jax.ShapeDtypeStruct((n, N_HEADS, D_HEAD, D_HEAD), F32),
                   jax.ShapeDtypeStruct((n, N_HEADS, CHUNK, CHUNK), F32)],
        scratch_shapes=[pltpu.VMEM((N_HEADS, D_HEAD, D_HEAD), F32)],
    )(qkv, qkv, qkv, smalls, gc_row)


def gdn_bwd(qkv, smalls, gc_row, states, tinv, do, S):
    n = S // CHUNK

    def body(q_ref, k_ref, v_ref, sm_ref, gr_ref, st_ref, t_ref, do_ref, dqkv_ref, dsm_ref, dgr_ref, dstate):
        @pl.when(pl.program_id(0) == 0)
        def _():
            dstate[...] = jnp.zeros_like(dstate)

        t_saved = t_ref[0]
        _, vjp = jax.vjp(lambda *a: chunk_fn(*a, t_known=t_saved), _heads(q_ref), _heads(k_ref), _heads(v_ref),
                         _head_cols(sm_ref, 8), gr_ref[0][:, None, :], _head_cols(sm_ref, 16), st_ref[0])
        dq, dk, dv, dgc, dgr, dbt, ds = vjp((_heads(do_ref), dstate[...]))
        for h in range(N_HEADS):
            dqkv_ref[:, h * D_HEAD:(h + 1) * D_HEAD] = dq[h]
            dqkv_ref[:, WIDTH + h * D_HEAD:WIDTH + (h + 1) * D_HEAD] = dk[h]
            dqkv_ref[:, 2 * WIDTH + h * D_HEAD:2 * WIDTH + (h + 1) * D_HEAD] = dv[h]
        dstate[...] = ds
        cols = [dgc[h] for h in range(N_HEADS)] + [dbt[h] for h in range(N_HEADS)]
        dsm_ref[...] = jnp.concatenate([jnp.zeros((CHUNK, 8), F32)] + cols + [jnp.zeros((CHUNK, 128 - 24), F32)],
                                       axis=1)
        dgr_ref[0] = jnp.concatenate([dgr[h] for h in range(N_HEADS)], axis=0)

    rev = lambda c: (lambda i: (n - 1 - i, c))
    return pl.pallas_call(
        body, name="gdn_chunk_bwd", grid=(n,),
        in_specs=[pl.BlockSpec((CHUNK, WIDTH), rev(g)) for g in range(3)]
        + [pl.BlockSpec((CHUNK, 128), rev(0)), pl.BlockSpec((1, N_HEADS, CHUNK), lambda i: (n - 1 - i, 0, 0)),
           pl.BlockSpec((1, N_HEADS, D_HEAD, D_HEAD), lambda i: (n - 1 - i, 0, 0, 0)),
           pl.BlockSpec((1, N_HEADS, CHUNK, CHUNK), lambda i: (n - 1 - i, 0, 0, 0)),
           pl.BlockSpec((CHUNK, WIDTH), rev(0))],
        out_specs=[pl.BlockSpec((CHUNK, 3 * WIDTH), rev(0)), pl.BlockSpec((CHUNK, 128), rev(0)),
                   pl.BlockSpec((1, N_HEADS, CHUNK), lambda i: (n - 1 - i, 0, 0))],
        out_shape=[jax.ShapeDtypeStruct((S, 3 * WIDTH), F32)] + [jax.ShapeDtypeStruct((S, 128), F32),
                                                                  jax.ShapeDtypeStruct((n, N_HEADS, CHUNK), F32)],
        scratch_shapes=[pltpu.VMEM((N_HEADS, D_HEAD, D_HEAD), F32)],
    )(qkv, qkv, qkv, smalls, gc_row, states, tinv, do)


def _flip(a, bit):
    return 1 - a if bit else a


def allgather8(name, buf):
    R, W = buf.shape

    def body(x_ref, out_ref, send_sems, recv_sems, local_sem):
        x, y, c = lax.axis_index("x"), lax.axis_index("y"), lax.axis_index("c")

        def slot(px, py, pc):
            return out_ref.at[4 * px + 2 * py + pc]

        mine = pltpu.make_async_copy(x_ref, slot(x, y, c), local_sem)
        mine.start()
        sends = []
        for r in range(1, N_DEV):
            peer = (_flip(x, r & 4), _flip(y, r & 2), _flip(c, r & 1))
            cp = pltpu.make_async_remote_copy(src_ref=x_ref, dst_ref=slot(x, y, c), send_sem=send_sems.at[r - 1],
                                              recv_sem=recv_sems.at[r - 1], device_id=peer, device_id_type=MESH_ID)
            cp.start()
            sends.append(cp)
        for r in range(1, N_DEV):
            peer = (_flip(x, r & 4), _flip(y, r & 2), _flip(c, r & 1))
            pltpu.make_async_remote_copy(src_ref=x_ref, dst_ref=slot(*peer), send_sem=send_sems.at[r - 1],
                                         recv_sem=recv_sems.at[r - 1], device_id=peer,
                                         device_id_type=MESH_ID).wait_recv()
        for cp in sends:
            cp.wait_send()
        mine.wait()

    return pl.pallas_call(
        body, name=name,
        out_shape=jax.ShapeDtypeStruct((N_DEV, R, W), buf.dtype),
        in_specs=[pl.BlockSpec(memory_space=pltpu.VMEM)],
        out_specs=pl.BlockSpec(memory_space=pltpu.VMEM),
        scratch_shapes=[pltpu.SemaphoreType.DMA((N_DEV - 1,)), pltpu.SemaphoreType.DMA((N_DEV - 1,)),
                        pltpu.SemaphoreType.DMA],
    )(buf)


def allgather_chips(name, blk):
    R = blk.shape[0]
    half = R // 2
    n_far = N_CHIPS - 1

    def body(x_ref, out_ref, send_sems, recv_sems, local_sem):
        x, y, c = lax.axis_index("x"), lax.axis_index("y"), lax.axis_index("c")
        sibling = (x, y, 1 - c)
        chips = [(_flip(x, r & 2), _flip(y, r & 1)) for r in range(1, N_CHIPS)]

        def rows(px, py, pc):
            return out_ref.at[2 * px + py, pl.ds(pc * half, half), :]

        def copy(k, dst, to, src):
            return pltpu.make_async_remote_copy(src_ref=src, dst_ref=dst, send_sem=send_sems.at[k],
                                                recv_sem=recv_sems.at[k], device_id=to, device_id_type=MESH_ID)

        mine = pltpu.make_async_copy(x_ref, out_ref.at[2 * x + y], local_sem)
        mine.start()
        my_half = x_ref.at[pl.ds(c * half, half), :]
        first = [copy(j, rows(x, y, c), (*chip, c), my_half) for j, chip in enumerate(chips)]
        for cp in first:
            cp.start()
        passed = [copy(n_far + j, rows(*chip, c), sibling, rows(*chip, c)) for j, chip in enumerate(chips)]
        for j, chip in enumerate(chips):
            copy(j, rows(*chip, c), (*chip, c), my_half).wait_recv()
            passed[j].start()
        for j, chip in enumerate(chips):
            copy(n_far + j, rows(*chip, 1 - c), sibling, my_half).wait_recv()
        for cp in first + passed:
            cp.wait_send()
        mine.wait()

    return pl.pallas_call(
        body, name=name,
        out_shape=jax.ShapeDtypeStruct((N_CHIPS,) + blk.shape, blk.dtype),
        in_specs=[pl.BlockSpec(memory_space=pl.ANY)],
        out_specs=pl.BlockSpec(memory_space=pl.ANY),
        scratch_shapes=[pltpu.SemaphoreType.DMA((2 * n_far,)), pltpu.SemaphoreType.DMA((2 * n_far,)),
                        pltpu.SemaphoreType.DMA],
    )(blk)


def scatter_chips(name, parts):
    def body(x_ref, out_ref, send_sems, recv_sems):
        x, y, c = lax.axis_index("x"), lax.axis_index("y"), lax.axis_index("c")
        sends = []
        for r in range(1, N_CHIPS):
            peer = (_flip(x, r & 2), _flip(y, r & 1), c)
            cp = pltpu.make_async_remote_copy(src_ref=x_ref.at[2 * peer[0] + peer[1]], dst_ref=out_ref.at[r - 1],
                                              send_sem=send_sems.at[r - 1], recv_sem=recv_sems.at[r - 1],
                                              device_id=peer, device_id_type=MESH_ID)
            cp.start()
            sends.append(cp)
        for cp in sends:
            cp.wait_recv()
        for cp in sends:
            cp.wait_send()

    return pl.pallas_call(
        body, name=name,
        out_shape=jax.ShapeDtypeStruct((N_CHIPS - 1,) + parts.shape[1:], parts.dtype),
        in_specs=[pl.BlockSpec(memory_space=pl.ANY)],
        out_specs=pl.BlockSpec(memory_space=pl.ANY),
        scratch_shapes=[pltpu.SemaphoreType.DMA((N_CHIPS - 1,)), pltpu.SemaphoreType.DMA((N_CHIPS - 1,))],
    )(parts)


def sibling_send_other_half(name, g4):
    n, R, W = g4.shape
    half = R // 2

    def body(x_ref, out_ref, send_sem, recv_sem):
        x, y, c = lax.axis_index("x"), lax.axis_index("y"), lax.axis_index("c")
        cp = pltpu.make_async_remote_copy(src_ref=x_ref.at[:, pl.ds((1 - c) * half, half), :], dst_ref=out_ref,
                                          send_sem=send_sem, recv_sem=recv_sem, device_id=(x, y, 1 - c),
                                          device_id_type=MESH_ID)
        cp.start()
        cp.wait_recv()
        cp.wait_send()

    return pl.pallas_call(
        body, name=name,
        out_shape=jax.ShapeDtypeStruct((n, half, W), g4.dtype),
        in_specs=[pl.BlockSpec(memory_space=pl.ANY)],
        out_specs=pl.BlockSpec(memory_space=pl.ANY),
        scratch_shapes=[pltpu.SemaphoreType.DMA, pltpu.SemaphoreType.DMA],
    )(g4)


def sibling_merge(name, mine):
    half, W = mine.shape

    def body(x_ref, out_ref, send_sem, recv_sem, local_sem):
        x, y, c = lax.axis_index("x"), lax.axis_index("y"), lax.axis_index("c")
        keep = pltpu.make_async_copy(x_ref, out_ref.at[pl.ds(c * half, half), :], local_sem)
        keep.start()
        cp = pltpu.make_async_remote_copy(src_ref=x_ref, dst_ref=out_ref.at[pl.ds(c * half, half), :],
                                          send_sem=send_sem, recv_sem=recv_sem, device_id=(x, y, 1 - c),
                                          device_id_type=MESH_ID)
        cp.start()
        pltpu.make_async_remote_copy(src_ref=x_ref, dst_ref=out_ref.at[pl.ds((1 - c) * half, half), :],
                                     send_sem=send_sem, recv_sem=recv_sem, device_id=(x, y, 1 - c),
                                     device_id_type=MESH_ID).wait_recv()
        cp.wait_send()
        keep.wait()

    return pl.pallas_call(
        body, name=name,
        out_shape=jax.ShapeDtypeStruct((2 * half, W), mine.dtype),
        in_specs=[pl.BlockSpec(memory_space=pl.ANY)],
        out_specs=pl.BlockSpec(memory_space=pl.ANY),
        scratch_shapes=[pltpu.SemaphoreType.DMA, pltpu.SemaphoreType.DMA, pltpu.SemaphoreType.DMA],
    )(mine)


def sum_parts(name, own, recv, tr):
    R, W = own.shape
    tr = _tile(R, tr)

    def body(o_ref, r_ref, out_ref):
        acc = o_ref[...]
        for k in range(recv.shape[0]):
            acc = acc + r_ref[k].astype(F32)
        out_ref[...] = acc

    return pl.pallas_call(
        body, name=name, grid=(R // tr,),
        in_specs=[pl.BlockSpec((tr, W), lambda i: (i, 0)), pl.BlockSpec((recv.shape[0], tr, W), lambda i: (0, i, 0))],
        out_specs=pl.BlockSpec((tr, W), lambda i: (i, 0)),
        out_shape=jax.ShapeDtypeStruct((R, W), F32),
    )(own, recv)


def sum_own_half(name, g4, recv, c_idx, tr):
    n, R, W = g4.shape
    half = R // 2
    tr = _tile(half, tr)
    nb = half // tr

    def body(c_ref, g_ref, r_ref, o_ref, ob_ref):
        s = g_ref[0] + r_ref[0]
        o_ref[0] = s
        ob_ref[0] = s.astype(ob_ref.dtype)

    mine = pl.BlockSpec((1, tr, W), lambda j, i, c: (j, i, 0))
    return pl.pallas_call(
        body, name=name,
        grid_spec=pltpu.PrefetchScalarGridSpec(
            num_scalar_prefetch=1, grid=(n, nb),
            in_specs=[pl.BlockSpec((1, tr, W), lambda j, i, c: (j, c[0] * nb + i, 0)), mine],
            out_specs=[mine, mine]),
        out_shape=[jax.ShapeDtypeStruct((n, half, W), F32), jax.ShapeDtypeStruct((n, half, W), MXU_DTYPE)],
    )(c_idx, g4, recv)


def sum_devices(name, g):
    def body(g_ref, o_ref):
        acc = g_ref[0]
        for d in range(1, N_DEV):
            acc = acc + g_ref[d]
        o_ref[...] = acc

    return pl.pallas_call(body, name=name, out_shape=jax.ShapeDtypeStruct(g.shape[1:], F32))(g)


def ada_fwd(c_all, w_ada):
    K, N = w_ada.shape
    tn = _tile(N, 512)

    def body(c_ref, w_ref, o_ref):
        o_ref[...] = mm_nn(_silu(c_ref[...]), w_ref[...])

    return pl.pallas_call(
        body, name="ada_fwd", grid=(N // tn,),
        in_specs=[pl.BlockSpec((N_DEV, K), lambda j: (0, 0)), pl.BlockSpec((K, tn), lambda j: (0, j))],
        out_specs=pl.BlockSpec((N_DEV, tn), lambda j: (0, j)),
        out_shape=jax.ShapeDtypeStruct((N_DEV, N), F32),
    )(c_all, w_ada)


def ada_grad(c_all, dmod):
    K = c_all.shape[1]
    N = dmod.shape[1]
    tn = _tile(N, 512)

    def body(c_ref, d_ref, o_ref):
        o_ref[...] = mm_tn(_silu(c_ref[...]), d_ref[...])

    return pl.pallas_call(
        body, name="ada_grad", grid=(N // tn,),
        in_specs=[pl.BlockSpec((N_DEV, K), lambda j: (0, 0)), pl.BlockSpec((N_DEV, tn), lambda j: (0, j))],
        out_specs=pl.BlockSpec((K, tn), lambda j: (0, j)),
        out_shape=jax.ShapeDtypeStruct((K, N), F32),
    )(c_all, dmod)


def adamw(name, w, gparts, m, v, tr):
    R, W = w.shape
    tr = _tile(R, tr)
    ng = len(gparts)

    def body(w_ref, *refs):
        g_refs, (m_ref, v_ref, g_out, d_out, m_out, v_out) = refs[:ng], refs[ng:]
        g = g_refs[0][...]
        for r in g_refs[1:]:
            g = g + r[...]
        m1 = ADAM_B1 * m_ref[...] + (1.0 - ADAM_B1) * g
        v1 = ADAM_B2 * v_ref[...] + (1.0 - ADAM_B2) * jnp.square(g)
        m_hat = m1 / (1.0 - ADAM_B1 ** ADAM_STEP)
        v_hat = v1 / (1.0 - ADAM_B2 ** ADAM_STEP)
        g_out[...] = g
        d_out[...] = -ADAM_LR * (m_hat / (jnp.sqrt(v_hat) + ADAM_EPS) + ADAM_WD * w_ref[...])
        m_out[...] = m1
        v_out[...] = v1

    spec = pl.BlockSpec((tr, W), lambda i: (i, 0))
    return pl.pallas_call(
        body, name=name, grid=(R // tr,),
        in_specs=[spec] * (3 + ng), out_specs=[spec] * 4,
        out_shape=[jax.ShapeDtypeStruct((R, W), F32)] * 4,
    )(w, *gparts, m, v)


_REF_SEGMENTS = ((0, 4 * WIDTH, 0), (4 * WIDTH, 4 * WIDTH + 8, 4 * WIDTH), (4 * WIDTH + 8, 8 * WIDTH + 8, -8),
                 (8 * WIDTH + 8, IN_WIDTH, 0))


def _internal_from_blocks(blocks, n):
    a = 4 * WIDTH
    pieces = []
    for lo, hi in ((0, a), (a + 8, 2 * a + 8), (a, a + 8), (2 * a + 8, IN_WIDTH)):
        for j in range(N_CHIPS):
            s, e = max(lo, j * n), min(hi, (j + 1) * n)
            if s < e:
                pieces.append(blocks[j][:, s - j * n:e - j * n])
    pieces.append(jnp.zeros(blocks.shape[1:2] + (128 - N_SMALL,), blocks.dtype))
    return jnp.concatenate(pieces, axis=1)


def _block_from_internal(g, j, n):
    pieces = []
    for lo, hi, shift in _REF_SEGMENTS:
        s, e = max(lo, j * n), min(hi, (j + 1) * n)
        if s < e:
            pieces.append(g[:, s + shift:e + shift])
    return jnp.concatenate(pieces, axis=1)


def _pad_lanes(v, n=128):
    return jnp.pad(v, ((0, 0), (0, n - v.shape[1])))


def kernel(x, c, norm_g, w_ada, b_ada, w_in, b_fgate, fox_qn_g, fox_kn_g, gdn_conv_w, gdn_A_log, gdn_dt_bias, gdn_norm_g, w_out, final_g, loss_target, m_norm_g, m_w_ada, m_b_ada, m_w_in, m_b_fgate, m_fox_qn_g, m_fox_kn_g, m_gdn_conv_w, m_gdn_A_log, m_gdn_dt_bias, m_gdn_norm_g, m_w_out, m_final_g, v_norm_g, v_w_ada, v_b_ada, v_w_in, v_b_fgate, v_fox_qn_g, v_fox_kn_g, v_gdn_conv_w, v_gdn_A_log, v_gdn_dt_bias, v_gdn_norm_g, v_w_out, v_final_g):
    S = x.shape[1]
    H = N_HEADS
    ix, iy, ic = lax.axis_index("x"), lax.axis_index("y"), lax.axis_index("c")
    chip = 2 * ix + iy
    me = 2 * chip + ic
    x2, tgt = x[0], loss_target[0]
    ts_wide = _tile(S, 256)
    ts = _tile(S, 512)
    tq = _tile(S, 512)
    nq = S // tq
    nch = S // CHUNK
    conv_cols_chip = CONV_COLS // N_CHIPS

    pay = jnp.concatenate([c, _pad_lanes(gdn_conv_w[0], D_MODEL), jnp.zeros((3, D_MODEL), F32)], axis=0)
    g1 = allgather8("ag_c", pay)
    c_all = g1[:, 0, :]
    conv_w = jnp.concatenate([g1[2 * j, 1:1 + CONV_K, :conv_cols_chip] for j in range(N_CHIPS)], axis=1)
    g2 = allgather8("ag_mod", ada_fwd(c_all, w_ada[0]))
    mod_rows = jnp.concatenate([g2[2 * j] for j in range(N_CHIPS)], axis=1)
    mod = lax.dynamic_slice_in_dim(mod_rows, me, 1, axis=0) + b_ada
    shift, scale, gate = mod[:, :D_MODEL], mod[:, D_MODEL:2 * D_MODEL], mod[:, 2 * D_MODEL:]

    n_in = w_in.shape[2]
    win4 = allgather_chips("ag_w_in", w_in[0].astype(MXU_DTYPE))
    w_all = _internal_from_blocks(win4, n_in)
    wout4 = allgather_chips("ag_w_out", w_out[0].astype(MXU_DTYPE))
    w_out_full = wout4.reshape(2 * WIDTH, D_MODEL)

    (h_b,) = rowwise_fwd("prologue", f_prologue, S, ts_wide, 1, [(x2, "row", D_MODEL, 0)], [norm_g, scale, shift],
                         [("row", D_MODEL, 0, D_MODEL, MXU_DTYPE)])
    p_all = matmul("in_proj", h_b, w_all, F32, tm=512, tn=640, tk=2048)
    pv = jnp.concatenate([b_fgate, gdn_dt_bias, jnp.zeros((1, 128 - 16), F32)], axis=1)
    av = jnp.concatenate([jnp.zeros((1, 8), F32), gdn_A_log, jnp.zeros((1, 128 - 16), F32)], axis=1)
    smalls = smalls_fwd(p_all, pv, av, S, ts)
    frow = jnp.transpose(smalls[:, :H]).reshape(H, nq, 1, tq)
    gc_row = jnp.transpose(smalls[:, 8:16].reshape(nch, CHUNK, H), (0, 2, 1))
    head_b = ("head", D_HEAD, 0, WIDTH, MXU_DTYPE)
    head_f = ("head", D_HEAD, 0, WIDTH, F32)
    pcol = lambda cb: (p_all, "head", D_HEAD, cb)
    qn, kn, vb = rowwise_fwd("fox_pre", f_foxpre, S, ts, H, [pcol(CB_FQ), pcol(CB_FK), pcol(CB_FV)],
                             [fox_qn_g, fox_kn_g], [head_b] * 3)
    o_fox, lse = flash_fwd(qn, kn, vb, frow, S, tq)
    y_conv = conv_fwd(p_all, conv_w, S, ts)
    qkv = gdn_pre_fwd(y_conv, S, ts_wide)
    o_gdn, states, tinv = gdn_fwd(qkv, smalls, gc_row, S)
    (mix_f,) = rowwise_fwd("post_fox", f_postf, S, ts, H, [(o_fox, "head", D_HEAD, 0), pcol(CB_FZ)], [], [head_b])
    (mix_g,) = rowwise_fwd("post_gdn", f_postg, S, ts, H, [(o_gdn, "head", D_HEAD, 0), pcol(CB_GZ)], [gdn_norm_g],
                           [head_b])
    mixed = jnp.concatenate([mix_f, mix_g], axis=1)
    mo = matmul("out_proj", mixed, w_out_full, F32)

    wide = lambda a: (a, "row", D_MODEL, 0)
    dx_res, dmo, d_gate, d_final_g, loss_acc = rowwise_bwd(
        "loss", f_loss, S, ts_wide, 1, [wide(x2), wide(mo), wide(tgt)], [gate, final_g[None, :]], [],
        [(0, "row", D_MODEL, 0, D_MODEL, F32), (1, "row", D_MODEL, 0, D_MODEL, MXU_DTYPE)], primal_sum=True)

    d_mixed = matmul("d_mixed", dmo, w_out_full, F32, trans_b=True)
    g_w_out = matmul("g_w_out", jnp.transpose(mixed), dmo, F32)
    dp = lax.empty((S, P_COLS), MXU_DTYPE)
    into_dp = lambda idx, cb: (idx, "head", D_HEAD, cb, P_COLS, MXU_DTYPE, dp)
    do_fox, dp = rowwise_bwd(
        "post_fox_bwd", f_postf, S, ts, H, [(o_fox, "head", D_HEAD, 0), pcol(CB_FZ)], [],
        [(d_mixed, "head", D_HEAD, 0)], [(0,) + head_b, into_dp(1, CB_FZ)])
    do_gdn, dp, d_gdn_norm_g = rowwise_bwd(
        "post_gdn_bwd", f_postg, S, ts, H, [(o_gdn, "head", D_HEAD, 0), pcol(CB_GZ)], [gdn_norm_g],
        [(d_mixed, "head", D_HEAD, 8)], [(0,) + head_f, into_dp(1, CB_GZ)])
    d_qn, dfq_row, delta = flash_bwd_dq(qn, kn, vb, do_fox, o_fox, frow, lse, S, tq)
    d_kn, dp, dfk_row = flash_bwd_dkv(qn, kn, vb, do_fox, frow, lse, delta, dp, S, tq)
    dp, d_qn_g = rowwise_bwd("fox_q_bwd", f_rms1, S, ts, H, [pcol(CB_FQ)], [fox_qn_g], [(d_qn, "head", D_HEAD, 0)],
                             [into_dp(0, CB_FQ)])
    dp, d_kn_g = rowwise_bwd("fox_k_bwd", f_rms1, S, ts, H, [pcol(CB_FK)], [fox_kn_g], [(d_kn, "head", D_HEAD, 0)],
                             [into_dp(0, CB_FK)])
    d_qkv, dsm_chunk, dgc_row = gdn_bwd(qkv, smalls, gc_row, states, tinv, do_gdn, S)
    dp, d_conv_w = conv_bwd(p_all, gdn_pre_bwd(y_conv, d_qkv, S, ts_wide), conv_w, dp, S, ts)
    d_f = jnp.transpose((dfq_row + dfk_row).reshape(H, S))
    d_gc = jnp.transpose(dgc_row, (0, 2, 1)).reshape(S, H)
    dsm = dsm_chunk + jnp.concatenate([d_f, d_gc, jnp.zeros((S, 128 - 16), F32)], axis=1)
    dp, d_pv, d_av = smalls_bwd(p_all, pv, av, dsm, dp, S, ts)
    d_h = matmul("d_h", dp, w_all, F32, tm=512, tn=1024, tk=1664, trans_b=True)
    g_w_all = matmul("g_w_in", jnp.transpose(h_b), dp, F32, tm=512, tn=640, tk=2048)
    d_x, d_norm_g, d_scale, d_shift = rowwise_bwd(
        "prologue_bwd", f_prologue_res, S, ts_wide, 1, [wide(x2)], [norm_g, scale, shift],
        [wide(d_h), wide(dx_res)], [(0, "row", D_MODEL, 0, D_MODEL, F32)])

    parts = [d_shift, d_scale, d_gate, d_norm_g, d_final_g, d_conv_w.reshape(1, CONV_K * CONV_COLS), d_qn_g, d_kn_g,
             d_gdn_norm_g, d_pv, d_av, loss_acc]
    n_pay = sum(p.shape[1] for p in parts)
    pay_w = -(-n_pay // 1024) * 1024
    pay2 = jnp.concatenate(parts + [jnp.zeros((1, pay_w - n_pay), F32)], axis=1).reshape(8, pay_w // 8)
    g3 = allgather8("ag_small", pay2)
    tot = sum_devices("sum_small", g3).reshape(1, pay_w)
    dmod_all = g3.reshape(N_DEV, pay_w)[:, :3 * D_MODEL]
    o = 3 * D_MODEL
    g_b_ada = tot[:, :o]
    g_norm_g, g_final_g = tot[:, o:o + D_MODEL], tot[:, o + D_MODEL:o + 2 * D_MODEL]
    o += 2 * D_MODEL
    g_conv_full = tot[:, o:o + CONV_K * CONV_COLS].reshape(CONV_K, CONV_COLS)
    g_conv = lax.dynamic_slice_in_dim(g_conv_full, chip * conv_cols_chip, conv_cols_chip, axis=1)
    o += CONV_K * CONV_COLS
    g_qn_g, g_kn_g, g_gdn_ng = tot[:, o:o + 128], tot[:, o + 128:o + 256], tot[:, o + 256:o + 384]
    g_pv, g_av = tot[:, o + 384:o + 512], tot[:, o + 512:o + 640]
    loss = tot[0, o + 640]
    g_b_fgate, g_dt_bias, g_a_log = g_pv[:, :8], g_pv[:, 8:16], g_av[:, 8:16]

    def small_vec(vals):
        flat = [norm for norm in vals[:3]] + [vals[3].reshape(1, CONV_K * conv_cols_chip)] + list(vals[4:7]) \
            + [_pad_lanes(s) for s in vals[7:]]
        n = sum(f.shape[1] for f in flat)
        nw = -(-n // 1024) * 1024
        return jnp.concatenate(flat + [jnp.zeros((1, nw - n), F32)], axis=1).reshape(8, nw // 8)

    sw = small_vec([norm_g, b_ada, final_g[None, :], gdn_conv_w[0], fox_qn_g, fox_kn_g, gdn_norm_g, b_fgate, gdn_A_log,
                    gdn_dt_bias])
    sm = small_vec([m_norm_g, m_b_ada, m_final_g[None, :], m_gdn_conv_w[0], m_fox_qn_g, m_fox_kn_g, m_gdn_norm_g,
                    m_b_fgate, m_gdn_A_log, m_gdn_dt_bias])
    sv = small_vec([v_norm_g, v_b_ada, v_final_g[None, :], v_gdn_conv_w[0], v_fox_qn_g, v_fox_kn_g, v_gdn_norm_g,
                    v_b_fgate, v_gdn_A_log, v_gdn_dt_bias])
    sg = small_vec([g_norm_g, g_b_ada, g_final_g, g_conv, g_qn_g, g_kn_g, g_gdn_ng, g_b_fgate, g_a_log, g_dt_bias])
    small_out = adamw("adamw_small", sw, [sg], sm, sv, 8)

    def split_small(a):
        a = a.reshape(1, -1)
        out, o = [], 0
        for n, shp in ((D_MODEL, (1, D_MODEL)), (3 * D_MODEL, (1, 3 * D_MODEL)), (D_MODEL, (D_MODEL,)),
                       (CONV_K * conv_cols_chip, (1, CONV_K, conv_cols_chip)), (128, (1, 128)), (128, (1, 128)),
                       (128, (1, 128))):
            out.append(a[:, o:o + n].reshape(shp))
            o += n
        for _ in range(3):
            out.append(a[:, o:o + 8])
            o += 128
        return out

    n_ada = w_ada.shape[2]
    g_w_ada = ada_grad(c_all, lax.dynamic_slice_in_dim(dmod_all, chip * n_ada, n_ada, axis=1))
    ada_out = adamw("adamw_w_ada", w_ada[0], [g_w_ada], m_w_ada[0], v_w_ada[0], 128)

    c_idx = jnp.reshape(ic, (1,)).astype(jnp.int32)

    def reduce_scatter(tag, g4, tr):
        recv = sibling_send_other_half("d2d_" + tag, g4)
        p4, p4_sent = sum_own_half("sum2_" + tag, g4, recv, c_idx, tr)
        far = scatter_chips("rs_" + tag, p4_sent)
        own = lax.dynamic_index_in_dim(p4, chip, axis=0, keepdims=False)
        return sibling_merge("merge_" + tag, sum_parts("sum_" + tag, own, far, tr))

    g4_in = jnp.stack([_block_from_internal(g_w_all, j, n_in) for j in range(N_CHIPS)])
    in_out = adamw("adamw_w_in", w_in[0], [reduce_scatter("w_in", g4_in, 128)], m_w_in[0], v_w_in[0], 128)
    g4_out = g_w_out.reshape(N_CHIPS, w_out.shape[1], D_MODEL)
    out_out = adamw("adamw_w_out", w_out[0], [reduce_scatter("w_out", g4_out, 128)], m_w_out[0], v_w_out[0], 128)

    res = [loss, d_x[None]]
    for k in range(4):
        s = split_small(small_out[k])
        big = [ada_out[k][None], in_out[k][None], out_out[k][None]]
        res += [s[0], big[0], s[1], big[1], s[7], s[4], s[5], s[3], s[8], s[9], s[6], big[2], s[2]]
    return tuple(res)
```

```python
import functools

import jax
import jax.numpy as jnp
from jax import lax
from jax.experimental import pallas as pl
from jax.experimental.pallas import tpu as pltpu

F32 = jnp.float32
MXU_DTYPE = jnp.bfloat16
HIGHEST = lax.Precision.HIGHEST
SOLVE_PRECISION = lax.Precision.HIGH
MESH_ID = pl.DeviceIdType.MESH

D_MODEL = 2048
N_HEADS = 8
D_HEAD = 128
WIDTH = N_HEADS * D_HEAD
CHUNK = 64
CONV_K = 4
EPS = 1e-6
NEG = -1e30
ATT_SCALE = D_HEAD ** -0.5
N_SMALL = 3 * N_HEADS
P_COLS = 8 * WIDTH + 128
CB_FQ, CB_FK, CB_FV, CB_FZ, CB_GQ, CB_GK, CB_GV, CB_GZ, CB_SMALL = 0, 8, 16, 24, 32, 40, 48, 56, 64
IN_WIDTH = 8 * WIDTH + N_SMALL
N_CHIPS = 4
N_DEV = 8

ADAM_LR, ADAM_B1, ADAM_B2, ADAM_EPS, ADAM_WD, ADAM_STEP = 0.001, 0.9, 0.999, 1e-08, 0.01, 10


def _dotg(a, b, dims):
    return lax.dot_general(a.astype(MXU_DTYPE), b.astype(MXU_DTYPE), (dims, ((), ())), preferred_element_type=F32)


@jax.custom_vjp
def mm_nn(a, b):
    return _dotg(a, b, ((1,), (0,)))


@jax.custom_vjp
def mm_nt(a, b):
    return _dotg(a, b, ((1,), (1,)))


@jax.custom_vjp
def mm_tn(a, b):
    return _dotg(a, b, ((0,), (0,)))


mm_nn.defvjp(lambda a, b: (mm_nn(a, b), (a, b)), lambda r, g: (mm_nt(g, r[1]), mm_tn(r[0], g)))
mm_nt.defvjp(lambda a, b: (mm_nt(a, b), (a, b)), lambda r, g: (mm_nn(g, r[1]), mm_tn(g, r[0])))
mm_tn.defvjp(lambda a, b: (mm_tn(a, b), (a, b)), lambda r, g: (mm_nt(r[1], g), mm_nn(r[0], g)))


def hdot(a, b):
    return lax.dot_general(a, b, (((1,), (0,)), ((), ())), precision=HIGHEST, preferred_element_type=F32)


def _sigmoid(x):
    return 0.5 * (jnp.tanh(0.5 * x) + 1.0)


def _silu(x):
    return x * _sigmoid(x)


def _softplus(x):
    return jnp.maximum(x, 0.0) + jnp.log(1.0 + jnp.exp(-jnp.abs(x)))


def _log_sigmoid(x):
    return jnp.minimum(x, 0.0) - jnp.log(1.0 + jnp.exp(-jnp.abs(x)))


def _rms(x, g):
    return x * lax.rsqrt(jnp.mean(x * x, axis=-1, keepdims=True) + EPS) * g


def _l2n(x):
    return x * lax.rsqrt(jnp.sum(x * x, axis=-1, keepdims=True) + EPS)


def _tile(n, pref):
    t = min(n, pref)
    assert n % t == 0, (n, pref)
    return t


def _bspec(kind, w, col0, ts):
    if kind == "row":
        return pl.BlockSpec((ts, w), lambda i, h: (i, col0))
    if kind == "head":
        return pl.BlockSpec((ts, w), lambda i, h: (i, col0 + h))
    assert kind == "head3"
    return pl.BlockSpec((1, ts, w), lambda i, h: (h, i, 0))


def _ld(ref, kind):
    return ref[0] if kind == "head3" else ref[...]


def _st(ref, kind, val):
    if kind == "head3":
        ref[0] = val.astype(ref.dtype)
    else:
        ref[...] = val.astype(ref.dtype)


def _full_spec(a):
    nd = a.ndim
    return pl.BlockSpec(a.shape, lambda i, h: (0,) * nd)


def _out_struct(kind, w, S, width_total, dtype, nh):
    if kind == "head3":
        return jax.ShapeDtypeStruct((nh, S, w), dtype)
    return jax.ShapeDtypeStruct((S, width_total), dtype)


def rowwise_fwd(name, f, S, ts, nh, ins, params, outs):
    n_in, n_p = len(ins), len(params)
    into = {j: o[5] for j, o in enumerate(outs) if len(o) > 5}
    outs = [o[:5] for o in outs]
    n_al = len(into)

    def body(*refs):
        vals = [_ld(r, s[1]).astype(F32) for r, s in zip(refs[:n_in], ins)]
        pv = [r[...] for r in refs[n_in:n_in + n_p]]
        res = f(*vals, *pv)
        for r, s, o in zip(refs[n_in + n_p + n_al:], outs, res):
            _st(r, s[0], o)

    return pl.pallas_call(
        body, name=name, grid=(S // ts, nh),
        in_specs=[_bspec(k, w, c0, ts) for (_, k, w, c0) in ins] + [_full_spec(p) for p in params]
        + [pl.BlockSpec(memory_space=pl.ANY)] * n_al,
        out_specs=[_bspec(k, w, c0, ts) for (k, w, c0, _, _) in outs],
        out_shape=[jax.ShapeDtypeStruct(into[j].shape, into[j].dtype) if j in into else _out_struct(k, w, S, tw, dt, nh)
                   for j, (k, w, c0, tw, dt) in enumerate(outs)],
        input_output_aliases={n_in + n_p + n: j for n, j in enumerate(sorted(into))},
    )(*[a for (a, _, _, _) in ins], *params, *[into[j] for j in sorted(into)])


def rowwise_bwd(name, f, S, ts, nh, ins, params, cts, row_grads, extra=None, extra_outs=(), primal_sum=False):
    n_in, n_p, n_ct = len(ins), len(params), len(cts)
    n_rg, n_ex = len(row_grads), len(extra_outs)
    into = {j: rg[6] for j, rg in enumerate(row_grads) if len(rg) > 6}
    row_grads = [rg[:6] for rg in row_grads]
    n_al = len(into)

    def body(*refs):
        first = (pl.program_id(0) == 0) & (pl.program_id(1) == 0)
        in_refs = refs[:n_in]
        p_refs = refs[n_in:n_in + n_p]
        ct_refs = refs[n_in + n_p:n_in + n_p + n_ct]
        o = n_in + n_p + n_ct + n_al
        rg_refs = refs[o:o + n_rg]
        ex_refs = refs[o + n_rg:o + n_rg + n_ex]
        acc_refs = refs[o + n_rg + n_ex:]
        vals = [_ld(r, s[1]).astype(F32) for r, s in zip(in_refs, ins)]
        pv = [r[...] for r in p_refs]
        res, vjp = jax.vjp(f, *vals, *pv)
        if primal_sum:
            cv = [jnp.ones_like(res[0])] + [_ld(r, s[1]).astype(F32) for r, s in zip(ct_refs, cts)]
        else:
            cv = [_ld(r, s[1]).astype(F32) for r, s in zip(ct_refs, cts)]
        grads = vjp(tuple(cv))
        for r, s in zip(rg_refs, row_grads):
            _st(r, s[1], grads[s[0]])
        if extra is not None:
            for r, s, e in zip(ex_refs, extra_outs, extra(vals, cv, grads)):
                _st(r, s[0], e)

        @pl.when(first)
        def _():
            for r in acc_refs:
                r[...] = jnp.zeros_like(r)

        for r, g in zip(acc_refs[:n_p], grads[n_in:]):
            r[...] += g
        if primal_sum:
            acc_refs[n_p][...] += jnp.sum(res[0])

    acc_shapes = [jax.ShapeDtypeStruct(p.shape, F32) for p in params]
    acc_specs = [_full_spec(p) for p in params]
    if primal_sum:
        acc_shapes.append(jax.ShapeDtypeStruct((1, 128), F32))
        acc_specs.append(pl.BlockSpec((1, 128), lambda i, h: (0, 0)))
    rg_shapes = [jax.ShapeDtypeStruct(into[j].shape, into[j].dtype) if j in into else _out_struct(k, w, S, tw, dt, nh)
                 for j, (_, k, w, c0, tw, dt) in enumerate(row_grads)]
    first_alias = n_in + n_p + n_ct
    return pl.pallas_call(
        body, name=name, grid=(S // ts, nh),
        in_specs=[_bspec(k, w, c0, ts) for (_, k, w, c0) in ins] + [_full_spec(p) for p in params]
        + [_bspec(k, w, c0, ts) for (_, k, w, c0) in cts] + [pl.BlockSpec(memory_space=pl.ANY)] * n_al,
        out_specs=[_bspec(k, w, c0, ts) for (_, k, w, c0, _, _) in row_grads]
        + [_bspec(k, w, c0, ts) for (k, w, c0, _, _) in extra_outs] + acc_specs,
        out_shape=rg_shapes + [_out_struct(k, w, S, tw, dt, nh) for (k, w, c0, tw, dt) in extra_outs] + acc_shapes,
        input_output_aliases={first_alias + n: j for n, j in enumerate(sorted(into))},
    )(*[a for (a, _, _, _) in ins], *params, *[a for (a, _, _, _) in cts], *[into[j] for j in sorted(into)])


def f_prologue(x, ng, sc, sh):
    return (_rms(x, ng) * (1.0 + sc) + sh,)


def f_prologue_res(x, ng, sc, sh):
    return (_rms(x, ng) * (1.0 + sc) + sh, x)


def f_loss(x, mo, tgt, gate, fg):
    y = _rms(x + gate * mo, fg)
    return (0.5 * jnp.mean(jnp.square(y - tgt), axis=-1, keepdims=True),)


def f_foxpre(fq, fk, fv, qg, kg):
    return (_rms(fq, qg) * ATT_SCALE, _rms(fk, kg), fv)


def f_rms1(x, g):
    return (_rms(x, g),)


def f_postf(o, z):
    return (o * _silu(z),)


def f_postg(o, z, ng):
    return (_rms(o, ng) * _silu(z),)


def f_gdnpre(yq, yk, yv):
    return (_l2n(_silu(yq)) * ATT_SCALE, _l2n(_silu(yk)), _silu(yv))


def matmul(name, a, b, out_dtype, tm=512, tn=512, tk=2048, trans_a=False, trans_b=False):
    M, K = a.shape[::-1] if trans_a else a.shape
    N, K2 = b.shape if trans_b else b.shape[::-1]
    assert K == K2
    tm, tn, tk = _tile(M, tm), _tile(N, tn), _tile(K, tk)
    nk = K // tk
    a_spec = pl.BlockSpec((tk, tm), lambda i, j, k: (k, i)) if trans_a else pl.BlockSpec((tm, tk), lambda i, j, k: (i, k))
    b_spec = pl.BlockSpec((tn, tk), lambda i, j, k: (j, k)) if trans_b else pl.BlockSpec((tk, tn), lambda i, j, k: (k, j))

    def body(a_ref, b_ref, o_ref, *scr):
        part = lax.dot_general(a_ref[...], b_ref[...], (((0 if trans_a else 1,), (1 if trans_b else 0,)), ((), ())),
                               preferred_element_type=F32)
        if nk == 1:
            o_ref[...] = part.astype(o_ref.dtype)
            return
        acc = scr[0]
        k = pl.program_id(2)

        @pl.when(k == 0)
        def _():
            acc[...] = part

        @pl.when(k > 0)
        def _():
            acc[...] += part

        @pl.when(k == nk - 1)
        def _():
            o_ref[...] = acc[...].astype(o_ref.dtype)

    return pl.pallas_call(
        body, name=name, grid=(M // tm, N // tn, nk),
        in_specs=[a_spec, b_spec],
        out_specs=pl.BlockSpec((tm, tn), lambda i, j, k: (i, j)),
        out_shape=jax.ShapeDtypeStruct((M, N), out_dtype),
        scratch_shapes=[] if nk == 1 else [pltpu.VMEM((tm, tn), F32)],
        compiler_params=pltpu.CompilerParams(dimension_semantics=("parallel", "parallel", "arbitrary")),
    )(a, b)


def _small_f(z, pv, av):
    lane = lax.broadcasted_iota(jnp.int32, z.shape, 1)
    zz = z + pv
    logf = _log_sigmoid(zz)
    gval = -jnp.exp(av) * _softplus(zz)
    beta = _sigmoid(zz)
    return jnp.where(lane < 8, logf, jnp.where(lane < 16, gval, jnp.where(lane < 24, beta, 0.0)))


def _tri(ts, upper):
    r = lax.broadcasted_iota(jnp.int32, (ts, ts), 0)
    c = lax.broadcasted_iota(jnp.int32, (ts, ts), 1)
    keep = (r <= c) if upper else (r >= c)
    same_chunk = (r // CHUNK) == (c // CHUNK)
    return keep.astype(F32), (keep & same_chunk).astype(F32)


def smalls_fwd(p_all, pv, av, S, ts):
    nt = S // ts

    def body(z_ref, pv_ref, av_ref, o_ref, carry):
        i = pl.program_id(0)

        @pl.when(i == 0)
        def _():
            carry[...] = jnp.zeros_like(carry)

        e = _small_f(z_ref[...], pv_ref[...], av_ref[...])
        lane = lax.broadcasted_iota(jnp.int32, e.shape, 1)
        l_all, l_chunk = _tri(ts, upper=False)
        cum_all = hdot(l_all, e) + carry[...]
        cum_chunk = hdot(l_chunk, e)
        carry[...] = cum_all[ts - 1:ts, :]
        o_ref[...] = jnp.where(lane < 8, cum_all, jnp.where(lane < 16, cum_chunk, e))

    return pl.pallas_call(
        body, name="smalls_fwd", grid=(nt,),
        in_specs=[pl.BlockSpec((ts, 128), lambda i: (i, CB_SMALL)), pl.BlockSpec((1, 128), lambda i: (0, 0)),
                  pl.BlockSpec((1, 128), lambda i: (0, 0))],
        out_specs=pl.BlockSpec((ts, 128), lambda i: (i, 0)),
        out_shape=jax.ShapeDtypeStruct((S, 128), F32),
        scratch_shapes=[pltpu.VMEM((1, 128), F32)],
    )(p_all, pv, av)


def smalls_bwd(p_all, pv, av, dsm, dp, S, ts):
    nt = S // ts

    def body(z_ref, pv_ref, av_ref, d_ref, dp_in, dz_ref, dpv_ref, dav_ref, carry):
        i = pl.program_id(0)

        @pl.when(i == 0)
        def _():
            carry[...] = jnp.zeros_like(carry)
            dpv_ref[...] = jnp.zeros_like(dpv_ref)
            dav_ref[...] = jnp.zeros_like(dav_ref)

        d = d_ref[...]
        lane = lax.broadcasted_iota(jnp.int32, d.shape, 1)
        u_all, u_chunk = _tri(ts, upper=True)
        rev_all = hdot(u_all, d) + carry[...]
        rev_chunk = hdot(u_chunk, d)
        carry[...] = rev_all[0:1, :]
        de = jnp.where(lane < 8, rev_all, jnp.where(lane < 16, rev_chunk, d))
        _, vjp = jax.vjp(_small_f, z_ref[...], pv_ref[...], av_ref[...])
        dz, dpv, dav = vjp(de)
        dz_ref[...] = dz.astype(dz_ref.dtype)
        dpv_ref[...] += dpv
        dav_ref[...] += dav

    rev = lambda i: (nt - 1 - i, 0)
    small_cols = pl.BlockSpec((ts, 128), lambda i: (nt - 1 - i, CB_SMALL))
    return pl.pallas_call(
        body, name="smalls_bwd", grid=(nt,),
        in_specs=[small_cols, pl.BlockSpec((1, 128), lambda i: (0, 0)), pl.BlockSpec((1, 128), lambda i: (0, 0)),
                  pl.BlockSpec((ts, 128), rev), pl.BlockSpec(memory_space=pl.ANY)],
        out_specs=[small_cols, pl.BlockSpec((1, 128), lambda i: (0, 0)), pl.BlockSpec((1, 128), lambda i: (0, 0))],
        out_shape=[jax.ShapeDtypeStruct(dp.shape, dp.dtype), jax.ShapeDtypeStruct((1, 128), F32),
                   jax.ShapeDtypeStruct((1, 128), F32)],
        scratch_shapes=[pltpu.VMEM((1, 128), F32)],
        input_output_aliases={4: 0},
    )(p_all, pv, av, dsm, dp)


def _col_to_row(col):
    return jnp.transpose(jnp.broadcast_to(col, (col.shape[0], 128)))[0:1, :]


def _row_to_col(row):
    return jnp.transpose(jnp.broadcast_to(row, (128, row.shape[1])))[:, 0:1]


def _causal_mask(s, keys_first):
    r = lax.broadcasted_iota(jnp.int32, s.shape, 0)
    c = lax.broadcasted_iota(jnp.int32, s.shape, 1)
    return jnp.where((r <= c) if keys_first else (c <= r), s, NEG)


def flash_fwd(qs, kn, vb, frow, S, tq):
    nq = S // tq
    tk = tq

    def body(q_ref, k_ref, v_ref, fr_ref, o_ref, lse_ref):
        i = pl.program_id(1)
        q = q_ref[...]
        f_base = fr_ref[0, i][:, 0:1]

        def tile(j, carry, diagonal):
            m, l, acc = carry
            off = pl.multiple_of(j * tk, tk)
            k = k_ref[pl.ds(off, tk), :]
            v = v_ref[pl.ds(off, tk), :]
            s = mm_nt(q, k) - (fr_ref[0, j] - f_base)
            if diagonal:
                s = _causal_mask(s, keys_first=False)
            m_new = jnp.maximum(m, jnp.max(s, axis=-1, keepdims=True))
            a = jnp.exp(m - m_new)
            p = jnp.exp(s - m_new)
            l = a * l + jnp.sum(p, axis=-1, keepdims=True)
            acc = a * acc + mm_nn(p, v)
            return m_new, l, acc

        init = (jnp.full((tq, 1), NEG, F32), jnp.zeros((tq, 1), F32), jnp.zeros((tq, D_HEAD), F32))
        carry = lax.fori_loop(0, i, lambda j, c: tile(j, c, False), init)
        m, l, acc = tile(i, carry, True)
        o_ref[...] = acc / l
        lse_ref[0, 0] = _col_to_row(m + jnp.log(l))

    return pl.pallas_call(
        body, name="flash_fwd", grid=(N_HEADS, nq),
        in_specs=[pl.BlockSpec((tq, D_HEAD), lambda h, i: (i, h)), pl.BlockSpec((S, D_HEAD), lambda h, i: (0, h)),
                  pl.BlockSpec((S, D_HEAD), lambda h, i: (0, h)),
                  pl.BlockSpec((1, nq, 1, tk), lambda h, i: (h, 0, 0, 0))],
        out_specs=[pl.BlockSpec((tq, D_HEAD), lambda h, i: (i, h)),
                   pl.BlockSpec((1, 1, 1, tq), lambda h, i: (h, i, 0, 0))],
        out_shape=[jax.ShapeDtypeStruct((S, WIDTH), F32), jax.ShapeDtypeStruct((N_HEADS, nq, 1, tq), F32)],
    )(qs, kn, vb, frow)


def flash_bwd_dq(qs, kn, vb, do, o, frow, lse_row, S, tq):
    nq = S // tq
    tk = tq

    def body(q_ref, k_ref, v_ref, do_ref, o_ref, fr_ref, lse_ref, dq_ref, dfq_ref, dl_ref):
        i = pl.program_id(1)
        q = q_ref[...]
        do_ = do_ref[...]
        lse_ = _row_to_col(lse_ref[0, 0])
        dl = jnp.sum(do_.astype(F32) * o_ref[...], axis=-1, keepdims=True)
        dl_ref[0, 0] = _col_to_row(dl)
        f_base = fr_ref[0, i][:, 0:1]

        def tile(j, carry, diagonal):
            dq, dfq = carry
            off = pl.multiple_of(j * tk, tk)
            k = k_ref[pl.ds(off, tk), :]
            v = v_ref[pl.ds(off, tk), :]
            s = mm_nt(q, k) - (fr_ref[0, j] - f_base)
            if diagonal:
                s = _causal_mask(s, keys_first=False)
            p = jnp.exp(s - lse_)
            ds = p * (mm_nt(do_, v) - dl)
            return dq + mm_nn(ds, k), dfq + jnp.sum(ds, axis=-1, keepdims=True)

        carry = lax.fori_loop(0, i, lambda j, c: tile(j, c, False),
                              (jnp.zeros((tq, D_HEAD), F32), jnp.zeros((tq, 1), F32)))
        dq, dfq = tile(i, carry, True)
        dq_ref[...] = dq * ATT_SCALE
        dfq_ref[0, 0] = _col_to_row(dfq)

    row = pl.BlockSpec((1, 1, 1, tq), lambda h, i: (h, i, 0, 0))
    blk = pl.BlockSpec((tq, D_HEAD), lambda h, i: (i, h))
    full = pl.BlockSpec((S, D_HEAD), lambda h, i: (0, h))
    stat = jax.ShapeDtypeStruct((N_HEADS, nq, 1, tq), F32)
    return pl.pallas_call(
        body, name="flash_bwd_dq", grid=(N_HEADS, nq),
        in_specs=[blk, full, full, blk, blk, pl.BlockSpec((1, nq, 1, tk), lambda h, i: (h, 0, 0, 0)), row],
        out_specs=[blk, row, row],
        out_shape=[jax.ShapeDtypeStruct((S, WIDTH), F32), stat, stat],
    )(qs, kn, vb, do, o, frow, lse_row)


def flash_bwd_dkv(qs, kn, vb, do, frow, lse_row, delta_row, dp, S, tq):
    nq = S // tq
    tk = tq

    def body(q_ref, k_ref, v_ref, do_ref, fr_ref, lse_ref, dl_ref, dp_in, dk_ref, dv_ref, dfk_ref):
        j = pl.program_id(1)
        k = k_ref[...]
        v = v_ref[...]
        fk = _row_to_col(fr_ref[0, j])

        def tile(i, carry, diagonal):
            dk, dv, dfk = carry
            off = pl.multiple_of(i * tq, tq)
            q = q_ref[pl.ds(off, tq), :]
            do_ = do_ref[pl.ds(off, tq), :]
            st = mm_nt(k, q) - (fk - fr_ref[0, i][:, 0:1])
            if diagonal:
                st = _causal_mask(st, keys_first=True)
            pt = jnp.exp(st - lse_ref[0, i])
            dst = pt * (mm_nt(v, do_) - dl_ref[0, i])
            return dk + mm_nn(dst, q), dv + mm_nn(pt, do_), dfk - jnp.sum(dst, axis=-1, keepdims=True)

        z = jnp.zeros((tk, D_HEAD), F32)
        carry = tile(j, (z, z, jnp.zeros((tk, 1), F32)), True)
        dk, dv, dfk = lax.fori_loop(j + 1, nq, lambda i, c: tile(i, c, False), carry)
        dk_ref[...] = dk
        dv_ref[...] = dv.astype(dv_ref.dtype)
        dfk_ref[0, 0] = _col_to_row(dfk)

    blk = pl.BlockSpec((tk, D_HEAD), lambda h, j: (j, h))
    full = pl.BlockSpec((S, D_HEAD), lambda h, j: (0, h))
    rows = pl.BlockSpec((1, nq, 1, tq), lambda h, j: (h, 0, 0, 0))
    return pl.pallas_call(
        body, name="flash_bwd_dkv", grid=(N_HEADS, nq),
        in_specs=[full, blk, blk, full, rows, rows, rows, pl.BlockSpec(memory_space=pl.ANY)],
        out_specs=[blk, pl.BlockSpec((tk, D_HEAD), lambda h, j: (j, CB_FV + h)),
                   pl.BlockSpec((1, 1, 1, tk), lambda h, j: (h, j, 0, 0))],
        out_shape=[jax.ShapeDtypeStruct((S, WIDTH), F32), jax.ShapeDtypeStruct(dp.shape, dp.dtype),
                   jax.ShapeDtypeStruct((N_HEADS, nq, 1, tk), F32)],
        input_output_aliases={7: 1},
    )(qs, kn, vb, do, frow, lse_row, delta_row, dp)


CONV_COLS = 3 * WIDTH
CONV_TC = 512


def conv_fwd(p_all, conv_w, S, ts):
    nt, tc = S // ts, CONV_TC
    cb0 = CB_GQ * 128 // tc
    r8 = ts // 8

    def body(x_ref, halo_ref, w_ref, y_ref, scr):
        i = pl.program_id(1)
        scr[0:8, :] = jnp.where(i > 0, halo_ref[...], 0.0)
        scr[8:8 + ts, :] = x_ref[...]
        w = w_ref[...]
        y = w[3:4, :] * x_ref[...]
        for j in range(CONV_K - 1):
            y = y + w[j:j + 1, :] * scr[5 + j:5 + j + ts, :]
        y_ref[...] = y

    return pl.pallas_call(
        body, name="conv_fwd", grid=(CONV_COLS // tc, nt),
        in_specs=[pl.BlockSpec((ts, tc), lambda c, i: (i, cb0 + c)),
                  pl.BlockSpec((8, tc), lambda c, i: (jnp.maximum(i * r8 - 1, 0), cb0 + c)),
                  pl.BlockSpec((CONV_K, tc), lambda c, i: (0, c))],
        out_specs=pl.BlockSpec((ts, tc), lambda c, i: (i, c)),
        out_shape=jax.ShapeDtypeStruct((S, CONV_COLS), F32),
        scratch_shapes=[pltpu.VMEM((ts + 8, tc), F32)],
    )(p_all, p_all, conv_w)


def conv_bwd(p_all, dy, conv_w, dp, S, ts):
    nt, tc = S // ts, CONV_TC
    cb0 = CB_GQ * 128 // tc
    r8 = ts // 8

    def body(x_ref, xh_ref, dy_ref, dyh_ref, w_ref, dp_in, dx_ref, dw_ref, xs, ds):
        i = pl.program_id(1)
        xs[0:8, :] = jnp.where(i > 0, xh_ref[...], 0.0)
        xs[8:8 + ts, :] = x_ref[...]
        ds[0:ts, :] = dy_ref[...]
        ds[ts:ts + 8, :] = jnp.where(i < nt - 1, dyh_ref[...], 0.0)
        w = w_ref[...]
        d = dy_ref[...]
        dx = w[3:4, :] * d
        for j in range(CONV_K - 1):
            dx = dx + w[j:j + 1, :] * ds[3 - j:3 - j + ts, :]
        dx_ref[...] = dx.astype(dx_ref.dtype)

        @pl.when(i == 0)
        def _():
            dw_ref[...] = jnp.zeros_like(dw_ref)

        rows = [jnp.sum(d * xs[5 + j:5 + j + ts, :], axis=0, keepdims=True) for j in range(CONV_K - 1)]
        rows.append(jnp.sum(d * x_ref[...], axis=0, keepdims=True))
        dw_ref[...] += jnp.concatenate(rows, axis=0)

    return pl.pallas_call(
        body, name="conv_bwd", grid=(CONV_COLS // tc, nt),
        in_specs=[pl.BlockSpec((ts, tc), lambda c, i: (i, cb0 + c)),
                  pl.BlockSpec((8, tc), lambda c, i: (jnp.maximum(i * r8 - 1, 0), cb0 + c)),
                  pl.BlockSpec((ts, tc), lambda c, i: (i, c)),
                  pl.BlockSpec((8, tc), lambda c, i: (jnp.minimum((i + 1) * r8, nt * r8 - 1), c)),
                  pl.BlockSpec((CONV_K, tc), lambda c, i: (0, c)), pl.BlockSpec(memory_space=pl.ANY)],
        out_specs=[pl.BlockSpec((ts, tc), lambda c, i: (i, cb0 + c)), pl.BlockSpec((CONV_K, tc), lambda c, i: (0, c))],
        out_shape=[jax.ShapeDtypeStruct(dp.shape, dp.dtype), jax.ShapeDtypeStruct((CONV_K, CONV_COLS), F32)],
        scratch_shapes=[pltpu.VMEM((ts + 8, tc), F32), pltpu.VMEM((ts + 8, tc), F32)],
        input_output_aliases={5: 0},
    )(p_all, p_all, dy, dy, conv_w, dp)


def _bdot(a, b, ca, cb):
    return lax.dot_general(a.astype(MXU_DTYPE), b.astype(MXU_DTYPE), (((ca,), (cb,)), ((0,), (0,))),
                           preferred_element_type=F32)


@jax.custom_vjp
def bmm_nn(a, b):
    return _bdot(a, b, 2, 1)


@jax.custom_vjp
def bmm_nt(a, b):
    return _bdot(a, b, 2, 2)


@jax.custom_vjp
def bmm_tn(a, b):
    return _bdot(a, b, 1, 1)


bmm_nn.defvjp(lambda a, b: (bmm_nn(a, b), (a, b)), lambda r, g: (bmm_nt(g, r[1]), bmm_tn(r[0], g)))
bmm_nt.defvjp(lambda a, b: (bmm_nt(a, b), (a, b)), lambda r, g: (bmm_nn(g, r[1]), bmm_tn(g, r[0])))
bmm_tn.defvjp(lambda a, b: (bmm_tn(a, b), (a, b)), lambda r, g: (bmm_nt(r[1], g), bmm_nn(r[0], g)))


def bdot_hi(a, b, ca=2, cb=1):
    return lax.dot_general(a, b, (((ca,), (cb,)), ((0,), (0,))), precision=SOLVE_PRECISION,
                           preferred_element_type=F32)


def _tri_inv_product(m):
    ii = lax.broadcasted_iota(jnp.int32, m.shape, 1)
    jj = lax.broadcasted_iota(jnp.int32, m.shape, 2)
    t = (ii == jj).astype(F32) - m
    pw = bdot_hi(m, m)
    for _ in range(4):
        t = t + bdot_hi(t, pw)
        pw = bdot_hi(pw, pw)
    return t + bdot_hi(t, pw)


def _tri_inv_bwd(t, g):
    return -bdot_hi(bdot_hi(t, g, 1, 1), t, 2, 2)


@jax.custom_vjp
def tri_inv(m):
    return _tri_inv_product(m)


@jax.custom_vjp
def tri_inv_known(m, t):
    return t


tri_inv.defvjp(lambda m: (lambda t: (t, t))(_tri_inv_product(m)), lambda t, g: (_tri_inv_bwd(t, g),))
tri_inv_known.defvjp(lambda m, t: (t, t), lambda t, g: (_tri_inv_bwd(t, g), jnp.zeros_like(t)))


def chunk_fn(q, k, v, gc_col, gc_row, beta, state, t_known=None):
    shape = (N_HEADS, CHUNK, CHUNK)
    ii = lax.broadcasted_iota(jnp.int32, shape, 1)
    jj = lax.broadcasted_iota(jnp.int32, shape, 2)
    lower = ii >= jj
    strict = ii > jj
    decay = jnp.exp(jnp.where(lower, gc_col - gc_row, NEG))
    kb = k * beta
    vb = v * beta
    m = jnp.where(strict, bmm_nt(kb, k) * decay, 0.0)
    t = tri_inv(m) if t_known is None else tri_inv_known(m, t_known)
    u = bdot_hi(t, vb)
    w = bdot_hi(t, kb * jnp.exp(gc_col))
    attn = jnp.where(lower, bmm_nt(q, k) * decay, 0.0)
    v_new = u - bmm_nn(w, state)
    o = bmm_nn(q * jnp.exp(gc_col), state) + bmm_nn(attn, v_new)
    g_last = gc_col[:, CHUNK - 1:CHUNK, :]
    k_dec = k * jnp.exp(g_last - gc_col)
    new_state = state * jnp.exp(g_last) + bmm_tn(k_dec, v_new)
    return (o, new_state, t) if t_known is None else (o, new_state)


def _heads(ref):
    return jnp.stack([ref[:, h * D_HEAD:(h + 1) * D_HEAD] for h in range(N_HEADS)], axis=0)


def _head_cols(ref, lane0):
    return jnp.stack([ref[:, lane0 + h:lane0 + h + 1] for h in range(N_HEADS)], axis=0)


def _qkv_head_cols(h):
    return [slice(g * WIDTH + h * D_HEAD, g * WIDTH + (h + 1) * D_HEAD) for g in range(3)]


def gdn_pre_fwd(y, S, ts):
    def body(y_ref, o_ref):
        for h in range(N_HEADS):
            cols = _qkv_head_cols(h)
            for c, o in zip(cols, f_gdnpre(*[y_ref[:, c] for c in cols])):
                o_ref[:, c] = o

    spec = pl.BlockSpec((ts, CONV_COLS), lambda i: (i, 0))
    return pl.pallas_call(body, name="gdn_pre", grid=(S // ts,), in_specs=[spec], out_specs=spec,
                          out_shape=jax.ShapeDtypeStruct((S, CONV_COLS), F32))(y)


def gdn_pre_bwd(y, dqkv, S, ts):
    def body(y_ref, d_ref, o_ref):
        for h in range(N_HEADS):
            cols = _qkv_head_cols(h)
            _, vjp = jax.vjp(f_gdnpre, *[y_ref[:, c] for c in cols])
            for c, g in zip(cols, vjp(tuple(d_ref[:, c] for c in cols))):
                o_ref[:, c] = g

    spec = pl.BlockSpec((ts, CONV_COLS), lambda i: (i, 0))
    return pl.pallas_call(body, name="gdn_pre_bwd", grid=(S // ts,), in_specs=[spec, spec], out_specs=spec,
                          out_shape=jax.ShapeDtypeStruct((S, CONV_COLS), F32))(y, dqkv)


def gdn_fwd(qkv, smalls, gc_row, S):
    n = S // CHUNK

    def body(q_ref, k_ref, v_ref, sm_ref, gr_ref, o_ref, st_ref, t_ref, state):
        @pl.when(pl.program_id(0) == 0)
        def _():
            state[...] = jnp.zeros_like(state)

        s0 = state[...]
        st_ref[0] = s0
        o, s1, t = chunk_fn(_heads(q_ref), _heads(k_ref), _heads(v_ref), _head_cols(sm_ref, 8),
                            gr_ref[0][:, None, :], _head_cols(sm_ref, 16), s0)
        for h in range(N_HEADS):
            o_ref[:, h * D_HEAD:(h + 1) * D_HEAD] = o[h]
        state[...] = s1
        t_ref[0] = t

    return pl.pallas_call(
        body, name="gdn_chunk_fwd", grid=(n,),
        in_specs=[pl.BlockSpec((CHUNK, WIDTH), lambda i, g=g: (i, g)) for g in range(3)]
        + [pl.BlockSpec((CHUNK, 128), lambda i: (i, 0)), pl.BlockSpec((1, N_HEADS, CHUNK), lambda i: (i, 0, 0))],
        out_specs=[pl.BlockSpec((CHUNK, WIDTH), lambda i: (i, 0)),
                   pl.BlockSpec((1, N_HEADS, D_HEAD, D_HEAD), lambda i: (i, 0, 0, 0)),
                   pl.BlockSpec((1, N_HEADS, CHUNK, CHUNK), lambda i: (i, 0, 0, 0))],
        out_shape=[jax.ShapeDtypeStruct((S, WIDTH), F32), jax.ShapeDtypeStruct((n, N_HEADS, D_HEAD, D_HEAD), F32),
                   jax.ShapeDtypeStruct((n, N_HEADS, CHUNK, CHUNK), F32)],
        scratch_shapes=[pltpu.VMEM((N_HEADS, D_HEAD, D_HEAD), F32)],
    )(qkv, qkv, qkv, smalls, gc_row)


def gdn_bwd(qkv, smalls, gc_row, states, tinv, do, S):
    n = S // CHUNK

    def body(q_ref, k_ref, v_ref, sm_ref, gr_ref, st_ref, t_ref, do_ref, dqkv_ref, dsm_ref, dgr_ref, dstate):
        @pl.when(pl.program_id(0) == 0)
        def _():
            dstate[...] = jnp.zeros_like(dstate)

        t_saved = t_ref[0]
        _, vjp = jax.vjp(lambda *a: chunk_fn(*a, t_known=t_saved), _heads(q_ref), _heads(k_ref), _heads(v_ref),
                         _head_cols(sm_ref, 8), gr_ref[0][:, None, :], _head_cols(sm_ref, 16), st_ref[0])
        dq, dk, dv, dgc, dgr, dbt, ds = vjp((_heads(do_ref), dstate[...]))
        for h in range(N_HEADS):
            dqkv_ref[:, h * D_HEAD:(h + 1) * D_HEAD] = dq[h]
            dqkv_ref[:, WIDTH + h * D_HEAD:WIDTH + (h + 1) * D_HEAD] = dk[h]
            dqkv_ref[:, 2 * WIDTH + h * D_HEAD:2 * WIDTH + (h + 1) * D_HEAD] = dv[h]
        dstate[...] = ds
        cols = [dgc[h] for h in range(N_HEADS)] + [dbt[h] for h in range(N_HEADS)]
        dsm_ref[...] = jnp.concatenate([jnp.zeros((CHUNK, 8), F32)] + cols + [jnp.zeros((CHUNK, 128 - 24), F32)],
                                       axis=1)
        dgr_ref[0] = jnp.concatenate([dgr[h] for h in range(N_HEADS)], axis=0)

    rev = lambda c: (lambda i: (n - 1 - i, c))
    return pl.pallas_call(
        body, name="gdn_chunk_bwd", grid=(n,),
        in_specs=[pl.BlockSpec((CHUNK, WIDTH), rev(g)) for g in range(3)]
        + [pl.BlockSpec((CHUNK, 128), rev(0)), pl.BlockSpec((1, N_HEADS, CHUNK), lambda i: (n - 1 - i, 0, 0)),
           pl.BlockSpec((1, N_HEADS, D_HEAD, D_HEAD), lambda i: (n - 1 - i, 0, 0, 0)),
           pl.BlockSpec((1, N_HEADS, CHUNK, CHUNK), lambda i: (n - 1 - i, 0, 0, 0)),
           pl.BlockSpec((CHUNK, WIDTH), rev(0))],
        out_specs=[pl.BlockSpec((CHUNK, 3 * WIDTH), rev(0)), pl.BlockSpec((CHUNK, 128), rev(0)),
                   pl.BlockSpec((1, N_HEADS, CHUNK), lambda i: (n - 1 - i, 0, 0))],
        out_shape=[jax.ShapeDtypeStruct((S, 3 * WIDTH), F32)] + [jax.ShapeDtypeStruct((S, 128), F32),
                                                                  jax.ShapeDtypeStruct((n, N_HEADS, CHUNK), F32)],
        scratch_shapes=[pltpu.VMEM((N_HEADS, D_HEAD, D_HEAD), F32)],
    )(qkv, qkv, qkv, smalls, gc_row, states, tinv, do)


def _flip(a, bit):
    return 1 - a if bit else a


def allgather8(name, buf):
    R, W = buf.shape

    def body(x_ref, out_ref, send_sems, recv_sems, local_sem):
        x, y, c = lax.axis_index("x"), lax.axis_index("y"), lax.axis_index("c")

        def slot(px, py, pc):
            return out_ref.at[4 * px + 2 * py + pc]

        mine = pltpu.make_async_copy(x_ref, slot(x, y, c), local_sem)
        mine.start()
        sends = []
        for r in range(1, N_DEV):
            peer = (_flip(x, r & 4), _flip(y, r & 2), _flip(c, r & 1))
            cp = pltpu.make_async_remote_copy(src_ref=x_ref, dst_ref=slot(x, y, c), send_sem=send_sems.at[r - 1],
                                              recv_sem=recv_sems.at[r - 1], device_id=peer, device_id_type=MESH_ID)
            cp.start()
            sends.append(cp)
        for r in range(1, N_DEV):
            peer = (_flip(x, r & 4), _flip(y, r & 2), _flip(c, r & 1))
            pltpu.make_async_remote_copy(src_ref=x_ref, dst_ref=slot(*peer), send_sem=send_sems.at[r - 1],
                                         recv_sem=recv_sems.at[r - 1], device_id=peer,
                                         device_id_type=MESH_ID).wait_recv()
        for cp in sends:
            cp.wait_send()
        mine.wait()

    return pl.pallas_call(
        body, name=name,
        out_shape=jax.ShapeDtypeStruct((N_DEV, R, W), buf.dtype),
        in_specs=[pl.BlockSpec(memory_space=pltpu.VMEM)],
        out_specs=pl.BlockSpec(memory_space=pltpu.VMEM),
        scratch_shapes=[pltpu.SemaphoreType.DMA((N_DEV - 1,)), pltpu.SemaphoreType.DMA((N_DEV - 1,)),
                        pltpu.SemaphoreType.DMA],
    )(buf)


def allgather_chips(name, blk):
    R, W = blk.shape
    half = R // 2
    n_far = N_CHIPS - 1

    def body(x_ref, out_ref, send_sems, recv_sems, local_sem):
        x, y, c = lax.axis_index("x"), lax.axis_index("y"), lax.axis_index("c")
        sibling = (x, y, 1 - c)
        chips = [(_flip(x, r & 2), _flip(y, r & 1)) for r in range(1, N_CHIPS)]

        def rows(px, py, pc):
            return out_ref.at[2 * px + py, pc]

        def copy(k, dst, to, src):
            return pltpu.make_async_remote_copy(src_ref=src, dst_ref=dst, send_sem=send_sems.at[k],
                                                recv_sem=recv_sems.at[k], device_id=to, device_id_type=MESH_ID)

        mine = pltpu.make_async_copy(x_ref, out_ref.at[2 * x + y], local_sem)
        mine.start()
        my_half = x_ref.at[c]
        first = [copy(j, rows(x, y, c), (*chip, c), my_half) for j, chip in enumerate(chips)]
        for cp in first:
            cp.start()
        passed = [copy(n_far + j, rows(*chip, c), sibling, rows(*chip, c)) for j, chip in enumerate(chips)]
        for j, chip in enumerate(chips):
            copy(j, rows(*chip, c), (*chip, c), my_half).wait_recv()
            passed[j].start()
        for j, chip in enumerate(chips):
            copy(n_far + j, rows(*chip, 1 - c), sibling, my_half).wait_recv()
        for cp in first + passed:
            cp.wait_send()
        mine.wait()

    return pl.pallas_call(
        body, name=name,
        out_shape=jax.ShapeDtypeStruct((N_CHIPS, 2, half, W), blk.dtype),
        in_specs=[pl.BlockSpec(memory_space=pl.ANY)],
        out_specs=pl.BlockSpec(memory_space=pl.ANY),
        scratch_shapes=[pltpu.SemaphoreType.DMA((2 * n_far,)), pltpu.SemaphoreType.DMA((2 * n_far,)),
                        pltpu.SemaphoreType.DMA],
    )(blk.reshape(2, half, W)).reshape(N_CHIPS, R, W)


def scatter_chips(name, parts):
    def body(x_ref, out_ref, send_sems, recv_sems):
        x, y, c = lax.axis_index("x"), lax.axis_index("y"), lax.axis_index("c")
        sends = []
        for r in range(1, N_CHIPS):
            peer = (_flip(x, r & 2), _flip(y, r & 1), c)
            cp = pltpu.make_async_remote_copy(src_ref=x_ref.at[2 * peer[0] + peer[1]], dst_ref=out_ref.at[r - 1],
                                              send_sem=send_sems.at[r - 1], recv_sem=recv_sems.at[r - 1],
                                              device_id=peer, device_id_type=MESH_ID)
            cp.start()
            sends.append(cp)
        for cp in sends:
            cp.wait_recv()
        for cp in sends:
            cp.wait_send()

    return pl.pallas_call(
        body, name=name,
        out_shape=jax.ShapeDtypeStruct((N_CHIPS - 1,) + parts.shape[1:], parts.dtype),
        in_specs=[pl.BlockSpec(memory_space=pl.ANY)],
        out_specs=pl.BlockSpec(memory_space=pl.ANY),
        scratch_shapes=[pltpu.SemaphoreType.DMA((N_CHIPS - 1,)), pltpu.SemaphoreType.DMA((N_CHIPS - 1,))],
    )(parts)


def sibling_send_other_half(name, g4):
    n, R, W = g4.shape
    half = R // 2

    def body(x_ref, out_ref, send_sem, recv_sem):
        x, y, c = lax.axis_index("x"), lax.axis_index("y"), lax.axis_index("c")
        cp = pltpu.make_async_remote_copy(src_ref=x_ref.at[:, pl.ds((1 - c) * half, half), :], dst_ref=out_ref,
                                          send_sem=send_sem, recv_sem=recv_sem, device_id=(x, y, 1 - c),
                                          device_id_type=MESH_ID)
        cp.start()
        cp.wait_recv()
        cp.wait_send()

    return pl.pallas_call(
        body, name=name,
        out_shape=jax.ShapeDtypeStruct((n, half, W), g4.dtype),
        in_specs=[pl.BlockSpec(memory_space=pl.ANY)],
        out_specs=pl.BlockSpec(memory_space=pl.ANY),
        scratch_shapes=[pltpu.SemaphoreType.DMA, pltpu.SemaphoreType.DMA],
    )(g4)


def sibling_merge(name, mine):
    half, W = mine.shape

    def body(x_ref, out_ref, send_sem, recv_sem, local_sem):
        x, y, c = lax.axis_index("x"), lax.axis_index("y"), lax.axis_index("c")
        keep = pltpu.make_async_copy(x_ref, out_ref.at[c], local_sem)
        keep.start()
        cp = pltpu.make_async_remote_copy(src_ref=x_ref, dst_ref=out_ref.at[c], send_sem=send_sem, recv_sem=recv_sem,
                                          device_id=(x, y, 1 - c), device_id_type=MESH_ID)
        cp.start()
        pltpu.make_async_remote_copy(src_ref=x_ref, dst_ref=out_ref.at[1 - c], send_sem=send_sem, recv_sem=recv_sem,
                                     device_id=(x, y, 1 - c), device_id_type=MESH_ID).wait_recv()
        cp.wait_send()
        keep.wait()

    return pl.pallas_call(
        body, name=name,
        out_shape=jax.ShapeDtypeStruct((2, half, W), mine.dtype),
        in_specs=[pl.BlockSpec(memory_space=pl.ANY)],
        out_specs=pl.BlockSpec(memory_space=pl.ANY),
        scratch_shapes=[pltpu.SemaphoreType.DMA, pltpu.SemaphoreType.DMA, pltpu.SemaphoreType.DMA],
    )(mine).reshape(2 * half, W)


def sum_parts(name, own, recv, tr):
    R, W = own.shape
    tr = _tile(R, tr)

    def body(o_ref, r_ref, out_ref):
        acc = o_ref[...]
        for k in range(recv.shape[0]):
            acc = acc + r_ref[k].astype(F32)
        out_ref[...] = acc

    return pl.pallas_call(
        body, name=name, grid=(R // tr,),
        in_specs=[pl.BlockSpec((tr, W), lambda i: (i, 0)), pl.BlockSpec((recv.shape[0], tr, W), lambda i: (0, i, 0))],
        out_specs=pl.BlockSpec((tr, W), lambda i: (i, 0)),
        out_shape=jax.ShapeDtypeStruct((R, W), F32),
    )(own, recv)


def sum_own_half(name, g4, recv, c_idx, tr):
    n, R, W = g4.shape
    half = R // 2
    tr = _tile(half, tr)
    nb = half // tr

    def body(c_ref, g_ref, r_ref, o_ref, ob_ref):
        s = g_ref[0] + r_ref[0]
        o_ref[0] = s
        ob_ref[0] = s.astype(ob_ref.dtype)

    mine = pl.BlockSpec((1, tr, W), lambda j, i, c: (j, i, 0))
    return pl.pallas_call(
        body, name=name,
        grid_spec=pltpu.PrefetchScalarGridSpec(
            num_scalar_prefetch=1, grid=(n, nb),
            in_specs=[pl.BlockSpec((1, tr, W), lambda j, i, c: (j, c[0] * nb + i, 0)), mine],
            out_specs=[mine, mine]),
        out_shape=[jax.ShapeDtypeStruct((n, half, W), F32), jax.ShapeDtypeStruct((n, half, W), MXU_DTYPE)],
    )(c_idx, g4, recv)


def sum_devices(name, g):
    def body(g_ref, o_ref):
        acc = g_ref[0]
        for d in range(1, N_DEV):
            acc = acc + g_ref[d]
        o_ref[...] = acc

    return pl.pallas_call(body, name=name, out_shape=jax.ShapeDtypeStruct(g.shape[1:], F32))(g)


def ada_fwd(c_all, w_ada):
    K, N = w_ada.shape
    tn = _tile(N, 512)

    def body(c_ref, w_ref, o_ref):
        o_ref[...] = mm_nn(_silu(c_ref[...]), w_ref[...])

    return pl.pallas_call(
        body, name="ada_fwd", grid=(N // tn,),
        in_specs=[pl.BlockSpec((N_DEV, K), lambda j: (0, 0)), pl.BlockSpec((K, tn), lambda j: (0, j))],
        out_specs=pl.BlockSpec((N_DEV, tn), lambda j: (0, j)),
        out_shape=jax.ShapeDtypeStruct((N_DEV, N), F32),
    )(c_all, w_ada)


def ada_grad(c_all, dmod):
    K = c_all.shape[1]
    N = dmod.shape[1]
    tn = _tile(N, 512)

    def body(c_ref, d_ref, o_ref):
        o_ref[...] = mm_tn(_silu(c_ref[...]), d_ref[...])

    return pl.pallas_call(
        body, name="ada_grad", grid=(N // tn,),
        in_specs=[pl.BlockSpec((N_DEV, K), lambda j: (0, 0)), pl.BlockSpec((N_DEV, tn), lambda j: (0, j))],
        out_specs=pl.BlockSpec((K, tn), lambda j: (0, j)),
        out_shape=jax.ShapeDtypeStruct((K, N), F32),
    )(c_all, dmod)


def adamw(name, w, gparts, m, v, tr):
    R, W = w.shape
    tr = _tile(R, tr)
    ng = len(gparts)

    def body(w_ref, *refs):
        g_refs, (m_ref, v_ref, g_out, d_out, m_out, v_out) = refs[:ng], refs[ng:]
        g = g_refs[0][...]
        for r in g_refs[1:]:
            g = g + r[...]
        m1 = ADAM_B1 * m_ref[...] + (1.0 - ADAM_B1) * g
        v1 = ADAM_B2 * v_ref[...] + (1.0 - ADAM_B2) * jnp.square(g)
        m_hat = m1 / (1.0 - ADAM_B1 ** ADAM_STEP)
        v_hat = v1 / (1.0 - ADAM_B2 ** ADAM_STEP)
        g_out[...] = g
        d_out[...] = -ADAM_LR * (m_hat / (jnp.sqrt(v_hat) + ADAM_EPS) + ADAM_WD * w_ref[...])
        m_out[...] = m1
        v_out[...] = v1

    spec = pl.BlockSpec((tr, W), lambda i: (i, 0))
    return pl.pallas_call(
        body, name=name, grid=(R // tr,),
        in_specs=[spec] * (3 + ng), out_specs=[spec] * 4,
        out_shape=[jax.ShapeDtypeStruct((R, W), F32)] * 4,
    )(w, *gparts, m, v)


_REF_SEGMENTS = ((0, 4 * WIDTH, 0), (4 * WIDTH, 4 * WIDTH + 8, 4 * WIDTH), (4 * WIDTH + 8, 8 * WIDTH + 8, -8),
                 (8 * WIDTH + 8, IN_WIDTH, 0))


def _internal_from_blocks(blocks, n):
    a = 4 * WIDTH
    pieces = []
    for lo, hi in ((0, a), (a + 8, 2 * a + 8), (a, a + 8), (2 * a + 8, IN_WIDTH)):
        for j in range(N_CHIPS):
            s, e = max(lo, j * n), min(hi, (j + 1) * n)
            if s < e:
                pieces.append(blocks[j][:, s - j * n:e - j * n])
    pieces.append(jnp.zeros(blocks.shape[1:2] + (128 - N_SMALL,), blocks.dtype))
    return jnp.concatenate(pieces, axis=1)


def _block_from_internal(g, j, n):
    pieces = []
    for lo, hi, shift in _REF_SEGMENTS:
        s, e = max(lo, j * n), min(hi, (j + 1) * n)
        if s < e:
            pieces.append(g[:, s + shift:e + shift])
    return jnp.concatenate(pieces, axis=1)


def _pad_lanes(v, n=128):
    return jnp.pad(v, ((0, 0), (0, n - v.shape[1])))


def kernel(x, c, norm_g, w_ada, b_ada, w_in, b_fgate, fox_qn_g, fox_kn_g, gdn_conv_w, gdn_A_log, gdn_dt_bias, gdn_norm_g, w_out, final_g, loss_target, m_norm_g, m_w_ada, m_b_ada, m_w_in, m_b_fgate, m_fox_qn_g, m_fox_kn_g, m_gdn_conv_w, m_gdn_A_log, m_gdn_dt_bias, m_gdn_norm_g, m_w_out, m_final_g, v_norm_g, v_w_ada, v_b_ada, v_w_in, v_b_fgate, v_fox_qn_g, v_fox_kn_g, v_gdn_conv_w, v_gdn_A_log, v_gdn_dt_bias, v_gdn_norm_g, v_w_out, v_final_g):
    S = x.shape[1]
    H = N_HEADS
    ix, iy, ic = lax.axis_index("x"), lax.axis_index("y"), lax.axis_index("c")
    chip = 2 * ix + iy
    me = 2 * chip + ic
    x2, tgt = x[0], loss_target[0]
    ts_wide = _tile(S, 256)
    ts = _tile(S, 512)
    tq = _tile(S, 512)
    nq = S // tq
    nch = S // CHUNK
    conv_cols_chip = CONV_COLS // N_CHIPS

    pay = jnp.concatenate([c, _pad_lanes(gdn_conv_w[0], D_MODEL), jnp.zeros((3, D_MODEL), F32)], axis=0)
    g1 = allgather8("ag_c", pay)
    c_all = g1[:, 0, :]
    conv_w = jnp.concatenate([g1[2 * j, 1:1 + CONV_K, :conv_cols_chip] for j in range(N_CHIPS)], axis=1)
    g2 = allgather8("ag_mod", ada_fwd(c_all, w_ada[0]))
    mod_rows = jnp.concatenate([g2[2 * j] for j in range(N_CHIPS)], axis=1)
    mod = lax.dynamic_slice_in_dim(mod_rows, me, 1, axis=0) + b_ada
    shift, scale, gate = mod[:, :D_MODEL], mod[:, D_MODEL:2 * D_MODEL], mod[:, 2 * D_MODEL:]

    n_in = w_in.shape[2]
    win4 = allgather_chips("ag_w_in", w_in[0].astype(MXU_DTYPE))
    w_all = _internal_from_blocks(win4, n_in)
    wout4 = allgather_chips("ag_w_out", w_out[0].astype(MXU_DTYPE))
    w_out_full = wout4.reshape(2 * WIDTH, D_MODEL)

    (h_b,) = rowwise_fwd("prologue", f_prologue, S, ts_wide, 1, [(x2, "row", D_MODEL, 0)], [norm_g, scale, shift],
                         [("row", D_MODEL, 0, D_MODEL, MXU_DTYPE)])
    p_all = matmul("in_proj", h_b, w_all, F32, tm=512, tn=640, tk=2048)
    pv = jnp.concatenate([b_fgate, gdn_dt_bias, jnp.zeros((1, 128 - 16), F32)], axis=1)
    av = jnp.concatenate([jnp.zeros((1, 8), F32), gdn_A_log, jnp.zeros((1, 128 - 16), F32)], axis=1)
    smalls = smalls_fwd(p_all, pv, av, S, ts)
    frow = jnp.transpose(smalls[:, :H]).reshape(H, nq, 1, tq)
    gc_row = jnp.transpose(smalls[:, 8:16].reshape(nch, CHUNK, H), (0, 2, 1))
    head_b = ("head", D_HEAD, 0, WIDTH, MXU_DTYPE)
    head_f = ("head", D_HEAD, 0, WIDTH, F32)
    pcol = lambda cb: (p_all, "head", D_HEAD, cb)
    qn, kn, vb = rowwise_fwd("fox_pre", f_foxpre, S, ts, H, [pcol(CB_FQ), pcol(CB_FK), pcol(CB_FV)],
                             [fox_qn_g, fox_kn_g], [head_b] * 3)
    o_fox, lse = flash_fwd(qn, kn, vb, frow, S, tq)
    y_conv = conv_fwd(p_all, conv_w, S, ts)
    qkv = gdn_pre_fwd(y_conv, S, ts_wide)
    o_gdn, states, tinv = gdn_fwd(qkv, smalls, gc_row, S)
    mixed = lax.empty((S, 2 * WIDTH), MXU_DTYPE)
    (mixed,) = rowwise_fwd("post_fox", f_postf, S, ts, H, [(o_fox, "head", D_HEAD, 0), pcol(CB_FZ)], [],
                           [("head", D_HEAD, 0, 2 * WIDTH, MXU_DTYPE, mixed)])
    (mixed,) = rowwise_fwd("post_gdn", f_postg, S, ts, H, [(o_gdn, "head", D_HEAD, 0), pcol(CB_GZ)], [gdn_norm_g],
                           [("head", D_HEAD, H, 2 * WIDTH, MXU_DTYPE, mixed)])
    mo = matmul("out_proj", mixed, w_out_full, F32)

    wide = lambda a: (a, "row", D_MODEL, 0)
    dx_res, dmo, d_gate, d_final_g, loss_acc = rowwise_bwd(
        "loss", f_loss, S, ts_wide, 1, [wide(x2), wide(mo), wide(tgt)], [gate, final_g[None, :]], [],
        [(0, "row", D_MODEL, 0, D_MODEL, F32), (1, "row", D_MODEL, 0, D_MODEL, MXU_DTYPE)], primal_sum=True)

    d_mixed = matmul("d_mixed", dmo, w_out_full, F32, trans_b=True)
    g_w_out = matmul("g_w_out", mixed, dmo, F32, trans_a=True)
    dp = lax.empty((S, P_COLS), MXU_DTYPE)
    into_dp = lambda idx, cb: (idx, "head", D_HEAD, cb, P_COLS, MXU_DTYPE, dp)
    do_fox, dp = rowwise_bwd(
        "post_fox_bwd", f_postf, S, ts, H, [(o_fox, "head", D_HEAD, 0), pcol(CB_FZ)], [],
        [(d_mixed, "head", D_HEAD, 0)], [(0,) + head_b, into_dp(1, CB_FZ)])
    do_gdn, dp, d_gdn_norm_g = rowwise_bwd(
        "post_gdn_bwd", f_postg, S, ts, H, [(o_gdn, "head", D_HEAD, 0), pcol(CB_GZ)], [gdn_norm_g],
        [(d_mixed, "head", D_HEAD, 8)], [(0,) + head_f, into_dp(1, CB_GZ)])
    d_qn, dfq_row, delta = flash_bwd_dq(qn, kn, vb, do_fox, o_fox, frow, lse, S, tq)
    d_kn, dp, dfk_row = flash_bwd_dkv(qn, kn, vb, do_fox, frow, lse, delta, dp, S, tq)
    dp, d_qn_g = rowwise_bwd("fox_q_bwd", f_rms1, S, ts, H, [pcol(CB_FQ)], [fox_qn_g], [(d_qn, "head", D_HEAD, 0)],
                             [into_dp(0, CB_FQ)])
    dp, d_kn_g = rowwise_bwd("fox_k_bwd", f_rms1, S, ts, H, [pcol(CB_FK)], [fox_kn_g], [(d_kn, "head", D_HEAD, 0)],
                             [into_dp(0, CB_FK)])
    d_qkv, dsm_chunk, dgc_row = gdn_bwd(qkv, smalls, gc_row, states, tinv, do_gdn, S)
    dp, d_conv_w = conv_bwd(p_all, gdn_pre_bwd(y_conv, d_qkv, S, ts_wide), conv_w, dp, S, ts)
    d_f = jnp.transpose((dfq_row + dfk_row).reshape(H, S))
    d_gc = jnp.transpose(dgc_row, (0, 2, 1)).reshape(S, H)
    dsm = dsm_chunk + jnp.concatenate([d_f, d_gc, jnp.zeros((S, 128 - 16), F32)], axis=1)
    dp, d_pv, d_av = smalls_bwd(p_all, pv, av, dsm, dp, S, ts)
    d_h = matmul("d_h", dp, w_all, F32, tm=512, tn=1024, tk=1664, trans_b=True)
    g_w_all = matmul("g_w_in", h_b, dp, F32, tm=512, tn=640, tk=2048, trans_a=True)
    d_x, d_norm_g, d_scale, d_shift = rowwise_bwd(
        "prologue_bwd", f_prologue_res, S, ts_wide, 1, [wide(x2)], [norm_g, scale, shift],
        [wide(d_h), wide(dx_res)], [(0, "row", D_MODEL, 0, D_MODEL, F32)])

    parts = [d_shift, d_scale, d_gate, d_norm_g, d_final_g, d_conv_w.reshape(1, CONV_K * CONV_COLS), d_qn_g, d_kn_g,
             d_gdn_norm_g, d_pv, d_av, loss_acc]
    n_pay = sum(p.shape[1] for p in parts)
    pay_w = -(-n_pay // 1024) * 1024
    pay2 = jnp.concatenate(parts + [jnp.zeros((1, pay_w - n_pay), F32)], axis=1).reshape(8, pay_w // 8)
    g3 = allgather8("ag_small", pay2)
    tot = sum_devices("sum_small", g3).reshape(1, pay_w)
    dmod_all = g3.reshape(N_DEV, pay_w)[:, :3 * D_MODEL]
    o = 3 * D_MODEL
    g_b_ada = tot[:, :o]
    g_norm_g, g_final_g = tot[:, o:o + D_MODEL], tot[:, o + D_MODEL:o + 2 * D_MODEL]
    o += 2 * D_MODEL
    g_conv_full = tot[:, o:o + CONV_K * CONV_COLS].reshape(CONV_K, CONV_COLS)
    g_conv = lax.dynamic_slice_in_dim(g_conv_full, chip * conv_cols_chip, conv_cols_chip, axis=1)
    o += CONV_K * CONV_COLS
    g_qn_g, g_kn_g, g_gdn_ng = tot[:, o:o + 128], tot[:, o + 128:o + 256], tot[:, o + 256:o + 384]
    g_pv, g_av = tot[:, o + 384:o + 512], tot[:, o + 512:o + 640]
    loss = tot[0, o + 640]
    g_b_fgate, g_dt_bias, g_a_log = g_pv[:, :8], g_pv[:, 8:16], g_av[:, 8:16]

    def small_vec(vals):
        flat = [norm for norm in vals[:3]] + [vals[3].reshape(1, CONV_K * conv_cols_chip)] + list(vals[4:7]) \
            + [_pad_lanes(s) for s in vals[7:]]
        n = sum(f.shape[1] for f in flat)
        nw = -(-n // 1024) * 1024
        return jnp.concatenate(flat + [jnp.zeros((1, nw - n), F32)], axis=1).reshape(8, nw // 8)

    sw = small_vec([norm_g, b_ada, final_g[None, :], gdn_conv_w[0], fox_qn_g, fox_kn_g, gdn_norm_g, b_fgate, gdn_A_log,
                    gdn_dt_bias])
    sm = small_vec([m_norm_g, m_b_ada, m_final_g[None, :], m_gdn_conv_w[0], m_fox_qn_g, m_fox_kn_g, m_gdn_norm_g,
                    m_b_fgate, m_gdn_A_log, m_gdn_dt_bias])
    sv = small_vec([v_norm_g, v_b_ada, v_final_g[None, :], v_gdn_conv_w[0], v_fox_qn_g, v_fox_kn_g, v_gdn_norm_g,
                    v_b_fgate, v_gdn_A_log, v_gdn_dt_bias])
    sg = small_vec([g_norm_g, g_b_ada, g_final_g, g_conv, g_qn_g, g_kn_g, g_gdn_ng, g_b_fgate, g_a_log, g_dt_bias])
    small_out = adamw("adamw_small", sw, [sg], sm, sv, 8)

    def split_small(a):
        a = a.reshape(1, -1)
        out, o = [], 0
        for n, shp in ((D_MODEL, (1, D_MODEL)), (3 * D_MODEL, (1, 3 * D_MODEL)), (D_MODEL, (D_MODEL,)),
                       (CONV_K * conv_cols_chip, (1, CONV_K, conv_cols_chip)), (128, (1, 128)), (128, (1, 128)),
                       (128, (1, 128))):
            out.append(a[:, o:o + n].reshape(shp))
            o += n
        for _ in range(3):
            out.append(a[:, o:o + 8])
            o += 128
        return out

    n_ada = w_ada.shape[2]
    g_w_ada = ada_grad(c_all, lax.dynamic_slice_in_dim(dmod_all, chip * n_ada, n_ada, axis=1))
    ada_out = adamw("adamw_w_ada", w_ada[0], [g_w_ada], m_w_ada[0], v_w_ada[0], 128)

    c_idx = jnp.reshape(ic, (1,)).astype(jnp.int32)

    def reduce_scatter(tag, g4, tr):
        recv = sibling_send_other_half("d2d_" + tag, g4)
        p4, p4_sent = sum_own_half("sum2_" + tag, g4, recv, c_idx, tr)
        far = scatter_chips("rs_" + tag, p4_sent)
        own = lax.dynamic_index_in_dim(p4, chip, axis=0, keepdims=False)
        return sibling_merge("merge_" + tag, sum_parts("sum_" + tag, own, far, tr))

    g4_in = jnp.stack([_block_from_internal(g_w_all, j, n_in) for j in range(N_CHIPS)])
    in_out = adamw("adamw_w_in", w_in[0], [reduce_scatter("w_in", g4_in, 128)], m_w_in[0], v_w_in[0], 128)
    g4_out = g_w_out.reshape(N_CHIPS, w_out.shape[1], D_MODEL)
    out_out = adamw("adamw_w_out", w_out[0], [reduce_scatter("w_out", g4_out, 128)], m_w_out[0], v_w_out[0], 128)

    res = [loss, d_x[None]]
    for k in range(4):
        s = split_small(small_out[k])
        big = [ada_out[k][None], in_out[k][None], out_out[k][None]]
        res += [s[0], big[0], s[1], big[1], s[7], s[4], s[5], s[3], s[8], s[9], s[6], big[2], s[2]]
    return tuple(res)
```

```python
import functools

import jax
import jax.numpy as jnp
from jax import lax
from jax.experimental import pallas as pl
from jax.experimental.pallas import tpu as pltpu

F32 = jnp.float32
MXU_DTYPE = jnp.bfloat16
HIGHEST = lax.Precision.HIGHEST
SOLVE_PRECISION = lax.Precision.HIGH
MESH_ID = pl.DeviceIdType.MESH

D_MODEL = 2048
N_HEADS = 8
D_HEAD = 128
WIDTH = N_HEADS * D_HEAD
CHUNK = 64
CONV_K = 4
EPS = 1e-6
NEG = -1e30
ATT_SCALE = D_HEAD ** -0.5
N_SMALL = 3 * N_HEADS
P_COLS = 8 * WIDTH + 128
CB_FQ, CB_FK, CB_FV, CB_FZ, CB_GQ, CB_GK, CB_GV, CB_GZ, CB_SMALL = 0, 8, 16, 24, 32, 40, 48, 56, 64
IN_WIDTH = 8 * WIDTH + N_SMALL
N_CHIPS = 4
N_DEV = 8

ADAM_LR, ADAM_B1, ADAM_B2, ADAM_EPS, ADAM_WD, ADAM_STEP = 0.001, 0.9, 0.999, 1e-08, 0.01, 10


def _dotg(a, b, dims):
    return lax.dot_general(a.astype(MXU_DTYPE), b.astype(MXU_DTYPE), (dims, ((), ())), preferred_element_type=F32)


@jax.custom_vjp
def mm_nn(a, b):
    return _dotg(a, b, ((1,), (0,)))


@jax.custom_vjp
def mm_nt(a, b):
    return _dotg(a, b, ((1,), (1,)))


@jax.custom_vjp
def mm_tn(a, b):
    return _dotg(a, b, ((0,), (0,)))


mm_nn.defvjp(lambda a, b: (mm_nn(a, b), (a, b)), lambda r, g: (mm_nt(g, r[1]), mm_tn(r[0], g)))
mm_nt.defvjp(lambda a, b: (mm_nt(a, b), (a, b)), lambda r, g: (mm_nn(g, r[1]), mm_tn(g, r[0])))
mm_tn.defvjp(lambda a, b: (mm_tn(a, b), (a, b)), lambda r, g: (mm_nt(r[1], g), mm_nn(r[0], g)))


def hdot(a, b):
    return lax.dot_general(a, b, (((1,), (0,)), ((), ())), precision=HIGHEST, preferred_element_type=F32)


def _sigmoid(x):
    return 0.5 * (jnp.tanh(0.5 * x) + 1.0)


def _silu(x):
    return x * _sigmoid(x)


def _softplus(x):
    return jnp.maximum(x, 0.0) + jnp.log(1.0 + jnp.exp(-jnp.abs(x)))


def _log_sigmoid(x):
    return jnp.minimum(x, 0.0) - jnp.log(1.0 + jnp.exp(-jnp.abs(x)))


def _rms(x, g):
    return x * lax.rsqrt(jnp.mean(x * x, axis=-1, keepdims=True) + EPS) * g


def _l2n(x):
    return x * lax.rsqrt(jnp.sum(x * x, axis=-1, keepdims=True) + EPS)


def _tile(n, pref):
    t = min(n, pref)
    assert n % t == 0, (n, pref)
    return t


def _bspec(kind, w, col0, ts):
    if kind == "row":
        return pl.BlockSpec((ts, w), lambda i, h: (i, col0))
    if kind == "head":
        return pl.BlockSpec((ts, w), lambda i, h: (i, col0 + h))
    assert kind == "head3"
    return pl.BlockSpec((1, ts, w), lambda i, h: (h, i, 0))


def _ld(ref, kind):
    return ref[0] if kind == "head3" else ref[...]


def _st(ref, kind, val):
    if kind == "head3":
        ref[0] = val.astype(ref.dtype)
    else:
        ref[...] = val.astype(ref.dtype)


def _full_spec(a):
    nd = a.ndim
    return pl.BlockSpec(a.shape, lambda i, h: (0,) * nd)


def _out_struct(kind, w, S, width_total, dtype, nh):
    if kind == "head3":
        return jax.ShapeDtypeStruct((nh, S, w), dtype)
    return jax.ShapeDtypeStruct((S, width_total), dtype)


def rowwise_fwd(name, f, S, ts, nh, ins, params, outs):
    n_in, n_p = len(ins), len(params)
    into = {j: o[5] for j, o in enumerate(outs) if len(o) > 5}
    outs = [o[:5] for o in outs]
    n_al = len(into)

    def body(*refs):
        vals = [_ld(r, s[1]).astype(F32) for r, s in zip(refs[:n_in], ins)]
        pv = [r[...] for r in refs[n_in:n_in + n_p]]
        res = f(*vals, *pv)
        for r, s, o in zip(refs[n_in + n_p + n_al:], outs, res):
            _st(r, s[0], o)

    return pl.pallas_call(
        body, name=name, grid=(S // ts, nh),
        in_specs=[_bspec(k, w, c0, ts) for (_, k, w, c0) in ins] + [_full_spec(p) for p in params]
        + [pl.BlockSpec(memory_space=pl.ANY)] * n_al,
        out_specs=[_bspec(k, w, c0, ts) for (k, w, c0, _, _) in outs],
        out_shape=[jax.ShapeDtypeStruct(into[j].shape, into[j].dtype) if j in into else _out_struct(k, w, S, tw, dt, nh)
                   for j, (k, w, c0, tw, dt) in enumerate(outs)],
        input_output_aliases={n_in + n_p + n: j for n, j in enumerate(sorted(into))},
    )(*[a for (a, _, _, _) in ins], *params, *[into[j] for j in sorted(into)])


def rowwise_bwd(name, f, S, ts, nh, ins, params, cts, row_grads, extra=None, extra_outs=(), primal_sum=False):
    n_in, n_p, n_ct = len(ins), len(params), len(cts)
    n_rg, n_ex = len(row_grads), len(extra_outs)
    into = {j: rg[6] for j, rg in enumerate(row_grads) if len(rg) > 6}
    row_grads = [rg[:6] for rg in row_grads]
    n_al = len(into)

    def body(*refs):
        first = (pl.program_id(0) == 0) & (pl.program_id(1) == 0)
        in_refs = refs[:n_in]
        p_refs = refs[n_in:n_in + n_p]
        ct_refs = refs[n_in + n_p:n_in + n_p + n_ct]
        o = n_in + n_p + n_ct + n_al
        rg_refs = refs[o:o + n_rg]
        ex_refs = refs[o + n_rg:o + n_rg + n_ex]
        acc_refs = refs[o + n_rg + n_ex:]
        vals = [_ld(r, s[1]).astype(F32) for r, s in zip(in_refs, ins)]
        pv = [r[...] for r in p_refs]
        res, vjp = jax.vjp(f, *vals, *pv)
        if primal_sum:
            cv = [jnp.ones_like(res[0])] + [_ld(r, s[1]).astype(F32) for r, s in zip(ct_refs, cts)]
        else:
            cv = [_ld(r, s[1]).astype(F32) for r, s in zip(ct_refs, cts)]
        grads = vjp(tuple(cv))
        for r, s in zip(rg_refs, row_grads):
            _st(r, s[1], grads[s[0]])
        if extra is not None:
            for r, s, e in zip(ex_refs, extra_outs, extra(vals, cv, grads)):
                _st(r, s[0], e)

        @pl.when(first)
        def _():
            for r in acc_refs:
                r[...] = jnp.zeros_like(r)

        for r, g in zip(acc_refs[:n_p], grads[n_in:]):
            r[...] += g
        if primal_sum:
            acc_refs[n_p][...] += jnp.sum(res[0])

    acc_shapes = [jax.ShapeDtypeStruct(p.shape, F32) for p in params]
    acc_specs = [_full_spec(p) for p in params]
    if primal_sum:
        acc_shapes.append(jax.ShapeDtypeStruct((1, 128), F32))
        acc_specs.append(pl.BlockSpec((1, 128), lambda i, h: (0, 0)))
    rg_shapes = [jax.ShapeDtypeStruct(into[j].shape, into[j].dtype) if j in into else _out_struct(k, w, S, tw, dt, nh)
                 for j, (_, k, w, c0, tw, dt) in enumerate(row_grads)]
    first_alias = n_in + n_p + n_ct
    return pl.pallas_call(
        body, name=name, grid=(S // ts, nh),
        in_specs=[_bspec(k, w, c0, ts) for (_, k, w, c0) in ins] + [_full_spec(p) for p in params]
        + [_bspec(k, w, c0, ts) for (_, k, w, c0) in cts] + [pl.BlockSpec(memory_space=pl.ANY)] * n_al,
        out_specs=[_bspec(k, w, c0, ts) for (_, k, w, c0, _, _) in row_grads]
        + [_bspec(k, w, c0, ts) for (k, w, c0, _, _) in extra_outs] + acc_specs,
        out_shape=rg_shapes + [_out_struct(k, w, S, tw, dt, nh) for (k, w, c0, tw, dt) in extra_outs] + acc_shapes,
        input_output_aliases={first_alias + n: j for n, j in enumerate(sorted(into))},
    )(*[a for (a, _, _, _) in ins], *params, *[a for (a, _, _, _) in cts], *[into[j] for j in sorted(into)])


def f_prologue(x, ng, sc, sh):
    return (_rms(x, ng) * (1.0 + sc) + sh,)


def f_prologue_res(x, ng, sc, sh):
    return (_rms(x, ng) * (1.0 + sc) + sh, x)


def f_loss(x, mo, tgt, gate, fg):
    y = _rms(x + gate * mo, fg)
    return (0.5 * jnp.mean(jnp.square(y - tgt), axis=-1, keepdims=True),)


def f_foxpre(fq, fk, fv, qg, kg):
    return (_rms(fq, qg) * ATT_SCALE, _rms(fk, kg), fv)


def f_rms1(x, g):
    return (_rms(x, g),)


def f_postf(o, z):
    return (o * _silu(z),)


def f_postg(o, z, ng):
    return (_rms(o, ng) * _silu(z),)


def f_gdnpre(yq, yk, yv):
    return (_l2n(_silu(yq)) * ATT_SCALE, _l2n(_silu(yk)), _silu(yv))


def matmul(name, a, b, out_dtype, tm=512, tn=512, tk=2048, trans_a=False, trans_b=False):
    M, K = a.shape[::-1] if trans_a else a.shape
    N, K2 = b.shape if trans_b else b.shape[::-1]
    assert K == K2
    tm, tn, tk = _tile(M, tm), _tile(N, tn), _tile(K, tk)
    nk = K // tk
    a_spec = pl.BlockSpec((tk, tm), lambda i, j, k: (k, i)) if trans_a else pl.BlockSpec((tm, tk), lambda i, j, k: (i, k))
    b_spec = pl.BlockSpec((tn, tk), lambda i, j, k: (j, k)) if trans_b else pl.BlockSpec((tk, tn), lambda i, j, k: (k, j))

    def body(a_ref, b_ref, o_ref, *scr):
        part = lax.dot_general(a_ref[...], b_ref[...], (((0 if trans_a else 1,), (1 if trans_b else 0,)), ((), ())),
                               preferred_element_type=F32)
        if nk == 1:
            o_ref[...] = part.astype(o_ref.dtype)
            return
        acc = scr[0]
        k = pl.program_id(2)

        @pl.when(k == 0)
        def _():
            acc[...] = part

        @pl.when(k > 0)
        def _():
            acc[...] += part

        @pl.when(k == nk - 1)
        def _():
            o_ref[...] = acc[...].astype(o_ref.dtype)

    return pl.pallas_call(
        body, name=name, grid=(M // tm, N // tn, nk),
        in_specs=[a_spec, b_spec],
        out_specs=pl.BlockSpec((tm, tn), lambda i, j, k: (i, j)),
        out_shape=jax.ShapeDtypeStruct((M, N), out_dtype),
        scratch_shapes=[] if nk == 1 else [pltpu.VMEM((tm, tn), F32)],
        compiler_params=pltpu.CompilerParams(dimension_semantics=("parallel", "parallel", "arbitrary")),
    )(a, b)


def _small_f(z, pv, av):
    lane = lax.broadcasted_iota(jnp.int32, z.shape, 1)
    zz = z + pv
    logf = _log_sigmoid(zz)
    gval = -jnp.exp(av) * _softplus(zz)
    beta = _sigmoid(zz)
    return jnp.where(lane < 8, logf, jnp.where(lane < 16, gval, jnp.where(lane < 24, beta, 0.0)))


def _tri(ts, upper):
    r = lax.broadcasted_iota(jnp.int32, (ts, ts), 0)
    c = lax.broadcasted_iota(jnp.int32, (ts, ts), 1)
    keep = (r <= c) if upper else (r >= c)
    same_chunk = (r // CHUNK) == (c // CHUNK)
    return keep.astype(F32), (keep & same_chunk).astype(F32)


def smalls_fwd(p_all, pv, av, S, ts):
    nt = S // ts

    def body(z_ref, pv_ref, av_ref, o_ref, carry):
        i = pl.program_id(0)

        @pl.when(i == 0)
        def _():
            carry[...] = jnp.zeros_like(carry)

        e = _small_f(z_ref[...], pv_ref[...], av_ref[...])
        lane = lax.broadcasted_iota(jnp.int32, e.shape, 1)
        l_all, l_chunk = _tri(ts, upper=False)
        cum_all = hdot(l_all, e) + carry[...]
        cum_chunk = hdot(l_chunk, e)
        carry[...] = cum_all[ts - 1:ts, :]
        o_ref[...] = jnp.where(lane < 8, cum_all, jnp.where(lane < 16, cum_chunk, e))

    return pl.pallas_call(
        body, name="smalls_fwd", grid=(nt,),
        in_specs=[pl.BlockSpec((ts, 128), lambda i: (i, CB_SMALL)), pl.BlockSpec((1, 128), lambda i: (0, 0)),
                  pl.BlockSpec((1, 128), lambda i: (0, 0))],
        out_specs=pl.BlockSpec((ts, 128), lambda i: (i, 0)),
        out_shape=jax.ShapeDtypeStruct((S, 128), F32),
        scratch_shapes=[pltpu.VMEM((1, 128), F32)],
    )(p_all, pv, av)


def smalls_bwd(p_all, pv, av, dsm, dp, S, ts):
    nt = S // ts

    def body(z_ref, pv_ref, av_ref, d_ref, dp_in, dz_ref, dpv_ref, dav_ref, carry):
        i = pl.program_id(0)

        @pl.when(i == 0)
        def _():
            carry[...] = jnp.zeros_like(carry)
            dpv_ref[...] = jnp.zeros_like(dpv_ref)
            dav_ref[...] = jnp.zeros_like(dav_ref)

        d = d_ref[...]
        lane = lax.broadcasted_iota(jnp.int32, d.shape, 1)
        u_all, u_chunk = _tri(ts, upper=True)
        rev_all = hdot(u_all, d) + carry[...]
        rev_chunk = hdot(u_chunk, d)
        carry[...] = rev_all[0:1, :]
        de = jnp.where(lane < 8, rev_all, jnp.where(lane < 16, rev_chunk, d))
        _, vjp = jax.vjp(_small_f, z_ref[...], pv_ref[...], av_ref[...])
        dz, dpv, dav = vjp(de)
        dz_ref[...] = dz.astype(dz_ref.dtype)
        dpv_ref[...] += dpv
        dav_ref[...] += dav

    rev = lambda i: (nt - 1 - i, 0)
    small_cols = pl.BlockSpec((ts, 128), lambda i: (nt - 1 - i, CB_SMALL))
    return pl.pallas_call(
        body, name="smalls_bwd", grid=(nt,),
        in_specs=[small_cols, pl.BlockSpec((1, 128), lambda i: (0, 0)), pl.BlockSpec((1, 128), lambda i: (0, 0)),
                  pl.BlockSpec((ts, 128), rev), pl.BlockSpec(memory_space=pl.ANY)],
        out_specs=[small_cols, pl.BlockSpec((1, 128), lambda i: (0, 0)), pl.BlockSpec((1, 128), lambda i: (0, 0))],
        out_shape=[jax.ShapeDtypeStruct(dp.shape, dp.dtype), jax.ShapeDtypeStruct((1, 128), F32),
                   jax.ShapeDtypeStruct((1, 128), F32)],
        scratch_shapes=[pltpu.VMEM((1, 128), F32)],
        input_output_aliases={4: 0},
    )(p_all, pv, av, dsm, dp)


def _col_to_row(col):
    return jnp.transpose(jnp.broadcast_to(col, (col.shape[0], 128)))[0:1, :]


def _row_to_col(row):
    return jnp.transpose(jnp.broadcast_to(row, (128, row.shape[1])))[:, 0:1]


def _causal_mask(s, keys_first):
    r = lax.broadcasted_iota(jnp.int32, s.shape, 0)
    c = lax.broadcasted_iota(jnp.int32, s.shape, 1)
    return jnp.where((r <= c) if keys_first else (c <= r), s, NEG)


def flash_fwd(qs, kn, vb, frow, S, tq):
    nq = S // tq
    tk = tq

    def body(q_ref, k_ref, v_ref, fr_ref, o_ref, lse_ref):
        i = pl.program_id(1)
        q = q_ref[...]
        f_base = fr_ref[0, i][:, 0:1]

        def tile(j, carry, diagonal):
            m, l, acc = carry
            off = pl.multiple_of(j * tk, tk)
            k = k_ref[pl.ds(off, tk), :]
            v = v_ref[pl.ds(off, tk), :]
            s = mm_nt(q, k) - (fr_ref[0, j] - f_base)
            if diagonal:
                s = _causal_mask(s, keys_first=False)
            m_new = jnp.maximum(m, jnp.max(s, axis=-1, keepdims=True))
            a = jnp.exp(m - m_new)
            p = jnp.exp(s - m_new)
            l = a * l + jnp.sum(p, axis=-1, keepdims=True)
            acc = a * acc + mm_nn(p, v)
            return m_new, l, acc

        init = (jnp.full((tq, 1), NEG, F32), jnp.zeros((tq, 1), F32), jnp.zeros((tq, D_HEAD), F32))
        carry = lax.fori_loop(0, i, lambda j, c: tile(j, c, False), init)
        m, l, acc = tile(i, carry, True)
        o_ref[...] = acc / l
        lse_ref[0, 0] = _col_to_row(m + jnp.log(l))

    return pl.pallas_call(
        body, name="flash_fwd", grid=(N_HEADS, nq),
        in_specs=[pl.BlockSpec((tq, D_HEAD), lambda h, i: (i, h)), pl.BlockSpec((S, D_HEAD), lambda h, i: (0, h)),
                  pl.BlockSpec((S, D_HEAD), lambda h, i: (0, h)),
                  pl.BlockSpec((1, nq, 1, tk), lambda h, i: (h, 0, 0, 0))],
        out_specs=[pl.BlockSpec((tq, D_HEAD), lambda h, i: (i, h)),
                   pl.BlockSpec((1, 1, 1, tq), lambda h, i: (h, i, 0, 0))],
        out_shape=[jax.ShapeDtypeStruct((S, WIDTH), F32), jax.ShapeDtypeStruct((N_HEADS, nq, 1, tq), F32)],
    )(qs, kn, vb, frow)


def flash_delta(do, o, S, tq):
    nq = S // tq

    def body(do_ref, o_ref, dl_ref):
        dl_ref[0, 0] = _col_to_row(jnp.sum(do_ref[...].astype(F32) * o_ref[...], axis=-1, keepdims=True))

    blk = pl.BlockSpec((tq, D_HEAD), lambda h, i: (i, h))
    return pl.pallas_call(
        body, name="flash_delta", grid=(N_HEADS, nq), in_specs=[blk, blk],
        out_specs=pl.BlockSpec((1, 1, 1, tq), lambda h, i: (h, i, 0, 0)),
        out_shape=jax.ShapeDtypeStruct((N_HEADS, nq, 1, tq), F32),
    )(do, o)


def flash_bwd(qs, kn, vb, do, frow, lse_row, delta_row, dp, S, tq):
    nq = S // tq
    tk = tq

    def body(q_ref, k_ref, v_ref, do_ref, fr_ref, lse_ref, dl_ref, dp_in, dq_ref, dk_ref, dv_ref, dfq_ref, dfk_ref):
        j = pl.program_id(1)
        k = k_ref[...]
        v = v_ref[...]
        fk = _row_to_col(fr_ref[0, j])

        @pl.when(j == 0)
        def _():
            dq_ref[...] = jnp.zeros_like(dq_ref)
            dfq_ref[...] = jnp.zeros_like(dfq_ref)

        def tile(i, carry, diagonal):
            dk, dv, dfk = carry
            off = pl.multiple_of(i * tq, tq)
            q = q_ref[pl.ds(off, tq), :]
            do_ = do_ref[pl.ds(off, tq), :]
            st = mm_nt(k, q) - (fk - fr_ref[0, i][:, 0:1])
            if diagonal:
                st = _causal_mask(st, keys_first=True)
            pt = jnp.exp(st - lse_ref[0, i])
            dst = pt * (mm_nt(v, do_) - dl_ref[0, i])
            dq_ref[pl.ds(off, tq), :] += mm_tn(dst, k) * ATT_SCALE
            dfq_ref[0, i] += jnp.sum(dst, axis=0, keepdims=True)
            return dk + mm_nn(dst, q), dv + mm_nn(pt, do_), dfk - jnp.sum(dst, axis=-1, keepdims=True)

        z = jnp.zeros((tk, D_HEAD), F32)
        carry = tile(j, (z, z, jnp.zeros((tk, 1), F32)), True)
        dk, dv, dfk = lax.fori_loop(j + 1, nq, lambda i, c: tile(i, c, False), carry)
        dk_ref[...] = dk
        dv_ref[...] = dv.astype(dv_ref.dtype)
        dfk_ref[0, 0] = _col_to_row(dfk)

    blk = pl.BlockSpec((tk, D_HEAD), lambda h, j: (j, h))
    full = pl.BlockSpec((S, D_HEAD), lambda h, j: (0, h))
    rows = pl.BlockSpec((1, nq, 1, tq), lambda h, j: (h, 0, 0, 0))
    stat = jax.ShapeDtypeStruct((N_HEADS, nq, 1, tq), F32)
    return pl.pallas_call(
        body, name="flash_bwd", grid=(N_HEADS, nq),
        in_specs=[full, blk, blk, full, rows, rows, rows, pl.BlockSpec(memory_space=pl.ANY)],
        out_specs=[full, blk, pl.BlockSpec((tk, D_HEAD), lambda h, j: (j, CB_FV + h)), rows,
                   pl.BlockSpec((1, 1, 1, tk), lambda h, j: (h, j, 0, 0))],
        out_shape=[jax.ShapeDtypeStruct((S, WIDTH), F32), jax.ShapeDtypeStruct((S, WIDTH), F32),
                   jax.ShapeDtypeStruct(dp.shape, dp.dtype), stat, stat],
        input_output_aliases={7: 2},
    )(qs, kn, vb, do, frow, lse_row, delta_row, dp)


CONV_COLS = 3 * WIDTH
CONV_TC = 512


def conv_fwd(p_all, conv_w, S, ts):
    nt, tc = S // ts, CONV_TC
    cb0 = CB_GQ * 128 // tc
    r8 = ts // 8

    def body(x_ref, halo_ref, w_ref, y_ref, scr):
        i = pl.program_id(1)
        scr[0:8, :] = jnp.where(i > 0, halo_ref[...], 0.0)
        scr[8:8 + ts, :] = x_ref[...]
        w = w_ref[...]
        y = w[3:4, :] * x_ref[...]
        for j in range(CONV_K - 1):
            y = y + w[j:j + 1, :] * scr[5 + j:5 + j + ts, :]
        y_ref[...] = y

    return pl.pallas_call(
        body, name="conv_fwd", grid=(CONV_COLS // tc, nt),
        in_specs=[pl.BlockSpec((ts, tc), lambda c, i: (i, cb0 + c)),
                  pl.BlockSpec((8, tc), lambda c, i: (jnp.maximum(i * r8 - 1, 0), cb0 + c)),
                  pl.BlockSpec((CONV_K, tc), lambda c, i: (0, c))],
        out_specs=pl.BlockSpec((ts, tc), lambda c, i: (i, c)),
        out_shape=jax.ShapeDtypeStruct((S, CONV_COLS), F32),
        scratch_shapes=[pltpu.VMEM((ts + 8, tc), F32)],
    )(p_all, p_all, conv_w)


def conv_bwd(p_all, dy, conv_w, dp, S, ts):
    nt, tc = S // ts, CONV_TC
    cb0 = CB_GQ * 128 // tc
    r8 = ts // 8

    def body(x_ref, xh_ref, dy_ref, dyh_ref, w_ref, dp_in, dx_ref, dw_ref, xs, ds):
        i = pl.program_id(1)
        xs[0:8, :] = jnp.where(i > 0, xh_ref[...], 0.0)
        xs[8:8 + ts, :] = x_ref[...]
        ds[0:ts, :] = dy_ref[...]
        ds[ts:ts + 8, :] = jnp.where(i < nt - 1, dyh_ref[...], 0.0)
        w = w_ref[...]
        d = dy_ref[...]
        dx = w[3:4, :] * d
        for j in range(CONV_K - 1):
            dx = dx + w[j:j + 1, :] * ds[3 - j:3 - j + ts, :]
        dx_ref[...] = dx.astype(dx_ref.dtype)

        @pl.when(i == 0)
        def _():
            dw_ref[...] = jnp.zeros_like(dw_ref)

        rows = [jnp.sum(d * xs[5 + j:5 + j + ts, :], axis=0, keepdims=True) for j in range(CONV_K - 1)]
        rows.append(jnp.sum(d * x_ref[...], axis=0, keepdims=True))
        dw_ref[...] += jnp.concatenate(rows, axis=0)

    return pl.pallas_call(
        body, name="conv_bwd", grid=(CONV_COLS // tc, nt),
        in_specs=[pl.BlockSpec((ts, tc), lambda c, i: (i, cb0 + c)),
                  pl.BlockSpec((8, tc), lambda c, i: (jnp.maximum(i * r8 - 1, 0), cb0 + c)),
                  pl.BlockSpec((ts, tc), lambda c, i: (i, c)),
                  pl.BlockSpec((8, tc), lambda c, i: (jnp.minimum((i + 1) * r8, nt * r8 - 1), c)),
                  pl.BlockSpec((CONV_K, tc), lambda c, i: (0, c)), pl.BlockSpec(memory_space=pl.ANY)],
        out_specs=[pl.BlockSpec((ts, tc), lambda c, i: (i, cb0 + c)), pl.BlockSpec((CONV_K, tc), lambda c, i: (0, c))],
        out_shape=[jax.ShapeDtypeStruct(dp.shape, dp.dtype), jax.ShapeDtypeStruct((CONV_K, CONV_COLS), F32)],
        scratch_shapes=[pltpu.VMEM((ts + 8, tc), F32), pltpu.VMEM((ts + 8, tc), F32)],
        input_output_aliases={5: 0},
    )(p_all, p_all, dy, dy, conv_w, dp)


def _bdot(a, b, ca, cb):
    return lax.dot_general(a.astype(MXU_DTYPE), b.astype(MXU_DTYPE), (((ca,), (cb,)), ((0,), (0,))),
                           preferred_element_type=F32)


@jax.custom_vjp
def bmm_nn(a, b):
    return _bdot(a, b, 2, 1)


@jax.custom_vjp
def bmm_nt(a, b):
    return _bdot(a, b, 2, 2)


@jax.custom_vjp
def bmm_tn(a, b):
    return _bdot(a, b, 1, 1)


bmm_nn.defvjp(lambda a, b: (bmm_nn(a, b), (a, b)), lambda r, g: (bmm_nt(g, r[1]), bmm_tn(r[0], g)))
bmm_nt.defvjp(lambda a, b: (bmm_nt(a, b), (a, b)), lambda r, g: (bmm_nn(g, r[1]), bmm_tn(g, r[0])))
bmm_tn.defvjp(lambda a, b: (bmm_tn(a, b), (a, b)), lambda r, g: (bmm_nt(r[1], g), bmm_nn(r[0], g)))


def bdot_hi(a, b, ca=2, cb=1):
    return lax.dot_general(a, b, (((ca,), (cb,)), ((0,), (0,))), precision=SOLVE_PRECISION,
                           preferred_element_type=F32)


def _tri_inv_product(m):
    ii = lax.broadcasted_iota(jnp.int32, m.shape, 1)
    jj = lax.broadcasted_iota(jnp.int32, m.shape, 2)
    t = (ii == jj).astype(F32) - m
    pw = bdot_hi(m, m)
    for _ in range(4):
        t = t + bdot_hi(t, pw)
        pw = bdot_hi(pw, pw)
    return t + bdot_hi(t, pw)


def _tri_inv_bwd(t, g):
    return -bdot_hi(bdot_hi(t, g, 1, 1), t, 2, 2)


@jax.custom_vjp
def tri_inv(m):
    return _tri_inv_product(m)


@jax.custom_vjp
def tri_inv_known(m, t):
    return t


tri_inv.defvjp(lambda m: (lambda t: (t, t))(_tri_inv_product(m)), lambda t, g: (_tri_inv_bwd(t, g),))
tri_inv_known.defvjp(lambda m, t: (t, t), lambda t, g: (_tri_inv_bwd(t, g), jnp.zeros_like(t)))


def chunk_fn(q, k, v, gc_col, gc_row, beta, state, t_known=None):
    shape = (N_HEADS, CHUNK, CHUNK)
    ii = lax.broadcasted_iota(jnp.int32, shape, 1)
    jj = lax.broadcasted_iota(jnp.int32, shape, 2)
    lower = ii >= jj
    strict = ii > jj
    decay = jnp.exp(jnp.where(lower, gc_col - gc_row, NEG))
    kb = k * beta
    vb = v * beta
    m = jnp.where(strict, bmm_nt(kb, k) * decay, 0.0)
    t = tri_inv(m) if t_known is None else tri_inv_known(m, t_known)
    u = bdot_hi(t, vb)
    w = bdot_hi(t, kb * jnp.exp(gc_col))
    attn = jnp.where(lower, bmm_nt(q, k) * decay, 0.0)
    v_new = u - bmm_nn(w, state)
    o = bmm_nn(q * jnp.exp(gc_col), state) + bmm_nn(attn, v_new)
    g_last = gc_col[:, CHUNK - 1:CHUNK, :]
    k_dec = k * jnp.exp(g_last - gc_col)
    new_state = state * jnp.exp(g_last) + bmm_tn(k_dec, v_new)
    return (o, new_state, t) if t_known is None else (o, new_state)


def _heads(ref):
    return jnp.stack([ref[:, h * D_HEAD:(h + 1) * D_HEAD] for h in range(N_HEADS)], axis=0)


def _head_cols(ref, lane0):
    return jnp.stack([ref[:, lane0 + h:lane0 + h + 1] for h in range(N_HEADS)], axis=0)


def _qkv_head_cols(h):
    return [slice(g * WIDTH + h * D_HEAD, g * WIDTH + (h + 1) * D_HEAD) for g in range(3)]


def gdn_pre_fwd(y, S, ts):
    def body(y_ref, o_ref):
        for h in range(N_HEADS):
            cols = _qkv_head_cols(h)
            for c, o in zip(cols, f_gdnpre(*[y_ref[:, c] for c in cols])):
                o_ref[:, c] = o

    spec = pl.BlockSpec((ts, CONV_COLS), lambda i: (i, 0))
    return pl.pallas_call(body, name="gdn_pre", grid=(S // ts,), in_specs=[spec], out_specs=spec,
                          out_shape=jax.ShapeDtypeStruct((S, CONV_COLS), F32))(y)


def gdn_pre_bwd(y, dqkv, S, ts):
    def body(y_ref, d_ref, o_ref):
        for h in range(N_HEADS):
            cols = _qkv_head_cols(h)
            _, vjp = jax.vjp(f_gdnpre, *[y_ref[:, c] for c in cols])
            for c, g in zip(cols, vjp(tuple(d_ref[:, c] for c in cols))):
                o_ref[:, c] = g

    spec = pl.BlockSpec((ts, CONV_COLS), lambda i: (i, 0))
    return pl.pallas_call(body, name="gdn_pre_bwd", grid=(S // ts,), in_specs=[spec, spec], out_specs=spec,
                          out_shape=jax.ShapeDtypeStruct((S, CONV_COLS), F32))(y, dqkv)


def gdn_fwd(qkv, smalls, gc_row, S):
    n = S // CHUNK

    def body(q_ref, k_ref, v_ref, sm_ref, gr_ref, o_ref, st_ref, t_ref, state):
        @pl.when(pl.program_id(0) == 0)
        def _():
            state[...] = jnp.zeros_like(state)

        s0 = state[...]
        st_ref[0] = s0
        o, s1, t = chunk_fn(_heads(q_ref), _heads(k_ref), _heads(v_ref), _head_cols(sm_ref, 8),
                            gr_ref[0][:, None, :], _head_cols(sm_ref, 16), s0)
        for h in range(N_HEADS):
            o_ref[:, h * D_HEAD:(h + 1) * D_HEAD] = o[h]
        state[...] = s1
        t_ref[0] = t

    return pl.pallas_call(
        body, name="gdn_chunk_fwd", grid=(n,),
        in_specs=[pl.BlockSpec((CHUNK, WIDTH), lambda i, g=g: (i, g)) for g in range(3)]
        + [pl.BlockSpec((CHUNK, 128), lambda i: (i, 0)), pl.BlockSpec((1, N_HEADS, CHUNK), lambda i: (i, 0, 0))],
        out_specs=[pl.BlockSpec((CHUNK, WIDTH), lambda i: (i, 0)),
                   pl.BlockSpec((1, N_HEADS, D_HEAD, D_HEAD), lambda i: (i, 0, 0, 0)),
                   pl.BlockSpec((1, N_HEADS, CHUNK, CHUNK), lambda i: (i, 0, 0, 0))],
        out_shape=[jax.ShapeDtypeStruct((S, WIDTH), F32), jax.ShapeDtypeStruct((n, N_HEADS, D_HEAD, D_HEAD), F32),
                   jax.ShapeDtypeStruct((n, N_HEADS, CHUNK, CHUNK), F32)],
        scratch_shapes=[pltpu.VMEM((N_HEADS, D_HEAD, D_HEAD), F32)],
    )(qkv, qkv, qkv, smalls, gc_row)


def gdn_bwd(qkv, smalls, gc_row, states, tinv, do, S):
    n = S // CHUNK

    def body(q_ref, k_ref, v_ref, sm_ref, gr_ref, st_ref, t_ref, do_ref, dqkv_ref, dsm_ref, dgr_ref, dstate):
        @pl.when(pl.program_id(0) == 0)
        def _():
            dstate[...] = jnp.zeros_like(dstate)

        t_saved = t_ref[0]
        _, vjp = jax.vjp(lambda *a: chunk_fn(*a, t_known=t_saved), _heads(q_ref), _heads(k_ref), _heads(v_ref),
                         _head_cols(sm_ref, 8), gr_ref[0][:, None, :], _head_cols(sm_ref, 16), st_ref[0])
        dq, dk, dv, dgc, dgr, dbt, ds = vjp((_heads(do_ref), dstate[...]))
        for h in range(N_HEADS):
            dqkv_ref[:, h * D_HEAD:(h + 1) * D_HEAD] = dq[h]
            dqkv_ref[:, WIDTH + h * D_HEAD:WIDTH + (h + 1) * D_HEAD] = dk[h]
            dqkv_ref[:, 2 * WIDTH + h * D_HEAD:2 * WIDTH + (h + 1) * D_HEAD] = dv[h]
        dstate[...] = ds
        cols = [dgc[h] for h in range(N_HEADS)] + [dbt[h] for h in range(N_HEADS)]
        dsm_ref[...] = jnp.concatenate([jnp.zeros((CHUNK, 8), F32)] + cols + [jnp.zeros((CHUNK, 128 - 24), F32)],
                                       axis=1)
        dgr_ref[0] = jnp.concatenate([dgr[h] for h in range(N_HEADS)], axis=0)

    rev = lambda c: (lambda i: (n - 1 - i, c))
    return pl.pallas_call(
        body, name="gdn_chunk_bwd", grid=(n,),
        in_specs=[pl.BlockSpec((CHUNK, WIDTH), rev(g)) for g in range(3)]
        + [pl.BlockSpec((CHUNK, 128), rev(0)), pl.BlockSpec((1, N_HEADS, CHUNK), lambda i: (n - 1 - i, 0, 0)),
           pl.BlockSpec((1, N_HEADS, D_HEAD, D_HEAD), lambda i: (n - 1 - i, 0, 0, 0)),
           pl.BlockSpec((1, N_HEADS, CHUNK, CHUNK), lambda i: (n - 1 - i, 0, 0, 0)),
           pl.BlockSpec((CHUNK, WIDTH), rev(0))],
        out_specs=[pl.BlockSpec((CHUNK, 3 * WIDTH), rev(0)), pl.BlockSpec((CHUNK, 128), rev(0)),
                   pl.BlockSpec((1, N_HEADS, CHUNK), lambda i: (n - 1 - i, 0, 0))],
        out_shape=[jax.ShapeDtypeStruct((S, 3 * WIDTH), F32)] + [jax.ShapeDtypeStruct((S, 128), F32),
                                                                  jax.ShapeDtypeStruct((n, N_HEADS, CHUNK), F32)],
        scratch_shapes=[pltpu.VMEM((N_HEADS, D_HEAD, D_HEAD), F32)],
    )(qkv, qkv, qkv, smalls, gc_row, states, tinv, do)


def _flip(a, bit):
    return 1 - a if bit else a


def allgather8(name, buf):
    R, W = buf.shape

    def body(x_ref, out_ref, send_sems, recv_sems, local_sem):
        x, y, c = lax.axis_index("x"), lax.axis_index("y"), lax.axis_index("c")

        def slot(px, py, pc):
            return out_ref.at[4 * px + 2 * py + pc]

        mine = pltpu.make_async_copy(x_ref, slot(x, y, c), local_sem)
        mine.start()
        sends = []
        for r in range(1, N_DEV):
            peer = (_flip(x, r & 4), _flip(y, r & 2), _flip(c, r & 1))
            cp = pltpu.make_async_remote_copy(src_ref=x_ref, dst_ref=slot(x, y, c), send_sem=send_sems.at[r - 1],
                                              recv_sem=recv_sems.at[r - 1], device_id=peer, device_id_type=MESH_ID)
            cp.start()
            sends.append(cp)
        for r in range(1, N_DEV):
            peer = (_flip(x, r & 4), _flip(y, r & 2), _flip(c, r & 1))
            pltpu.make_async_remote_copy(src_ref=x_ref, dst_ref=slot(*peer), send_sem=send_sems.at[r - 1],
                                         recv_sem=recv_sems.at[r - 1], device_id=peer,
                                         device_id_type=MESH_ID).wait_recv()
        for cp in sends:
            cp.wait_send()
        mine.wait()

    return pl.pallas_call(
        body, name=name,
        out_shape=jax.ShapeDtypeStruct((N_DEV, R, W), buf.dtype),
        in_specs=[pl.BlockSpec(memory_space=pltpu.VMEM)],
        out_specs=pl.BlockSpec(memory_space=pltpu.VMEM),
        scratch_shapes=[pltpu.SemaphoreType.DMA((N_DEV - 1,)), pltpu.SemaphoreType.DMA((N_DEV - 1,)),
                        pltpu.SemaphoreType.DMA],
    )(buf)


def allgather_chips(name, blk):
    R, W = blk.shape
    half = R // 2
    n_far = N_CHIPS - 1

    def body(x_ref, out_ref, send_sems, recv_sems):
        x, y, c = lax.axis_index("x"), lax.axis_index("y"), lax.axis_index("c")
        sibling = (x, y, 1 - c)
        chips = [(_flip(x, r & 2), _flip(y, r & 1)) for r in range(1, N_CHIPS)]

        def rows(px, py, pc):
            return out_ref.at[2 * px + py, pc]

        def copy(k, dst, to, src):
            return pltpu.make_async_remote_copy(src_ref=src, dst_ref=dst, send_sem=send_sems.at[k],
                                                recv_sem=recv_sems.at[k], device_id=to, device_id_type=MESH_ID)

        my_half = x_ref.at[c]
        first = [copy(j, rows(x, y, c), (*chip, c), my_half) for j, chip in enumerate(chips)]
        for cp in first:
            cp.start()
        passed = [copy(n_far + j, rows(*chip, c), sibling, rows(*chip, c)) for j, chip in enumerate(chips)]
        for j, chip in enumerate(chips):
            copy(j, rows(*chip, c), (*chip, c), my_half).wait_recv()
            passed[j].start()
        for j, chip in enumerate(chips):
            copy(n_far + j, rows(*chip, 1 - c), sibling, my_half).wait_recv()
        for cp in first + passed:
            cp.wait_send()

    far = pl.pallas_call(
        body, name=name,
        out_shape=jax.ShapeDtypeStruct((N_CHIPS, 2, half, W), blk.dtype),
        in_specs=[pl.BlockSpec(memory_space=pl.ANY)],
        out_specs=pl.BlockSpec(memory_space=pl.ANY),
        scratch_shapes=[pltpu.SemaphoreType.DMA((2 * n_far,)), pltpu.SemaphoreType.DMA((2 * n_far,))],
    )(blk.reshape(2, half, W)).reshape(N_CHIPS, R, W)
    own_chip = 2 * lax.axis_index("x") + lax.axis_index("y")
    return lax.dynamic_update_index_in_dim(far, blk, own_chip, axis=0)


def scatter_chips(name, parts):
    def body(x_ref, out_ref, send_sems, recv_sems):
        x, y, c = lax.axis_index("x"), lax.axis_index("y"), lax.axis_index("c")
        sends = []
        for r in range(1, N_CHIPS):
            peer = (_flip(x, r & 2), _flip(y, r & 1), c)
            cp = pltpu.make_async_remote_copy(src_ref=x_ref.at[2 * peer[0] + peer[1]], dst_ref=out_ref.at[r - 1],
                                              send_sem=send_sems.at[r - 1], recv_sem=recv_sems.at[r - 1],
                                              device_id=peer, device_id_type=MESH_ID)
            cp.start()
            sends.append(cp)
        for cp in sends:
            cp.wait_recv()
        for cp in sends:
            cp.wait_send()

    return pl.pallas_call(
        body, name=name,
        out_shape=jax.ShapeDtypeStruct((N_CHIPS - 1,) + parts.shape[1:], parts.dtype),
        in_specs=[pl.BlockSpec(memory_space=pl.ANY)],
        out_specs=pl.BlockSpec(memory_space=pl.ANY),
        scratch_shapes=[pltpu.SemaphoreType.DMA((N_CHIPS - 1,)), pltpu.SemaphoreType.DMA((N_CHIPS - 1,))],
    )(parts)


def sibling_send_other_half(name, g4):
    n, R, W = g4.shape
    half = R // 2

    def body(x_ref, out_ref, send_sem, recv_sem):
        x, y, c = lax.axis_index("x"), lax.axis_index("y"), lax.axis_index("c")
        cp = pltpu.make_async_remote_copy(src_ref=x_ref.at[:, pl.ds((1 - c) * half, half), :], dst_ref=out_ref,
                                          send_sem=send_sem, recv_sem=recv_sem, device_id=(x, y, 1 - c),
                                          device_id_type=MESH_ID)
        cp.start()
        cp.wait_recv()
        cp.wait_send()

    return pl.pallas_call(
        body, name=name,
        out_shape=jax.ShapeDtypeStruct((n, half, W), g4.dtype),
        in_specs=[pl.BlockSpec(memory_space=pl.ANY)],
        out_specs=pl.BlockSpec(memory_space=pl.ANY),
        scratch_shapes=[pltpu.SemaphoreType.DMA, pltpu.SemaphoreType.DMA],
    )(g4)


def sibling_merge(name, mine):
    def body(x_ref, out_ref, send_sem, recv_sem):
        x, y, c = lax.axis_index("x"), lax.axis_index("y"), lax.axis_index("c")
        cp = pltpu.make_async_remote_copy(src_ref=x_ref, dst_ref=out_ref, send_sem=send_sem, recv_sem=recv_sem,
                                          device_id=(x, y, 1 - c), device_id_type=MESH_ID)
        cp.start()
        cp.wait_recv()
        cp.wait_send()

    other = pl.pallas_call(
        body, name=name,
        out_shape=jax.ShapeDtypeStruct(mine.shape, mine.dtype),
        in_specs=[pl.BlockSpec(memory_space=pl.ANY)],
        out_specs=pl.BlockSpec(memory_space=pl.ANY),
        scratch_shapes=[pltpu.SemaphoreType.DMA, pltpu.SemaphoreType.DMA],
    )(mine)
    first = lax.axis_index("c") == 0
    return jnp.concatenate([jnp.where(first, mine, other), jnp.where(first, other, mine)], axis=0)


def sum_parts(name, own, recv, tr):
    R, W = own.shape
    tr = _tile(R, tr)

    def body(o_ref, r_ref, out_ref):
        acc = o_ref[...]
        for k in range(recv.shape[0]):
            acc = acc + r_ref[k].astype(F32)
        out_ref[...] = acc

    return pl.pallas_call(
        body, name=name, grid=(R // tr,),
        in_specs=[pl.BlockSpec((tr, W), lambda i: (i, 0)), pl.BlockSpec((recv.shape[0], tr, W), lambda i: (0, i, 0))],
        out_specs=pl.BlockSpec((tr, W), lambda i: (i, 0)),
        out_shape=jax.ShapeDtypeStruct((R, W), F32),
    )(own, recv)


def sum_own_half(name, g4, recv, c_idx, tr):
    n, R, W = g4.shape
    half = R // 2
    tr = _tile(half, tr)
    nb = half // tr

    def body(c_ref, g_ref, r_ref, o_ref, ob_ref):
        s = g_ref[0] + r_ref[0]
        o_ref[0] = s
        ob_ref[0] = s.astype(ob_ref.dtype)

    mine = pl.BlockSpec((1, tr, W), lambda j, i, c: (j, i, 0))
    return pl.pallas_call(
        body, name=name,
        grid_spec=pltpu.PrefetchScalarGridSpec(
            num_scalar_prefetch=1, grid=(n, nb),
            in_specs=[pl.BlockSpec((1, tr, W), lambda j, i, c: (j, c[0] * nb + i, 0)), mine],
            out_specs=[mine, mine]),
        out_shape=[jax.ShapeDtypeStruct((n, half, W), F32), jax.ShapeDtypeStruct((n, half, W), MXU_DTYPE)],
    )(c_idx, g4, recv)


def sum_devices(name, g):
    def body(g_ref, o_ref):
        acc = g_ref[0]
        for d in range(1, N_DEV):
            acc = acc + g_ref[d]
        o_ref[...] = acc

    return pl.pallas_call(body, name=name, out_shape=jax.ShapeDtypeStruct(g.shape[1:], F32))(g)


def ada_fwd(c_all, w_ada):
    K, N = w_ada.shape
    tn = _tile(N, 512)

    def body(c_ref, w_ref, o_ref):
        o_ref[...] = mm_nn(_silu(c_ref[...]), w_ref[...])

    return pl.pallas_call(
        body, name="ada_fwd", grid=(N // tn,),
        in_specs=[pl.BlockSpec((N_DEV, K), lambda j: (0, 0)), pl.BlockSpec((K, tn), lambda j: (0, j))],
        out_specs=pl.BlockSpec((N_DEV, tn), lambda j: (0, j)),
        out_shape=jax.ShapeDtypeStruct((N_DEV, N), F32),
    )(c_all, w_ada)


def ada_grad(c_all, dmod):
    K = c_all.shape[1]
    N = dmod.shape[1]
    tn = _tile(N, 512)

    def body(c_ref, d_ref, o_ref):
        o_ref[...] = mm_tn(_silu(c_ref[...]), d_ref[...])

    return pl.pallas_call(
        body, name="ada_grad", grid=(N // tn,),
        in_specs=[pl.BlockSpec((N_DEV, K), lambda j: (0, 0)), pl.BlockSpec((N_DEV, tn), lambda j: (0, j))],
        out_specs=pl.BlockSpec((K, tn), lambda j: (0, j)),
        out_shape=jax.ShapeDtypeStruct((K, N), F32),
    )(c_all, dmod)


def adamw(name, w, gparts, m, v, tr):
    R, W = w.shape
    tr = _tile(R, tr)
    ng = len(gparts)

    def body(w_ref, *refs):
        g_refs, (m_ref, v_ref, g_out, d_out, m_out, v_out) = refs[:ng], refs[ng:]
        g = g_refs[0][...]
        for r in g_refs[1:]:
            g = g + r[...]
        m1 = ADAM_B1 * m_ref[...] + (1.0 - ADAM_B1) * g
        v1 = ADAM_B2 * v_ref[...] + (1.0 - ADAM_B2) * jnp.square(g)
        m_hat = m1 / (1.0 - ADAM_B1 ** ADAM_STEP)
        v_hat = v1 / (1.0 - ADAM_B2 ** ADAM_STEP)
        g_out[...] = g
        d_out[...] = -ADAM_LR * (m_hat / (jnp.sqrt(v_hat) + ADAM_EPS) + ADAM_WD * w_ref[...])
        m_out[...] = m1
        v_out[...] = v1

    spec = pl.BlockSpec((tr, W), lambda i: (i, 0))
    return pl.pallas_call(
        body, name=name, grid=(R // tr,),
        in_specs=[spec] * (3 + ng), out_specs=[spec] * 4,
        out_shape=[jax.ShapeDtypeStruct((R, W), F32)] * 4,
    )(w, *gparts, m, v)


_REF_SEGMENTS = ((0, 4 * WIDTH, 0), (4 * WIDTH, 4 * WIDTH + 8, 4 * WIDTH), (4 * WIDTH + 8, 8 * WIDTH + 8, -8),
                 (8 * WIDTH + 8, IN_WIDTH, 0))


def _internal_from_blocks(blocks, n):
    a = 4 * WIDTH
    pieces = []
    for lo, hi in ((0, a), (a + 8, 2 * a + 8), (a, a + 8), (2 * a + 8, IN_WIDTH)):
        for j in range(N_CHIPS):
            s, e = max(lo, j * n), min(hi, (j + 1) * n)
            if s < e:
                pieces.append(blocks[j][:, s - j * n:e - j * n])
    pieces.append(jnp.zeros(blocks.shape[1:2] + (128 - N_SMALL,), blocks.dtype))
    return jnp.concatenate(pieces, axis=1)


def _block_from_internal(g, j, n):
    pieces = []
    for lo, hi, shift in _REF_SEGMENTS:
        s, e = max(lo, j * n), min(hi, (j + 1) * n)
        if s < e:
            pieces.append(g[:, s + shift:e + shift])
    return jnp.concatenate(pieces, axis=1)


def _pad_lanes(v, n=128):
    return jnp.pad(v, ((0, 0), (0, n - v.shape[1])))


def kernel(x, c, norm_g, w_ada, b_ada, w_in, b_fgate, fox_qn_g, fox_kn_g, gdn_conv_w, gdn_A_log, gdn_dt_bias, gdn_norm_g, w_out, final_g, loss_target, m_norm_g, m_w_ada, m_b_ada, m_w_in, m_b_fgate, m_fox_qn_g, m_fox_kn_g, m_gdn_conv_w, m_gdn_A_log, m_gdn_dt_bias, m_gdn_norm_g, m_w_out, m_final_g, v_norm_g, v_w_ada, v_b_ada, v_w_in, v_b_fgate, v_fox_qn_g, v_fox_kn_g, v_gdn_conv_w, v_gdn_A_log, v_gdn_dt_bias, v_gdn_norm_g, v_w_out, v_final_g):
    S = x.shape[1]
    H = N_HEADS
    ix, iy, ic = lax.axis_index("x"), lax.axis_index("y"), lax.axis_index("c")
    chip = 2 * ix + iy
    me = 2 * chip + ic
    x2, tgt = x[0], loss_target[0]
    ts_wide = _tile(S, 256)
    ts = _tile(S, 512)
    tq = _tile(S, 512)
    nq = S // tq
    nch = S // CHUNK
    conv_cols_chip = CONV_COLS // N_CHIPS

    pay = jnp.concatenate([c, _pad_lanes(gdn_conv_w[0], D_MODEL), jnp.zeros((3, D_MODEL), F32)], axis=0)
    g1 = allgather8("ag_c", pay)
    c_all = g1[:, 0, :]
    conv_w = jnp.concatenate([g1[2 * j, 1:1 + CONV_K, :conv_cols_chip] for j in range(N_CHIPS)], axis=1)
    g2 = allgather8("ag_mod", ada_fwd(c_all, w_ada[0]))
    mod_rows = jnp.concatenate([g2[2 * j] for j in range(N_CHIPS)], axis=1)
    mod = lax.dynamic_slice_in_dim(mod_rows, me, 1, axis=0) + b_ada
    shift, scale, gate = mod[:, :D_MODEL], mod[:, D_MODEL:2 * D_MODEL], mod[:, 2 * D_MODEL:]

    n_in = w_in.shape[2]
    win4 = allgather_chips("ag_w_in", w_in[0].astype(MXU_DTYPE))
    w_all = _internal_from_blocks(win4, n_in)
    wout4 = allgather_chips("ag_w_out", w_out[0].astype(MXU_DTYPE))
    w_out_full = wout4.reshape(2 * WIDTH, D_MODEL)

    (h_b,) = rowwise_fwd("prologue", f_prologue, S, ts_wide, 1, [(x2, "row", D_MODEL, 0)], [norm_g, scale, shift],
                         [("row", D_MODEL, 0, D_MODEL, MXU_DTYPE)])
    p_all = matmul("in_proj", h_b, w_all, F32, tm=512, tn=640, tk=2048)
    pv = jnp.concatenate([b_fgate, gdn_dt_bias, jnp.zeros((1, 128 - 16), F32)], axis=1)
    av = jnp.concatenate([jnp.zeros((1, 8), F32), gdn_A_log, jnp.zeros((1, 128 - 16), F32)], axis=1)
    smalls = smalls_fwd(p_all, pv, av, S, ts)
    frow = jnp.transpose(smalls[:, :H]).reshape(H, nq, 1, tq)
    gc_row = jnp.transpose(smalls[:, 8:16].reshape(nch, CHUNK, H), (0, 2, 1))
    head_b = ("head", D_HEAD, 0, WIDTH, MXU_DTYPE)
    head_f = ("head", D_HEAD, 0, WIDTH, F32)
    pcol = lambda cb: (p_all, "head", D_HEAD, cb)
    qn, kn, vb = rowwise_fwd("fox_pre", f_foxpre, S, ts, H, [pcol(CB_FQ), pcol(CB_FK), pcol(CB_FV)],
                             [fox_qn_g, fox_kn_g], [head_b] * 3)
    o_fox, lse = flash_fwd(qn, kn, vb, frow, S, tq)
    y_conv = conv_fwd(p_all, conv_w, S, ts)
    qkv = gdn_pre_fwd(y_conv, S, ts_wide)
    o_gdn, states, tinv = gdn_fwd(qkv, smalls, gc_row, S)
    mixed = lax.empty((S, 2 * WIDTH), MXU_DTYPE)
    (mixed,) = rowwise_fwd("post_fox", f_postf, S, ts, H, [(o_fox, "head", D_HEAD, 0), pcol(CB_FZ)], [],
                           [("head", D_HEAD, 0, 2 * WIDTH, MXU_DTYPE, mixed)])
    (mixed,) = rowwise_fwd("post_gdn", f_postg, S, ts, H, [(o_gdn, "head", D_HEAD, 0), pcol(CB_GZ)], [gdn_norm_g],
                           [("head", D_HEAD, H, 2 * WIDTH, MXU_DTYPE, mixed)])
    mo = matmul("out_proj", mixed, w_out_full, F32)

    wide = lambda a: (a, "row", D_MODEL, 0)
    dx_res, dmo, d_gate, d_final_g, loss_acc = rowwise_bwd(
        "loss", f_loss, S, ts_wide, 1, [wide(x2), wide(mo), wide(tgt)], [gate, final_g[None, :]], [],
        [(0, "row", D_MODEL, 0, D_MODEL, F32), (1, "row", D_MODEL, 0, D_MODEL, MXU_DTYPE)], primal_sum=True)

    d_mixed = matmul("d_mixed", dmo, w_out_full, F32, trans_b=True)
    g_w_out = matmul("g_w_out", mixed, dmo, F32, trans_a=True)
    dp = lax.empty((S, P_COLS), MXU_DTYPE)
    into_dp = lambda idx, cb: (idx, "head", D_HEAD, cb, P_COLS, MXU_DTYPE, dp)
    do_fox, dp = rowwise_bwd(
        "post_fox_bwd", f_postf, S, ts, H, [(o_fox, "head", D_HEAD, 0), pcol(CB_FZ)], [],
        [(d_mixed, "head", D_HEAD, 0)], [(0,) + head_b, into_dp(1, CB_FZ)])
    do_gdn, dp, d_gdn_norm_g = rowwise_bwd(
        "post_gdn_bwd", f_postg, S, ts, H, [(o_gdn, "head", D_HEAD, 0), pcol(CB_GZ)], [gdn_norm_g],
        [(d_mixed, "head", D_HEAD, 8)], [(0,) + head_f, into_dp(1, CB_GZ)])
    d_qn, d_kn, dp, dfq_row, dfk_row = flash_bwd(qn, kn, vb, do_fox, frow, lse, flash_delta(do_fox, o_fox, S, tq), dp,
                                                 S, tq)
    dp, d_qn_g = rowwise_bwd("fox_q_bwd", f_rms1, S, ts, H, [pcol(CB_FQ)], [fox_qn_g], [(d_qn, "head", D_HEAD, 0)],
                             [into_dp(0, CB_FQ)])
    dp, d_kn_g = rowwise_bwd("fox_k_bwd", f_rms1, S, ts, H, [pcol(CB_FK)], [fox_kn_g], [(d_kn, "head", D_HEAD, 0)],
                             [into_dp(0, CB_FK)])
    d_qkv, dsm_chunk, dgc_row = gdn_bwd(qkv, smalls, gc_row, states, tinv, do_gdn, S)
    dp, d_conv_w = conv_bwd(p_all, gdn_pre_bwd(y_conv, d_qkv, S, ts_wide), conv_w, dp, S, ts)
    d_f = jnp.transpose((dfq_row + dfk_row).reshape(H, S))
    d_gc = jnp.transpose(dgc_row, (0, 2, 1)).reshape(S, H)
    dsm = dsm_chunk + jnp.concatenate([d_f, d_gc, jnp.zeros((S, 128 - 16), F32)], axis=1)
    dp, d_pv, d_av = smalls_bwd(p_all, pv, av, dsm, dp, S, ts)
    d_h = matmul("d_h", dp, w_all, F32, tm=512, tn=1024, tk=1664, trans_b=True)
    g_w_all = matmul("g_w_in", h_b, dp, F32, tm=512, tn=640, tk=2048, trans_a=True)
    d_x, d_norm_g, d_scale, d_shift = rowwise_bwd(
        "prologue_bwd", f_prologue_res, S, ts_wide, 1, [wide(x2)], [norm_g, scale, shift],
        [wide(d_h), wide(dx_res)], [(0, "row", D_MODEL, 0, D_MODEL, F32)])

    parts = [d_shift, d_scale, d_gate, d_norm_g, d_final_g, d_conv_w.reshape(1, CONV_K * CONV_COLS), d_qn_g, d_kn_g,
             d_gdn_norm_g, d_pv, d_av, loss_acc]
    n_pay = sum(p.shape[1] for p in parts)
    pay_w = -(-n_pay // 1024) * 1024
    pay2 = jnp.concatenate(parts + [jnp.zeros((1, pay_w - n_pay), F32)], axis=1).reshape(8, pay_w // 8)
    g3 = allgather8("ag_small", pay2)
    tot = sum_devices("sum_small", g3).reshape(1, pay_w)
    dmod_all = g3.reshape(N_DEV, pay_w)[:, :3 * D_MODEL]
    o = 3 * D_MODEL
    g_b_ada = tot[:, :o]
    g_norm_g, g_final_g = tot[:, o:o + D_MODEL], tot[:, o + D_MODEL:o + 2 * D_MODEL]
    o += 2 * D_MODEL
    g_conv_full = tot[:, o:o + CONV_K * CONV_COLS].reshape(CONV_K, CONV_COLS)
    g_conv = lax.dynamic_slice_in_dim(g_conv_full, chip * conv_cols_chip, conv_cols_chip, axis=1)
    o += CONV_K * CONV_COLS
    g_qn_g, g_kn_g, g_gdn_ng = tot[:, o:o + 128], tot[:, o + 128:o + 256], tot[:, o + 256:o + 384]
    g_pv, g_av = tot[:, o + 384:o + 512], tot[:, o + 512:o + 640]
    loss = tot[0, o + 640]
    g_b_fgate, g_dt_bias, g_a_log = g_pv[:, :8], g_pv[:, 8:16], g_av[:, 8:16]

    def small_vec(vals):
        flat = [norm for norm in vals[:3]] + [vals[3].reshape(1, CONV_K * conv_cols_chip)] + list(vals[4:7]) \
            + [_pad_lanes(s) for s in vals[7:]]
        n = sum(f.shape[1] for f in flat)
        nw = -(-n // 1024) * 1024
        return jnp.concatenate(flat + [jnp.zeros((1, nw - n), F32)], axis=1).reshape(8, nw // 8)

    sw = small_vec([norm_g, b_ada, final_g[None, :], gdn_conv_w[0], fox_qn_g, fox_kn_g, gdn_norm_g, b_fgate, gdn_A_log,
                    gdn_dt_bias])
    sm = small_vec([m_norm_g, m_b_ada, m_final_g[None, :], m_gdn_conv_w[0], m_fox_qn_g, m_fox_kn_g, m_gdn_norm_g,
                    m_b_fgate, m_gdn_A_log, m_gdn_dt_bias])
    sv = small_vec([v_norm_g, v_b_ada, v_final_g[None, :], v_gdn_conv_w[0], v_fox_qn_g, v_fox_kn_g, v_gdn_norm_g,
                    v_b_fgate, v_gdn_A_log, v_gdn_dt_bias])
    sg = small_vec([g_norm_g, g_b_ada, g_final_g, g_conv, g_qn_g, g_kn_g, g_gdn_ng, g_b_fgate, g_a_log, g_dt_bias])
    small_out = adamw("adamw_small", sw, [sg], sm, sv, 8)

    def split_small(a):
        a = a.reshape(1, -1)
        out, o = [], 0
        for n, shp in ((D_MODEL, (1, D_MODEL)), (3 * D_MODEL, (1, 3 * D_MODEL)), (D_MODEL, (D_MODEL,)),
                       (CONV_K * conv_cols_chip, (1, CONV_K, conv_cols_chip)), (128, (1, 128)), (128, (1, 128)),
                       (128, (1, 128))):
            out.append(a[:, o:o + n].reshape(shp))
            o += n
        for _ in range(3):
            out.append(a[:, o:o + 8])
            o += 128
        return out

    n_ada = w_ada.shape[2]
    g_w_ada = ada_grad(c_all, lax.dynamic_slice_in_dim(dmod_all, chip * n_ada, n_ada, axis=1))
    ada_out = adamw("adamw_w_ada", w_ada[0], [g_w_ada], m_w_ada[0], v_w_ada[0], 128)

    c_idx = jnp.reshape(ic, (1,)).astype(jnp.int32)

    def reduce_scatter(tag, g4, tr):
        recv = sibling_send_other_half("d2d_" + tag, g4)
        p4, p4_sent = sum_own_half("sum2_" + tag, g4, recv, c_idx, tr)
        far = scatter_chips("rs_" + tag, p4_sent)
        own = lax.dynamic_index_in_dim(p4, chip, axis=0, keepdims=False)
        return sibling_merge("merge_" + tag, sum_parts("sum_" + tag, own, far, tr))

    g4_in = jnp.stack([_block_from_internal(g_w_all, j, n_in) for j in range(N_CHIPS)])
    in_out = adamw("adamw_w_in", w_in[0], [reduce_scatter("w_in", g4_in, 128)], m_w_in[0], v_w_in[0], 128)
    g4_out = g_w_out.reshape(N_CHIPS, w_out.shape[1], D_MODEL)
    out_out = adamw("adamw_w_out", w_out[0], [reduce_scatter("w_out", g4_out, 128)], m_w_out[0], v_w_out[0], 128)

    res = [loss, d_x[None]]
    for k in range(4):
        s = split_small(small_out[k])
        big = [ada_out[k][None], in_out[k][None], out_out[k][None]]
        res += [s[0], big[0], s[1], big[1], s[7], s[4], s[5], s[3], s[8], s[9], s[6], big[2], s[2]]
    return tuple(res)
```

```python
import functools

import jax
import jax.numpy as jnp
from jax import lax
from jax.experimental import pallas as pl
from jax.experimental.pallas import tpu as pltpu

F32 = jnp.float32
MXU_DTYPE = jnp.bfloat16
HIGHEST = lax.Precision.HIGHEST
SOLVE_PRECISION = lax.Precision.HIGH
MESH_ID = pl.DeviceIdType.MESH

D_MODEL = 2048
N_HEADS = 8
D_HEAD = 128
WIDTH = N_HEADS * D_HEAD
CHUNK = 64
CONV_K = 4
EPS = 1e-6
NEG = -1e30
ATT_SCALE = D_HEAD ** -0.5
N_SMALL = 3 * N_HEADS
P_COLS = 8 * WIDTH + 128
CB_FQ, CB_FK, CB_FV, CB_FZ, CB_GQ, CB_GK, CB_GV, CB_GZ, CB_SMALL = 0, 8, 16, 24, 32, 40, 48, 56, 64
IN_WIDTH = 8 * WIDTH + N_SMALL
N_CHIPS = 4
N_DEV = 8

ADAM_LR, ADAM_B1, ADAM_B2, ADAM_EPS, ADAM_WD, ADAM_STEP = 0.001, 0.9, 0.999, 1e-08, 0.01, 10


def _dotg(a, b, dims):
    return lax.dot_general(a.astype(MXU_DTYPE), b.astype(MXU_DTYPE), (dims, ((), ())), preferred_element_type=F32)


@jax.custom_vjp
def mm_nn(a, b):
    return _dotg(a, b, ((1,), (0,)))


@jax.custom_vjp
def mm_nt(a, b):
    return _dotg(a, b, ((1,), (1,)))


@jax.custom_vjp
def mm_tn(a, b):
    return _dotg(a, b, ((0,), (0,)))


mm_nn.defvjp(lambda a, b: (mm_nn(a, b), (a, b)), lambda r, g: (mm_nt(g, r[1]), mm_tn(r[0], g)))
mm_nt.defvjp(lambda a, b: (mm_nt(a, b), (a, b)), lambda r, g: (mm_nn(g, r[1]), mm_tn(g, r[0])))
mm_tn.defvjp(lambda a, b: (mm_tn(a, b), (a, b)), lambda r, g: (mm_nt(r[1], g), mm_nn(r[0], g)))


def hdot(a, b):
    return lax.dot_general(a, b, (((1,), (0,)), ((), ())), precision=HIGHEST, preferred_element_type=F32)


def _sigmoid(x):
    return 0.5 * (jnp.tanh(0.5 * x) + 1.0)


def _silu(x):
    return x * _sigmoid(x)


def _softplus(x):
    return jnp.maximum(x, 0.0) + jnp.log(1.0 + jnp.exp(-jnp.abs(x)))


def _log_sigmoid(x):
    return jnp.minimum(x, 0.0) - jnp.log(1.0 + jnp.exp(-jnp.abs(x)))


def _rms(x, g):
    return x * lax.rsqrt(jnp.mean(x * x, axis=-1, keepdims=True) + EPS) * g


def _l2n(x):
    return x * lax.rsqrt(jnp.sum(x * x, axis=-1, keepdims=True) + EPS)


def _tile(n, pref):
    t = min(n, pref)
    assert n % t == 0, (n, pref)
    return t


def _bspec(kind, w, col0, ts):
    if kind == "row":
        return pl.BlockSpec((ts, w), lambda i, h: (i, col0))
    if kind == "head":
        return pl.BlockSpec((ts, w), lambda i, h: (i, col0 + h))
    assert kind == "head3"
    return pl.BlockSpec((1, ts, w), lambda i, h: (h, i, 0))


def _ld(ref, kind):
    return ref[0] if kind == "head3" else ref[...]


def _st(ref, kind, val):
    if kind == "head3":
        ref[0] = val.astype(ref.dtype)
    else:
        ref[...] = val.astype(ref.dtype)


def _full_spec(a):
    nd = a.ndim
    return pl.BlockSpec(a.shape, lambda i, h: (0,) * nd)


def _out_struct(kind, w, S, width_total, dtype, nh):
    if kind == "head3":
        return jax.ShapeDtypeStruct((nh, S, w), dtype)
    return jax.ShapeDtypeStruct((S, width_total), dtype)


def rowwise_fwd(name, f, S, ts, nh, ins, params, outs):
    n_in, n_p = len(ins), len(params)
    into = {j: o[5] for j, o in enumerate(outs) if len(o) > 5}
    outs = [o[:5] for o in outs]
    n_al = len(into)

    def body(*refs):
        vals = [_ld(r, s[1]).astype(F32) for r, s in zip(refs[:n_in], ins)]
        pv = [r[...] for r in refs[n_in:n_in + n_p]]
        res = f(*vals, *pv)
        for r, s, o in zip(refs[n_in + n_p + n_al:], outs, res):
            _st(r, s[0], o)

    return pl.pallas_call(
        body, name=name, grid=(S // ts, nh),
        in_specs=[_bspec(k, w, c0, ts) for (_, k, w, c0) in ins] + [_full_spec(p) for p in params]
        + [pl.BlockSpec(memory_space=pl.ANY)] * n_al,
        out_specs=[_bspec(k, w, c0, ts) for (k, w, c0, _, _) in outs],
        out_shape=[jax.ShapeDtypeStruct(into[j].shape, into[j].dtype) if j in into else _out_struct(k, w, S, tw, dt, nh)
                   for j, (k, w, c0, tw, dt) in enumerate(outs)],
        input_output_aliases={n_in + n_p + n: j for n, j in enumerate(sorted(into))},
    )(*[a for (a, _, _, _) in ins], *params, *[into[j] for j in sorted(into)])


def rowwise_bwd(name, f, S, ts, nh, ins, params, cts, row_grads, extra=None, extra_outs=(), primal_sum=False):
    n_in, n_p, n_ct = len(ins), len(params), len(cts)
    n_rg, n_ex = len(row_grads), len(extra_outs)
    into = {j: rg[6] for j, rg in enumerate(row_grads) if len(rg) > 6}
    row_grads = [rg[:6] for rg in row_grads]
    n_al = len(into)

    def body(*refs):
        first = (pl.program_id(0) == 0) & (pl.program_id(1) == 0)
        in_refs = refs[:n_in]
        p_refs = refs[n_in:n_in + n_p]
        ct_refs = refs[n_in + n_p:n_in + n_p + n_ct]
        o = n_in + n_p + n_ct + n_al
        rg_refs = refs[o:o + n_rg]
        ex_refs = refs[o + n_rg:o + n_rg + n_ex]
        acc_refs = refs[o + n_rg + n_ex:]
        vals = [_ld(r, s[1]).astype(F32) for r, s in zip(in_refs, ins)]
        pv = [r[...] for r in p_refs]
        res, vjp = jax.vjp(f, *vals, *pv)
        if primal_sum:
            cv = [jnp.ones_like(res[0])] + [_ld(r, s[1]).astype(F32) for r, s in zip(ct_refs, cts)]
        else:
            cv = [_ld(r, s[1]).astype(F32) for r, s in zip(ct_refs, cts)]
        grads = vjp(tuple(cv))
        for r, s in zip(rg_refs, row_grads):
            _st(r, s[1], grads[s[0]])
        if extra is not None:
            for r, s, e in zip(ex_refs, extra_outs, extra(vals, cv, grads)):
                _st(r, s[0], e)

        @pl.when(first)
        def _():
            for r in acc_refs:
                r[...] = jnp.zeros_like(r)

        for r, g in zip(acc_refs[:n_p], grads[n_in:]):
            r[...] += g
        if primal_sum:
            acc_refs[n_p][...] += jnp.sum(res[0])

    acc_shapes = [jax.ShapeDtypeStruct(p.shape, F32) for p in params]
    acc_specs = [_full_spec(p) for p in params]
    if primal_sum:
        acc_shapes.append(jax.ShapeDtypeStruct((1, 128), F32))
        acc_specs.append(pl.BlockSpec((1, 128), lambda i, h: (0, 0)))
    rg_shapes = [jax.ShapeDtypeStruct(into[j].shape, into[j].dtype) if j in into else _out_struct(k, w, S, tw, dt, nh)
                 for j, (_, k, w, c0, tw, dt) in enumerate(row_grads)]
    first_alias = n_in + n_p + n_ct
    return pl.pallas_call(
        body, name=name, grid=(S // ts, nh),
        in_specs=[_bspec(k, w, c0, ts) for (_, k, w, c0) in ins] + [_full_spec(p) for p in params]
        + [_bspec(k, w, c0, ts) for (_, k, w, c0) in cts] + [pl.BlockSpec(memory_space=pl.ANY)] * n_al,
        out_specs=[_bspec(k, w, c0, ts) for (_, k, w, c0, _, _) in row_grads]
        + [_bspec(k, w, c0, ts) for (k, w, c0, _, _) in extra_outs] + acc_specs,
        out_shape=rg_shapes + [_out_struct(k, w, S, tw, dt, nh) for (k, w, c0, tw, dt) in extra_outs] + acc_shapes,
        input_output_aliases={first_alias + n: j for n, j in enumerate(sorted(into))},
    )(*[a for (a, _, _, _) in ins], *params, *[a for (a, _, _, _) in cts], *[into[j] for j in sorted(into)])


def f_prologue(x, ng, sc, sh):
    return (_rms(x, ng) * (1.0 + sc) + sh,)


def f_prologue_res(x, ng, sc, sh):
    return (_rms(x, ng) * (1.0 + sc) + sh, x)


def f_loss(x, mo, tgt, gate, fg):
    y = _rms(x + gate * mo, fg)
    return (0.5 * jnp.mean(jnp.square(y - tgt), axis=-1, keepdims=True),)


def f_foxpre(fq, fk, fv, qg, kg):
    return (_rms(fq, qg) * ATT_SCALE, _rms(fk, kg), fv)


def f_rms1(x, g):
    return (_rms(x, g),)


def f_postf(o, z):
    return (o * _silu(z),)


def f_postg(o, z, ng):
    return (_rms(o, ng) * _silu(z),)


def f_gdnpre(yq, yk, yv):
    return (_l2n(_silu(yq)) * ATT_SCALE, _l2n(_silu(yk)), _silu(yv))


def _chip_exchange(kind, x_ref, out_ref, send_sems, recv_sems):
    x, y, c = lax.axis_index("x"), lax.axis_index("y"), lax.axis_index("c")
    sends, arrivals = [], []
    for r in range(1, N_CHIPS):
        px, py = _flip(x, r & 2), _flip(y, r & 1)
        if kind == "scatter":
            src, dst, landed = x_ref.at[2 * px + py], out_ref.at[r - 1], out_ref.at[r - 1]
        else:
            src, dst, landed = x_ref, out_ref.at[2 * x + y], out_ref.at[2 * px + py]
        mk = functools.partial(pltpu.make_async_remote_copy, send_sem=send_sems.at[r - 1], recv_sem=recv_sems.at[r - 1],
                               device_id=(px, py, c), device_id_type=MESH_ID)
        sends.append(mk(src_ref=src, dst_ref=dst))
        arrivals.append(mk(src_ref=src, dst_ref=landed))
    return sends, arrivals


def _exchange_shape(kind, x):
    return jax.ShapeDtypeStruct(((N_CHIPS - 1,) + x.shape[1:]) if kind == "scatter" else ((N_CHIPS,) + x.shape), x.dtype)


def matmul(name, a, b, out_dtype, tm=512, tn=512, tk=2048, trans_a=False, trans_b=False, rides=()):
    M, K = a.shape[::-1] if trans_a else a.shape
    N, K2 = b.shape if trans_b else b.shape[::-1]
    assert K == K2
    tm, tn, tk = _tile(M, tm), _tile(N, tn), _tile(K, tk)
    ni, nj, nk = M // tm, N // tn, K // tk
    a_spec = pl.BlockSpec((tk, tm), lambda i, j, k: (k, i)) if trans_a else pl.BlockSpec((tm, tk), lambda i, j, k: (i, k))
    b_spec = pl.BlockSpec((tn, tk), lambda i, j, k: (j, k)) if trans_b else pl.BlockSpec((tk, tn), lambda i, j, k: (k, j))
    n_ride = len(rides)

    def body(a_ref, b_ref, *rest):
        ride_in, o_ref, ride_out = rest[:n_ride], rest[n_ride], rest[n_ride + 1:2 * n_ride + 1]
        scr = rest[2 * n_ride + 1:]
        i, j, k = pl.program_id(0), pl.program_id(1), pl.program_id(2)
        exchanges = [_chip_exchange(kind, ride_in[n], ride_out[n], scr[2 * n], scr[2 * n + 1])
                     for n, (kind, _) in enumerate(rides)]

        @pl.when((i == 0) & (j == 0) & (k == 0))
        def _():
            for sends, _ in exchanges:
                for cp in sends:
                    cp.start()

        part = lax.dot_general(a_ref[...], b_ref[...], (((0 if trans_a else 1,), (1 if trans_b else 0,)), ((), ())),
                               preferred_element_type=F32)
        if nk == 1:
            o_ref[...] = part.astype(o_ref.dtype)
        else:
            acc = scr[2 * n_ride]

            @pl.when(k == 0)
            def _():
                acc[...] = part

            @pl.when(k > 0)
            def _():
                acc[...] += part

            @pl.when(k == nk - 1)
            def _():
                o_ref[...] = acc[...].astype(o_ref.dtype)

        @pl.when((i == ni - 1) & (j == nj - 1) & (k == nk - 1))
        def _():
            for sends, arrivals in exchanges:
                for cp in arrivals:
                    cp.wait_recv()
                for cp in sends:
                    cp.wait_send()

    sems = [pltpu.SemaphoreType.DMA((N_CHIPS - 1,))] * (2 * n_ride)
    any_spec = pl.BlockSpec(memory_space=pl.ANY)
    res = pl.pallas_call(
        body, name=name, grid=(ni, nj, nk),
        in_specs=[a_spec, b_spec] + [any_spec] * n_ride,
        out_specs=[pl.BlockSpec((tm, tn), lambda i, j, k: (i, j))] + [any_spec] * n_ride,
        out_shape=[jax.ShapeDtypeStruct((M, N), out_dtype)] + [_exchange_shape(kind, x) for kind, x in rides],
        scratch_shapes=sems + ([] if nk == 1 else [pltpu.VMEM((tm, tn), F32)]),
        compiler_params=pltpu.CompilerParams(
            dimension_semantics=("arbitrary",) * 3 if rides else ("parallel", "parallel", "arbitrary")),
    )(a, b, *[x for _, x in rides])
    return tuple(res) if rides else res[0]


def _small_f(z, pv, av):
    lane = lax.broadcasted_iota(jnp.int32, z.shape, 1)
    zz = z + pv
    logf = _log_sigmoid(zz)
    gval = -jnp.exp(av) * _softplus(zz)
    beta = _sigmoid(zz)
    return jnp.where(lane < 8, logf, jnp.where(lane < 16, gval, jnp.where(lane < 24, beta, 0.0)))


def _tri(ts, upper):
    r = lax.broadcasted_iota(jnp.int32, (ts, ts), 0)
    c = lax.broadcasted_iota(jnp.int32, (ts, ts), 1)
    keep = (r <= c) if upper else (r >= c)
    same_chunk = (r // CHUNK) == (c // CHUNK)
    return keep.astype(F32), (keep & same_chunk).astype(F32)


def smalls_fwd(p_all, pv, av, S, ts):
    nt = S // ts

    def body(z_ref, pv_ref, av_ref, o_ref, carry):
        i = pl.program_id(0)

        @pl.when(i == 0)
        def _():
            carry[...] = jnp.zeros_like(carry)

        e = _small_f(z_ref[...], pv_ref[...], av_ref[...])
        lane = lax.broadcasted_iota(jnp.int32, e.shape, 1)
        l_all, l_chunk = _tri(ts, upper=False)
        cum_all = hdot(l_all, e) + carry[...]
        cum_chunk = hdot(l_chunk, e)
        carry[...] = cum_all[ts - 1:ts, :]
        o_ref[...] = jnp.where(lane < 8, cum_all, jnp.where(lane < 16, cum_chunk, e))

    return pl.pallas_call(
        body, name="smalls_fwd", grid=(nt,),
        in_specs=[pl.BlockSpec((ts, 128), lambda i: (i, CB_SMALL)), pl.BlockSpec((1, 128), lambda i: (0, 0)),
                  pl.BlockSpec((1, 128), lambda i: (0, 0))],
        out_specs=pl.BlockSpec((ts, 128), lambda i: (i, 0)),
        out_shape=jax.ShapeDtypeStruct((S, 128), F32),
        scratch_shapes=[pltpu.VMEM((1, 128), F32)],
    )(p_all, pv, av)


def smalls_bwd(p_all, pv, av, dsm, dp, S, ts):
    nt = S // ts

    def body(z_ref, pv_ref, av_ref, d_ref, dp_in, dz_ref, dpv_ref, dav_ref, carry):
        i = pl.program_id(0)

        @pl.when(i == 0)
        def _():
            carry[...] = jnp.zeros_like(carry)
            dpv_ref[...] = jnp.zeros_like(dpv_ref)
            dav_ref[...] = jnp.zeros_like(dav_ref)

        d = d_ref[...]
        lane = lax.broadcasted_iota(jnp.int32, d.shape, 1)
        u_all, u_chunk = _tri(ts, upper=True)
        rev_all = hdot(u_all, d) + carry[...]
        rev_chunk = hdot(u_chunk, d)
        carry[...] = rev_all[0:1, :]
        de = jnp.where(lane < 8, rev_all, jnp.where(lane < 16, rev_chunk, d))
        _, vjp = jax.vjp(_small_f, z_ref[...], pv_ref[...], av_ref[...])
        dz, dpv, dav = vjp(de)
        dz_ref[...] = dz.astype(dz_ref.dtype)
        dpv_ref[...] += dpv
        dav_ref[...] += dav

    rev = lambda i: (nt - 1 - i, 0)
    small_cols = pl.BlockSpec((ts, 128), lambda i: (nt - 1 - i, CB_SMALL))
    return pl.pallas_call(
        body, name="smalls_bwd", grid=(nt,),
        in_specs=[small_cols, pl.BlockSpec((1, 128), lambda i: (0, 0)), pl.BlockSpec((1, 128), lambda i: (0, 0)),
                  pl.BlockSpec((ts, 128), rev), pl.BlockSpec(memory_space=pl.ANY)],
        out_specs=[small_cols, pl.BlockSpec((1, 128), lambda i: (0, 0)), pl.BlockSpec((1, 128), lambda i: (0, 0))],
        out_shape=[jax.ShapeDtypeStruct(dp.shape, dp.dtype), jax.ShapeDtypeStruct((1, 128), F32),
                   jax.ShapeDtypeStruct((1, 128), F32)],
        scratch_shapes=[pltpu.VMEM((1, 128), F32)],
        input_output_aliases={4: 0},
    )(p_all, pv, av, dsm, dp)


def _col_to_row(col):
    return jnp.transpose(jnp.broadcast_to(col, (col.shape[0], 128)))[0:1, :]


def _row_to_col(row):
    return jnp.transpose(jnp.broadcast_to(row, (128, row.shape[1])))[:, 0:1]


def _causal_mask(s, keys_first):
    r = lax.broadcasted_iota(jnp.int32, s.shape, 0)
    c = lax.broadcasted_iota(jnp.int32, s.shape, 1)
    return jnp.where((r <= c) if keys_first else (c <= r), s, NEG)


def flash_fwd(qs, kn, vb, frow, S, tq):
    nq = S // tq
    tk = tq

    def body(q_ref, k_ref, v_ref, fr_ref, o_ref, lse_ref):
        i = pl.program_id(1)
        q = q_ref[...]
        f_base = fr_ref[0, i][:, 0:1]

        def tile(j, carry, diagonal):
            m, l, acc = carry
            off = pl.multiple_of(j * tk, tk)
            k = k_ref[pl.ds(off, tk), :]
            v = v_ref[pl.ds(off, tk), :]
            s = mm_nt(q, k) - (fr_ref[0, j] - f_base)
            if diagonal:
                s = _causal_mask(s, keys_first=False)
            m_new = jnp.maximum(m, jnp.max(s, axis=-1, keepdims=True))
            a = jnp.exp(m - m_new)
            p = jnp.exp(s - m_new)
            l = a * l + jnp.sum(p, axis=-1, keepdims=True)
            acc = a * acc + mm_nn(p, v)
            return m_new, l, acc

        init = (jnp.full((tq, 1), NEG, F32), jnp.zeros((tq, 1), F32), jnp.zeros((tq, D_HEAD), F32))
        carry = lax.fori_loop(0, i, lambda j, c: tile(j, c, False), init)
        m, l, acc = tile(i, carry, True)
        o_ref[...] = acc / l
        lse_ref[0, 0] = _col_to_row(m + jnp.log(l))

    return pl.pallas_call(
        body, name="flash_fwd", grid=(N_HEADS, nq),
        in_specs=[pl.BlockSpec((tq, D_HEAD), lambda h, i: (i, h)), pl.BlockSpec((S, D_HEAD), lambda h, i: (0, h)),
                  pl.BlockSpec((S, D_HEAD), lambda h, i: (0, h)),
                  pl.BlockSpec((1, nq, 1, tk), lambda h, i: (h, 0, 0, 0))],
        out_specs=[pl.BlockSpec((tq, D_HEAD), lambda h, i: (i, h)),
                   pl.BlockSpec((1, 1, 1, tq), lambda h, i: (h, i, 0, 0))],
        out_shape=[jax.ShapeDtypeStruct((S, WIDTH), F32), jax.ShapeDtypeStruct((N_HEADS, nq, 1, tq), F32)],
    )(qs, kn, vb, frow)


def flash_delta(do, o, S, tq):
    nq = S // tq
    per = _tile(nq, 4)

    def body(do_ref, o_ref, dl_ref):
        for n in range(per):
            rows = pl.ds(n * tq, tq)
            dl_ref[0, n] = _col_to_row(jnp.sum(do_ref[rows, :].astype(F32) * o_ref[rows, :], axis=-1, keepdims=True))

    blk = pl.BlockSpec((per * tq, D_HEAD), lambda h, i: (i, h))
    return pl.pallas_call(
        body, name="flash_delta", grid=(N_HEADS, nq // per), in_specs=[blk, blk],
        out_specs=pl.BlockSpec((1, per, 1, tq), lambda h, i: (h, i, 0, 0)),
        out_shape=jax.ShapeDtypeStruct((N_HEADS, nq, 1, tq), F32),
    )(do, o)


def flash_bwd(qs, kn, vb, do, frow, lse_row, delta_row, dp, S, tq):
    nq = S // tq
    tk = tq

    def body(q_ref, k_ref, v_ref, do_ref, fr_ref, lse_ref, dl_ref, dp_in, dq_ref, dk_ref, dv_ref, dfq_ref, dfk_ref):
        j = pl.program_id(1)
        k = k_ref[...]
        v = v_ref[...]
        fk = _row_to_col(fr_ref[0, j])

        @pl.when(j == 0)
        def _():
            dq_ref[...] = jnp.zeros_like(dq_ref)
            dfq_ref[...] = jnp.zeros_like(dfq_ref)

        def tile(i, carry, diagonal):
            dk, dv, dfk = carry
            off = pl.multiple_of(i * tq, tq)
            q = q_ref[pl.ds(off, tq), :]
            do_ = do_ref[pl.ds(off, tq), :]
            st = mm_nt(k, q) - (fk - fr_ref[0, i][:, 0:1])
            if diagonal:
                st = _causal_mask(st, keys_first=True)
            pt = jnp.exp(st - lse_ref[0, i])
            dst = pt * (mm_nt(v, do_) - dl_ref[0, i])
            dq_ref[pl.ds(off, tq), :] += mm_tn(dst, k) * ATT_SCALE
            dfq_ref[0, i] += jnp.sum(dst, axis=0, keepdims=True)
            return dk + mm_nn(dst, q), dv + mm_nn(pt, do_), dfk - jnp.sum(dst, axis=-1, keepdims=True)

        z = jnp.zeros((tk, D_HEAD), F32)
        carry = tile(j, (z, z, jnp.zeros((tk, 1), F32)), True)
        dk, dv, dfk = lax.fori_loop(j + 1, nq, lambda i, c: tile(i, c, False), carry)
        dk_ref[...] = dk
        dv_ref[...] = dv.astype(dv_ref.dtype)
        dfk_ref[0, 0] = _col_to_row(dfk)

    blk = pl.BlockSpec((tk, D_HEAD), lambda h, j: (j, h))
    full = pl.BlockSpec((S, D_HEAD), lambda h, j: (0, h))
    rows = pl.BlockSpec((1, nq, 1, tq), lambda h, j: (h, 0, 0, 0))
    stat = jax.ShapeDtypeStruct((N_HEADS, nq, 1, tq), F32)
    return pl.pallas_call(
        body, name="flash_bwd", grid=(N_HEADS, nq),
        in_specs=[full, blk, blk, full, rows, rows, rows, pl.BlockSpec(memory_space=pl.ANY)],
        out_specs=[full, blk, pl.BlockSpec((tk, D_HEAD), lambda h, j: (j, CB_FV + h)), rows,
                   pl.BlockSpec((1, 1, 1, tk), lambda h, j: (h, j, 0, 0))],
        out_shape=[jax.ShapeDtypeStruct((S, WIDTH), F32), jax.ShapeDtypeStruct((S, WIDTH), F32),
                   jax.ShapeDtypeStruct(dp.shape, dp.dtype), stat, stat],
        input_output_aliases={7: 2},
    )(qs, kn, vb, do, frow, lse_row, delta_row, dp)


CONV_COLS = 3 * WIDTH
CONV_TC = 512


def conv_fwd(p_all, conv_w, S, ts):
    nt, tc = S // ts, CONV_TC
    cb0 = CB_GQ * 128 // tc
    r8 = ts // 8

    def body(x_ref, halo_ref, w_ref, y_ref, scr):
        i = pl.program_id(1)
        scr[0:8, :] = jnp.where(i > 0, halo_ref[...], 0.0)
        scr[8:8 + ts, :] = x_ref[...]
        w = w_ref[...]
        y = w[3:4, :] * x_ref[...]
        for j in range(CONV_K - 1):
            y = y + w[j:j + 1, :] * scr[5 + j:5 + j + ts, :]
        y_ref[...] = y

    return pl.pallas_call(
        body, name="conv_fwd", grid=(CONV_COLS // tc, nt),
        in_specs=[pl.BlockSpec((ts, tc), lambda c, i: (i, cb0 + c)),
                  pl.BlockSpec((8, tc), lambda c, i: (jnp.maximum(i * r8 - 1, 0), cb0 + c)),
                  pl.BlockSpec((CONV_K, tc), lambda c, i: (0, c))],
        out_specs=pl.BlockSpec((ts, tc), lambda c, i: (i, c)),
        out_shape=jax.ShapeDtypeStruct((S, CONV_COLS), F32),
        scratch_shapes=[pltpu.VMEM((ts + 8, tc), F32)],
    )(p_all, p_all, conv_w)


def conv_bwd(p_all, dy, conv_w, dp, S, ts):
    nt, tc = S // ts, CONV_TC
    cb0 = CB_GQ * 128 // tc
    r8 = ts // 8

    def body(x_ref, xh_ref, dy_ref, dyh_ref, w_ref, dp_in, dx_ref, dw_ref, xs, ds):
        i = pl.program_id(1)
        xs[0:8, :] = jnp.where(i > 0, xh_ref[...], 0.0)
        xs[8:8 + ts, :] = x_ref[...]
        ds[0:ts, :] = dy_ref[...]
        ds[ts:ts + 8, :] = jnp.where(i < nt - 1, dyh_ref[...], 0.0)
        w = w_ref[...]
        d = dy_ref[...]
        dx = w[3:4, :] * d
        for j in range(CONV_K - 1):
            dx = dx + w[j:j + 1, :] * ds[3 - j:3 - j + ts, :]
        dx_ref[...] = dx.astype(dx_ref.dtype)

        @pl.when(i == 0)
        def _():
            dw_ref[...] = jnp.zeros_like(dw_ref)

        rows = [jnp.sum(d * xs[5 + j:5 + j + ts, :], axis=0, keepdims=True) for j in range(CONV_K - 1)]
        rows.append(jnp.sum(d * x_ref[...], axis=0, keepdims=True))
        dw_ref[...] += jnp.concatenate(rows, axis=0)

    return pl.pallas_call(
        body, name="conv_bwd", grid=(CONV_COLS // tc, nt),
        in_specs=[pl.BlockSpec((ts, tc), lambda c, i: (i, cb0 + c)),
                  pl.BlockSpec((8, tc), lambda c, i: (jnp.maximum(i * r8 - 1, 0), cb0 + c)),
                  pl.BlockSpec((ts, tc), lambda c, i: (i, c)),
                  pl.BlockSpec((8, tc), lambda c, i: (jnp.minimum((i + 1) * r8, nt * r8 - 1), c)),
                  pl.BlockSpec((CONV_K, tc), lambda c, i: (0, c)), pl.BlockSpec(memory_space=pl.ANY)],
        out_specs=[pl.BlockSpec((ts, tc), lambda c, i: (i, cb0 + c)), pl.BlockSpec((CONV_K, tc), lambda c, i: (0, c))],
        out_shape=[jax.ShapeDtypeStruct(dp.shape, dp.dtype), jax.ShapeDtypeStruct((CONV_K, CONV_COLS), F32)],
        scratch_shapes=[pltpu.VMEM((ts + 8, tc), F32), pltpu.VMEM((ts + 8, tc), F32)],
        input_output_aliases={5: 0},
    )(p_all, p_all, dy, dy, conv_w, dp)


def _bdot(a, b, ca, cb):
    return lax.dot_general(a.astype(MXU_DTYPE), b.astype(MXU_DTYPE), (((ca,), (cb,)), ((0,), (0,))),
                           preferred_element_type=F32)


@jax.custom_vjp
def bmm_nn(a, b):
    return _bdot(a, b, 2, 1)


@jax.custom_vjp
def bmm_nt(a, b):
    return _bdot(a, b, 2, 2)


@jax.custom_vjp
def bmm_tn(a, b):
    return _bdot(a, b, 1, 1)


bmm_nn.defvjp(lambda a, b: (bmm_nn(a, b), (a, b)), lambda r, g: (bmm_nt(g, r[1]), bmm_tn(r[0], g)))
bmm_nt.defvjp(lambda a, b: (bmm_nt(a, b), (a, b)), lambda r, g: (bmm_nn(g, r[1]), bmm_tn(g, r[0])))
bmm_tn.defvjp(lambda a, b: (bmm_tn(a, b), (a, b)), lambda r, g: (bmm_nt(r[1], g), bmm_nn(r[0], g)))


def bdot_hi(a, b, ca=2, cb=1):
    return lax.dot_general(a, b, (((ca,), (cb,)), ((0,), (0,))), precision=SOLVE_PRECISION,
                           preferred_element_type=F32)


def _tri_inv_product(m):
    ii = lax.broadcasted_iota(jnp.int32, m.shape, 1)
    jj = lax.broadcasted_iota(jnp.int32, m.shape, 2)
    t = (ii == jj).astype(F32) - m
    pw = bdot_hi(m, m)
    for _ in range(4):
        t = t + bdot_hi(t, pw)
        pw = bdot_hi(pw, pw)
    return t + bdot_hi(t, pw)


def _tri_inv_bwd(t, g):
    return -bdot_hi(bdot_hi(t, g, 1, 1), t, 2, 2)


@jax.custom_vjp
def tri_inv(m):
    return _tri_inv_product(m)


@jax.custom_vjp
def tri_inv_known(m, t):
    return t


tri_inv.defvjp(lambda m: (lambda t: (t, t))(_tri_inv_product(m)), lambda t, g: (_tri_inv_bwd(t, g),))
tri_inv_known.defvjp(lambda m, t: (t, t), lambda t, g: (_tri_inv_bwd(t, g), jnp.zeros_like(t)))


def chunk_fn(q, k, v, gc_col, gc_row, beta, state, t_known=None):
    shape = (N_HEADS, CHUNK, CHUNK)
    ii = lax.broadcasted_iota(jnp.int32, shape, 1)
    jj = lax.broadcasted_iota(jnp.int32, shape, 2)
    lower = ii >= jj
    strict = ii > jj
    decay = jnp.exp(jnp.where(lower, gc_col - gc_row, NEG))
    kb = k * beta
    vb = v * beta
    m = jnp.where(strict, bmm_nt(kb, k) * decay, 0.0)
    t = tri_inv(m) if t_known is None else tri_inv_known(m, t_known)
    u = bdot_hi(t, vb)
    w = bdot_hi(t, kb * jnp.exp(gc_col))
    attn = jnp.where(lower, bmm_nt(q, k) * decay, 0.0)
    v_new = u - bmm_nn(w, state)
    o = bmm_nn(q * jnp.exp(gc_col), state) + bmm_nn(attn, v_new)
    g_last = gc_col[:, CHUNK - 1:CHUNK, :]
    k_dec = k * jnp.exp(g_last - gc_col)
    new_state = state * jnp.exp(g_last) + bmm_tn(k_dec, v_new)
    return (o, new_state, t) if t_known is None else (o, new_state)


def _heads(ref):
    return jnp.stack([ref[:, h * D_HEAD:(h + 1) * D_HEAD] for h in range(N_HEADS)], axis=0)


def _head_cols(ref, lane0):
    return jnp.stack([ref[:, lane0 + h:lane0 + h + 1] for h in range(N_HEADS)], axis=0)


def _qkv_head_cols(h):
    return [slice(g * WIDTH + h * D_HEAD, g * WIDTH + (h + 1) * D_HEAD) for g in range(3)]


def gdn_pre_fwd(y, S, ts):
    def body(y_ref, o_ref):
        for h in range(N_HEADS):
            cols = _qkv_head_cols(h)
            for c, o in zip(cols, f_gdnpre(*[y_ref[:, c] for c in cols])):
                o_ref[:, c] = o

    spec = pl.BlockSpec((ts, CONV_COLS), lambda i: (i, 0))
    return pl.pallas_call(body, name="gdn_pre", grid=(S // ts,), in_specs=[spec], out_specs=spec,
                          out_shape=jax.ShapeDtypeStruct((S, CONV_COLS), F32))(y)


def gdn_pre_bwd(y, dqkv, S, ts):
    def body(y_ref, d_ref, o_ref):
        for h in range(N_HEADS):
            cols = _qkv_head_cols(h)
            _, vjp = jax.vjp(f_gdnpre, *[y_ref[:, c] for c in cols])
            for c, g in zip(cols, vjp(tuple(d_ref[:, c] for c in cols))):
                o_ref[:, c] = g

    spec = pl.BlockSpec((ts, CONV_COLS), lambda i: (i, 0))
    return pl.pallas_call(body, name="gdn_pre_bwd", grid=(S // ts,), in_specs=[spec, spec], out_specs=spec,
                          out_shape=jax.ShapeDtypeStruct((S, CONV_COLS), F32))(y, dqkv)


def gdn_fwd(qkv, smalls, gc_row, S):
    n = S // CHUNK

    def body(q_ref, k_ref, v_ref, sm_ref, gr_ref, o_ref, st_ref, t_ref, state):
        @pl.when(pl.program_id(0) == 0)
        def _():
            state[...] = jnp.zeros_like(state)

        s0 = state[...]
        st_ref[0] = s0
        o, s1, t = chunk_fn(_heads(q_ref), _heads(k_ref), _heads(v_ref), _head_cols(sm_ref, 8),
                            gr_ref[0][:, None, :], _head_cols(sm_ref, 16), s0)
        for h in range(N_HEADS):
            o_ref[:, h * D_HEAD:(h + 1) * D_HEAD] = o[h]
        state[...] = s1
        t_ref[0] = t

    return pl.pallas_call(
        body, name="gdn_chunk_fwd", grid=(n,),
        in_specs=[pl.BlockSpec((CHUNK, WIDTH), lambda i, g=g: (i, g)) for g in range(3)]
        + [pl.BlockSpec((CHUNK, 128), lambda i: (i, 0)), pl.BlockSpec((1, N_HEADS, CHUNK), lambda i: (i, 0, 0))],
        out_specs=[pl.BlockSpec((CHUNK, WIDTH), lambda i: (i, 0)),
                   pl.BlockSpec((1, N_HEADS, D_HEAD, D_HEAD), lambda i: (i, 0, 0, 0)),
                   pl.BlockSpec((1, N_HEADS, CHUNK, CHUNK), lambda i: (i, 0, 0, 0))],
        out_shape=[jax.ShapeDtypeStruct((S, WIDTH), F32), jax.ShapeDtypeStruct((n, N_HEADS, D_HEAD, D_HEAD), F32),
                   jax.ShapeDtypeStruct((n, N_HEADS, CHUNK, CHUNK), F32)],
        scratch_shapes=[pltpu.VMEM((N_HEADS, D_HEAD, D_HEAD), F32)],
    )(qkv, qkv, qkv, smalls, gc_row)


def gdn_bwd(qkv, smalls, gc_row, states, tinv, do, S):
    n = S // CHUNK

    def body(q_ref, k_ref, v_ref, sm_ref, gr_ref, st_ref, t_ref, do_ref, dqkv_ref, dsm_ref, dgr_ref, dstate):
        @pl.when(pl.program_id(0) == 0)
        def _():
            dstate[...] = jnp.zeros_like(dstate)

        t_saved = t_ref[0]
        _, vjp = jax.vjp(lambda *a: chunk_fn(*a, t_known=t_saved), _heads(q_ref), _heads(k_ref), _heads(v_ref),
                         _head_cols(sm_ref, 8), gr_ref[0][:, None, :], _head_cols(sm_ref, 16), st_ref[0])
        dq, dk, dv, dgc, dgr, dbt, ds = vjp((_heads(do_ref), dstate[...]))
        for h in range(N_HEADS):
            dqkv_ref[:, h * D_HEAD:(h + 1) * D_HEAD] = dq[h]
            dqkv_ref[:, WIDTH + h * D_HEAD:WIDTH + (h + 1) * D_HEAD] = dk[h]
            dqkv_ref[:, 2 * WIDTH + h * D_HEAD:2 * WIDTH + (h + 1) * D_HEAD] = dv[h]
        dstate[...] = ds
        cols = [dgc[h] for h in range(N_HEADS)] + [dbt[h] for h in range(N_HEADS)]
        dsm_ref[...] = jnp.concatenate([jnp.zeros((CHUNK, 8), F32)] + cols + [jnp.zeros((CHUNK, 128 - 24), F32)],
                                       axis=1)
        dgr_ref[0] = jnp.concatenate([dgr[h] for h in range(N_HEADS)], axis=0)

    rev = lambda c: (lambda i: (n - 1 - i, c))
    return pl.pallas_call(
        body, name="gdn_chunk_bwd", grid=(n,),
        in_specs=[pl.BlockSpec((CHUNK, WIDTH), rev(g)) for g in range(3)]
        + [pl.BlockSpec((CHUNK, 128), rev(0)), pl.BlockSpec((1, N_HEADS, CHUNK), lambda i: (n - 1 - i, 0, 0)),
           pl.BlockSpec((1, N_HEADS, D_HEAD, D_HEAD), lambda i: (n - 1 - i, 0, 0, 0)),
           pl.BlockSpec((1, N_HEADS, CHUNK, CHUNK), lambda i: (n - 1 - i, 0, 0, 0)),
           pl.BlockSpec((CHUNK, WIDTH), rev(0))],
        out_specs=[pl.BlockSpec((CHUNK, 3 * WIDTH), rev(0)), pl.BlockSpec((CHUNK, 128), rev(0)),
                   pl.BlockSpec((1, N_HEADS, CHUNK), lambda i: (n - 1 - i, 0, 0))],
        out_shape=[jax.ShapeDtypeStruct((S, 3 * WIDTH), F32)] + [jax.ShapeDtypeStruct((S, 128), F32),
                                                                  jax.ShapeDtypeStruct((n, N_HEADS, CHUNK), F32)],
        scratch_shapes=[pltpu.VMEM((N_HEADS, D_HEAD, D_HEAD), F32)],
    )(qkv, qkv, qkv, smalls, gc_row, states, tinv, do)


def _flip(a, bit):
    return 1 - a if bit else a


def allgather8(name, buf):
    R, W = buf.shape

    def body(x_ref, out_ref, send_sems, recv_sems, local_sem):
        x, y, c = lax.axis_index("x"), lax.axis_index("y"), lax.axis_index("c")

        def slot(px, py, pc):
            return out_ref.at[4 * px + 2 * py + pc]

        mine = pltpu.make_async_copy(x_ref, slot(x, y, c), local_sem)
        mine.start()
        sends = []
        for r in range(1, N_DEV):
            peer = (_flip(x, r & 4), _flip(y, r & 2), _flip(c, r & 1))
            cp = pltpu.make_async_remote_copy(src_ref=x_ref, dst_ref=slot(x, y, c), send_sem=send_sems.at[r - 1],
                                              recv_sem=recv_sems.at[r - 1], device_id=peer, device_id_type=MESH_ID)
            cp.start()
            sends.append(cp)
        for r in range(1, N_DEV):
            peer = (_flip(x, r & 4), _flip(y, r & 2), _flip(c, r & 1))
            pltpu.make_async_remote_copy(src_ref=x_ref, dst_ref=slot(*peer), send_sem=send_sems.at[r - 1],
                                         recv_sem=recv_sems.at[r - 1], device_id=peer,
                                         device_id_type=MESH_ID).wait_recv()
        for cp in sends:
            cp.wait_send()
        mine.wait()

    return pl.pallas_call(
        body, name=name,
        out_shape=jax.ShapeDtypeStruct((N_DEV, R, W), buf.dtype),
        in_specs=[pl.BlockSpec(memory_space=pltpu.VMEM)],
        out_specs=pl.BlockSpec(memory_space=pltpu.VMEM),
        scratch_shapes=[pltpu.SemaphoreType.DMA((N_DEV - 1,)), pltpu.SemaphoreType.DMA((N_DEV - 1,)),
                        pltpu.SemaphoreType.DMA],
    )(buf)


def allgather_chips(name, blk):
    R, W = blk.shape
    half = R // 2
    n_far = N_CHIPS - 1

    def body(x_ref, out_ref, send_sems, recv_sems):
        x, y, c = lax.axis_index("x"), lax.axis_index("y"), lax.axis_index("c")
        sibling = (x, y, 1 - c)
        chips = [(_flip(x, r & 2), _flip(y, r & 1)) for r in range(1, N_CHIPS)]

        def rows(px, py, pc):
            return out_ref.at[2 * px + py, pc]

        def copy(k, dst, to, src):
            return pltpu.make_async_remote_copy(src_ref=src, dst_ref=dst, send_sem=send_sems.at[k],
                                                recv_sem=recv_sems.at[k], device_id=to, device_id_type=MESH_ID)

        my_half = x_ref.at[c]
        first = [copy(j, rows(x, y, c), (*chip, c), my_half) for j, chip in enumerate(chips)]
        for cp in first:
            cp.start()
        passed = [copy(n_far + j, rows(*chip, c), sibling, rows(*chip, c)) for j, chip in enumerate(chips)]
        for j, chip in enumerate(chips):
            copy(j, rows(*chip, c), (*chip, c), my_half).wait_recv()
            passed[j].start()
        for j, chip in enumerate(chips):
            copy(n_far + j, rows(*chip, 1 - c), sibling, my_half).wait_recv()
        for cp in first + passed:
            cp.wait_send()

    far = pl.pallas_call(
        body, name=name,
        out_shape=jax.ShapeDtypeStruct((N_CHIPS, 2, half, W), blk.dtype),
        in_specs=[pl.BlockSpec(memory_space=pl.ANY)],
        out_specs=pl.BlockSpec(memory_space=pl.ANY),
        scratch_shapes=[pltpu.SemaphoreType.DMA((2 * n_far,)), pltpu.SemaphoreType.DMA((2 * n_far,))],
    )(blk.reshape(2, half, W)).reshape(N_CHIPS, R, W)
    own_chip = 2 * lax.axis_index("x") + lax.axis_index("y")
    return lax.dynamic_update_index_in_dim(far, blk, own_chip, axis=0)


def sibling_send_other_half(name, g4):
    n, R, W = g4.shape
    half = R // 2

    def body(x_ref, out_ref, send_sem, recv_sem):
        x, y, c = lax.axis_index("x"), lax.axis_index("y"), lax.axis_index("c")
        cp = pltpu.make_async_remote_copy(src_ref=x_ref.at[:, pl.ds((1 - c) * half, half), :], dst_ref=out_ref,
                                          send_sem=send_sem, recv_sem=recv_sem, device_id=(x, y, 1 - c),
                                          device_id_type=MESH_ID)
        cp.start()
        cp.wait_recv()
        cp.wait_send()

    return pl.pallas_call(
        body, name=name,
        out_shape=jax.ShapeDtypeStruct((n, half, W), g4.dtype),
        in_specs=[pl.BlockSpec(memory_space=pl.ANY)],
        out_specs=pl.BlockSpec(memory_space=pl.ANY),
        scratch_shapes=[pltpu.SemaphoreType.DMA, pltpu.SemaphoreType.DMA],
    )(g4)


def sibling_merge(name, mine):
    def body(x_ref, out_ref, send_sem, recv_sem):
        x, y, c = lax.axis_index("x"), lax.axis_index("y"), lax.axis_index("c")
        cp = pltpu.make_async_remote_copy(src_ref=x_ref, dst_ref=out_ref, send_sem=send_sem, recv_sem=recv_sem,
                                          device_id=(x, y, 1 - c), device_id_type=MESH_ID)
        cp.start()
        cp.wait_recv()
        cp.wait_send()

    other = pl.pallas_call(
        body, name=name,
        out_shape=jax.ShapeDtypeStruct(mine.shape, mine.dtype),
        in_specs=[pl.BlockSpec(memory_space=pl.ANY)],
        out_specs=pl.BlockSpec(memory_space=pl.ANY),
        scratch_shapes=[pltpu.SemaphoreType.DMA, pltpu.SemaphoreType.DMA],
    )(mine)
    first = lax.axis_index("c") == 0
    return jnp.concatenate([jnp.where(first, mine, other), jnp.where(first, other, mine)], axis=0)


def sum_parts(name, own, recv, tr):
    R, W = own.shape
    tr = _tile(R, tr)

    def body(o_ref, r_ref, out_ref):
        acc = o_ref[...]
        for k in range(recv.shape[0]):
            acc = acc + r_ref[k].astype(F32)
        out_ref[...] = acc

    return pl.pallas_call(
        body, name=name, grid=(R // tr,),
        in_specs=[pl.BlockSpec((tr, W), lambda i: (i, 0)), pl.BlockSpec((recv.shape[0], tr, W), lambda i: (0, i, 0))],
        out_specs=pl.BlockSpec((tr, W), lambda i: (i, 0)),
        out_shape=jax.ShapeDtypeStruct((R, W), F32),
    )(own, recv)


def sum_own_half(name, g4, recv, c_idx, tr):
    n, R, W = g4.shape
    half = R // 2
    tr = _tile(half, tr)
    nb = half // tr

    def body(c_ref, g_ref, r_ref, o_ref, ob_ref):
        s = g_ref[0] + r_ref[0]
        o_ref[0] = s
        ob_ref[0] = s.astype(ob_ref.dtype)

    mine = pl.BlockSpec((1, tr, W), lambda j, i, c: (j, i, 0))
    return pl.pallas_call(
        body, name=name,
        grid_spec=pltpu.PrefetchScalarGridSpec(
            num_scalar_prefetch=1, grid=(n, nb),
            in_specs=[pl.BlockSpec((1, tr, W), lambda j, i, c: (j, c[0] * nb + i, 0)), mine],
            out_specs=[mine, mine]),
        out_shape=[jax.ShapeDtypeStruct((n, half, W), F32), jax.ShapeDtypeStruct((n, half, W), MXU_DTYPE)],
    )(c_idx, g4, recv)


def sum_devices(name, g):
    def body(g_ref, o_ref):
        acc = g_ref[0]
        for d in range(1, N_DEV):
            acc = acc + g_ref[d]
        o_ref[...] = acc

    return pl.pallas_call(body, name=name, out_shape=jax.ShapeDtypeStruct(g.shape[1:], F32))(g)


def ada_fwd(c_all, w_ada):
    K, N = w_ada.shape
    tn = _tile(N, 512)

    def body(c_ref, w_ref, o_ref):
        o_ref[...] = mm_nn(_silu(c_ref[...]), w_ref[...])

    return pl.pallas_call(
        body, name="ada_fwd", grid=(N // tn,),
        in_specs=[pl.BlockSpec((N_DEV, K), lambda j: (0, 0)), pl.BlockSpec((K, tn), lambda j: (0, j))],
        out_specs=pl.BlockSpec((N_DEV, tn), lambda j: (0, j)),
        out_shape=jax.ShapeDtypeStruct((N_DEV, N), F32),
    )(c_all, w_ada)


def ada_grad(c_all, dmod):
    K = c_all.shape[1]
    N = dmod.shape[1]
    tn = _tile(N, 512)

    def body(c_ref, d_ref, o_ref):
        o_ref[...] = mm_tn(_silu(c_ref[...]), d_ref[...])

    return pl.pallas_call(
        body, name="ada_grad", grid=(N // tn,),
        in_specs=[pl.BlockSpec((N_DEV, K), lambda j: (0, 0)), pl.BlockSpec((N_DEV, tn), lambda j: (0, j))],
        out_specs=pl.BlockSpec((K, tn), lambda j: (0, j)),
        out_shape=jax.ShapeDtypeStruct((K, N), F32),
    )(c_all, dmod)


def adamw(name, w, gparts, m, v, tr):
    R, W = w.shape
    tr = _tile(R, tr)
    ng = len(gparts)

    def body(w_ref, *refs):
        g_refs, (m_ref, v_ref, g_out, d_out, m_out, v_out) = refs[:ng], refs[ng:]
        g = g_refs[0][...]
        for r in g_refs[1:]:
            g = g + r[...]
        m1 = ADAM_B1 * m_ref[...] + (1.0 - ADAM_B1) * g
        v1 = ADAM_B2 * v_ref[...] + (1.0 - ADAM_B2) * jnp.square(g)
        m_hat = m1 / (1.0 - ADAM_B1 ** ADAM_STEP)
        v_hat = v1 / (1.0 - ADAM_B2 ** ADAM_STEP)
        g_out[...] = g
        d_out[...] = -ADAM_LR * (m_hat / (jnp.sqrt(v_hat) + ADAM_EPS) + ADAM_WD * w_ref[...])
        m_out[...] = m1
        v_out[...] = v1

    spec = pl.BlockSpec((tr, W), lambda i: (i, 0))
    return pl.pallas_call(
        body, name=name, grid=(R // tr,),
        in_specs=[spec] * (3 + ng), out_specs=[spec] * 4,
        out_shape=[jax.ShapeDtypeStruct((R, W), F32)] * 4,
    )(w, *gparts, m, v)


_REF_SEGMENTS = ((0, 4 * WIDTH, 0), (4 * WIDTH, 4 * WIDTH + 8, 4 * WIDTH), (4 * WIDTH + 8, 8 * WIDTH + 8, -8),
                 (8 * WIDTH + 8, IN_WIDTH, 0))


def _internal_from_blocks(blocks, n):
    a = 4 * WIDTH
    pieces = []
    for lo, hi in ((0, a), (a + 8, 2 * a + 8), (a, a + 8), (2 * a + 8, IN_WIDTH)):
        for j in range(N_CHIPS):
            s, e = max(lo, j * n), min(hi, (j + 1) * n)
            if s < e:
                pieces.append(blocks[j][:, s - j * n:e - j * n])
    pieces.append(jnp.zeros(blocks.shape[1:2] + (128 - N_SMALL,), blocks.dtype))
    return jnp.concatenate(pieces, axis=1)


def _block_from_internal(g, j, n):
    pieces = []
    for lo, hi, shift in _REF_SEGMENTS:
        s, e = max(lo, j * n), min(hi, (j + 1) * n)
        if s < e:
            pieces.append(g[:, s + shift:e + shift])
    return jnp.concatenate(pieces, axis=1)


def _pad_lanes(v, n=128):
    return jnp.pad(v, ((0, 0), (0, n - v.shape[1])))


def kernel(x, c, norm_g, w_ada, b_ada, w_in, b_fgate, fox_qn_g, fox_kn_g, gdn_conv_w, gdn_A_log, gdn_dt_bias, gdn_norm_g, w_out, final_g, loss_target, m_norm_g, m_w_ada, m_b_ada, m_w_in, m_b_fgate, m_fox_qn_g, m_fox_kn_g, m_gdn_conv_w, m_gdn_A_log, m_gdn_dt_bias, m_gdn_norm_g, m_w_out, m_final_g, v_norm_g, v_w_ada, v_b_ada, v_w_in, v_b_fgate, v_fox_qn_g, v_fox_kn_g, v_gdn_conv_w, v_gdn_A_log, v_gdn_dt_bias, v_gdn_norm_g, v_w_out, v_final_g):
    S = x.shape[1]
    H = N_HEADS
    ix, iy, ic = lax.axis_index("x"), lax.axis_index("y"), lax.axis_index("c")
    chip = 2 * ix + iy
    me = 2 * chip + ic
    x2, tgt = x[0], loss_target[0]
    ts_wide = _tile(S, 256)
    ts = _tile(S, 512)
    ts_head = _tile(S, 2048)
    tq = _tile(S, 512)
    nq = S // tq
    nch = S // CHUNK
    conv_cols_chip = CONV_COLS // N_CHIPS

    pay = jnp.concatenate([c, _pad_lanes(gdn_conv_w[0], D_MODEL), jnp.zeros((3, D_MODEL), F32)], axis=0)
    g1 = allgather8("ag_c", pay)
    c_all = g1[:, 0, :]
    conv_w = jnp.concatenate([g1[2 * j, 1:1 + CONV_K, :conv_cols_chip] for j in range(N_CHIPS)], axis=1)
    g2 = allgather8("ag_mod", ada_fwd(c_all, w_ada[0]))
    mod_rows = jnp.concatenate([g2[2 * j] for j in range(N_CHIPS)], axis=1)
    mod = lax.dynamic_slice_in_dim(mod_rows, me, 1, axis=0) + b_ada
    shift, scale, gate = mod[:, :D_MODEL], mod[:, D_MODEL:2 * D_MODEL], mod[:, 2 * D_MODEL:]

    n_in = w_in.shape[2]
    win4 = allgather_chips("ag_w_in", w_in[0].astype(MXU_DTYPE))
    w_all = _internal_from_blocks(win4, n_in)
    w_out_b = w_out[0].astype(MXU_DTYPE)

    (h_b,) = rowwise_fwd("prologue", f_prologue, S, ts_wide, 1, [(x2, "row", D_MODEL, 0)], [norm_g, scale, shift],
                         [("row", D_MODEL, 0, D_MODEL, MXU_DTYPE)])
    p_all, wout_far = matmul("in_proj", h_b, w_all, F32, tm=512, tn=640, tk=2048, rides=[("gather", w_out_b)])
    w_out_full = lax.dynamic_update_index_in_dim(wout_far, w_out_b, chip, axis=0).reshape(2 * WIDTH, D_MODEL)
    pv = jnp.concatenate([b_fgate, gdn_dt_bias, jnp.zeros((1, 128 - 16), F32)], axis=1)
    av = jnp.concatenate([jnp.zeros((1, 8), F32), gdn_A_log, jnp.zeros((1, 128 - 16), F32)], axis=1)
    smalls = smalls_fwd(p_all, pv, av, S, ts)
    frow = jnp.transpose(smalls[:, :H]).reshape(H, nq, 1, tq)
    gc_row = jnp.transpose(smalls[:, 8:16].reshape(nch, CHUNK, H), (0, 2, 1))
    head_b = ("head", D_HEAD, 0, WIDTH, MXU_DTYPE)
    head_f = ("head", D_HEAD, 0, WIDTH, F32)
    pcol = lambda cb: (p_all, "head", D_HEAD, cb)
    qn, kn, vb = rowwise_fwd("fox_pre", f_foxpre, S, ts_head, H,[pcol(CB_FQ), pcol(CB_FK), pcol(CB_FV)],
                             [fox_qn_g, fox_kn_g], [head_b] * 3)
    o_fox, lse = flash_fwd(qn, kn, vb, frow, S, tq)
    y_conv = conv_fwd(p_all, conv_w, S, ts)
    qkv = gdn_pre_fwd(y_conv, S, ts_wide)
    o_gdn, states, tinv = gdn_fwd(qkv, smalls, gc_row, S)
    mixed = lax.empty((S, 2 * WIDTH), MXU_DTYPE)
    (mixed,) = rowwise_fwd("post_fox", f_postf, S, ts_head, H,[(o_fox, "head", D_HEAD, 0), pcol(CB_FZ)], [],
                           [("head", D_HEAD, 0, 2 * WIDTH, MXU_DTYPE, mixed)])
    (mixed,) = rowwise_fwd("post_gdn", f_postg, S, ts_head, H,[(o_gdn, "head", D_HEAD, 0), pcol(CB_GZ)], [gdn_norm_g],
                           [("head", D_HEAD, H, 2 * WIDTH, MXU_DTYPE, mixed)])
    mo = matmul("out_proj", mixed, w_out_full, F32)

    wide = lambda a: (a, "row", D_MODEL, 0)
    dx_res, dmo, d_gate, d_final_g, loss_acc = rowwise_bwd(
        "loss", f_loss, S, ts_wide, 1, [wide(x2), wide(mo), wide(tgt)], [gate, final_g[None, :]], [],
        [(0, "row", D_MODEL, 0, D_MODEL, F32), (1, "row", D_MODEL, 0, D_MODEL, MXU_DTYPE)], primal_sum=True)

    d_mixed = matmul("d_mixed", dmo, w_out_full, F32, trans_b=True)
    g_w_out = matmul("g_w_out", mixed, dmo, F32, trans_a=True)
    dp = lax.empty((S, P_COLS), MXU_DTYPE)
    into_dp = lambda idx, cb: (idx, "head", D_HEAD, cb, P_COLS, MXU_DTYPE, dp)
    do_fox, dp = rowwise_bwd(
        "post_fox_bwd", f_postf, S, ts_head, H,[(o_fox, "head", D_HEAD, 0), pcol(CB_FZ)], [],
        [(d_mixed, "head", D_HEAD, 0)], [(0,) + head_b, into_dp(1, CB_FZ)])
    do_gdn, dp, d_gdn_norm_g = rowwise_bwd(
        "post_gdn_bwd", f_postg, S, ts_head, H,[(o_gdn, "head", D_HEAD, 0), pcol(CB_GZ)], [gdn_norm_g],
        [(d_mixed, "head", D_HEAD, 8)], [(0,) + head_f, into_dp(1, CB_GZ)])
    d_qn, d_kn, dp, dfq_row, dfk_row = flash_bwd(qn, kn, vb, do_fox, frow, lse, flash_delta(do_fox, o_fox, S, tq), dp,
                                                 S, tq)
    dp, d_qn_g = rowwise_bwd("fox_q_bwd", f_rms1, S, ts_head, H,[pcol(CB_FQ)], [fox_qn_g], [(d_qn, "head", D_HEAD, 0)],
                             [into_dp(0, CB_FQ)])
    dp, d_kn_g = rowwise_bwd("fox_k_bwd", f_rms1, S, ts_head, H,[pcol(CB_FK)], [fox_kn_g], [(d_kn, "head", D_HEAD, 0)],
                             [into_dp(0, CB_FK)])
    d_qkv, dsm_chunk, dgc_row = gdn_bwd(qkv, smalls, gc_row, states, tinv, do_gdn, S)
    dp, d_conv_w = conv_bwd(p_all, gdn_pre_bwd(y_conv, d_qkv, S, ts_wide), conv_w, dp, S, ts)
    d_f = jnp.transpose((dfq_row + dfk_row).reshape(H, S))
    d_gc = jnp.transpose(dgc_row, (0, 2, 1)).reshape(S, H)
    dsm = dsm_chunk + jnp.concatenate([d_f, d_gc, jnp.zeros((S, 128 - 16), F32)], axis=1)
    dp, d_pv, d_av = smalls_bwd(p_all, pv, av, dsm, dp, S, ts)
    g_w_all = matmul("g_w_in", h_b, dp, F32, tm=512, tn=640, tk=2048, trans_a=True)

    c_idx = jnp.reshape(ic, (1,)).astype(jnp.int32)

    def chip_sums(tag, g4):
        recv = sibling_send_other_half("d2d_" + tag, g4)
        return sum_own_half("sum2_" + tag, g4, recv, c_idx, 128)

    def finish(tag, p4, far):
        own = lax.dynamic_index_in_dim(p4, chip, axis=0, keepdims=False)
        return sibling_merge("merge_" + tag, sum_parts("sum_" + tag, own, far, 128))

    p4_in, p4_in_sent = chip_sums("w_in", jnp.stack([_block_from_internal(g_w_all, j, n_in) for j in range(N_CHIPS)]))
    p4_out, p4_out_sent = chip_sums("w_out", g_w_out.reshape(N_CHIPS, w_out.shape[1], D_MODEL))
    d_h, far_in, far_out = matmul("d_h", dp, w_all, F32, tm=512, tn=1024, tk=1664, trans_b=True,
                                  rides=[("scatter", p4_in_sent), ("scatter", p4_out_sent)])
    g_w_in_full = finish("w_in", p4_in, far_in)
    g_w_out_full = finish("w_out", p4_out, far_out)
    d_x, d_norm_g, d_scale, d_shift = rowwise_bwd(
        "prologue_bwd", f_prologue_res, S, ts_wide, 1, [wide(x2)], [norm_g, scale, shift],
        [wide(d_h), wide(dx_res)], [(0, "row", D_MODEL, 0, D_MODEL, F32)])

    parts = [d_shift, d_scale, d_gate, d_norm_g, d_final_g, d_conv_w.reshape(1, CONV_K * CONV_COLS), d_qn_g, d_kn_g,
             d_gdn_norm_g, d_pv, d_av, loss_acc]
    n_pay = sum(p.shape[1] for p in parts)
    pay_w = -(-n_pay // 1024) * 1024
    pay2 = jnp.concatenate(parts + [jnp.zeros((1, pay_w - n_pay), F32)], axis=1).reshape(8, pay_w // 8)
    g3 = allgather8("ag_small", pay2)
    tot = sum_devices("sum_small", g3).reshape(1, pay_w)
    dmod_all = g3.reshape(N_DEV, pay_w)[:, :3 * D_MODEL]
    o = 3 * D_MODEL
    g_b_ada = tot[:, :o]
    g_norm_g, g_final_g = tot[:, o:o + D_MODEL], tot[:, o + D_MODEL:o + 2 * D_MODEL]
    o += 2 * D_MODEL
    g_conv_full = tot[:, o:o + CONV_K * CONV_COLS].reshape(CONV_K, CONV_COLS)
    g_conv = lax.dynamic_slice_in_dim(g_conv_full, chip * conv_cols_chip, conv_cols_chip, axis=1)
    o += CONV_K * CONV_COLS
    g_qn_g, g_kn_g, g_gdn_ng = tot[:, o:o + 128], tot[:, o + 128:o + 256], tot[:, o + 256:o + 384]
    g_pv, g_av = tot[:, o + 384:o + 512], tot[:, o + 512:o + 640]
    loss = tot[0, o + 640]
    g_b_fgate, g_dt_bias, g_a_log = g_pv[:, :8], g_pv[:, 8:16], g_av[:, 8:16]

    def small_vec(vals):
        flat = [norm for norm in vals[:3]] + [vals[3].reshape(1, CONV_K * conv_cols_chip)] + list(vals[4:7]) \
            + [_pad_lanes(s) for s in vals[7:]]
        n = sum(f.shape[1] for f in flat)
        nw = -(-n // 1024) * 1024
        return jnp.concatenate(flat + [jnp.zeros((1, nw - n), F32)], axis=1).reshape(8, nw // 8)

    sw = small_vec([norm_g, b_ada, final_g[None, :], gdn_conv_w[0], fox_qn_g, fox_kn_g, gdn_norm_g, b_fgate, gdn_A_log,
                    gdn_dt_bias])
    sm = small_vec([m_norm_g, m_b_ada, m_final_g[None, :], m_gdn_conv_w[0], m_fox_qn_g, m_fox_kn_g, m_gdn_norm_g,
                    m_b_fgate, m_gdn_A_log, m_gdn_dt_bias])
    sv = small_vec([v_norm_g, v_b_ada, v_final_g[None, :], v_gdn_conv_w[0], v_fox_qn_g, v_fox_kn_g, v_gdn_norm_g,
                    v_b_fgate, v_gdn_A_log, v_gdn_dt_bias])
    sg = small_vec([g_norm_g, g_b_ada, g_final_g, g_conv, g_qn_g, g_kn_g, g_gdn_ng, g_b_fgate, g_a_log, g_dt_bias])
    small_out = adamw("adamw_small", sw, [sg], sm, sv, 8)

    def split_small(a):
        a = a.reshape(1, -1)
        out, o = [], 0
        for n, shp in ((D_MODEL, (1, D_MODEL)), (3 * D_MODEL, (1, 3 * D_MODEL)), (D_MODEL, (D_MODEL,)),
                       (CONV_K * conv_cols_chip, (1, CONV_K, conv_cols_chip)), (128, (1, 128)), (128, (1, 128)),
                       (128, (1, 128))):
            out.append(a[:, o:o + n].reshape(shp))
            o += n
        for _ in range(3):
            out.append(a[:, o:o + 8])
            o += 128
        return out

    n_ada = w_ada.shape[2]
    g_w_ada = ada_grad(c_all, lax.dynamic_slice_in_dim(dmod_all, chip * n_ada, n_ada, axis=1))
    ada_out = adamw("adamw_w_ada", w_ada[0], [g_w_ada], m_w_ada[0], v_w_ada[0], 128)

    in_out = adamw("adamw_w_in", w_in[0], [g_w_in_full], m_w_in[0], v_w_in[0], 128)
    out_out = adamw("adamw_w_out", w_out[0], [g_w_out_full], m_w_out[0], v_w_out[0], 128)

    res = [loss, d_x[None]]
    for k in range(4):
        s = split_small(small_out[k])
        big = [ada_out[k][None], in_out[k][None], out_out[k][None]]
        res += [s[0], big[0], s[1], big[1], s[7], s[4], s[5], s[3], s[8], s[9], s[6], big[2], s[2]]
    return tuple(res)
```

```python
import functools

import jax
import jax.numpy as jnp
from jax import lax
from jax.experimental import pallas as pl
from jax.experimental.pallas import tpu as pltpu

F32 = jnp.float32
MXU_DTYPE = jnp.bfloat16
HIGHEST = lax.Precision.HIGHEST
SOLVE_PRECISION = lax.Precision.HIGH
MESH_ID = pl.DeviceIdType.MESH

D_MODEL = 2048
N_HEADS = 8
D_HEAD = 128
WIDTH = N_HEADS * D_HEAD
CHUNK = 64
CONV_K = 4
EPS = 1e-6
NEG = -1e30
ATT_SCALE = D_HEAD ** -0.5
N_SMALL = 3 * N_HEADS
P_COLS = 8 * WIDTH + 128
CB_FQ, CB_FK, CB_FV, CB_FZ, CB_GQ, CB_GK, CB_GV, CB_GZ, CB_SMALL = 0, 8, 16, 24, 32, 40, 48, 56, 64
IN_WIDTH = 8 * WIDTH + N_SMALL
N_CHIPS = 4
N_DEV = 8

ADAM_LR, ADAM_B1, ADAM_B2, ADAM_EPS, ADAM_WD, ADAM_STEP = 0.001, 0.9, 0.999, 1e-08, 0.01, 10


def _dotg(a, b, dims):
    return lax.dot_general(a.astype(MXU_DTYPE), b.astype(MXU_DTYPE), (dims, ((), ())), preferred_element_type=F32)


@jax.custom_vjp
def mm_nn(a, b):
    return _dotg(a, b, ((1,), (0,)))


@jax.custom_vjp
def mm_nt(a, b):
    return _dotg(a, b, ((1,), (1,)))


@jax.custom_vjp
def mm_tn(a, b):
    return _dotg(a, b, ((0,), (0,)))


mm_nn.defvjp(lambda a, b: (mm_nn(a, b), (a, b)), lambda r, g: (mm_nt(g, r[1]), mm_tn(r[0], g)))
mm_nt.defvjp(lambda a, b: (mm_nt(a, b), (a, b)), lambda r, g: (mm_nn(g, r[1]), mm_tn(g, r[0])))
mm_tn.defvjp(lambda a, b: (mm_tn(a, b), (a, b)), lambda r, g: (mm_nt(r[1], g), mm_nn(r[0], g)))


def hdot(a, b):
    return lax.dot_general(a, b, (((1,), (0,)), ((), ())), precision=HIGHEST, preferred_element_type=F32)


def _sigmoid(x):
    return 0.5 * (jnp.tanh(0.5 * x) + 1.0)


def _silu(x):
    return x * _sigmoid(x)


def _softplus(x):
    return jnp.maximum(x, 0.0) + jnp.log(1.0 + jnp.exp(-jnp.abs(x)))


def _log_sigmoid(x):
    return jnp.minimum(x, 0.0) - jnp.log(1.0 + jnp.exp(-jnp.abs(x)))


def _rms(x, g):
    return x * lax.rsqrt(jnp.mean(x * x, axis=-1, keepdims=True) + EPS) * g


def _l2n(x):
    return x * lax.rsqrt(jnp.sum(x * x, axis=-1, keepdims=True) + EPS)


def _tile(n, pref):
    t = min(n, pref)
    assert n % t == 0, (n, pref)
    return t


def _bspec(kind, w, col0, ts):
    if kind == "row":
        return pl.BlockSpec((ts, w), lambda i, h: (i, col0))
    if kind == "head":
        return pl.BlockSpec((ts, w), lambda i, h: (i, col0 + h))
    assert kind == "head3"
    return pl.BlockSpec((1, ts, w), lambda i, h: (h, i, 0))


def _ld(ref, kind):
    return ref[0] if kind == "head3" else ref[...]


def _st(ref, kind, val):
    if kind == "head3":
        ref[0] = val.astype(ref.dtype)
    else:
        ref[...] = val.astype(ref.dtype)


def _full_spec(a):
    nd = a.ndim
    return pl.BlockSpec(a.shape, lambda i, h: (0,) * nd)


def _out_struct(kind, w, S, width_total, dtype, nh):
    if kind == "head3":
        return jax.ShapeDtypeStruct((nh, S, w), dtype)
    return jax.ShapeDtypeStruct((S, width_total), dtype)


def rowwise_fwd(name, f, S, ts, nh, ins, params, outs):
    n_in, n_p = len(ins), len(params)
    into = {j: o[5] for j, o in enumerate(outs) if len(o) > 5}
    outs = [o[:5] for o in outs]
    n_al = len(into)

    def body(*refs):
        vals = [_ld(r, s[1]).astype(F32) for r, s in zip(refs[:n_in], ins)]
        pv = [r[...] for r in refs[n_in:n_in + n_p]]
        res = f(*vals, *pv)
        for r, s, o in zip(refs[n_in + n_p + n_al:], outs, res):
            _st(r, s[0], o)

    return pl.pallas_call(
        body, name=name, grid=(S // ts, nh),
        in_specs=[_bspec(k, w, c0, ts) for (_, k, w, c0) in ins] + [_full_spec(p) for p in params]
        + [pl.BlockSpec(memory_space=pl.ANY)] * n_al,
        out_specs=[_bspec(k, w, c0, ts) for (k, w, c0, _, _) in outs],
        out_shape=[jax.ShapeDtypeStruct(into[j].shape, into[j].dtype) if j in into else _out_struct(k, w, S, tw, dt, nh)
                   for j, (k, w, c0, tw, dt) in enumerate(outs)],
        input_output_aliases={n_in + n_p + n: j for n, j in enumerate(sorted(into))},
    )(*[a for (a, _, _, _) in ins], *params, *[into[j] for j in sorted(into)])


def rowwise_bwd(name, f, S, ts, nh, ins, params, cts, row_grads, extra=None, extra_outs=(), primal_sum=False):
    n_in, n_p, n_ct = len(ins), len(params), len(cts)
    n_rg, n_ex = len(row_grads), len(extra_outs)
    into = {j: rg[6] for j, rg in enumerate(row_grads) if len(rg) > 6}
    row_grads = [rg[:6] for rg in row_grads]
    n_al = len(into)

    def body(*refs):
        first = (pl.program_id(0) == 0) & (pl.program_id(1) == 0)
        in_refs = refs[:n_in]
        p_refs = refs[n_in:n_in + n_p]
        ct_refs = refs[n_in + n_p:n_in + n_p + n_ct]
        o = n_in + n_p + n_ct + n_al
        rg_refs = refs[o:o + n_rg]
        ex_refs = refs[o + n_rg:o + n_rg + n_ex]
        acc_refs = refs[o + n_rg + n_ex:]
        vals = [_ld(r, s[1]).astype(F32) for r, s in zip(in_refs, ins)]
        pv = [r[...] for r in p_refs]
        res, vjp = jax.vjp(f, *vals, *pv)
        if primal_sum:
            cv = [jnp.ones_like(res[0])] + [_ld(r, s[1]).astype(F32) for r, s in zip(ct_refs, cts)]
        else:
            cv = [_ld(r, s[1]).astype(F32) for r, s in zip(ct_refs, cts)]
        grads = vjp(tuple(cv))
        for r, s in zip(rg_refs, row_grads):
            _st(r, s[1], grads[s[0]])
        if extra is not None:
            for r, s, e in zip(ex_refs, extra_outs, extra(vals, cv, grads)):
                _st(r, s[0], e)

        @pl.when(first)
        def _():
            for r in acc_refs:
                r[...] = jnp.zeros_like(r)

        for r, g in zip(acc_refs[:n_p], grads[n_in:]):
            r[...] += g
        if primal_sum:
            acc_refs[n_p][...] += jnp.sum(res[0])

    acc_shapes = [jax.ShapeDtypeStruct(p.shape, F32) for p in params]
    acc_specs = [_full_spec(p) for p in params]
    if primal_sum:
        acc_shapes.append(jax.ShapeDtypeStruct((1, 128), F32))
        acc_specs.append(pl.BlockSpec((1, 128), lambda i, h: (0, 0)))
    rg_shapes = [jax.ShapeDtypeStruct(into[j].shape, into[j].dtype) if j in into else _out_struct(k, w, S, tw, dt, nh)
                 for j, (_, k, w, c0, tw, dt) in enumerate(row_grads)]
    first_alias = n_in + n_p + n_ct
    return pl.pallas_call(
        body, name=name, grid=(S // ts, nh),
        in_specs=[_bspec(k, w, c0, ts) for (_, k, w, c0) in ins] + [_full_spec(p) for p in params]
        + [_bspec(k, w, c0, ts) for (_, k, w, c0) in cts] + [pl.BlockSpec(memory_space=pl.ANY)] * n_al,
        out_specs=[_bspec(k, w, c0, ts) for (_, k, w, c0, _, _) in row_grads]
        + [_bspec(k, w, c0, ts) for (k, w, c0, _, _) in extra_outs] + acc_specs,
        out_shape=rg_shapes + [_out_struct(k, w, S, tw, dt, nh) for (k, w, c0, tw, dt) in extra_outs] + acc_shapes,
        input_output_aliases={first_alias + n: j for n, j in enumerate(sorted(into))},
    )(*[a for (a, _, _, _) in ins], *params, *[a for (a, _, _, _) in cts], *[into[j] for j in sorted(into)])


def f_prologue(x, ng, sc, sh):
    return (_rms(x, ng) * (1.0 + sc) + sh,)


def f_prologue_res(x, ng, sc, sh):
    return (_rms(x, ng) * (1.0 + sc) + sh, x)


def f_loss(x, mo, tgt, gate, fg):
    y = _rms(x + gate * mo, fg)
    return (0.5 * jnp.mean(jnp.square(y - tgt), axis=-1, keepdims=True),)


def f_foxpre(fq, fk, fv, qg, kg):
    return (_rms(fq, qg) * ATT_SCALE, _rms(fk, kg), fv)


def f_rms1(x, g):
    return (_rms(x, g),)


def f_postf(o, z):
    return (o * _silu(z),)


def f_postg(o, z, ng):
    return (_rms(o, ng) * _silu(z),)


def f_gdnpre(yq, yk, yv):
    return (_l2n(_silu(yq)) * ATT_SCALE, _l2n(_silu(yk)), _silu(yv))


def _chip_exchange(kind, x_ref, out_ref, send_sems, recv_sems):
    x, y, c = lax.axis_index("x"), lax.axis_index("y"), lax.axis_index("c")
    sends, arrivals = [], []
    for r in range(1, N_CHIPS):
        px, py = _flip(x, r & 2), _flip(y, r & 1)
        if kind == "scatter":
            src, dst, landed = x_ref.at[2 * px + py], out_ref.at[r - 1], out_ref.at[r - 1]
        else:
            src, dst, landed = x_ref, out_ref.at[2 * x + y], out_ref.at[2 * px + py]
        mk = functools.partial(pltpu.make_async_remote_copy, send_sem=send_sems.at[r - 1], recv_sem=recv_sems.at[r - 1],
                               device_id=(px, py, c), device_id_type=MESH_ID)
        sends.append(mk(src_ref=src, dst_ref=dst))
        arrivals.append(mk(src_ref=src, dst_ref=landed))
    return sends, arrivals


def _exchange_shape(kind, x):
    return jax.ShapeDtypeStruct(((N_CHIPS - 1,) + x.shape[1:]) if kind == "scatter" else ((N_CHIPS,) + x.shape), x.dtype)


def matmul(name, a, b, out_dtype, tm=512, tn=512, tk=2048, trans_a=False, trans_b=False, rides=()):
    M, K = a.shape[::-1] if trans_a else a.shape
    N, K2 = b.shape if trans_b else b.shape[::-1]
    assert K == K2
    tm, tn, tk = _tile(M, tm), _tile(N, tn), _tile(K, tk)
    ni, nj, nk = M // tm, N // tn, K // tk
    a_spec = pl.BlockSpec((tk, tm), lambda i, j, k: (k, i)) if trans_a else pl.BlockSpec((tm, tk), lambda i, j, k: (i, k))
    b_spec = pl.BlockSpec((tn, tk), lambda i, j, k: (j, k)) if trans_b else pl.BlockSpec((tk, tn), lambda i, j, k: (k, j))
    n_ride = len(rides)

    def body(a_ref, b_ref, *rest):
        ride_in, o_ref, ride_out = rest[:n_ride], rest[n_ride], rest[n_ride + 1:2 * n_ride + 1]
        scr = rest[2 * n_ride + 1:]
        i, j, k = pl.program_id(0), pl.program_id(1), pl.program_id(2)
        exchanges = [_chip_exchange(kind, ride_in[n], ride_out[n], scr[2 * n], scr[2 * n + 1])
                     for n, (kind, _) in enumerate(rides)]

        @pl.when((i == 0) & (j == 0) & (k == 0))
        def _():
            for sends, _ in exchanges:
                for cp in sends:
                    cp.start()

        part = lax.dot_general(a_ref[...], b_ref[...], (((0 if trans_a else 1,), (1 if trans_b else 0,)), ((), ())),
                               preferred_element_type=F32)
        if nk == 1:
            o_ref[...] = part.astype(o_ref.dtype)
        else:
            acc = scr[2 * n_ride]

            @pl.when(k == 0)
            def _():
                acc[...] = part

            @pl.when(k > 0)
            def _():
                acc[...] += part

            @pl.when(k == nk - 1)
            def _():
                o_ref[...] = acc[...].astype(o_ref.dtype)

        @pl.when((i == ni - 1) & (j == nj - 1) & (k == nk - 1))
        def _():
            for sends, arrivals in exchanges:
                for cp in arrivals:
                    cp.wait_recv()
                for cp in sends:
                    cp.wait_send()

    sems = [pltpu.SemaphoreType.DMA((N_CHIPS - 1,))] * (2 * n_ride)
    any_spec = pl.BlockSpec(memory_space=pl.ANY)
    res = pl.pallas_call(
        body, name=name, grid=(ni, nj, nk),
        in_specs=[a_spec, b_spec] + [any_spec] * n_ride,
        out_specs=[pl.BlockSpec((tm, tn), lambda i, j, k: (i, j))] + [any_spec] * n_ride,
        out_shape=[jax.ShapeDtypeStruct((M, N), out_dtype)] + [_exchange_shape(kind, x) for kind, x in rides],
        scratch_shapes=sems + ([] if nk == 1 else [pltpu.VMEM((tm, tn), F32)]),
        compiler_params=pltpu.CompilerParams(
            dimension_semantics=("arbitrary",) * 3 if rides else ("parallel", "parallel", "arbitrary")),
    )(a, b, *[x for _, x in rides])
    return tuple(res) if rides else res[0]


def _small_f(z, pv, av):
    lane = lax.broadcasted_iota(jnp.int32, z.shape, 1)
    zz = z + pv
    logf = _log_sigmoid(zz)
    gval = -jnp.exp(av) * _softplus(zz)
    beta = _sigmoid(zz)
    return jnp.where(lane < 8, logf, jnp.where(lane < 16, gval, jnp.where(lane < 24, beta, 0.0)))


def _tri(ts, upper):
    r = lax.broadcasted_iota(jnp.int32, (ts, ts), 0)
    c = lax.broadcasted_iota(jnp.int32, (ts, ts), 1)
    keep = (r <= c) if upper else (r >= c)
    same_chunk = (r // CHUNK) == (c // CHUNK)
    return keep.astype(F32), (keep & same_chunk).astype(F32)


def smalls_fwd(p_all, pv, av, S, ts):
    nt = S // ts

    def body(z_ref, pv_ref, av_ref, o_ref, carry):
        i = pl.program_id(0)

        @pl.when(i == 0)
        def _():
            carry[...] = jnp.zeros_like(carry)

        e = _small_f(z_ref[...], pv_ref[...], av_ref[...])
        lane = lax.broadcasted_iota(jnp.int32, e.shape, 1)
        l_all, l_chunk = _tri(ts, upper=False)
        cum_all = hdot(l_all, e) + carry[...]
        cum_chunk = hdot(l_chunk, e)
        carry[...] = cum_all[ts - 1:ts, :]
        o_ref[...] = jnp.where(lane < 8, cum_all, jnp.where(lane < 16, cum_chunk, e))

    return pl.pallas_call(
        body, name="smalls_fwd", grid=(nt,),
        in_specs=[pl.BlockSpec((ts, 128), lambda i: (i, CB_SMALL)), pl.BlockSpec((1, 128), lambda i: (0, 0)),
                  pl.BlockSpec((1, 128), lambda i: (0, 0))],
        out_specs=pl.BlockSpec((ts, 128), lambda i: (i, 0)),
        out_shape=jax.ShapeDtypeStruct((S, 128), F32),
        scratch_shapes=[pltpu.VMEM((1, 128), F32)],
    )(p_all, pv, av)


def smalls_bwd(p_all, pv, av, dsm, dp, S, ts):
    nt = S // ts

    def body(z_ref, pv_ref, av_ref, d_ref, dp_in, dz_ref, dpv_ref, dav_ref, carry):
        i = pl.program_id(0)

        @pl.when(i == 0)
        def _():
            carry[...] = jnp.zeros_like(carry)
            dpv_ref[...] = jnp.zeros_like(dpv_ref)
            dav_ref[...] = jnp.zeros_like(dav_ref)

        d = d_ref[...]
        lane = lax.broadcasted_iota(jnp.int32, d.shape, 1)
        u_all, u_chunk = _tri(ts, upper=True)
        rev_all = hdot(u_all, d) + carry[...]
        rev_chunk = hdot(u_chunk, d)
        carry[...] = rev_all[0:1, :]
        de = jnp.where(lane < 8, rev_all, jnp.where(lane < 16, rev_chunk, d))
        _, vjp = jax.vjp(_small_f, z_ref[...], pv_ref[...], av_ref[...])
        dz, dpv, dav = vjp(de)
        dz_ref[...] = dz.astype(dz_ref.dtype)
        dpv_ref[...] += dpv
        dav_ref[...] += dav

    rev = lambda i: (nt - 1 - i, 0)
    small_cols = pl.BlockSpec((ts, 128), lambda i: (nt - 1 - i, CB_SMALL))
    return pl.pallas_call(
        body, name="smalls_bwd", grid=(nt,),
        in_specs=[small_cols, pl.BlockSpec((1, 128), lambda i: (0, 0)), pl.BlockSpec((1, 128), lambda i: (0, 0)),
                  pl.BlockSpec((ts, 128), rev), pl.BlockSpec(memory_space=pl.ANY)],
        out_specs=[small_cols, pl.BlockSpec((1, 128), lambda i: (0, 0)), pl.BlockSpec((1, 128), lambda i: (0, 0))],
        out_shape=[jax.ShapeDtypeStruct(dp.shape, dp.dtype), jax.ShapeDtypeStruct((1, 128), F32),
                   jax.ShapeDtypeStruct((1, 128), F32)],
        scratch_shapes=[pltpu.VMEM((1, 128), F32)],
        input_output_aliases={4: 0},
    )(p_all, pv, av, dsm, dp)


def _col_to_row(col):
    return jnp.transpose(jnp.broadcast_to(col, (col.shape[0], 128)))[0:1, :]


def _row_to_col(row):
    return jnp.transpose(jnp.broadcast_to(row, (128, row.shape[1])))[:, 0:1]


def _causal_mask(s, keys_first):
    r = lax.broadcasted_iota(jnp.int32, s.shape, 0)
    c = lax.broadcasted_iota(jnp.int32, s.shape, 1)
    return jnp.where((r <= c) if keys_first else (c <= r), s, NEG)


def flash_fwd(qs, kn, vb, frow, S, tq):
    nq = S // tq
    tk = tq

    def body(q_ref, k_ref, v_ref, fr_ref, o_ref, lse_ref):
        i = pl.program_id(1)
        q = q_ref[...]
        f_base = fr_ref[0, i][:, 0:1]

        def tile(j, carry, diagonal):
            m, l, acc = carry
            off = pl.multiple_of(j * tk, tk)
            k = k_ref[pl.ds(off, tk), :]
            v = v_ref[pl.ds(off, tk), :]
            s = mm_nt(q, k) - (fr_ref[0, j] - f_base)
            if diagonal:
                s = _causal_mask(s, keys_first=False)
            m_new = jnp.maximum(m, jnp.max(s, axis=-1, keepdims=True))
            a = jnp.exp(m - m_new)
            p = jnp.exp(s - m_new)
            l = a * l + jnp.sum(p, axis=-1, keepdims=True)
            acc = a * acc + mm_nn(p, v)
            return m_new, l, acc

        init = (jnp.full((tq, 1), NEG, F32), jnp.zeros((tq, 1), F32), jnp.zeros((tq, D_HEAD), F32))
        carry = lax.fori_loop(0, i, lambda j, c: tile(j, c, False), init)
        m, l, acc = tile(i, carry, True)
        o_ref[...] = acc / l
        lse_ref[0, 0] = _col_to_row(m + jnp.log(l))

    return pl.pallas_call(
        body, name="flash_fwd", grid=(N_HEADS, nq),
        in_specs=[pl.BlockSpec((tq, D_HEAD), lambda h, i: (i, h)), pl.BlockSpec((S, D_HEAD), lambda h, i: (0, h)),
                  pl.BlockSpec((S, D_HEAD), lambda h, i: (0, h)),
                  pl.BlockSpec((1, nq, 1, tk), lambda h, i: (h, 0, 0, 0))],
        out_specs=[pl.BlockSpec((tq, D_HEAD), lambda h, i: (i, h)),
                   pl.BlockSpec((1, 1, 1, tq), lambda h, i: (h, i, 0, 0))],
        out_shape=[jax.ShapeDtypeStruct((S, WIDTH), F32), jax.ShapeDtypeStruct((N_HEADS, nq, 1, tq), F32)],
    )(qs, kn, vb, frow)


def flash_delta(do, o, S, tq):
    nq = S // tq
    per = _tile(nq, 4)

    def body(do_ref, o_ref, dl_ref):
        for n in range(per):
            rows = pl.ds(n * tq, tq)
            dl_ref[0, n] = _col_to_row(jnp.sum(do_ref[rows, :].astype(F32) * o_ref[rows, :], axis=-1, keepdims=True))

    blk = pl.BlockSpec((per * tq, D_HEAD), lambda h, i: (i, h))
    return pl.pallas_call(
        body, name="flash_delta", grid=(N_HEADS, nq // per), in_specs=[blk, blk],
        out_specs=pl.BlockSpec((1, per, 1, tq), lambda h, i: (h, i, 0, 0)),
        out_shape=jax.ShapeDtypeStruct((N_HEADS, nq, 1, tq), F32),
    )(do, o)


def flash_bwd(qs, kn, vb, do, frow, lse_row, delta_row, dp, S, tq):
    nq = S // tq
    tk = tq

    def body(q_ref, k_ref, v_ref, do_ref, fr_ref, lse_ref, dl_ref, dp_in, dq_ref, dk_ref, dv_ref, dfq_ref, dfk_ref):
        j = pl.program_id(1)
        k = k_ref[...]
        v = v_ref[...]
        fk = _row_to_col(fr_ref[0, j])

        @pl.when(j == 0)
        def _():
            dq_ref[...] = jnp.zeros_like(dq_ref)
            dfq_ref[...] = jnp.zeros_like(dfq_ref)

        def tile(i, carry, diagonal):
            dk, dv, dfk = carry
            off = pl.multiple_of(i * tq, tq)
            q = q_ref[pl.ds(off, tq), :]
            do_ = do_ref[pl.ds(off, tq), :]
            st = mm_nt(k, q) - (fk - fr_ref[0, i][:, 0:1])
            if diagonal:
                st = _causal_mask(st, keys_first=True)
            pt = jnp.exp(st - lse_ref[0, i])
            dst = pt * (mm_nt(v, do_) - dl_ref[0, i])
            dq_ref[pl.ds(off, tq), :] += mm_tn(dst, k) * ATT_SCALE
            dfq_ref[0, i] += jnp.sum(dst, axis=0, keepdims=True)
            return dk + mm_nn(dst, q), dv + mm_nn(pt, do_), dfk - jnp.sum(dst, axis=-1, keepdims=True)

        z = jnp.zeros((tk, D_HEAD), F32)
        carry = tile(j, (z, z, jnp.zeros((tk, 1), F32)), True)
        dk, dv, dfk = lax.fori_loop(j + 1, nq, lambda i, c: tile(i, c, False), carry)
        dk_ref[...] = dk
        dv_ref[...] = dv.astype(dv_ref.dtype)
        dfk_ref[0, 0] = _col_to_row(dfk)

    blk = pl.BlockSpec((tk, D_HEAD), lambda h, j: (j, h))
    full = pl.BlockSpec((S, D_HEAD), lambda h, j: (0, h))
    rows = pl.BlockSpec((1, nq, 1, tq), lambda h, j: (h, 0, 0, 0))
    stat = jax.ShapeDtypeStruct((N_HEADS, nq, 1, tq), F32)
    return pl.pallas_call(
        body, name="flash_bwd", grid=(N_HEADS, nq),
        in_specs=[full, blk, blk, full, rows, rows, rows, pl.BlockSpec(memory_space=pl.ANY)],
        out_specs=[full, blk, pl.BlockSpec((tk, D_HEAD), lambda h, j: (j, CB_FV + h)), rows,
                   pl.BlockSpec((1, 1, 1, tk), lambda h, j: (h, j, 0, 0))],
        out_shape=[jax.ShapeDtypeStruct((S, WIDTH), F32), jax.ShapeDtypeStruct((S, WIDTH), F32),
                   jax.ShapeDtypeStruct(dp.shape, dp.dtype), stat, stat],
        input_output_aliases={7: 2},
    )(qs, kn, vb, do, frow, lse_row, delta_row, dp)


CONV_COLS = 3 * WIDTH
CONV_TC = 512


def conv_fwd(p_all, conv_w, S, ts):
    nt, tc = S // ts, CONV_TC
    cb0 = CB_GQ * 128 // tc
    r8 = ts // 8

    def body(x_ref, halo_ref, w_ref, y_ref, scr):
        i = pl.program_id(1)
        scr[0:8, :] = jnp.where(i > 0, halo_ref[...], 0.0)
        scr[8:8 + ts, :] = x_ref[...]
        w = w_ref[...]
        y = w[3:4, :] * x_ref[...]
        for j in range(CONV_K - 1):
            y = y + w[j:j + 1, :] * scr[5 + j:5 + j + ts, :]
        y_ref[...] = y

    return pl.pallas_call(
        body, name="conv_fwd", grid=(CONV_COLS // tc, nt),
        in_specs=[pl.BlockSpec((ts, tc), lambda c, i: (i, cb0 + c)),
                  pl.BlockSpec((8, tc), lambda c, i: (jnp.maximum(i * r8 - 1, 0), cb0 + c)),
                  pl.BlockSpec((CONV_K, tc), lambda c, i: (0, c))],
        out_specs=pl.BlockSpec((ts, tc), lambda c, i: (i, c)),
        out_shape=jax.ShapeDtypeStruct((S, CONV_COLS), F32),
        scratch_shapes=[pltpu.VMEM((ts + 8, tc), F32)],
    )(p_all, p_all, conv_w)


def conv_bwd(p_all, dy, conv_w, dp, S, ts):
    nt, tc = S // ts, CONV_TC
    cb0 = CB_GQ * 128 // tc
    r8 = ts // 8

    def body(x_ref, xh_ref, dy_ref, dyh_ref, w_ref, dp_in, dx_ref, dw_ref, xs, ds):
        i = pl.program_id(1)
        xs[0:8, :] = jnp.where(i > 0, xh_ref[...], 0.0)
        xs[8:8 + ts, :] = x_ref[...]
        ds[0:ts, :] = dy_ref[...]
        ds[ts:ts + 8, :] = jnp.where(i < nt - 1, dyh_ref[...], 0.0)
        w = w_ref[...]
        d = dy_ref[...]
        dx = w[3:4, :] * d
        for j in range(CONV_K - 1):
            dx = dx + w[j:j + 1, :] * ds[3 - j:3 - j + ts, :]
        dx_ref[...] = dx.astype(dx_ref.dtype)

        @pl.when(i == 0)
        def _():
            dw_ref[...] = jnp.zeros_like(dw_ref)

        rows = [jnp.sum(d * xs[5 + j:5 + j + ts, :], axis=0, keepdims=True) for j in range(CONV_K - 1)]
        rows.append(jnp.sum(d * x_ref[...], axis=0, keepdims=True))
        dw_ref[...] += jnp.concatenate(rows, axis=0)

    return pl.pallas_call(
        body, name="conv_bwd", grid=(CONV_COLS // tc, nt),
        in_specs=[pl.BlockSpec((ts, tc), lambda c, i: (i, cb0 + c)),
                  pl.BlockSpec((8, tc), lambda c, i: (jnp.maximum(i * r8 - 1, 0), cb0 + c)),
                  pl.BlockSpec((ts, tc), lambda c, i: (i, c)),
                  pl.BlockSpec((8, tc), lambda c, i: (jnp.minimum((i + 1) * r8, nt * r8 - 1), c)),
                  pl.BlockSpec((CONV_K, tc), lambda c, i: (0, c)), pl.BlockSpec(memory_space=pl.ANY)],
        out_specs=[pl.BlockSpec((ts, tc), lambda c, i: (i, cb0 + c)), pl.BlockSpec((CONV_K, tc), lambda c, i: (0, c))],
        out_shape=[jax.ShapeDtypeStruct(dp.shape, dp.dtype), jax.ShapeDtypeStruct((CONV_K, CONV_COLS), F32)],
        scratch_shapes=[pltpu.VMEM((ts + 8, tc), F32), pltpu.VMEM((ts + 8, tc), F32)],
        input_output_aliases={5: 0},
    )(p_all, p_all, dy, dy, conv_w, dp)


def _bdot(a, b, ca, cb):
    return lax.dot_general(a.astype(MXU_DTYPE), b.astype(MXU_DTYPE), (((ca,), (cb,)), ((0,), (0,))),
                           preferred_element_type=F32)


@jax.custom_vjp
def bmm_nn(a, b):
    return _bdot(a, b, 2, 1)


@jax.custom_vjp
def bmm_nt(a, b):
    return _bdot(a, b, 2, 2)


@jax.custom_vjp
def bmm_tn(a, b):
    return _bdot(a, b, 1, 1)


bmm_nn.defvjp(lambda a, b: (bmm_nn(a, b), (a, b)), lambda r, g: (bmm_nt(g, r[1]), bmm_tn(r[0], g)))
bmm_nt.defvjp(lambda a, b: (bmm_nt(a, b), (a, b)), lambda r, g: (bmm_nn(g, r[1]), bmm_tn(g, r[0])))
bmm_tn.defvjp(lambda a, b: (bmm_tn(a, b), (a, b)), lambda r, g: (bmm_nt(r[1], g), bmm_nn(r[0], g)))


def bdot_hi(a, b, ca=2, cb=1):
    return lax.dot_general(a, b, (((ca,), (cb,)), ((0,), (0,))), precision=SOLVE_PRECISION,
                           preferred_element_type=F32)


def _tri_inv_product(m):
    ii = lax.broadcasted_iota(jnp.int32, m.shape, 1)
    jj = lax.broadcasted_iota(jnp.int32, m.shape, 2)
    t = (ii == jj).astype(F32) - m
    pw = bdot_hi(m, m)
    for _ in range(4):
        t = t + bdot_hi(t, pw)
        pw = bdot_hi(pw, pw)
    return t + bdot_hi(t, pw)


def _tri_inv_bwd(t, g):
    return -bdot_hi(bdot_hi(t, g, 1, 1), t, 2, 2)


@jax.custom_vjp
def tri_inv(m):
    return _tri_inv_product(m)


@jax.custom_vjp
def tri_inv_known(m, t):
    return t


tri_inv.defvjp(lambda m: (lambda t: (t, t))(_tri_inv_product(m)), lambda t, g: (_tri_inv_bwd(t, g),))
tri_inv_known.defvjp(lambda m, t: (t, t), lambda t, g: (_tri_inv_bwd(t, g), jnp.zeros_like(t)))


def chunk_fn(q, k, v, gc_col, gc_row, beta, state, t_known=None):
    shape = (N_HEADS, CHUNK, CHUNK)
    ii = lax.broadcasted_iota(jnp.int32, shape, 1)
    jj = lax.broadcasted_iota(jnp.int32, shape, 2)
    lower = ii >= jj
    strict = ii > jj
    decay = jnp.exp(jnp.where(lower, gc_col - gc_row, NEG))
    kb = k * beta
    vb = v * beta
    m = jnp.where(strict, bmm_nt(kb, k) * decay, 0.0)
    t = tri_inv(m) if t_known is None else tri_inv_known(m, t_known)
    u = bdot_hi(t, vb)
    w = bdot_hi(t, kb * jnp.exp(gc_col))
    attn = jnp.where(lower, bmm_nt(q, k) * decay, 0.0)
    v_new = u - bmm_nn(w, state)
    o = bmm_nn(q * jnp.exp(gc_col), state) + bmm_nn(attn, v_new)
    g_last = gc_col[:, CHUNK - 1:CHUNK, :]
    k_dec = k * jnp.exp(g_last - gc_col)
    new_state = state * jnp.exp(g_last) + bmm_tn(k_dec, v_new)
    return (o, new_state, t) if t_known is None else (o, new_state)


def _heads(ref):
    return jnp.stack([ref[:, h * D_HEAD:(h + 1) * D_HEAD] for h in range(N_HEADS)], axis=0)


def _head_cols(ref, lane0):
    return jnp.stack([ref[:, lane0 + h:lane0 + h + 1] for h in range(N_HEADS)], axis=0)


def _qkv_head_cols(h):
    return [slice(g * WIDTH + h * D_HEAD, g * WIDTH + (h + 1) * D_HEAD) for g in range(3)]


def gdn_pre_fwd(y, S, ts):
    def body(y_ref, o_ref):
        for h in range(N_HEADS):
            cols = _qkv_head_cols(h)
            for c, o in zip(cols, f_gdnpre(*[y_ref[:, c] for c in cols])):
                o_ref[:, c] = o

    spec = pl.BlockSpec((ts, CONV_COLS), lambda i: (i, 0))
    return pl.pallas_call(body, name="gdn_pre", grid=(S // ts,), in_specs=[spec], out_specs=spec,
                          out_shape=jax.ShapeDtypeStruct((S, CONV_COLS), F32))(y)


def gdn_pre_bwd(y, dqkv, S, ts):
    def body(y_ref, d_ref, o_ref):
        for h in range(N_HEADS):
            cols = _qkv_head_cols(h)
            _, vjp = jax.vjp(f_gdnpre, *[y_ref[:, c] for c in cols])
            for c, g in zip(cols, vjp(tuple(d_ref[:, c] for c in cols))):
                o_ref[:, c] = g

    spec = pl.BlockSpec((ts, CONV_COLS), lambda i: (i, 0))
    return pl.pallas_call(body, name="gdn_pre_bwd", grid=(S // ts,), in_specs=[spec, spec], out_specs=spec,
                          out_shape=jax.ShapeDtypeStruct((S, CONV_COLS), F32))(y, dqkv)


def gdn_fwd(qkv, smalls, gc_row, S):
    n = S // CHUNK

    def body(q_ref, k_ref, v_ref, sm_ref, gr_ref, o_ref, st_ref, t_ref, state):
        @pl.when(pl.program_id(0) == 0)
        def _():
            state[...] = jnp.zeros_like(state)

        s0 = state[...]
        st_ref[0] = s0
        o, s1, t = chunk_fn(_heads(q_ref), _heads(k_ref), _heads(v_ref), _head_cols(sm_ref, 8),
                            gr_ref[0][:, None, :], _head_cols(sm_ref, 16), s0)
        for h in range(N_HEADS):
            o_ref[:, h * D_HEAD:(h + 1) * D_HEAD] = o[h]
        state[...] = s1
        t_ref[0] = t

    return pl.pallas_call(
        body, name="gdn_chunk_fwd", grid=(n,),
        in_specs=[pl.BlockSpec((CHUNK, WIDTH), lambda i, g=g: (i, g)) for g in range(3)]
        + [pl.BlockSpec((CHUNK, 128), lambda i: (i, 0)), pl.BlockSpec((1, N_HEADS, CHUNK), lambda i: (i, 0, 0))],
        out_specs=[pl.BlockSpec((CHUNK, WIDTH), lambda i: (i, 0)),
                   pl.BlockSpec((1, N_HEADS, D_HEAD, D_HEAD), lambda i: (i, 0, 0, 0)),
                   pl.BlockSpec((1, N_HEADS, CHUNK, CHUNK), lambda i: (i, 0, 0, 0))],
        out_shape=[jax.ShapeDtypeStruct((S, WIDTH), F32), jax.ShapeDtypeStruct((n, N_HEADS, D_HEAD, D_HEAD), F32),
                   jax.ShapeDtypeStruct((n, N_HEADS, CHUNK, CHUNK), F32)],
        scratch_shapes=[pltpu.VMEM((N_HEADS, D_HEAD, D_HEAD), F32)],
    )(qkv, qkv, qkv, smalls, gc_row)


def gdn_bwd(qkv, smalls, gc_row, states, tinv, do, S):
    n = S // CHUNK

    def body(q_ref, k_ref, v_ref, sm_ref, gr_ref, st_ref, t_ref, do_ref, dqkv_ref, dsm_ref, dgr_ref, dstate):
        @pl.when(pl.program_id(0) == 0)
        def _():
            dstate[...] = jnp.zeros_like(dstate)

        t_saved = t_ref[0]
        _, vjp = jax.vjp(lambda *a: chunk_fn(*a, t_known=t_saved), _heads(q_ref), _heads(k_ref), _heads(v_ref),
                         _head_cols(sm_ref, 8), gr_ref[0][:, None, :], _head_cols(sm_ref, 16), st_ref[0])
        dq, dk, dv, dgc, dgr, dbt, ds = vjp((_heads(do_ref), dstate[...]))
        for h in range(N_HEADS):
            dqkv_ref[:, h * D_HEAD:(h + 1) * D_HEAD] = dq[h]
            dqkv_ref[:, WIDTH + h * D_HEAD:WIDTH + (h + 1) * D_HEAD] = dk[h]
            dqkv_ref[:, 2 * WIDTH + h * D_HEAD:2 * WIDTH + (h + 1) * D_HEAD] = dv[h]
        dstate[...] = ds
        cols = [dgc[h] for h in range(N_HEADS)] + [dbt[h] for h in range(N_HEADS)]
        dsm_ref[...] = jnp.concatenate([jnp.zeros((CHUNK, 8), F32)] + cols + [jnp.zeros((CHUNK, 128 - 24), F32)],
                                       axis=1)
        dgr_ref[0] = jnp.concatenate([dgr[h] for h in range(N_HEADS)], axis=0)

    rev = lambda c: (lambda i: (n - 1 - i, c))
    return pl.pallas_call(
        body, name="gdn_chunk_bwd", grid=(n,),
        in_specs=[pl.BlockSpec((CHUNK, WIDTH), rev(g)) for g in range(3)]
        + [pl.BlockSpec((CHUNK, 128), rev(0)), pl.BlockSpec((1, N_HEADS, CHUNK), lambda i: (n - 1 - i, 0, 0)),
           pl.BlockSpec((1, N_HEADS, D_HEAD, D_HEAD), lambda i: (n - 1 - i, 0, 0, 0)),
           pl.BlockSpec((1, N_HEADS, CHUNK, CHUNK), lambda i: (n - 1 - i, 0, 0, 0)),
           pl.BlockSpec((CHUNK, WIDTH), rev(0))],
        out_specs=[pl.BlockSpec((CHUNK, 3 * WIDTH), rev(0)), pl.BlockSpec((CHUNK, 128), rev(0)),
                   pl.BlockSpec((1, N_HEADS, CHUNK), lambda i: (n - 1 - i, 0, 0))],
        out_shape=[jax.ShapeDtypeStruct((S, 3 * WIDTH), F32)] + [jax.ShapeDtypeStruct((S, 128), F32),
                                                                  jax.ShapeDtypeStruct((n, N_HEADS, CHUNK), F32)],
        scratch_shapes=[pltpu.VMEM((N_HEADS, D_HEAD, D_HEAD), F32)],
    )(qkv, qkv, qkv, smalls, gc_row, states, tinv, do)


def _flip(a, bit):
    return 1 - a if bit else a


def allgather8(name, buf):
    R, W = buf.shape

    def body(x_ref, out_ref, send_sems, recv_sems, local_sem):
        x, y, c = lax.axis_index("x"), lax.axis_index("y"), lax.axis_index("c")

        def slot(px, py, pc):
            return out_ref.at[4 * px + 2 * py + pc]

        mine = pltpu.make_async_copy(x_ref, slot(x, y, c), local_sem)
        mine.start()
        sends = []
        for r in range(1, N_DEV):
            peer = (_flip(x, r & 4), _flip(y, r & 2), _flip(c, r & 1))
            cp = pltpu.make_async_remote_copy(src_ref=x_ref, dst_ref=slot(x, y, c), send_sem=send_sems.at[r - 1],
                                              recv_sem=recv_sems.at[r - 1], device_id=peer, device_id_type=MESH_ID)
            cp.start()
            sends.append(cp)
        for r in range(1, N_DEV):
            peer = (_flip(x, r & 4), _flip(y, r & 2), _flip(c, r & 1))
            pltpu.make_async_remote_copy(src_ref=x_ref, dst_ref=slot(*peer), send_sem=send_sems.at[r - 1],
                                         recv_sem=recv_sems.at[r - 1], device_id=peer,
                                         device_id_type=MESH_ID).wait_recv()
        for cp in sends:
            cp.wait_send()
        mine.wait()

    return pl.pallas_call(
        body, name=name,
        out_shape=jax.ShapeDtypeStruct((N_DEV, R, W), buf.dtype),
        in_specs=[pl.BlockSpec(memory_space=pltpu.VMEM)],
        out_specs=pl.BlockSpec(memory_space=pltpu.VMEM),
        scratch_shapes=[pltpu.SemaphoreType.DMA((N_DEV - 1,)), pltpu.SemaphoreType.DMA((N_DEV - 1,)),
                        pltpu.SemaphoreType.DMA],
    )(buf)


def allgather_chips(name, blk):
    R, W = blk.shape
    half = R // 2
    n_far = N_CHIPS - 1

    def body(x_ref, out_ref, send_sems, recv_sems):
        x, y, c = lax.axis_index("x"), lax.axis_index("y"), lax.axis_index("c")
        sibling = (x, y, 1 - c)
        chips = [(_flip(x, r & 2), _flip(y, r & 1)) for r in range(1, N_CHIPS)]

        def rows(px, py, pc):
            return out_ref.at[2 * px + py, pc]

        def copy(k, dst, to, src):
            return pltpu.make_async_remote_copy(src_ref=src, dst_ref=dst, send_sem=send_sems.at[k],
                                                recv_sem=recv_sems.at[k], device_id=to, device_id_type=MESH_ID)

        my_half = x_ref.at[c]
        first = [copy(j, rows(x, y, c), (*chip, c), my_half) for j, chip in enumerate(chips)]
        for cp in first:
            cp.start()
        passed = [copy(n_far + j, rows(*chip, c), sibling, rows(*chip, c)) for j, chip in enumerate(chips)]
        for j, chip in enumerate(chips):
            copy(j, rows(*chip, c), (*chip, c), my_half).wait_recv()
            passed[j].start()
        for j, chip in enumerate(chips):
            copy(n_far + j, rows(*chip, 1 - c), sibling, my_half).wait_recv()
        for cp in first + passed:
            cp.wait_send()

    far = pl.pallas_call(
        body, name=name,
        out_shape=jax.ShapeDtypeStruct((N_CHIPS, 2, half, W), blk.dtype),
        in_specs=[pl.BlockSpec(memory_space=pl.ANY)],
        out_specs=pl.BlockSpec(memory_space=pl.ANY),
        scratch_shapes=[pltpu.SemaphoreType.DMA((2 * n_far,)), pltpu.SemaphoreType.DMA((2 * n_far,))],
    )(blk.reshape(2, half, W)).reshape(N_CHIPS, R, W)
    own_chip = 2 * lax.axis_index("x") + lax.axis_index("y")
    return lax.dynamic_update_index_in_dim(far, blk, own_chip, axis=0)


def sibling_send_other_half(name, g4):
    n, R, W = g4.shape
    half = R // 2

    def body(x_ref, out_ref, send_sem, recv_sem):
        x, y, c = lax.axis_index("x"), lax.axis_index("y"), lax.axis_index("c")
        cp = pltpu.make_async_remote_copy(src_ref=x_ref.at[:, pl.ds((1 - c) * half, half), :], dst_ref=out_ref,
                                          send_sem=send_sem, recv_sem=recv_sem, device_id=(x, y, 1 - c),
                                          device_id_type=MESH_ID)
        cp.start()
        cp.wait_recv()
        cp.wait_send()

    return pl.pallas_call(
        body, name=name,
        out_shape=jax.ShapeDtypeStruct((n, half, W), g4.dtype),
        in_specs=[pl.BlockSpec(memory_space=pl.ANY)],
        out_specs=pl.BlockSpec(memory_space=pl.ANY),
        scratch_shapes=[pltpu.SemaphoreType.DMA, pltpu.SemaphoreType.DMA],
    )(g4)


def sibling_merge(name, mine):
    def body(x_ref, out_ref, send_sem, recv_sem):
        x, y, c = lax.axis_index("x"), lax.axis_index("y"), lax.axis_index("c")
        cp = pltpu.make_async_remote_copy(src_ref=x_ref, dst_ref=out_ref, send_sem=send_sem, recv_sem=recv_sem,
                                          device_id=(x, y, 1 - c), device_id_type=MESH_ID)
        cp.start()
        cp.wait_recv()
        cp.wait_send()

    other = pl.pallas_call(
        body, name=name,
        out_shape=jax.ShapeDtypeStruct(mine.shape, mine.dtype),
        in_specs=[pl.BlockSpec(memory_space=pl.ANY)],
        out_specs=pl.BlockSpec(memory_space=pl.ANY),
        scratch_shapes=[pltpu.SemaphoreType.DMA, pltpu.SemaphoreType.DMA],
    )(mine)
    first = lax.axis_index("c") == 0
    return jnp.concatenate([jnp.where(first, mine, other), jnp.where(first, other, mine)], axis=0)


def sum_parts(name, own, recv, tr):
    R, W = own.shape
    tr = _tile(R, tr)

    def body(o_ref, r_ref, out_ref):
        acc = o_ref[...]
        for k in range(recv.shape[0]):
            acc = acc + r_ref[k].astype(F32)
        out_ref[...] = acc

    return pl.pallas_call(
        body, name=name, grid=(R // tr,),
        in_specs=[pl.BlockSpec((tr, W), lambda i: (i, 0)), pl.BlockSpec((recv.shape[0], tr, W), lambda i: (0, i, 0))],
        out_specs=pl.BlockSpec((tr, W), lambda i: (i, 0)),
        out_shape=jax.ShapeDtypeStruct((R, W), F32),
    )(own, recv)


def sum_own_half(name, g4, recv, c_idx, tr):
    n, R, W = g4.shape
    half = R // 2
    tr = _tile(half, tr)
    nb = half // tr

    def body(c_ref, g_ref, r_ref, o_ref, ob_ref):
        s = g_ref[0] + r_ref[0]
        o_ref[0] = s
        ob_ref[0] = s.astype(ob_ref.dtype)

    mine = pl.BlockSpec((1, tr, W), lambda j, i, c: (j, i, 0))
    return pl.pallas_call(
        body, name=name,
        grid_spec=pltpu.PrefetchScalarGridSpec(
            num_scalar_prefetch=1, grid=(n, nb),
            in_specs=[pl.BlockSpec((1, tr, W), lambda j, i, c: (j, c[0] * nb + i, 0)), mine],
            out_specs=[mine, mine]),
        out_shape=[jax.ShapeDtypeStruct((n, half, W), F32), jax.ShapeDtypeStruct((n, half, W), MXU_DTYPE)],
    )(c_idx, g4, recv)


def sum_devices(name, g):
    def body(g_ref, o_ref):
        acc = g_ref[0]
        for d in range(1, N_DEV):
            acc = acc + g_ref[d]
        o_ref[...] = acc

    return pl.pallas_call(body, name=name, out_shape=jax.ShapeDtypeStruct(g.shape[1:], F32))(g)


def ada_fwd(c_all, w_ada):
    K, N = w_ada.shape
    tn = _tile(N, 512)

    def body(c_ref, w_ref, o_ref):
        o_ref[...] = mm_nn(_silu(c_ref[...]), w_ref[...])

    return pl.pallas_call(
        body, name="ada_fwd", grid=(N // tn,),
        in_specs=[pl.BlockSpec((N_DEV, K), lambda j: (0, 0)), pl.BlockSpec((K, tn), lambda j: (0, j))],
        out_specs=pl.BlockSpec((N_DEV, tn), lambda j: (0, j)),
        out_shape=jax.ShapeDtypeStruct((N_DEV, N), F32),
    )(c_all, w_ada)


def ada_grad(c_all, dmod):
    K = c_all.shape[1]
    N = dmod.shape[1]
    tn = _tile(N, 512)

    def body(c_ref, d_ref, o_ref):
        o_ref[...] = mm_tn(_silu(c_ref[...]), d_ref[...])

    return pl.pallas_call(
        body, name="ada_grad", grid=(N // tn,),
        in_specs=[pl.BlockSpec((N_DEV, K), lambda j: (0, 0)), pl.BlockSpec((N_DEV, tn), lambda j: (0, j))],
        out_specs=pl.BlockSpec((K, tn), lambda j: (0, j)),
        out_shape=jax.ShapeDtypeStruct((K, N), F32),
    )(c_all, dmod)


def adamw(name, w, gparts, m, v, tr):
    R, W = w.shape
    tr = _tile(R, tr)
    ng = len(gparts)

    def body(w_ref, *refs):
        g_refs, (m_ref, v_ref, g_out, d_out, m_out, v_out) = refs[:ng], refs[ng:]
        g = g_refs[0][...]
        for r in g_refs[1:]:
            g = g + r[...]
        m1 = ADAM_B1 * m_ref[...] + (1.0 - ADAM_B1) * g
        v1 = ADAM_B2 * v_ref[...] + (1.0 - ADAM_B2) * jnp.square(g)
        m_hat = m1 / (1.0 - ADAM_B1 ** ADAM_STEP)
        v_hat = v1 / (1.0 - ADAM_B2 ** ADAM_STEP)
        g_out[...] = g
        d_out[...] = -ADAM_LR * (m_hat / (jnp.sqrt(v_hat) + ADAM_EPS) + ADAM_WD * w_ref[...])
        m_out[...] = m1
        v_out[...] = v1

    spec = pl.BlockSpec((tr, W), lambda i: (i, 0))
    return pl.pallas_call(
        body, name=name, grid=(R // tr,),
        in_specs=[spec] * (3 + ng), out_specs=[spec] * 4,
        out_shape=[jax.ShapeDtypeStruct((R, W), F32)] * 4,
    )(w, *gparts, m, v)


_REF_SEGMENTS = ((0, 4 * WIDTH, 0), (4 * WIDTH, 4 * WIDTH + 8, 4 * WIDTH), (4 * WIDTH + 8, 8 * WIDTH + 8, -8),
                 (8 * WIDTH + 8, IN_WIDTH, 0))


def _internal_from_blocks(blocks, n):
    a = 4 * WIDTH
    pieces = []
    for lo, hi in ((0, a), (a + 8, 2 * a + 8), (a, a + 8), (2 * a + 8, IN_WIDTH)):
        for j in range(N_CHIPS):
            s, e = max(lo, j * n), min(hi, (j + 1) * n)
            if s < e:
                pieces.append(blocks[j][:, s - j * n:e - j * n])
    pieces.append(jnp.zeros(blocks.shape[1:2] + (128 - N_SMALL,), blocks.dtype))
    return jnp.concatenate(pieces, axis=1)


def _block_from_internal(g, j, n):
    pieces = []
    for lo, hi, shift in _REF_SEGMENTS:
        s, e = max(lo, j * n), min(hi, (j + 1) * n)
        if s < e:
            pieces.append(g[:, s + shift:e + shift])
    return jnp.concatenate(pieces, axis=1)


def _pad_lanes(v, n=128):
    return jnp.pad(v, ((0, 0), (0, n - v.shape[1])))


def kernel(x, c, norm_g, w_ada, b_ada, w_in, b_fgate, fox_qn_g, fox_kn_g, gdn_conv_w, gdn_A_log, gdn_dt_bias, gdn_norm_g, w_out, final_g, loss_target, m_norm_g, m_w_ada, m_b_ada, m_w_in, m_b_fgate, m_fox_qn_g, m_fox_kn_g, m_gdn_conv_w, m_gdn_A_log, m_gdn_dt_bias, m_gdn_norm_g, m_w_out, m_final_g, v_norm_g, v_w_ada, v_b_ada, v_w_in, v_b_fgate, v_fox_qn_g, v_fox_kn_g, v_gdn_conv_w, v_gdn_A_log, v_gdn_dt_bias, v_gdn_norm_g, v_w_out, v_final_g):
    S = x.shape[1]
    H = N_HEADS
    ix, iy, ic = lax.axis_index("x"), lax.axis_index("y"), lax.axis_index("c")
    chip = 2 * ix + iy
    me = 2 * chip + ic
    x2, tgt = x[0], loss_target[0]
    ts_wide = _tile(S, 256)
    ts = _tile(S, 512)
    ts_head = _tile(S, 2048)
    tq = _tile(S, 1024)
    nq = S // tq
    nch = S // CHUNK
    conv_cols_chip = CONV_COLS // N_CHIPS

    pay = jnp.concatenate([c, _pad_lanes(gdn_conv_w[0], D_MODEL), jnp.zeros((3, D_MODEL), F32)], axis=0)
    g1 = allgather8("ag_c", pay)
    c_all = g1[:, 0, :]
    conv_w = jnp.concatenate([g1[2 * j, 1:1 + CONV_K, :conv_cols_chip] for j in range(N_CHIPS)], axis=1)
    g2 = allgather8("ag_mod", ada_fwd(c_all, w_ada[0]))
    mod_rows = jnp.concatenate([g2[2 * j] for j in range(N_CHIPS)], axis=1)
    mod = lax.dynamic_slice_in_dim(mod_rows, me, 1, axis=0) + b_ada
    shift, scale, gate = mod[:, :D_MODEL], mod[:, D_MODEL:2 * D_MODEL], mod[:, 2 * D_MODEL:]

    n_in = w_in.shape[2]
    win4 = allgather_chips("ag_w_in", w_in[0].astype(MXU_DTYPE))
    w_all = _internal_from_blocks(win4, n_in)
    w_out_b = w_out[0].astype(MXU_DTYPE)

    (h_b,) = rowwise_fwd("prologue", f_prologue, S, ts_wide, 1, [(x2, "row", D_MODEL, 0)], [norm_g, scale, shift],
                         [("row", D_MODEL, 0, D_MODEL, MXU_DTYPE)])
    p_all, wout_far = matmul("in_proj", h_b, w_all, F32, tm=1024, tn=1664, tk=2048, rides=[("gather", w_out_b)])
    w_out_full = lax.dynamic_update_index_in_dim(wout_far, w_out_b, chip, axis=0).reshape(2 * WIDTH, D_MODEL)
    pv = jnp.concatenate([b_fgate, gdn_dt_bias, jnp.zeros((1, 128 - 16), F32)], axis=1)
    av = jnp.concatenate([jnp.zeros((1, 8), F32), gdn_A_log, jnp.zeros((1, 128 - 16), F32)], axis=1)
    smalls = smalls_fwd(p_all, pv, av, S, ts)
    frow = jnp.transpose(smalls[:, :H]).reshape(H, nq, 1, tq)
    gc_row = jnp.transpose(smalls[:, 8:16].reshape(nch, CHUNK, H), (0, 2, 1))
    head_b = ("head", D_HEAD, 0, WIDTH, MXU_DTYPE)
    head_f = ("head", D_HEAD, 0, WIDTH, F32)
    pcol = lambda cb: (p_all, "head", D_HEAD, cb)
    qn, kn, vb = rowwise_fwd("fox_pre", f_foxpre, S, ts_head, H,[pcol(CB_FQ), pcol(CB_FK), pcol(CB_FV)],
                             [fox_qn_g, fox_kn_g], [head_b] * 3)
    o_fox, lse = flash_fwd(qn, kn, vb, frow, S, tq)
    y_conv = conv_fwd(p_all, conv_w, S, ts)
    qkv = gdn_pre_fwd(y_conv, S, ts_wide)
    o_gdn, states, tinv = gdn_fwd(qkv, smalls, gc_row, S)
    mixed = lax.empty((S, 2 * WIDTH), MXU_DTYPE)
    (mixed,) = rowwise_fwd("post_fox", f_postf, S, ts_head, H,[(o_fox, "head", D_HEAD, 0), pcol(CB_FZ)], [],
                           [("head", D_HEAD, 0, 2 * WIDTH, MXU_DTYPE, mixed)])
    (mixed,) = rowwise_fwd("post_gdn", f_postg, S, ts_head, H,[(o_gdn, "head", D_HEAD, 0), pcol(CB_GZ)], [gdn_norm_g],
                           [("head", D_HEAD, H, 2 * WIDTH, MXU_DTYPE, mixed)])
    mo = matmul("out_proj", mixed, w_out_full, F32, tm=1024, tn=2048, tk=2048)

    wide = lambda a: (a, "row", D_MODEL, 0)
    dx_res, dmo, d_gate, d_final_g, loss_acc = rowwise_bwd(
        "loss", f_loss, S, ts_wide, 1, [wide(x2), wide(mo), wide(tgt)], [gate, final_g[None, :]], [],
        [(0, "row", D_MODEL, 0, D_MODEL, F32), (1, "row", D_MODEL, 0, D_MODEL, MXU_DTYPE)], primal_sum=True)

    d_mixed = matmul("d_mixed", dmo, w_out_full, F32, tm=1024, tn=2048, tk=2048, trans_b=True)
    g_w_out = matmul("g_w_out", mixed, dmo, F32, tm=1024, tn=2048, tk=2048, trans_a=True)
    dp = lax.empty((S, P_COLS), MXU_DTYPE)
    into_dp = lambda idx, cb: (idx, "head", D_HEAD, cb, P_COLS, MXU_DTYPE, dp)
    do_fox, dp = rowwise_bwd(
        "post_fox_bwd", f_postf, S, ts_head, H,[(o_fox, "head", D_HEAD, 0), pcol(CB_FZ)], [],
        [(d_mixed, "head", D_HEAD, 0)], [(0,) + head_b, into_dp(1, CB_FZ)])
    do_gdn, dp, d_gdn_norm_g = rowwise_bwd(
        "post_gdn_bwd", f_postg, S, ts_head, H,[(o_gdn, "head", D_HEAD, 0), pcol(CB_GZ)], [gdn_norm_g],
        [(d_mixed, "head", D_HEAD, 8)], [(0,) + head_f, into_dp(1, CB_GZ)])
    d_qn, d_kn, dp, dfq_row, dfk_row = flash_bwd(qn, kn, vb, do_fox, frow, lse, flash_delta(do_fox, o_fox, S, tq), dp,
                                                 S, tq)
    dp, d_qn_g = rowwise_bwd("fox_q_bwd", f_rms1, S, ts_head, H,[pcol(CB_FQ)], [fox_qn_g], [(d_qn, "head", D_HEAD, 0)],
                             [into_dp(0, CB_FQ)])
    dp, d_kn_g = rowwise_bwd("fox_k_bwd", f_rms1, S, ts_head, H,[pcol(CB_FK)], [fox_kn_g], [(d_kn, "head", D_HEAD, 0)],
                             [into_dp(0, CB_FK)])
    d_qkv, dsm_chunk, dgc_row = gdn_bwd(qkv, smalls, gc_row, states, tinv, do_gdn, S)
    dp, d_conv_w = conv_bwd(p_all, gdn_pre_bwd(y_conv, d_qkv, S, ts_wide), conv_w, dp, S, ts)
    d_f = jnp.transpose((dfq_row + dfk_row).reshape(H, S))
    d_gc = jnp.transpose(dgc_row, (0, 2, 1)).reshape(S, H)
    dsm = dsm_chunk + jnp.concatenate([d_f, d_gc, jnp.zeros((S, 128 - 16), F32)], axis=1)
    dp, d_pv, d_av = smalls_bwd(p_all, pv, av, dsm, dp, S, ts)
    g_w_all = matmul("g_w_in", h_b, dp, F32, tm=1024, tn=1664, tk=2048, trans_a=True)

    c_idx = jnp.reshape(ic, (1,)).astype(jnp.int32)

    def chip_sums(tag, g4):
        recv = sibling_send_other_half("d2d_" + tag, g4)
        return sum_own_half("sum2_" + tag, g4, recv, c_idx, 128)

    def finish(tag, p4, far):
        own = lax.dynamic_index_in_dim(p4, chip, axis=0, keepdims=False)
        return sibling_merge("merge_" + tag, sum_parts("sum_" + tag, own, far, 128))

    p4_in, p4_in_sent = chip_sums("w_in", jnp.stack([_block_from_internal(g_w_all, j, n_in) for j in range(N_CHIPS)]))
    p4_out, p4_out_sent = chip_sums("w_out", g_w_out.reshape(N_CHIPS, w_out.shape[1], D_MODEL))
    d_h, far_in, far_out = matmul("d_h", dp, w_all, F32, tm=1024, tn=2048, tk=1664, trans_b=True,
                                  rides=[("scatter", p4_in_sent), ("scatter", p4_out_sent)])
    g_w_in_full = finish("w_in", p4_in, far_in)
    g_w_out_full = finish("w_out", p4_out, far_out)
    d_x, d_norm_g, d_scale, d_shift = rowwise_bwd(
        "prologue_bwd", f_prologue_res, S, ts_wide, 1, [wide(x2)], [norm_g, scale, shift],
        [wide(d_h), wide(dx_res)], [(0, "row", D_MODEL, 0, D_MODEL, F32)])

    parts = [d_shift, d_scale, d_gate, d_norm_g, d_final_g, d_conv_w.reshape(1, CONV_K * CONV_COLS), d_qn_g, d_kn_g,
             d_gdn_norm_g, d_pv, d_av, loss_acc]
    n_pay = sum(p.shape[1] for p in parts)
    pay_w = -(-n_pay // 1024) * 1024
    pay2 = jnp.concatenate(parts + [jnp.zeros((1, pay_w - n_pay), F32)], axis=1).reshape(8, pay_w // 8)
    g3 = allgather8("ag_small", pay2)
    tot = sum_devices("sum_small", g3).reshape(1, pay_w)
    dmod_all = g3.reshape(N_DEV, pay_w)[:, :3 * D_MODEL]
    o = 3 * D_MODEL
    g_b_ada = tot[:, :o]
    g_norm_g, g_final_g = tot[:, o:o + D_MODEL], tot[:, o + D_MODEL:o + 2 * D_MODEL]
    o += 2 * D_MODEL
    g_conv_full = tot[:, o:o + CONV_K * CONV_COLS].reshape(CONV_K, CONV_COLS)
    g_conv = lax.dynamic_slice_in_dim(g_conv_full, chip * conv_cols_chip, conv_cols_chip, axis=1)
    o += CONV_K * CONV_COLS
    g_qn_g, g_kn_g, g_gdn_ng = tot[:, o:o + 128], tot[:, o + 128:o + 256], tot[:, o + 256:o + 384]
    g_pv, g_av = tot[:, o + 384:o + 512], tot[:, o + 512:o + 640]
    loss = tot[0, o + 640]
    g_b_fgate, g_dt_bias, g_a_log = g_pv[:, :8], g_pv[:, 8:16], g_av[:, 8:16]

    def small_vec(vals):
        flat = [norm for norm in vals[:3]] + [vals[3].reshape(1, CONV_K * conv_cols_chip)] + list(vals[4:7]) \
            + [_pad_lanes(s) for s in vals[7:]]
        n = sum(f.shape[1] for f in flat)
        nw = -(-n // 1024) * 1024
        return jnp.concatenate(flat + [jnp.zeros((1, nw - n), F32)], axis=1).reshape(8, nw // 8)

    sw = small_vec([norm_g, b_ada, final_g[None, :], gdn_conv_w[0], fox_qn_g, fox_kn_g, gdn_norm_g, b_fgate, gdn_A_log,
                    gdn_dt_bias])
    sm = small_vec([m_norm_g, m_b_ada, m_final_g[None, :], m_gdn_conv_w[0], m_fox_qn_g, m_fox_kn_g, m_gdn_norm_g,
                    m_b_fgate, m_gdn_A_log, m_gdn_dt_bias])
    sv = small_vec([v_norm_g, v_b_ada, v_final_g[None, :], v_gdn_conv_w[0], v_fox_qn_g, v_fox_kn_g, v_gdn_norm_g,
                    v_b_fgate, v_gdn_A_log, v_gdn_dt_bias])
    sg = small_vec([g_norm_g, g_b_ada, g_final_g, g_conv, g_qn_g, g_kn_g, g_gdn_ng, g_b_fgate, g_a_log, g_dt_bias])
    small_out = adamw("adamw_small", sw, [sg], sm, sv, 8)

    def split_small(a):
        a = a.reshape(1, -1)
        out, o = [], 0
        for n, shp in ((D_MODEL, (1, D_MODEL)), (3 * D_MODEL, (1, 3 * D_MODEL)), (D_MODEL, (D_MODEL,)),
                       (CONV_K * conv_cols_chip, (1, CONV_K, conv_cols_chip)), (128, (1, 128)), (128, (1, 128)),
                       (128, (1, 128))):
            out.append(a[:, o:o + n].reshape(shp))
            o += n
        for _ in range(3):
            out.append(a[:, o:o + 8])
            o += 128
        return out

    n_ada = w_ada.shape[2]
    g_w_ada = ada_grad(c_all, lax.dynamic_slice_in_dim(dmod_all, chip * n_ada, n_ada, axis=1))
    ada_out = adamw("adamw_w_ada", w_ada[0], [g_w_ada], m_w_ada[0], v_w_ada[0], 128)

    in_out = adamw("adamw_w_in", w_in[0], [g_w_in_full], m_w_in[0], v_w_in[0], 128)
    out_out = adamw("adamw_w_out", w_out[0], [g_w_out_full], m_w_out[0], v_w_out[0], 128)

    res = [loss, d_x[None]]
    for k in range(4):
        s = split_small(small_out[k])
        big = [ada_out[k][None], in_out[k][None], out_out[k][None]]
        res += [s[0], big[0], s[1], big[1], s[7], s[4], s[5], s[3], s[8], s[9], s[6], big[2], s[2]]
    return tuple(res)
```

```python
import functools

import jax
import jax.numpy as jnp
from jax import lax
from jax.experimental import pallas as pl
from jax.experimental.pallas import tpu as pltpu

F32 = jnp.float32
MXU_DTYPE = jnp.bfloat16
HIGHEST = lax.Precision.HIGHEST
SOLVE_PRECISION = lax.Precision.HIGH
MESH_ID = pl.DeviceIdType.MESH

D_MODEL = 2048
N_HEADS = 8
D_HEAD = 128
WIDTH = N_HEADS * D_HEAD
CHUNK = 64
CONV_K = 4
EPS = 1e-6
NEG = -1e30
ATT_SCALE = D_HEAD ** -0.5
N_SMALL = 3 * N_HEADS
P_COLS = 8 * WIDTH + 128
CB_FQ, CB_FK, CB_FV, CB_FZ, CB_GQ, CB_GK, CB_GV, CB_GZ, CB_SMALL = 0, 8, 16, 24, 32, 40, 48, 56, 64
IN_WIDTH = 8 * WIDTH + N_SMALL
N_CHIPS = 4
N_DEV = 8

ADAM_LR, ADAM_B1, ADAM_B2, ADAM_EPS, ADAM_WD, ADAM_STEP = 0.001, 0.9, 0.999, 1e-08, 0.01, 10


def _dotg(a, b, dims):
    return lax.dot_general(a.astype(MXU_DTYPE), b.astype(MXU_DTYPE), (dims, ((), ())), preferred_element_type=F32)


@jax.custom_vjp
def mm_nn(a, b):
    return _dotg(a, b, ((1,), (0,)))


@jax.custom_vjp
def mm_nt(a, b):
    return _dotg(a, b, ((1,), (1,)))


@jax.custom_vjp
def mm_tn(a, b):
    return _dotg(a, b, ((0,), (0,)))


mm_nn.defvjp(lambda a, b: (mm_nn(a, b), (a, b)), lambda r, g: (mm_nt(g, r[1]), mm_tn(r[0], g)))
mm_nt.defvjp(lambda a, b: (mm_nt(a, b), (a, b)), lambda r, g: (mm_nn(g, r[1]), mm_tn(g, r[0])))
mm_tn.defvjp(lambda a, b: (mm_tn(a, b), (a, b)), lambda r, g: (mm_nt(r[1], g), mm_nn(r[0], g)))


def hdot(a, b):
    return lax.dot_general(a, b, (((1,), (0,)), ((), ())), precision=HIGHEST, preferred_element_type=F32)


def _sigmoid(x):
    return 0.5 * (jnp.tanh(0.5 * x) + 1.0)


def _silu(x):
    return x * _sigmoid(x)


def _softplus(x):
    return jnp.maximum(x, 0.0) + jnp.log(1.0 + jnp.exp(-jnp.abs(x)))


def _log_sigmoid(x):
    return jnp.minimum(x, 0.0) - jnp.log(1.0 + jnp.exp(-jnp.abs(x)))


def _rms(x, g):
    return x * lax.rsqrt(jnp.mean(x * x, axis=-1, keepdims=True) + EPS) * g


def _l2n(x):
    return x * lax.rsqrt(jnp.sum(x * x, axis=-1, keepdims=True) + EPS)


def _tile(n, pref):
    t = min(n, pref)
    assert n % t == 0, (n, pref)
    return t


def _bspec(kind, w, col0, ts):
    if kind == "row":
        return pl.BlockSpec((ts, w), lambda i, h: (i, col0))
    if kind == "head":
        return pl.BlockSpec((ts, w), lambda i, h: (i, col0 + h))
    assert kind == "head3"
    return pl.BlockSpec((1, ts, w), lambda i, h: (h, i, 0))


def _ld(ref, kind):
    return ref[0] if kind == "head3" else ref[...]


def _st(ref, kind, val):
    if kind == "head3":
        ref[0] = val.astype(ref.dtype)
    else:
        ref[...] = val.astype(ref.dtype)


def _full_spec(a):
    nd = a.ndim
    return pl.BlockSpec(a.shape, lambda i, h: (0,) * nd)


def _out_struct(kind, w, S, width_total, dtype, nh):
    if kind == "head3":
        return jax.ShapeDtypeStruct((nh, S, w), dtype)
    return jax.ShapeDtypeStruct((S, width_total), dtype)


def rowwise_fwd(name, f, S, ts, nh, ins, params, outs):
    n_in, n_p = len(ins), len(params)
    into = {j: o[5] for j, o in enumerate(outs) if len(o) > 5}
    outs = [o[:5] for o in outs]
    n_al = len(into)

    def body(*refs):
        vals = [_ld(r, s[1]).astype(F32) for r, s in zip(refs[:n_in], ins)]
        pv = [r[...] for r in refs[n_in:n_in + n_p]]
        res = f(*vals, *pv)
        for r, s, o in zip(refs[n_in + n_p + n_al:], outs, res):
            _st(r, s[0], o)

    return pl.pallas_call(
        body, name=name, grid=(S // ts, nh),
        in_specs=[_bspec(k, w, c0, ts) for (_, k, w, c0) in ins] + [_full_spec(p) for p in params]
        + [pl.BlockSpec(memory_space=pl.ANY)] * n_al,
        out_specs=[_bspec(k, w, c0, ts) for (k, w, c0, _, _) in outs],
        out_shape=[jax.ShapeDtypeStruct(into[j].shape, into[j].dtype) if j in into else _out_struct(k, w, S, tw, dt, nh)
                   for j, (k, w, c0, tw, dt) in enumerate(outs)],
        input_output_aliases={n_in + n_p + n: j for n, j in enumerate(sorted(into))},
    )(*[a for (a, _, _, _) in ins], *params, *[into[j] for j in sorted(into)])


def rowwise_bwd(name, f, S, ts, nh, ins, params, cts, row_grads, extra=None, extra_outs=(), primal_sum=False):
    n_in, n_p, n_ct = len(ins), len(params), len(cts)
    n_rg, n_ex = len(row_grads), len(extra_outs)
    into = {j: rg[6] for j, rg in enumerate(row_grads) if len(rg) > 6}
    row_grads = [rg[:6] for rg in row_grads]
    n_al = len(into)

    def body(*refs):
        first = (pl.program_id(0) == 0) & (pl.program_id(1) == 0)
        in_refs = refs[:n_in]
        p_refs = refs[n_in:n_in + n_p]
        ct_refs = refs[n_in + n_p:n_in + n_p + n_ct]
        o = n_in + n_p + n_ct + n_al
        rg_refs = refs[o:o + n_rg]
        ex_refs = refs[o + n_rg:o + n_rg + n_ex]
        acc_refs = refs[o + n_rg + n_ex:]
        vals = [_ld(r, s[1]).astype(F32) for r, s in zip(in_refs, ins)]
        pv = [r[...] for r in p_refs]
        res, vjp = jax.vjp(f, *vals, *pv)
        if primal_sum:
            cv = [jnp.ones_like(res[0])] + [_ld(r, s[1]).astype(F32) for r, s in zip(ct_refs, cts)]
        else:
            cv = [_ld(r, s[1]).astype(F32) for r, s in zip(ct_refs, cts)]
        grads = vjp(tuple(cv))
        for r, s in zip(rg_refs, row_grads):
            _st(r, s[1], grads[s[0]])
        if extra is not None:
            for r, s, e in zip(ex_refs, extra_outs, extra(vals, cv, grads)):
                _st(r, s[0], e)

        @pl.when(first)
        def _():
            for r in acc_refs:
                r[...] = jnp.zeros_like(r)

        for r, g in zip(acc_refs[:n_p], grads[n_in:]):
            r[...] += g
        if primal_sum:
            acc_refs[n_p][...] += jnp.sum(res[0])

    acc_shapes = [jax.ShapeDtypeStruct(p.shape, F32) for p in params]
    acc_specs = [_full_spec(p) for p in params]
    if primal_sum:
        acc_shapes.append(jax.ShapeDtypeStruct((1, 128), F32))
        acc_specs.append(pl.BlockSpec((1, 128), lambda i, h: (0, 0)))
    rg_shapes = [jax.ShapeDtypeStruct(into[j].shape, into[j].dtype) if j in into else _out_struct(k, w, S, tw, dt, nh)
                 for j, (_, k, w, c0, tw, dt) in enumerate(row_grads)]
    first_alias = n_in + n_p + n_ct
    return pl.pallas_call(
        body, name=name, grid=(S // ts, nh),
        in_specs=[_bspec(k, w, c0, ts) for (_, k, w, c0) in ins] + [_full_spec(p) for p in params]
        + [_bspec(k, w, c0, ts) for (_, k, w, c0) in cts] + [pl.BlockSpec(memory_space=pl.ANY)] * n_al,
        out_specs=[_bspec(k, w, c0, ts) for (_, k, w, c0, _, _) in row_grads]
        + [_bspec(k, w, c0, ts) for (k, w, c0, _, _) in extra_outs] + acc_specs,
        out_shape=rg_shapes + [_out_struct(k, w, S, tw, dt, nh) for (k, w, c0, tw, dt) in extra_outs] + acc_shapes,
        input_output_aliases={first_alias + n: j for n, j in enumerate(sorted(into))},
    )(*[a for (a, _, _, _) in ins], *params, *[a for (a, _, _, _) in cts], *[into[j] for j in sorted(into)])


def f_prologue(x, ng, sc, sh):
    return (_rms(x, ng) * (1.0 + sc) + sh,)


def f_prologue_res(x, ng, sc, sh):
    return (_rms(x, ng) * (1.0 + sc) + sh, x)


def f_loss(x, mo, tgt, gate, fg):
    y = _rms(x + gate * mo, fg)
    return (0.5 * jnp.mean(jnp.square(y - tgt), axis=-1, keepdims=True),)


def f_foxpre(fq, fk, fv, qg, kg):
    return (_rms(fq, qg) * ATT_SCALE, _rms(fk, kg), fv)


def f_postf(o, z):
    return (o * _silu(z),)


def f_postg(o, z, ng):
    return (_rms(o, ng) * _silu(z),)


def fox_norm_bwd(p_all, d_qn, d_kn, qg, kg, dp, S, ts):
    def body(p_ref, dq_ref, dk_ref, qg_ref, kg_ref, dp_in, o_ref, dqg_ref, dkg_ref):
        @pl.when(pl.program_id(0) == 0)
        def _():
            dqg_ref[...] = jnp.zeros_like(dqg_ref)
            dkg_ref[...] = jnp.zeros_like(dkg_ref)

        for ct_ref, g_ref, dg_ref, col0 in ((dq_ref, qg_ref, dqg_ref, 0), (dk_ref, kg_ref, dkg_ref, WIDTH)):
            dg = jnp.zeros((1, D_HEAD), F32)
            for h in range(N_HEADS):
                _, vjp = jax.vjp(_rms, p_ref[:, col0 + h * D_HEAD:col0 + (h + 1) * D_HEAD], g_ref[...])
                dx, dg_h = vjp(ct_ref[:, h * D_HEAD:(h + 1) * D_HEAD])
                o_ref[:, col0 + h * D_HEAD:col0 + (h + 1) * D_HEAD] = dx.astype(o_ref.dtype)
                dg = dg + dg_h
            dg_ref[...] += dg

    gain = pl.BlockSpec((1, D_HEAD), lambda i: (0, 0))
    both = pl.BlockSpec((ts, 2 * WIDTH), lambda i: (i, 0))
    one = pl.BlockSpec((ts, WIDTH), lambda i: (i, 0))
    return pl.pallas_call(
        body, name="fox_norm_bwd", grid=(S // ts,),
        in_specs=[both, one, one, gain, gain, pl.BlockSpec(memory_space=pl.ANY)],
        out_specs=[both, gain, gain],
        out_shape=[jax.ShapeDtypeStruct(dp.shape, dp.dtype), jax.ShapeDtypeStruct((1, D_HEAD), F32),
                   jax.ShapeDtypeStruct((1, D_HEAD), F32)],
        input_output_aliases={5: 0},
    )(p_all, d_qn, d_kn, qg, kg, dp)


def _chip_exchange(kind, x_ref, out_ref, send_sems, recv_sems):
    x, y, c = lax.axis_index("x"), lax.axis_index("y"), lax.axis_index("c")
    sends, arrivals = [], []
    for r in range(1, N_CHIPS):
        px, py = _flip(x, r & 2), _flip(y, r & 1)
        if kind == "scatter":
            src, dst, landed = x_ref.at[2 * px + py], out_ref.at[r - 1], out_ref.at[r - 1]
        else:
            src, dst, landed = x_ref, out_ref.at[2 * x + y], out_ref.at[2 * px + py]
        mk = functools.partial(pltpu.make_async_remote_copy, send_sem=send_sems.at[r - 1], recv_sem=recv_sems.at[r - 1],
                               device_id=(px, py, c), device_id_type=MESH_ID)
        sends.append(mk(src_ref=src, dst_ref=dst))
        arrivals.append(mk(src_ref=src, dst_ref=landed))
    return sends, arrivals


def _exchange_shape(kind, x):
    return jax.ShapeDtypeStruct(((N_CHIPS - 1,) + x.shape[1:]) if kind == "scatter" else ((N_CHIPS,) + x.shape), x.dtype)


def matmul(name, a, b, out_dtype, tm=512, tn=512, tk=2048, trans_a=False, trans_b=False, rides=()):
    M, K = a.shape[::-1] if trans_a else a.shape
    N, K2 = b.shape if trans_b else b.shape[::-1]
    assert K == K2
    tm, tn, tk = _tile(M, tm), _tile(N, tn), _tile(K, tk)
    ni, nj, nk = M // tm, N // tn, K // tk
    a_spec = pl.BlockSpec((tk, tm), lambda i, j, k: (k, i)) if trans_a else pl.BlockSpec((tm, tk), lambda i, j, k: (i, k))
    b_spec = pl.BlockSpec((tn, tk), lambda i, j, k: (j, k)) if trans_b else pl.BlockSpec((tk, tn), lambda i, j, k: (k, j))
    n_ride = len(rides)

    def body(a_ref, b_ref, *rest):
        ride_in, o_ref, ride_out = rest[:n_ride], rest[n_ride], rest[n_ride + 1:2 * n_ride + 1]
        scr = rest[2 * n_ride + 1:]
        i, j, k = pl.program_id(0), pl.program_id(1), pl.program_id(2)
        exchanges = [_chip_exchange(kind, ride_in[n], ride_out[n], scr[2 * n], scr[2 * n + 1])
                     for n, (kind, _) in enumerate(rides)]

        @pl.when((i == 0) & (j == 0) & (k == 0))
        def _():
            for sends, _ in exchanges:
                for cp in sends:
                    cp.start()

        part = lax.dot_general(a_ref[...], b_ref[...], (((0 if trans_a else 1,), (1 if trans_b else 0,)), ((), ())),
                               preferred_element_type=F32)
        if nk == 1:
            o_ref[...] = part.astype(o_ref.dtype)
        else:
            acc = scr[2 * n_ride]

            @pl.when(k == 0)
            def _():
                acc[...] = part

            @pl.when(k > 0)
            def _():
                acc[...] += part

            @pl.when(k == nk - 1)
            def _():
                o_ref[...] = acc[...].astype(o_ref.dtype)

        @pl.when((i == ni - 1) & (j == nj - 1) & (k == nk - 1))
        def _():
            for sends, arrivals in exchanges:
                for cp in arrivals:
                    cp.wait_recv()
                for cp in sends:
                    cp.wait_send()

    sems = [pltpu.SemaphoreType.DMA((N_CHIPS - 1,))] * (2 * n_ride)
    any_spec = pl.BlockSpec(memory_space=pl.ANY)
    res = pl.pallas_call(
        body, name=name, grid=(ni, nj, nk),
        in_specs=[a_spec, b_spec] + [any_spec] * n_ride,
        out_specs=[pl.BlockSpec((tm, tn), lambda i, j, k: (i, j))] + [any_spec] * n_ride,
        out_shape=[jax.ShapeDtypeStruct((M, N), out_dtype)] + [_exchange_shape(kind, x) for kind, x in rides],
        scratch_shapes=sems + ([] if nk == 1 else [pltpu.VMEM((tm, tn), F32)]),
        compiler_params=pltpu.CompilerParams(
            dimension_semantics=("arbitrary",) * 3 if rides else ("parallel", "parallel", "arbitrary")),
    )(a, b, *[x for _, x in rides])
    return tuple(res) if rides else res[0]


def _small_f(z, pv, av):
    lane = lax.broadcasted_iota(jnp.int32, z.shape, 1)
    zz = z + pv
    logf = _log_sigmoid(zz)
    gval = -jnp.exp(av) * _softplus(zz)
    beta = _sigmoid(zz)
    return jnp.where(lane < 8, logf, jnp.where(lane < 16, gval, jnp.where(lane < 24, beta, 0.0)))


def _tri(ts, upper):
    r = lax.broadcasted_iota(jnp.int32, (ts, ts), 0)
    c = lax.broadcasted_iota(jnp.int32, (ts, ts), 1)
    keep = (r <= c) if upper else (r >= c)
    same_chunk = (r // CHUNK) == (c // CHUNK)
    return keep.astype(F32), (keep & same_chunk).astype(F32)


def smalls_fwd(p_all, pv, av, S, ts):
    nt = S // ts

    def body(z_ref, pv_ref, av_ref, o_ref, carry):
        i = pl.program_id(0)

        @pl.when(i == 0)
        def _():
            carry[...] = jnp.zeros_like(carry)

        e = _small_f(z_ref[...], pv_ref[...], av_ref[...])
        lane = lax.broadcasted_iota(jnp.int32, e.shape, 1)
        l_all, l_chunk = _tri(ts, upper=False)
        cum_all = hdot(l_all, e) + carry[...]
        cum_chunk = hdot(l_chunk, e)
        carry[...] = cum_all[ts - 1:ts, :]
        o_ref[...] = jnp.where(lane < 8, cum_all, jnp.where(lane < 16, cum_chunk, e))

    return pl.pallas_call(
        body, name="smalls_fwd", grid=(nt,),
        in_specs=[pl.BlockSpec((ts, 128), lambda i: (i, CB_SMALL)), pl.BlockSpec((1, 128), lambda i: (0, 0)),
                  pl.BlockSpec((1, 128), lambda i: (0, 0))],
        out_specs=pl.BlockSpec((ts, 128), lambda i: (i, 0)),
        out_shape=jax.ShapeDtypeStruct((S, 128), F32),
        scratch_shapes=[pltpu.VMEM((1, 128), F32)],
    )(p_all, pv, av)


def smalls_bwd(p_all, pv, av, dsm, dp, S, ts):
    nt = S // ts

    def body(z_ref, pv_ref, av_ref, d_ref, dp_in, dz_ref, dpv_ref, dav_ref, carry):
        i = pl.program_id(0)

        @pl.when(i == 0)
        def _():
            carry[...] = jnp.zeros_like(carry)
            dpv_ref[...] = jnp.zeros_like(dpv_ref)
            dav_ref[...] = jnp.zeros_like(dav_ref)

        d = d_ref[...]
        lane = lax.broadcasted_iota(jnp.int32, d.shape, 1)
        u_all, u_chunk = _tri(ts, upper=True)
        rev_all = hdot(u_all, d) + carry[...]
        rev_chunk = hdot(u_chunk, d)
        carry[...] = rev_all[0:1, :]
        de = jnp.where(lane < 8, rev_all, jnp.where(lane < 16, rev_chunk, d))
        _, vjp = jax.vjp(_small_f, z_ref[...], pv_ref[...], av_ref[...])
        dz, dpv, dav = vjp(de)
        dz_ref[...] = dz.astype(dz_ref.dtype)
        dpv_ref[...] += dpv
        dav_ref[...] += dav

    rev = lambda i: (nt - 1 - i, 0)
    small_cols = pl.BlockSpec((ts, 128), lambda i: (nt - 1 - i, CB_SMALL))
    return pl.pallas_call(
        body, name="smalls_bwd", grid=(nt,),
        in_specs=[small_cols, pl.BlockSpec((1, 128), lambda i: (0, 0)), pl.BlockSpec((1, 128), lambda i: (0, 0)),
                  pl.BlockSpec((ts, 128), rev), pl.BlockSpec(memory_space=pl.ANY)],
        out_specs=[small_cols, pl.BlockSpec((1, 128), lambda i: (0, 0)), pl.BlockSpec((1, 128), lambda i: (0, 0))],
        out_shape=[jax.ShapeDtypeStruct(dp.shape, dp.dtype), jax.ShapeDtypeStruct((1, 128), F32),
                   jax.ShapeDtypeStruct((1, 128), F32)],
        scratch_shapes=[pltpu.VMEM((1, 128), F32)],
        input_output_aliases={4: 0},
    )(p_all, pv, av, dsm, dp)


def _col_to_row(col):
    return jnp.transpose(jnp.broadcast_to(col, (col.shape[0], 128)))[0:1, :]


def _row_to_col(row):
    return jnp.transpose(jnp.broadcast_to(row, (128, row.shape[1])))[:, 0:1]


def _causal_mask(s, keys_first):
    r = lax.broadcasted_iota(jnp.int32, s.shape, 0)
    c = lax.broadcasted_iota(jnp.int32, s.shape, 1)
    return jnp.where((r <= c) if keys_first else (c <= r), s, NEG)


def flash_fwd(qs, kn, vb, frow, S, tq):
    nq = S // tq
    tk = tq

    def body(q_ref, k_ref, v_ref, fr_ref, o_ref, lse_ref):
        i = pl.program_id(1)
        q = q_ref[...]
        f_base = fr_ref[0, i][:, 0:1]

        def tile(j, carry, diagonal):
            m, l, acc = carry
            off = pl.multiple_of(j * tk, tk)
            k = k_ref[pl.ds(off, tk), :]
            v = v_ref[pl.ds(off, tk), :]
            s = mm_nt(q, k) - (fr_ref[0, j] - f_base)
            if diagonal:
                s = _causal_mask(s, keys_first=False)
            m_new = jnp.maximum(m, jnp.max(s, axis=-1, keepdims=True))
            a = jnp.exp(m - m_new)
            p = jnp.exp(s - m_new)
            l = a * l + jnp.sum(p, axis=-1, keepdims=True)
            acc = a * acc + mm_nn(p, v)
            return m_new, l, acc

        init = (jnp.full((tq, 1), NEG, F32), jnp.zeros((tq, 1), F32), jnp.zeros((tq, D_HEAD), F32))
        carry = lax.fori_loop(0, i, lambda j, c: tile(j, c, False), init)
        m, l, acc = tile(i, carry, True)
        o_ref[...] = acc / l
        lse_ref[0, 0] = _col_to_row(m + jnp.log(l))

    return pl.pallas_call(
        body, name="flash_fwd", grid=(N_HEADS, nq),
        in_specs=[pl.BlockSpec((tq, D_HEAD), lambda h, i: (i, h)), pl.BlockSpec((S, D_HEAD), lambda h, i: (0, h)),
                  pl.BlockSpec((S, D_HEAD), lambda h, i: (0, h)),
                  pl.BlockSpec((1, nq, 1, tk), lambda h, i: (h, 0, 0, 0))],
        out_specs=[pl.BlockSpec((tq, D_HEAD), lambda h, i: (i, h)),
                   pl.BlockSpec((1, 1, 1, tq), lambda h, i: (h, i, 0, 0))],
        out_shape=[jax.ShapeDtypeStruct((S, WIDTH), F32), jax.ShapeDtypeStruct((N_HEADS, nq, 1, tq), F32)],
    )(qs, kn, vb, frow)


def flash_delta(do, o, S, tq):
    nq = S // tq
    per = _tile(nq, 4)

    def body(do_ref, o_ref, dl_ref):
        for n in range(per):
            rows = pl.ds(n * tq, tq)
            dl_ref[0, n] = _col_to_row(jnp.sum(do_ref[rows, :].astype(F32) * o_ref[rows, :], axis=-1, keepdims=True))

    blk = pl.BlockSpec((per * tq, D_HEAD), lambda h, i: (i, h))
    return pl.pallas_call(
        body, name="flash_delta", grid=(N_HEADS, nq // per), in_specs=[blk, blk],
        out_specs=pl.BlockSpec((1, per, 1, tq), lambda h, i: (h, i, 0, 0)),
        out_shape=jax.ShapeDtypeStruct((N_HEADS, nq, 1, tq), F32),
    )(do, o)


def flash_bwd(qs, kn, vb, do, frow, lse_row, delta_row, dp, S, tq):
    nq = S // tq
    tk = tq

    def body(q_ref, k_ref, v_ref, do_ref, fr_ref, lse_ref, dl_ref, dp_in, dq_ref, dk_ref, dv_ref, dfq_ref, dfk_ref):
        j = pl.program_id(1)
        k = k_ref[...]
        v = v_ref[...]
        fk = _row_to_col(fr_ref[0, j])

        @pl.when(j == 0)
        def _():
            dq_ref[...] = jnp.zeros_like(dq_ref)
            dfq_ref[...] = jnp.zeros_like(dfq_ref)

        def tile(i, carry, diagonal):
            dk, dv, dfk = carry
            off = pl.multiple_of(i * tq, tq)
            q = q_ref[pl.ds(off, tq), :]
            do_ = do_ref[pl.ds(off, tq), :]
            st = mm_nt(k, q) - (fk - fr_ref[0, i][:, 0:1])
            if diagonal:
                st = _causal_mask(st, keys_first=True)
            pt = jnp.exp(st - lse_ref[0, i])
            dst = pt * (mm_nt(v, do_) - dl_ref[0, i])
            dq_ref[pl.ds(off, tq), :] += mm_tn(dst, k) * ATT_SCALE
            dfq_ref[0, i] += jnp.sum(dst, axis=0, keepdims=True)
            return dk + mm_nn(dst, q), dv + mm_nn(pt, do_), dfk - jnp.sum(dst, axis=-1, keepdims=True)

        z = jnp.zeros((tk, D_HEAD), F32)
        carry = tile(j, (z, z, jnp.zeros((tk, 1), F32)), True)
        dk, dv, dfk = lax.fori_loop(j + 1, nq, lambda i, c: tile(i, c, False), carry)
        dk_ref[...] = dk
        dv_ref[...] = dv.astype(dv_ref.dtype)
        dfk_ref[0, 0] = _col_to_row(dfk)

    blk = pl.BlockSpec((tk, D_HEAD), lambda h, j: (j, h))
    full = pl.BlockSpec((S, D_HEAD), lambda h, j: (0, h))
    rows = pl.BlockSpec((1, nq, 1, tq), lambda h, j: (h, 0, 0, 0))
    stat = jax.ShapeDtypeStruct((N_HEADS, nq, 1, tq), F32)
    return pl.pallas_call(
        body, name="flash_bwd", grid=(N_HEADS, nq),
        in_specs=[full, blk, blk, full, rows, rows, rows, pl.BlockSpec(memory_space=pl.ANY)],
        out_specs=[full, blk, pl.BlockSpec((tk, D_HEAD), lambda h, j: (j, CB_FV + h)), rows,
                   pl.BlockSpec((1, 1, 1, tk), lambda h, j: (h, j, 0, 0))],
        out_shape=[jax.ShapeDtypeStruct((S, WIDTH), F32), jax.ShapeDtypeStruct((S, WIDTH), F32),
                   jax.ShapeDtypeStruct(dp.shape, dp.dtype), stat, stat],
        input_output_aliases={7: 2},
    )(qs, kn, vb, do, frow, lse_row, delta_row, dp)


CONV_COLS = 3 * WIDTH
CONV_TC = 512


def _gdn_act(y, group):
    s = _silu(y)
    if group == 2:
        return s
    return _l2n(s) * ATT_SCALE if group == 0 else _l2n(s)


def _gdn_act_bwd(y, d, group):
    _, vjp = jax.vjp(functools.partial(_gdn_act, group=group), y)
    return vjp(d)[0]


def conv_act_fwd(p_all, conv_w, S, ts):
    nt, tc = S // ts, CONV_TC
    cb0 = CB_GQ * 128 // tc
    r8 = ts // 8
    tiles_per_group = WIDTH // tc

    def body(x_ref, halo_ref, w_ref, y_ref, a_ref, scr):
        c, i = pl.program_id(0), pl.program_id(1)
        scr[0:8, :] = jnp.where(i > 0, halo_ref[...], 0.0)
        scr[8:8 + ts, :] = x_ref[...]
        w = w_ref[...]
        y = w[3:4, :] * x_ref[...]
        for j in range(CONV_K - 1):
            y = y + w[j:j + 1, :] * scr[5 + j:5 + j + ts, :]
        y_ref[...] = y
        for group in range(3):
            @pl.when(c // tiles_per_group == group)
            def _(group=group):
                for h in range(tc // D_HEAD):
                    sl = slice(h * D_HEAD, (h + 1) * D_HEAD)
                    a_ref[:, sl] = _gdn_act(y_ref[:, sl], group)

    out = pl.BlockSpec((ts, tc), lambda c, i: (i, c))
    return pl.pallas_call(
        body, name="conv_act_fwd", grid=(CONV_COLS // tc, nt),
        in_specs=[pl.BlockSpec((ts, tc), lambda c, i: (i, cb0 + c)),
                  pl.BlockSpec((8, tc), lambda c, i: (jnp.maximum(i * r8 - 1, 0), cb0 + c)),
                  pl.BlockSpec((CONV_K, tc), lambda c, i: (0, c))],
        out_specs=[out, out],
        out_shape=[jax.ShapeDtypeStruct((S, CONV_COLS), F32)] * 2,
        scratch_shapes=[pltpu.VMEM((ts + 8, tc), F32)],
    )(p_all, p_all, conv_w)


def conv_act_bwd(p_all, y, dqkv, conv_w, dp, S, ts):
    nt, tc = S // ts, CONV_TC
    cb0 = CB_GQ * 128 // tc
    r8 = ts // 8
    tiles_per_group = WIDTH // tc

    def body(x_ref, xh_ref, y_ref, yh_ref, d_ref, dh_ref, w_ref, dp_in, dx_ref, dw_ref, xs, ds):
        c, i = pl.program_id(0), pl.program_id(1)
        xs[0:8, :] = jnp.where(i > 0, xh_ref[...], 0.0)
        xs[8:8 + ts, :] = x_ref[...]
        for group in range(3):
            @pl.when(c // tiles_per_group == group)
            def _(group=group):
                for h in range(tc // D_HEAD):
                    sl = slice(h * D_HEAD, (h + 1) * D_HEAD)
                    ds[0:ts, sl] = _gdn_act_bwd(y_ref[:, sl], d_ref[:, sl], group)
                    ds[ts:ts + 8, sl] = jnp.where(i < nt - 1, _gdn_act_bwd(yh_ref[:, sl], dh_ref[:, sl], group), 0.0)
        w = w_ref[...]
        d = ds[0:ts, :]
        dx = w[3:4, :] * d
        for j in range(CONV_K - 1):
            dx = dx + w[j:j + 1, :] * ds[3 - j:3 - j + ts, :]
        dx_ref[...] = dx.astype(dx_ref.dtype)

        @pl.when(i == 0)
        def _():
            dw_ref[...] = jnp.zeros_like(dw_ref)

        rows = [jnp.sum(d * xs[5 + j:5 + j + ts, :], axis=0, keepdims=True) for j in range(CONV_K - 1)]
        rows.append(jnp.sum(d * x_ref[...], axis=0, keepdims=True))
        dw_ref[...] += jnp.concatenate(rows, axis=0)

    tile = pl.BlockSpec((ts, tc), lambda c, i: (i, c))
    next8 = pl.BlockSpec((8, tc), lambda c, i: (jnp.minimum((i + 1) * r8, nt * r8 - 1), c))
    return pl.pallas_call(
        body, name="conv_act_bwd", grid=(CONV_COLS // tc, nt),
        in_specs=[pl.BlockSpec((ts, tc), lambda c, i: (i, cb0 + c)),
                  pl.BlockSpec((8, tc), lambda c, i: (jnp.maximum(i * r8 - 1, 0), cb0 + c)),
                  tile, next8, tile, next8,
                  pl.BlockSpec((CONV_K, tc), lambda c, i: (0, c)), pl.BlockSpec(memory_space=pl.ANY)],
        out_specs=[pl.BlockSpec((ts, tc), lambda c, i: (i, cb0 + c)), pl.BlockSpec((CONV_K, tc), lambda c, i: (0, c))],
        out_shape=[jax.ShapeDtypeStruct(dp.shape, dp.dtype), jax.ShapeDtypeStruct((CONV_K, CONV_COLS), F32)],
        scratch_shapes=[pltpu.VMEM((ts + 8, tc), F32), pltpu.VMEM((ts + 8, tc), F32)],
        input_output_aliases={7: 0},
    )(p_all, p_all, y, y, dqkv, dqkv, conv_w, dp)


def _bdot(a, b, ca, cb):
    return lax.dot_general(a.astype(MXU_DTYPE), b.astype(MXU_DTYPE), (((ca,), (cb,)), ((0,), (0,))),
                           preferred_element_type=F32)


@jax.custom_vjp
def bmm_nn(a, b):
    return _bdot(a, b, 2, 1)


@jax.custom_vjp
def bmm_nt(a, b):
    return _bdot(a, b, 2, 2)


@jax.custom_vjp
def bmm_tn(a, b):
    return _bdot(a, b, 1, 1)


bmm_nn.defvjp(lambda a, b: (bmm_nn(a, b), (a, b)), lambda r, g: (bmm_nt(g, r[1]), bmm_tn(r[0], g)))
bmm_nt.defvjp(lambda a, b: (bmm_nt(a, b), (a, b)), lambda r, g: (bmm_nn(g, r[1]), bmm_tn(g, r[0])))
bmm_tn.defvjp(lambda a, b: (bmm_tn(a, b), (a, b)), lambda r, g: (bmm_nt(r[1], g), bmm_nn(r[0], g)))


def bdot_hi(a, b, ca=2, cb=1):
    return lax.dot_general(a, b, (((ca,), (cb,)), ((0,), (0,))), precision=SOLVE_PRECISION,
                           preferred_element_type=F32)


def _tri_inv_product(m):
    ii = lax.broadcasted_iota(jnp.int32, m.shape, 1)
    jj = lax.broadcasted_iota(jnp.int32, m.shape, 2)
    t = (ii == jj).astype(F32) - m
    pw = bdot_hi(m, m)
    for _ in range(4):
        t = t + bdot_hi(t, pw)
        pw = bdot_hi(pw, pw)
    return t + bdot_hi(t, pw)


def _tri_inv_bwd(t, g):
    return -bdot_hi(bdot_hi(t, g, 1, 1), t, 2, 2)


@jax.custom_vjp
def tri_inv(m):
    return _tri_inv_product(m)


@jax.custom_vjp
def tri_inv_known(m, t):
    return t


tri_inv.defvjp(lambda m: (lambda t: (t, t))(_tri_inv_product(m)), lambda t, g: (_tri_inv_bwd(t, g),))
tri_inv_known.defvjp(lambda m, t: (t, t), lambda t, g: (_tri_inv_bwd(t, g), jnp.zeros_like(t)))


def chunk_fn(q, k, v, gc_col, gc_row, beta, state, t_known=None):
    shape = (N_HEADS, CHUNK, CHUNK)
    ii = lax.broadcasted_iota(jnp.int32, shape, 1)
    jj = lax.broadcasted_iota(jnp.int32, shape, 2)
    lower = ii >= jj
    strict = ii > jj
    decay = jnp.exp(jnp.where(lower, gc_col - gc_row, NEG))
    kb = k * beta
    vb = v * beta
    m = jnp.where(strict, bmm_nt(kb, k) * decay, 0.0)
    t = tri_inv(m) if t_known is None else tri_inv_known(m, t_known)
    u = bdot_hi(t, vb)
    w = bdot_hi(t, kb * jnp.exp(gc_col))
    attn = jnp.where(lower, bmm_nt(q, k) * decay, 0.0)
    v_new = u - bmm_nn(w, state)
    o = bmm_nn(q * jnp.exp(gc_col), state) + bmm_nn(attn, v_new)
    g_last = gc_col[:, CHUNK - 1:CHUNK, :]
    k_dec = k * jnp.exp(g_last - gc_col)
    new_state = state * jnp.exp(g_last) + bmm_tn(k_dec, v_new)
    return (o, new_state, t) if t_known is None else (o, new_state)


def _heads(ref):
    return jnp.stack([ref[:, h * D_HEAD:(h + 1) * D_HEAD] for h in range(N_HEADS)], axis=0)


def _head_cols(ref, lane0):
    return jnp.stack([ref[:, lane0 + h:lane0 + h + 1] for h in range(N_HEADS)], axis=0)


def gdn_fwd(qkv, smalls, gc_row, S):
    n = S // CHUNK

    def body(q_ref, k_ref, v_ref, sm_ref, gr_ref, o_ref, st_ref, t_ref, state):
        @pl.when(pl.program_id(0) == 0)
        def _():
            state[...] = jnp.zeros_like(state)

        s0 = state[...]
        st_ref[0] = s0
        o, s1, t = chunk_fn(_heads(q_ref), _heads(k_ref), _heads(v_ref), _head_cols(sm_ref, 8),
                            gr_ref[0][:, None, :], _head_cols(sm_ref, 16), s0)
        for h in range(N_HEADS):
            o_ref[:, h * D_HEAD:(h + 1) * D_HEAD] = o[h]
        state[...] = s1
        t_ref[0] = t

    return pl.pallas_call(
        body, name="gdn_chunk_fwd", grid=(n,),
        in_specs=[pl.BlockSpec((CHUNK, WIDTH), lambda i, g=g: (i, g)) for g in range(3)]
        + [pl.BlockSpec((CHUNK, 128), lambda i: (i, 0)), pl.BlockSpec((1, N_HEADS, CHUNK), lambda i: (i, 0, 0))],
        out_specs=[pl.BlockSpec((CHUNK, WIDTH), lambda i: (i, 0)),
                   pl.BlockSpec((1, N_HEADS, D_HEAD, D_HEAD), lambda i: (i, 0, 0, 0)),
                   pl.BlockSpec((1, N_HEADS, CHUNK, CHUNK), lambda i: (i, 0, 0, 0))],
        out_shape=[jax.ShapeDtypeStruct((S, WIDTH), F32), jax.ShapeDtypeStruct((n, N_HEADS, D_HEAD, D_HEAD), F32),
                   jax.ShapeDtypeStruct((n, N_HEADS, CHUNK, CHUNK), F32)],
        scratch_shapes=[pltpu.VMEM((N_HEADS, D_HEAD, D_HEAD), F32)],
    )(qkv, qkv, qkv, smalls, gc_row)


def gdn_bwd(qkv, smalls, gc_row, states, tinv, do, S):
    n = S // CHUNK

    def body(q_ref, k_ref, v_ref, sm_ref, gr_ref, st_ref, t_ref, do_ref, dqkv_ref, dsm_ref, dgr_ref, dstate):
        @pl.when(pl.program_id(0) == 0)
        def _():
            dstate[...] = jnp.zeros_like(dstate)

        t_saved = t_ref[0]
        _, vjp = jax.vjp(lambda *a: chunk_fn(*a, t_known=t_saved), _heads(q_ref), _heads(k_ref), _heads(v_ref),
                         _head_cols(sm_ref, 8), gr_ref[0][:, None, :], _head_cols(sm_ref, 16), st_ref[0])
        dq, dk, dv, dgc, dgr, dbt, ds = vjp((_heads(do_ref), dstate[...]))
        for h in range(N_HEADS):
            dqkv_ref[:, h * D_HEAD:(h + 1) * D_HEAD] = dq[h]
            dqkv_ref[:, WIDTH + h * D_HEAD:WIDTH + (h + 1) * D_HEAD] = dk[h]
            dqkv_ref[:, 2 * WIDTH + h * D_HEAD:2 * WIDTH + (h + 1) * D_HEAD] = dv[h]
        dstate[...] = ds
        cols = [dgc[h] for h in range(N_HEADS)] + [dbt[h] for h in range(N_HEADS)]
        dsm_ref[...] = jnp.concatenate([jnp.zeros((CHUNK, 8), F32)] + cols + [jnp.zeros((CHUNK, 128 - 24), F32)],
                                       axis=1)
        dgr_ref[0] = jnp.concatenate([dgr[h] for h in range(N_HEADS)], axis=0)

    rev = lambda c: (lambda i: (n - 1 - i, c))
    return pl.pallas_call(
        body, name="gdn_chunk_bwd", grid=(n,),
        in_specs=[pl.BlockSpec((CHUNK, WIDTH), rev(g)) for g in range(3)]
        + [pl.BlockSpec((CHUNK, 128), rev(0)), pl.BlockSpec((1, N_HEADS, CHUNK), lambda i: (n - 1 - i, 0, 0)),
           pl.BlockSpec((1, N_HEADS, D_HEAD, D_HEAD), lambda i: (n - 1 - i, 0, 0, 0)),
           pl.BlockSpec((1, N_HEADS, CHUNK, CHUNK), lambda i: (n - 1 - i, 0, 0, 0)),
           pl.BlockSpec((CHUNK, WIDTH), rev(0))],
        out_specs=[pl.BlockSpec((CHUNK, 3 * WIDTH), rev(0)), pl.BlockSpec((CHUNK, 128), rev(0)),
                   pl.BlockSpec((1, N_HEADS, CHUNK), lambda i: (n - 1 - i, 0, 0))],
        out_shape=[jax.ShapeDtypeStruct((S, 3 * WIDTH), F32)] + [jax.ShapeDtypeStruct((S, 128), F32),
                                                                  jax.ShapeDtypeStruct((n, N_HEADS, CHUNK), F32)],
        scratch_shapes=[pltpu.VMEM((N_HEADS, D_HEAD, D_HEAD), F32)],
    )(qkv, qkv, qkv, smalls, gc_row, states, tinv, do)


def _flip(a, bit):
    return 1 - a if bit else a


def allgather8(name, buf):
    R, W = buf.shape

    def body(x_ref, out_ref, send_sems, recv_sems, local_sem):
        x, y, c = lax.axis_index("x"), lax.axis_index("y"), lax.axis_index("c")

        def slot(px, py, pc):
            return out_ref.at[4 * px + 2 * py + pc]

        mine = pltpu.make_async_copy(x_ref, slot(x, y, c), local_sem)
        mine.start()
        sends = []
        for r in range(1, N_DEV):
            peer = (_flip(x, r & 4), _flip(y, r & 2), _flip(c, r & 1))
            cp = pltpu.make_async_remote_copy(src_ref=x_ref, dst_ref=slot(x, y, c), send_sem=send_sems.at[r - 1],
                                              recv_sem=recv_sems.at[r - 1], device_id=peer, device_id_type=MESH_ID)
            cp.start()
            sends.append(cp)
        for r in range(1, N_DEV):
            peer = (_flip(x, r & 4), _flip(y, r & 2), _flip(c, r & 1))
            pltpu.make_async_remote_copy(src_ref=x_ref, dst_ref=slot(*peer), send_sem=send_sems.at[r - 1],
                                         recv_sem=recv_sems.at[r - 1], device_id=peer,
                                         device_id_type=MESH_ID).wait_recv()
        for cp in sends:
            cp.wait_send()
        mine.wait()

    return pl.pallas_call(
        body, name=name,
        out_shape=jax.ShapeDtypeStruct((N_DEV, R, W), buf.dtype),
        in_specs=[pl.BlockSpec(memory_space=pltpu.VMEM)],
        out_specs=pl.BlockSpec(memory_space=pltpu.VMEM),
        scratch_shapes=[pltpu.SemaphoreType.DMA((N_DEV - 1,)), pltpu.SemaphoreType.DMA((N_DEV - 1,)),
                        pltpu.SemaphoreType.DMA],
    )(buf)


def allgather_chips(name, blk):
    R, W = blk.shape
    half = R // 2
    n_far = N_CHIPS - 1

    def body(x_ref, out_ref, send_sems, recv_sems):
        x, y, c = lax.axis_index("x"), lax.axis_index("y"), lax.axis_index("c")
        sibling = (x, y, 1 - c)
        chips = [(_flip(x, r & 2), _flip(y, r & 1)) for r in range(1, N_CHIPS)]

        def rows(px, py, pc):
            return out_ref.at[2 * px + py, pc]

        def copy(k, dst, to, src):
            return pltpu.make_async_remote_copy(src_ref=src, dst_ref=dst, send_sem=send_sems.at[k],
                                                recv_sem=recv_sems.at[k], device_id=to, device_id_type=MESH_ID)

        my_half = x_ref.at[c]
        first = [copy(j, rows(x, y, c), (*chip, c), my_half) for j, chip in enumerate(chips)]
        for cp in first:
            cp.start()
        passed = [copy(n_far + j, rows(*chip, c), sibling, rows(*chip, c)) for j, chip in enumerate(chips)]
        for j, chip in enumerate(chips):
            copy(j, rows(*chip, c), (*chip, c), my_half).wait_recv()
            passed[j].start()
        for j, chip in enumerate(chips):
            copy(n_far + j, rows(*chip, 1 - c), sibling, my_half).wait_recv()
        for cp in first + passed:
            cp.wait_send()

    far = pl.pallas_call(
        body, name=name,
        out_shape=jax.ShapeDtypeStruct((N_CHIPS, 2, half, W), blk.dtype),
        in_specs=[pl.BlockSpec(memory_space=pl.ANY)],
        out_specs=pl.BlockSpec(memory_space=pl.ANY),
        scratch_shapes=[pltpu.SemaphoreType.DMA((2 * n_far,)), pltpu.SemaphoreType.DMA((2 * n_far,))],
    )(blk.reshape(2, half, W)).reshape(N_CHIPS, R, W)
    own_chip = 2 * lax.axis_index("x") + lax.axis_index("y")
    return lax.dynamic_update_index_in_dim(far, blk, own_chip, axis=0)


def sibling_send_other_half(name, g4):
    n, R, W = g4.shape
    half = R // 2

    def body(x_ref, out_ref, send_sem, recv_sem):
        x, y, c = lax.axis_index("x"), lax.axis_index("y"), lax.axis_index("c")
        cp = pltpu.make_async_remote_copy(src_ref=x_ref.at[:, pl.ds((1 - c) * half, half), :], dst_ref=out_ref,
                                          send_sem=send_sem, recv_sem=recv_sem, device_id=(x, y, 1 - c),
                                          device_id_type=MESH_ID)
        cp.start()
        cp.wait_recv()
        cp.wait_send()

    return pl.pallas_call(
        body, name=name,
        out_shape=jax.ShapeDtypeStruct((n, half, W), g4.dtype),
        in_specs=[pl.BlockSpec(memory_space=pl.ANY)],
        out_specs=pl.BlockSpec(memory_space=pl.ANY),
        scratch_shapes=[pltpu.SemaphoreType.DMA, pltpu.SemaphoreType.DMA],
    )(g4)


def sibling_merge(name, mine):
    def body(x_ref, out_ref, send_sem, recv_sem):
        x, y, c = lax.axis_index("x"), lax.axis_index("y"), lax.axis_index("c")
        cp = pltpu.make_async_remote_copy(src_ref=x_ref, dst_ref=out_ref, send_sem=send_sem, recv_sem=recv_sem,
                                          device_id=(x, y, 1 - c), device_id_type=MESH_ID)
        cp.start()
        cp.wait_recv()
        cp.wait_send()

    other = pl.pallas_call(
        body, name=name,
        out_shape=jax.ShapeDtypeStruct(mine.shape, mine.dtype),
        in_specs=[pl.BlockSpec(memory_space=pl.ANY)],
        out_specs=pl.BlockSpec(memory_space=pl.ANY),
        scratch_shapes=[pltpu.SemaphoreType.DMA, pltpu.SemaphoreType.DMA],
    )(mine)
    first = lax.axis_index("c") == 0
    return jnp.concatenate([jnp.where(first, mine, other), jnp.where(first, other, mine)], axis=0)


def sum_parts(name, own, recv, tr):
    R, W = own.shape
    tr = _tile(R, tr)

    def body(o_ref, r_ref, out_ref):
        acc = o_ref[...]
        for k in range(recv.shape[0]):
            acc = acc + r_ref[k].astype(F32)
        out_ref[...] = acc

    return pl.pallas_call(
        body, name=name, grid=(R // tr,),
        in_specs=[pl.BlockSpec((tr, W), lambda i: (i, 0)), pl.BlockSpec((recv.shape[0], tr, W), lambda i: (0, i, 0))],
        out_specs=pl.BlockSpec((tr, W), lambda i: (i, 0)),
        out_shape=jax.ShapeDtypeStruct((R, W), F32),
    )(own, recv)


def sum_own_half(name, g4, recv, c_idx, tr):
    n, R, W = g4.shape
    half = R // 2
    tr = _tile(half, tr)
    nb = half // tr

    def body(c_ref, g_ref, r_ref, o_ref, ob_ref):
        s = g_ref[0] + r_ref[0]
        o_ref[0] = s
        ob_ref[0] = s.astype(ob_ref.dtype)

    mine = pl.BlockSpec((1, tr, W), lambda j, i, c: (j, i, 0))
    return pl.pallas_call(
        body, name=name,
        grid_spec=pltpu.PrefetchScalarGridSpec(
            num_scalar_prefetch=1, grid=(n, nb),
            in_specs=[pl.BlockSpec((1, tr, W), lambda j, i, c: (j, c[0] * nb + i, 0)), mine],
            out_specs=[mine, mine]),
        out_shape=[jax.ShapeDtypeStruct((n, half, W), F32), jax.ShapeDtypeStruct((n, half, W), MXU_DTYPE)],
    )(c_idx, g4, recv)


def sum_devices(name, g):
    def body(g_ref, o_ref):
        acc = g_ref[0]
        for d in range(1, N_DEV):
            acc = acc + g_ref[d]
        o_ref[...] = acc

    return pl.pallas_call(body, name=name, out_shape=jax.ShapeDtypeStruct(g.shape[1:], F32))(g)


def ada_fwd(c_all, w_ada):
    K, N = w_ada.shape
    tn = _tile(N, 512)

    def body(c_ref, w_ref, o_ref):
        o_ref[...] = mm_nn(_silu(c_ref[...]), w_ref[...])

    return pl.pallas_call(
        body, name="ada_fwd", grid=(N // tn,),
        in_specs=[pl.BlockSpec((N_DEV, K), lambda j: (0, 0)), pl.BlockSpec((K, tn), lambda j: (0, j))],
        out_specs=pl.BlockSpec((N_DEV, tn), lambda j: (0, j)),
        out_shape=jax.ShapeDtypeStruct((N_DEV, N), F32),
    )(c_all, w_ada)


def ada_grad(c_all, dmod):
    K = c_all.shape[1]
    N = dmod.shape[1]
    tn = _tile(N, 512)

    def body(c_ref, d_ref, o_ref):
        o_ref[...] = mm_tn(_silu(c_ref[...]), d_ref[...])

    return pl.pallas_call(
        body, name="ada_grad", grid=(N // tn,),
        in_specs=[pl.BlockSpec((N_DEV, K), lambda j: (0, 0)), pl.BlockSpec((N_DEV, tn), lambda j: (0, j))],
        out_specs=pl.BlockSpec((K, tn), lambda j: (0, j)),
        out_shape=jax.ShapeDtypeStruct((K, N), F32),
    )(c_all, dmod)


def adamw(name, w, gparts, m, v, tr):
    R, W = w.shape
    tr = _tile(R, tr)
    ng = len(gparts)

    def body(w_ref, *refs):
        g_refs, (m_ref, v_ref, g_out, d_out, m_out, v_out) = refs[:ng], refs[ng:]
        g = g_refs[0][...]
        for r in g_refs[1:]:
            g = g + r[...]
        m1 = ADAM_B1 * m_ref[...] + (1.0 - ADAM_B1) * g
        v1 = ADAM_B2 * v_ref[...] + (1.0 - ADAM_B2) * jnp.square(g)
        m_hat = m1 / (1.0 - ADAM_B1 ** ADAM_STEP)
        v_hat = v1 / (1.0 - ADAM_B2 ** ADAM_STEP)
        g_out[...] = g
        d_out[...] = -ADAM_LR * (m_hat / (jnp.sqrt(v_hat) + ADAM_EPS) + ADAM_WD * w_ref[...])
        m_out[...] = m1
        v_out[...] = v1

    spec = pl.BlockSpec((tr, W), lambda i: (i, 0))
    return pl.pallas_call(
        body, name=name, grid=(R // tr,),
        in_specs=[spec] * (3 + ng), out_specs=[spec] * 4,
        out_shape=[jax.ShapeDtypeStruct((R, W), F32)] * 4,
    )(w, *gparts, m, v)


_REF_SEGMENTS = ((0, 4 * WIDTH, 0), (4 * WIDTH, 4 * WIDTH + 8, 4 * WIDTH), (4 * WIDTH + 8, 8 * WIDTH + 8, -8),
                 (8 * WIDTH + 8, IN_WIDTH, 0))


def _internal_from_blocks(blocks, n):
    a = 4 * WIDTH
    pieces = []
    for lo, hi in ((0, a), (a + 8, 2 * a + 8), (a, a + 8), (2 * a + 8, IN_WIDTH)):
        for j in range(N_CHIPS):
            s, e = max(lo, j * n), min(hi, (j + 1) * n)
            if s < e:
                pieces.append(blocks[j][:, s - j * n:e - j * n])
    pieces.append(jnp.zeros(blocks.shape[1:2] + (128 - N_SMALL,), blocks.dtype))
    return jnp.concatenate(pieces, axis=1)


def _block_from_internal(g, j, n):
    pieces = []
    for lo, hi, shift in _REF_SEGMENTS:
        s, e = max(lo, j * n), min(hi, (j + 1) * n)
        if s < e:
            pieces.append(g[:, s + shift:e + shift])
    return jnp.concatenate(pieces, axis=1)


def _pad_lanes(v, n=128):
    return jnp.pad(v, ((0, 0), (0, n - v.shape[1])))


def kernel(x, c, norm_g, w_ada, b_ada, w_in, b_fgate, fox_qn_g, fox_kn_g, gdn_conv_w, gdn_A_log, gdn_dt_bias, gdn_norm_g, w_out, final_g, loss_target, m_norm_g, m_w_ada, m_b_ada, m_w_in, m_b_fgate, m_fox_qn_g, m_fox_kn_g, m_gdn_conv_w, m_gdn_A_log, m_gdn_dt_bias, m_gdn_norm_g, m_w_out, m_final_g, v_norm_g, v_w_ada, v_b_ada, v_w_in, v_b_fgate, v_fox_qn_g, v_fox_kn_g, v_gdn_conv_w, v_gdn_A_log, v_gdn_dt_bias, v_gdn_norm_g, v_w_out, v_final_g):
    S = x.shape[1]
    H = N_HEADS
    ix, iy, ic = lax.axis_index("x"), lax.axis_index("y"), lax.axis_index("c")
    chip = 2 * ix + iy
    me = 2 * chip + ic
    x2, tgt = x[0], loss_target[0]
    ts_wide = _tile(S, 256)
    ts = _tile(S, 512)
    ts_head = _tile(S, 2048)
    tq = _tile(S, 1024)
    nq = S // tq
    nch = S // CHUNK
    conv_cols_chip = CONV_COLS // N_CHIPS

    pay = jnp.concatenate([c, _pad_lanes(gdn_conv_w[0], D_MODEL), jnp.zeros((3, D_MODEL), F32)], axis=0)
    g1 = allgather8("ag_c", pay)
    c_all = g1[:, 0, :]
    conv_w = jnp.concatenate([g1[2 * j, 1:1 + CONV_K, :conv_cols_chip] for j in range(N_CHIPS)], axis=1)
    g2 = allgather8("ag_mod", ada_fwd(c_all, w_ada[0]))
    mod_rows = jnp.concatenate([g2[2 * j] for j in range(N_CHIPS)], axis=1)
    mod = lax.dynamic_slice_in_dim(mod_rows, me, 1, axis=0) + b_ada
    shift, scale, gate = mod[:, :D_MODEL], mod[:, D_MODEL:2 * D_MODEL], mod[:, 2 * D_MODEL:]

    n_in = w_in.shape[2]
    win4 = allgather_chips("ag_w_in", w_in[0].astype(MXU_DTYPE))
    w_all = _internal_from_blocks(win4, n_in)
    w_out_b = w_out[0].astype(MXU_DTYPE)

    (h_b,) = rowwise_fwd("prologue", f_prologue, S, ts_wide, 1, [(x2, "row", D_MODEL, 0)], [norm_g, scale, shift],
                         [("row", D_MODEL, 0, D_MODEL, MXU_DTYPE)])
    p_all, wout_far = matmul("in_proj", h_b, w_all, F32, tm=1024, tn=1664, tk=2048, rides=[("gather", w_out_b)])
    w_out_full = lax.dynamic_update_index_in_dim(wout_far, w_out_b, chip, axis=0).reshape(2 * WIDTH, D_MODEL)
    pv = jnp.concatenate([b_fgate, gdn_dt_bias, jnp.zeros((1, 128 - 16), F32)], axis=1)
    av = jnp.concatenate([jnp.zeros((1, 8), F32), gdn_A_log, jnp.zeros((1, 128 - 16), F32)], axis=1)
    smalls = smalls_fwd(p_all, pv, av, S, ts)
    frow = jnp.transpose(smalls[:, :H]).reshape(H, nq, 1, tq)
    gc_row = jnp.transpose(smalls[:, 8:16].reshape(nch, CHUNK, H), (0, 2, 1))
    head_b = ("head", D_HEAD, 0, WIDTH, MXU_DTYPE)
    head_f = ("head", D_HEAD, 0, WIDTH, F32)
    pcol = lambda cb: (p_all, "head", D_HEAD, cb)
    qn, kn, vb = rowwise_fwd("fox_pre", f_foxpre, S, ts_head, H,[pcol(CB_FQ), pcol(CB_FK), pcol(CB_FV)],
                             [fox_qn_g, fox_kn_g], [head_b] * 3)
    o_fox, lse = flash_fwd(qn, kn, vb, frow, S, tq)
    y_conv, qkv = conv_act_fwd(p_all, conv_w, S, ts)
    o_gdn, states, tinv = gdn_fwd(qkv, smalls, gc_row, S)
    mixed = lax.empty((S, 2 * WIDTH), MXU_DTYPE)
    (mixed,) = rowwise_fwd("post_fox", f_postf, S, ts_head, H,[(o_fox, "head", D_HEAD, 0), pcol(CB_FZ)], [],
                           [("head", D_HEAD, 0, 2 * WIDTH, MXU_DTYPE, mixed)])
    (mixed,) = rowwise_fwd("post_gdn", f_postg, S, ts_head, H,[(o_gdn, "head", D_HEAD, 0), pcol(CB_GZ)], [gdn_norm_g],
                           [("head", D_HEAD, H, 2 * WIDTH, MXU_DTYPE, mixed)])
    mo = matmul("out_proj", mixed, w_out_full, F32, tm=1024, tn=2048, tk=2048)

    wide = lambda a: (a, "row", D_MODEL, 0)
    dx_res, dmo, d_gate, d_final_g, loss_acc = rowwise_bwd(
        "loss", f_loss, S, ts_wide, 1, [wide(x2), wide(mo), wide(tgt)], [gate, final_g[None, :]], [],
        [(0, "row", D_MODEL, 0, D_MODEL, F32), (1, "row", D_MODEL, 0, D_MODEL, MXU_DTYPE)], primal_sum=True)

    d_mixed = matmul("d_mixed", dmo, w_out_full, F32, tm=1024, tn=2048, tk=2048, trans_b=True)
    g_w_out = matmul("g_w_out", mixed, dmo, F32, tm=1024, tn=2048, tk=2048, trans_a=True)
    dp = lax.empty((S, P_COLS), MXU_DTYPE)
    into_dp = lambda idx, cb: (idx, "head", D_HEAD, cb, P_COLS, MXU_DTYPE, dp)
    do_fox, dp = rowwise_bwd(
        "post_fox_bwd", f_postf, S, ts_head, H,[(o_fox, "head", D_HEAD, 0), pcol(CB_FZ)], [],
        [(d_mixed, "head", D_HEAD, 0)], [(0,) + head_b, into_dp(1, CB_FZ)])
    do_gdn, dp, d_gdn_norm_g = rowwise_bwd(
        "post_gdn_bwd", f_postg, S, ts_head, H,[(o_gdn, "head", D_HEAD, 0), pcol(CB_GZ)], [gdn_norm_g],
        [(d_mixed, "head", D_HEAD, 8)], [(0,) + head_f, into_dp(1, CB_GZ)])
    d_qn, d_kn, dp, dfq_row, dfk_row = flash_bwd(qn, kn, vb, do_fox, frow, lse, flash_delta(do_fox, o_fox, S, tq), dp,
                                                 S, tq)
    dp, d_qn_g, d_kn_g = fox_norm_bwd(p_all, d_qn, d_kn, fox_qn_g, fox_kn_g, dp, S, ts_wide)
    d_qkv, dsm_chunk, dgc_row = gdn_bwd(qkv, smalls, gc_row, states, tinv, do_gdn, S)
    dp, d_conv_w = conv_act_bwd(p_all, y_conv, d_qkv, conv_w, dp, S, ts)
    d_f = jnp.transpose((dfq_row + dfk_row).reshape(H, S))
    d_gc = jnp.transpose(dgc_row, (0, 2, 1)).reshape(S, H)
    dsm = dsm_chunk + jnp.concatenate([d_f, d_gc, jnp.zeros((S, 128 - 16), F32)], axis=1)
    dp, d_pv, d_av = smalls_bwd(p_all, pv, av, dsm, dp, S, ts)
    g_w_all = matmul("g_w_in", h_b, dp, F32, tm=1024, tn=1664, tk=2048, trans_a=True)

    c_idx = jnp.reshape(ic, (1,)).astype(jnp.int32)

    def chip_sums(tag, g4):
        recv = sibling_send_other_half("d2d_" + tag, g4)
        return sum_own_half("sum2_" + tag, g4, recv, c_idx, 128)

    def finish(tag, p4, far):
        own = lax.dynamic_index_in_dim(p4, chip, axis=0, keepdims=False)
        return sibling_merge("merge_" + tag, sum_parts("sum_" + tag, own, far, 128))

    p4_in, p4_in_sent = chip_sums("w_in", jnp.stack([_block_from_internal(g_w_all, j, n_in) for j in range(N_CHIPS)]))
    p4_out, p4_out_sent = chip_sums("w_out", g_w_out.reshape(N_CHIPS, w_out.shape[1], D_MODEL))
    d_h, far_in, far_out = matmul("d_h", dp, w_all, F32, tm=1024, tn=2048, tk=1664, trans_b=True,
                                  rides=[("scatter", p4_in_sent), ("scatter", p4_out_sent)])
    g_w_in_full = finish("w_in", p4_in, far_in)
    g_w_out_full = finish("w_out", p4_out, far_out)
    d_x, d_norm_g, d_scale, d_shift = rowwise_bwd(
        "prologue_bwd", f_prologue_res, S, ts_wide, 1, [wide(x2)], [norm_g, scale, shift],
        [wide(d_h), wide(dx_res)], [(0, "row", D_MODEL, 0, D_MODEL, F32)])

    parts = [d_shift, d_scale, d_gate, d_norm_g, d_final_g, d_conv_w.reshape(1, CONV_K * CONV_COLS), d_qn_g, d_kn_g,
             d_gdn_norm_g, d_pv, d_av, loss_acc]
    n_pay = sum(p.shape[1] for p in parts)
    pay_w = -(-n_pay // 1024) * 1024
    pay2 = jnp.concatenate(parts + [jnp.zeros((1, pay_w - n_pay), F32)], axis=1).reshape(8, pay_w // 8)
    g3 = allgather8("ag_small", pay2)
    tot = sum_devices("sum_small", g3).reshape(1, pay_w)
    dmod_all = g3.reshape(N_DEV, pay_w)[:, :3 * D_MODEL]
    o = 3 * D_MODEL
    g_b_ada = tot[:, :o]
    g_norm_g, g_final_g = tot[:, o:o + D_MODEL], tot[:, o + D_MODEL:o + 2 * D_MODEL]
    o += 2 * D_MODEL
    g_conv_full = tot[:, o:o + CONV_K * CONV_COLS].reshape(CONV_K, CONV_COLS)
    g_conv = lax.dynamic_slice_in_dim(g_conv_full, chip * conv_cols_chip, conv_cols_chip, axis=1)
    o += CONV_K * CONV_COLS
    g_qn_g, g_kn_g, g_gdn_ng = tot[:, o:o + 128], tot[:, o + 128:o + 256], tot[:, o + 256:o + 384]
    g_pv, g_av = tot[:, o + 384:o + 512], tot[:, o + 512:o + 640]
    loss = tot[0, o + 640]
    g_b_fgate, g_dt_bias, g_a_log = g_pv[:, :8], g_pv[:, 8:16], g_av[:, 8:16]

    def small_vec(vals):
        flat = [norm for norm in vals[:3]] + [vals[3].reshape(1, CONV_K * conv_cols_chip)] + list(vals[4:7]) \
            + [_pad_lanes(s) for s in vals[7:]]
        n = sum(f.shape[1] for f in flat)
        nw = -(-n // 1024) * 1024
        return jnp.concatenate(flat + [jnp.zeros((1, nw - n), F32)], axis=1).reshape(8, nw // 8)

    sw = small_vec([norm_g, b_ada, final_g[None, :], gdn_conv_w[0], fox_qn_g, fox_kn_g, gdn_norm_g, b_fgate, gdn_A_log,
                    gdn_dt_bias])
    sm = small_vec([m_norm_g, m_b_ada, m_final_g[None, :], m_gdn_conv_w[0], m_fox_qn_g, m_fox_kn_g, m_gdn_norm_g,
                    m_b_fgate, m_gdn_A_log, m_gdn_dt_bias])
    sv = small_vec([v_norm_g, v_b_ada, v_final_g[None, :], v_gdn_conv_w[0], v_fox_qn_g, v_fox_kn_g, v_gdn_norm_g,
                    v_b_fgate, v_gdn_A_log, v_gdn_dt_bias])
    sg = small_vec([g_norm_g, g_b_ada, g_final_g, g_conv, g_qn_g, g_kn_g, g_gdn_ng, g_b_fgate, g_a_log, g_dt_bias])
    small_out = adamw("adamw_small", sw, [sg], sm, sv, 8)

    def split_small(a):
        a = a.reshape(1, -1)
        out, o = [], 0
        for n, shp in ((D_MODEL, (1, D_MODEL)), (3 * D_MODEL, (1, 3 * D_MODEL)), (D_MODEL, (D_MODEL,)),
                       (CONV_K * conv_cols_chip, (1, CONV_K, conv_cols_chip)), (128, (1, 128)), (128, (1, 128)),
                       (128, (1, 128))):
            out.append(a[:, o:o + n].reshape(shp))
            o += n
        for _ in range(3):
            out.append(a[:, o:o + 8])
            o += 128
        return out

    n_ada = w_ada.shape[2]
    g_w_ada = ada_grad(c_all, lax.dynamic_slice_in_dim(dmod_all, chip * n_ada, n_ada, axis=1))
    ada_out = adamw("adamw_w_ada", w_ada[0], [g_w_ada], m_w_ada[0], v_w_ada[0], 128)

    in_out = adamw("adamw_w_in", w_in[0], [g_w_in_full], m_w_in[0], v_w_in[0], 128)
    out_out = adamw("adamw_w_out", w_out[0], [g_w_out_full], m_w_out[0], v_w_out[0], 128)

    res = [loss, d_x[None]]
    for k in range(4):
        s = split_small(small_out[k])
        big = [ada_out[k][None], in_out[k][None], out_out[k][None]]
        res += [s[0], big[0], s[1], big[1], s[7], s[4], s[5], s[3], s[8], s[9], s[6], big[2], s[2]]
    return tuple(res)
```

```python
import functools

import jax
import jax.numpy as jnp
from jax import lax
from jax.experimental import pallas as pl
from jax.experimental.pallas import tpu as pltpu

F32 = jnp.float32
MXU_DTYPE = jnp.bfloat16
HIGHEST = lax.Precision.HIGHEST
SOLVE_PRECISION = lax.Precision.HIGH
MESH_ID = pl.DeviceIdType.MESH

D_MODEL = 2048
N_HEADS = 8
D_HEAD = 128
WIDTH = N_HEADS * D_HEAD
CHUNK = 64
CONV_K = 4
EPS = 1e-6
NEG = -1e30
ATT_SCALE = D_HEAD ** -0.5
N_SMALL = 3 * N_HEADS
P_COLS = 8 * WIDTH + 128
CB_FQ, CB_FK, CB_FV, CB_FZ, CB_GQ, CB_GK, CB_GV, CB_GZ, CB_SMALL = 0, 8, 16, 24, 32, 40, 48, 56, 64
IN_WIDTH = 8 * WIDTH + N_SMALL
N_CHIPS = 4
N_DEV = 8

ADAM_LR, ADAM_B1, ADAM_B2, ADAM_EPS, ADAM_WD, ADAM_STEP = 0.001, 0.9, 0.999, 1e-08, 0.01, 10


def _dotg(a, b, dims):
    return lax.dot_general(a.astype(MXU_DTYPE), b.astype(MXU_DTYPE), (dims, ((), ())), preferred_element_type=F32)


@jax.custom_vjp
def mm_nn(a, b):
    return _dotg(a, b, ((1,), (0,)))


@jax.custom_vjp
def mm_nt(a, b):
    return _dotg(a, b, ((1,), (1,)))


@jax.custom_vjp
def mm_tn(a, b):
    return _dotg(a, b, ((0,), (0,)))


mm_nn.defvjp(lambda a, b: (mm_nn(a, b), (a, b)), lambda r, g: (mm_nt(g, r[1]), mm_tn(r[0], g)))
mm_nt.defvjp(lambda a, b: (mm_nt(a, b), (a, b)), lambda r, g: (mm_nn(g, r[1]), mm_tn(g, r[0])))
mm_tn.defvjp(lambda a, b: (mm_tn(a, b), (a, b)), lambda r, g: (mm_nt(r[1], g), mm_nn(r[0], g)))


def hdot(a, b):
    return lax.dot_general(a, b, (((1,), (0,)), ((), ())), precision=HIGHEST, preferred_element_type=F32)


def _sigmoid(x):
    return 0.5 * (jnp.tanh(0.5 * x) + 1.0)


def _silu(x):
    return x * _sigmoid(x)


def _softplus(x):
    return jnp.maximum(x, 0.0) + jnp.log(1.0 + jnp.exp(-jnp.abs(x)))


def _log_sigmoid(x):
    return jnp.minimum(x, 0.0) - jnp.log(1.0 + jnp.exp(-jnp.abs(x)))


def _rms(x, g):
    return x * lax.rsqrt(jnp.mean(x * x, axis=-1, keepdims=True) + EPS) * g


def _l2n(x):
    return x * lax.rsqrt(jnp.sum(x * x, axis=-1, keepdims=True) + EPS)


def _tile(n, pref):
    t = min(n, pref)
    assert n % t == 0, (n, pref)
    return t


def _bspec(kind, w, col0, ts):
    if kind == "row":
        return pl.BlockSpec((ts, w), lambda i, h: (i, col0))
    if kind == "head":
        return pl.BlockSpec((ts, w), lambda i, h: (i, col0 + h))
    assert kind == "head3"
    return pl.BlockSpec((1, ts, w), lambda i, h: (h, i, 0))


def _ld(ref, kind):
    return ref[0] if kind == "head3" else ref[...]


def _st(ref, kind, val):
    if kind == "head3":
        ref[0] = val.astype(ref.dtype)
    else:
        ref[...] = val.astype(ref.dtype)


def _full_spec(a):
    nd = a.ndim
    return pl.BlockSpec(a.shape, lambda i, h: (0,) * nd)


def _out_struct(kind, w, S, width_total, dtype, nh):
    if kind == "head3":
        return jax.ShapeDtypeStruct((nh, S, w), dtype)
    return jax.ShapeDtypeStruct((S, width_total), dtype)


def rowwise_fwd(name, f, S, ts, nh, ins, params, outs):
    n_in, n_p = len(ins), len(params)
    into = {j: o[5] for j, o in enumerate(outs) if len(o) > 5}
    outs = [o[:5] for o in outs]
    n_al = len(into)

    def body(*refs):
        vals = [_ld(r, s[1]).astype(F32) for r, s in zip(refs[:n_in], ins)]
        pv = [r[...] for r in refs[n_in:n_in + n_p]]
        res = f(*vals, *pv)
        for r, s, o in zip(refs[n_in + n_p + n_al:], outs, res):
            _st(r, s[0], o)

    return pl.pallas_call(
        body, name=name, grid=(S // ts, nh),
        in_specs=[_bspec(k, w, c0, ts) for (_, k, w, c0) in ins] + [_full_spec(p) for p in params]
        + [pl.BlockSpec(memory_space=pl.ANY)] * n_al,
        out_specs=[_bspec(k, w, c0, ts) for (k, w, c0, _, _) in outs],
        out_shape=[jax.ShapeDtypeStruct(into[j].shape, into[j].dtype) if j in into else _out_struct(k, w, S, tw, dt, nh)
                   for j, (k, w, c0, tw, dt) in enumerate(outs)],
        input_output_aliases={n_in + n_p + n: j for n, j in enumerate(sorted(into))},
    )(*[a for (a, _, _, _) in ins], *params, *[into[j] for j in sorted(into)])


def rowwise_bwd(name, f, S, ts, nh, ins, params, cts, row_grads, extra=None, extra_outs=(), primal_sum=False):
    n_in, n_p, n_ct = len(ins), len(params), len(cts)
    n_rg, n_ex = len(row_grads), len(extra_outs)
    into = {j: rg[6] for j, rg in enumerate(row_grads) if len(rg) > 6}
    row_grads = [rg[:6] for rg in row_grads]
    n_al = len(into)

    def body(*refs):
        first = (pl.program_id(0) == 0) & (pl.program_id(1) == 0)
        in_refs = refs[:n_in]
        p_refs = refs[n_in:n_in + n_p]
        ct_refs = refs[n_in + n_p:n_in + n_p + n_ct]
        o = n_in + n_p + n_ct + n_al
        rg_refs = refs[o:o + n_rg]
        ex_refs = refs[o + n_rg:o + n_rg + n_ex]
        acc_refs = refs[o + n_rg + n_ex:]
        vals = [_ld(r, s[1]).astype(F32) for r, s in zip(in_refs, ins)]
        pv = [r[...] for r in p_refs]
        res, vjp = jax.vjp(f, *vals, *pv)
        if primal_sum:
            cv = [jnp.ones_like(res[0])] + [_ld(r, s[1]).astype(F32) for r, s in zip(ct_refs, cts)]
        else:
            cv = [_ld(r, s[1]).astype(F32) for r, s in zip(ct_refs, cts)]
        grads = vjp(tuple(cv))
        for r, s in zip(rg_refs, row_grads):
            _st(r, s[1], grads[s[0]])
        if extra is not None:
            for r, s, e in zip(ex_refs, extra_outs, extra(vals, cv, grads)):
                _st(r, s[0], e)

        @pl.when(first)
        def _():
            for r in acc_refs:
                r[...] = jnp.zeros_like(r)

        for r, g in zip(acc_refs[:n_p], grads[n_in:]):
            r[...] += g
        if primal_sum:
            acc_refs[n_p][...] += jnp.sum(res[0])

    acc_shapes = [jax.ShapeDtypeStruct(p.shape, F32) for p in params]
    acc_specs = [_full_spec(p) for p in params]
    if primal_sum:
        acc_shapes.append(jax.ShapeDtypeStruct((1, 128), F32))
        acc_specs.append(pl.BlockSpec((1, 128), lambda i, h: (0, 0)))
    rg_shapes = [jax.ShapeDtypeStruct(into[j].shape, into[j].dtype) if j in into else _out_struct(k, w, S, tw, dt, nh)
                 for j, (_, k, w, c0, tw, dt) in enumerate(row_grads)]
    first_alias = n_in + n_p + n_ct
    return pl.pallas_call(
        body, name=name, grid=(S // ts, nh),
        in_specs=[_bspec(k, w, c0, ts) for (_, k, w, c0) in ins] + [_full_spec(p) for p in params]
        + [_bspec(k, w, c0, ts) for (_, k, w, c0) in cts] + [pl.BlockSpec(memory_space=pl.ANY)] * n_al,
        out_specs=[_bspec(k, w, c0, ts) for (_, k, w, c0, _, _) in row_grads]
        + [_bspec(k, w, c0, ts) for (k, w, c0, _, _) in extra_outs] + acc_specs,
        out_shape=rg_shapes + [_out_struct(k, w, S, tw, dt, nh) for (k, w, c0, tw, dt) in extra_outs] + acc_shapes,
        input_output_aliases={first_alias + n: j for n, j in enumerate(sorted(into))},
    )(*[a for (a, _, _, _) in ins], *params, *[a for (a, _, _, _) in cts], *[into[j] for j in sorted(into)])


def f_prologue(x, ng, sc, sh):
    return (_rms(x, ng) * (1.0 + sc) + sh,)


def f_prologue_res(x, ng, sc, sh):
    return (_rms(x, ng) * (1.0 + sc) + sh, x)


def f_loss(x, mo, tgt, gate, fg):
    y = _rms(x + gate * mo, fg)
    return (0.5 * jnp.mean(jnp.square(y - tgt), axis=-1, keepdims=True),)


def f_foxpre(fq, fk, fv, qg, kg):
    return (_rms(fq, qg) * ATT_SCALE, _rms(fk, kg), fv)


def f_postf(o, z):
    return (o * _silu(z),)


def f_postg(o, z, ng):
    return (_rms(o, ng) * _silu(z),)


def fox_norm_bwd(p_all, d_qn, d_kn, qg, kg, dp, S, ts):
    def body(p_ref, dq_ref, dk_ref, qg_ref, kg_ref, dp_in, o_ref, dqg_ref, dkg_ref):
        @pl.when(pl.program_id(0) == 0)
        def _():
            dqg_ref[...] = jnp.zeros_like(dqg_ref)
            dkg_ref[...] = jnp.zeros_like(dkg_ref)

        for ct_ref, g_ref, dg_ref, col0 in ((dq_ref, qg_ref, dqg_ref, 0), (dk_ref, kg_ref, dkg_ref, WIDTH)):
            dg = jnp.zeros((1, D_HEAD), F32)
            for h in range(N_HEADS):
                _, vjp = jax.vjp(_rms, p_ref[:, col0 + h * D_HEAD:col0 + (h + 1) * D_HEAD], g_ref[...])
                dx, dg_h = vjp(ct_ref[:, h * D_HEAD:(h + 1) * D_HEAD])
                o_ref[:, col0 + h * D_HEAD:col0 + (h + 1) * D_HEAD] = dx.astype(o_ref.dtype)
                dg = dg + dg_h
            dg_ref[...] += dg

    gain = pl.BlockSpec((1, D_HEAD), lambda i: (0, 0))
    both = pl.BlockSpec((ts, 2 * WIDTH), lambda i: (i, 0))
    one = pl.BlockSpec((ts, WIDTH), lambda i: (i, 0))
    return pl.pallas_call(
        body, name="fox_norm_bwd", grid=(S // ts,),
        in_specs=[both, one, one, gain, gain, pl.BlockSpec(memory_space=pl.ANY)],
        out_specs=[both, gain, gain],
        out_shape=[jax.ShapeDtypeStruct(dp.shape, dp.dtype), jax.ShapeDtypeStruct((1, D_HEAD), F32),
                   jax.ShapeDtypeStruct((1, D_HEAD), F32)],
        input_output_aliases={5: 0},
    )(p_all, d_qn, d_kn, qg, kg, dp)


def _chip_exchange(kind, x_ref, out_ref, send_sems, recv_sems):
    x, y, c = lax.axis_index("x"), lax.axis_index("y"), lax.axis_index("c")
    sends, arrivals = [], []
    for r in range(1, N_CHIPS):
        px, py = _flip(x, r & 2), _flip(y, r & 1)
        if kind == "scatter":
            src, dst, landed = x_ref.at[2 * px + py], out_ref.at[r - 1], out_ref.at[r - 1]
        else:
            src, dst, landed = x_ref, out_ref.at[2 * x + y], out_ref.at[2 * px + py]
        mk = functools.partial(pltpu.make_async_remote_copy, send_sem=send_sems.at[r - 1], recv_sem=recv_sems.at[r - 1],
                               device_id=(px, py, c), device_id_type=MESH_ID)
        sends.append(mk(src_ref=src, dst_ref=dst))
        arrivals.append(mk(src_ref=src, dst_ref=landed))
    return sends, arrivals


def _exchange_shape(kind, x):
    return jax.ShapeDtypeStruct(((N_CHIPS - 1,) + x.shape[1:]) if kind == "scatter" else ((N_CHIPS,) + x.shape), x.dtype)


def matmul(name, a, b, out_dtype, tm=512, tn=512, tk=2048, trans_a=False, trans_b=False, rides=()):
    M, K = a.shape[::-1] if trans_a else a.shape
    N, K2 = b.shape if trans_b else b.shape[::-1]
    assert K == K2
    tm, tn, tk = _tile(M, tm), _tile(N, tn), _tile(K, tk)
    ni, nj, nk = M // tm, N // tn, K // tk
    a_spec = pl.BlockSpec((tk, tm), lambda i, j, k: (k, i)) if trans_a else pl.BlockSpec((tm, tk), lambda i, j, k: (i, k))
    b_spec = pl.BlockSpec((tn, tk), lambda i, j, k: (j, k)) if trans_b else pl.BlockSpec((tk, tn), lambda i, j, k: (k, j))
    n_ride = len(rides)

    def body(a_ref, b_ref, *rest):
        ride_in, o_ref, ride_out = rest[:n_ride], rest[n_ride], rest[n_ride + 1:2 * n_ride + 1]
        scr = rest[2 * n_ride + 1:]
        i, j, k = pl.program_id(0), pl.program_id(1), pl.program_id(2)
        exchanges = [_chip_exchange(kind, ride_in[n], ride_out[n], scr[2 * n], scr[2 * n + 1])
                     for n, (kind, _) in enumerate(rides)]

        @pl.when((i == 0) & (j == 0) & (k == 0))
        def _():
            for sends, _ in exchanges:
                for cp in sends:
                    cp.start()

        part = lax.dot_general(a_ref[...], b_ref[...], (((0 if trans_a else 1,), (1 if trans_b else 0,)), ((), ())),
                               preferred_element_type=F32)
        if nk == 1:
            o_ref[...] = part.astype(o_ref.dtype)
        else:
            acc = scr[2 * n_ride]

            @pl.when(k == 0)
            def _():
                acc[...] = part

            @pl.when(k > 0)
            def _():
                acc[...] += part

            @pl.when(k == nk - 1)
            def _():
                o_ref[...] = acc[...].astype(o_ref.dtype)

        @pl.when((i == ni - 1) & (j == nj - 1) & (k == nk - 1))
        def _():
            for sends, arrivals in exchanges:
                for cp in arrivals:
                    cp.wait_recv()
                for cp in sends:
                    cp.wait_send()

    sems = [pltpu.SemaphoreType.DMA((N_CHIPS - 1,))] * (2 * n_ride)
    any_spec = pl.BlockSpec(memory_space=pl.ANY)
    res = pl.pallas_call(
        body, name=name, grid=(ni, nj, nk),
        in_specs=[a_spec, b_spec] + [any_spec] * n_ride,
        out_specs=[pl.BlockSpec((tm, tn), lambda i, j, k: (i, j))] + [any_spec] * n_ride,
        out_shape=[jax.ShapeDtypeStruct((M, N), out_dtype)] + [_exchange_shape(kind, x) for kind, x in rides],
        scratch_shapes=sems + ([] if nk == 1 else [pltpu.VMEM((tm, tn), F32)]),
        compiler_params=pltpu.CompilerParams(
            dimension_semantics=("arbitrary",) * 3 if rides else ("parallel", "parallel", "arbitrary")),
    )(a, b, *[x for _, x in rides])
    return tuple(res) if rides else res[0]


def _small_f(z, pv, av):
    lane = lax.broadcasted_iota(jnp.int32, z.shape, 1)
    zz = z + pv
    logf = _log_sigmoid(zz)
    gval = -jnp.exp(av) * _softplus(zz)
    beta = _sigmoid(zz)
    return jnp.where(lane < 8, logf, jnp.where(lane < 16, gval, jnp.where(lane < 24, beta, 0.0)))


def _tri(ts, upper):
    r = lax.broadcasted_iota(jnp.int32, (ts, ts), 0)
    c = lax.broadcasted_iota(jnp.int32, (ts, ts), 1)
    keep = (r <= c) if upper else (r >= c)
    same_chunk = (r // CHUNK) == (c // CHUNK)
    return keep.astype(F32), (keep & same_chunk).astype(F32)


def smalls_fwd(p_all, pv, av, S, ts):
    nt = S // ts

    def body(z_ref, pv_ref, av_ref, o_ref, carry):
        i = pl.program_id(0)

        @pl.when(i == 0)
        def _():
            carry[...] = jnp.zeros_like(carry)

        e = _small_f(z_ref[...], pv_ref[...], av_ref[...])
        lane = lax.broadcasted_iota(jnp.int32, e.shape, 1)
        l_all, l_chunk = _tri(ts, upper=False)
        cum_all = hdot(l_all, e) + carry[...]
        cum_chunk = hdot(l_chunk, e)
        carry[...] = cum_all[ts - 1:ts, :]
        o_ref[...] = jnp.where(lane < 8, cum_all, jnp.where(lane < 16, cum_chunk, e))

    return pl.pallas_call(
        body, name="smalls_fwd", grid=(nt,),
        in_specs=[pl.BlockSpec((ts, 128), lambda i: (i, CB_SMALL)), pl.BlockSpec((1, 128), lambda i: (0, 0)),
                  pl.BlockSpec((1, 128), lambda i: (0, 0))],
        out_specs=pl.BlockSpec((ts, 128), lambda i: (i, 0)),
        out_shape=jax.ShapeDtypeStruct((S, 128), F32),
        scratch_shapes=[pltpu.VMEM((1, 128), F32)],
    )(p_all, pv, av)


def smalls_bwd(p_all, pv, av, dsm, dp, S, ts):
    nt = S // ts

    def body(z_ref, pv_ref, av_ref, d_ref, dp_in, dz_ref, dpv_ref, dav_ref, carry):
        i = pl.program_id(0)

        @pl.when(i == 0)
        def _():
            carry[...] = jnp.zeros_like(carry)
            dpv_ref[...] = jnp.zeros_like(dpv_ref)
            dav_ref[...] = jnp.zeros_like(dav_ref)

        d = d_ref[...]
        lane = lax.broadcasted_iota(jnp.int32, d.shape, 1)
        u_all, u_chunk = _tri(ts, upper=True)
        rev_all = hdot(u_all, d) + carry[...]
        rev_chunk = hdot(u_chunk, d)
        carry[...] = rev_all[0:1, :]
        de = jnp.where(lane < 8, rev_all, jnp.where(lane < 16, rev_chunk, d))
        _, vjp = jax.vjp(_small_f, z_ref[...], pv_ref[...], av_ref[...])
        dz, dpv, dav = vjp(de)
        dz_ref[...] = dz.astype(dz_ref.dtype)
        dpv_ref[...] += dpv
        dav_ref[...] += dav

    rev = lambda i: (nt - 1 - i, 0)
    small_cols = pl.BlockSpec((ts, 128), lambda i: (nt - 1 - i, CB_SMALL))
    return pl.pallas_call(
        body, name="smalls_bwd", grid=(nt,),
        in_specs=[small_cols, pl.BlockSpec((1, 128), lambda i: (0, 0)), pl.BlockSpec((1, 128), lambda i: (0, 0)),
                  pl.BlockSpec((ts, 128), rev), pl.BlockSpec(memory_space=pl.ANY)],
        out_specs=[small_cols, pl.BlockSpec((1, 128), lambda i: (0, 0)), pl.BlockSpec((1, 128), lambda i: (0, 0))],
        out_shape=[jax.ShapeDtypeStruct(dp.shape, dp.dtype), jax.ShapeDtypeStruct((1, 128), F32),
                   jax.ShapeDtypeStruct((1, 128), F32)],
        scratch_shapes=[pltpu.VMEM((1, 128), F32)],
        input_output_aliases={4: 0},
    )(p_all, pv, av, dsm, dp)


def _col_to_row(col):
    return jnp.transpose(jnp.broadcast_to(col, (col.shape[0], 128)))[0:1, :]


def _row_to_col(row):
    return jnp.transpose(jnp.broadcast_to(row, (128, row.shape[1])))[:, 0:1]


def _causal_mask(s, keys_first):
    r = lax.broadcasted_iota(jnp.int32, s.shape, 0)
    c = lax.broadcasted_iota(jnp.int32, s.shape, 1)
    return jnp.where((r <= c) if keys_first else (c <= r), s, NEG)


def flash_fwd(qs, kn, vb, frow, S, tq):
    nq = S // tq
    tk = tq

    def body(q_ref, k_ref, v_ref, fr_ref, o_ref, lse_ref):
        i = pl.program_id(1)
        q = q_ref[...]
        f_base = fr_ref[0, i][:, 0:1]

        def tile(j, carry, diagonal):
            m, l, acc = carry
            off = pl.multiple_of(j * tk, tk)
            k = k_ref[pl.ds(off, tk), :]
            v = v_ref[pl.ds(off, tk), :]
            s = mm_nt(q, k) - (fr_ref[0, j] - f_base)
            if diagonal:
                s = _causal_mask(s, keys_first=False)
            m_new = jnp.maximum(m, jnp.max(s, axis=-1, keepdims=True))
            a = jnp.exp(m - m_new)
            p = jnp.exp(s - m_new)
            l = a * l + jnp.sum(p, axis=-1, keepdims=True)
            acc = a * acc + mm_nn(p, v)
            return m_new, l, acc

        init = (jnp.full((tq, 1), NEG, F32), jnp.zeros((tq, 1), F32), jnp.zeros((tq, D_HEAD), F32))
        carry = lax.fori_loop(0, i, lambda j, c: tile(j, c, False), init)
        m, l, acc = tile(i, carry, True)
        o_ref[...] = acc / l
        lse_ref[0, 0] = _col_to_row(m + jnp.log(l))

    return pl.pallas_call(
        body, name="flash_fwd", grid=(N_HEADS, nq),
        in_specs=[pl.BlockSpec((tq, D_HEAD), lambda h, i: (i, h)), pl.BlockSpec((S, D_HEAD), lambda h, i: (0, h)),
                  pl.BlockSpec((S, D_HEAD), lambda h, i: (0, h)),
                  pl.BlockSpec((1, nq, 1, tk), lambda h, i: (h, 0, 0, 0))],
        out_specs=[pl.BlockSpec((tq, D_HEAD), lambda h, i: (i, h)),
                   pl.BlockSpec((1, 1, 1, tq), lambda h, i: (h, i, 0, 0))],
        out_shape=[jax.ShapeDtypeStruct((S, WIDTH), F32), jax.ShapeDtypeStruct((N_HEADS, nq, 1, tq), F32)],
    )(qs, kn, vb, frow)


def flash_delta(do, o, S, tq):
    nq = S // tq
    per = _tile(nq, 4)

    def body(do_ref, o_ref, dl_ref):
        for n in range(per):
            rows = pl.ds(n * tq, tq)
            dl_ref[0, n] = _col_to_row(jnp.sum(do_ref[rows, :].astype(F32) * o_ref[rows, :], axis=-1, keepdims=True))

    blk = pl.BlockSpec((per * tq, D_HEAD), lambda h, i: (i, h))
    return pl.pallas_call(
        body, name="flash_delta", grid=(N_HEADS, nq // per), in_specs=[blk, blk],
        out_specs=pl.BlockSpec((1, per, 1, tq), lambda h, i: (h, i, 0, 0)),
        out_shape=jax.ShapeDtypeStruct((N_HEADS, nq, 1, tq), F32),
    )(do, o)


def flash_bwd(qs, kn, vb, do, frow, lse_row, delta_row, dp, S, tq):
    nq = S // tq
    tk = tq

    def body(q_ref, k_ref, v_ref, do_ref, fr_ref, lse_ref, dl_ref, dp_in, dq_ref, dk_ref, dv_ref, dfq_ref, dfk_ref):
        j = pl.program_id(1)
        k = k_ref[...]
        v = v_ref[...]
        fk = _row_to_col(fr_ref[0, j])

        @pl.when(j == 0)
        def _():
            dq_ref[...] = jnp.zeros_like(dq_ref)
            dfq_ref[...] = jnp.zeros_like(dfq_ref)

        def tile(i, carry, diagonal):
            dk, dv, dfk = carry
            off = pl.multiple_of(i * tq, tq)
            q = q_ref[pl.ds(off, tq), :]
            do_ = do_ref[pl.ds(off, tq), :]
            st = mm_nt(k, q) - (fk - fr_ref[0, i][:, 0:1])
            if diagonal:
                st = _causal_mask(st, keys_first=True)
            pt = jnp.exp(st - lse_ref[0, i])
            dst = pt * (mm_nt(v, do_) - dl_ref[0, i])
            dq_ref[pl.ds(off, tq), :] += mm_tn(dst, k) * ATT_SCALE
            dfq_ref[0, i] += jnp.sum(dst, axis=0, keepdims=True)
            return dk + mm_nn(dst, q), dv + mm_nn(pt, do_), dfk - jnp.sum(dst, axis=-1, keepdims=True)

        z = jnp.zeros((tk, D_HEAD), F32)
        carry = tile(j, (z, z, jnp.zeros((tk, 1), F32)), True)
        dk, dv, dfk = lax.fori_loop(j + 1, nq, lambda i, c: tile(i, c, False), carry)
        dk_ref[...] = dk
        dv_ref[...] = dv.astype(dv_ref.dtype)
        dfk_ref[0, 0] = _col_to_row(dfk)

    blk = pl.BlockSpec((tk, D_HEAD), lambda h, j: (j, h))
    full = pl.BlockSpec((S, D_HEAD), lambda h, j: (0, h))
    rows = pl.BlockSpec((1, nq, 1, tq), lambda h, j: (h, 0, 0, 0))
    stat = jax.ShapeDtypeStruct((N_HEADS, nq, 1, tq), F32)
    return pl.pallas_call(
        body, name="flash_bwd", grid=(N_HEADS, nq),
        in_specs=[full, blk, blk, full, rows, rows, rows, pl.BlockSpec(memory_space=pl.ANY)],
        out_specs=[full, blk, pl.BlockSpec((tk, D_HEAD), lambda h, j: (j, CB_FV + h)), rows,
                   pl.BlockSpec((1, 1, 1, tk), lambda h, j: (h, j, 0, 0))],
        out_shape=[jax.ShapeDtypeStruct((S, WIDTH), F32), jax.ShapeDtypeStruct((S, WIDTH), F32),
                   jax.ShapeDtypeStruct(dp.shape, dp.dtype), stat, stat],
        input_output_aliases={7: 2},
    )(qs, kn, vb, do, frow, lse_row, delta_row, dp)


CONV_COLS = 3 * WIDTH
CONV_TC = 512


def _gdn_act(y, group):
    s = _silu(y)
    if group == 2:
        return s
    return _l2n(s) * ATT_SCALE if group == 0 else _l2n(s)


def _gdn_act_bwd(y, d, group):
    _, vjp = jax.vjp(functools.partial(_gdn_act, group=group), y)
    return vjp(d)[0]


def conv_act_fwd(p_all, conv_w, S, ts):
    nt, tc = S // ts, CONV_TC
    cb0 = CB_GQ * 128 // tc
    r8 = ts // 8
    tiles_per_group = WIDTH // tc

    def body(x_ref, halo_ref, w_ref, y_ref, a_ref, scr):
        c, i = pl.program_id(0), pl.program_id(1)
        scr[0:8, :] = jnp.where(i > 0, halo_ref[...], 0.0)
        scr[8:8 + ts, :] = x_ref[...]
        w = w_ref[...]
        y = w[3:4, :] * x_ref[...]
        for j in range(CONV_K - 1):
            y = y + w[j:j + 1, :] * scr[5 + j:5 + j + ts, :]
        y_ref[...] = y
        for group in range(3):
            @pl.when(c // tiles_per_group == group)
            def _(group=group):
                for h in range(tc // D_HEAD):
                    sl = slice(h * D_HEAD, (h + 1) * D_HEAD)
                    a_ref[:, sl] = _gdn_act(y_ref[:, sl], group)

    out = pl.BlockSpec((ts, tc), lambda c, i: (i, c))
    return pl.pallas_call(
        body, name="conv_act_fwd", grid=(CONV_COLS // tc, nt),
        in_specs=[pl.BlockSpec((ts, tc), lambda c, i: (i, cb0 + c)),
                  pl.BlockSpec((8, tc), lambda c, i: (jnp.maximum(i * r8 - 1, 0), cb0 + c)),
                  pl.BlockSpec((CONV_K, tc), lambda c, i: (0, c))],
        out_specs=[out, out],
        out_shape=[jax.ShapeDtypeStruct((S, CONV_COLS), F32)] * 2,
        scratch_shapes=[pltpu.VMEM((ts + 8, tc), F32)],
    )(p_all, p_all, conv_w)


def conv_act_bwd(p_all, y, dqkv, conv_w, dp, S, ts):
    nt, tc = S // ts, CONV_TC
    cb0 = CB_GQ * 128 // tc
    r8 = ts // 8
    tiles_per_group = WIDTH // tc

    def body(x_ref, xh_ref, y_ref, yh_ref, d_ref, dh_ref, w_ref, dp_in, dx_ref, dw_ref, xs, ds):
        c, i = pl.program_id(0), pl.program_id(1)
        xs[0:8, :] = jnp.where(i > 0, xh_ref[...], 0.0)
        xs[8:8 + ts, :] = x_ref[...]
        for group in range(3):
            @pl.when(c // tiles_per_group == group)
            def _(group=group):
                for h in range(tc // D_HEAD):
                    sl = slice(h * D_HEAD, (h + 1) * D_HEAD)
                    ds[0:ts, sl] = _gdn_act_bwd(y_ref[:, sl], d_ref[:, sl], group)
                    ds[ts:ts + 8, sl] = jnp.where(i < nt - 1, _gdn_act_bwd(yh_ref[:, sl], dh_ref[:, sl], group), 0.0)
        w = w_ref[...]
        d = ds[0:ts, :]
        dx = w[3:4, :] * d
        for j in range(CONV_K - 1):
            dx = dx + w[j:j + 1, :] * ds[3 - j:3 - j + ts, :]
        dx_ref[...] = dx.astype(dx_ref.dtype)

        @pl.when(i == 0)
        def _():
            dw_ref[...] = jnp.zeros_like(dw_ref)

        rows = [jnp.sum(d * xs[5 + j:5 + j + ts, :], axis=0, keepdims=True) for j in range(CONV_K - 1)]
        rows.append(jnp.sum(d * x_ref[...], axis=0, keepdims=True))
        dw_ref[...] += jnp.concatenate(rows, axis=0)

    tile = pl.BlockSpec((ts, tc), lambda c, i: (i, c))
    next8 = pl.BlockSpec((8, tc), lambda c, i: (jnp.minimum((i + 1) * r8, nt * r8 - 1), c))
    return pl.pallas_call(
        body, name="conv_act_bwd", grid=(CONV_COLS // tc, nt),
        in_specs=[pl.BlockSpec((ts, tc), lambda c, i: (i, cb0 + c)),
                  pl.BlockSpec((8, tc), lambda c, i: (jnp.maximum(i * r8 - 1, 0), cb0 + c)),
                  tile, next8, tile, next8,
                  pl.BlockSpec((CONV_K, tc), lambda c, i: (0, c)), pl.BlockSpec(memory_space=pl.ANY)],
        out_specs=[pl.BlockSpec((ts, tc), lambda c, i: (i, cb0 + c)), pl.BlockSpec((CONV_K, tc), lambda c, i: (0, c))],
        out_shape=[jax.ShapeDtypeStruct(dp.shape, dp.dtype), jax.ShapeDtypeStruct((CONV_K, CONV_COLS), F32)],
        scratch_shapes=[pltpu.VMEM((ts + 8, tc), F32), pltpu.VMEM((ts + 8, tc), F32)],
        input_output_aliases={7: 0},
    )(p_all, p_all, y, y, dqkv, dqkv, conv_w, dp)


def _bdot(a, b, ca, cb):
    return lax.dot_general(a.astype(MXU_DTYPE), b.astype(MXU_DTYPE), (((ca,), (cb,)), ((0,), (0,))),
                           preferred_element_type=F32)


@jax.custom_vjp
def bmm_nn(a, b):
    return _bdot(a, b, 2, 1)


@jax.custom_vjp
def bmm_nt(a, b):
    return _bdot(a, b, 2, 2)


@jax.custom_vjp
def bmm_tn(a, b):
    return _bdot(a, b, 1, 1)


bmm_nn.defvjp(lambda a, b: (bmm_nn(a, b), (a, b)), lambda r, g: (bmm_nt(g, r[1]), bmm_tn(r[0], g)))
bmm_nt.defvjp(lambda a, b: (bmm_nt(a, b), (a, b)), lambda r, g: (bmm_nn(g, r[1]), bmm_tn(g, r[0])))
bmm_tn.defvjp(lambda a, b: (bmm_tn(a, b), (a, b)), lambda r, g: (bmm_nt(r[1], g), bmm_nn(r[0], g)))


def bdot_hi(a, b, ca=2, cb=1):
    return lax.dot_general(a, b, (((ca,), (cb,)), ((0,), (0,))), precision=SOLVE_PRECISION,
                           preferred_element_type=F32)


def _tri_inv_product(m):
    ii = lax.broadcasted_iota(jnp.int32, m.shape, 1)
    jj = lax.broadcasted_iota(jnp.int32, m.shape, 2)
    t = (ii == jj).astype(F32) - m
    pw = bdot_hi(m, m)
    for _ in range(4):
        t = t + bdot_hi(t, pw)
        pw = bdot_hi(pw, pw)
    return t + bdot_hi(t, pw)


def _tri_inv_bwd(t, g):
    return -bdot_hi(bdot_hi(t, g, 1, 1), t, 2, 2)


@jax.custom_vjp
def tri_inv(m):
    return _tri_inv_product(m)


@jax.custom_vjp
def tri_inv_known(m, t):
    return t


tri_inv.defvjp(lambda m: (lambda t: (t, t))(_tri_inv_product(m)), lambda t, g: (_tri_inv_bwd(t, g),))
tri_inv_known.defvjp(lambda m, t: (t, t), lambda t, g: (_tri_inv_bwd(t, g), jnp.zeros_like(t)))


def chunk_fn(q, k, v, gc_col, gc_row, beta, state, t_known=None):
    shape = (N_HEADS, CHUNK, CHUNK)
    ii = lax.broadcasted_iota(jnp.int32, shape, 1)
    jj = lax.broadcasted_iota(jnp.int32, shape, 2)
    lower = ii >= jj
    strict = ii > jj
    decay = jnp.exp(jnp.where(lower, gc_col - gc_row, NEG))
    kb = k * beta
    vb = v * beta
    m = jnp.where(strict, bmm_nt(kb, k) * decay, 0.0)
    t = tri_inv(m) if t_known is None else tri_inv_known(m, t_known)
    u = bdot_hi(t, vb)
    w = bdot_hi(t, kb * jnp.exp(gc_col))
    attn = jnp.where(lower, bmm_nt(q, k) * decay, 0.0)
    v_new = u - bmm_nn(w, state)
    o = bmm_nn(q * jnp.exp(gc_col), state) + bmm_nn(attn, v_new)
    g_last = gc_col[:, CHUNK - 1:CHUNK, :]
    k_dec = k * jnp.exp(g_last - gc_col)
    new_state = state * jnp.exp(g_last) + bmm_tn(k_dec, v_new)
    return (o, new_state, t) if t_known is None else (o, new_state)


GDN_CHUNKS_PER_STEP = 4


def _heads(ref, rows):
    return jnp.stack([ref[rows, h * D_HEAD:(h + 1) * D_HEAD] for h in range(N_HEADS)], axis=0)


def _head_cols(ref, rows, lane0):
    return jnp.stack([ref[rows, lane0 + h:lane0 + h + 1] for h in range(N_HEADS)], axis=0)


def gdn_fwd(qkv, smalls, gc_row, S):
    n = S // CHUNK
    per = _tile(n, GDN_CHUNKS_PER_STEP)
    rows_per = per * CHUNK

    def body(q_ref, k_ref, v_ref, sm_ref, gr_ref, o_ref, st_ref, t_ref, state):
        @pl.when(pl.program_id(0) == 0)
        def _():
            state[...] = jnp.zeros_like(state)

        s = state[...]
        for b in range(per):
            rows = pl.ds(b * CHUNK, CHUNK)
            st_ref[b] = s
            o, s, t = chunk_fn(_heads(q_ref, rows), _heads(k_ref, rows), _heads(v_ref, rows),
                               _head_cols(sm_ref, rows, 8), gr_ref[b][:, None, :], _head_cols(sm_ref, rows, 16), s)
            for h in range(N_HEADS):
                o_ref[rows, h * D_HEAD:(h + 1) * D_HEAD] = o[h]
            t_ref[b] = t
        state[...] = s

    return pl.pallas_call(
        body, name="gdn_chunk_fwd", grid=(n // per,),
        in_specs=[pl.BlockSpec((rows_per, WIDTH), lambda i, g=g: (i, g)) for g in range(3)]
        + [pl.BlockSpec((rows_per, 128), lambda i: (i, 0)), pl.BlockSpec((per, N_HEADS, CHUNK), lambda i: (i, 0, 0))],
        out_specs=[pl.BlockSpec((rows_per, WIDTH), lambda i: (i, 0)),
                   pl.BlockSpec((per, N_HEADS, D_HEAD, D_HEAD), lambda i: (i, 0, 0, 0)),
                   pl.BlockSpec((per, N_HEADS, CHUNK, CHUNK), lambda i: (i, 0, 0, 0))],
        out_shape=[jax.ShapeDtypeStruct((S, WIDTH), F32), jax.ShapeDtypeStruct((n, N_HEADS, D_HEAD, D_HEAD), F32),
                   jax.ShapeDtypeStruct((n, N_HEADS, CHUNK, CHUNK), F32)],
        scratch_shapes=[pltpu.VMEM((N_HEADS, D_HEAD, D_HEAD), F32)],
    )(qkv, qkv, qkv, smalls, gc_row)


def gdn_bwd(qkv, smalls, gc_row, states, tinv, do, S):
    n = S // CHUNK
    per = _tile(n, GDN_CHUNKS_PER_STEP)
    rows_per = per * CHUNK
    nb = n // per

    def body(q_ref, k_ref, v_ref, sm_ref, gr_ref, st_ref, t_ref, do_ref, dqkv_ref, dsm_ref, dgr_ref, dstate):
        @pl.when(pl.program_id(0) == 0)
        def _():
            dstate[...] = jnp.zeros_like(dstate)

        ds = dstate[...]
        for b in reversed(range(per)):
            rows = pl.ds(b * CHUNK, CHUNK)
            t_saved = t_ref[b]
            _, vjp = jax.vjp(lambda *a: chunk_fn(*a, t_known=t_saved), _heads(q_ref, rows), _heads(k_ref, rows),
                             _heads(v_ref, rows), _head_cols(sm_ref, rows, 8), gr_ref[b][:, None, :],
                             _head_cols(sm_ref, rows, 16), st_ref[b])
            dq, dk, dv, dgc, dgr, dbt, ds = vjp((_heads(do_ref, rows), ds))
            for h in range(N_HEADS):
                dqkv_ref[rows, h * D_HEAD:(h + 1) * D_HEAD] = dq[h]
                dqkv_ref[rows, WIDTH + h * D_HEAD:WIDTH + (h + 1) * D_HEAD] = dk[h]
                dqkv_ref[rows, 2 * WIDTH + h * D_HEAD:2 * WIDTH + (h + 1) * D_HEAD] = dv[h]
            cols = [dgc[h] for h in range(N_HEADS)] + [dbt[h] for h in range(N_HEADS)]
            dsm_ref[rows, :] = jnp.concatenate([jnp.zeros((CHUNK, 8), F32)] + cols
                                               + [jnp.zeros((CHUNK, 128 - 24), F32)], axis=1)
            dgr_ref[b] = jnp.concatenate([dgr[h] for h in range(N_HEADS)], axis=0)
        dstate[...] = ds

    rev = lambda c: (lambda i: (nb - 1 - i, c))
    rev4 = lambda i: (nb - 1 - i, 0, 0, 0)
    return pl.pallas_call(
        body, name="gdn_chunk_bwd", grid=(nb,),
        in_specs=[pl.BlockSpec((rows_per, WIDTH), rev(g)) for g in range(3)]
        + [pl.BlockSpec((rows_per, 128), rev(0)), pl.BlockSpec((per, N_HEADS, CHUNK), lambda i: (nb - 1 - i, 0, 0)),
           pl.BlockSpec((per, N_HEADS, D_HEAD, D_HEAD), rev4), pl.BlockSpec((per, N_HEADS, CHUNK, CHUNK), rev4),
           pl.BlockSpec((rows_per, WIDTH), rev(0))],
        out_specs=[pl.BlockSpec((rows_per, 3 * WIDTH), rev(0)), pl.BlockSpec((rows_per, 128), rev(0)),
                   pl.BlockSpec((per, N_HEADS, CHUNK), lambda i: (nb - 1 - i, 0, 0))],
        out_shape=[jax.ShapeDtypeStruct((S, 3 * WIDTH), F32)] + [jax.ShapeDtypeStruct((S, 128), F32),
                                                                  jax.ShapeDtypeStruct((n, N_HEADS, CHUNK), F32)],
        scratch_shapes=[pltpu.VMEM((N_HEADS, D_HEAD, D_HEAD), F32)],
    )(qkv, qkv, qkv, smalls, gc_row, states, tinv, do)


def _flip(a, bit):
    return 1 - a if bit else a


def allgather8(name, buf):
    R, W = buf.shape

    def body(x_ref, out_ref, send_sems, recv_sems, local_sem):
        x, y, c = lax.axis_index("x"), lax.axis_index("y"), lax.axis_index("c")

        def slot(px, py, pc):
            return out_ref.at[4 * px + 2 * py + pc]

        mine = pltpu.make_async_copy(x_ref, slot(x, y, c), local_sem)
        mine.start()
        sends = []
        for r in range(1, N_DEV):
            peer = (_flip(x, r & 4), _flip(y, r & 2), _flip(c, r & 1))
            cp = pltpu.make_async_remote_copy(src_ref=x_ref, dst_ref=slot(x, y, c), send_sem=send_sems.at[r - 1],
                                              recv_sem=recv_sems.at[r - 1], device_id=peer, device_id_type=MESH_ID)
            cp.start()
            sends.append(cp)
        for r in range(1, N_DEV):
            peer = (_flip(x, r & 4), _flip(y, r & 2), _flip(c, r & 1))
            pltpu.make_async_remote_copy(src_ref=x_ref, dst_ref=slot(*peer), send_sem=send_sems.at[r - 1],
                                         recv_sem=recv_sems.at[r - 1], device_id=peer,
                                         device_id_type=MESH_ID).wait_recv()
        for cp in sends:
            cp.wait_send()
        mine.wait()

    return pl.pallas_call(
        body, name=name,
        out_shape=jax.ShapeDtypeStruct((N_DEV, R, W), buf.dtype),
        in_specs=[pl.BlockSpec(memory_space=pltpu.VMEM)],
        out_specs=pl.BlockSpec(memory_space=pltpu.VMEM),
        scratch_shapes=[pltpu.SemaphoreType.DMA((N_DEV - 1,)), pltpu.SemaphoreType.DMA((N_DEV - 1,)),
                        pltpu.SemaphoreType.DMA],
    )(buf)


def allgather_chips(name, blk):
    R, W = blk.shape
    half = R // 2
    n_far = N_CHIPS - 1

    def body(x_ref, out_ref, send_sems, recv_sems):
        x, y, c = lax.axis_index("x"), lax.axis_index("y"), lax.axis_index("c")
        sibling = (x, y, 1 - c)
        chips = [(_flip(x, r & 2), _flip(y, r & 1)) for r in range(1, N_CHIPS)]

        def rows(px, py, pc):
            return out_ref.at[2 * px + py, pc]

        def copy(k, dst, to, src):
            return pltpu.make_async_remote_copy(src_ref=src, dst_ref=dst, send_sem=send_sems.at[k],
                                                recv_sem=recv_sems.at[k], device_id=to, device_id_type=MESH_ID)

        my_half = x_ref.at[c]
        first = [copy(j, rows(x, y, c), (*chip, c), my_half) for j, chip in enumerate(chips)]
        for cp in first:
            cp.start()
        passed = [copy(n_far + j, rows(*chip, c), sibling, rows(*chip, c)) for j, chip in enumerate(chips)]
        for j, chip in enumerate(chips):
            copy(j, rows(*chip, c), (*chip, c), my_half).wait_recv()
            passed[j].start()
        for j, chip in enumerate(chips):
            copy(n_far + j, rows(*chip, 1 - c), sibling, my_half).wait_recv()
        for cp in first + passed:
            cp.wait_send()

    far = pl.pallas_call(
        body, name=name,
        out_shape=jax.ShapeDtypeStruct((N_CHIPS, 2, half, W), blk.dtype),
        in_specs=[pl.BlockSpec(memory_space=pl.ANY)],
        out_specs=pl.BlockSpec(memory_space=pl.ANY),
        scratch_shapes=[pltpu.SemaphoreType.DMA((2 * n_far,)), pltpu.SemaphoreType.DMA((2 * n_far,))],
    )(blk.reshape(2, half, W)).reshape(N_CHIPS, R, W)
    own_chip = 2 * lax.axis_index("x") + lax.axis_index("y")
    return lax.dynamic_update_index_in_dim(far, blk, own_chip, axis=0)


def sibling_send_other_half(name, g4):
    n, R, W = g4.shape
    half = R // 2

    def body(x_ref, out_ref, send_sem, recv_sem):
        x, y, c = lax.axis_index("x"), lax.axis_index("y"), lax.axis_index("c")
        cp = pltpu.make_async_remote_copy(src_ref=x_ref.at[:, pl.ds((1 - c) * half, half), :], dst_ref=out_ref,
                                          send_sem=send_sem, recv_sem=recv_sem, device_id=(x, y, 1 - c),
                                          device_id_type=MESH_ID)
        cp.start()
        cp.wait_recv()
        cp.wait_send()

    return pl.pallas_call(
        body, name=name,
        out_shape=jax.ShapeDtypeStruct((n, half, W), g4.dtype),
        in_specs=[pl.BlockSpec(memory_space=pl.ANY)],
        out_specs=pl.BlockSpec(memory_space=pl.ANY),
        scratch_shapes=[pltpu.SemaphoreType.DMA, pltpu.SemaphoreType.DMA],
    )(g4)


def sibling_merge(name, mine):
    def body(x_ref, out_ref, send_sem, recv_sem):
        x, y, c = lax.axis_index("x"), lax.axis_index("y"), lax.axis_index("c")
        cp = pltpu.make_async_remote_copy(src_ref=x_ref, dst_ref=out_ref, send_sem=send_sem, recv_sem=recv_sem,
                                          device_id=(x, y, 1 - c), device_id_type=MESH_ID)
        cp.start()
        cp.wait_recv()
        cp.wait_send()

    other = pl.pallas_call(
        body, name=name,
        out_shape=jax.ShapeDtypeStruct(mine.shape, mine.dtype),
        in_specs=[pl.BlockSpec(memory_space=pl.ANY)],
        out_specs=pl.BlockSpec(memory_space=pl.ANY),
        scratch_shapes=[pltpu.SemaphoreType.DMA, pltpu.SemaphoreType.DMA],
    )(mine)
    first = lax.axis_index("c") == 0
    return jnp.concatenate([jnp.where(first, mine, other), jnp.where(first, other, mine)], axis=0)


def sum_parts(name, own, recv, tr):
    R, W = own.shape
    tr = _tile(R, tr)

    def body(o_ref, r_ref, out_ref):
        acc = o_ref[...]
        for k in range(recv.shape[0]):
            acc = acc + r_ref[k].astype(F32)
        out_ref[...] = acc

    return pl.pallas_call(
        body, name=name, grid=(R // tr,),
        in_specs=[pl.BlockSpec((tr, W), lambda i: (i, 0)), pl.BlockSpec((recv.shape[0], tr, W), lambda i: (0, i, 0))],
        out_specs=pl.BlockSpec((tr, W), lambda i: (i, 0)),
        out_shape=jax.ShapeDtypeStruct((R, W), F32),
    )(own, recv)


def sum_own_half(name, g4, recv, c_idx, tr):
    n, R, W = g4.shape
    half = R // 2
    tr = _tile(half, tr)
    nb = half // tr

    def body(c_ref, g_ref, r_ref, o_ref, ob_ref):
        s = g_ref[0] + r_ref[0]
        o_ref[0] = s
        ob_ref[0] = s.astype(ob_ref.dtype)

    mine = pl.BlockSpec((1, tr, W), lambda j, i, c: (j, i, 0))
    return pl.pallas_call(
        body, name=name,
        grid_spec=pltpu.PrefetchScalarGridSpec(
            num_scalar_prefetch=1, grid=(n, nb),
            in_specs=[pl.BlockSpec((1, tr, W), lambda j, i, c: (j, c[0] * nb + i, 0)), mine],
            out_specs=[mine, mine]),
        out_shape=[jax.ShapeDtypeStruct((n, half, W), F32), jax.ShapeDtypeStruct((n, half, W), MXU_DTYPE)],
    )(c_idx, g4, recv)


def sum_devices(name, g):
    def body(g_ref, o_ref):
        acc = g_ref[0]
        for d in range(1, N_DEV):
            acc = acc + g_ref[d]
        o_ref[...] = acc

    return pl.pallas_call(body, name=name, out_shape=jax.ShapeDtypeStruct(g.shape[1:], F32))(g)


def ada_fwd(c_all, w_ada):
    K, N = w_ada.shape
    tn = _tile(N, 512)

    def body(c_ref, w_ref, o_ref):
        o_ref[...] = mm_nn(_silu(c_ref[...]), w_ref[...])

    return pl.pallas_call(
        body, name="ada_fwd", grid=(N // tn,),
        in_specs=[pl.BlockSpec((N_DEV, K), lambda j: (0, 0)), pl.BlockSpec((K, tn), lambda j: (0, j))],
        out_specs=pl.BlockSpec((N_DEV, tn), lambda j: (0, j)),
        out_shape=jax.ShapeDtypeStruct((N_DEV, N), F32),
    )(c_all, w_ada)


def ada_grad(c_all, dmod):
    K = c_all.shape[1]
    N = dmod.shape[1]
    tn = _tile(N, 512)

    def body(c_ref, d_ref, o_ref):
        o_ref[...] = mm_tn(_silu(c_ref[...]), d_ref[...])

    return pl.pallas_call(
        body, name="ada_grad", grid=(N // tn,),
        in_specs=[pl.BlockSpec((N_DEV, K), lambda j: (0, 0)), pl.BlockSpec((N_DEV, tn), lambda j: (0, j))],
        out_specs=pl.BlockSpec((K, tn), lambda j: (0, j)),
        out_shape=jax.ShapeDtypeStruct((K, N), F32),
    )(c_all, dmod)


def adamw(name, w, gparts, m, v, tr, tc=None):
    R, W = w.shape
    tr = R if tc else _tile(R, tr)
    ng = len(gparts)

    def body(w_ref, *refs):
        g_refs, (m_ref, v_ref, g_out, d_out, m_out, v_out) = refs[:ng], refs[ng:]
        g = g_refs[0][...]
        for r in g_refs[1:]:
            g = g + r[...]
        m1 = ADAM_B1 * m_ref[...] + (1.0 - ADAM_B1) * g
        v1 = ADAM_B2 * v_ref[...] + (1.0 - ADAM_B2) * jnp.square(g)
        m_hat = m1 / (1.0 - ADAM_B1 ** ADAM_STEP)
        v_hat = v1 / (1.0 - ADAM_B2 ** ADAM_STEP)
        g_out[...] = g
        d_out[...] = -ADAM_LR * (m_hat / (jnp.sqrt(v_hat) + ADAM_EPS) + ADAM_WD * w_ref[...])
        m_out[...] = m1
        v_out[...] = v1

    spec = pl.BlockSpec((R, tc), lambda i: (0, i)) if tc else pl.BlockSpec((tr, W), lambda i: (i, 0))
    return pl.pallas_call(
        body, name=name, grid=(W // tc if tc else R // tr,),
        in_specs=[spec] * (3 + ng), out_specs=[spec] * 4,
        out_shape=[jax.ShapeDtypeStruct((R, W), F32)] * 4,
    )(w, *gparts, m, v)


_REF_SEGMENTS = ((0, 4 * WIDTH, 0), (4 * WIDTH, 4 * WIDTH + 8, 4 * WIDTH), (4 * WIDTH + 8, 8 * WIDTH + 8, -8),
                 (8 * WIDTH + 8, IN_WIDTH, 0))


def _internal_from_blocks(blocks, n):
    a = 4 * WIDTH
    pieces = []
    for lo, hi in ((0, a), (a + 8, 2 * a + 8), (a, a + 8), (2 * a + 8, IN_WIDTH)):
        for j in range(N_CHIPS):
            s, e = max(lo, j * n), min(hi, (j + 1) * n)
            if s < e:
                pieces.append(blocks[j][:, s - j * n:e - j * n])
    pieces.append(jnp.zeros(blocks.shape[1:2] + (128 - N_SMALL,), blocks.dtype))
    return jnp.concatenate(pieces, axis=1)


def _block_from_internal(g, j, n):
    pieces = []
    for lo, hi, shift in _REF_SEGMENTS:
        s, e = max(lo, j * n), min(hi, (j + 1) * n)
        if s < e:
            pieces.append(g[:, s + shift:e + shift])
    return jnp.concatenate(pieces, axis=1)


def _pad_lanes(v, n=128):
    return jnp.pad(v, ((0, 0), (0, n - v.shape[1])))


def kernel(x, c, norm_g, w_ada, b_ada, w_in, b_fgate, fox_qn_g, fox_kn_g, gdn_conv_w, gdn_A_log, gdn_dt_bias, gdn_norm_g, w_out, final_g, loss_target, m_norm_g, m_w_ada, m_b_ada, m_w_in, m_b_fgate, m_fox_qn_g, m_fox_kn_g, m_gdn_conv_w, m_gdn_A_log, m_gdn_dt_bias, m_gdn_norm_g, m_w_out, m_final_g, v_norm_g, v_w_ada, v_b_ada, v_w_in, v_b_fgate, v_fox_qn_g, v_fox_kn_g, v_gdn_conv_w, v_gdn_A_log, v_gdn_dt_bias, v_gdn_norm_g, v_w_out, v_final_g):
    S = x.shape[1]
    H = N_HEADS
    ix, iy, ic = lax.axis_index("x"), lax.axis_index("y"), lax.axis_index("c")
    chip = 2 * ix + iy
    me = 2 * chip + ic
    x2, tgt = x[0], loss_target[0]
    ts_wide = _tile(S, 256)
    ts = _tile(S, 512)
    ts_head = _tile(S, 2048)
    tq = _tile(S, 1024)
    nq = S // tq
    nch = S // CHUNK
    conv_cols_chip = CONV_COLS // N_CHIPS

    pay = jnp.concatenate([c, _pad_lanes(gdn_conv_w[0], D_MODEL), jnp.zeros((3, D_MODEL), F32)], axis=0)
    g1 = allgather8("ag_c", pay)
    c_all = g1[:, 0, :]
    conv_w = jnp.concatenate([g1[2 * j, 1:1 + CONV_K, :conv_cols_chip] for j in range(N_CHIPS)], axis=1)
    g2 = allgather8("ag_mod", ada_fwd(c_all, w_ada[0]))
    mod_rows = jnp.concatenate([g2[2 * j] for j in range(N_CHIPS)], axis=1)
    mod = lax.dynamic_slice_in_dim(mod_rows, me, 1, axis=0) + b_ada
    shift, scale, gate = mod[:, :D_MODEL], mod[:, D_MODEL:2 * D_MODEL], mod[:, 2 * D_MODEL:]

    n_in = w_in.shape[2]
    win4 = allgather_chips("ag_w_in", w_in[0].astype(MXU_DTYPE))
    w_all = _internal_from_blocks(win4, n_in)
    w_out_b = w_out[0].astype(MXU_DTYPE)

    (h_b,) = rowwise_fwd("prologue", f_prologue, S, ts_wide, 1, [(x2, "row", D_MODEL, 0)], [norm_g, scale, shift],
                         [("row", D_MODEL, 0, D_MODEL, MXU_DTYPE)])
    p_all, wout_far = matmul("in_proj", h_b, w_all, F32, tm=1024, tn=1664, tk=2048, rides=[("gather", w_out_b)])
    w_out_full = lax.dynamic_update_index_in_dim(wout_far, w_out_b, chip, axis=0).reshape(2 * WIDTH, D_MODEL)
    pv = jnp.concatenate([b_fgate, gdn_dt_bias, jnp.zeros((1, 128 - 16), F32)], axis=1)
    av = jnp.concatenate([jnp.zeros((1, 8), F32), gdn_A_log, jnp.zeros((1, 128 - 16), F32)], axis=1)
    smalls = smalls_fwd(p_all, pv, av, S, ts)
    frow = jnp.transpose(smalls[:, :H]).reshape(H, nq, 1, tq)
    gc_row = jnp.transpose(smalls[:, 8:16].reshape(nch, CHUNK, H), (0, 2, 1))
    head_b = ("head", D_HEAD, 0, WIDTH, MXU_DTYPE)
    head_f = ("head", D_HEAD, 0, WIDTH, F32)
    pcol = lambda cb: (p_all, "head", D_HEAD, cb)
    qn, kn, vb = rowwise_fwd("fox_pre", f_foxpre, S, ts_head, H,[pcol(CB_FQ), pcol(CB_FK), pcol(CB_FV)],
                             [fox_qn_g, fox_kn_g], [head_b] * 3)
    o_fox, lse = flash_fwd(qn, kn, vb, frow, S, tq)
    y_conv, qkv = conv_act_fwd(p_all, conv_w, S, ts)
    o_gdn, states, tinv = gdn_fwd(qkv, smalls, gc_row, S)
    mixed = lax.empty((S, 2 * WIDTH), MXU_DTYPE)
    (mixed,) = rowwise_fwd("post_fox", f_postf, S, ts_head, H,[(o_fox, "head", D_HEAD, 0), pcol(CB_FZ)], [],
                           [("head", D_HEAD, 0, 2 * WIDTH, MXU_DTYPE, mixed)])
    (mixed,) = rowwise_fwd("post_gdn", f_postg, S, ts_head, H,[(o_gdn, "head", D_HEAD, 0), pcol(CB_GZ)], [gdn_norm_g],
                           [("head", D_HEAD, H, 2 * WIDTH, MXU_DTYPE, mixed)])
    mo = matmul("out_proj", mixed, w_out_full, F32, tm=1024, tn=2048, tk=2048)

    wide = lambda a: (a, "row", D_MODEL, 0)
    dx_res, dmo, d_gate, d_final_g, loss_acc = rowwise_bwd(
        "loss", f_loss, S, ts_wide, 1, [wide(x2), wide(mo), wide(tgt)], [gate, final_g[None, :]], [],
        [(0, "row", D_MODEL, 0, D_MODEL, F32), (1, "row", D_MODEL, 0, D_MODEL, MXU_DTYPE)], primal_sum=True)

    d_mixed = matmul("d_mixed", dmo, w_out_full, F32, tm=1024, tn=2048, tk=2048, trans_b=True)
    g_w_out = matmul("g_w_out", mixed, dmo, F32, tm=1024, tn=2048, tk=2048, trans_a=True)
    dp = lax.empty((S, P_COLS), MXU_DTYPE)
    into_dp = lambda idx, cb: (idx, "head", D_HEAD, cb, P_COLS, MXU_DTYPE, dp)
    do_fox, dp = rowwise_bwd(
        "post_fox_bwd", f_postf, S, ts_head, H,[(o_fox, "head", D_HEAD, 0), pcol(CB_FZ)], [],
        [(d_mixed, "head", D_HEAD, 0)], [(0,) + head_b, into_dp(1, CB_FZ)])
    do_gdn, dp, d_gdn_norm_g = rowwise_bwd(
        "post_gdn_bwd", f_postg, S, ts_head, H,[(o_gdn, "head", D_HEAD, 0), pcol(CB_GZ)], [gdn_norm_g],
        [(d_mixed, "head", D_HEAD, 8)], [(0,) + head_f, into_dp(1, CB_GZ)])
    d_qn, d_kn, dp, dfq_row, dfk_row = flash_bwd(qn, kn, vb, do_fox, frow, lse, flash_delta(do_fox, o_fox, S, tq), dp,
                                                 S, tq)
    dp, d_qn_g, d_kn_g = fox_norm_bwd(p_all, d_qn, d_kn, fox_qn_g, fox_kn_g, dp, S, ts_wide)
    d_qkv, dsm_chunk, dgc_row = gdn_bwd(qkv, smalls, gc_row, states, tinv, do_gdn, S)
    dp, d_conv_w = conv_act_bwd(p_all, y_conv, d_qkv, conv_w, dp, S, ts)
    d_f = jnp.transpose((dfq_row + dfk_row).reshape(H, S))
    d_gc = jnp.transpose(dgc_row, (0, 2, 1)).reshape(S, H)
    dsm = dsm_chunk + jnp.concatenate([d_f, d_gc, jnp.zeros((S, 128 - 16), F32)], axis=1)
    dp, d_pv, d_av = smalls_bwd(p_all, pv, av, dsm, dp, S, ts)
    g_w_all = matmul("g_w_in", h_b, dp, F32, tm=1024, tn=1664, tk=2048, trans_a=True)

    c_idx = jnp.reshape(ic, (1,)).astype(jnp.int32)

    def chip_sums(tag, g4):
        recv = sibling_send_other_half("d2d_" + tag, g4)
        return sum_own_half("sum2_" + tag, g4, recv, c_idx, 128)

    def finish(tag, p4, far):
        own = lax.dynamic_index_in_dim(p4, chip, axis=0, keepdims=False)
        return sibling_merge("merge_" + tag, sum_parts("sum_" + tag, own, far, 128))

    p4_in, p4_in_sent = chip_sums("w_in", jnp.stack([_block_from_internal(g_w_all, j, n_in) for j in range(N_CHIPS)]))
    p4_out, p4_out_sent = chip_sums("w_out", g_w_out.reshape(N_CHIPS, w_out.shape[1], D_MODEL))
    d_h, far_in, far_out = matmul("d_h", dp, w_all, F32, tm=1024, tn=2048, tk=1664, trans_b=True,
                                  rides=[("scatter", p4_in_sent), ("scatter", p4_out_sent)])
    g_w_in_full = finish("w_in", p4_in, far_in)
    g_w_out_full = finish("w_out", p4_out, far_out)
    d_x, d_norm_g, d_scale, d_shift = rowwise_bwd(
        "prologue_bwd", f_prologue_res, S, ts_wide, 1, [wide(x2)], [norm_g, scale, shift],
        [wide(d_h), wide(dx_res)], [(0, "row", D_MODEL, 0, D_MODEL, F32)])

    parts = [d_shift, d_scale, d_gate, d_norm_g, d_final_g, d_conv_w.reshape(1, CONV_K * CONV_COLS), d_qn_g, d_kn_g,
             d_gdn_norm_g, d_pv, d_av, loss_acc]
    n_pay = sum(p.shape[1] for p in parts)
    pay_w = -(-n_pay // 1024) * 1024
    pay2 = jnp.concatenate(parts + [jnp.zeros((1, pay_w - n_pay), F32)], axis=1).reshape(8, pay_w // 8)
    g3 = allgather8("ag_small", pay2)
    tot = sum_devices("sum_small", g3).reshape(1, pay_w)
    dmod_all = g3.reshape(N_DEV, pay_w)[:, :3 * D_MODEL]
    o = 3 * D_MODEL
    g_b_ada = tot[:, :o]
    g_norm_g, g_final_g = tot[:, o:o + D_MODEL], tot[:, o + D_MODEL:o + 2 * D_MODEL]
    o += 2 * D_MODEL
    g_conv_full = tot[:, o:o + CONV_K * CONV_COLS].reshape(CONV_K, CONV_COLS)
    g_conv = lax.dynamic_slice_in_dim(g_conv_full, chip * conv_cols_chip, conv_cols_chip, axis=1)
    o += CONV_K * CONV_COLS
    g_qn_g, g_kn_g, g_gdn_ng = tot[:, o:o + 128], tot[:, o + 128:o + 256], tot[:, o + 256:o + 384]
    g_pv, g_av = tot[:, o + 384:o + 512], tot[:, o + 512:o + 640]
    loss = tot[0, o + 640]
    g_b_fgate, g_dt_bias, g_a_log = g_pv[:, :8], g_pv[:, 8:16], g_av[:, 8:16]

    def small_vec(vals):
        flat = [norm for norm in vals[:3]] + [vals[3].reshape(1, CONV_K * conv_cols_chip)] + list(vals[4:7]) \
            + [_pad_lanes(s) for s in vals[7:]]
        n = sum(f.shape[1] for f in flat)
        nw = -(-n // 1024) * 1024
        return jnp.concatenate(flat + [jnp.zeros((1, nw - n), F32)], axis=1).reshape(8, nw // 8)

    sw = small_vec([norm_g, b_ada, final_g[None, :], gdn_conv_w[0], fox_qn_g, fox_kn_g, gdn_norm_g, b_fgate, gdn_A_log,
                    gdn_dt_bias])
    sm = small_vec([m_norm_g, m_b_ada, m_final_g[None, :], m_gdn_conv_w[0], m_fox_qn_g, m_fox_kn_g, m_gdn_norm_g,
                    m_b_fgate, m_gdn_A_log, m_gdn_dt_bias])
    sv = small_vec([v_norm_g, v_b_ada, v_final_g[None, :], v_gdn_conv_w[0], v_fox_qn_g, v_fox_kn_g, v_gdn_norm_g,
                    v_b_fgate, v_gdn_A_log, v_gdn_dt_bias])
    sg = small_vec([g_norm_g, g_b_ada, g_final_g, g_conv, g_qn_g, g_kn_g, g_gdn_ng, g_b_fgate, g_a_log, g_dt_bias])
    small_out = adamw("adamw_small", sw, [sg], sm, sv, 8)

    def split_small(a):
        a = a.reshape(1, -1)
        out, o = [], 0
        for n, shp in ((D_MODEL, (1, D_MODEL)), (3 * D_MODEL, (1, 3 * D_MODEL)), (D_MODEL, (D_MODEL,)),
                       (CONV_K * conv_cols_chip, (1, CONV_K, conv_cols_chip)), (128, (1, 128)), (128, (1, 128)),
                       (128, (1, 128))):
            out.append(a[:, o:o + n].reshape(shp))
            o += n
        for _ in range(3):
            out.append(a[:, o:o + 8])
            o += 128
        return out

    n_ada = w_ada.shape[2]
    g_w_ada = ada_grad(c_all, lax.dynamic_slice_in_dim(dmod_all, chip * n_ada, n_ada, axis=1))
    ada_out = adamw("adamw_w_ada", w_ada[0], [g_w_ada], m_w_ada[0], v_w_ada[0], 128)

    in_out = [jnp.transpose(o) for o in adamw("adamw_w_in", jnp.transpose(w_in[0]), [jnp.transpose(g_w_in_full)],
                                              jnp.transpose(m_w_in[0]), jnp.transpose(v_w_in[0]), None, tc=256)]
    out_out = adamw("adamw_w_out", w_out[0], [g_w_out_full], m_w_out[0], v_w_out[0], 128)

    res = [loss, d_x[None]]
    for k in range(4):
        s = split_small(small_out[k])
        big = [ada_out[k][None], in_out[k][None], out_out[k][None]]
        res += [s[0], big[0], s[1], big[1], s[7], s[4], s[5], s[3], s[8], s[9], s[6], big[2], s[2]]
    return tuple(res)
```

```python
import functools

import jax
import jax.numpy as jnp
from jax import lax
from jax.experimental import pallas as pl
from jax.experimental.pallas import tpu as pltpu

F32 = jnp.float32
MXU_DTYPE = jnp.bfloat16
HIGHEST = lax.Precision.HIGHEST
SOLVE_PRECISION = lax.Precision.HIGH
MESH_ID = pl.DeviceIdType.MESH

D_MODEL = 2048
N_HEADS = 8
D_HEAD = 128
WIDTH = N_HEADS * D_HEAD
CHUNK = 64
CONV_K = 4
EPS = 1e-6
NEG = -1e30
ATT_SCALE = D_HEAD ** -0.5
N_SMALL = 3 * N_HEADS
P_COLS = 8 * WIDTH + 128
CB_FQ, CB_FK, CB_FV, CB_FZ, CB_GQ, CB_GK, CB_GV, CB_GZ, CB_SMALL = 0, 8, 16, 24, 32, 40, 48, 56, 64
IN_WIDTH = 8 * WIDTH + N_SMALL
N_CHIPS = 4
N_DEV = 8

ADAM_LR, ADAM_B1, ADAM_B2, ADAM_EPS, ADAM_WD, ADAM_STEP = 0.001, 0.9, 0.999, 1e-08, 0.01, 10


def _dotg(a, b, dims):
    return lax.dot_general(a.astype(MXU_DTYPE), b.astype(MXU_DTYPE), (dims, ((), ())), preferred_element_type=F32)


@jax.custom_vjp
def mm_nn(a, b):
    return _dotg(a, b, ((1,), (0,)))


@jax.custom_vjp
def mm_nt(a, b):
    return _dotg(a, b, ((1,), (1,)))


@jax.custom_vjp
def mm_tn(a, b):
    return _dotg(a, b, ((0,), (0,)))


mm_nn.defvjp(lambda a, b: (mm_nn(a, b), (a, b)), lambda r, g: (mm_nt(g, r[1]), mm_tn(r[0], g)))
mm_nt.defvjp(lambda a, b: (mm_nt(a, b), (a, b)), lambda r, g: (mm_nn(g, r[1]), mm_tn(g, r[0])))
mm_tn.defvjp(lambda a, b: (mm_tn(a, b), (a, b)), lambda r, g: (mm_nt(r[1], g), mm_nn(r[0], g)))


def hdot(a, b):
    return lax.dot_general(a, b, (((1,), (0,)), ((), ())), precision=HIGHEST, preferred_element_type=F32)


def _sigmoid(x):
    return 0.5 * (jnp.tanh(0.5 * x) + 1.0)


def _silu(x):
    return x * _sigmoid(x)


def _softplus(x):
    return jnp.maximum(x, 0.0) + jnp.log(1.0 + jnp.exp(-jnp.abs(x)))


def _log_sigmoid(x):
    return jnp.minimum(x, 0.0) - jnp.log(1.0 + jnp.exp(-jnp.abs(x)))


def _rms(x, g):
    return x * lax.rsqrt(jnp.mean(x * x, axis=-1, keepdims=True) + EPS) * g


def _l2n(x):
    return x * lax.rsqrt(jnp.sum(x * x, axis=-1, keepdims=True) + EPS)


def _tile(n, pref):
    t = min(n, pref)
    assert n % t == 0, (n, pref)
    return t


def _bspec(kind, w, col0, ts):
    if kind == "row":
        return pl.BlockSpec((ts, w), lambda i, h: (i, col0))
    if kind == "head":
        return pl.BlockSpec((ts, w), lambda i, h: (i, col0 + h))
    assert kind == "head3"
    return pl.BlockSpec((1, ts, w), lambda i, h: (h, i, 0))


def _ld(ref, kind):
    return ref[0] if kind == "head3" else ref[...]


def _st(ref, kind, val):
    if kind == "head3":
        ref[0] = val.astype(ref.dtype)
    else:
        ref[...] = val.astype(ref.dtype)


def _full_spec(a):
    nd = a.ndim
    return pl.BlockSpec(a.shape, lambda i, h: (0,) * nd)


def _out_struct(kind, w, S, width_total, dtype, nh):
    if kind == "head3":
        return jax.ShapeDtypeStruct((nh, S, w), dtype)
    return jax.ShapeDtypeStruct((S, width_total), dtype)


def rowwise_fwd(name, f, S, ts, nh, ins, params, outs):
    n_in, n_p = len(ins), len(params)
    into = {j: o[5] for j, o in enumerate(outs) if len(o) > 5}
    outs = [o[:5] for o in outs]
    n_al = len(into)

    def body(*refs):
        vals = [_ld(r, s[1]).astype(F32) for r, s in zip(refs[:n_in], ins)]
        pv = [r[...] for r in refs[n_in:n_in + n_p]]
        res = f(*vals, *pv)
        for r, s, o in zip(refs[n_in + n_p + n_al:], outs, res):
            _st(r, s[0], o)

    return pl.pallas_call(
        body, name=name, grid=(S // ts, nh),
        in_specs=[_bspec(k, w, c0, ts) for (_, k, w, c0) in ins] + [_full_spec(p) for p in params]
        + [pl.BlockSpec(memory_space=pl.ANY)] * n_al,
        out_specs=[_bspec(k, w, c0, ts) for (k, w, c0, _, _) in outs],
        out_shape=[jax.ShapeDtypeStruct(into[j].shape, into[j].dtype) if j in into else _out_struct(k, w, S, tw, dt, nh)
                   for j, (k, w, c0, tw, dt) in enumerate(outs)],
        input_output_aliases={n_in + n_p + n: j for n, j in enumerate(sorted(into))},
    )(*[a for (a, _, _, _) in ins], *params, *[into[j] for j in sorted(into)])


def rowwise_bwd(name, f, S, ts, nh, ins, params, cts, row_grads, extra=None, extra_outs=(), primal_sum=False):
    n_in, n_p, n_ct = len(ins), len(params), len(cts)
    n_rg, n_ex = len(row_grads), len(extra_outs)
    into = {j: rg[6] for j, rg in enumerate(row_grads) if len(rg) > 6}
    row_grads = [rg[:6] for rg in row_grads]
    n_al = len(into)

    def body(*refs):
        first = (pl.program_id(0) == 0) & (pl.program_id(1) == 0)
        in_refs = refs[:n_in]
        p_refs = refs[n_in:n_in + n_p]
        ct_refs = refs[n_in + n_p:n_in + n_p + n_ct]
        o = n_in + n_p + n_ct + n_al
        rg_refs = refs[o:o + n_rg]
        ex_refs = refs[o + n_rg:o + n_rg + n_ex]
        acc_refs = refs[o + n_rg + n_ex:]
        vals = [_ld(r, s[1]).astype(F32) for r, s in zip(in_refs, ins)]
        pv = [r[...] for r in p_refs]
        res, vjp = jax.vjp(f, *vals, *pv)
        if primal_sum:
            cv = [jnp.ones_like(res[0])] + [_ld(r, s[1]).astype(F32) for r, s in zip(ct_refs, cts)]
        else:
            cv = [_ld(r, s[1]).astype(F32) for r, s in zip(ct_refs, cts)]
        grads = vjp(tuple(cv))
        for r, s in zip(rg_refs, row_grads):
            _st(r, s[1], grads[s[0]])
        if extra is not None:
            for r, s, e in zip(ex_refs, extra_outs, extra(vals, cv, grads)):
                _st(r, s[0], e)

        @pl.when(first)
        def _():
            for r in acc_refs:
                r[...] = jnp.zeros_like(r)

        for r, g in zip(acc_refs[:n_p], grads[n_in:]):
            r[...] += g
        if primal_sum:
            acc_refs[n_p][...] += jnp.sum(res[0])

    acc_shapes = [jax.ShapeDtypeStruct(p.shape, F32) for p in params]
    acc_specs = [_full_spec(p) for p in params]
    if primal_sum:
        acc_shapes.append(jax.ShapeDtypeStruct((1, 128), F32))
        acc_specs.append(pl.BlockSpec((1, 128), lambda i, h: (0, 0)))
    rg_shapes = [jax.ShapeDtypeStruct(into[j].shape, into[j].dtype) if j in into else _out_struct(k, w, S, tw, dt, nh)
                 for j, (_, k, w, c0, tw, dt) in enumerate(row_grads)]
    first_alias = n_in + n_p + n_ct
    return pl.pallas_call(
        body, name=name, grid=(S // ts, nh),
        in_specs=[_bspec(k, w, c0, ts) for (_, k, w, c0) in ins] + [_full_spec(p) for p in params]
        + [_bspec(k, w, c0, ts) for (_, k, w, c0) in cts] + [pl.BlockSpec(memory_space=pl.ANY)] * n_al,
        out_specs=[_bspec(k, w, c0, ts) for (_, k, w, c0, _, _) in row_grads]
        + [_bspec(k, w, c0, ts) for (k, w, c0, _, _) in extra_outs] + acc_specs,
        out_shape=rg_shapes + [_out_struct(k, w, S, tw, dt, nh) for (k, w, c0, tw, dt) in extra_outs] + acc_shapes,
        input_output_aliases={first_alias + n: j for n, j in enumerate(sorted(into))},
    )(*[a for (a, _, _, _) in ins], *params, *[a for (a, _, _, _) in cts], *[into[j] for j in sorted(into)])


def f_prologue(x, ng, sc, sh):
    return (_rms(x, ng) * (1.0 + sc) + sh,)


def f_prologue_res(x, ng, sc, sh):
    return (_rms(x, ng) * (1.0 + sc) + sh, x)


def f_loss(x, mo, tgt, gate, fg):
    y = _rms(x + gate * mo, fg)
    return (0.5 * jnp.mean(jnp.square(y - tgt), axis=-1, keepdims=True),)


def f_foxpre(fq, fk, fv, qg, kg):
    return (_rms(fq, qg) * ATT_SCALE, _rms(fk, kg), fv)


def f_postf(o, z):
    return (o * _silu(z),)


def f_postg(o, z, ng):
    return (_rms(o, ng) * _silu(z),)


def fox_norm_bwd(p_all, d_qn, d_kn, qg, kg, dp, S, ts):
    def body(p_ref, dq_ref, dk_ref, qg_ref, kg_ref, dp_in, o_ref, dqg_ref, dkg_ref):
        @pl.when(pl.program_id(0) == 0)
        def _():
            dqg_ref[...] = jnp.zeros_like(dqg_ref)
            dkg_ref[...] = jnp.zeros_like(dkg_ref)

        for ct_ref, g_ref, dg_ref, col0 in ((dq_ref, qg_ref, dqg_ref, 0), (dk_ref, kg_ref, dkg_ref, WIDTH)):
            dg = jnp.zeros((1, D_HEAD), F32)
            for h in range(N_HEADS):
                _, vjp = jax.vjp(_rms, p_ref[:, col0 + h * D_HEAD:col0 + (h + 1) * D_HEAD], g_ref[...])
                dx, dg_h = vjp(ct_ref[:, h * D_HEAD:(h + 1) * D_HEAD])
                o_ref[:, col0 + h * D_HEAD:col0 + (h + 1) * D_HEAD] = dx.astype(o_ref.dtype)
                dg = dg + dg_h
            dg_ref[...] += dg

    gain = pl.BlockSpec((1, D_HEAD), lambda i: (0, 0))
    both = pl.BlockSpec((ts, 2 * WIDTH), lambda i: (i, 0))
    one = pl.BlockSpec((ts, WIDTH), lambda i: (i, 0))
    return pl.pallas_call(
        body, name="fox_norm_bwd", grid=(S // ts,),
        in_specs=[both, one, one, gain, gain, pl.BlockSpec(memory_space=pl.ANY)],
        out_specs=[both, gain, gain],
        out_shape=[jax.ShapeDtypeStruct(dp.shape, dp.dtype), jax.ShapeDtypeStruct((1, D_HEAD), F32),
                   jax.ShapeDtypeStruct((1, D_HEAD), F32)],
        input_output_aliases={5: 0},
    )(p_all, d_qn, d_kn, qg, kg, dp)


def _chip_exchange(kind, x_ref, out_ref, send_sems, recv_sems):
    x, y, c = lax.axis_index("x"), lax.axis_index("y"), lax.axis_index("c")
    sends, arrivals = [], []
    for r in range(1, N_CHIPS):
        px, py = _flip(x, r & 2), _flip(y, r & 1)
        if kind == "scatter":
            src, dst, landed = x_ref.at[2 * px + py], out_ref.at[r - 1], out_ref.at[r - 1]
        else:
            src, dst, landed = x_ref, out_ref.at[2 * x + y], out_ref.at[2 * px + py]
        mk = functools.partial(pltpu.make_async_remote_copy, send_sem=send_sems.at[r - 1], recv_sem=recv_sems.at[r - 1],
                               device_id=(px, py, c), device_id_type=MESH_ID)
        sends.append(mk(src_ref=src, dst_ref=dst))
        arrivals.append(mk(src_ref=src, dst_ref=landed))
    return sends, arrivals


def _exchange_shape(kind, x):
    return jax.ShapeDtypeStruct(((N_CHIPS - 1,) + x.shape[1:]) if kind == "scatter" else ((N_CHIPS,) + x.shape), x.dtype)


def matmul(name, a, b, out_dtype, tm=512, tn=512, tk=2048, trans_a=False, trans_b=False, rides=()):
    M, K = a.shape[::-1] if trans_a else a.shape
    N, K2 = b.shape if trans_b else b.shape[::-1]
    assert K == K2
    tm, tn, tk = _tile(M, tm), _tile(N, tn), _tile(K, tk)
    ni, nj, nk = M // tm, N // tn, K // tk
    a_spec = pl.BlockSpec((tk, tm), lambda i, j, k: (k, i)) if trans_a else pl.BlockSpec((tm, tk), lambda i, j, k: (i, k))
    b_spec = pl.BlockSpec((tn, tk), lambda i, j, k: (j, k)) if trans_b else pl.BlockSpec((tk, tn), lambda i, j, k: (k, j))
    n_ride = len(rides)

    def body(a_ref, b_ref, *rest):
        ride_in, o_ref, ride_out = rest[:n_ride], rest[n_ride], rest[n_ride + 1:2 * n_ride + 1]
        scr = rest[2 * n_ride + 1:]
        i, j, k = pl.program_id(0), pl.program_id(1), pl.program_id(2)
        exchanges = [_chip_exchange(kind, ride_in[n], ride_out[n], scr[2 * n], scr[2 * n + 1])
                     for n, (kind, _) in enumerate(rides)]

        @pl.when((i == 0) & (j == 0) & (k == 0))
        def _():
            for sends, _ in exchanges:
                for cp in sends:
                    cp.start()

        part = lax.dot_general(a_ref[...], b_ref[...], (((0 if trans_a else 1,), (1 if trans_b else 0,)), ((), ())),
                               preferred_element_type=F32)
        if nk == 1:
            o_ref[...] = part.astype(o_ref.dtype)
        else:
            acc = scr[2 * n_ride]

            @pl.when(k == 0)
            def _():
                acc[...] = part

            @pl.when(k > 0)
            def _():
                acc[...] += part

            @pl.when(k == nk - 1)
            def _():
                o_ref[...] = acc[...].astype(o_ref.dtype)

        @pl.when((i == ni - 1) & (j == nj - 1) & (k == nk - 1))
        def _():
            for sends, arrivals in exchanges:
                for cp in arrivals:
                    cp.wait_recv()
                for cp in sends:
                    cp.wait_send()

    sems = [pltpu.SemaphoreType.DMA((N_CHIPS - 1,))] * (2 * n_ride)
    any_spec = pl.BlockSpec(memory_space=pl.ANY)
    res = pl.pallas_call(
        body, name=name, grid=(ni, nj, nk),
        in_specs=[a_spec, b_spec] + [any_spec] * n_ride,
        out_specs=[pl.BlockSpec((tm, tn), lambda i, j, k: (i, j))] + [any_spec] * n_ride,
        out_shape=[jax.ShapeDtypeStruct((M, N), out_dtype)] + [_exchange_shape(kind, x) for kind, x in rides],
        scratch_shapes=sems + ([] if nk == 1 else [pltpu.VMEM((tm, tn), F32)]),
        compiler_params=pltpu.CompilerParams(
            dimension_semantics=("arbitrary",) * 3 if rides else ("parallel", "parallel", "arbitrary")),
    )(a, b, *[x for _, x in rides])
    return tuple(res) if rides else res[0]


def _small_f(z, pv, av):
    lane = lax.broadcasted_iota(jnp.int32, z.shape, 1)
    zz = z + pv
    logf = _log_sigmoid(zz)
    gval = -jnp.exp(av) * _softplus(zz)
    beta = _sigmoid(zz)
    return jnp.where(lane < 8, logf, jnp.where(lane < 16, gval, jnp.where(lane < 24, beta, 0.0)))


def _tri(ts, upper):
    r = lax.broadcasted_iota(jnp.int32, (ts, ts), 0)
    c = lax.broadcasted_iota(jnp.int32, (ts, ts), 1)
    keep = (r <= c) if upper else (r >= c)
    same_chunk = (r // CHUNK) == (c // CHUNK)
    return keep.astype(F32), (keep & same_chunk).astype(F32)


def smalls_fwd(p_all, pv, av, S, ts):
    nt = S // ts

    def body(z_ref, pv_ref, av_ref, o_ref, carry):
        i = pl.program_id(0)

        @pl.when(i == 0)
        def _():
            carry[...] = jnp.zeros_like(carry)

        e = _small_f(z_ref[...], pv_ref[...], av_ref[...])
        lane = lax.broadcasted_iota(jnp.int32, e.shape, 1)
        l_all, l_chunk = _tri(ts, upper=False)
        cum_all = hdot(l_all, e) + carry[...]
        cum_chunk = hdot(l_chunk, e)
        carry[...] = cum_all[ts - 1:ts, :]
        o_ref[...] = jnp.where(lane < 8, cum_all, jnp.where(lane < 16, cum_chunk, e))

    return pl.pallas_call(
        body, name="smalls_fwd", grid=(nt,),
        in_specs=[pl.BlockSpec((ts, 128), lambda i: (i, CB_SMALL)), pl.BlockSpec((1, 128), lambda i: (0, 0)),
                  pl.BlockSpec((1, 128), lambda i: (0, 0))],
        out_specs=pl.BlockSpec((ts, 128), lambda i: (i, 0)),
        out_shape=jax.ShapeDtypeStruct((S, 128), F32),
        scratch_shapes=[pltpu.VMEM((1, 128), F32)],
    )(p_all, pv, av)


def smalls_bwd(p_all, pv, av, dsm, dp, S, ts):
    nt = S // ts

    def body(z_ref, pv_ref, av_ref, d_ref, dp_in, dz_ref, dpv_ref, dav_ref, carry):
        i = pl.program_id(0)

        @pl.when(i == 0)
        def _():
            carry[...] = jnp.zeros_like(carry)
            dpv_ref[...] = jnp.zeros_like(dpv_ref)
            dav_ref[...] = jnp.zeros_like(dav_ref)

        d = d_ref[...]
        lane = lax.broadcasted_iota(jnp.int32, d.shape, 1)
        u_all, u_chunk = _tri(ts, upper=True)
        rev_all = hdot(u_all, d) + carry[...]
        rev_chunk = hdot(u_chunk, d)
        carry[...] = rev_all[0:1, :]
        de = jnp.where(lane < 8, rev_all, jnp.where(lane < 16, rev_chunk, d))
        _, vjp = jax.vjp(_small_f, z_ref[...], pv_ref[...], av_ref[...])
        dz, dpv, dav = vjp(de)
        dz_ref[...] = dz.astype(dz_ref.dtype)
        dpv_ref[...] += dpv
        dav_ref[...] += dav

    rev = lambda i: (nt - 1 - i, 0)
    small_cols = pl.BlockSpec((ts, 128), lambda i: (nt - 1 - i, CB_SMALL))
    return pl.pallas_call(
        body, name="smalls_bwd", grid=(nt,),
        in_specs=[small_cols, pl.BlockSpec((1, 128), lambda i: (0, 0)), pl.BlockSpec((1, 128), lambda i: (0, 0)),
                  pl.BlockSpec((ts, 128), rev), pl.BlockSpec(memory_space=pl.ANY)],
        out_specs=[small_cols, pl.BlockSpec((1, 128), lambda i: (0, 0)), pl.BlockSpec((1, 128), lambda i: (0, 0))],
        out_shape=[jax.ShapeDtypeStruct(dp.shape, dp.dtype), jax.ShapeDtypeStruct((1, 128), F32),
                   jax.ShapeDtypeStruct((1, 128), F32)],
        scratch_shapes=[pltpu.VMEM((1, 128), F32)],
        input_output_aliases={4: 0},
    )(p_all, pv, av, dsm, dp)


def _col_to_row(col):
    return jnp.transpose(jnp.broadcast_to(col, (col.shape[0], 128)))[0:1, :]


def _row_to_col(row):
    return jnp.transpose(jnp.broadcast_to(row, (128, row.shape[1])))[:, 0:1]


def _causal_mask(s, keys_first):
    r = lax.broadcasted_iota(jnp.int32, s.shape, 0)
    c = lax.broadcasted_iota(jnp.int32, s.shape, 1)
    return jnp.where((r <= c) if keys_first else (c <= r), s, NEG)


def flash_fwd(qs, kn, vb, frow, S, tq):
    nq = S // tq
    tk = tq

    def body(q_ref, k_ref, v_ref, fr_ref, o_ref, lse_ref):
        i = pl.program_id(1)
        q = q_ref[...]
        f_base = fr_ref[0, i][:, 0:1]

        def tile(j, carry, diagonal):
            m, l, acc = carry
            off = pl.multiple_of(j * tk, tk)
            k = k_ref[pl.ds(off, tk), :]
            v = v_ref[pl.ds(off, tk), :]
            s = mm_nt(q, k) - (fr_ref[0, j] - f_base)
            if diagonal:
                s = _causal_mask(s, keys_first=False)
            m_new = jnp.maximum(m, jnp.max(s, axis=-1, keepdims=True))
            a = jnp.exp(m - m_new)
            p = jnp.exp(s - m_new)
            l = a * l + jnp.sum(p, axis=-1, keepdims=True)
            acc = a * acc + mm_nn(p, v)
            return m_new, l, acc

        init = (jnp.full((tq, 1), NEG, F32), jnp.zeros((tq, 1), F32), jnp.zeros((tq, D_HEAD), F32))
        carry = lax.fori_loop(0, i, lambda j, c: tile(j, c, False), init)
        m, l, acc = tile(i, carry, True)
        o_ref[...] = acc / l
        lse_ref[0, 0] = _col_to_row(m + jnp.log(l))

    return pl.pallas_call(
        body, name="flash_fwd", grid=(N_HEADS, nq),
        in_specs=[pl.BlockSpec((tq, D_HEAD), lambda h, i: (i, h)), pl.BlockSpec((S, D_HEAD), lambda h, i: (0, h)),
                  pl.BlockSpec((S, D_HEAD), lambda h, i: (0, h)),
                  pl.BlockSpec((1, nq, 1, tk), lambda h, i: (h, 0, 0, 0))],
        out_specs=[pl.BlockSpec((tq, D_HEAD), lambda h, i: (i, h)),
                   pl.BlockSpec((1, 1, 1, tq), lambda h, i: (h, i, 0, 0))],
        out_shape=[jax.ShapeDtypeStruct((S, WIDTH), F32), jax.ShapeDtypeStruct((N_HEADS, nq, 1, tq), F32)],
    )(qs, kn, vb, frow)


def flash_delta(do, o, S, tq):
    nq = S // tq
    per = _tile(nq, 4)

    def body(do_ref, o_ref, dl_ref):
        for n in range(per):
            rows = pl.ds(n * tq, tq)
            dl_ref[0, n] = _col_to_row(jnp.sum(do_ref[rows, :].astype(F32) * o_ref[rows, :], axis=-1, keepdims=True))

    blk = pl.BlockSpec((per * tq, D_HEAD), lambda h, i: (i, h))
    return pl.pallas_call(
        body, name="flash_delta", grid=(N_HEADS, nq // per), in_specs=[blk, blk],
        out_specs=pl.BlockSpec((1, per, 1, tq), lambda h, i: (h, i, 0, 0)),
        out_shape=jax.ShapeDtypeStruct((N_HEADS, nq, 1, tq), F32),
    )(do, o)


def flash_bwd(qs, kn, vb, do, frow, lse_row, delta_row, dp, S, tq):
    nq = S // tq
    tk = tq

    def body(q_ref, k_ref, v_ref, do_ref, fr_ref, lse_ref, dl_ref, dp_in, dq_ref, dk_ref, dv_ref, dfq_ref, dfk_ref):
        j = pl.program_id(1)
        k = k_ref[...]
        v = v_ref[...]
        fk = _row_to_col(fr_ref[0, j])

        @pl.when(j == 0)
        def _():
            dq_ref[...] = jnp.zeros_like(dq_ref)
            dfq_ref[...] = jnp.zeros_like(dfq_ref)

        def tile(i, carry, diagonal):
            dk, dv, dfk = carry
            off = pl.multiple_of(i * tq, tq)
            q = q_ref[pl.ds(off, tq), :]
            do_ = do_ref[pl.ds(off, tq), :]
            st = mm_nt(k, q) - (fk - fr_ref[0, i][:, 0:1])
            if diagonal:
                st = _causal_mask(st, keys_first=True)
            pt = jnp.exp(st - lse_ref[0, i])
            dst = pt * (mm_nt(v, do_) - dl_ref[0, i])
            dq_ref[pl.ds(off, tq), :] += mm_tn(dst, k) * ATT_SCALE
            dfq_ref[0, i] += jnp.sum(dst, axis=0, keepdims=True)
            return dk + mm_nn(dst, q), dv + mm_nn(pt, do_), dfk - jnp.sum(dst, axis=-1, keepdims=True)

        z = jnp.zeros((tk, D_HEAD), F32)
        carry = tile(j, (z, z, jnp.zeros((tk, 1), F32)), True)
        dk, dv, dfk = lax.fori_loop(j + 1, nq, lambda i, c: tile(i, c, False), carry)
        dk_ref[...] = dk
        dv_ref[...] = dv.astype(dv_ref.dtype)
        dfk_ref[0, 0] = _col_to_row(dfk)

    blk = pl.BlockSpec((tk, D_HEAD), lambda h, j: (j, h))
    full = pl.BlockSpec((S, D_HEAD), lambda h, j: (0, h))
    rows = pl.BlockSpec((1, nq, 1, tq), lambda h, j: (h, 0, 0, 0))
    stat = jax.ShapeDtypeStruct((N_HEADS, nq, 1, tq), F32)
    return pl.pallas_call(
        body, name="flash_bwd", grid=(N_HEADS, nq),
        in_specs=[full, blk, blk, full, rows, rows, rows, pl.BlockSpec(memory_space=pl.ANY)],
        out_specs=[full, blk, pl.BlockSpec((tk, D_HEAD), lambda h, j: (j, CB_FV + h)), rows,
                   pl.BlockSpec((1, 1, 1, tk), lambda h, j: (h, j, 0, 0))],
        out_shape=[jax.ShapeDtypeStruct((S, WIDTH), F32), jax.ShapeDtypeStruct((S, WIDTH), F32),
                   jax.ShapeDtypeStruct(dp.shape, dp.dtype), stat, stat],
        input_output_aliases={7: 2},
    )(qs, kn, vb, do, frow, lse_row, delta_row, dp)


CONV_COLS = 3 * WIDTH
CONV_TC = 512


def _gdn_act(y, group):
    s = _silu(y)
    if group == 2:
        return s
    return _l2n(s) * ATT_SCALE if group == 0 else _l2n(s)


def _gdn_act_bwd(y, d, group):
    _, vjp = jax.vjp(functools.partial(_gdn_act, group=group), y)
    return vjp(d)[0]


def conv_act_fwd(p_all, conv_w, S, ts):
    nt, tc = S // ts, CONV_TC
    cb0 = CB_GQ * 128 // tc
    r8 = ts // 8
    tiles_per_group = WIDTH // tc

    def body(x_ref, halo_ref, w_ref, y_ref, a_ref, scr):
        c, i = pl.program_id(0), pl.program_id(1)
        scr[0:8, :] = jnp.where(i > 0, halo_ref[...], 0.0)
        scr[8:8 + ts, :] = x_ref[...]
        w = w_ref[...]
        y = w[3:4, :] * x_ref[...]
        for j in range(CONV_K - 1):
            y = y + w[j:j + 1, :] * scr[5 + j:5 + j + ts, :]
        y_ref[...] = y
        for group in range(3):
            @pl.when(c // tiles_per_group == group)
            def _(group=group):
                for h in range(tc // D_HEAD):
                    sl = slice(h * D_HEAD, (h + 1) * D_HEAD)
                    a_ref[:, sl] = _gdn_act(y_ref[:, sl], group)

    out = pl.BlockSpec((ts, tc), lambda c, i: (i, c))
    return pl.pallas_call(
        body, name="conv_act_fwd", grid=(CONV_COLS // tc, nt),
        in_specs=[pl.BlockSpec((ts, tc), lambda c, i: (i, cb0 + c)),
                  pl.BlockSpec((8, tc), lambda c, i: (jnp.maximum(i * r8 - 1, 0), cb0 + c)),
                  pl.BlockSpec((CONV_K, tc), lambda c, i: (0, c))],
        out_specs=[out, out],
        out_shape=[jax.ShapeDtypeStruct((S, CONV_COLS), F32)] * 2,
        scratch_shapes=[pltpu.VMEM((ts + 8, tc), F32)],
    )(p_all, p_all, conv_w)


def conv_act_bwd(p_all, y, dqkv, conv_w, dp, S, ts):
    nt, tc = S // ts, CONV_TC
    cb0 = CB_GQ * 128 // tc
    r8 = ts // 8
    tiles_per_group = WIDTH // tc

    def body(x_ref, xh_ref, y_ref, yh_ref, d_ref, dh_ref, w_ref, dp_in, dx_ref, dw_ref, xs, ds):
        c, i = pl.program_id(0), pl.program_id(1)
        xs[0:8, :] = jnp.where(i > 0, xh_ref[...], 0.0)
        xs[8:8 + ts, :] = x_ref[...]
        for group in range(3):
            @pl.when(c // tiles_per_group == group)
            def _(group=group):
                for h in range(tc // D_HEAD):
                    sl = slice(h * D_HEAD, (h + 1) * D_HEAD)
                    ds[0:ts, sl] = _gdn_act_bwd(y_ref[:, sl], d_ref[:, sl], group)
                    ds[ts:ts + 8, sl] = jnp.where(i < nt - 1, _gdn_act_bwd(yh_ref[:, sl], dh_ref[:, sl], group), 0.0)
        w = w_ref[...]
        d = ds[0:ts, :]
        dx = w[3:4, :] * d
        for j in range(CONV_K - 1):
            dx = dx + w[j:j + 1, :] * ds[3 - j:3 - j + ts, :]
        dx_ref[...] = dx.astype(dx_ref.dtype)

        @pl.when(i == 0)
        def _():
            dw_ref[...] = jnp.zeros_like(dw_ref)

        rows = [jnp.sum(d * xs[5 + j:5 + j + ts, :], axis=0, keepdims=True) for j in range(CONV_K - 1)]
        rows.append(jnp.sum(d * x_ref[...], axis=0, keepdims=True))
        dw_ref[...] += jnp.concatenate(rows, axis=0)

    tile = pl.BlockSpec((ts, tc), lambda c, i: (i, c))
    next8 = pl.BlockSpec((8, tc), lambda c, i: (jnp.minimum((i + 1) * r8, nt * r8 - 1), c))
    return pl.pallas_call(
        body, name="conv_act_bwd", grid=(CONV_COLS // tc, nt),
        in_specs=[pl.BlockSpec((ts, tc), lambda c, i: (i, cb0 + c)),
                  pl.BlockSpec((8, tc), lambda c, i: (jnp.maximum(i * r8 - 1, 0), cb0 + c)),
                  tile, next8, tile, next8,
                  pl.BlockSpec((CONV_K, tc), lambda c, i: (0, c)), pl.BlockSpec(memory_space=pl.ANY)],
        out_specs=[pl.BlockSpec((ts, tc), lambda c, i: (i, cb0 + c)), pl.BlockSpec((CONV_K, tc), lambda c, i: (0, c))],
        out_shape=[jax.ShapeDtypeStruct(dp.shape, dp.dtype), jax.ShapeDtypeStruct((CONV_K, CONV_COLS), F32)],
        scratch_shapes=[pltpu.VMEM((ts + 8, tc), F32), pltpu.VMEM((ts + 8, tc), F32)],
        input_output_aliases={7: 0},
    )(p_all, p_all, y, y, dqkv, dqkv, conv_w, dp)


def _bdot(a, b, ca, cb):
    return lax.dot_general(a.astype(MXU_DTYPE), b.astype(MXU_DTYPE), (((ca,), (cb,)), ((0,), (0,))),
                           preferred_element_type=F32)


@jax.custom_vjp
def bmm_nn(a, b):
    return _bdot(a, b, 2, 1)


@jax.custom_vjp
def bmm_nt(a, b):
    return _bdot(a, b, 2, 2)


@jax.custom_vjp
def bmm_tn(a, b):
    return _bdot(a, b, 1, 1)


bmm_nn.defvjp(lambda a, b: (bmm_nn(a, b), (a, b)), lambda r, g: (bmm_nt(g, r[1]), bmm_tn(r[0], g)))
bmm_nt.defvjp(lambda a, b: (bmm_nt(a, b), (a, b)), lambda r, g: (bmm_nn(g, r[1]), bmm_tn(g, r[0])))
bmm_tn.defvjp(lambda a, b: (bmm_tn(a, b), (a, b)), lambda r, g: (bmm_nt(r[1], g), bmm_nn(r[0], g)))


def bdot_hi(a, b, ca=2, cb=1):
    return lax.dot_general(a, b, (((ca,), (cb,)), ((0,), (0,))), precision=SOLVE_PRECISION,
                           preferred_element_type=F32)


def _tri_inv_product(m):
    ii = lax.broadcasted_iota(jnp.int32, m.shape, 1)
    jj = lax.broadcasted_iota(jnp.int32, m.shape, 2)
    t = (ii == jj).astype(F32) - m
    pw = bdot_hi(m, m)
    for _ in range(4):
        t = t + bdot_hi(t, pw)
        pw = bdot_hi(pw, pw)
    return t + bdot_hi(t, pw)


def _tri_inv_bwd(t, g):
    return -bdot_hi(bdot_hi(t, g, 1, 1), t, 2, 2)


@jax.custom_vjp
def tri_inv(m):
    return _tri_inv_product(m)


@jax.custom_vjp
def tri_inv_known(m, t):
    return t


tri_inv.defvjp(lambda m: (lambda t: (t, t))(_tri_inv_product(m)), lambda t, g: (_tri_inv_bwd(t, g),))
tri_inv_known.defvjp(lambda m, t: (t, t), lambda t, g: (_tri_inv_bwd(t, g), jnp.zeros_like(t)))


def chunk_fn(q, k, v, gc_col, gc_row, beta, state, t_known=None):
    shape = (N_HEADS, CHUNK, CHUNK)
    ii = lax.broadcasted_iota(jnp.int32, shape, 1)
    jj = lax.broadcasted_iota(jnp.int32, shape, 2)
    lower = ii >= jj
    strict = ii > jj
    decay = jnp.exp(jnp.where(lower, gc_col - gc_row, NEG))
    kb = k * beta
    vb = v * beta
    m = jnp.where(strict, bmm_nt(kb, k) * decay, 0.0)
    t = tri_inv(m) if t_known is None else tri_inv_known(m, t_known)
    u = bdot_hi(t, vb)
    w = bdot_hi(t, kb * jnp.exp(gc_col))
    attn = jnp.where(lower, bmm_nt(q, k) * decay, 0.0)
    v_new = u - bmm_nn(w, state)
    o = bmm_nn(q * jnp.exp(gc_col), state) + bmm_nn(attn, v_new)
    g_last = gc_col[:, CHUNK - 1:CHUNK, :]
    k_dec = k * jnp.exp(g_last - gc_col)
    new_state = state * jnp.exp(g_last) + bmm_tn(k_dec, v_new)
    return (o, new_state, t) if t_known is None else (o, new_state)


GDN_CHUNKS_PER_STEP = 4


def _heads(ref, rows):
    return jnp.stack([ref[rows, h * D_HEAD:(h + 1) * D_HEAD] for h in range(N_HEADS)], axis=0)


def _head_cols(ref, rows, lane0):
    return jnp.stack([ref[rows, lane0 + h:lane0 + h + 1] for h in range(N_HEADS)], axis=0)


def gdn_fwd(qkv, smalls, gc_row, S):
    n = S // CHUNK
    per = _tile(n, GDN_CHUNKS_PER_STEP)
    rows_per = per * CHUNK

    def body(q_ref, k_ref, v_ref, sm_ref, gr_ref, o_ref, st_ref, t_ref, state):
        @pl.when(pl.program_id(0) == 0)
        def _():
            state[...] = jnp.zeros_like(state)

        s = state[...]
        for b in range(per):
            rows = pl.ds(b * CHUNK, CHUNK)
            st_ref[b] = s
            o, s, t = chunk_fn(_heads(q_ref, rows), _heads(k_ref, rows), _heads(v_ref, rows),
                               _head_cols(sm_ref, rows, 8), gr_ref[b][:, None, :], _head_cols(sm_ref, rows, 16), s)
            for h in range(N_HEADS):
                o_ref[rows, h * D_HEAD:(h + 1) * D_HEAD] = o[h]
            t_ref[b] = t
        state[...] = s

    return pl.pallas_call(
        body, name="gdn_chunk_fwd", grid=(n // per,),
        in_specs=[pl.BlockSpec((rows_per, WIDTH), lambda i, g=g: (i, g)) for g in range(3)]
        + [pl.BlockSpec((rows_per, 128), lambda i: (i, 0)), pl.BlockSpec((per, N_HEADS, CHUNK), lambda i: (i, 0, 0))],
        out_specs=[pl.BlockSpec((rows_per, WIDTH), lambda i: (i, 0)),
                   pl.BlockSpec((per, N_HEADS, D_HEAD, D_HEAD), lambda i: (i, 0, 0, 0)),
                   pl.BlockSpec((per, N_HEADS, CHUNK, CHUNK), lambda i: (i, 0, 0, 0))],
        out_shape=[jax.ShapeDtypeStruct((S, WIDTH), F32), jax.ShapeDtypeStruct((n, N_HEADS, D_HEAD, D_HEAD), F32),
                   jax.ShapeDtypeStruct((n, N_HEADS, CHUNK, CHUNK), F32)],
        scratch_shapes=[pltpu.VMEM((N_HEADS, D_HEAD, D_HEAD), F32)],
    )(qkv, qkv, qkv, smalls, gc_row)


def gdn_bwd(qkv, smalls, gc_row, states, tinv, do, S):
    n = S // CHUNK
    per = _tile(n, GDN_CHUNKS_PER_STEP)
    rows_per = per * CHUNK
    nb = n // per

    def body(q_ref, k_ref, v_ref, sm_ref, gr_ref, st_ref, t_ref, do_ref, dqkv_ref, dsm_ref, dgr_ref, dstate):
        @pl.when(pl.program_id(0) == 0)
        def _():
            dstate[...] = jnp.zeros_like(dstate)

        ds = dstate[...]
        for b in reversed(range(per)):
            rows = pl.ds(b * CHUNK, CHUNK)
            t_saved = t_ref[b]
            _, vjp = jax.vjp(lambda *a: chunk_fn(*a, t_known=t_saved), _heads(q_ref, rows), _heads(k_ref, rows),
                             _heads(v_ref, rows), _head_cols(sm_ref, rows, 8), gr_ref[b][:, None, :],
                             _head_cols(sm_ref, rows, 16), st_ref[b])
            dq, dk, dv, dgc, dgr, dbt, ds = vjp((_heads(do_ref, rows), ds))
            for h in range(N_HEADS):
                dqkv_ref[rows, h * D_HEAD:(h + 1) * D_HEAD] = dq[h]
                dqkv_ref[rows, WIDTH + h * D_HEAD:WIDTH + (h + 1) * D_HEAD] = dk[h]
                dqkv_ref[rows, 2 * WIDTH + h * D_HEAD:2 * WIDTH + (h + 1) * D_HEAD] = dv[h]
            cols = [dgc[h] for h in range(N_HEADS)] + [dbt[h] for h in range(N_HEADS)]
            dsm_ref[rows, :] = jnp.concatenate([jnp.zeros((CHUNK, 8), F32)] + cols
                                               + [jnp.zeros((CHUNK, 128 - 24), F32)], axis=1)
            dgr_ref[b] = jnp.concatenate([dgr[h] for h in range(N_HEADS)], axis=0)
        dstate[...] = ds

    rev = lambda c: (lambda i: (nb - 1 - i, c))
    rev4 = lambda i: (nb - 1 - i, 0, 0, 0)
    return pl.pallas_call(
        body, name="gdn_chunk_bwd", grid=(nb,),
        in_specs=[pl.BlockSpec((rows_per, WIDTH), rev(g)) for g in range(3)]
        + [pl.BlockSpec((rows_per, 128), rev(0)), pl.BlockSpec((per, N_HEADS, CHUNK), lambda i: (nb - 1 - i, 0, 0)),
           pl.BlockSpec((per, N_HEADS, D_HEAD, D_HEAD), rev4), pl.BlockSpec((per, N_HEADS, CHUNK, CHUNK), rev4),
           pl.BlockSpec((rows_per, WIDTH), rev(0))],
        out_specs=[pl.BlockSpec((rows_per, 3 * WIDTH), rev(0)), pl.BlockSpec((rows_per, 128), rev(0)),
                   pl.BlockSpec((per, N_HEADS, CHUNK), lambda i: (nb - 1 - i, 0, 0))],
        out_shape=[jax.ShapeDtypeStruct((S, 3 * WIDTH), F32)] + [jax.ShapeDtypeStruct((S, 128), F32),
                                                                  jax.ShapeDtypeStruct((n, N_HEADS, CHUNK), F32)],
        scratch_shapes=[pltpu.VMEM((N_HEADS, D_HEAD, D_HEAD), F32)],
    )(qkv, qkv, qkv, smalls, gc_row, states, tinv, do)


def _flip(a, bit):
    return 1 - a if bit else a


def allgather8(name, buf):
    R, W = buf.shape

    def body(x_ref, out_ref, send_sems, recv_sems, local_sem):
        x, y, c = lax.axis_index("x"), lax.axis_index("y"), lax.axis_index("c")

        def slot(px, py, pc):
            return out_ref.at[4 * px + 2 * py + pc]

        mine = pltpu.make_async_copy(x_ref, slot(x, y, c), local_sem)
        mine.start()
        sends = []
        for r in range(1, N_DEV):
            peer = (_flip(x, r & 4), _flip(y, r & 2), _flip(c, r & 1))
            cp = pltpu.make_async_remote_copy(src_ref=x_ref, dst_ref=slot(x, y, c), send_sem=send_sems.at[r - 1],
                                              recv_sem=recv_sems.at[r - 1], device_id=peer, device_id_type=MESH_ID)
            cp.start()
            sends.append(cp)
        for r in range(1, N_DEV):
            peer = (_flip(x, r & 4), _flip(y, r & 2), _flip(c, r & 1))
            pltpu.make_async_remote_copy(src_ref=x_ref, dst_ref=slot(*peer), send_sem=send_sems.at[r - 1],
                                         recv_sem=recv_sems.at[r - 1], device_id=peer,
                                         device_id_type=MESH_ID).wait_recv()
        for cp in sends:
            cp.wait_send()
        mine.wait()

    return pl.pallas_call(
        body, name=name,
        out_shape=jax.ShapeDtypeStruct((N_DEV, R, W), buf.dtype),
        in_specs=[pl.BlockSpec(memory_space=pltpu.VMEM)],
        out_specs=pl.BlockSpec(memory_space=pltpu.VMEM),
        scratch_shapes=[pltpu.SemaphoreType.DMA((N_DEV - 1,)), pltpu.SemaphoreType.DMA((N_DEV - 1,)),
                        pltpu.SemaphoreType.DMA],
    )(buf)


def allgather_chips(name, blk):
    R, W = blk.shape
    half = R // 2
    n_far = N_CHIPS - 1

    def body(x_ref, out_ref, send_sems, recv_sems):
        x, y, c = lax.axis_index("x"), lax.axis_index("y"), lax.axis_index("c")
        sibling = (x, y, 1 - c)
        chips = [(_flip(x, r & 2), _flip(y, r & 1)) for r in range(1, N_CHIPS)]

        def rows(px, py, pc):
            return out_ref.at[2 * px + py, pc]

        def copy(k, dst, to, src):
            return pltpu.make_async_remote_copy(src_ref=src, dst_ref=dst, send_sem=send_sems.at[k],
                                                recv_sem=recv_sems.at[k], device_id=to, device_id_type=MESH_ID)

        my_half = x_ref.at[c]
        first = [copy(j, rows(x, y, c), (*chip, c), my_half) for j, chip in enumerate(chips)]
        for cp in first:
            cp.start()
        passed = [copy(n_far + j, rows(*chip, c), sibling, rows(*chip, c)) for j, chip in enumerate(chips)]
        for j, chip in enumerate(chips):
            copy(j, rows(*chip, c), (*chip, c), my_half).wait_recv()
            passed[j].start()
        for j, chip in enumerate(chips):
            copy(n_far + j, rows(*chip, 1 - c), sibling, my_half).wait_recv()
        for cp in first + passed:
            cp.wait_send()

    far = pl.pallas_call(
        body, name=name,
        out_shape=jax.ShapeDtypeStruct((N_CHIPS, 2, half, W), blk.dtype),
        in_specs=[pl.BlockSpec(memory_space=pl.ANY)],
        out_specs=pl.BlockSpec(memory_space=pl.ANY),
        scratch_shapes=[pltpu.SemaphoreType.DMA((2 * n_far,)), pltpu.SemaphoreType.DMA((2 * n_far,))],
    )(blk.reshape(2, half, W)).reshape(N_CHIPS, R, W)
    own_chip = 2 * lax.axis_index("x") + lax.axis_index("y")
    return lax.dynamic_update_index_in_dim(far, blk, own_chip, axis=0)


def sibling_send_other_half(name, g4):
    n, R, W = g4.shape
    half = R // 2

    def body(x_ref, out_ref, send_sem, recv_sem):
        x, y, c = lax.axis_index("x"), lax.axis_index("y"), lax.axis_index("c")
        cp = pltpu.make_async_remote_copy(src_ref=x_ref.at[:, pl.ds((1 - c) * half, half), :], dst_ref=out_ref,
                                          send_sem=send_sem, recv_sem=recv_sem, device_id=(x, y, 1 - c),
                                          device_id_type=MESH_ID)
        cp.start()
        cp.wait_recv()
        cp.wait_send()

    return pl.pallas_call(
        body, name=name,
        out_shape=jax.ShapeDtypeStruct((n, half, W), g4.dtype),
        in_specs=[pl.BlockSpec(memory_space=pl.ANY)],
        out_specs=pl.BlockSpec(memory_space=pl.ANY),
        scratch_shapes=[pltpu.SemaphoreType.DMA, pltpu.SemaphoreType.DMA],
    )(g4)


def sibling_merge(name, mine):
    def body(x_ref, out_ref, send_sem, recv_sem):
        x, y, c = lax.axis_index("x"), lax.axis_index("y"), lax.axis_index("c")
        cp = pltpu.make_async_remote_copy(src_ref=x_ref, dst_ref=out_ref, send_sem=send_sem, recv_sem=recv_sem,
                                          device_id=(x, y, 1 - c), device_id_type=MESH_ID)
        cp.start()
        cp.wait_recv()
        cp.wait_send()

    other = pl.pallas_call(
        body, name=name,
        out_shape=jax.ShapeDtypeStruct(mine.shape, mine.dtype),
        in_specs=[pl.BlockSpec(memory_space=pl.ANY)],
        out_specs=pl.BlockSpec(memory_space=pl.ANY),
        scratch_shapes=[pltpu.SemaphoreType.DMA, pltpu.SemaphoreType.DMA],
    )(mine)
    first = lax.axis_index("c") == 0
    return jnp.concatenate([jnp.where(first, mine, other), jnp.where(first, other, mine)], axis=0)


def sum_parts(name, own, recv, tr):
    R, W = own.shape
    tr = _tile(R, tr)

    def body(o_ref, r_ref, out_ref):
        acc = o_ref[...]
        for k in range(recv.shape[0]):
            acc = acc + r_ref[k].astype(F32)
        out_ref[...] = acc

    return pl.pallas_call(
        body, name=name, grid=(R // tr,),
        in_specs=[pl.BlockSpec((tr, W), lambda i: (i, 0)), pl.BlockSpec((recv.shape[0], tr, W), lambda i: (0, i, 0))],
        out_specs=pl.BlockSpec((tr, W), lambda i: (i, 0)),
        out_shape=jax.ShapeDtypeStruct((R, W), F32),
    )(own, recv)


def sum_own_half(name, g4, recv, c_idx, tr, rounded=True):
    n, R, W = g4.shape
    half = R // 2
    tr = _tile(half, tr)
    nb = half // tr

    def body(c_ref, g_ref, r_ref, o_ref, *ob_ref):
        s = g_ref[0] + r_ref[0]
        o_ref[0] = s
        if rounded:
            ob_ref[0][0] = s.astype(ob_ref[0].dtype)

    mine = pl.BlockSpec((1, tr, W), lambda j, i, c: (j, i, 0))
    shapes = [jax.ShapeDtypeStruct((n, half, W), F32)] + ([jax.ShapeDtypeStruct((n, half, W), MXU_DTYPE)] * rounded)
    return pl.pallas_call(
        body, name=name,
        grid_spec=pltpu.PrefetchScalarGridSpec(
            num_scalar_prefetch=1, grid=(n, nb),
            in_specs=[pl.BlockSpec((1, tr, W), lambda j, i, c: (j, c[0] * nb + i, 0)), mine],
            out_specs=[mine] * len(shapes)),
        out_shape=shapes,
    )(c_idx, g4, recv)


def sum_devices(name, g):
    def body(g_ref, o_ref):
        acc = g_ref[0]
        for d in range(1, N_DEV):
            acc = acc + g_ref[d]
        o_ref[...] = acc

    return pl.pallas_call(body, name=name, out_shape=jax.ShapeDtypeStruct(g.shape[1:], F32))(g)


def ada_fwd(c_all, w_ada):
    K, N = w_ada.shape
    tn = _tile(N, 512)

    def body(c_ref, w_ref, o_ref):
        o_ref[...] = mm_nn(_silu(c_ref[...]), w_ref[...])

    return pl.pallas_call(
        body, name="ada_fwd", grid=(N // tn,),
        in_specs=[pl.BlockSpec((N_DEV, K), lambda j: (0, 0)), pl.BlockSpec((K, tn), lambda j: (0, j))],
        out_specs=pl.BlockSpec((N_DEV, tn), lambda j: (0, j)),
        out_shape=jax.ShapeDtypeStruct((N_DEV, N), F32),
    )(c_all, w_ada)


def ada_grad(c_all, dmod):
    K = c_all.shape[1]
    N = dmod.shape[1]
    tn = _tile(N, 512)

    def body(c_ref, d_ref, o_ref):
        o_ref[...] = mm_tn(_silu(c_ref[...]), d_ref[...])

    return pl.pallas_call(
        body, name="ada_grad", grid=(N // tn,),
        in_specs=[pl.BlockSpec((N_DEV, K), lambda j: (0, 0)), pl.BlockSpec((N_DEV, tn), lambda j: (0, j))],
        out_specs=pl.BlockSpec((K, tn), lambda j: (0, j)),
        out_shape=jax.ShapeDtypeStruct((K, N), F32),
    )(c_all, dmod)


def adamw(name, w, gparts, m, v, tr, tc=None):
    R, W = w.shape
    tr = R if tc else _tile(R, tr)
    ng = len(gparts)

    def body(w_ref, *refs):
        g_refs, (m_ref, v_ref, g_out, d_out, m_out, v_out) = refs[:ng], refs[ng:]
        g = g_refs[0][...]
        for r in g_refs[1:]:
            g = g + r[...]
        g_out[...] = g
        d_out[...], m_out[...], v_out[...] = _adamw_update(w_ref[...], g, m_ref[...], v_ref[...])

    spec = pl.BlockSpec((R, tc), lambda i: (0, i)) if tc else pl.BlockSpec((tr, W), lambda i: (i, 0))
    return pl.pallas_call(
        body, name=name, grid=(W // tc if tc else R // tr,),
        in_specs=[spec] * (3 + ng), out_specs=[spec] * 4,
        out_shape=[jax.ShapeDtypeStruct((R, W), F32)] * 4,
    )(w, *gparts, m, v)


def _adamw_update(w, g, m, v):
    m1 = ADAM_B1 * m + (1.0 - ADAM_B1) * g
    v1 = ADAM_B2 * v + (1.0 - ADAM_B2) * jnp.square(g)
    m_hat = m1 / (1.0 - ADAM_B1 ** ADAM_STEP)
    v_hat = v1 / (1.0 - ADAM_B2 ** ADAM_STEP)
    return -ADAM_LR * (m_hat / (jnp.sqrt(v_hat) + ADAM_EPS) + ADAM_WD * w), m1, v1


def adamw_small(name, ws, gs, ms, vs):
    n = len(ws)

    def body(*refs):
        for k in range(n):
            w_ref, g_ref, m_ref, v_ref = (refs[j * n + k] for j in range(4))
            outs = refs[4 * n + 3 * k:4 * n + 3 * k + 3]
            for o_ref, val in zip(outs, _adamw_update(w_ref[...], g_ref[...], m_ref[...], v_ref[...])):
                o_ref[...] = val

    return pl.pallas_call(
        body, name=name, out_shape=[jax.ShapeDtypeStruct(w.shape, F32) for w in ws for _ in range(3)],
    )(*ws, *gs, *ms, *vs)


_REF_SEGMENTS = ((0, 4 * WIDTH, 0), (4 * WIDTH, 4 * WIDTH + 8, 4 * WIDTH), (4 * WIDTH + 8, 8 * WIDTH + 8, -8),
                 (8 * WIDTH + 8, IN_WIDTH, 0))


def _internal_from_blocks(blocks, n):
    a = 4 * WIDTH
    pieces = []
    for lo, hi in ((0, a), (a + 8, 2 * a + 8), (a, a + 8), (2 * a + 8, IN_WIDTH)):
        for j in range(N_CHIPS):
            s, e = max(lo, j * n), min(hi, (j + 1) * n)
            if s < e:
                pieces.append(blocks[j][:, s - j * n:e - j * n])
    pieces.append(jnp.zeros(blocks.shape[1:2] + (128 - N_SMALL,), blocks.dtype))
    return jnp.concatenate(pieces, axis=1)


def _block_from_internal(g, j, n):
    pieces = []
    for lo, hi, shift in _REF_SEGMENTS:
        s, e = max(lo, j * n), min(hi, (j + 1) * n)
        if s < e:
            pieces.append(g[:, s + shift:e + shift])
    return jnp.concatenate(pieces, axis=1)


def _pad_lanes(v, n=128):
    return jnp.pad(v, ((0, 0), (0, n - v.shape[1])))


def kernel(x, c, norm_g, w_ada, b_ada, w_in, b_fgate, fox_qn_g, fox_kn_g, gdn_conv_w, gdn_A_log, gdn_dt_bias, gdn_norm_g, w_out, final_g, loss_target, m_norm_g, m_w_ada, m_b_ada, m_w_in, m_b_fgate, m_fox_qn_g, m_fox_kn_g, m_gdn_conv_w, m_gdn_A_log, m_gdn_dt_bias, m_gdn_norm_g, m_w_out, m_final_g, v_norm_g, v_w_ada, v_b_ada, v_w_in, v_b_fgate, v_fox_qn_g, v_fox_kn_g, v_gdn_conv_w, v_gdn_A_log, v_gdn_dt_bias, v_gdn_norm_g, v_w_out, v_final_g):
    S = x.shape[1]
    H = N_HEADS
    ix, iy, ic = lax.axis_index("x"), lax.axis_index("y"), lax.axis_index("c")
    chip = 2 * ix + iy
    me = 2 * chip + ic
    x2, tgt = x[0], loss_target[0]
    ts_wide = _tile(S, 256)
    ts = _tile(S, 512)
    ts_head = _tile(S, 2048)
    tq = _tile(S, 1024)
    nq = S // tq
    nch = S // CHUNK
    conv_cols_chip = CONV_COLS // N_CHIPS

    pay = jnp.concatenate([c, _pad_lanes(gdn_conv_w[0], D_MODEL), jnp.zeros((3, D_MODEL), F32)], axis=0)
    g1 = allgather8("ag_c", pay)
    c_all = g1[:, 0, :]
    conv_w = jnp.concatenate([g1[2 * j, 1:1 + CONV_K, :conv_cols_chip] for j in range(N_CHIPS)], axis=1)
    g2 = allgather8("ag_mod", ada_fwd(c_all, w_ada[0]))
    mod_rows = jnp.concatenate([g2[2 * j] for j in range(N_CHIPS)], axis=1)
    mod = lax.dynamic_slice_in_dim(mod_rows, me, 1, axis=0) + b_ada
    shift, scale, gate = mod[:, :D_MODEL], mod[:, D_MODEL:2 * D_MODEL], mod[:, 2 * D_MODEL:]

    n_in = w_in.shape[2]
    win4 = allgather_chips("ag_w_in", w_in[0].astype(MXU_DTYPE))
    w_all = _internal_from_blocks(win4, n_in)
    w_out_b = w_out[0].astype(MXU_DTYPE)

    (h_b,) = rowwise_fwd("prologue", f_prologue, S, ts_wide, 1, [(x2, "row", D_MODEL, 0)], [norm_g, scale, shift],
                         [("row", D_MODEL, 0, D_MODEL, MXU_DTYPE)])
    p_all, wout_far = matmul("in_proj", h_b, w_all, F32, tm=1024, tn=1664, tk=2048, rides=[("gather", w_out_b)])
    w_out_full = lax.dynamic_update_index_in_dim(wout_far, w_out_b, chip, axis=0).reshape(2 * WIDTH, D_MODEL)
    pv = jnp.concatenate([b_fgate, gdn_dt_bias, jnp.zeros((1, 128 - 16), F32)], axis=1)
    av = jnp.concatenate([jnp.zeros((1, 8), F32), gdn_A_log, jnp.zeros((1, 128 - 16), F32)], axis=1)
    smalls = smalls_fwd(p_all, pv, av, S, ts)
    frow = jnp.transpose(smalls[:, :H]).reshape(H, nq, 1, tq)
    gc_row = jnp.transpose(smalls[:, 8:16].reshape(nch, CHUNK, H), (0, 2, 1))
    head_b = ("head", D_HEAD, 0, WIDTH, MXU_DTYPE)
    head_f = ("head", D_HEAD, 0, WIDTH, F32)
    pcol = lambda cb: (p_all, "head", D_HEAD, cb)
    qn, kn, vb = rowwise_fwd("fox_pre", f_foxpre, S, ts_head, H,[pcol(CB_FQ), pcol(CB_FK), pcol(CB_FV)],
                             [fox_qn_g, fox_kn_g], [head_b] * 3)
    o_fox, lse = flash_fwd(qn, kn, vb, frow, S, tq)
    y_conv, qkv = conv_act_fwd(p_all, conv_w, S, ts)
    o_gdn, states, tinv = gdn_fwd(qkv, smalls, gc_row, S)
    mixed = lax.empty((S, 2 * WIDTH), MXU_DTYPE)
    (mixed,) = rowwise_fwd("post_fox", f_postf, S, ts_head, H,[(o_fox, "head", D_HEAD, 0), pcol(CB_FZ)], [],
                           [("head", D_HEAD, 0, 2 * WIDTH, MXU_DTYPE, mixed)])
    (mixed,) = rowwise_fwd("post_gdn", f_postg, S, ts_head, H,[(o_gdn, "head", D_HEAD, 0), pcol(CB_GZ)], [gdn_norm_g],
                           [("head", D_HEAD, H, 2 * WIDTH, MXU_DTYPE, mixed)])
    mo = matmul("out_proj", mixed, w_out_full, F32, tm=1024, tn=2048, tk=2048)

    wide = lambda a: (a, "row", D_MODEL, 0)
    dx_res, dmo, d_gate, d_final_g, loss_acc = rowwise_bwd(
        "loss", f_loss, S, ts_wide, 1, [wide(x2), wide(mo), wide(tgt)], [gate, final_g[None, :]], [],
        [(0, "row", D_MODEL, 0, D_MODEL, F32), (1, "row", D_MODEL, 0, D_MODEL, MXU_DTYPE)], primal_sum=True)

    d_mixed = matmul("d_mixed", dmo, w_out_full, F32, tm=1024, tn=2048, tk=2048, trans_b=True)
    g_w_out = matmul("g_w_out", mixed, dmo, F32, tm=1024, tn=2048, tk=2048, trans_a=True)
    dp = lax.empty((S, P_COLS), MXU_DTYPE)
    into_dp = lambda idx, cb: (idx, "head", D_HEAD, cb, P_COLS, MXU_DTYPE, dp)
    do_fox, dp = rowwise_bwd(
        "post_fox_bwd", f_postf, S, ts_head, H,[(o_fox, "head", D_HEAD, 0), pcol(CB_FZ)], [],
        [(d_mixed, "head", D_HEAD, 0)], [(0,) + head_b, into_dp(1, CB_FZ)])
    do_gdn, dp, d_gdn_norm_g = rowwise_bwd(
        "post_gdn_bwd", f_postg, S, ts_head, H,[(o_gdn, "head", D_HEAD, 0), pcol(CB_GZ)], [gdn_norm_g],
        [(d_mixed, "head", D_HEAD, 8)], [(0,) + head_f, into_dp(1, CB_GZ)])
    d_qn, d_kn, dp, dfq_row, dfk_row = flash_bwd(qn, kn, vb, do_fox, frow, lse, flash_delta(do_fox, o_fox, S, tq), dp,
                                                 S, tq)
    dp, d_qn_g, d_kn_g = fox_norm_bwd(p_all, d_qn, d_kn, fox_qn_g, fox_kn_g, dp, S, ts_wide)
    d_qkv, dsm_chunk, dgc_row = gdn_bwd(qkv, smalls, gc_row, states, tinv, do_gdn, S)
    dp, d_conv_w = conv_act_bwd(p_all, y_conv, d_qkv, conv_w, dp, S, ts)
    d_f = jnp.transpose((dfq_row + dfk_row).reshape(H, S))
    d_gc = jnp.transpose(dgc_row, (0, 2, 1)).reshape(S, H)
    dsm = dsm_chunk + jnp.concatenate([d_f, d_gc, jnp.zeros((S, 128 - 16), F32)], axis=1)
    dp, d_pv, d_av = smalls_bwd(p_all, pv, av, dsm, dp, S, ts)
    g_w_all = matmul("g_w_in", h_b, dp, F32, tm=1024, tn=1664, tk=2048, trans_a=True)

    c_idx = jnp.reshape(ic, (1,)).astype(jnp.int32)

    def chip_sums(tag, g4):
        recv = sibling_send_other_half("d2d_" + tag, g4)
        return sum_own_half("sum2_" + tag, g4, recv, c_idx, 128)

    def finish(tag, p4, far):
        own = lax.dynamic_index_in_dim(p4, chip, axis=0, keepdims=False)
        return sibling_merge("merge_" + tag, sum_parts("sum_" + tag, own, far, 128))

    g_int = g_w_all[None]
    (p_int,) = sum_own_half("sum2_w_in", g_int, sibling_send_other_half("d2d_w_in", g_int), c_idx, 128, rounded=False)
    p4_in = jnp.stack([_block_from_internal(p_int[0], j, n_in) for j in range(N_CHIPS)])
    p4_in_sent = p4_in.astype(MXU_DTYPE)
    p4_out, p4_out_sent = chip_sums("w_out", g_w_out.reshape(N_CHIPS, w_out.shape[1], D_MODEL))
    d_h, far_in, far_out = matmul("d_h", dp, w_all, F32, tm=1024, tn=2048, tk=1664, trans_b=True,
                                  rides=[("scatter", p4_in_sent), ("scatter", p4_out_sent)])
    g_w_in_full = finish("w_in", p4_in, far_in)
    g_w_out_full = finish("w_out", p4_out, far_out)
    d_x, d_norm_g, d_scale, d_shift = rowwise_bwd(
        "prologue_bwd", f_prologue_res, S, ts_wide, 1, [wide(x2)], [norm_g, scale, shift],
        [wide(d_h), wide(dx_res)], [(0, "row", D_MODEL, 0, D_MODEL, F32)])

    parts = [d_shift, d_scale, d_gate, d_norm_g, d_final_g, d_conv_w.reshape(1, CONV_K * CONV_COLS), d_qn_g, d_kn_g,
             d_gdn_norm_g, d_pv, d_av, loss_acc]
    n_pay = sum(p.shape[1] for p in parts)
    pay_w = -(-n_pay // 1024) * 1024
    pay2 = jnp.concatenate(parts + [jnp.zeros((1, pay_w - n_pay), F32)], axis=1).reshape(8, pay_w // 8)
    g3 = allgather8("ag_small", pay2)
    tot = sum_devices("sum_small", g3).reshape(1, pay_w)
    dmod_all = g3.reshape(N_DEV, pay_w)[:, :3 * D_MODEL]
    o = 3 * D_MODEL
    g_b_ada = tot[:, :o]
    g_norm_g, g_final_g = tot[:, o:o + D_MODEL], tot[:, o + D_MODEL:o + 2 * D_MODEL]
    o += 2 * D_MODEL
    g_conv_full = tot[:, o:o + CONV_K * CONV_COLS].reshape(CONV_K, CONV_COLS)
    g_conv = lax.dynamic_slice_in_dim(g_conv_full, chip * conv_cols_chip, conv_cols_chip, axis=1)
    o += CONV_K * CONV_COLS
    g_qn_g, g_kn_g, g_gdn_ng = tot[:, o:o + 128], tot[:, o + 128:o + 256], tot[:, o + 256:o + 384]
    g_pv, g_av = tot[:, o + 384:o + 512], tot[:, o + 512:o + 640]
    loss = tot[0, o + 640]
    g_b_fgate, g_dt_bias, g_a_log = g_pv[:, :8], g_pv[:, 8:16], g_av[:, 8:16]

    small_g = [g_norm_g, g_b_ada, g_final_g, g_conv, g_qn_g, g_kn_g, g_gdn_ng, g_b_fgate, g_a_log, g_dt_bias]
    small_upd = adamw_small(
        "adamw_small",
        [norm_g, b_ada, final_g[None, :], gdn_conv_w[0], fox_qn_g, fox_kn_g, gdn_norm_g, b_fgate, gdn_A_log, gdn_dt_bias],
        small_g,
        [m_norm_g, m_b_ada, m_final_g[None, :], m_gdn_conv_w[0], m_fox_qn_g, m_fox_kn_g, m_gdn_norm_g, m_b_fgate,
         m_gdn_A_log, m_gdn_dt_bias],
        [v_norm_g, v_b_ada, v_final_g[None, :], v_gdn_conv_w[0], v_fox_qn_g, v_fox_kn_g, v_gdn_norm_g, v_b_fgate,
         v_gdn_A_log, v_gdn_dt_bias])

    def small_leaves(k):
        vals = small_g if k == 0 else [small_upd[3 * p + k - 1] for p in range(len(small_g))]
        return [v[0] if p == 2 else (v[None] if p == 3 else v) for p, v in enumerate(vals)]

    n_ada = w_ada.shape[2]
    g_w_ada = ada_grad(c_all, lax.dynamic_slice_in_dim(dmod_all, chip * n_ada, n_ada, axis=1))
    ada_out = adamw("adamw_w_ada", w_ada[0], [g_w_ada], m_w_ada[0], v_w_ada[0], 128)

    in_out = [jnp.transpose(o) for o in adamw("adamw_w_in", jnp.transpose(w_in[0]), [jnp.transpose(g_w_in_full)],
                                              jnp.transpose(m_w_in[0]), jnp.transpose(v_w_in[0]), None, tc=256)]
    out_out = adamw("adamw_w_out", w_out[0], [g_w_out_full], m_w_out[0], v_w_out[0], 128)

    res = [loss, d_x[None]]
    for k in range(4):
        s = small_leaves(k)
        big = [ada_out[k][None], in_out[k][None], out_out[k][None]]
        res += [s[0], big[0], s[1], big[1], s[7], s[4], s[5], s[3], s[8], s[9], s[6], big[2], s[2]]
    return tuple(res)
```

```python
import functools

import jax
import jax.numpy as jnp
from jax import lax
from jax.experimental import pallas as pl
from jax.experimental.pallas import tpu as pltpu

F32 = jnp.float32
MXU_DTYPE = jnp.bfloat16
HIGHEST = lax.Precision.HIGHEST
SOLVE_PRECISION = lax.Precision.HIGH
MESH_ID = pl.DeviceIdType.MESH

D_MODEL = 2048
N_HEADS = 8
D_HEAD = 128
WIDTH = N_HEADS * D_HEAD
CHUNK = 64
CONV_K = 4
EPS = 1e-6
NEG = -1e30
ATT_SCALE = D_HEAD ** -0.5
N_SMALL = 3 * N_HEADS
P_COLS = 8 * WIDTH + 128
CB_FQ, CB_FK, CB_FV, CB_FZ, CB_GQ, CB_GK, CB_GV, CB_GZ, CB_SMALL = 0, 8, 16, 24, 32, 40, 48, 56, 64
IN_WIDTH = 8 * WIDTH + N_SMALL
N_CHIPS = 4
N_DEV = 8

ADAM_LR, ADAM_B1, ADAM_B2, ADAM_EPS, ADAM_WD, ADAM_STEP = 0.001, 0.9, 0.999, 1e-08, 0.01, 10


def _dotg(a, b, dims):
    return lax.dot_general(a.astype(MXU_DTYPE), b.astype(MXU_DTYPE), (dims, ((), ())), preferred_element_type=F32)


@jax.custom_vjp
def mm_nn(a, b):
    return _dotg(a, b, ((1,), (0,)))


@jax.custom_vjp
def mm_nt(a, b):
    return _dotg(a, b, ((1,), (1,)))


@jax.custom_vjp
def mm_tn(a, b):
    return _dotg(a, b, ((0,), (0,)))


mm_nn.defvjp(lambda a, b: (mm_nn(a, b), (a, b)), lambda r, g: (mm_nt(g, r[1]), mm_tn(r[0], g)))
mm_nt.defvjp(lambda a, b: (mm_nt(a, b), (a, b)), lambda r, g: (mm_nn(g, r[1]), mm_tn(g, r[0])))
mm_tn.defvjp(lambda a, b: (mm_tn(a, b), (a, b)), lambda r, g: (mm_nt(r[1], g), mm_nn(r[0], g)))


def hdot(a, b):
    return lax.dot_general(a, b, (((1,), (0,)), ((), ())), precision=HIGHEST, preferred_element_type=F32)


def _sigmoid(x):
    return 0.5 * (jnp.tanh(0.5 * x) + 1.0)


def _silu(x):
    return x * _sigmoid(x)


def _softplus(x):
    return jnp.maximum(x, 0.0) + jnp.log(1.0 + jnp.exp(-jnp.abs(x)))


def _log_sigmoid(x):
    return jnp.minimum(x, 0.0) - jnp.log(1.0 + jnp.exp(-jnp.abs(x)))


def _rms(x, g):
    return x * lax.rsqrt(jnp.mean(x * x, axis=-1, keepdims=True) + EPS) * g


def _l2n(x):
    return x * lax.rsqrt(jnp.sum(x * x, axis=-1, keepdims=True) + EPS)


def _tile(n, pref):
    t = min(n, pref)
    assert n % t == 0, (n, pref)
    return t


def _bspec(kind, w, col0, ts):
    if kind == "row":
        return pl.BlockSpec((ts, w), lambda i, h: (i, col0))
    if kind == "head":
        return pl.BlockSpec((ts, w), lambda i, h: (i, col0 + h))
    assert kind == "stat"
    return pl.BlockSpec((1, ts // w, 1, w), lambda i, h: (h, i, 0, 0))


def _ld(ref, kind):
    return ref[...]


def _st(ref, kind, val):
    if kind == "stat":
        w = ref.shape[3]
        for n in range(ref.shape[1]):
            ref[0, n] = _col_to_row(val[n * w:(n + 1) * w, :])
    else:
        ref[...] = val.astype(ref.dtype)


def _full_spec(a):
    nd = a.ndim
    return pl.BlockSpec(a.shape, lambda i, h: (0,) * nd)


def _out_struct(kind, w, S, width_total, dtype, nh):
    if kind == "stat":
        return jax.ShapeDtypeStruct((nh, S // w, 1, w), dtype)
    return jax.ShapeDtypeStruct((S, width_total), dtype)


def rowwise_fwd(name, f, S, ts, nh, ins, params, outs):
    n_in, n_p = len(ins), len(params)
    into = {j: o[5] for j, o in enumerate(outs) if len(o) > 5}
    outs = [o[:5] for o in outs]
    n_al = len(into)

    def body(*refs):
        vals = [_ld(r, s[1]).astype(F32) for r, s in zip(refs[:n_in], ins)]
        pv = [r[...] for r in refs[n_in:n_in + n_p]]
        res = f(*vals, *pv)
        for r, s, o in zip(refs[n_in + n_p + n_al:], outs, res):
            _st(r, s[0], o)

    return pl.pallas_call(
        body, name=name, grid=(S // ts, nh),
        in_specs=[_bspec(k, w, c0, ts) for (_, k, w, c0) in ins] + [_full_spec(p) for p in params]
        + [pl.BlockSpec(memory_space=pl.ANY)] * n_al,
        out_specs=[_bspec(k, w, c0, ts) for (k, w, c0, _, _) in outs],
        out_shape=[jax.ShapeDtypeStruct(into[j].shape, into[j].dtype) if j in into else _out_struct(k, w, S, tw, dt, nh)
                   for j, (k, w, c0, tw, dt) in enumerate(outs)],
        input_output_aliases={n_in + n_p + n: j for n, j in enumerate(sorted(into))},
    )(*[a for (a, _, _, _) in ins], *params, *[into[j] for j in sorted(into)])


def rowwise_bwd(name, f, S, ts, nh, ins, params, cts, row_grads, extra=None, extra_outs=(), primal_sum=False):
    n_in, n_p, n_ct = len(ins), len(params), len(cts)
    n_rg, n_ex = len(row_grads), len(extra_outs)
    into = {j: rg[6] for j, rg in enumerate(row_grads) if len(rg) > 6}
    row_grads = [rg[:6] for rg in row_grads]
    n_al = len(into)

    def body(*refs):
        first = (pl.program_id(0) == 0) & (pl.program_id(1) == 0)
        in_refs = refs[:n_in]
        p_refs = refs[n_in:n_in + n_p]
        ct_refs = refs[n_in + n_p:n_in + n_p + n_ct]
        o = n_in + n_p + n_ct + n_al
        rg_refs = refs[o:o + n_rg]
        ex_refs = refs[o + n_rg:o + n_rg + n_ex]
        acc_refs = refs[o + n_rg + n_ex:]
        vals = [_ld(r, s[1]).astype(F32) for r, s in zip(in_refs, ins)]
        pv = [r[...] for r in p_refs]
        res, vjp = jax.vjp(f, *vals, *pv)
        if primal_sum:
            cv = [jnp.ones_like(res[0])] + [_ld(r, s[1]).astype(F32) for r, s in zip(ct_refs, cts)]
        else:
            cv = [_ld(r, s[1]).astype(F32) for r, s in zip(ct_refs, cts)]
        grads = vjp(tuple(cv))
        for r, s in zip(rg_refs, row_grads):
            _st(r, s[1], grads[s[0]])
        if extra is not None:
            for r, s, e in zip(ex_refs, extra_outs, extra(vals, cv, grads)):
                _st(r, s[0], e)

        @pl.when(first)
        def _():
            for r in acc_refs:
                r[...] = jnp.zeros_like(r)

        for r, g in zip(acc_refs[:n_p], grads[n_in:]):
            r[...] += g
        if primal_sum:
            acc_refs[n_p][...] += jnp.sum(res[0])

    acc_shapes = [jax.ShapeDtypeStruct(p.shape, F32) for p in params]
    acc_specs = [_full_spec(p) for p in params]
    if primal_sum:
        acc_shapes.append(jax.ShapeDtypeStruct((1, 128), F32))
        acc_specs.append(pl.BlockSpec((1, 128), lambda i, h: (0, 0)))
    rg_shapes = [jax.ShapeDtypeStruct(into[j].shape, into[j].dtype) if j in into else _out_struct(k, w, S, tw, dt, nh)
                 for j, (_, k, w, c0, tw, dt) in enumerate(row_grads)]
    first_alias = n_in + n_p + n_ct
    return pl.pallas_call(
        body, name=name, grid=(S // ts, nh),
        in_specs=[_bspec(k, w, c0, ts) for (_, k, w, c0) in ins] + [_full_spec(p) for p in params]
        + [_bspec(k, w, c0, ts) for (_, k, w, c0) in cts] + [pl.BlockSpec(memory_space=pl.ANY)] * n_al,
        out_specs=[_bspec(k, w, c0, ts) for (_, k, w, c0, _, _) in row_grads]
        + [_bspec(k, w, c0, ts) for (k, w, c0, _, _) in extra_outs] + acc_specs,
        out_shape=rg_shapes + [_out_struct(k, w, S, tw, dt, nh) for (k, w, c0, tw, dt) in extra_outs] + acc_shapes,
        input_output_aliases={first_alias + n: j for n, j in enumerate(sorted(into))},
    )(*[a for (a, _, _, _) in ins], *params, *[a for (a, _, _, _) in cts], *[into[j] for j in sorted(into)])


def f_prologue(x, ng, sc, sh):
    return (_rms(x, ng) * (1.0 + sc) + sh,)


def f_prologue_res(x, ng, sc, sh):
    return (_rms(x, ng) * (1.0 + sc) + sh, x)


def f_loss(x, mo, tgt, gate, fg):
    y = _rms(x + gate * mo, fg)
    return (0.5 * jnp.mean(jnp.square(y - tgt), axis=-1, keepdims=True),)


def f_foxpre(fq, fk, fv, qg, kg):
    return (_rms(fq, qg) * ATT_SCALE, _rms(fk, kg), fv)


def f_postf(o, z):
    return (o * _silu(z),)


def f_postg(o, z, ng):
    return (_rms(o, ng) * _silu(z),)


def fox_norm_bwd(p_all, d_qn, d_kn, qg, kg, dp, S, ts):
    def body(p_ref, dq_ref, dk_ref, qg_ref, kg_ref, dp_in, o_ref, dqg_ref, dkg_ref):
        @pl.when(pl.program_id(0) == 0)
        def _():
            dqg_ref[...] = jnp.zeros_like(dqg_ref)
            dkg_ref[...] = jnp.zeros_like(dkg_ref)

        for ct_ref, g_ref, dg_ref, col0 in ((dq_ref, qg_ref, dqg_ref, 0), (dk_ref, kg_ref, dkg_ref, WIDTH)):
            dg = jnp.zeros((1, D_HEAD), F32)
            for h in range(N_HEADS):
                _, vjp = jax.vjp(_rms, p_ref[:, col0 + h * D_HEAD:col0 + (h + 1) * D_HEAD], g_ref[...])
                dx, dg_h = vjp(ct_ref[:, h * D_HEAD:(h + 1) * D_HEAD])
                o_ref[:, col0 + h * D_HEAD:col0 + (h + 1) * D_HEAD] = dx.astype(o_ref.dtype)
                dg = dg + dg_h
            dg_ref[...] += dg

    gain = pl.BlockSpec((1, D_HEAD), lambda i: (0, 0))
    both = pl.BlockSpec((ts, 2 * WIDTH), lambda i: (i, 0))
    one = pl.BlockSpec((ts, WIDTH), lambda i: (i, 0))
    return pl.pallas_call(
        body, name="fox_norm_bwd", grid=(S // ts,),
        in_specs=[both, one, one, gain, gain, pl.BlockSpec(memory_space=pl.ANY)],
        out_specs=[both, gain, gain],
        out_shape=[jax.ShapeDtypeStruct(dp.shape, dp.dtype), jax.ShapeDtypeStruct((1, D_HEAD), F32),
                   jax.ShapeDtypeStruct((1, D_HEAD), F32)],
        input_output_aliases={5: 0},
    )(p_all, d_qn, d_kn, qg, kg, dp)


def _chip_exchange(kind, x_ref, out_ref, send_sems, recv_sems):
    x, y, c = lax.axis_index("x"), lax.axis_index("y"), lax.axis_index("c")
    sends, arrivals = [], []
    for r in range(1, N_CHIPS):
        px, py = _flip(x, r & 2), _flip(y, r & 1)
        if kind == "scatter":
            src, dst, landed = x_ref.at[2 * px + py], out_ref.at[r - 1], out_ref.at[r - 1]
        else:
            src, dst, landed = x_ref, out_ref.at[2 * x + y], out_ref.at[2 * px + py]
        mk = functools.partial(pltpu.make_async_remote_copy, send_sem=send_sems.at[r - 1], recv_sem=recv_sems.at[r - 1],
                               device_id=(px, py, c), device_id_type=MESH_ID)
        sends.append(mk(src_ref=src, dst_ref=dst))
        arrivals.append(mk(src_ref=src, dst_ref=landed))
    return sends, arrivals


def _exchange_shape(kind, x):
    return jax.ShapeDtypeStruct(((N_CHIPS - 1,) + x.shape[1:]) if kind == "scatter" else ((N_CHIPS,) + x.shape), x.dtype)


def matmul(name, a, b, out_dtype, tm=512, tn=512, tk=2048, trans_a=False, trans_b=False, rides=()):
    M, K = a.shape[::-1] if trans_a else a.shape
    N, K2 = b.shape if trans_b else b.shape[::-1]
    assert K == K2
    tm, tn, tk = _tile(M, tm), _tile(N, tn), _tile(K, tk)
    ni, nj, nk = M // tm, N // tn, K // tk
    a_spec = pl.BlockSpec((tk, tm), lambda i, j, k: (k, i)) if trans_a else pl.BlockSpec((tm, tk), lambda i, j, k: (i, k))
    b_spec = pl.BlockSpec((tn, tk), lambda i, j, k: (j, k)) if trans_b else pl.BlockSpec((tk, tn), lambda i, j, k: (k, j))
    n_ride = len(rides)

    def body(a_ref, b_ref, *rest):
        ride_in, o_ref, ride_out = rest[:n_ride], rest[n_ride], rest[n_ride + 1:2 * n_ride + 1]
        scr = rest[2 * n_ride + 1:]
        i, j, k = pl.program_id(0), pl.program_id(1), pl.program_id(2)
        exchanges = [_chip_exchange(kind, ride_in[n], ride_out[n], scr[2 * n], scr[2 * n + 1])
                     for n, (kind, _) in enumerate(rides)]

        @pl.when((i == 0) & (j == 0) & (k == 0))
        def _():
            for sends, _ in exchanges:
                for cp in sends:
                    cp.start()

        part = lax.dot_general(a_ref[...], b_ref[...], (((0 if trans_a else 1,), (1 if trans_b else 0,)), ((), ())),
                               preferred_element_type=F32)
        if nk == 1:
            o_ref[...] = part.astype(o_ref.dtype)
        else:
            acc = scr[2 * n_ride]

            @pl.when(k == 0)
            def _():
                acc[...] = part

            @pl.when(k > 0)
            def _():
                acc[...] += part

            @pl.when(k == nk - 1)
            def _():
                o_ref[...] = acc[...].astype(o_ref.dtype)

        @pl.when((i == ni - 1) & (j == nj - 1) & (k == nk - 1))
        def _():
            for sends, arrivals in exchanges:
                for cp in arrivals:
                    cp.wait_recv()
                for cp in sends:
                    cp.wait_send()

    sems = [pltpu.SemaphoreType.DMA((N_CHIPS - 1,))] * (2 * n_ride)
    any_spec = pl.BlockSpec(memory_space=pl.ANY)
    res = pl.pallas_call(
        body, name=name, grid=(ni, nj, nk),
        in_specs=[a_spec, b_spec] + [any_spec] * n_ride,
        out_specs=[pl.BlockSpec((tm, tn), lambda i, j, k: (i, j))] + [any_spec] * n_ride,
        out_shape=[jax.ShapeDtypeStruct((M, N), out_dtype)] + [_exchange_shape(kind, x) for kind, x in rides],
        scratch_shapes=sems + ([] if nk == 1 else [pltpu.VMEM((tm, tn), F32)]),
        compiler_params=pltpu.CompilerParams(
            dimension_semantics=("arbitrary",) * 3 if rides else ("parallel", "parallel", "arbitrary")),
    )(a, b, *[x for _, x in rides])
    return tuple(res) if rides else res[0]


def _small_f(z, pv, av):
    lane = lax.broadcasted_iota(jnp.int32, z.shape, 1)
    zz = z + pv
    logf = _log_sigmoid(zz)
    gval = -jnp.exp(av) * _softplus(zz)
    beta = _sigmoid(zz)
    return jnp.where(lane < 8, logf, jnp.where(lane < 16, gval, jnp.where(lane < 24, beta, 0.0)))


def _tri(ts, upper):
    r = lax.broadcasted_iota(jnp.int32, (ts, ts), 0)
    c = lax.broadcasted_iota(jnp.int32, (ts, ts), 1)
    keep = (r <= c) if upper else (r >= c)
    same_chunk = (r // CHUNK) == (c // CHUNK)
    return keep.astype(F32), (keep & same_chunk).astype(F32)


def smalls_fwd(p_all, pv, av, S, ts):
    nt = S // ts

    def body(z_ref, pv_ref, av_ref, o_ref, carry):
        i = pl.program_id(0)

        @pl.when(i == 0)
        def _():
            carry[...] = jnp.zeros_like(carry)

        e = _small_f(z_ref[...], pv_ref[...], av_ref[...])
        lane = lax.broadcasted_iota(jnp.int32, e.shape, 1)
        l_all, l_chunk = _tri(ts, upper=False)
        cum_all = hdot(l_all, e) + carry[...]
        cum_chunk = hdot(l_chunk, e)
        carry[...] = cum_all[ts - 1:ts, :]
        o_ref[...] = jnp.where(lane < 8, cum_all, jnp.where(lane < 16, cum_chunk, e))

    return pl.pallas_call(
        body, name="smalls_fwd", grid=(nt,),
        in_specs=[pl.BlockSpec((ts, 128), lambda i: (i, CB_SMALL)), pl.BlockSpec((1, 128), lambda i: (0, 0)),
                  pl.BlockSpec((1, 128), lambda i: (0, 0))],
        out_specs=pl.BlockSpec((ts, 128), lambda i: (i, 0)),
        out_shape=jax.ShapeDtypeStruct((S, 128), F32),
        scratch_shapes=[pltpu.VMEM((1, 128), F32)],
    )(p_all, pv, av)


def smalls_bwd(p_all, pv, av, dsm, dp, S, ts):
    nt = S // ts

    def body(z_ref, pv_ref, av_ref, d_ref, dp_in, dz_ref, dpv_ref, dav_ref, carry):
        i = pl.program_id(0)

        @pl.when(i == 0)
        def _():
            carry[...] = jnp.zeros_like(carry)
            dpv_ref[...] = jnp.zeros_like(dpv_ref)
            dav_ref[...] = jnp.zeros_like(dav_ref)

        d = d_ref[...]
        lane = lax.broadcasted_iota(jnp.int32, d.shape, 1)
        u_all, u_chunk = _tri(ts, upper=True)
        rev_all = hdot(u_all, d) + carry[...]
        rev_chunk = hdot(u_chunk, d)
        carry[...] = rev_all[0:1, :]
        de = jnp.where(lane < 8, rev_all, jnp.where(lane < 16, rev_chunk, d))
        _, vjp = jax.vjp(_small_f, z_ref[...], pv_ref[...], av_ref[...])
        dz, dpv, dav = vjp(de)
        dz_ref[...] = dz.astype(dz_ref.dtype)
        dpv_ref[...] += dpv
        dav_ref[...] += dav

    rev = lambda i: (nt - 1 - i, 0)
    small_cols = pl.BlockSpec((ts, 128), lambda i: (nt - 1 - i, CB_SMALL))
    return pl.pallas_call(
        body, name="smalls_bwd", grid=(nt,),
        in_specs=[small_cols, pl.BlockSpec((1, 128), lambda i: (0, 0)), pl.BlockSpec((1, 128), lambda i: (0, 0)),
                  pl.BlockSpec((ts, 128), rev), pl.BlockSpec(memory_space=pl.ANY)],
        out_specs=[small_cols, pl.BlockSpec((1, 128), lambda i: (0, 0)), pl.BlockSpec((1, 128), lambda i: (0, 0))],
        out_shape=[jax.ShapeDtypeStruct(dp.shape, dp.dtype), jax.ShapeDtypeStruct((1, 128), F32),
                   jax.ShapeDtypeStruct((1, 128), F32)],
        scratch_shapes=[pltpu.VMEM((1, 128), F32)],
        input_output_aliases={4: 0},
    )(p_all, pv, av, dsm, dp)


def _col_to_row(col):
    return jnp.transpose(jnp.broadcast_to(col, (col.shape[0], 128)))[0:1, :]


def _row_to_col(row):
    return jnp.transpose(jnp.broadcast_to(row, (128, row.shape[1])))[:, 0:1]


def _causal_mask(s, keys_first):
    r = lax.broadcasted_iota(jnp.int32, s.shape, 0)
    c = lax.broadcasted_iota(jnp.int32, s.shape, 1)
    return jnp.where((r <= c) if keys_first else (c <= r), s, NEG)


def flash_fwd(qs, kn, vb, frow, S, tq):
    nq = S // tq
    tk = tq

    def body(q_ref, k_ref, v_ref, fr_ref, o_ref, lse_ref):
        i = pl.program_id(1)
        q = q_ref[...]
        f_base = fr_ref[0, i][:, 0:1]

        def tile(j, carry, diagonal):
            m, l, acc = carry
            off = pl.multiple_of(j * tk, tk)
            k = k_ref[pl.ds(off, tk), :]
            v = v_ref[pl.ds(off, tk), :]
            s = mm_nt(q, k) - (fr_ref[0, j] - f_base)
            if diagonal:
                s = _causal_mask(s, keys_first=False)
            m_new = jnp.maximum(m, jnp.max(s, axis=-1, keepdims=True))
            a = jnp.exp(m - m_new)
            p = jnp.exp(s - m_new)
            l = a * l + jnp.sum(p, axis=-1, keepdims=True)
            acc = a * acc + mm_nn(p, v)
            return m_new, l, acc

        init = (jnp.full((tq, 1), NEG, F32), jnp.zeros((tq, 1), F32), jnp.zeros((tq, D_HEAD), F32))
        carry = lax.fori_loop(0, i, lambda j, c: tile(j, c, False), init)
        m, l, acc = tile(i, carry, True)
        o_ref[...] = acc / l
        lse_ref[0, 0] = _col_to_row(m + jnp.log(l))

    return pl.pallas_call(
        body, name="flash_fwd", grid=(N_HEADS, nq),
        in_specs=[pl.BlockSpec((tq, D_HEAD), lambda h, i: (i, h)), pl.BlockSpec((S, D_HEAD), lambda h, i: (0, h)),
                  pl.BlockSpec((S, D_HEAD), lambda h, i: (0, h)),
                  pl.BlockSpec((1, nq, 1, tk), lambda h, i: (h, 0, 0, 0))],
        out_specs=[pl.BlockSpec((tq, D_HEAD), lambda h, i: (i, h)),
                   pl.BlockSpec((1, 1, 1, tq), lambda h, i: (h, i, 0, 0))],
        out_shape=[jax.ShapeDtypeStruct((S, WIDTH), F32), jax.ShapeDtypeStruct((N_HEADS, nq, 1, tq), F32)],
    )(qs, kn, vb, frow)


def flash_bwd(qs, kn, vb, do, frow, lse_row, delta_row, dp, S, tq):
    nq = S // tq
    tk = tq

    def body(q_ref, k_ref, v_ref, do_ref, fr_ref, lse_ref, dl_ref, dp_in, dq_ref, dk_ref, dv_ref, dfq_ref, dfk_ref):
        j = pl.program_id(1)
        k = k_ref[...]
        v = v_ref[...]
        fk = _row_to_col(fr_ref[0, j])

        @pl.when(j == 0)
        def _():
            dq_ref[...] = jnp.zeros_like(dq_ref)
            dfq_ref[...] = jnp.zeros_like(dfq_ref)

        def tile(i, carry, diagonal):
            dk, dv, dfk = carry
            off = pl.multiple_of(i * tq, tq)
            q = q_ref[pl.ds(off, tq), :]
            do_ = do_ref[pl.ds(off, tq), :]
            st = mm_nt(k, q) - (fk - fr_ref[0, i][:, 0:1])
            if diagonal:
                st = _causal_mask(st, keys_first=True)
            pt = jnp.exp(st - lse_ref[0, i])
            dst = pt * (mm_nt(v, do_) - dl_ref[0, i])
            dq_ref[pl.ds(off, tq), :] += mm_tn(dst, k) * ATT_SCALE
            dfq_ref[0, i] += jnp.sum(dst, axis=0, keepdims=True)
            return dk + mm_nn(dst, q), dv + mm_nn(pt, do_), dfk - jnp.sum(dst, axis=-1, keepdims=True)

        z = jnp.zeros((tk, D_HEAD), F32)
        carry = tile(j, (z, z, jnp.zeros((tk, 1), F32)), True)
        dk, dv, dfk = lax.fori_loop(j + 1, nq, lambda i, c: tile(i, c, False), carry)
        dk_ref[...] = dk
        dv_ref[...] = dv.astype(dv_ref.dtype)
        dfk_ref[0, 0] = _col_to_row(dfk)

    blk = pl.BlockSpec((tk, D_HEAD), lambda h, j: (j, h))
    full = pl.BlockSpec((S, D_HEAD), lambda h, j: (0, h))
    rows = pl.BlockSpec((1, nq, 1, tq), lambda h, j: (h, 0, 0, 0))
    stat = jax.ShapeDtypeStruct((N_HEADS, nq, 1, tq), F32)
    return pl.pallas_call(
        body, name="flash_bwd", grid=(N_HEADS, nq),
        in_specs=[full, blk, blk, full, rows, rows, rows, pl.BlockSpec(memory_space=pl.ANY)],
        out_specs=[full, blk, pl.BlockSpec((tk, D_HEAD), lambda h, j: (j, CB_FV + h)), rows,
                   pl.BlockSpec((1, 1, 1, tk), lambda h, j: (h, j, 0, 0))],
        out_shape=[jax.ShapeDtypeStruct((S, WIDTH), F32), jax.ShapeDtypeStruct((S, WIDTH), F32),
                   jax.ShapeDtypeStruct(dp.shape, dp.dtype), stat, stat],
        input_output_aliases={7: 2},
    )(qs, kn, vb, do, frow, lse_row, delta_row, dp)


CONV_COLS = 3 * WIDTH
CONV_TC = 512


def _gdn_act(y, group):
    s = _silu(y)
    if group == 2:
        return s
    return _l2n(s) * ATT_SCALE if group == 0 else _l2n(s)


def _gdn_act_bwd(y, d, group):
    _, vjp = jax.vjp(functools.partial(_gdn_act, group=group), y)
    return vjp(d)[0]


def conv_act_fwd(p_all, conv_w, S, ts):
    nt, tc = S // ts, CONV_TC
    cb0 = CB_GQ * 128 // tc
    r8 = ts // 8
    tiles_per_group = WIDTH // tc

    def body(x_ref, halo_ref, w_ref, y_ref, a_ref, scr):
        c, i = pl.program_id(0), pl.program_id(1)
        scr[0:8, :] = jnp.where(i > 0, halo_ref[...], 0.0)
        scr[8:8 + ts, :] = x_ref[...]
        w = w_ref[...]
        y = w[3:4, :] * x_ref[...]
        for j in range(CONV_K - 1):
            y = y + w[j:j + 1, :] * scr[5 + j:5 + j + ts, :]
        y_ref[...] = y
        for group in range(3):
            @pl.when(c // tiles_per_group == group)
            def _(group=group):
                for h in range(tc // D_HEAD):
                    sl = slice(h * D_HEAD, (h + 1) * D_HEAD)
                    a_ref[:, sl] = _gdn_act(y_ref[:, sl], group)

    out = pl.BlockSpec((ts, tc), lambda c, i: (i, c))
    return pl.pallas_call(
        body, name="conv_act_fwd", grid=(CONV_COLS // tc, nt),
        in_specs=[pl.BlockSpec((ts, tc), lambda c, i: (i, cb0 + c)),
                  pl.BlockSpec((8, tc), lambda c, i: (jnp.maximum(i * r8 - 1, 0), cb0 + c)),
                  pl.BlockSpec((CONV_K, tc), lambda c, i: (0, c))],
        out_specs=[out, out],
        out_shape=[jax.ShapeDtypeStruct((S, CONV_COLS), F32)] * 2,
        scratch_shapes=[pltpu.VMEM((ts + 8, tc), F32)],
    )(p_all, p_all, conv_w)


def conv_act_bwd(p_all, y, dqkv, conv_w, dp, S, ts):
    nt, tc = S // ts, CONV_TC
    cb0 = CB_GQ * 128 // tc
    r8 = ts // 8
    tiles_per_group = WIDTH // tc

    def body(x_ref, xh_ref, y_ref, yh_ref, d_ref, dh_ref, w_ref, dp_in, dx_ref, dw_ref, xs, ds):
        c, i = pl.program_id(0), pl.program_id(1)
        xs[0:8, :] = jnp.where(i > 0, xh_ref[...], 0.0)
        xs[8:8 + ts, :] = x_ref[...]
        for group in range(3):
            @pl.when(c // tiles_per_group == group)
            def _(group=group):
                for h in range(tc // D_HEAD):
                    sl = slice(h * D_HEAD, (h + 1) * D_HEAD)
                    ds[0:ts, sl] = _gdn_act_bwd(y_ref[:, sl], d_ref[:, sl], group)
                    ds[ts:ts + 8, sl] = jnp.where(i < nt - 1, _gdn_act_bwd(yh_ref[:, sl], dh_ref[:, sl], group), 0.0)
        w = w_ref[...]
        d = ds[0:ts, :]
        dx = w[3:4, :] * d
        for j in range(CONV_K - 1):
            dx = dx + w[j:j + 1, :] * ds[3 - j:3 - j + ts, :]
        dx_ref[...] = dx.astype(dx_ref.dtype)

        @pl.when(i == 0)
        def _():
            dw_ref[...] = jnp.zeros_like(dw_ref)

        rows = [jnp.sum(d * xs[5 + j:5 + j + ts, :], axis=0, keepdims=True) for j in range(CONV_K - 1)]
        rows.append(jnp.sum(d * x_ref[...], axis=0, keepdims=True))
        dw_ref[...] += jnp.concatenate(rows, axis=0)

    tile = pl.BlockSpec((ts, tc), lambda c, i: (i, c))
    next8 = pl.BlockSpec((8, tc), lambda c, i: (jnp.minimum((i + 1) * r8, nt * r8 - 1), c))
    return pl.pallas_call(
        body, name="conv_act_bwd", grid=(CONV_COLS // tc, nt),
        in_specs=[pl.BlockSpec((ts, tc), lambda c, i: (i, cb0 + c)),
                  pl.BlockSpec((8, tc), lambda c, i: (jnp.maximum(i * r8 - 1, 0), cb0 + c)),
                  tile, next8, tile, next8,
                  pl.BlockSpec((CONV_K, tc), lambda c, i: (0, c)), pl.BlockSpec(memory_space=pl.ANY)],
        out_specs=[pl.BlockSpec((ts, tc), lambda c, i: (i, cb0 + c)), pl.BlockSpec((CONV_K, tc), lambda c, i: (0, c))],
        out_shape=[jax.ShapeDtypeStruct(dp.shape, dp.dtype), jax.ShapeDtypeStruct((CONV_K, CONV_COLS), F32)],
        scratch_shapes=[pltpu.VMEM((ts + 8, tc), F32), pltpu.VMEM((ts + 8, tc), F32)],
        input_output_aliases={7: 0},
    )(p_all, p_all, y, y, dqkv, dqkv, conv_w, dp)


def _bdot(a, b, ca, cb):
    return lax.dot_general(a.astype(MXU_DTYPE), b.astype(MXU_DTYPE), (((ca,), (cb,)), ((0,), (0,))),
                           preferred_element_type=F32)


@jax.custom_vjp
def bmm_nn(a, b):
    return _bdot(a, b, 2, 1)


@jax.custom_vjp
def bmm_nt(a, b):
    return _bdot(a, b, 2, 2)


@jax.custom_vjp
def bmm_tn(a, b):
    return _bdot(a, b, 1, 1)


bmm_nn.defvjp(lambda a, b: (bmm_nn(a, b), (a, b)), lambda r, g: (bmm_nt(g, r[1]), bmm_tn(r[0], g)))
bmm_nt.defvjp(lambda a, b: (bmm_nt(a, b), (a, b)), lambda r, g: (bmm_nn(g, r[1]), bmm_tn(g, r[0])))
bmm_tn.defvjp(lambda a, b: (bmm_tn(a, b), (a, b)), lambda r, g: (bmm_nt(r[1], g), bmm_nn(r[0], g)))


def bdot_hi(a, b, ca=2, cb=1):
    return lax.dot_general(a, b, (((ca,), (cb,)), ((0,), (0,))), precision=SOLVE_PRECISION,
                           preferred_element_type=F32)


def _tri_inv_product(m):
    ii = lax.broadcasted_iota(jnp.int32, m.shape, 1)
    jj = lax.broadcasted_iota(jnp.int32, m.shape, 2)
    t = (ii == jj).astype(F32) - m
    pw = bdot_hi(m, m)
    for _ in range(4):
        t = t + bdot_hi(t, pw)
        pw = bdot_hi(pw, pw)
    return t + bdot_hi(t, pw)


def _tri_inv_bwd(t, g):
    return -bdot_hi(bdot_hi(t, g, 1, 1), t, 2, 2)


@jax.custom_vjp
def tri_inv(m):
    return _tri_inv_product(m)


@jax.custom_vjp
def tri_inv_known(m, t):
    return t


tri_inv.defvjp(lambda m: (lambda t: (t, t))(_tri_inv_product(m)), lambda t, g: (_tri_inv_bwd(t, g),))
tri_inv_known.defvjp(lambda m, t: (t, t), lambda t, g: (_tri_inv_bwd(t, g), jnp.zeros_like(t)))


def chunk_fn(q, k, v, gc_col, gc_row, beta, state, t_known=None):
    shape = (N_HEADS, CHUNK, CHUNK)
    ii = lax.broadcasted_iota(jnp.int32, shape, 1)
    jj = lax.broadcasted_iota(jnp.int32, shape, 2)
    lower = ii >= jj
    strict = ii > jj
    decay = jnp.exp(jnp.where(lower, gc_col - gc_row, NEG))
    kb = k * beta
    vb = v * beta
    m = jnp.where(strict, bmm_nt(kb, k) * decay, 0.0)
    t = tri_inv(m) if t_known is None else tri_inv_known(m, t_known)
    u = bdot_hi(t, vb)
    w = bdot_hi(t, kb * jnp.exp(gc_col))
    attn = jnp.where(lower, bmm_nt(q, k) * decay, 0.0)
    v_new = u - bmm_nn(w, state)
    o = bmm_nn(q * jnp.exp(gc_col), state) + bmm_nn(attn, v_new)
    g_last = gc_col[:, CHUNK - 1:CHUNK, :]
    k_dec = k * jnp.exp(g_last - gc_col)
    new_state = state * jnp.exp(g_last) + bmm_tn(k_dec, v_new)
    return (o, new_state, t) if t_known is None else (o, new_state)


GDN_CHUNKS_PER_STEP = 4


def _heads(ref, rows):
    return jnp.stack([ref[rows, h * D_HEAD:(h + 1) * D_HEAD] for h in range(N_HEADS)], axis=0)


def _head_cols(ref, rows, lane0):
    return jnp.stack([ref[rows, lane0 + h:lane0 + h + 1] for h in range(N_HEADS)], axis=0)


def gdn_fwd(qkv, smalls, gc_row, S):
    n = S // CHUNK
    per = _tile(n, GDN_CHUNKS_PER_STEP)
    rows_per = per * CHUNK

    def body(q_ref, k_ref, v_ref, sm_ref, gr_ref, o_ref, st_ref, t_ref, state):
        @pl.when(pl.program_id(0) == 0)
        def _():
            state[...] = jnp.zeros_like(state)

        s = state[...]
        for b in range(per):
            rows = pl.ds(b * CHUNK, CHUNK)
            st_ref[b] = s
            o, s, t = chunk_fn(_heads(q_ref, rows), _heads(k_ref, rows), _heads(v_ref, rows),
                               _head_cols(sm_ref, rows, 8), gr_ref[b][:, None, :], _head_cols(sm_ref, rows, 16), s)
            for h in range(N_HEADS):
                o_ref[rows, h * D_HEAD:(h + 1) * D_HEAD] = o[h]
            t_ref[b] = t
        state[...] = s

    return pl.pallas_call(
        body, name="gdn_chunk_fwd", grid=(n // per,),
        in_specs=[pl.BlockSpec((rows_per, WIDTH), lambda i, g=g: (i, g)) for g in range(3)]
        + [pl.BlockSpec((rows_per, 128), lambda i: (i, 0)), pl.BlockSpec((per, N_HEADS, CHUNK), lambda i: (i, 0, 0))],
        out_specs=[pl.BlockSpec((rows_per, WIDTH), lambda i: (i, 0)),
                   pl.BlockSpec((per, N_HEADS, D_HEAD, D_HEAD), lambda i: (i, 0, 0, 0)),
                   pl.BlockSpec((per, N_HEADS, CHUNK, CHUNK), lambda i: (i, 0, 0, 0))],
        out_shape=[jax.ShapeDtypeStruct((S, WIDTH), F32), jax.ShapeDtypeStruct((n, N_HEADS, D_HEAD, D_HEAD), F32),
                   jax.ShapeDtypeStruct((n, N_HEADS, CHUNK, CHUNK), F32)],
        scratch_shapes=[pltpu.VMEM((N_HEADS, D_HEAD, D_HEAD), F32)],
    )(qkv, qkv, qkv, smalls, gc_row)


def gdn_bwd(qkv, smalls, gc_row, states, tinv, do, S):
    n = S // CHUNK
    per = _tile(n, GDN_CHUNKS_PER_STEP)
    rows_per = per * CHUNK
    nb = n // per

    def body(q_ref, k_ref, v_ref, sm_ref, gr_ref, st_ref, t_ref, do_ref, dqkv_ref, dsm_ref, dgr_ref, dstate):
        @pl.when(pl.program_id(0) == 0)
        def _():
            dstate[...] = jnp.zeros_like(dstate)

        ds = dstate[...]
        for b in reversed(range(per)):
            rows = pl.ds(b * CHUNK, CHUNK)
            t_saved = t_ref[b]
            _, vjp = jax.vjp(lambda *a: chunk_fn(*a, t_known=t_saved), _heads(q_ref, rows), _heads(k_ref, rows),
                             _heads(v_ref, rows), _head_cols(sm_ref, rows, 8), gr_ref[b][:, None, :],
                             _head_cols(sm_ref, rows, 16), st_ref[b])
            dq, dk, dv, dgc, dgr, dbt, ds = vjp((_heads(do_ref, rows), ds))
            for h in range(N_HEADS):
                dqkv_ref[rows, h * D_HEAD:(h + 1) * D_HEAD] = dq[h]
                dqkv_ref[rows, WIDTH + h * D_HEAD:WIDTH + (h + 1) * D_HEAD] = dk[h]
                dqkv_ref[rows, 2 * WIDTH + h * D_HEAD:2 * WIDTH + (h + 1) * D_HEAD] = dv[h]
            cols = [dgc[h] for h in range(N_HEADS)] + [dbt[h] for h in range(N_HEADS)]
            dsm_ref[rows, :] = jnp.concatenate([jnp.zeros((CHUNK, 8), F32)] + cols
                                               + [jnp.zeros((CHUNK, 128 - 24), F32)], axis=1)
            dgr_ref[b] = jnp.concatenate([dgr[h] for h in range(N_HEADS)], axis=0)
        dstate[...] = ds

    rev = lambda c: (lambda i: (nb - 1 - i, c))
    rev4 = lambda i: (nb - 1 - i, 0, 0, 0)
    return pl.pallas_call(
        body, name="gdn_chunk_bwd", grid=(nb,),
        in_specs=[pl.BlockSpec((rows_per, WIDTH), rev(g)) for g in range(3)]
        + [pl.BlockSpec((rows_per, 128), rev(0)), pl.BlockSpec((per, N_HEADS, CHUNK), lambda i: (nb - 1 - i, 0, 0)),
           pl.BlockSpec((per, N_HEADS, D_HEAD, D_HEAD), rev4), pl.BlockSpec((per, N_HEADS, CHUNK, CHUNK), rev4),
           pl.BlockSpec((rows_per, WIDTH), rev(0))],
        out_specs=[pl.BlockSpec((rows_per, 3 * WIDTH), rev(0)), pl.BlockSpec((rows_per, 128), rev(0)),
                   pl.BlockSpec((per, N_HEADS, CHUNK), lambda i: (nb - 1 - i, 0, 0))],
        out_shape=[jax.ShapeDtypeStruct((S, 3 * WIDTH), F32)] + [jax.ShapeDtypeStruct((S, 128), F32),
                                                                  jax.ShapeDtypeStruct((n, N_HEADS, CHUNK), F32)],
        scratch_shapes=[pltpu.VMEM((N_HEADS, D_HEAD, D_HEAD), F32)],
    )(qkv, qkv, qkv, smalls, gc_row, states, tinv, do)


def _flip(a, bit):
    return 1 - a if bit else a


def allgather8(name, buf):
    R, W = buf.shape

    def body(x_ref, out_ref, send_sems, recv_sems, local_sem):
        x, y, c = lax.axis_index("x"), lax.axis_index("y"), lax.axis_index("c")

        def slot(px, py, pc):
            return out_ref.at[4 * px + 2 * py + pc]

        mine = pltpu.make_async_copy(x_ref, slot(x, y, c), local_sem)
        mine.start()
        sends = []
        for r in range(1, N_DEV):
            peer = (_flip(x, r & 4), _flip(y, r & 2), _flip(c, r & 1))
            cp = pltpu.make_async_remote_copy(src_ref=x_ref, dst_ref=slot(x, y, c), send_sem=send_sems.at[r - 1],
                                              recv_sem=recv_sems.at[r - 1], device_id=peer, device_id_type=MESH_ID)
            cp.start()
            sends.append(cp)
        for r in range(1, N_DEV):
            peer = (_flip(x, r & 4), _flip(y, r & 2), _flip(c, r & 1))
            pltpu.make_async_remote_copy(src_ref=x_ref, dst_ref=slot(*peer), send_sem=send_sems.at[r - 1],
                                         recv_sem=recv_sems.at[r - 1], device_id=peer,
                                         device_id_type=MESH_ID).wait_recv()
        for cp in sends:
            cp.wait_send()
        mine.wait()

    return pl.pallas_call(
        body, name=name,
        out_shape=jax.ShapeDtypeStruct((N_DEV, R, W), buf.dtype),
        in_specs=[pl.BlockSpec(memory_space=pltpu.VMEM)],
        out_specs=pl.BlockSpec(memory_space=pltpu.VMEM),
        scratch_shapes=[pltpu.SemaphoreType.DMA((N_DEV - 1,)), pltpu.SemaphoreType.DMA((N_DEV - 1,)),
                        pltpu.SemaphoreType.DMA],
    )(buf)


def allgather_chips(name, blk):
    R, W = blk.shape
    half = R // 2
    n_far = N_CHIPS - 1

    def body(x_ref, out_ref, send_sems, recv_sems):
        x, y, c = lax.axis_index("x"), lax.axis_index("y"), lax.axis_index("c")
        sibling = (x, y, 1 - c)
        chips = [(_flip(x, r & 2), _flip(y, r & 1)) for r in range(1, N_CHIPS)]

        def rows(px, py, pc):
            return out_ref.at[2 * px + py, pc]

        def copy(k, dst, to, src):
            return pltpu.make_async_remote_copy(src_ref=src, dst_ref=dst, send_sem=send_sems.at[k],
                                                recv_sem=recv_sems.at[k], device_id=to, device_id_type=MESH_ID)

        my_half = x_ref.at[c]
        first = [copy(j, rows(x, y, c), (*chip, c), my_half) for j, chip in enumerate(chips)]
        for cp in first:
            cp.start()
        passed = [copy(n_far + j, rows(*chip, c), sibling, rows(*chip, c)) for j, chip in enumerate(chips)]
        for j, chip in enumerate(chips):
            copy(j, rows(*chip, c), (*chip, c), my_half).wait_recv()
            passed[j].start()
        for j, chip in enumerate(chips):
            copy(n_far + j, rows(*chip, 1 - c), sibling, my_half).wait_recv()
        for cp in first + passed:
            cp.wait_send()

    far = pl.pallas_call(
        body, name=name,
        out_shape=jax.ShapeDtypeStruct((N_CHIPS, 2, half, W), blk.dtype),
        in_specs=[pl.BlockSpec(memory_space=pl.ANY)],
        out_specs=pl.BlockSpec(memory_space=pl.ANY),
        scratch_shapes=[pltpu.SemaphoreType.DMA((2 * n_far,)), pltpu.SemaphoreType.DMA((2 * n_far,))],
    )(blk.reshape(2, half, W)).reshape(N_CHIPS, R, W)
    own_chip = 2 * lax.axis_index("x") + lax.axis_index("y")
    return lax.dynamic_update_index_in_dim(far, blk, own_chip, axis=0)


def sibling_send_other_half(name, g4):
    n, R, W = g4.shape
    half = R // 2

    def body(x_ref, out_ref, send_sem, recv_sem):
        x, y, c = lax.axis_index("x"), lax.axis_index("y"), lax.axis_index("c")
        cp = pltpu.make_async_remote_copy(src_ref=x_ref.at[:, pl.ds((1 - c) * half, half), :], dst_ref=out_ref,
                                          send_sem=send_sem, recv_sem=recv_sem, device_id=(x, y, 1 - c),
                                          device_id_type=MESH_ID)
        cp.start()
        cp.wait_recv()
        cp.wait_send()

    return pl.pallas_call(
        body, name=name,
        out_shape=jax.ShapeDtypeStruct((n, half, W), g4.dtype),
        in_specs=[pl.BlockSpec(memory_space=pl.ANY)],
        out_specs=pl.BlockSpec(memory_space=pl.ANY),
        scratch_shapes=[pltpu.SemaphoreType.DMA, pltpu.SemaphoreType.DMA],
    )(g4)


def sibling_merge(name, mine):
    def body(x_ref, out_ref, send_sem, recv_sem):
        x, y, c = lax.axis_index("x"), lax.axis_index("y"), lax.axis_index("c")
        cp = pltpu.make_async_remote_copy(src_ref=x_ref, dst_ref=out_ref, send_sem=send_sem, recv_sem=recv_sem,
                                          device_id=(x, y, 1 - c), device_id_type=MESH_ID)
        cp.start()
        cp.wait_recv()
        cp.wait_send()

    other = pl.pallas_call(
        body, name=name,
        out_shape=jax.ShapeDtypeStruct(mine.shape, mine.dtype),
        in_specs=[pl.BlockSpec(memory_space=pl.ANY)],
        out_specs=pl.BlockSpec(memory_space=pl.ANY),
        scratch_shapes=[pltpu.SemaphoreType.DMA, pltpu.SemaphoreType.DMA],
    )(mine)
    first = lax.axis_index("c") == 0
    return jnp.concatenate([jnp.where(first, mine, other), jnp.where(first, other, mine)], axis=0)


def sum_parts(name, own, recv, tr):
    R, W = own.shape
    tr = _tile(R, tr)

    def body(o_ref, r_ref, out_ref):
        acc = o_ref[...]
        for k in range(recv.shape[0]):
            acc = acc + r_ref[k].astype(F32)
        out_ref[...] = acc

    return pl.pallas_call(
        body, name=name, grid=(R // tr,),
        in_specs=[pl.BlockSpec((tr, W), lambda i: (i, 0)), pl.BlockSpec((recv.shape[0], tr, W), lambda i: (0, i, 0))],
        out_specs=pl.BlockSpec((tr, W), lambda i: (i, 0)),
        out_shape=jax.ShapeDtypeStruct((R, W), F32),
    )(own, recv)


def sum_own_half(name, g4, recv, c_idx, tr, rounded=True):
    n, R, W = g4.shape
    half = R // 2
    tr = _tile(half, tr)
    nb = half // tr

    def body(c_ref, g_ref, r_ref, o_ref, *ob_ref):
        s = g_ref[0] + r_ref[0]
        o_ref[0] = s
        if rounded:
            ob_ref[0][0] = s.astype(ob_ref[0].dtype)

    mine = pl.BlockSpec((1, tr, W), lambda j, i, c: (j, i, 0))
    shapes = [jax.ShapeDtypeStruct((n, half, W), F32)] + ([jax.ShapeDtypeStruct((n, half, W), MXU_DTYPE)] * rounded)
    return pl.pallas_call(
        body, name=name,
        grid_spec=pltpu.PrefetchScalarGridSpec(
            num_scalar_prefetch=1, grid=(n, nb),
            in_specs=[pl.BlockSpec((1, tr, W), lambda j, i, c: (j, c[0] * nb + i, 0)), mine],
            out_specs=[mine] * len(shapes)),
        out_shape=shapes,
    )(c_idx, g4, recv)


def sum_devices(name, g):
    def body(g_ref, o_ref):
        acc = g_ref[0]
        for d in range(1, N_DEV):
            acc = acc + g_ref[d]
        o_ref[...] = acc

    return pl.pallas_call(body, name=name, out_shape=jax.ShapeDtypeStruct(g.shape[1:], F32))(g)


def ada_fwd(c_all, w_ada):
    K, N = w_ada.shape
    tn = _tile(N, 512)

    def body(c_ref, w_ref, o_ref):
        o_ref[...] = mm_nn(_silu(c_ref[...]), w_ref[...])

    return pl.pallas_call(
        body, name="ada_fwd", grid=(N // tn,),
        in_specs=[pl.BlockSpec((N_DEV, K), lambda j: (0, 0)), pl.BlockSpec((K, tn), lambda j: (0, j))],
        out_specs=pl.BlockSpec((N_DEV, tn), lambda j: (0, j)),
        out_shape=jax.ShapeDtypeStruct((N_DEV, N), F32),
    )(c_all, w_ada)


def ada_grad(c_all, dmod):
    K = c_all.shape[1]
    N = dmod.shape[1]
    tn = _tile(N, 512)

    def body(c_ref, d_ref, o_ref):
        o_ref[...] = mm_tn(_silu(c_ref[...]), d_ref[...])

    return pl.pallas_call(
        body, name="ada_grad", grid=(N // tn,),
        in_specs=[pl.BlockSpec((N_DEV, K), lambda j: (0, 0)), pl.BlockSpec((N_DEV, tn), lambda j: (0, j))],
        out_specs=pl.BlockSpec((K, tn), lambda j: (0, j)),
        out_shape=jax.ShapeDtypeStruct((K, N), F32),
    )(c_all, dmod)


def adamw(name, w, gparts, m, v, tr, tc=None):
    R, W = w.shape
    tr = R if tc else _tile(R, tr)
    ng = len(gparts)

    def body(w_ref, *refs):
        g_refs, (m_ref, v_ref, g_out, d_out, m_out, v_out) = refs[:ng], refs[ng:]
        g = g_refs[0][...]
        for r in g_refs[1:]:
            g = g + r[...]
        g_out[...] = g
        d_out[...], m_out[...], v_out[...] = _adamw_update(w_ref[...], g, m_ref[...], v_ref[...])

    spec = pl.BlockSpec((R, tc), lambda i: (0, i)) if tc else pl.BlockSpec((tr, W), lambda i: (i, 0))
    return pl.pallas_call(
        body, name=name, grid=(W // tc if tc else R // tr,),
        in_specs=[spec] * (3 + ng), out_specs=[spec] * 4,
        out_shape=[jax.ShapeDtypeStruct((R, W), F32)] * 4,
    )(w, *gparts, m, v)


def _adamw_update(w, g, m, v):
    m1 = ADAM_B1 * m + (1.0 - ADAM_B1) * g
    v1 = ADAM_B2 * v + (1.0 - ADAM_B2) * jnp.square(g)
    m_hat = m1 / (1.0 - ADAM_B1 ** ADAM_STEP)
    v_hat = v1 / (1.0 - ADAM_B2 ** ADAM_STEP)
    return -ADAM_LR * (m_hat / (jnp.sqrt(v_hat) + ADAM_EPS) + ADAM_WD * w), m1, v1


def adamw_small(name, ws, gs, ms, vs):
    n = len(ws)

    def body(*refs):
        for k in range(n):
            w_ref, g_ref, m_ref, v_ref = (refs[j * n + k] for j in range(4))
            outs = refs[4 * n + 3 * k:4 * n + 3 * k + 3]
            for o_ref, val in zip(outs, _adamw_update(w_ref[...], g_ref[...], m_ref[...], v_ref[...])):
                o_ref[...] = val

    return pl.pallas_call(
        body, name=name, out_shape=[jax.ShapeDtypeStruct(w.shape, F32) for w in ws for _ in range(3)],
    )(*ws, *gs, *ms, *vs)


_REF_SEGMENTS = ((0, 4 * WIDTH, 0), (4 * WIDTH, 4 * WIDTH + 8, 4 * WIDTH), (4 * WIDTH + 8, 8 * WIDTH + 8, -8),
                 (8 * WIDTH + 8, IN_WIDTH, 0))


def _internal_from_blocks(blocks, n):
    a = 4 * WIDTH
    pieces = []
    for lo, hi in ((0, a), (a + 8, 2 * a + 8), (a, a + 8), (2 * a + 8, IN_WIDTH)):
        for j in range(N_CHIPS):
            s, e = max(lo, j * n), min(hi, (j + 1) * n)
            if s < e:
                pieces.append(blocks[j][:, s - j * n:e - j * n])
    pieces.append(jnp.zeros(blocks.shape[1:2] + (128 - N_SMALL,), blocks.dtype))
    return jnp.concatenate(pieces, axis=1)


def _block_from_internal(g, j, n):
    pieces = []
    for lo, hi, shift in _REF_SEGMENTS:
        s, e = max(lo, j * n), min(hi, (j + 1) * n)
        if s < e:
            pieces.append(g[:, s + shift:e + shift])
    return jnp.concatenate(pieces, axis=1)


def _pad_lanes(v, n=128):
    return jnp.pad(v, ((0, 0), (0, n - v.shape[1])))


def kernel(x, c, norm_g, w_ada, b_ada, w_in, b_fgate, fox_qn_g, fox_kn_g, gdn_conv_w, gdn_A_log, gdn_dt_bias, gdn_norm_g, w_out, final_g, loss_target, m_norm_g, m_w_ada, m_b_ada, m_w_in, m_b_fgate, m_fox_qn_g, m_fox_kn_g, m_gdn_conv_w, m_gdn_A_log, m_gdn_dt_bias, m_gdn_norm_g, m_w_out, m_final_g, v_norm_g, v_w_ada, v_b_ada, v_w_in, v_b_fgate, v_fox_qn_g, v_fox_kn_g, v_gdn_conv_w, v_gdn_A_log, v_gdn_dt_bias, v_gdn_norm_g, v_w_out, v_final_g):
    S = x.shape[1]
    H = N_HEADS
    ix, iy, ic = lax.axis_index("x"), lax.axis_index("y"), lax.axis_index("c")
    chip = 2 * ix + iy
    me = 2 * chip + ic
    x2, tgt = x[0], loss_target[0]
    ts_wide = _tile(S, 256)
    ts = _tile(S, 512)
    ts_conv = _tile(S, 1024)
    ts_head = _tile(S, 2048)
    tq = _tile(S, 1024)
    nq = S // tq
    nch = S // CHUNK
    conv_cols_chip = CONV_COLS // N_CHIPS

    pay = jnp.concatenate([c, _pad_lanes(gdn_conv_w[0], D_MODEL), jnp.zeros((3, D_MODEL), F32)], axis=0)
    g1 = allgather8("ag_c", pay)
    c_all = g1[:, 0, :]
    conv_w = jnp.concatenate([g1[2 * j, 1:1 + CONV_K, :conv_cols_chip] for j in range(N_CHIPS)], axis=1)
    g2 = allgather8("ag_mod", ada_fwd(c_all, w_ada[0]))
    mod_rows = jnp.concatenate([g2[2 * j] for j in range(N_CHIPS)], axis=1)
    mod = lax.dynamic_slice_in_dim(mod_rows, me, 1, axis=0) + b_ada
    shift, scale, gate = mod[:, :D_MODEL], mod[:, D_MODEL:2 * D_MODEL], mod[:, 2 * D_MODEL:]

    n_in = w_in.shape[2]
    win4 = allgather_chips("ag_w_in", w_in[0].astype(MXU_DTYPE))
    w_all = _internal_from_blocks(win4, n_in)
    w_out_b = w_out[0].astype(MXU_DTYPE)

    (h_b,) = rowwise_fwd("prologue", f_prologue, S, ts_wide, 1, [(x2, "row", D_MODEL, 0)], [norm_g, scale, shift],
                         [("row", D_MODEL, 0, D_MODEL, MXU_DTYPE)])
    p_all, wout_far = matmul("in_proj", h_b, w_all, F32, tm=1024, tn=1664, tk=2048, rides=[("gather", w_out_b)])
    w_out_full = lax.dynamic_update_index_in_dim(wout_far, w_out_b, chip, axis=0).reshape(2 * WIDTH, D_MODEL)
    pv = jnp.concatenate([b_fgate, gdn_dt_bias, jnp.zeros((1, 128 - 16), F32)], axis=1)
    av = jnp.concatenate([jnp.zeros((1, 8), F32), gdn_A_log, jnp.zeros((1, 128 - 16), F32)], axis=1)
    smalls = smalls_fwd(p_all, pv, av, S, ts)
    frow = jnp.transpose(smalls[:, :H]).reshape(H, nq, 1, tq)
    gc_row = jnp.transpose(smalls[:, 8:16].reshape(nch, CHUNK, H), (0, 2, 1))
    head_b = ("head", D_HEAD, 0, WIDTH, MXU_DTYPE)
    head_f = ("head", D_HEAD, 0, WIDTH, F32)
    pcol = lambda cb: (p_all, "head", D_HEAD, cb)
    qn, kn, vb = rowwise_fwd("fox_pre", f_foxpre, S, ts_head, H,[pcol(CB_FQ), pcol(CB_FK), pcol(CB_FV)],
                             [fox_qn_g, fox_kn_g], [head_b] * 3)
    o_fox, lse = flash_fwd(qn, kn, vb, frow, S, tq)
    y_conv, qkv = conv_act_fwd(p_all, conv_w, S, ts_conv)
    o_gdn, states, tinv = gdn_fwd(qkv, smalls, gc_row, S)
    mixed = lax.empty((S, 2 * WIDTH), MXU_DTYPE)
    (mixed,) = rowwise_fwd("post_fox", f_postf, S, ts_head, H,[(o_fox, "head", D_HEAD, 0), pcol(CB_FZ)], [],
                           [("head", D_HEAD, 0, 2 * WIDTH, MXU_DTYPE, mixed)])
    (mixed,) = rowwise_fwd("post_gdn", f_postg, S, ts_head, H,[(o_gdn, "head", D_HEAD, 0), pcol(CB_GZ)], [gdn_norm_g],
                           [("head", D_HEAD, H, 2 * WIDTH, MXU_DTYPE, mixed)])
    mo = matmul("out_proj", mixed, w_out_full, F32, tm=1024, tn=2048, tk=2048)

    wide = lambda a: (a, "row", D_MODEL, 0)
    dx_res, dmo, d_gate, d_final_g, loss_acc = rowwise_bwd(
        "loss", f_loss, S, ts_wide, 1, [wide(x2), wide(mo), wide(tgt)], [gate, final_g[None, :]], [],
        [(0, "row", D_MODEL, 0, D_MODEL, F32), (1, "row", D_MODEL, 0, D_MODEL, MXU_DTYPE)], primal_sum=True)

    d_mixed = matmul("d_mixed", dmo, w_out_full, F32, tm=1024, tn=2048, tk=2048, trans_b=True)
    g_w_out = matmul("g_w_out", mixed, dmo, F32, tm=1024, tn=2048, tk=2048, trans_a=True)
    dp = lax.empty((S, P_COLS), MXU_DTYPE)
    into_dp = lambda idx, cb: (idx, "head", D_HEAD, cb, P_COLS, MXU_DTYPE, dp)
    do_fox, dp, delta = rowwise_bwd(
        "post_fox_bwd", f_postf, S, ts_head, H,[(o_fox, "head", D_HEAD, 0), pcol(CB_FZ)], [],
        [(d_mixed, "head", D_HEAD, 0)], [(0,) + head_b, into_dp(1, CB_FZ)],
        extra=lambda vals, cv, grads: (jnp.sum(grads[0].astype(MXU_DTYPE).astype(F32) * vals[0], axis=-1,
                                               keepdims=True),),
        extra_outs=[("stat", tq, 0, None, F32)])
    do_gdn, dp, d_gdn_norm_g = rowwise_bwd(
        "post_gdn_bwd", f_postg, S, ts_head, H,[(o_gdn, "head", D_HEAD, 0), pcol(CB_GZ)], [gdn_norm_g],
        [(d_mixed, "head", D_HEAD, 8)], [(0,) + head_f, into_dp(1, CB_GZ)])
    d_qn, d_kn, dp, dfq_row, dfk_row = flash_bwd(qn, kn, vb, do_fox, frow, lse, delta, dp, S, tq)
    dp, d_qn_g, d_kn_g = fox_norm_bwd(p_all, d_qn, d_kn, fox_qn_g, fox_kn_g, dp, S, ts_wide)
    d_qkv, dsm_chunk, dgc_row = gdn_bwd(qkv, smalls, gc_row, states, tinv, do_gdn, S)
    dp, d_conv_w = conv_act_bwd(p_all, y_conv, d_qkv, conv_w, dp, S, ts_conv)
    d_f = jnp.transpose((dfq_row + dfk_row).reshape(H, S))
    d_gc = jnp.transpose(dgc_row, (0, 2, 1)).reshape(S, H)
    dsm = dsm_chunk + jnp.concatenate([d_f, d_gc, jnp.zeros((S, 128 - 16), F32)], axis=1)
    dp, d_pv, d_av = smalls_bwd(p_all, pv, av, dsm, dp, S, ts)
    g_w_all = matmul("g_w_in", h_b, dp, F32, tm=1024, tn=1664, tk=2048, trans_a=True)

    c_idx = jnp.reshape(ic, (1,)).astype(jnp.int32)

    def chip_sums(tag, g4):
        recv = sibling_send_other_half("d2d_" + tag, g4)
        return sum_own_half("sum2_" + tag, g4, recv, c_idx, 128)

    def finish(tag, p4, far):
        own = lax.dynamic_index_in_dim(p4, chip, axis=0, keepdims=False)
        return sibling_merge("merge_" + tag, sum_parts("sum_" + tag, own, far, 128))

    g_int = g_w_all[None]
    (p_int,) = sum_own_half("sum2_w_in", g_int, sibling_send_other_half("d2d_w_in", g_int), c_idx, 128, rounded=False)
    p4_in = jnp.stack([_block_from_internal(p_int[0], j, n_in) for j in range(N_CHIPS)])
    p4_in_sent = p4_in.astype(MXU_DTYPE)
    p4_out, p4_out_sent = chip_sums("w_out", g_w_out.reshape(N_CHIPS, w_out.shape[1], D_MODEL))
    d_h, far_in, far_out = matmul("d_h", dp, w_all, F32, tm=1024, tn=2048, tk=1664, trans_b=True,
                                  rides=[("scatter", p4_in_sent), ("scatter", p4_out_sent)])
    g_w_in_full = finish("w_in", p4_in, far_in)
    g_w_out_full = finish("w_out", p4_out, far_out)
    d_x, d_norm_g, d_scale, d_shift = rowwise_bwd(
        "prologue_bwd", f_prologue_res, S, ts_wide, 1, [wide(x2)], [norm_g, scale, shift],
        [wide(d_h), wide(dx_res)], [(0, "row", D_MODEL, 0, D_MODEL, F32)])

    parts = [d_shift, d_scale, d_gate, d_norm_g, d_final_g, d_conv_w.reshape(1, CONV_K * CONV_COLS), d_qn_g, d_kn_g,
             d_gdn_norm_g, d_pv, d_av, loss_acc]
    n_pay = sum(p.shape[1] for p in parts)
    pay_w = -(-n_pay // 1024) * 1024
    pay2 = jnp.concatenate(parts + [jnp.zeros((1, pay_w - n_pay), F32)], axis=1).reshape(8, pay_w // 8)
    g3 = allgather8("ag_small", pay2)
    tot = sum_devices("sum_small", g3).reshape(1, pay_w)
    dmod_all = g3.reshape(N_DEV, pay_w)[:, :3 * D_MODEL]
    o = 3 * D_MODEL
    g_b_ada = tot[:, :o]
    g_norm_g, g_final_g = tot[:, o:o + D_MODEL], tot[:, o + D_MODEL:o + 2 * D_MODEL]
    o += 2 * D_MODEL
    g_conv_full = tot[:, o:o + CONV_K * CONV_COLS].reshape(CONV_K, CONV_COLS)
    g_conv = lax.dynamic_slice_in_dim(g_conv_full, chip * conv_cols_chip, conv_cols_chip, axis=1)
    o += CONV_K * CONV_COLS
    g_qn_g, g_kn_g, g_gdn_ng = tot[:, o:o + 128], tot[:, o + 128:o + 256], tot[:, o + 256:o + 384]
    g_pv, g_av = tot[:, o + 384:o + 512], tot[:, o + 512:o + 640]
    loss = tot[0, o + 640]
    g_b_fgate, g_dt_bias, g_a_log = g_pv[:, :8], g_pv[:, 8:16], g_av[:, 8:16]

    small_g = [g_norm_g, g_b_ada, g_final_g, g_conv, g_qn_g, g_kn_g, g_gdn_ng, g_b_fgate, g_a_log, g_dt_bias]
    small_upd = adamw_small(
        "adamw_small",
        [norm_g, b_ada, final_g[None, :], gdn_conv_w[0], fox_qn_g, fox_kn_g, gdn_norm_g, b_fgate, gdn_A_log, gdn_dt_bias],
        small_g,
        [m_norm_g, m_b_ada, m_final_g[None, :], m_gdn_conv_w[0], m_fox_qn_g, m_fox_kn_g, m_gdn_norm_g, m_b_fgate,
         m_gdn_A_log, m_gdn_dt_bias],
        [v_norm_g, v_b_ada, v_final_g[None, :], v_gdn_conv_w[0], v_fox_qn_g, v_fox_kn_g, v_gdn_norm_g, v_b_fgate,
         v_gdn_A_log, v_gdn_dt_bias])

    def small_leaves(k):
        vals = small_g if k == 0 else [small_upd[3 * p + k - 1] for p in range(len(small_g))]
        return [v[0] if p == 2 else (v[None] if p == 3 else v) for p, v in enumerate(vals)]

    n_ada = w_ada.shape[2]
    g_w_ada = ada_grad(c_all, lax.dynamic_slice_in_dim(dmod_all, chip * n_ada, n_ada, axis=1))
    ada_out = adamw("adamw_w_ada", w_ada[0], [g_w_ada], m_w_ada[0], v_w_ada[0], 128)

    in_out = [jnp.transpose(o) for o in adamw("adamw_w_in", jnp.transpose(w_in[0]), [jnp.transpose(g_w_in_full)],
                                              jnp.transpose(m_w_in[0]), jnp.transpose(v_w_in[0]), None, tc=256)]
    out_out = adamw("adamw_w_out", w_out[0], [g_w_out_full], m_w_out[0], v_w_out[0], 128)

    res = [loss, d_x[None]]
    for k in range(4):
        s = small_leaves(k)
        big = [ada_out[k][None], in_out[k][None], out_out[k][None]]
        res += [s[0], big[0], s[1], big[1], s[7], s[4], s[5], s[3], s[8], s[9], s[6], big[2], s[2]]
    return tuple(res)
```

```python
import functools

import jax
import jax.numpy as jnp
from jax import lax
from jax.experimental import pallas as pl
from jax.experimental.pallas import tpu as pltpu

F32 = jnp.float32
MXU_DTYPE = jnp.bfloat16
HIGHEST = lax.Precision.HIGHEST
SOLVE_PRECISION = lax.Precision.HIGH
MESH_ID = pl.DeviceIdType.MESH

D_MODEL = 2048
N_HEADS = 8
D_HEAD = 128
WIDTH = N_HEADS * D_HEAD
CHUNK = 64
CONV_K = 4
EPS = 1e-6
NEG = -1e30
ATT_SCALE = D_HEAD ** -0.5
N_SMALL = 3 * N_HEADS
P_COLS = 8 * WIDTH + 128
CB_FQ, CB_FK, CB_FV, CB_FZ, CB_GQ, CB_GK, CB_GV, CB_GZ, CB_SMALL = 0, 8, 16, 24, 32, 40, 48, 56, 64
IN_WIDTH = 8 * WIDTH + N_SMALL
N_CHIPS = 4
N_DEV = 8

ADAM_LR, ADAM_B1, ADAM_B2, ADAM_EPS, ADAM_WD, ADAM_STEP = 0.001, 0.9, 0.999, 1e-08, 0.01, 10


def _dotg(a, b, dims):
    return lax.dot_general(a.astype(MXU_DTYPE), b.astype(MXU_DTYPE), (dims, ((), ())), preferred_element_type=F32)


@jax.custom_vjp
def mm_nn(a, b):
    return _dotg(a, b, ((1,), (0,)))


@jax.custom_vjp
def mm_nt(a, b):
    return _dotg(a, b, ((1,), (1,)))


@jax.custom_vjp
def mm_tn(a, b):
    return _dotg(a, b, ((0,), (0,)))


mm_nn.defvjp(lambda a, b: (mm_nn(a, b), (a, b)), lambda r, g: (mm_nt(g, r[1]), mm_tn(r[0], g)))
mm_nt.defvjp(lambda a, b: (mm_nt(a, b), (a, b)), lambda r, g: (mm_nn(g, r[1]), mm_tn(g, r[0])))
mm_tn.defvjp(lambda a, b: (mm_tn(a, b), (a, b)), lambda r, g: (mm_nt(r[1], g), mm_nn(r[0], g)))


def hdot(a, b):
    return lax.dot_general(a, b, (((1,), (0,)), ((), ())), precision=HIGHEST, preferred_element_type=F32)


def _sigmoid(x):
    return 0.5 * (jnp.tanh(0.5 * x) + 1.0)


def _silu(x):
    return x * _sigmoid(x)


def _softplus(x):
    return jnp.maximum(x, 0.0) + jnp.log(1.0 + jnp.exp(-jnp.abs(x)))


def _log_sigmoid(x):
    return jnp.minimum(x, 0.0) - jnp.log(1.0 + jnp.exp(-jnp.abs(x)))


def _rms(x, g):
    return x * lax.rsqrt(jnp.mean(x * x, axis=-1, keepdims=True) + EPS) * g


def _l2n(x):
    return x * lax.rsqrt(jnp.sum(x * x, axis=-1, keepdims=True) + EPS)


def _tile(n, pref):
    t = min(n, pref)
    assert n % t == 0, (n, pref)
    return t


def _bspec(kind, w, col0, ts):
    if kind == "row":
        return pl.BlockSpec((ts, w), lambda i, h: (i, col0))
    if kind == "head":
        return pl.BlockSpec((ts, w), lambda i, h: (i, col0 + h))
    assert kind == "stat"
    return pl.BlockSpec((1, ts // w, 1, w), lambda i, h: (h, i, 0, 0))


def _ld(ref, kind):
    return ref[...]


def _st(ref, kind, val):
    if kind == "stat":
        w = ref.shape[3]
        for n in range(ref.shape[1]):
            ref[0, n] = _col_to_row(val[n * w:(n + 1) * w, :])
    else:
        ref[...] = val.astype(ref.dtype)


def _full_spec(a):
    nd = a.ndim
    return pl.BlockSpec(a.shape, lambda i, h: (0,) * nd)


def _out_struct(kind, w, S, width_total, dtype, nh):
    if kind == "stat":
        return jax.ShapeDtypeStruct((nh, S // w, 1, w), dtype)
    return jax.ShapeDtypeStruct((S, width_total), dtype)


def rowwise_fwd(name, f, S, ts, nh, ins, params, outs):
    n_in, n_p = len(ins), len(params)
    into = {j: o[5] for j, o in enumerate(outs) if len(o) > 5}
    outs = [o[:5] for o in outs]
    n_al = len(into)

    def body(*refs):
        vals = [_ld(r, s[1]).astype(F32) for r, s in zip(refs[:n_in], ins)]
        pv = [r[...] for r in refs[n_in:n_in + n_p]]
        res = f(*vals, *pv)
        for r, s, o in zip(refs[n_in + n_p + n_al:], outs, res):
            _st(r, s[0], o)

    return pl.pallas_call(
        body, name=name, grid=(S // ts, nh),
        in_specs=[_bspec(k, w, c0, ts) for (_, k, w, c0) in ins] + [_full_spec(p) for p in params]
        + [pl.BlockSpec(memory_space=pl.ANY)] * n_al,
        out_specs=[_bspec(k, w, c0, ts) for (k, w, c0, _, _) in outs],
        out_shape=[jax.ShapeDtypeStruct(into[j].shape, into[j].dtype) if j in into else _out_struct(k, w, S, tw, dt, nh)
                   for j, (k, w, c0, tw, dt) in enumerate(outs)],
        input_output_aliases={n_in + n_p + n: j for n, j in enumerate(sorted(into))},
    )(*[a for (a, _, _, _) in ins], *params, *[into[j] for j in sorted(into)])


def rowwise_bwd(name, f, S, ts, nh, ins, params, cts, row_grads, extra=None, extra_outs=(), primal_sum=False):
    n_in, n_p, n_ct = len(ins), len(params), len(cts)
    n_rg, n_ex = len(row_grads), len(extra_outs)
    into = {j: rg[6] for j, rg in enumerate(row_grads) if len(rg) > 6}
    row_grads = [rg[:6] for rg in row_grads]
    n_al = len(into)

    def body(*refs):
        first = (pl.program_id(0) == 0) & (pl.program_id(1) == 0)
        in_refs = refs[:n_in]
        p_refs = refs[n_in:n_in + n_p]
        ct_refs = refs[n_in + n_p:n_in + n_p + n_ct]
        o = n_in + n_p + n_ct + n_al
        rg_refs = refs[o:o + n_rg]
        ex_refs = refs[o + n_rg:o + n_rg + n_ex]
        acc_refs = refs[o + n_rg + n_ex:]
        vals = [_ld(r, s[1]).astype(F32) for r, s in zip(in_refs, ins)]
        pv = [r[...] for r in p_refs]
        res, vjp = jax.vjp(f, *vals, *pv)
        if primal_sum:
            cv = [jnp.ones_like(res[0])] + [_ld(r, s[1]).astype(F32) for r, s in zip(ct_refs, cts)]
        else:
            cv = [_ld(r, s[1]).astype(F32) for r, s in zip(ct_refs, cts)]
        grads = vjp(tuple(cv))
        for r, s in zip(rg_refs, row_grads):
            _st(r, s[1], grads[s[0]])
        if extra is not None:
            for r, s, e in zip(ex_refs, extra_outs, extra(vals, cv, grads)):
                _st(r, s[0], e)

        @pl.when(first)
        def _():
            for r in acc_refs:
                r[...] = jnp.zeros_like(r)

        for r, g in zip(acc_refs[:n_p], grads[n_in:]):
            r[...] += g
        if primal_sum:
            acc_refs[n_p][...] += jnp.sum(res[0])

    acc_shapes = [jax.ShapeDtypeStruct(p.shape, F32) for p in params]
    acc_specs = [_full_spec(p) for p in params]
    if primal_sum:
        acc_shapes.append(jax.ShapeDtypeStruct((1, 128), F32))
        acc_specs.append(pl.BlockSpec((1, 128), lambda i, h: (0, 0)))
    rg_shapes = [jax.ShapeDtypeStruct(into[j].shape, into[j].dtype) if j in into else _out_struct(k, w, S, tw, dt, nh)
                 for j, (_, k, w, c0, tw, dt) in enumerate(row_grads)]
    first_alias = n_in + n_p + n_ct
    return pl.pallas_call(
        body, name=name, grid=(S // ts, nh),
        in_specs=[_bspec(k, w, c0, ts) for (_, k, w, c0) in ins] + [_full_spec(p) for p in params]
        + [_bspec(k, w, c0, ts) for (_, k, w, c0) in cts] + [pl.BlockSpec(memory_space=pl.ANY)] * n_al,
        out_specs=[_bspec(k, w, c0, ts) for (_, k, w, c0, _, _) in row_grads]
        + [_bspec(k, w, c0, ts) for (k, w, c0, _, _) in extra_outs] + acc_specs,
        out_shape=rg_shapes + [_out_struct(k, w, S, tw, dt, nh) for (k, w, c0, tw, dt) in extra_outs] + acc_shapes,
        input_output_aliases={first_alias + n: j for n, j in enumerate(sorted(into))},
    )(*[a for (a, _, _, _) in ins], *params, *[a for (a, _, _, _) in cts], *[into[j] for j in sorted(into)])


def f_prologue(x, ng, sc, sh):
    return (_rms(x, ng) * (1.0 + sc) + sh,)


def f_prologue_res(x, ng, sc, sh):
    return (_rms(x, ng) * (1.0 + sc) + sh, x)


def f_loss(x, mo, tgt, gate, fg):
    y = _rms(x + gate * mo, fg)
    return (0.5 * jnp.mean(jnp.square(y - tgt), axis=-1, keepdims=True),)


def f_foxpre(fq, fk, fv, qg, kg):
    return (_rms(fq, qg) * ATT_SCALE, _rms(fk, kg), fv)


def f_postf(o, z):
    return (o * _silu(z),)


def f_postg(o, z, ng):
    return (_rms(o, ng) * _silu(z),)


def fox_norm_bwd(p_all, d_qn, d_kn, qg, kg, dp, S, ts):
    def body(p_ref, dq_ref, dk_ref, qg_ref, kg_ref, dp_in, o_ref, dqg_ref, dkg_ref):
        @pl.when(pl.program_id(0) == 0)
        def _():
            dqg_ref[...] = jnp.zeros_like(dqg_ref)
            dkg_ref[...] = jnp.zeros_like(dkg_ref)

        for ct_ref, g_ref, dg_ref, col0 in ((dq_ref, qg_ref, dqg_ref, 0), (dk_ref, kg_ref, dkg_ref, WIDTH)):
            dg = jnp.zeros((1, D_HEAD), F32)
            for h in range(N_HEADS):
                _, vjp = jax.vjp(_rms, p_ref[:, col0 + h * D_HEAD:col0 + (h + 1) * D_HEAD], g_ref[...])
                dx, dg_h = vjp(ct_ref[:, h * D_HEAD:(h + 1) * D_HEAD])
                o_ref[:, col0 + h * D_HEAD:col0 + (h + 1) * D_HEAD] = dx.astype(o_ref.dtype)
                dg = dg + dg_h
            dg_ref[...] += dg

    gain = pl.BlockSpec((1, D_HEAD), lambda i: (0, 0))
    both = pl.BlockSpec((ts, 2 * WIDTH), lambda i: (i, 0))
    one = pl.BlockSpec((ts, WIDTH), lambda i: (i, 0))
    return pl.pallas_call(
        body, name="fox_norm_bwd", grid=(S // ts,),
        in_specs=[both, one, one, gain, gain, pl.BlockSpec(memory_space=pl.ANY)],
        out_specs=[both, gain, gain],
        out_shape=[jax.ShapeDtypeStruct(dp.shape, dp.dtype), jax.ShapeDtypeStruct((1, D_HEAD), F32),
                   jax.ShapeDtypeStruct((1, D_HEAD), F32)],
        input_output_aliases={5: 0},
    )(p_all, d_qn, d_kn, qg, kg, dp)


def _chip_exchange(kind, x_ref, out_ref, send_sems, recv_sems):
    x, y, c = lax.axis_index("x"), lax.axis_index("y"), lax.axis_index("c")
    sends, arrivals = [], []
    for r in range(1, N_CHIPS):
        px, py = _flip(x, r & 2), _flip(y, r & 1)
        if kind == "scatter":
            src, dst, landed = x_ref.at[2 * px + py], out_ref.at[r - 1], out_ref.at[r - 1]
        else:
            src, dst, landed = x_ref, out_ref.at[2 * x + y], out_ref.at[2 * px + py]
        mk = functools.partial(pltpu.make_async_remote_copy, send_sem=send_sems.at[r - 1], recv_sem=recv_sems.at[r - 1],
                               device_id=(px, py, c), device_id_type=MESH_ID)
        sends.append(mk(src_ref=src, dst_ref=dst))
        arrivals.append(mk(src_ref=src, dst_ref=landed))
    return sends, arrivals


def _exchange_shape(kind, x):
    return jax.ShapeDtypeStruct(((N_CHIPS - 1,) + x.shape[1:]) if kind == "scatter" else ((N_CHIPS,) + x.shape), x.dtype)


def matmul(name, a, b, out_dtype, tm=512, tn=512, tk=2048, trans_a=False, trans_b=False, rides=()):
    M, K = a.shape[::-1] if trans_a else a.shape
    N, K2 = b.shape if trans_b else b.shape[::-1]
    assert K == K2
    tm, tn, tk = _tile(M, tm), _tile(N, tn), _tile(K, tk)
    ni, nj, nk = M // tm, N // tn, K // tk
    a_spec = pl.BlockSpec((tk, tm), lambda i, j, k: (k, i)) if trans_a else pl.BlockSpec((tm, tk), lambda i, j, k: (i, k))
    b_spec = pl.BlockSpec((tn, tk), lambda i, j, k: (j, k)) if trans_b else pl.BlockSpec((tk, tn), lambda i, j, k: (k, j))
    n_ride = len(rides)

    def body(a_ref, b_ref, *rest):
        ride_in, o_ref, ride_out = rest[:n_ride], rest[n_ride], rest[n_ride + 1:2 * n_ride + 1]
        scr = rest[2 * n_ride + 1:]
        i, j, k = pl.program_id(0), pl.program_id(1), pl.program_id(2)
        exchanges = [_chip_exchange(kind, ride_in[n], ride_out[n], scr[2 * n], scr[2 * n + 1])
                     for n, (kind, _) in enumerate(rides)]

        @pl.when((i == 0) & (j == 0) & (k == 0))
        def _():
            for sends, _ in exchanges:
                for cp in sends:
                    cp.start()

        part = lax.dot_general(a_ref[...], b_ref[...], (((0 if trans_a else 1,), (1 if trans_b else 0,)), ((), ())),
                               preferred_element_type=F32)
        if nk == 1:
            o_ref[...] = part.astype(o_ref.dtype)
        else:
            acc = scr[2 * n_ride]

            @pl.when(k == 0)
            def _():
                acc[...] = part

            @pl.when(k > 0)
            def _():
                acc[...] += part

            @pl.when(k == nk - 1)
            def _():
                o_ref[...] = acc[...].astype(o_ref.dtype)

        @pl.when((i == ni - 1) & (j == nj - 1) & (k == nk - 1))
        def _():
            for sends, arrivals in exchanges:
                for cp in arrivals:
                    cp.wait_recv()
                for cp in sends:
                    cp.wait_send()

    sems = [pltpu.SemaphoreType.DMA((N_CHIPS - 1,))] * (2 * n_ride)
    any_spec = pl.BlockSpec(memory_space=pl.ANY)
    res = pl.pallas_call(
        body, name=name, grid=(ni, nj, nk),
        in_specs=[a_spec, b_spec] + [any_spec] * n_ride,
        out_specs=[pl.BlockSpec((tm, tn), lambda i, j, k: (i, j))] + [any_spec] * n_ride,
        out_shape=[jax.ShapeDtypeStruct((M, N), out_dtype)] + [_exchange_shape(kind, x) for kind, x in rides],
        scratch_shapes=sems + ([] if nk == 1 else [pltpu.VMEM((tm, tn), F32)]),
        compiler_params=pltpu.CompilerParams(
            dimension_semantics=("arbitrary",) * 3 if rides else ("parallel", "parallel", "arbitrary")),
    )(a, b, *[x for _, x in rides])
    return tuple(res) if rides else res[0]


def _small_f(z, pv, av):
    lane = lax.broadcasted_iota(jnp.int32, z.shape, 1)
    zz = z + pv
    logf = _log_sigmoid(zz)
    gval = -jnp.exp(av) * _softplus(zz)
    beta = _sigmoid(zz)
    return jnp.where(lane < 8, logf, jnp.where(lane < 16, gval, jnp.where(lane < 24, beta, 0.0)))


def _tri(ts, upper):
    r = lax.broadcasted_iota(jnp.int32, (ts, ts), 0)
    c = lax.broadcasted_iota(jnp.int32, (ts, ts), 1)
    keep = (r <= c) if upper else (r >= c)
    same_chunk = (r // CHUNK) == (c // CHUNK)
    return keep.astype(F32), (keep & same_chunk).astype(F32)


def smalls_fwd(p_all, pv, av, S, ts):
    nt = S // ts

    def body(z_ref, pv_ref, av_ref, o_ref, carry):
        i = pl.program_id(0)

        @pl.when(i == 0)
        def _():
            carry[...] = jnp.zeros_like(carry)

        e = _small_f(z_ref[...], pv_ref[...], av_ref[...])
        lane = lax.broadcasted_iota(jnp.int32, e.shape, 1)
        l_all, l_chunk = _tri(ts, upper=False)
        cum_all = hdot(l_all, e) + carry[...]
        cum_chunk = hdot(l_chunk, e)
        carry[...] = cum_all[ts - 1:ts, :]
        o_ref[...] = jnp.where(lane < 8, cum_all, jnp.where(lane < 16, cum_chunk, e))

    return pl.pallas_call(
        body, name="smalls_fwd", grid=(nt,),
        in_specs=[pl.BlockSpec((ts, 128), lambda i: (i, CB_SMALL)), pl.BlockSpec((1, 128), lambda i: (0, 0)),
                  pl.BlockSpec((1, 128), lambda i: (0, 0))],
        out_specs=pl.BlockSpec((ts, 128), lambda i: (i, 0)),
        out_shape=jax.ShapeDtypeStruct((S, 128), F32),
        scratch_shapes=[pltpu.VMEM((1, 128), F32)],
    )(p_all, pv, av)


def smalls_bwd(p_all, pv, av, dsm, dp, S, ts):
    nt = S // ts

    def body(z_ref, pv_ref, av_ref, d_ref, dp_in, dz_ref, dpv_ref, dav_ref, carry):
        i = pl.program_id(0)

        @pl.when(i == 0)
        def _():
            carry[...] = jnp.zeros_like(carry)
            dpv_ref[...] = jnp.zeros_like(dpv_ref)
            dav_ref[...] = jnp.zeros_like(dav_ref)

        d = d_ref[...]
        lane = lax.broadcasted_iota(jnp.int32, d.shape, 1)
        u_all, u_chunk = _tri(ts, upper=True)
        rev_all = hdot(u_all, d) + carry[...]
        rev_chunk = hdot(u_chunk, d)
        carry[...] = rev_all[0:1, :]
        de = jnp.where(lane < 8, rev_all, jnp.where(lane < 16, rev_chunk, d))
        _, vjp = jax.vjp(_small_f, z_ref[...], pv_ref[...], av_ref[...])
        dz, dpv, dav = vjp(de)
        dz_ref[...] = dz.astype(dz_ref.dtype)
        dpv_ref[...] += dpv
        dav_ref[...] += dav

    rev = lambda i: (nt - 1 - i, 0)
    small_cols = pl.BlockSpec((ts, 128), lambda i: (nt - 1 - i, CB_SMALL))
    return pl.pallas_call(
        body, name="smalls_bwd", grid=(nt,),
        in_specs=[small_cols, pl.BlockSpec((1, 128), lambda i: (0, 0)), pl.BlockSpec((1, 128), lambda i: (0, 0)),
                  pl.BlockSpec((ts, 128), rev), pl.BlockSpec(memory_space=pl.ANY)],
        out_specs=[small_cols, pl.BlockSpec((1, 128), lambda i: (0, 0)), pl.BlockSpec((1, 128), lambda i: (0, 0))],
        out_shape=[jax.ShapeDtypeStruct(dp.shape, dp.dtype), jax.ShapeDtypeStruct((1, 128), F32),
                   jax.ShapeDtypeStruct((1, 128), F32)],
        scratch_shapes=[pltpu.VMEM((1, 128), F32)],
        input_output_aliases={4: 0},
    )(p_all, pv, av, dsm, dp)


def _col_to_row(col):
    return jnp.transpose(jnp.broadcast_to(col, (col.shape[0], 128)))[0:1, :]


def _row_to_col(row):
    return jnp.transpose(jnp.broadcast_to(row, (128, row.shape[1])))[:, 0:1]


def _causal_mask(s, keys_first):
    r = lax.broadcasted_iota(jnp.int32, s.shape, 0)
    c = lax.broadcasted_iota(jnp.int32, s.shape, 1)
    return jnp.where((r <= c) if keys_first else (c <= r), s, NEG)


def flash_fwd(qs, kn, vb, frow, S, tq):
    nq = S // tq
    tk = tq
    hp = 2
    wide = hp * D_HEAD
    head = lambda ref, h, rows=slice(None): ref[rows, h * D_HEAD:(h + 1) * D_HEAD]

    def body(q_ref, k_ref, v_ref, fr_ref, o_ref, lse_ref):
        i = pl.program_id(1)
        qs_ = [head(q_ref, h) for h in range(hp)]
        f_base = [fr_ref[h, i][:, 0:1] for h in range(hp)]

        def tile(j, carry, diagonal):
            rows = pl.ds(pl.multiple_of(j * tk, tk), tk)
            out = []
            for h in range(hp):
                m, l, acc = carry[3 * h:3 * h + 3]
                s = mm_nt(qs_[h], head(k_ref, h, rows)) - (fr_ref[h, j] - f_base[h])
                if diagonal:
                    s = _causal_mask(s, keys_first=False)
                m_new = jnp.maximum(m, jnp.max(s, axis=-1, keepdims=True))
                a = jnp.exp(m - m_new)
                p = jnp.exp(s - m_new)
                out += [m_new, a * l + jnp.sum(p, axis=-1, keepdims=True), a * acc + mm_nn(p, head(v_ref, h, rows))]
            return tuple(out)

        init = (jnp.full((tq, 1), NEG, F32), jnp.zeros((tq, 1), F32), jnp.zeros((tq, D_HEAD), F32)) * hp
        res = tile(i, lax.fori_loop(0, i, lambda j, c: tile(j, c, False), init), True)
        for h in range(hp):
            m, l, acc = res[3 * h:3 * h + 3]
            o_ref[:, h * D_HEAD:(h + 1) * D_HEAD] = acc / l
            lse_ref[h, 0] = _col_to_row(m + jnp.log(l))

    return pl.pallas_call(
        body, name="flash_fwd", grid=(N_HEADS // hp, nq),
        in_specs=[pl.BlockSpec((tq, wide), lambda h, i: (i, h)), pl.BlockSpec((S, wide), lambda h, i: (0, h)),
                  pl.BlockSpec((S, wide), lambda h, i: (0, h)),
                  pl.BlockSpec((hp, nq, 1, tk), lambda h, i: (h, 0, 0, 0))],
        out_specs=[pl.BlockSpec((tq, wide), lambda h, i: (i, h)),
                   pl.BlockSpec((hp, 1, 1, tq), lambda h, i: (h, i, 0, 0))],
        out_shape=[jax.ShapeDtypeStruct((S, WIDTH), F32), jax.ShapeDtypeStruct((N_HEADS, nq, 1, tq), F32)],
    )(qs, kn, vb, frow)


def flash_bwd(qs, kn, vb, do, frow, lse_row, delta_row, dp, S, tq):
    nq = S // tq
    tk = tq

    def body(q_ref, k_ref, v_ref, do_ref, fr_ref, lse_ref, dl_ref, dp_in, dq_ref, dk_ref, dv_ref, dfq_ref, dfk_ref):
        j = pl.program_id(1)
        k = k_ref[...]
        v = v_ref[...]
        fk = _row_to_col(fr_ref[0, j])

        @pl.when(j == 0)
        def _():
            dq_ref[...] = jnp.zeros_like(dq_ref)
            dfq_ref[...] = jnp.zeros_like(dfq_ref)

        def tile(i, carry, diagonal):
            dk, dv, dfk = carry
            off = pl.multiple_of(i * tq, tq)
            q = q_ref[pl.ds(off, tq), :]
            do_ = do_ref[pl.ds(off, tq), :]
            st = mm_nt(k, q) - (fk - fr_ref[0, i][:, 0:1])
            if diagonal:
                st = _causal_mask(st, keys_first=True)
            pt = jnp.exp(st - lse_ref[0, i])
            dst = pt * (mm_nt(v, do_) - dl_ref[0, i])
            dq_ref[pl.ds(off, tq), :] += mm_tn(dst, k) * ATT_SCALE
            dfq_ref[0, i] += jnp.sum(dst, axis=0, keepdims=True)
            return dk + mm_nn(dst, q), dv + mm_nn(pt, do_), dfk - jnp.sum(dst, axis=-1, keepdims=True)

        z = jnp.zeros((tk, D_HEAD), F32)
        carry = tile(j, (z, z, jnp.zeros((tk, 1), F32)), True)
        dk, dv, dfk = lax.fori_loop(j + 1, nq, lambda i, c: tile(i, c, False), carry)
        dk_ref[...] = dk
        dv_ref[...] = dv.astype(dv_ref.dtype)
        dfk_ref[0, 0] = _col_to_row(dfk)

    blk = pl.BlockSpec((tk, D_HEAD), lambda h, j: (j, h))
    full = pl.BlockSpec((S, D_HEAD), lambda h, j: (0, h))
    rows = pl.BlockSpec((1, nq, 1, tq), lambda h, j: (h, 0, 0, 0))
    stat = jax.ShapeDtypeStruct((N_HEADS, nq, 1, tq), F32)
    return pl.pallas_call(
        body, name="flash_bwd", grid=(N_HEADS, nq),
        in_specs=[full, blk, blk, full, rows, rows, rows, pl.BlockSpec(memory_space=pl.ANY)],
        out_specs=[full, blk, pl.BlockSpec((tk, D_HEAD), lambda h, j: (j, CB_FV + h)), rows,
                   pl.BlockSpec((1, 1, 1, tk), lambda h, j: (h, j, 0, 0))],
        out_shape=[jax.ShapeDtypeStruct((S, WIDTH), F32), jax.ShapeDtypeStruct((S, WIDTH), F32),
                   jax.ShapeDtypeStruct(dp.shape, dp.dtype), stat, stat],
        input_output_aliases={7: 2},
    )(qs, kn, vb, do, frow, lse_row, delta_row, dp)


CONV_COLS = 3 * WIDTH
CONV_TC = 512


def _gdn_act(y, group):
    s = _silu(y)
    if group == 2:
        return s
    return _l2n(s) * ATT_SCALE if group == 0 else _l2n(s)


def _gdn_act_bwd(y, d, group):
    _, vjp = jax.vjp(functools.partial(_gdn_act, group=group), y)
    return vjp(d)[0]


def conv_act_fwd(p_all, conv_w, S, ts):
    nt, tc = S // ts, CONV_TC
    cb0 = CB_GQ * 128 // tc
    r8 = ts // 8
    tiles_per_group = WIDTH // tc

    def body(x_ref, halo_ref, w_ref, y_ref, a_ref, scr):
        c, i = pl.program_id(0), pl.program_id(1)
        scr[0:8, :] = jnp.where(i > 0, halo_ref[...], 0.0)
        scr[8:8 + ts, :] = x_ref[...]
        w = w_ref[...]
        y = w[3:4, :] * x_ref[...]
        for j in range(CONV_K - 1):
            y = y + w[j:j + 1, :] * scr[5 + j:5 + j + ts, :]
        y_ref[...] = y
        for group in range(3):
            @pl.when(c // tiles_per_group == group)
            def _(group=group):
                for h in range(tc // D_HEAD):
                    sl = slice(h * D_HEAD, (h + 1) * D_HEAD)
                    a_ref[:, sl] = _gdn_act(y_ref[:, sl], group)

    out = pl.BlockSpec((ts, tc), lambda c, i: (i, c))
    return pl.pallas_call(
        body, name="conv_act_fwd", grid=(CONV_COLS // tc, nt),
        in_specs=[pl.BlockSpec((ts, tc), lambda c, i: (i, cb0 + c)),
                  pl.BlockSpec((8, tc), lambda c, i: (jnp.maximum(i * r8 - 1, 0), cb0 + c)),
                  pl.BlockSpec((CONV_K, tc), lambda c, i: (0, c))],
        out_specs=[out, out],
        out_shape=[jax.ShapeDtypeStruct((S, CONV_COLS), F32)] * 2,
        scratch_shapes=[pltpu.VMEM((ts + 8, tc), F32)],
    )(p_all, p_all, conv_w)


def conv_act_bwd(p_all, y, dqkv, conv_w, dp, S, ts):
    nt, tc = S // ts, CONV_TC
    cb0 = CB_GQ * 128 // tc
    r8 = ts // 8
    tiles_per_group = WIDTH // tc

    def body(x_ref, xh_ref, y_ref, yh_ref, d_ref, dh_ref, w_ref, dp_in, dx_ref, dw_ref, xs, ds):
        c, i = pl.program_id(0), pl.program_id(1)
        xs[0:8, :] = jnp.where(i > 0, xh_ref[...], 0.0)
        xs[8:8 + ts, :] = x_ref[...]
        for group in range(3):
            @pl.when(c // tiles_per_group == group)
            def _(group=group):
                for h in range(tc // D_HEAD):
                    sl = slice(h * D_HEAD, (h + 1) * D_HEAD)
                    ds[0:ts, sl] = _gdn_act_bwd(y_ref[:, sl], d_ref[:, sl], group)
                    ds[ts:ts + 8, sl] = jnp.where(i < nt - 1, _gdn_act_bwd(yh_ref[:, sl], dh_ref[:, sl], group), 0.0)
        w = w_ref[...]
        d = ds[0:ts, :]
        dx = w[3:4, :] * d
        for j in range(CONV_K - 1):
            dx = dx + w[j:j + 1, :] * ds[3 - j:3 - j + ts, :]
        dx_ref[...] = dx.astype(dx_ref.dtype)

        @pl.when(i == 0)
        def _():
            dw_ref[...] = jnp.zeros_like(dw_ref)

        rows = [jnp.sum(d * xs[5 + j:5 + j + ts, :], axis=0, keepdims=True) for j in range(CONV_K - 1)]
        rows.append(jnp.sum(d * x_ref[...], axis=0, keepdims=True))
        dw_ref[...] += jnp.concatenate(rows, axis=0)

    tile = pl.BlockSpec((ts, tc), lambda c, i: (i, c))
    next8 = pl.BlockSpec((8, tc), lambda c, i: (jnp.minimum((i + 1) * r8, nt * r8 - 1), c))
    return pl.pallas_call(
        body, name="conv_act_bwd", grid=(CONV_COLS // tc, nt),
        in_specs=[pl.BlockSpec((ts, tc), lambda c, i: (i, cb0 + c)),
                  pl.BlockSpec((8, tc), lambda c, i: (jnp.maximum(i * r8 - 1, 0), cb0 + c)),
                  tile, next8, tile, next8,
                  pl.BlockSpec((CONV_K, tc), lambda c, i: (0, c)), pl.BlockSpec(memory_space=pl.ANY)],
        out_specs=[pl.BlockSpec((ts, tc), lambda c, i: (i, cb0 + c)), pl.BlockSpec((CONV_K, tc), lambda c, i: (0, c))],
        out_shape=[jax.ShapeDtypeStruct(dp.shape, dp.dtype), jax.ShapeDtypeStruct((CONV_K, CONV_COLS), F32)],
        scratch_shapes=[pltpu.VMEM((ts + 8, tc), F32), pltpu.VMEM((ts + 8, tc), F32)],
        input_output_aliases={7: 0},
    )(p_all, p_all, y, y, dqkv, dqkv, conv_w, dp)


def _bdot(a, b, ca, cb):
    return lax.dot_general(a.astype(MXU_DTYPE), b.astype(MXU_DTYPE), (((ca,), (cb,)), ((0,), (0,))),
                           preferred_element_type=F32)


@jax.custom_vjp
def bmm_nn(a, b):
    return _bdot(a, b, 2, 1)


@jax.custom_vjp
def bmm_nt(a, b):
    return _bdot(a, b, 2, 2)


@jax.custom_vjp
def bmm_tn(a, b):
    return _bdot(a, b, 1, 1)


bmm_nn.defvjp(lambda a, b: (bmm_nn(a, b), (a, b)), lambda r, g: (bmm_nt(g, r[1]), bmm_tn(r[0], g)))
bmm_nt.defvjp(lambda a, b: (bmm_nt(a, b), (a, b)), lambda r, g: (bmm_nn(g, r[1]), bmm_tn(g, r[0])))
bmm_tn.defvjp(lambda a, b: (bmm_tn(a, b), (a, b)), lambda r, g: (bmm_nt(r[1], g), bmm_nn(r[0], g)))


def bdot_hi(a, b, ca=2, cb=1):
    return lax.dot_general(a, b, (((ca,), (cb,)), ((0,), (0,))), precision=SOLVE_PRECISION,
                           preferred_element_type=F32)


def _tri_inv_product(m):
    ii = lax.broadcasted_iota(jnp.int32, m.shape, 1)
    jj = lax.broadcasted_iota(jnp.int32, m.shape, 2)
    t = (ii == jj).astype(F32) - m
    pw = bdot_hi(m, m)
    for _ in range(4):
        t = t + bdot_hi(t, pw)
        pw = bdot_hi(pw, pw)
    return t + bdot_hi(t, pw)


def _tri_inv_bwd(t, g):
    return -bdot_hi(bdot_hi(t, g, 1, 1), t, 2, 2)


@jax.custom_vjp
def tri_inv(m):
    return _tri_inv_product(m)


@jax.custom_vjp
def tri_inv_known(m, t):
    return t


tri_inv.defvjp(lambda m: (lambda t: (t, t))(_tri_inv_product(m)), lambda t, g: (_tri_inv_bwd(t, g),))
tri_inv_known.defvjp(lambda m, t: (t, t), lambda t, g: (_tri_inv_bwd(t, g), jnp.zeros_like(t)))


def chunk_fn(q, k, v, gc_col, gc_row, beta, state, t_known=None):
    shape = (N_HEADS, CHUNK, CHUNK)
    ii = lax.broadcasted_iota(jnp.int32, shape, 1)
    jj = lax.broadcasted_iota(jnp.int32, shape, 2)
    lower = ii >= jj
    strict = ii > jj
    decay = jnp.exp(jnp.where(lower, gc_col - gc_row, NEG))
    kb = k * beta
    vb = v * beta
    m = jnp.where(strict, bmm_nt(kb, k) * decay, 0.0)
    t = tri_inv(m) if t_known is None else tri_inv_known(m, t_known)
    u = bdot_hi(t, vb)
    w = bdot_hi(t, kb * jnp.exp(gc_col))
    attn = jnp.where(lower, bmm_nt(q, k) * decay, 0.0)
    v_new = u - bmm_nn(w, state)
    o = bmm_nn(q * jnp.exp(gc_col), state) + bmm_nn(attn, v_new)
    g_last = gc_col[:, CHUNK - 1:CHUNK, :]
    k_dec = k * jnp.exp(g_last - gc_col)
    new_state = state * jnp.exp(g_last) + bmm_tn(k_dec, v_new)
    return (o, new_state, t) if t_known is None else (o, new_state)


GDN_CHUNKS_PER_STEP = 4


def _heads(ref, rows):
    return jnp.stack([ref[rows, h * D_HEAD:(h + 1) * D_HEAD] for h in range(N_HEADS)], axis=0)


def _head_cols(ref, rows, lane0):
    return jnp.stack([ref[rows, lane0 + h:lane0 + h + 1] for h in range(N_HEADS)], axis=0)


def gdn_fwd(qkv, smalls, gc_row, S):
    n = S // CHUNK
    per = _tile(n, GDN_CHUNKS_PER_STEP)
    rows_per = per * CHUNK

    def body(q_ref, k_ref, v_ref, sm_ref, gr_ref, o_ref, st_ref, t_ref, state):
        @pl.when(pl.program_id(0) == 0)
        def _():
            state[...] = jnp.zeros_like(state)

        s = state[...]
        for b in range(per):
            rows = pl.ds(b * CHUNK, CHUNK)
            st_ref[b] = s
            o, s, t = chunk_fn(_heads(q_ref, rows), _heads(k_ref, rows), _heads(v_ref, rows),
                               _head_cols(sm_ref, rows, 8), gr_ref[b][:, None, :], _head_cols(sm_ref, rows, 16), s)
            for h in range(N_HEADS):
                o_ref[rows, h * D_HEAD:(h + 1) * D_HEAD] = o[h]
            t_ref[b] = t
        state[...] = s

    return pl.pallas_call(
        body, name="gdn_chunk_fwd", grid=(n // per,),
        in_specs=[pl.BlockSpec((rows_per, WIDTH), lambda i, g=g: (i, g)) for g in range(3)]
        + [pl.BlockSpec((rows_per, 128), lambda i: (i, 0)), pl.BlockSpec((per, N_HEADS, CHUNK), lambda i: (i, 0, 0))],
        out_specs=[pl.BlockSpec((rows_per, WIDTH), lambda i: (i, 0)),
                   pl.BlockSpec((per, N_HEADS, D_HEAD, D_HEAD), lambda i: (i, 0, 0, 0)),
                   pl.BlockSpec((per, N_HEADS, CHUNK, CHUNK), lambda i: (i, 0, 0, 0))],
        out_shape=[jax.ShapeDtypeStruct((S, WIDTH), F32), jax.ShapeDtypeStruct((n, N_HEADS, D_HEAD, D_HEAD), F32),
                   jax.ShapeDtypeStruct((n, N_HEADS, CHUNK, CHUNK), F32)],
        scratch_shapes=[pltpu.VMEM((N_HEADS, D_HEAD, D_HEAD), F32)],
    )(qkv, qkv, qkv, smalls, gc_row)


def gdn_bwd(qkv, smalls, gc_row, states, tinv, do, S):
    n = S // CHUNK
    per = _tile(n, GDN_CHUNKS_PER_STEP)
    rows_per = per * CHUNK
    nb = n // per

    def body(q_ref, k_ref, v_ref, sm_ref, gr_ref, st_ref, t_ref, do_ref, dqkv_ref, dsm_ref, dgr_ref, dstate):
        @pl.when(pl.program_id(0) == 0)
        def _():
            dstate[...] = jnp.zeros_like(dstate)

        ds = dstate[...]
        for b in reversed(range(per)):
            rows = pl.ds(b * CHUNK, CHUNK)
            t_saved = t_ref[b]
            _, vjp = jax.vjp(lambda *a: chunk_fn(*a, t_known=t_saved), _heads(q_ref, rows), _heads(k_ref, rows),
                             _heads(v_ref, rows), _head_cols(sm_ref, rows, 8), gr_ref[b][:, None, :],
                             _head_cols(sm_ref, rows, 16), st_ref[b])
            dq, dk, dv, dgc, dgr, dbt, ds = vjp((_heads(do_ref, rows), ds))
            for h in range(N_HEADS):
                dqkv_ref[rows, h * D_HEAD:(h + 1) * D_HEAD] = dq[h]
                dqkv_ref[rows, WIDTH + h * D_HEAD:WIDTH + (h + 1) * D_HEAD] = dk[h]
                dqkv_ref[rows, 2 * WIDTH + h * D_HEAD:2 * WIDTH + (h + 1) * D_HEAD] = dv[h]
            cols = [dgc[h] for h in range(N_HEADS)] + [dbt[h] for h in range(N_HEADS)]
            dsm_ref[rows, :] = jnp.concatenate([jnp.zeros((CHUNK, 8), F32)] + cols
                                               + [jnp.zeros((CHUNK, 128 - 24), F32)], axis=1)
            dgr_ref[b] = jnp.concatenate([dgr[h] for h in range(N_HEADS)], axis=0)
        dstate[...] = ds

    rev = lambda c: (lambda i: (nb - 1 - i, c))
    rev4 = lambda i: (nb - 1 - i, 0, 0, 0)
    return pl.pallas_call(
        body, name="gdn_chunk_bwd", grid=(nb,),
        in_specs=[pl.BlockSpec((rows_per, WIDTH), rev(g)) for g in range(3)]
        + [pl.BlockSpec((rows_per, 128), rev(0)), pl.BlockSpec((per, N_HEADS, CHUNK), lambda i: (nb - 1 - i, 0, 0)),
           pl.BlockSpec((per, N_HEADS, D_HEAD, D_HEAD), rev4), pl.BlockSpec((per, N_HEADS, CHUNK, CHUNK), rev4),
           pl.BlockSpec((rows_per, WIDTH), rev(0))],
        out_specs=[pl.BlockSpec((rows_per, 3 * WIDTH), rev(0)), pl.BlockSpec((rows_per, 128), rev(0)),
                   pl.BlockSpec((per, N_HEADS, CHUNK), lambda i: (nb - 1 - i, 0, 0))],
        out_shape=[jax.ShapeDtypeStruct((S, 3 * WIDTH), F32)] + [jax.ShapeDtypeStruct((S, 128), F32),
                                                                  jax.ShapeDtypeStruct((n, N_HEADS, CHUNK), F32)],
        scratch_shapes=[pltpu.VMEM((N_HEADS, D_HEAD, D_HEAD), F32)],
    )(qkv, qkv, qkv, smalls, gc_row, states, tinv, do)


def _flip(a, bit):
    return 1 - a if bit else a


def allgather8(name, buf):
    R, W = buf.shape

    def body(x_ref, out_ref, send_sems, recv_sems, local_sem):
        x, y, c = lax.axis_index("x"), lax.axis_index("y"), lax.axis_index("c")

        def slot(px, py, pc):
            return out_ref.at[4 * px + 2 * py + pc]

        mine = pltpu.make_async_copy(x_ref, slot(x, y, c), local_sem)
        mine.start()
        sends = []
        for r in range(1, N_DEV):
            peer = (_flip(x, r & 4), _flip(y, r & 2), _flip(c, r & 1))
            cp = pltpu.make_async_remote_copy(src_ref=x_ref, dst_ref=slot(x, y, c), send_sem=send_sems.at[r - 1],
                                              recv_sem=recv_sems.at[r - 1], device_id=peer, device_id_type=MESH_ID)
            cp.start()
            sends.append(cp)
        for r in range(1, N_DEV):
            peer = (_flip(x, r & 4), _flip(y, r & 2), _flip(c, r & 1))
            pltpu.make_async_remote_copy(src_ref=x_ref, dst_ref=slot(*peer), send_sem=send_sems.at[r - 1],
                                         recv_sem=recv_sems.at[r - 1], device_id=peer,
                                         device_id_type=MESH_ID).wait_recv()
        for cp in sends:
            cp.wait_send()
        mine.wait()

    return pl.pallas_call(
        body, name=name,
        out_shape=jax.ShapeDtypeStruct((N_DEV, R, W), buf.dtype),
        in_specs=[pl.BlockSpec(memory_space=pltpu.VMEM)],
        out_specs=pl.BlockSpec(memory_space=pltpu.VMEM),
        scratch_shapes=[pltpu.SemaphoreType.DMA((N_DEV - 1,)), pltpu.SemaphoreType.DMA((N_DEV - 1,)),
                        pltpu.SemaphoreType.DMA],
    )(buf)


def allgather_chips(name, blk):
    R, W = blk.shape
    half = R // 2
    n_far = N_CHIPS - 1

    def body(x_ref, out_ref, send_sems, recv_sems):
        x, y, c = lax.axis_index("x"), lax.axis_index("y"), lax.axis_index("c")
        sibling = (x, y, 1 - c)
        chips = [(_flip(x, r & 2), _flip(y, r & 1)) for r in range(1, N_CHIPS)]

        def rows(px, py, pc):
            return out_ref.at[2 * px + py, pc]

        def copy(k, dst, to, src):
            return pltpu.make_async_remote_copy(src_ref=src, dst_ref=dst, send_sem=send_sems.at[k],
                                                recv_sem=recv_sems.at[k], device_id=to, device_id_type=MESH_ID)

        my_half = x_ref.at[c]
        first = [copy(j, rows(x, y, c), (*chip, c), my_half) for j, chip in enumerate(chips)]
        for cp in first:
            cp.start()
        passed = [copy(n_far + j, rows(*chip, c), sibling, rows(*chip, c)) for j, chip in enumerate(chips)]
        for j, chip in enumerate(chips):
            copy(j, rows(*chip, c), (*chip, c), my_half).wait_recv()
            passed[j].start()
        for j, chip in enumerate(chips):
            copy(n_far + j, rows(*chip, 1 - c), sibling, my_half).wait_recv()
        for cp in first + passed:
            cp.wait_send()

    far = pl.pallas_call(
        body, name=name,
        out_shape=jax.ShapeDtypeStruct((N_CHIPS, 2, half, W), blk.dtype),
        in_specs=[pl.BlockSpec(memory_space=pl.ANY)],
        out_specs=pl.BlockSpec(memory_space=pl.ANY),
        scratch_shapes=[pltpu.SemaphoreType.DMA((2 * n_far,)), pltpu.SemaphoreType.DMA((2 * n_far,))],
    )(blk.reshape(2, half, W)).reshape(N_CHIPS, R, W)
    own_chip = 2 * lax.axis_index("x") + lax.axis_index("y")
    return lax.dynamic_update_index_in_dim(far, blk, own_chip, axis=0)


def sibling_send_other_half(name, g4):
    n, R, W = g4.shape
    half = R // 2

    def body(x_ref, out_ref, send_sem, recv_sem):
        x, y, c = lax.axis_index("x"), lax.axis_index("y"), lax.axis_index("c")
        cp = pltpu.make_async_remote_copy(src_ref=x_ref.at[:, pl.ds((1 - c) * half, half), :], dst_ref=out_ref,
                                          send_sem=send_sem, recv_sem=recv_sem, device_id=(x, y, 1 - c),
                                          device_id_type=MESH_ID)
        cp.start()
        cp.wait_recv()
        cp.wait_send()

    return pl.pallas_call(
        body, name=name,
        out_shape=jax.ShapeDtypeStruct((n, half, W), g4.dtype),
        in_specs=[pl.BlockSpec(memory_space=pl.ANY)],
        out_specs=pl.BlockSpec(memory_space=pl.ANY),
        scratch_shapes=[pltpu.SemaphoreType.DMA, pltpu.SemaphoreType.DMA],
    )(g4)


def sibling_merge(name, mine):
    def body(x_ref, out_ref, send_sem, recv_sem):
        x, y, c = lax.axis_index("x"), lax.axis_index("y"), lax.axis_index("c")
        cp = pltpu.make_async_remote_copy(src_ref=x_ref, dst_ref=out_ref, send_sem=send_sem, recv_sem=recv_sem,
                                          device_id=(x, y, 1 - c), device_id_type=MESH_ID)
        cp.start()
        cp.wait_recv()
        cp.wait_send()

    other = pl.pallas_call(
        body, name=name,
        out_shape=jax.ShapeDtypeStruct(mine.shape, mine.dtype),
        in_specs=[pl.BlockSpec(memory_space=pl.ANY)],
        out_specs=pl.BlockSpec(memory_space=pl.ANY),
        scratch_shapes=[pltpu.SemaphoreType.DMA, pltpu.SemaphoreType.DMA],
    )(mine)
    first = lax.axis_index("c") == 0
    return jnp.concatenate([jnp.where(first, mine, other), jnp.where(first, other, mine)], axis=0)


def sum_parts(name, own, recv, tr):
    R, W = own.shape
    tr = _tile(R, tr)

    def body(o_ref, r_ref, out_ref):
        acc = o_ref[...]
        for k in range(recv.shape[0]):
            acc = acc + r_ref[k].astype(F32)
        out_ref[...] = acc

    return pl.pallas_call(
        body, name=name, grid=(R // tr,),
        in_specs=[pl.BlockSpec((tr, W), lambda i: (i, 0)), pl.BlockSpec((recv.shape[0], tr, W), lambda i: (0, i, 0))],
        out_specs=pl.BlockSpec((tr, W), lambda i: (i, 0)),
        out_shape=jax.ShapeDtypeStruct((R, W), F32),
    )(own, recv)


def sum_own_half(name, g4, recv, c_idx, tr, rounded=True):
    n, R, W = g4.shape
    half = R // 2
    tr = _tile(half, tr)
    nb = half // tr

    def body(c_ref, g_ref, r_ref, o_ref, *ob_ref):
        s = g_ref[0] + r_ref[0]
        o_ref[0] = s
        if rounded:
            ob_ref[0][0] = s.astype(ob_ref[0].dtype)

    mine = pl.BlockSpec((1, tr, W), lambda j, i, c: (j, i, 0))
    shapes = [jax.ShapeDtypeStruct((n, half, W), F32)] + ([jax.ShapeDtypeStruct((n, half, W), MXU_DTYPE)] * rounded)
    return pl.pallas_call(
        body, name=name,
        grid_spec=pltpu.PrefetchScalarGridSpec(
            num_scalar_prefetch=1, grid=(n, nb),
            in_specs=[pl.BlockSpec((1, tr, W), lambda j, i, c: (j, c[0] * nb + i, 0)), mine],
            out_specs=[mine] * len(shapes)),
        out_shape=shapes,
    )(c_idx, g4, recv)


def sum_devices(name, g):
    def body(g_ref, o_ref):
        acc = g_ref[0]
        for d in range(1, N_DEV):
            acc = acc + g_ref[d]
        o_ref[...] = acc

    return pl.pallas_call(body, name=name, out_shape=jax.ShapeDtypeStruct(g.shape[1:], F32))(g)


def ada_fwd(c_all, w_ada):
    K, N = w_ada.shape
    tn = _tile(N, 512)

    def body(c_ref, w_ref, o_ref):
        o_ref[...] = mm_nn(_silu(c_ref[...]), w_ref[...])

    return pl.pallas_call(
        body, name="ada_fwd", grid=(N // tn,),
        in_specs=[pl.BlockSpec((N_DEV, K), lambda j: (0, 0)), pl.BlockSpec((K, tn), lambda j: (0, j))],
        out_specs=pl.BlockSpec((N_DEV, tn), lambda j: (0, j)),
        out_shape=jax.ShapeDtypeStruct((N_DEV, N), F32),
    )(c_all, w_ada)


def ada_grad(c_all, dmod):
    K = c_all.shape[1]
    N = dmod.shape[1]
    tn = _tile(N, 512)

    def body(c_ref, d_ref, o_ref):
        o_ref[...] = mm_tn(_silu(c_ref[...]), d_ref[...])

    return pl.pallas_call(
        body, name="ada_grad", grid=(N // tn,),
        in_specs=[pl.BlockSpec((N_DEV, K), lambda j: (0, 0)), pl.BlockSpec((N_DEV, tn), lambda j: (0, j))],
        out_specs=pl.BlockSpec((K, tn), lambda j: (0, j)),
        out_shape=jax.ShapeDtypeStruct((K, N), F32),
    )(c_all, dmod)


def adamw(name, w, gparts, m, v, tr, tc=None):
    R, W = w.shape
    tr = R if tc else _tile(R, tr)
    ng = len(gparts)

    def body(w_ref, *refs):
        g_refs, (m_ref, v_ref, g_out, d_out, m_out, v_out) = refs[:ng], refs[ng:]
        g = g_refs[0][...]
        for r in g_refs[1:]:
            g = g + r[...]
        g_out[...] = g
        d_out[...], m_out[...], v_out[...] = _adamw_update(w_ref[...], g, m_ref[...], v_ref[...])

    spec = pl.BlockSpec((R, tc), lambda i: (0, i)) if tc else pl.BlockSpec((tr, W), lambda i: (i, 0))
    return pl.pallas_call(
        body, name=name, grid=(W // tc if tc else R // tr,),
        in_specs=[spec] * (3 + ng), out_specs=[spec] * 4,
        out_shape=[jax.ShapeDtypeStruct((R, W), F32)] * 4,
    )(w, *gparts, m, v)


def _adamw_update(w, g, m, v):
    m1 = ADAM_B1 * m + (1.0 - ADAM_B1) * g
    v1 = ADAM_B2 * v + (1.0 - ADAM_B2) * jnp.square(g)
    m_hat = m1 / (1.0 - ADAM_B1 ** ADAM_STEP)
    v_hat = v1 / (1.0 - ADAM_B2 ** ADAM_STEP)
    return -ADAM_LR * (m_hat / (jnp.sqrt(v_hat) + ADAM_EPS) + ADAM_WD * w), m1, v1


def adamw_small(name, ws, gs, ms, vs):
    n = len(ws)

    def body(*refs):
        for k in range(n):
            w_ref, g_ref, m_ref, v_ref = (refs[j * n + k] for j in range(4))
            outs = refs[4 * n + 3 * k:4 * n + 3 * k + 3]
            for o_ref, val in zip(outs, _adamw_update(w_ref[...], g_ref[...], m_ref[...], v_ref[...])):
                o_ref[...] = val

    return pl.pallas_call(
        body, name=name, out_shape=[jax.ShapeDtypeStruct(w.shape, F32) for w in ws for _ in range(3)],
    )(*ws, *gs, *ms, *vs)


_REF_SEGMENTS = ((0, 4 * WIDTH, 0), (4 * WIDTH, 4 * WIDTH + 8, 4 * WIDTH), (4 * WIDTH + 8, 8 * WIDTH + 8, -8),
                 (8 * WIDTH + 8, IN_WIDTH, 0))


def _internal_from_blocks(blocks, n):
    a = 4 * WIDTH
    pieces = []
    for lo, hi in ((0, a), (a + 8, 2 * a + 8), (a, a + 8), (2 * a + 8, IN_WIDTH)):
        for j in range(N_CHIPS):
            s, e = max(lo, j * n), min(hi, (j + 1) * n)
            if s < e:
                pieces.append(blocks[j][:, s - j * n:e - j * n])
    pieces.append(jnp.zeros(blocks.shape[1:2] + (128 - N_SMALL,), blocks.dtype))
    return jnp.concatenate(pieces, axis=1)


def _block_from_internal(g, j, n):
    pieces = []
    for lo, hi, shift in _REF_SEGMENTS:
        s, e = max(lo, j * n), min(hi, (j + 1) * n)
        if s < e:
            pieces.append(g[:, s + shift:e + shift])
    return jnp.concatenate(pieces, axis=1)


def _pad_lanes(v, n=128):
    return jnp.pad(v, ((0, 0), (0, n - v.shape[1])))


def kernel(x, c, norm_g, w_ada, b_ada, w_in, b_fgate, fox_qn_g, fox_kn_g, gdn_conv_w, gdn_A_log, gdn_dt_bias, gdn_norm_g, w_out, final_g, loss_target, m_norm_g, m_w_ada, m_b_ada, m_w_in, m_b_fgate, m_fox_qn_g, m_fox_kn_g, m_gdn_conv_w, m_gdn_A_log, m_gdn_dt_bias, m_gdn_norm_g, m_w_out, m_final_g, v_norm_g, v_w_ada, v_b_ada, v_w_in, v_b_fgate, v_fox_qn_g, v_fox_kn_g, v_gdn_conv_w, v_gdn_A_log, v_gdn_dt_bias, v_gdn_norm_g, v_w_out, v_final_g):
    S = x.shape[1]
    H = N_HEADS
    ix, iy, ic = lax.axis_index("x"), lax.axis_index("y"), lax.axis_index("c")
    chip = 2 * ix + iy
    me = 2 * chip + ic
    x2, tgt = x[0], loss_target[0]
    ts_wide = _tile(S, 256)
    ts = _tile(S, 512)
    ts_conv = _tile(S, 1024)
    ts_head = _tile(S, 2048)
    tq = _tile(S, 1024)
    nq = S // tq
    nch = S // CHUNK
    conv_cols_chip = CONV_COLS // N_CHIPS

    pay = jnp.concatenate([c, _pad_lanes(gdn_conv_w[0], D_MODEL), jnp.zeros((3, D_MODEL), F32)], axis=0)
    g1 = allgather8("ag_c", pay)
    c_all = g1[:, 0, :]
    conv_w = jnp.concatenate([g1[2 * j, 1:1 + CONV_K, :conv_cols_chip] for j in range(N_CHIPS)], axis=1)
    g2 = allgather8("ag_mod", ada_fwd(c_all, w_ada[0]))
    mod_rows = jnp.concatenate([g2[2 * j] for j in range(N_CHIPS)], axis=1)
    mod = lax.dynamic_slice_in_dim(mod_rows, me, 1, axis=0) + b_ada
    shift, scale, gate = mod[:, :D_MODEL], mod[:, D_MODEL:2 * D_MODEL], mod[:, 2 * D_MODEL:]

    n_in = w_in.shape[2]
    win4 = allgather_chips("ag_w_in", w_in[0].astype(MXU_DTYPE))
    w_all = _internal_from_blocks(win4, n_in)
    w_out_b = w_out[0].astype(MXU_DTYPE)

    (h_b,) = rowwise_fwd("prologue", f_prologue, S, ts_wide, 1, [(x2, "row", D_MODEL, 0)], [norm_g, scale, shift],
                         [("row", D_MODEL, 0, D_MODEL, MXU_DTYPE)])
    p_all, wout_far = matmul("in_proj", h_b, w_all, F32, tm=1024, tn=1664, tk=2048, rides=[("gather", w_out_b)])
    w_out_full = lax.dynamic_update_index_in_dim(wout_far, w_out_b, chip, axis=0).reshape(2 * WIDTH, D_MODEL)
    pv = jnp.concatenate([b_fgate, gdn_dt_bias, jnp.zeros((1, 128 - 16), F32)], axis=1)
    av = jnp.concatenate([jnp.zeros((1, 8), F32), gdn_A_log, jnp.zeros((1, 128 - 16), F32)], axis=1)
    smalls = smalls_fwd(p_all, pv, av, S, ts)
    frow = jnp.transpose(smalls[:, :H]).reshape(H, nq, 1, tq)
    gc_row = jnp.transpose(smalls[:, 8:16].reshape(nch, CHUNK, H), (0, 2, 1))
    head_b = ("head", D_HEAD, 0, WIDTH, MXU_DTYPE)
    head_f = ("head", D_HEAD, 0, WIDTH, F32)
    pcol = lambda cb: (p_all, "head", D_HEAD, cb)
    qn, kn, vb = rowwise_fwd("fox_pre", f_foxpre, S, ts_head, H,[pcol(CB_FQ), pcol(CB_FK), pcol(CB_FV)],
                             [fox_qn_g, fox_kn_g], [head_b] * 3)
    o_fox, lse = flash_fwd(qn, kn, vb, frow, S, tq)
    y_conv, qkv = conv_act_fwd(p_all, conv_w, S, ts_conv)
    o_gdn, states, tinv = gdn_fwd(qkv, smalls, gc_row, S)
    mixed = lax.empty((S, 2 * WIDTH), MXU_DTYPE)
    (mixed,) = rowwise_fwd("post_fox", f_postf, S, ts_head, H,[(o_fox, "head", D_HEAD, 0), pcol(CB_FZ)], [],
                           [("head", D_HEAD, 0, 2 * WIDTH, MXU_DTYPE, mixed)])
    (mixed,) = rowwise_fwd("post_gdn", f_postg, S, ts_head, H,[(o_gdn, "head", D_HEAD, 0), pcol(CB_GZ)], [gdn_norm_g],
                           [("head", D_HEAD, H, 2 * WIDTH, MXU_DTYPE, mixed)])
    mo = matmul("out_proj", mixed, w_out_full, F32, tm=1024, tn=2048, tk=2048)

    wide = lambda a: (a, "row", D_MODEL, 0)
    dx_res, dmo, d_gate, d_final_g, loss_acc = rowwise_bwd(
        "loss", f_loss, S, ts_wide, 1, [wide(x2), wide(mo), wide(tgt)], [gate, final_g[None, :]], [],
        [(0, "row", D_MODEL, 0, D_MODEL, F32), (1, "row", D_MODEL, 0, D_MODEL, MXU_DTYPE)], primal_sum=True)

    d_mixed = matmul("d_mixed", dmo, w_out_full, F32, tm=1024, tn=2048, tk=2048, trans_b=True)
    g_w_out = matmul("g_w_out", mixed, dmo, F32, tm=1024, tn=2048, tk=2048, trans_a=True)
    dp = lax.empty((S, P_COLS), MXU_DTYPE)
    into_dp = lambda idx, cb: (idx, "head", D_HEAD, cb, P_COLS, MXU_DTYPE, dp)
    do_fox, dp, delta = rowwise_bwd(
        "post_fox_bwd", f_postf, S, ts_head, H,[(o_fox, "head", D_HEAD, 0), pcol(CB_FZ)], [],
        [(d_mixed, "head", D_HEAD, 0)], [(0,) + head_b, into_dp(1, CB_FZ)],
        extra=lambda vals, cv, grads: (jnp.sum(grads[0].astype(MXU_DTYPE).astype(F32) * vals[0], axis=-1,
                                               keepdims=True),),
        extra_outs=[("stat", tq, 0, None, F32)])
    do_gdn, dp, d_gdn_norm_g = rowwise_bwd(
        "post_gdn_bwd", f_postg, S, ts_head, H,[(o_gdn, "head", D_HEAD, 0), pcol(CB_GZ)], [gdn_norm_g],
        [(d_mixed, "head", D_HEAD, 8)], [(0,) + head_f, into_dp(1, CB_GZ)])
    d_qn, d_kn, dp, dfq_row, dfk_row = flash_bwd(qn, kn, vb, do_fox, frow, lse, delta, dp, S, tq)
    dp, d_qn_g, d_kn_g = fox_norm_bwd(p_all, d_qn, d_kn, fox_qn_g, fox_kn_g, dp, S, ts_wide)
    d_qkv, dsm_chunk, dgc_row = gdn_bwd(qkv, smalls, gc_row, states, tinv, do_gdn, S)
    dp, d_conv_w = conv_act_bwd(p_all, y_conv, d_qkv, conv_w, dp, S, ts_conv)
    d_f = jnp.transpose((dfq_row + dfk_row).reshape(H, S))
    d_gc = jnp.transpose(dgc_row, (0, 2, 1)).reshape(S, H)
    dsm = dsm_chunk + jnp.concatenate([d_f, d_gc, jnp.zeros((S, 128 - 16), F32)], axis=1)
    dp, d_pv, d_av = smalls_bwd(p_all, pv, av, dsm, dp, S, ts)
    g_w_all = matmul("g_w_in", h_b, dp, F32, tm=1024, tn=1664, tk=2048, trans_a=True)

    c_idx = jnp.reshape(ic, (1,)).astype(jnp.int32)

    def chip_sums(tag, g4):
        recv = sibling_send_other_half("d2d_" + tag, g4)
        return sum_own_half("sum2_" + tag, g4, recv, c_idx, 128)

    def finish(tag, p4, far):
        own = lax.dynamic_index_in_dim(p4, chip, axis=0, keepdims=False)
        return sibling_merge("merge_" + tag, sum_parts("sum_" + tag, own, far, 128))

    g_int = g_w_all[None]
    (p_int,) = sum_own_half("sum2_w_in", g_int, sibling_send_other_half("d2d_w_in", g_int), c_idx, 128, rounded=False)
    p4_in = jnp.stack([_block_from_internal(p_int[0], j, n_in) for j in range(N_CHIPS)])
    p4_in_sent = p4_in.astype(MXU_DTYPE)
    p4_out, p4_out_sent = chip_sums("w_out", g_w_out.reshape(N_CHIPS, w_out.shape[1], D_MODEL))
    d_h, far_in, far_out = matmul("d_h", dp, w_all, F32, tm=1024, tn=2048, tk=1664, trans_b=True,
                                  rides=[("scatter", p4_in_sent), ("scatter", p4_out_sent)])
    g_w_in_full = finish("w_in", p4_in, far_in)
    g_w_out_full = finish("w_out", p4_out, far_out)
    d_x, d_norm_g, d_scale, d_shift = rowwise_bwd(
        "prologue_bwd", f_prologue_res, S, ts_wide, 1, [wide(x2)], [norm_g, scale, shift],
        [wide(d_h), wide(dx_res)], [(0, "row", D_MODEL, 0, D_MODEL, F32)])

    parts = [d_shift, d_scale, d_gate, d_norm_g, d_final_g, d_conv_w.reshape(1, CONV_K * CONV_COLS), d_qn_g, d_kn_g,
             d_gdn_norm_g, d_pv, d_av, loss_acc]
    n_pay = sum(p.shape[1] for p in parts)
    pay_w = -(-n_pay // 1024) * 1024
    pay2 = jnp.concatenate(parts + [jnp.zeros((1, pay_w - n_pay), F32)], axis=1).reshape(8, pay_w // 8)
    g3 = allgather8("ag_small", pay2)
    tot = sum_devices("sum_small", g3).reshape(1, pay_w)
    dmod_all = g3.reshape(N_DEV, pay_w)[:, :3 * D_MODEL]
    o = 3 * D_MODEL
    g_b_ada = tot[:, :o]
    g_norm_g, g_final_g = tot[:, o:o + D_MODEL], tot[:, o + D_MODEL:o + 2 * D_MODEL]
    o += 2 * D_MODEL
    g_conv_full = tot[:, o:o + CONV_K * CONV_COLS].reshape(CONV_K, CONV_COLS)
    g_conv = lax.dynamic_slice_in_dim(g_conv_full, chip * conv_cols_chip, conv_cols_chip, axis=1)
    o += CONV_K * CONV_COLS
    g_qn_g, g_kn_g, g_gdn_ng = tot[:, o:o + 128], tot[:, o + 128:o + 256], tot[:, o + 256:o + 384]
    g_pv, g_av = tot[:, o + 384:o + 512], tot[:, o + 512:o + 640]
    loss = tot[0, o + 640]
    g_b_fgate, g_dt_bias, g_a_log = g_pv[:, :8], g_pv[:, 8:16], g_av[:, 8:16]

    small_g = [g_norm_g, g_b_ada, g_final_g, g_conv, g_qn_g, g_kn_g, g_gdn_ng, g_b_fgate, g_a_log, g_dt_bias]
    small_upd = adamw_small(
        "adamw_small",
        [norm_g, b_ada, final_g[None, :], gdn_conv_w[0], fox_qn_g, fox_kn_g, gdn_norm_g, b_fgate, gdn_A_log, gdn_dt_bias],
        small_g,
        [m_norm_g, m_b_ada, m_final_g[None, :], m_gdn_conv_w[0], m_fox_qn_g, m_fox_kn_g, m_gdn_norm_g, m_b_fgate,
         m_gdn_A_log, m_gdn_dt_bias],
        [v_norm_g, v_b_ada, v_final_g[None, :], v_gdn_conv_w[0], v_fox_qn_g, v_fox_kn_g, v_gdn_norm_g, v_b_fgate,
         v_gdn_A_log, v_gdn_dt_bias])

    def small_leaves(k):
        vals = small_g if k == 0 else [small_upd[3 * p + k - 1] for p in range(len(small_g))]
        return [v[0] if p == 2 else (v[None] if p == 3 else v) for p, v in enumerate(vals)]

    n_ada = w_ada.shape[2]
    g_w_ada = ada_grad(c_all, lax.dynamic_slice_in_dim(dmod_all, chip * n_ada, n_ada, axis=1))
    ada_out = adamw("adamw_w_ada", w_ada[0], [g_w_ada], m_w_ada[0], v_w_ada[0], 128)

    in_out = [jnp.transpose(o) for o in adamw("adamw_w_in", jnp.transpose(w_in[0]), [jnp.transpose(g_w_in_full)],
                                              jnp.transpose(m_w_in[0]), jnp.transpose(v_w_in[0]), None, tc=256)]
    out_out = adamw("adamw_w_out", w_out[0], [g_w_out_full], m_w_out[0], v_w_out[0], 128)

    res = [loss, d_x[None]]
    for k in range(4):
        s = small_leaves(k)
        big = [ada_out[k][None], in_out[k][None], out_out[k][None]]
        res += [s[0], big[0], s[1], big[1], s[7], s[4], s[5], s[3], s[8], s[9], s[6], big[2], s[2]]
    return tuple(res)
```

```python
import functools

import jax
import jax.numpy as jnp
from jax import lax
from jax.experimental import pallas as pl
from jax.experimental.pallas import tpu as pltpu

F32 = jnp.float32
MXU_DTYPE = jnp.bfloat16
HIGHEST = lax.Precision.HIGHEST
SOLVE_PRECISION = lax.Precision.HIGH
MESH_ID = pl.DeviceIdType.MESH

D_MODEL = 2048
N_HEADS = 8
D_HEAD = 128
WIDTH = N_HEADS * D_HEAD
CHUNK = 64
CONV_K = 4
EPS = 1e-6
NEG = -1e30
ATT_SCALE = D_HEAD ** -0.5
N_SMALL = 3 * N_HEADS
P_COLS = 8 * WIDTH + 128
CB_FQ, CB_FK, CB_FV, CB_FZ, CB_GQ, CB_GK, CB_GV, CB_GZ, CB_SMALL = 0, 8, 16, 24, 32, 40, 48, 56, 64
IN_WIDTH = 8 * WIDTH + N_SMALL
N_CHIPS = 4
N_DEV = 8

ADAM_LR, ADAM_B1, ADAM_B2, ADAM_EPS, ADAM_WD, ADAM_STEP = 0.001, 0.9, 0.999, 1e-08, 0.01, 10


def _dotg(a, b, dims):
    return lax.dot_general(a.astype(MXU_DTYPE), b.astype(MXU_DTYPE), (dims, ((), ())), preferred_element_type=F32)


@jax.custom_vjp
def mm_nn(a, b):
    return _dotg(a, b, ((1,), (0,)))


@jax.custom_vjp
def mm_nt(a, b):
    return _dotg(a, b, ((1,), (1,)))


@jax.custom_vjp
def mm_tn(a, b):
    return _dotg(a, b, ((0,), (0,)))


mm_nn.defvjp(lambda a, b: (mm_nn(a, b), (a, b)), lambda r, g: (mm_nt(g, r[1]), mm_tn(r[0], g)))
mm_nt.defvjp(lambda a, b: (mm_nt(a, b), (a, b)), lambda r, g: (mm_nn(g, r[1]), mm_tn(g, r[0])))
mm_tn.defvjp(lambda a, b: (mm_tn(a, b), (a, b)), lambda r, g: (mm_nt(r[1], g), mm_nn(r[0], g)))


def hdot(a, b):
    return lax.dot_general(a, b, (((1,), (0,)), ((), ())), precision=HIGHEST, preferred_element_type=F32)


def _sigmoid(x):
    return 0.5 * (jnp.tanh(0.5 * x) + 1.0)


def _silu(x):
    return x * _sigmoid(x)


def _softplus(x):
    return jnp.maximum(x, 0.0) + jnp.log(1.0 + jnp.exp(-jnp.abs(x)))


def _log_sigmoid(x):
    return jnp.minimum(x, 0.0) - jnp.log(1.0 + jnp.exp(-jnp.abs(x)))


def _rms(x, g):
    return x * lax.rsqrt(jnp.mean(x * x, axis=-1, keepdims=True) + EPS) * g


def _l2n(x):
    return x * lax.rsqrt(jnp.sum(x * x, axis=-1, keepdims=True) + EPS)


def _tile(n, pref):
    t = min(n, pref)
    assert n % t == 0, (n, pref)
    return t


def _bspec(kind, w, col0, ts):
    if kind == "row":
        return pl.BlockSpec((ts, w), lambda i, h: (i, col0))
    if kind == "head":
        return pl.BlockSpec((ts, w), lambda i, h: (i, col0 + h))
    assert kind == "stat"
    return pl.BlockSpec((1, ts // w, 1, w), lambda i, h: (h, i, 0, 0))


def _ld(ref, kind):
    return ref[...]


def _st(ref, kind, val):
    if kind == "stat":
        w = ref.shape[3]
        for n in range(ref.shape[1]):
            ref[0, n] = _col_to_row(val[n * w:(n + 1) * w, :])
    else:
        ref[...] = val.astype(ref.dtype)


def _full_spec(a):
    nd = a.ndim
    return pl.BlockSpec(a.shape, lambda i, h: (0,) * nd)


def _out_struct(kind, w, S, width_total, dtype, nh):
    if kind == "stat":
        return jax.ShapeDtypeStruct((nh, S // w, 1, w), dtype)
    return jax.ShapeDtypeStruct((S, width_total), dtype)


def rowwise_fwd(name, f, S, ts, nh, ins, params, outs):
    n_in, n_p = len(ins), len(params)
    into = {j: o[5] for j, o in enumerate(outs) if len(o) > 5}
    outs = [o[:5] for o in outs]
    n_al = len(into)

    def body(*refs):
        vals = [_ld(r, s[1]).astype(F32) for r, s in zip(refs[:n_in], ins)]
        pv = [r[...] for r in refs[n_in:n_in + n_p]]
        res = f(*vals, *pv)
        for r, s, o in zip(refs[n_in + n_p + n_al:], outs, res):
            _st(r, s[0], o)

    return pl.pallas_call(
        body, name=name, grid=(S // ts, nh),
        in_specs=[_bspec(k, w, c0, ts) for (_, k, w, c0) in ins] + [_full_spec(p) for p in params]
        + [pl.BlockSpec(memory_space=pl.ANY)] * n_al,
        out_specs=[_bspec(k, w, c0, ts) for (k, w, c0, _, _) in outs],
        out_shape=[jax.ShapeDtypeStruct(into[j].shape, into[j].dtype) if j in into else _out_struct(k, w, S, tw, dt, nh)
                   for j, (k, w, c0, tw, dt) in enumerate(outs)],
        input_output_aliases={n_in + n_p + n: j for n, j in enumerate(sorted(into))},
    )(*[a for (a, _, _, _) in ins], *params, *[into[j] for j in sorted(into)])


def rowwise_bwd(name, f, S, ts, nh, ins, params, cts, row_grads, extra=None, extra_outs=(), primal_sum=False):
    n_in, n_p, n_ct = len(ins), len(params), len(cts)
    n_rg, n_ex = len(row_grads), len(extra_outs)
    into = {j: rg[6] for j, rg in enumerate(row_grads) if len(rg) > 6}
    row_grads = [rg[:6] for rg in row_grads]
    n_al = len(into)

    def body(*refs):
        first = (pl.program_id(0) == 0) & (pl.program_id(1) == 0)
        in_refs = refs[:n_in]
        p_refs = refs[n_in:n_in + n_p]
        ct_refs = refs[n_in + n_p:n_in + n_p + n_ct]
        o = n_in + n_p + n_ct + n_al
        rg_refs = refs[o:o + n_rg]
        ex_refs = refs[o + n_rg:o + n_rg + n_ex]
        acc_refs = refs[o + n_rg + n_ex:]
        vals = [_ld(r, s[1]).astype(F32) for r, s in zip(in_refs, ins)]
        pv = [r[...] for r in p_refs]
        res, vjp = jax.vjp(f, *vals, *pv)
        if primal_sum:
            cv = [jnp.ones_like(res[0])] + [_ld(r, s[1]).astype(F32) for r, s in zip(ct_refs, cts)]
        else:
            cv = [_ld(r, s[1]).astype(F32) for r, s in zip(ct_refs, cts)]
        grads = vjp(tuple(cv))
        for r, s in zip(rg_refs, row_grads):
            _st(r, s[1], grads[s[0]])
        if extra is not None:
            for r, s, e in zip(ex_refs, extra_outs, extra(vals, cv, grads)):
                _st(r, s[0], e)

        @pl.when(first)
        def _():
            for r in acc_refs:
                r[...] = jnp.zeros_like(r)

        for r, g in zip(acc_refs[:n_p], grads[n_in:]):
            r[...] += g
        if primal_sum:
            acc_refs[n_p][...] += jnp.sum(res[0])

    acc_shapes = [jax.ShapeDtypeStruct(p.shape, F32) for p in params]
    acc_specs = [_full_spec(p) for p in params]
    if primal_sum:
        acc_shapes.append(jax.ShapeDtypeStruct((1, 128), F32))
        acc_specs.append(pl.BlockSpec((1, 128), lambda i, h: (0, 0)))
    rg_shapes = [jax.ShapeDtypeStruct(into[j].shape, into[j].dtype) if j in into else _out_struct(k, w, S, tw, dt, nh)
                 for j, (_, k, w, c0, tw, dt) in enumerate(row_grads)]
    first_alias = n_in + n_p + n_ct
    return pl.pallas_call(
        body, name=name, grid=(S // ts, nh),
        in_specs=[_bspec(k, w, c0, ts) for (_, k, w, c0) in ins] + [_full_spec(p) for p in params]
        + [_bspec(k, w, c0, ts) for (_, k, w, c0) in cts] + [pl.BlockSpec(memory_space=pl.ANY)] * n_al,
        out_specs=[_bspec(k, w, c0, ts) for (_, k, w, c0, _, _) in row_grads]
        + [_bspec(k, w, c0, ts) for (k, w, c0, _, _) in extra_outs] + acc_specs,
        out_shape=rg_shapes + [_out_struct(k, w, S, tw, dt, nh) for (k, w, c0, tw, dt) in extra_outs] + acc_shapes,
        input_output_aliases={first_alias + n: j for n, j in enumerate(sorted(into))},
    )(*[a for (a, _, _, _) in ins], *params, *[a for (a, _, _, _) in cts], *[into[j] for j in sorted(into)])


def f_prologue(x, ng, sc, sh):
    return (_rms(x, ng) * (1.0 + sc) + sh,)


def f_prologue_res(x, ng, sc, sh):
    return (_rms(x, ng) * (1.0 + sc) + sh, x)


def f_loss(x, mo, tgt, gate, fg):
    y = _rms(x + gate * mo, fg)
    return (0.5 * jnp.mean(jnp.square(y - tgt), axis=-1, keepdims=True),)


def f_foxpre(fq, fk, fv, qg, kg):
    return (_rms(fq, qg) * ATT_SCALE, _rms(fk, kg), fv)


def f_postf(o, z):
    return (o * _silu(z),)


def f_postg(o, z, ng):
    return (_rms(o, ng) * _silu(z),)


def fox_norm_bwd(p_all, d_qn, d_kn, qg, kg, dp, S, ts):
    def body(p_ref, dq_ref, dk_ref, qg_ref, kg_ref, dp_in, o_ref, dqg_ref, dkg_ref):
        @pl.when(pl.program_id(0) == 0)
        def _():
            dqg_ref[...] = jnp.zeros_like(dqg_ref)
            dkg_ref[...] = jnp.zeros_like(dkg_ref)

        for ct_ref, g_ref, dg_ref, col0 in ((dq_ref, qg_ref, dqg_ref, 0), (dk_ref, kg_ref, dkg_ref, WIDTH)):
            dg = jnp.zeros((1, D_HEAD), F32)
            for h in range(N_HEADS):
                _, vjp = jax.vjp(_rms, p_ref[:, col0 + h * D_HEAD:col0 + (h + 1) * D_HEAD], g_ref[...])
                dx, dg_h = vjp(ct_ref[:, h * D_HEAD:(h + 1) * D_HEAD])
                o_ref[:, col0 + h * D_HEAD:col0 + (h + 1) * D_HEAD] = dx.astype(o_ref.dtype)
                dg = dg + dg_h
            dg_ref[...] += dg

    gain = pl.BlockSpec((1, D_HEAD), lambda i: (0, 0))
    both = pl.BlockSpec((ts, 2 * WIDTH), lambda i: (i, 0))
    one = pl.BlockSpec((ts, WIDTH), lambda i: (i, 0))
    return pl.pallas_call(
        body, name="fox_norm_bwd", grid=(S // ts,),
        in_specs=[both, one, one, gain, gain, pl.BlockSpec(memory_space=pl.ANY)],
        out_specs=[both, gain, gain],
        out_shape=[jax.ShapeDtypeStruct(dp.shape, dp.dtype), jax.ShapeDtypeStruct((1, D_HEAD), F32),
                   jax.ShapeDtypeStruct((1, D_HEAD), F32)],
        input_output_aliases={5: 0},
    )(p_all, d_qn, d_kn, qg, kg, dp)


def _chip_exchange(kind, x_ref, out_ref, send_sems, recv_sems):
    x, y, c = lax.axis_index("x"), lax.axis_index("y"), lax.axis_index("c")
    sends, arrivals = [], []
    for r in range(1, N_CHIPS):
        px, py = _flip(x, r & 2), _flip(y, r & 1)
        if kind == "scatter":
            src, dst, landed = x_ref.at[2 * px + py], out_ref.at[r - 1], out_ref.at[r - 1]
        else:
            src, dst, landed = x_ref, out_ref.at[2 * x + y], out_ref.at[2 * px + py]
        mk = functools.partial(pltpu.make_async_remote_copy, send_sem=send_sems.at[r - 1], recv_sem=recv_sems.at[r - 1],
                               device_id=(px, py, c), device_id_type=MESH_ID)
        sends.append(mk(src_ref=src, dst_ref=dst))
        arrivals.append(mk(src_ref=src, dst_ref=landed))
    return sends, arrivals


def _exchange_shape(kind, x):
    return jax.ShapeDtypeStruct(((N_CHIPS - 1,) + x.shape[1:]) if kind == "scatter" else ((N_CHIPS,) + x.shape), x.dtype)


def matmul(name, a, b, out_dtype, tm=512, tn=512, tk=2048, trans_a=False, trans_b=False, rides=()):
    M, K = a.shape[::-1] if trans_a else a.shape
    N, K2 = b.shape if trans_b else b.shape[::-1]
    assert K == K2
    tm, tn, tk = _tile(M, tm), _tile(N, tn), _tile(K, tk)
    ni, nj, nk = M // tm, N // tn, K // tk
    a_spec = pl.BlockSpec((tk, tm), lambda i, j, k: (k, i)) if trans_a else pl.BlockSpec((tm, tk), lambda i, j, k: (i, k))
    b_spec = pl.BlockSpec((tn, tk), lambda i, j, k: (j, k)) if trans_b else pl.BlockSpec((tk, tn), lambda i, j, k: (k, j))
    n_ride = len(rides)

    def body(a_ref, b_ref, *rest):
        ride_in, o_ref, ride_out = rest[:n_ride], rest[n_ride], rest[n_ride + 1:2 * n_ride + 1]
        scr = rest[2 * n_ride + 1:]
        i, j, k = pl.program_id(0), pl.program_id(1), pl.program_id(2)
        exchanges = [_chip_exchange(kind, ride_in[n], ride_out[n], scr[2 * n], scr[2 * n + 1])
                     for n, (kind, _) in enumerate(rides)]

        @pl.when((i == 0) & (j == 0) & (k == 0))
        def _():
            for sends, _ in exchanges:
                for cp in sends:
                    cp.start()

        part = lax.dot_general(a_ref[...], b_ref[...], (((0 if trans_a else 1,), (1 if trans_b else 0,)), ((), ())),
                               preferred_element_type=F32)
        if nk == 1:
            o_ref[...] = part.astype(o_ref.dtype)
        else:
            acc = scr[2 * n_ride]

            @pl.when(k == 0)
            def _():
                acc[...] = part

            @pl.when(k > 0)
            def _():
                acc[...] += part

            @pl.when(k == nk - 1)
            def _():
                o_ref[...] = acc[...].astype(o_ref.dtype)

        @pl.when((i == ni - 1) & (j == nj - 1) & (k == nk - 1))
        def _():
            for sends, arrivals in exchanges:
                for cp in arrivals:
                    cp.wait_recv()
                for cp in sends:
                    cp.wait_send()

    sems = [pltpu.SemaphoreType.DMA((N_CHIPS - 1,))] * (2 * n_ride)
    any_spec = pl.BlockSpec(memory_space=pl.ANY)
    res = pl.pallas_call(
        body, name=name, grid=(ni, nj, nk),
        in_specs=[a_spec, b_spec] + [any_spec] * n_ride,
        out_specs=[pl.BlockSpec((tm, tn), lambda i, j, k: (i, j))] + [any_spec] * n_ride,
        out_shape=[jax.ShapeDtypeStruct((M, N), out_dtype)] + [_exchange_shape(kind, x) for kind, x in rides],
        scratch_shapes=sems + ([] if nk == 1 else [pltpu.VMEM((tm, tn), F32)]),
        compiler_params=pltpu.CompilerParams(
            dimension_semantics=("arbitrary",) * 3 if rides else ("parallel", "parallel", "arbitrary")),
    )(a, b, *[x for _, x in rides])
    return tuple(res) if rides else res[0]


def _small_f(z, pv, av):
    lane = lax.broadcasted_iota(jnp.int32, z.shape, 1)
    zz = z + pv
    logf = _log_sigmoid(zz)
    gval = -jnp.exp(av) * _softplus(zz)
    beta = _sigmoid(zz)
    return jnp.where(lane < 8, logf, jnp.where(lane < 16, gval, jnp.where(lane < 24, beta, 0.0)))


def _tri(ts, upper):
    r = lax.broadcasted_iota(jnp.int32, (ts, ts), 0)
    c = lax.broadcasted_iota(jnp.int32, (ts, ts), 1)
    keep = (r <= c) if upper else (r >= c)
    same_chunk = (r // CHUNK) == (c // CHUNK)
    return keep.astype(F32), (keep & same_chunk).astype(F32)


def smalls_fwd(p_all, pv, av, S, ts):
    nt = S // ts

    def body(z_ref, pv_ref, av_ref, o_ref, carry):
        i = pl.program_id(0)

        @pl.when(i == 0)
        def _():
            carry[...] = jnp.zeros_like(carry)

        e = _small_f(z_ref[...], pv_ref[...], av_ref[...])
        lane = lax.broadcasted_iota(jnp.int32, e.shape, 1)
        l_all, l_chunk = _tri(ts, upper=False)
        cum_all = hdot(l_all, e) + carry[...]
        cum_chunk = hdot(l_chunk, e)
        carry[...] = cum_all[ts - 1:ts, :]
        o_ref[...] = jnp.where(lane < 8, cum_all, jnp.where(lane < 16, cum_chunk, e))

    return pl.pallas_call(
        body, name="smalls_fwd", grid=(nt,),
        in_specs=[pl.BlockSpec((ts, 128), lambda i: (i, CB_SMALL)), pl.BlockSpec((1, 128), lambda i: (0, 0)),
                  pl.BlockSpec((1, 128), lambda i: (0, 0))],
        out_specs=pl.BlockSpec((ts, 128), lambda i: (i, 0)),
        out_shape=jax.ShapeDtypeStruct((S, 128), F32),
        scratch_shapes=[pltpu.VMEM((1, 128), F32)],
    )(p_all, pv, av)


def smalls_bwd(p_all, pv, av, dsm, dp, S, ts):
    nt = S // ts

    def body(z_ref, pv_ref, av_ref, d_ref, dp_in, dz_ref, dpv_ref, dav_ref, carry):
        i = pl.program_id(0)

        @pl.when(i == 0)
        def _():
            carry[...] = jnp.zeros_like(carry)
            dpv_ref[...] = jnp.zeros_like(dpv_ref)
            dav_ref[...] = jnp.zeros_like(dav_ref)

        d = d_ref[...]
        lane = lax.broadcasted_iota(jnp.int32, d.shape, 1)
        u_all, u_chunk = _tri(ts, upper=True)
        rev_all = hdot(u_all, d) + carry[...]
        rev_chunk = hdot(u_chunk, d)
        carry[...] = rev_all[0:1, :]
        de = jnp.where(lane < 8, rev_all, jnp.where(lane < 16, rev_chunk, d))
        _, vjp = jax.vjp(_small_f, z_ref[...], pv_ref[...], av_ref[...])
        dz, dpv, dav = vjp(de)
        dz_ref[...] = dz.astype(dz_ref.dtype)
        dpv_ref[...] += dpv
        dav_ref[...] += dav

    rev = lambda i: (nt - 1 - i, 0)
    small_cols = pl.BlockSpec((ts, 128), lambda i: (nt - 1 - i, CB_SMALL))
    return pl.pallas_call(
        body, name="smalls_bwd", grid=(nt,),
        in_specs=[small_cols, pl.BlockSpec((1, 128), lambda i: (0, 0)), pl.BlockSpec((1, 128), lambda i: (0, 0)),
                  pl.BlockSpec((ts, 128), rev), pl.BlockSpec(memory_space=pl.ANY)],
        out_specs=[small_cols, pl.BlockSpec((1, 128), lambda i: (0, 0)), pl.BlockSpec((1, 128), lambda i: (0, 0))],
        out_shape=[jax.ShapeDtypeStruct(dp.shape, dp.dtype), jax.ShapeDtypeStruct((1, 128), F32),
                   jax.ShapeDtypeStruct((1, 128), F32)],
        scratch_shapes=[pltpu.VMEM((1, 128), F32)],
        input_output_aliases={4: 0},
    )(p_all, pv, av, dsm, dp)


def _col_to_row(col):
    return jnp.transpose(jnp.broadcast_to(col, (col.shape[0], 128)))[0:1, :]


def _row_to_col(row):
    return jnp.transpose(jnp.broadcast_to(row, (128, row.shape[1])))[:, 0:1]


def _causal_mask(s, keys_first):
    r = lax.broadcasted_iota(jnp.int32, s.shape, 0)
    c = lax.broadcasted_iota(jnp.int32, s.shape, 1)
    return jnp.where((r <= c) if keys_first else (c <= r), s, NEG)


def flash_fwd(qs, kn, vb, frow, S, tq):
    nq = S // tq
    tk = tq
    hp = 2
    wide = hp * D_HEAD
    head = lambda ref, h, rows=slice(None): ref[rows, h * D_HEAD:(h + 1) * D_HEAD]

    def body(q_ref, k_ref, v_ref, fr_ref, o_ref, lse_ref):
        i = pl.program_id(1)
        qs_ = [head(q_ref, h) for h in range(hp)]
        f_base = [fr_ref[h, i][:, 0:1] for h in range(hp)]

        def tile(j, carry, diagonal):
            rows = pl.ds(pl.multiple_of(j * tk, tk), tk)
            out = []
            for h in range(hp):
                m, l, acc = carry[3 * h:3 * h + 3]
                s = mm_nt(qs_[h], head(k_ref, h, rows)) - (fr_ref[h, j] - f_base[h])
                if diagonal:
                    s = _causal_mask(s, keys_first=False)
                m_new = jnp.maximum(m, jnp.max(s, axis=-1, keepdims=True))
                a = jnp.exp(m - m_new)
                p = jnp.exp(s - m_new)
                out += [m_new, a * l + jnp.sum(p, axis=-1, keepdims=True), a * acc + mm_nn(p, head(v_ref, h, rows))]
            return tuple(out)

        init = (jnp.full((tq, 1), NEG, F32), jnp.zeros((tq, 1), F32), jnp.zeros((tq, D_HEAD), F32)) * hp
        res = tile(i, lax.fori_loop(0, i, lambda j, c: tile(j, c, False), init), True)
        for h in range(hp):
            m, l, acc = res[3 * h:3 * h + 3]
            o_ref[:, h * D_HEAD:(h + 1) * D_HEAD] = acc / l
            lse_ref[h, 0] = _col_to_row(m + jnp.log(l))

    return pl.pallas_call(
        body, name="flash_fwd", grid=(N_HEADS // hp, nq),
        in_specs=[pl.BlockSpec((tq, wide), lambda h, i: (i, h)), pl.BlockSpec((S, wide), lambda h, i: (0, h)),
                  pl.BlockSpec((S, wide), lambda h, i: (0, h)),
                  pl.BlockSpec((hp, nq, 1, tk), lambda h, i: (h, 0, 0, 0))],
        out_specs=[pl.BlockSpec((tq, wide), lambda h, i: (i, h)),
                   pl.BlockSpec((hp, 1, 1, tq), lambda h, i: (h, i, 0, 0))],
        out_shape=[jax.ShapeDtypeStruct((S, WIDTH), F32), jax.ShapeDtypeStruct((N_HEADS, nq, 1, tq), F32)],
    )(qs, kn, vb, frow)


def flash_bwd(qs, kn, vb, do, frow, lse_row, delta_row, dp, S, tq):
    nq = S // tq
    tk = tq
    hp = 2
    wide = hp * D_HEAD
    head = lambda ref, h, rows=slice(None): ref[rows, h * D_HEAD:(h + 1) * D_HEAD]

    def body(q_ref, k_ref, v_ref, do_ref, fr_ref, lse_ref, dl_ref, dp_in, dq_ref, dk_ref, dv_ref, dfq_ref, dfk_ref):
        j = pl.program_id(1)
        ks = [head(k_ref, h) for h in range(hp)]
        vs = [head(v_ref, h) for h in range(hp)]
        fks = [_row_to_col(fr_ref[h, j]) for h in range(hp)]

        @pl.when(j == 0)
        def _():
            dq_ref[...] = jnp.zeros_like(dq_ref)
            dfq_ref[...] = jnp.zeros_like(dfq_ref)

        def tile(i, carry, diagonal):
            rows = pl.ds(pl.multiple_of(i * tq, tq), tq)
            out = []
            for h in range(hp):
                dk, dv, dfk = carry[3 * h:3 * h + 3]
                q = head(q_ref, h, rows)
                do_ = head(do_ref, h, rows)
                st = mm_nt(ks[h], q) - (fks[h] - fr_ref[h, i][:, 0:1])
                if diagonal:
                    st = _causal_mask(st, keys_first=True)
                pt = jnp.exp(st - lse_ref[h, i])
                dst = pt * (mm_nt(vs[h], do_) - dl_ref[h, i])
                dq_ref[rows, h * D_HEAD:(h + 1) * D_HEAD] += mm_tn(dst, ks[h]) * ATT_SCALE
                dfq_ref[h, i] += jnp.sum(dst, axis=0, keepdims=True)
                out += [dk + mm_nn(dst, q), dv + mm_nn(pt, do_), dfk - jnp.sum(dst, axis=-1, keepdims=True)]
            return tuple(out)

        z = jnp.zeros((tk, D_HEAD), F32)
        carry = tile(j, (z, z, jnp.zeros((tk, 1), F32)) * hp, True)
        res = lax.fori_loop(j + 1, nq, lambda i, c: tile(i, c, False), carry)
        for h in range(hp):
            dk, dv, dfk = res[3 * h:3 * h + 3]
            dk_ref[:, h * D_HEAD:(h + 1) * D_HEAD] = dk
            dv_ref[:, h * D_HEAD:(h + 1) * D_HEAD] = dv.astype(dv_ref.dtype)
            dfk_ref[h, 0] = _col_to_row(dfk)

    blk = pl.BlockSpec((tk, wide), lambda h, j: (j, h))
    full = pl.BlockSpec((S, wide), lambda h, j: (0, h), pipeline_mode=pl.Buffered(1))
    rows = pl.BlockSpec((hp, nq, 1, tq), lambda h, j: (h, 0, 0, 0))
    stat = jax.ShapeDtypeStruct((N_HEADS, nq, 1, tq), F32)
    return pl.pallas_call(
        body, name="flash_bwd", grid=(N_HEADS // hp, nq),
        in_specs=[full, blk, blk, full, rows, rows, rows, pl.BlockSpec(memory_space=pl.ANY)],
        out_specs=[full, blk, pl.BlockSpec((tk, wide), lambda h, j: (j, CB_FV // hp + h)), rows,
                   pl.BlockSpec((hp, 1, 1, tk), lambda h, j: (h, j, 0, 0))],
        out_shape=[jax.ShapeDtypeStruct((S, WIDTH), F32), jax.ShapeDtypeStruct((S, WIDTH), F32),
                   jax.ShapeDtypeStruct(dp.shape, dp.dtype), stat, stat],
        input_output_aliases={7: 2},
    )(qs, kn, vb, do, frow, lse_row, delta_row, dp)


CONV_COLS = 3 * WIDTH
CONV_TC = 512


def _gdn_act(y, group):
    s = _silu(y)
    if group == 2:
        return s
    return _l2n(s) * ATT_SCALE if group == 0 else _l2n(s)


def _gdn_act_bwd(y, d, group):
    _, vjp = jax.vjp(functools.partial(_gdn_act, group=group), y)
    return vjp(d)[0]


def conv_act_fwd(p_all, conv_w, S, ts):
    nt, tc = S // ts, CONV_TC
    cb0 = CB_GQ * 128 // tc
    r8 = ts // 8
    tiles_per_group = WIDTH // tc

    def body(x_ref, halo_ref, w_ref, y_ref, a_ref, scr):
        c, i = pl.program_id(0), pl.program_id(1)
        scr[0:8, :] = jnp.where(i > 0, halo_ref[...], 0.0)
        scr[8:8 + ts, :] = x_ref[...]
        w = w_ref[...]
        y = w[3:4, :] * x_ref[...]
        for j in range(CONV_K - 1):
            y = y + w[j:j + 1, :] * scr[5 + j:5 + j + ts, :]
        y_ref[...] = y
        for group in range(3):
            @pl.when(c // tiles_per_group == group)
            def _(group=group):
                for h in range(tc // D_HEAD):
                    sl = slice(h * D_HEAD, (h + 1) * D_HEAD)
                    a_ref[:, sl] = _gdn_act(y_ref[:, sl], group)

    out = pl.BlockSpec((ts, tc), lambda c, i: (i, c))
    return pl.pallas_call(
        body, name="conv_act_fwd", grid=(CONV_COLS // tc, nt),
        in_specs=[pl.BlockSpec((ts, tc), lambda c, i: (i, cb0 + c)),
                  pl.BlockSpec((8, tc), lambda c, i: (jnp.maximum(i * r8 - 1, 0), cb0 + c)),
                  pl.BlockSpec((CONV_K, tc), lambda c, i: (0, c))],
        out_specs=[out, out],
        out_shape=[jax.ShapeDtypeStruct((S, CONV_COLS), F32)] * 2,
        scratch_shapes=[pltpu.VMEM((ts + 8, tc), F32)],
    )(p_all, p_all, conv_w)


def conv_act_bwd(p_all, y, dqkv, conv_w, dp, S, ts):
    nt, tc = S // ts, CONV_TC
    cb0 = CB_GQ * 128 // tc
    r8 = ts // 8
    tiles_per_group = WIDTH // tc

    def body(x_ref, xh_ref, y_ref, yh_ref, d_ref, dh_ref, w_ref, dp_in, dx_ref, dw_ref, xs, ds):
        c, i = pl.program_id(0), pl.program_id(1)
        xs[0:8, :] = jnp.where(i > 0, xh_ref[...], 0.0)
        xs[8:8 + ts, :] = x_ref[...]
        for group in range(3):
            @pl.when(c // tiles_per_group == group)
            def _(group=group):
                for h in range(tc // D_HEAD):
                    sl = slice(h * D_HEAD, (h + 1) * D_HEAD)
                    ds[0:ts, sl] = _gdn_act_bwd(y_ref[:, sl], d_ref[:, sl], group)
                    ds[ts:ts + 8, sl] = jnp.where(i < nt - 1, _gdn_act_bwd(yh_ref[:, sl], dh_ref[:, sl], group), 0.0)
        w = w_ref[...]
        d = ds[0:ts, :]
        dx = w[3:4, :] * d
        for j in range(CONV_K - 1):
            dx = dx + w[j:j + 1, :] * ds[3 - j:3 - j + ts, :]
        dx_ref[...] = dx.astype(dx_ref.dtype)

        @pl.when(i == 0)
        def _():
            dw_ref[...] = jnp.zeros_like(dw_ref)

        rows = [jnp.sum(d * xs[5 + j:5 + j + ts, :], axis=0, keepdims=True) for j in range(CONV_K - 1)]
        rows.append(jnp.sum(d * x_ref[...], axis=0, keepdims=True))
        dw_ref[...] += jnp.concatenate(rows, axis=0)

    tile = pl.BlockSpec((ts, tc), lambda c, i: (i, c))
    next8 = pl.BlockSpec((8, tc), lambda c, i: (jnp.minimum((i + 1) * r8, nt * r8 - 1), c))
    return pl.pallas_call(
        body, name="conv_act_bwd", grid=(CONV_COLS // tc, nt),
        in_specs=[pl.BlockSpec((ts, tc), lambda c, i: (i, cb0 + c)),
                  pl.BlockSpec((8, tc), lambda c, i: (jnp.maximum(i * r8 - 1, 0), cb0 + c)),
                  tile, next8, tile, next8,
                  pl.BlockSpec((CONV_K, tc), lambda c, i: (0, c)), pl.BlockSpec(memory_space=pl.ANY)],
        out_specs=[pl.BlockSpec((ts, tc), lambda c, i: (i, cb0 + c)), pl.BlockSpec((CONV_K, tc), lambda c, i: (0, c))],
        out_shape=[jax.ShapeDtypeStruct(dp.shape, dp.dtype), jax.ShapeDtypeStruct((CONV_K, CONV_COLS), F32)],
        scratch_shapes=[pltpu.VMEM((ts + 8, tc), F32), pltpu.VMEM((ts + 8, tc), F32)],
        input_output_aliases={7: 0},
    )(p_all, p_all, y, y, dqkv, dqkv, conv_w, dp)


def _bdot(a, b, ca, cb):
    return lax.dot_general(a.astype(MXU_DTYPE), b.astype(MXU_DTYPE), (((ca,), (cb,)), ((0,), (0,))),
                           preferred_element_type=F32)


@jax.custom_vjp
def bmm_nn(a, b):
    return _bdot(a, b, 2, 1)


@jax.custom_vjp
def bmm_nt(a, b):
    return _bdot(a, b, 2, 2)


@jax.custom_vjp
def bmm_tn(a, b):
    return _bdot(a, b, 1, 1)


bmm_nn.defvjp(lambda a, b: (bmm_nn(a, b), (a, b)), lambda r, g: (bmm_nt(g, r[1]), bmm_tn(r[0], g)))
bmm_nt.defvjp(lambda a, b: (bmm_nt(a, b), (a, b)), lambda r, g: (bmm_nn(g, r[1]), bmm_tn(g, r[0])))
bmm_tn.defvjp(lambda a, b: (bmm_tn(a, b), (a, b)), lambda r, g: (bmm_nt(r[1], g), bmm_nn(r[0], g)))


def bdot_hi(a, b, ca=2, cb=1):
    return lax.dot_general(a, b, (((ca,), (cb,)), ((0,), (0,))), precision=SOLVE_PRECISION,
                           preferred_element_type=F32)


def _tri_inv_product(m):
    ii = lax.broadcasted_iota(jnp.int32, m.shape, 1)
    jj = lax.broadcasted_iota(jnp.int32, m.shape, 2)
    t = (ii == jj).astype(F32) - m
    pw = bdot_hi(m, m)
    for _ in range(4):
        t = t + bdot_hi(t, pw)
        pw = bdot_hi(pw, pw)
    return t + bdot_hi(t, pw)


def _tri_inv_bwd(t, g):
    return -bdot_hi(bdot_hi(t, g, 1, 1), t, 2, 2)


@jax.custom_vjp
def tri_inv(m):
    return _tri_inv_product(m)


@jax.custom_vjp
def tri_inv_known(m, t):
    return t


tri_inv.defvjp(lambda m: (lambda t: (t, t))(_tri_inv_product(m)), lambda t, g: (_tri_inv_bwd(t, g),))
tri_inv_known.defvjp(lambda m, t: (t, t), lambda t, g: (_tri_inv_bwd(t, g), jnp.zeros_like(t)))


def chunk_fn(q, k, v, gc_col, gc_row, beta, state, t_known=None):
    shape = (N_HEADS, CHUNK, CHUNK)
    ii = lax.broadcasted_iota(jnp.int32, shape, 1)
    jj = lax.broadcasted_iota(jnp.int32, shape, 2)
    lower = ii >= jj
    strict = ii > jj
    decay = jnp.exp(jnp.where(lower, gc_col - gc_row, NEG))
    kb = k * beta
    vb = v * beta
    m = jnp.where(strict, bmm_nt(kb, k) * decay, 0.0)
    t = tri_inv(m) if t_known is None else tri_inv_known(m, t_known)
    u = bdot_hi(t, vb)
    w = bdot_hi(t, kb * jnp.exp(gc_col))
    attn = jnp.where(lower, bmm_nt(q, k) * decay, 0.0)
    v_new = u - bmm_nn(w, state)
    o = bmm_nn(q * jnp.exp(gc_col), state) + bmm_nn(attn, v_new)
    g_last = gc_col[:, CHUNK - 1:CHUNK, :]
    k_dec = k * jnp.exp(g_last - gc_col)
    new_state = state * jnp.exp(g_last) + bmm_tn(k_dec, v_new)
    return (o, new_state, t) if t_known is None else (o, new_state)


GDN_CHUNKS_PER_STEP = 4


def _heads(ref, rows):
    return jnp.stack([ref[rows, h * D_HEAD:(h + 1) * D_HEAD] for h in range(N_HEADS)], axis=0)


def _head_cols(ref, rows, lane0):
    return jnp.stack([ref[rows, lane0 + h:lane0 + h + 1] for h in range(N_HEADS)], axis=0)


def gdn_fwd(qkv, smalls, gc_row, S):
    n = S // CHUNK
    per = _tile(n, GDN_CHUNKS_PER_STEP)
    rows_per = per * CHUNK

    def body(q_ref, k_ref, v_ref, sm_ref, gr_ref, o_ref, st_ref, t_ref, state):
        @pl.when(pl.program_id(0) == 0)
        def _():
            state[...] = jnp.zeros_like(state)

        s = state[...]
        for b in range(per):
            rows = pl.ds(b * CHUNK, CHUNK)
            st_ref[b] = s
            o, s, t = chunk_fn(_heads(q_ref, rows), _heads(k_ref, rows), _heads(v_ref, rows),
                               _head_cols(sm_ref, rows, 8), gr_ref[b][:, None, :], _head_cols(sm_ref, rows, 16), s)
            for h in range(N_HEADS):
                o_ref[rows, h * D_HEAD:(h + 1) * D_HEAD] = o[h]
            t_ref[b] = t
        state[...] = s

    return pl.pallas_call(
        body, name="gdn_chunk_fwd", grid=(n // per,),
        in_specs=[pl.BlockSpec((rows_per, WIDTH), lambda i, g=g: (i, g)) for g in range(3)]
        + [pl.BlockSpec((rows_per, 128), lambda i: (i, 0)), pl.BlockSpec((per, N_HEADS, CHUNK), lambda i: (i, 0, 0))],
        out_specs=[pl.BlockSpec((rows_per, WIDTH), lambda i: (i, 0)),
                   pl.BlockSpec((per, N_HEADS, D_HEAD, D_HEAD), lambda i: (i, 0, 0, 0)),
                   pl.BlockSpec((per, N_HEADS, CHUNK, CHUNK), lambda i: (i, 0, 0, 0))],
        out_shape=[jax.ShapeDtypeStruct((S, WIDTH), F32), jax.ShapeDtypeStruct((n, N_HEADS, D_HEAD, D_HEAD), F32),
                   jax.ShapeDtypeStruct((n, N_HEADS, CHUNK, CHUNK), F32)],
        scratch_shapes=[pltpu.VMEM((N_HEADS, D_HEAD, D_HEAD), F32)],
    )(qkv, qkv, qkv, smalls, gc_row)


def gdn_bwd(qkv, smalls, gc_row, states, tinv, do, S):
    n = S // CHUNK
    per = _tile(n, GDN_CHUNKS_PER_STEP)
    rows_per = per * CHUNK
    nb = n // per

    def body(q_ref, k_ref, v_ref, sm_ref, gr_ref, st_ref, t_ref, do_ref, dqkv_ref, dsm_ref, dgr_ref, dstate):
        @pl.when(pl.program_id(0) == 0)
        def _():
            dstate[...] = jnp.zeros_like(dstate)

        ds = dstate[...]
        for b in reversed(range(per)):
            rows = pl.ds(b * CHUNK, CHUNK)
            t_saved = t_ref[b]
            _, vjp = jax.vjp(lambda *a: chunk_fn(*a, t_known=t_saved), _heads(q_ref, rows), _heads(k_ref, rows),
                             _heads(v_ref, rows), _head_cols(sm_ref, rows, 8), gr_ref[b][:, None, :],
                             _head_cols(sm_ref, rows, 16), st_ref[b])
            dq, dk, dv, dgc, dgr, dbt, ds = vjp((_heads(do_ref, rows), ds))
            for h in range(N_HEADS):
                dqkv_ref[rows, h * D_HEAD:(h + 1) * D_HEAD] = dq[h]
                dqkv_ref[rows, WIDTH + h * D_HEAD:WIDTH + (h + 1) * D_HEAD] = dk[h]
                dqkv_ref[rows, 2 * WIDTH + h * D_HEAD:2 * WIDTH + (h + 1) * D_HEAD] = dv[h]
            cols = [dgc[h] for h in range(N_HEADS)] + [dbt[h] for h in range(N_HEADS)]
            dsm_ref[rows, :] = jnp.concatenate([jnp.zeros((CHUNK, 8), F32)] + cols
                                               + [jnp.zeros((CHUNK, 128 - 24), F32)], axis=1)
            dgr_ref[b] = jnp.concatenate([dgr[h] for h in range(N_HEADS)], axis=0)
        dstate[...] = ds

    rev = lambda c: (lambda i: (nb - 1 - i, c))
    rev4 = lambda i: (nb - 1 - i, 0, 0, 0)
    return pl.pallas_call(
        body, name="gdn_chunk_bwd", grid=(nb,),
        in_specs=[pl.BlockSpec((rows_per, WIDTH), rev(g)) for g in range(3)]
        + [pl.BlockSpec((rows_per, 128), rev(0)), pl.BlockSpec((per, N_HEADS, CHUNK), lambda i: (nb - 1 - i, 0, 0)),
           pl.BlockSpec((per, N_HEADS, D_HEAD, D_HEAD), rev4), pl.BlockSpec((per, N_HEADS, CHUNK, CHUNK), rev4),
           pl.BlockSpec((rows_per, WIDTH), rev(0))],
        out_specs=[pl.BlockSpec((rows_per, 3 * WIDTH), rev(0)), pl.BlockSpec((rows_per, 128), rev(0)),
                   pl.BlockSpec((per, N_HEADS, CHUNK), lambda i: (nb - 1 - i, 0, 0))],
        out_shape=[jax.ShapeDtypeStruct((S, 3 * WIDTH), F32)] + [jax.ShapeDtypeStruct((S, 128), F32),
                                                                  jax.ShapeDtypeStruct((n, N_HEADS, CHUNK), F32)],
        scratch_shapes=[pltpu.VMEM((N_HEADS, D_HEAD, D_HEAD), F32)],
    )(qkv, qkv, qkv, smalls, gc_row, states, tinv, do)


def _flip(a, bit):
    return 1 - a if bit else a


def allgather8(name, buf):
    R, W = buf.shape

    def body(x_ref, out_ref, send_sems, recv_sems, local_sem):
        x, y, c = lax.axis_index("x"), lax.axis_index("y"), lax.axis_index("c")

        def slot(px, py, pc):
            return out_ref.at[4 * px + 2 * py + pc]

        mine = pltpu.make_async_copy(x_ref, slot(x, y, c), local_sem)
        mine.start()
        sends = []
        for r in range(1, N_DEV):
            peer = (_flip(x, r & 4), _flip(y, r & 2), _flip(c, r & 1))
            cp = pltpu.make_async_remote_copy(src_ref=x_ref, dst_ref=slot(x, y, c), send_sem=send_sems.at[r - 1],
                                              recv_sem=recv_sems.at[r - 1], device_id=peer, device_id_type=MESH_ID)
            cp.start()
            sends.append(cp)
        for r in range(1, N_DEV):
            peer = (_flip(x, r & 4), _flip(y, r & 2), _flip(c, r & 1))
            pltpu.make_async_remote_copy(src_ref=x_ref, dst_ref=slot(*peer), send_sem=send_sems.at[r - 1],
                                         recv_sem=recv_sems.at[r - 1], device_id=peer,
                                         device_id_type=MESH_ID).wait_recv()
        for cp in sends:
            cp.wait_send()
        mine.wait()

    return pl.pallas_call(
        body, name=name,
        out_shape=jax.ShapeDtypeStruct((N_DEV, R, W), buf.dtype),
        in_specs=[pl.BlockSpec(memory_space=pltpu.VMEM)],
        out_specs=pl.BlockSpec(memory_space=pltpu.VMEM),
        scratch_shapes=[pltpu.SemaphoreType.DMA((N_DEV - 1,)), pltpu.SemaphoreType.DMA((N_DEV - 1,)),
                        pltpu.SemaphoreType.DMA],
    )(buf)


def allgather_chips(name, blk):
    R, W = blk.shape
    half = R // 2
    n_far = N_CHIPS - 1

    def body(x_ref, out_ref, send_sems, recv_sems):
        x, y, c = lax.axis_index("x"), lax.axis_index("y"), lax.axis_index("c")
        sibling = (x, y, 1 - c)
        chips = [(_flip(x, r & 2), _flip(y, r & 1)) for r in range(1, N_CHIPS)]

        def rows(px, py, pc):
            return out_ref.at[2 * px + py, pc]

        def copy(k, dst, to, src):
            return pltpu.make_async_remote_copy(src_ref=src, dst_ref=dst, send_sem=send_sems.at[k],
                                                recv_sem=recv_sems.at[k], device_id=to, device_id_type=MESH_ID)

        my_half = x_ref.at[c]
        first = [copy(j, rows(x, y, c), (*chip, c), my_half) for j, chip in enumerate(chips)]
        for cp in first:
            cp.start()
        passed = [copy(n_far + j, rows(*chip, c), sibling, rows(*chip, c)) for j, chip in enumerate(chips)]
        for j, chip in enumerate(chips):
            copy(j, rows(*chip, c), (*chip, c), my_half).wait_recv()
            passed[j].start()
        for j, chip in enumerate(chips):
            copy(n_far + j, rows(*chip, 1 - c), sibling, my_half).wait_recv()
        for cp in first + passed:
            cp.wait_send()

    far = pl.pallas_call(
        body, name=name,
        out_shape=jax.ShapeDtypeStruct((N_CHIPS, 2, half, W), blk.dtype),
        in_specs=[pl.BlockSpec(memory_space=pl.ANY)],
        out_specs=pl.BlockSpec(memory_space=pl.ANY),
        scratch_shapes=[pltpu.SemaphoreType.DMA((2 * n_far,)), pltpu.SemaphoreType.DMA((2 * n_far,))],
    )(blk.reshape(2, half, W)).reshape(N_CHIPS, R, W)
    own_chip = 2 * lax.axis_index("x") + lax.axis_index("y")
    return lax.dynamic_update_index_in_dim(far, blk, own_chip, axis=0)


def sibling_send_other_half(name, g4):
    n, R, W = g4.shape
    half = R // 2

    def body(x_ref, out_ref, send_sem, recv_sem):
        x, y, c = lax.axis_index("x"), lax.axis_index("y"), lax.axis_index("c")
        cp = pltpu.make_async_remote_copy(src_ref=x_ref.at[:, pl.ds((1 - c) * half, half), :], dst_ref=out_ref,
                                          send_sem=send_sem, recv_sem=recv_sem, device_id=(x, y, 1 - c),
                                          device_id_type=MESH_ID)
        cp.start()
        cp.wait_recv()
        cp.wait_send()

    return pl.pallas_call(
        body, name=name,
        out_shape=jax.ShapeDtypeStruct((n, half, W), g4.dtype),
        in_specs=[pl.BlockSpec(memory_space=pl.ANY)],
        out_specs=pl.BlockSpec(memory_space=pl.ANY),
        scratch_shapes=[pltpu.SemaphoreType.DMA, pltpu.SemaphoreType.DMA],
    )(g4)


def sibling_merge(name, mine):
    def body(x_ref, out_ref, send_sem, recv_sem):
        x, y, c = lax.axis_index("x"), lax.axis_index("y"), lax.axis_index("c")
        cp = pltpu.make_async_remote_copy(src_ref=x_ref, dst_ref=out_ref, send_sem=send_sem, recv_sem=recv_sem,
                                          device_id=(x, y, 1 - c), device_id_type=MESH_ID)
        cp.start()
        cp.wait_recv()
        cp.wait_send()

    other = pl.pallas_call(
        body, name=name,
        out_shape=jax.ShapeDtypeStruct(mine.shape, mine.dtype),
        in_specs=[pl.BlockSpec(memory_space=pl.ANY)],
        out_specs=pl.BlockSpec(memory_space=pl.ANY),
        scratch_shapes=[pltpu.SemaphoreType.DMA, pltpu.SemaphoreType.DMA],
    )(mine)
    first = lax.axis_index("c") == 0
    return jnp.concatenate([jnp.where(first, mine, other), jnp.where(first, other, mine)], axis=0)


def sum_parts(name, own, recv, tr):
    R, W = own.shape
    tr = _tile(R, tr)

    def body(o_ref, r_ref, out_ref):
        acc = o_ref[...]
        for k in range(recv.shape[0]):
            acc = acc + r_ref[k].astype(F32)
        out_ref[...] = acc

    return pl.pallas_call(
        body, name=name, grid=(R // tr,),
        in_specs=[pl.BlockSpec((tr, W), lambda i: (i, 0)), pl.BlockSpec((recv.shape[0], tr, W), lambda i: (0, i, 0))],
        out_specs=pl.BlockSpec((tr, W), lambda i: (i, 0)),
        out_shape=jax.ShapeDtypeStruct((R, W), F32),
    )(own, recv)


def sum_own_half(name, g4, recv, c_idx, tr, rounded=True):
    n, R, W = g4.shape
    half = R // 2
    tr = _tile(half, tr)
    nb = half // tr

    def body(c_ref, g_ref, r_ref, o_ref, *ob_ref):
        s = g_ref[0] + r_ref[0]
        o_ref[0] = s
        if rounded:
            ob_ref[0][0] = s.astype(ob_ref[0].dtype)

    mine = pl.BlockSpec((1, tr, W), lambda j, i, c: (j, i, 0))
    shapes = [jax.ShapeDtypeStruct((n, half, W), F32)] + ([jax.ShapeDtypeStruct((n, half, W), MXU_DTYPE)] * rounded)
    return pl.pallas_call(
        body, name=name,
        grid_spec=pltpu.PrefetchScalarGridSpec(
            num_scalar_prefetch=1, grid=(n, nb),
            in_specs=[pl.BlockSpec((1, tr, W), lambda j, i, c: (j, c[0] * nb + i, 0)), mine],
            out_specs=[mine] * len(shapes)),
        out_shape=shapes,
    )(c_idx, g4, recv)


def sum_devices(name, g):
    def body(g_ref, o_ref):
        acc = g_ref[0]
        for d in range(1, N_DEV):
            acc = acc + g_ref[d]
        o_ref[...] = acc

    return pl.pallas_call(body, name=name, out_shape=jax.ShapeDtypeStruct(g.shape[1:], F32))(g)


def ada_fwd(c_all, w_ada):
    K, N = w_ada.shape
    tn = _tile(N, 512)

    def body(c_ref, w_ref, o_ref):
        o_ref[...] = mm_nn(_silu(c_ref[...]), w_ref[...])

    return pl.pallas_call(
        body, name="ada_fwd", grid=(N // tn,),
        in_specs=[pl.BlockSpec((N_DEV, K), lambda j: (0, 0)), pl.BlockSpec((K, tn), lambda j: (0, j))],
        out_specs=pl.BlockSpec((N_DEV, tn), lambda j: (0, j)),
        out_shape=jax.ShapeDtypeStruct((N_DEV, N), F32),
    )(c_all, w_ada)


def ada_grad(c_all, dmod):
    K = c_all.shape[1]
    N = dmod.shape[1]
    tn = _tile(N, 512)

    def body(c_ref, d_ref, o_ref):
        o_ref[...] = mm_tn(_silu(c_ref[...]), d_ref[...])

    return pl.pallas_call(
        body, name="ada_grad", grid=(N // tn,),
        in_specs=[pl.BlockSpec((N_DEV, K), lambda j: (0, 0)), pl.BlockSpec((N_DEV, tn), lambda j: (0, j))],
        out_specs=pl.BlockSpec((K, tn), lambda j: (0, j)),
        out_shape=jax.ShapeDtypeStruct((K, N), F32),
    )(c_all, dmod)


def adamw(name, w, gparts, m, v, tr, tc=None):
    R, W = w.shape
    tr = R if tc else _tile(R, tr)
    ng = len(gparts)

    def body(w_ref, *refs):
        g_refs, (m_ref, v_ref, g_out, d_out, m_out, v_out) = refs[:ng], refs[ng:]
        g = g_refs[0][...]
        for r in g_refs[1:]:
            g = g + r[...]
        g_out[...] = g
        d_out[...], m_out[...], v_out[...] = _adamw_update(w_ref[...], g, m_ref[...], v_ref[...])

    spec = pl.BlockSpec((R, tc), lambda i: (0, i)) if tc else pl.BlockSpec((tr, W), lambda i: (i, 0))
    return pl.pallas_call(
        body, name=name, grid=(W // tc if tc else R // tr,),
        in_specs=[spec] * (3 + ng), out_specs=[spec] * 4,
        out_shape=[jax.ShapeDtypeStruct((R, W), F32)] * 4,
    )(w, *gparts, m, v)


def _adamw_update(w, g, m, v):
    m1 = ADAM_B1 * m + (1.0 - ADAM_B1) * g
    v1 = ADAM_B2 * v + (1.0 - ADAM_B2) * jnp.square(g)
    m_hat = m1 / (1.0 - ADAM_B1 ** ADAM_STEP)
    v_hat = v1 / (1.0 - ADAM_B2 ** ADAM_STEP)
    return -ADAM_LR * (m_hat / (jnp.sqrt(v_hat) + ADAM_EPS) + ADAM_WD * w), m1, v1


def adamw_small(name, ws, gs, ms, vs):
    n = len(ws)

    def body(*refs):
        for k in range(n):
            w_ref, g_ref, m_ref, v_ref = (refs[j * n + k] for j in range(4))
            outs = refs[4 * n + 3 * k:4 * n + 3 * k + 3]
            for o_ref, val in zip(outs, _adamw_update(w_ref[...], g_ref[...], m_ref[...], v_ref[...])):
                o_ref[...] = val

    return pl.pallas_call(
        body, name=name, out_shape=[jax.ShapeDtypeStruct(w.shape, F32) for w in ws for _ in range(3)],
    )(*ws, *gs, *ms, *vs)


_REF_SEGMENTS = ((0, 4 * WIDTH, 0), (4 * WIDTH, 4 * WIDTH + 8, 4 * WIDTH), (4 * WIDTH + 8, 8 * WIDTH + 8, -8),
                 (8 * WIDTH + 8, IN_WIDTH, 0))


def _internal_from_blocks(blocks, n):
    a = 4 * WIDTH
    pieces = []
    for lo, hi in ((0, a), (a + 8, 2 * a + 8), (a, a + 8), (2 * a + 8, IN_WIDTH)):
        for j in range(N_CHIPS):
            s, e = max(lo, j * n), min(hi, (j + 1) * n)
            if s < e:
                pieces.append(blocks[j][:, s - j * n:e - j * n])
    pieces.append(jnp.zeros(blocks.shape[1:2] + (128 - N_SMALL,), blocks.dtype))
    return jnp.concatenate(pieces, axis=1)


def _block_from_internal(g, j, n):
    pieces = []
    for lo, hi, shift in _REF_SEGMENTS:
        s, e = max(lo, j * n), min(hi, (j + 1) * n)
        if s < e:
            pieces.append(g[:, s + shift:e + shift])
    return jnp.concatenate(pieces, axis=1)


def _pad_lanes(v, n=128):
    return jnp.pad(v, ((0, 0), (0, n - v.shape[1])))


def kernel(x, c, norm_g, w_ada, b_ada, w_in, b_fgate, fox_qn_g, fox_kn_g, gdn_conv_w, gdn_A_log, gdn_dt_bias, gdn_norm_g, w_out, final_g, loss_target, m_norm_g, m_w_ada, m_b_ada, m_w_in, m_b_fgate, m_fox_qn_g, m_fox_kn_g, m_gdn_conv_w, m_gdn_A_log, m_gdn_dt_bias, m_gdn_norm_g, m_w_out, m_final_g, v_norm_g, v_w_ada, v_b_ada, v_w_in, v_b_fgate, v_fox_qn_g, v_fox_kn_g, v_gdn_conv_w, v_gdn_A_log, v_gdn_dt_bias, v_gdn_norm_g, v_w_out, v_final_g):
    S = x.shape[1]
    H = N_HEADS
    ix, iy, ic = lax.axis_index("x"), lax.axis_index("y"), lax.axis_index("c")
    chip = 2 * ix + iy
    me = 2 * chip + ic
    x2, tgt = x[0], loss_target[0]
    ts_wide = _tile(S, 256)
    ts = _tile(S, 512)
    ts_conv = _tile(S, 1024)
    ts_head = _tile(S, 2048)
    tq = _tile(S, 1024)
    nq = S // tq
    nch = S // CHUNK
    conv_cols_chip = CONV_COLS // N_CHIPS

    pay = jnp.concatenate([c, _pad_lanes(gdn_conv_w[0], D_MODEL), jnp.zeros((3, D_MODEL), F32)], axis=0)
    g1 = allgather8("ag_c", pay)
    c_all = g1[:, 0, :]
    conv_w = jnp.concatenate([g1[2 * j, 1:1 + CONV_K, :conv_cols_chip] for j in range(N_CHIPS)], axis=1)
    g2 = allgather8("ag_mod", ada_fwd(c_all, w_ada[0]))
    mod_rows = jnp.concatenate([g2[2 * j] for j in range(N_CHIPS)], axis=1)
    mod = lax.dynamic_slice_in_dim(mod_rows, me, 1, axis=0) + b_ada
    shift, scale, gate = mod[:, :D_MODEL], mod[:, D_MODEL:2 * D_MODEL], mod[:, 2 * D_MODEL:]

    n_in = w_in.shape[2]
    win4 = allgather_chips("ag_w_in", w_in[0].astype(MXU_DTYPE))
    w_all = _internal_from_blocks(win4, n_in)
    w_out_b = w_out[0].astype(MXU_DTYPE)

    (h_b,) = rowwise_fwd("prologue", f_prologue, S, ts_wide, 1, [(x2, "row", D_MODEL, 0)], [norm_g, scale, shift],
                         [("row", D_MODEL, 0, D_MODEL, MXU_DTYPE)])
    p_all, wout_far = matmul("in_proj", h_b, w_all, F32, tm=1024, tn=1664, tk=2048, rides=[("gather", w_out_b)])
    w_out_full = lax.dynamic_update_index_in_dim(wout_far, w_out_b, chip, axis=0).reshape(2 * WIDTH, D_MODEL)
    pv = jnp.concatenate([b_fgate, gdn_dt_bias, jnp.zeros((1, 128 - 16), F32)], axis=1)
    av = jnp.concatenate([jnp.zeros((1, 8), F32), gdn_A_log, jnp.zeros((1, 128 - 16), F32)], axis=1)
    smalls = smalls_fwd(p_all, pv, av, S, ts)
    frow = jnp.transpose(smalls[:, :H]).reshape(H, nq, 1, tq)
    gc_row = jnp.transpose(smalls[:, 8:16].reshape(nch, CHUNK, H), (0, 2, 1))
    head_b = ("head", D_HEAD, 0, WIDTH, MXU_DTYPE)
    head_f = ("head", D_HEAD, 0, WIDTH, F32)
    pcol = lambda cb: (p_all, "head", D_HEAD, cb)
    qn, kn, vb = rowwise_fwd("fox_pre", f_foxpre, S, ts_head, H,[pcol(CB_FQ), pcol(CB_FK), pcol(CB_FV)],
                             [fox_qn_g, fox_kn_g], [head_b] * 3)
    o_fox, lse = flash_fwd(qn, kn, vb, frow, S, tq)
    y_conv, qkv = conv_act_fwd(p_all, conv_w, S, ts_conv)
    o_gdn, states, tinv = gdn_fwd(qkv, smalls, gc_row, S)
    mixed = lax.empty((S, 2 * WIDTH), MXU_DTYPE)
    (mixed,) = rowwise_fwd("post_fox", f_postf, S, ts_head, H,[(o_fox, "head", D_HEAD, 0), pcol(CB_FZ)], [],
                           [("head", D_HEAD, 0, 2 * WIDTH, MXU_DTYPE, mixed)])
    (mixed,) = rowwise_fwd("post_gdn", f_postg, S, ts_head, H,[(o_gdn, "head", D_HEAD, 0), pcol(CB_GZ)], [gdn_norm_g],
                           [("head", D_HEAD, H, 2 * WIDTH, MXU_DTYPE, mixed)])
    mo = matmul("out_proj", mixed, w_out_full, F32, tm=1024, tn=2048, tk=2048)

    wide = lambda a: (a, "row", D_MODEL, 0)
    dx_res, dmo, d_gate, d_final_g, loss_acc = rowwise_bwd(
        "loss", f_loss, S, ts_wide, 1, [wide(x2), wide(mo), wide(tgt)], [gate, final_g[None, :]], [],
        [(0, "row", D_MODEL, 0, D_MODEL, F32), (1, "row", D_MODEL, 0, D_MODEL, MXU_DTYPE)], primal_sum=True)

    d_mixed = matmul("d_mixed", dmo, w_out_full, F32, tm=1024, tn=2048, tk=2048, trans_b=True)
    g_w_out = matmul("g_w_out", mixed, dmo, F32, tm=1024, tn=2048, tk=2048, trans_a=True)
    dp = lax.empty((S, P_COLS), MXU_DTYPE)
    into_dp = lambda idx, cb: (idx, "head", D_HEAD, cb, P_COLS, MXU_DTYPE, dp)
    do_fox, dp, delta = rowwise_bwd(
        "post_fox_bwd", f_postf, S, ts_head, H,[(o_fox, "head", D_HEAD, 0), pcol(CB_FZ)], [],
        [(d_mixed, "head", D_HEAD, 0)], [(0,) + head_b, into_dp(1, CB_FZ)],
        extra=lambda vals, cv, grads: (jnp.sum(grads[0].astype(MXU_DTYPE).astype(F32) * vals[0], axis=-1,
                                               keepdims=True),),
        extra_outs=[("stat", tq, 0, None, F32)])
    do_gdn, dp, d_gdn_norm_g = rowwise_bwd(
        "post_gdn_bwd", f_postg, S, ts_head, H,[(o_gdn, "head", D_HEAD, 0), pcol(CB_GZ)], [gdn_norm_g],
        [(d_mixed, "head", D_HEAD, 8)], [(0,) + head_f, into_dp(1, CB_GZ)])
    d_qn, d_kn, dp, dfq_row, dfk_row = flash_bwd(qn, kn, vb, do_fox, frow, lse, delta, dp, S, tq)
    dp, d_qn_g, d_kn_g = fox_norm_bwd(p_all, d_qn, d_kn, fox_qn_g, fox_kn_g, dp, S, ts_wide)
    d_qkv, dsm_chunk, dgc_row = gdn_bwd(qkv, smalls, gc_row, states, tinv, do_gdn, S)
    dp, d_conv_w = conv_act_bwd(p_all, y_conv, d_qkv, conv_w, dp, S, ts_conv)
    d_f = jnp.transpose((dfq_row + dfk_row).reshape(H, S))
    d_gc = jnp.transpose(dgc_row, (0, 2, 1)).reshape(S, H)
    dsm = dsm_chunk + jnp.concatenate([d_f, d_gc, jnp.zeros((S, 128 - 16), F32)], axis=1)
    dp, d_pv, d_av = smalls_bwd(p_all, pv, av, dsm, dp, S, ts)
    g_w_all = matmul("g_w_in", h_b, dp, F32, tm=1024, tn=1664, tk=2048, trans_a=True)

    c_idx = jnp.reshape(ic, (1,)).astype(jnp.int32)

    def chip_sums(tag, g4):
        recv = sibling_send_other_half("d2d_" + tag, g4)
        return sum_own_half("sum2_" + tag, g4, recv, c_idx, 128)

    def finish(tag, p4, far):
        own = lax.dynamic_index_in_dim(p4, chip, axis=0, keepdims=False)
        return sibling_merge("merge_" + tag, sum_parts("sum_" + tag, own, far, 128))

    g_int = g_w_all[None]
    (p_int,) = sum_own_half("sum2_w_in", g_int, sibling_send_other_half("d2d_w_in", g_int), c_idx, 128, rounded=False)
    p4_in = jnp.stack([_block_from_internal(p_int[0], j, n_in) for j in range(N_CHIPS)])
    p4_in_sent = p4_in.astype(MXU_DTYPE)
    p4_out, p4_out_sent = chip_sums("w_out", g_w_out.reshape(N_CHIPS, w_out.shape[1], D_MODEL))
    d_h, far_in, far_out = matmul("d_h", dp, w_all, F32, tm=1024, tn=2048, tk=1664, trans_b=True,
                                  rides=[("scatter", p4_in_sent), ("scatter", p4_out_sent)])
    g_w_in_full = finish("w_in", p4_in, far_in)
    g_w_out_full = finish("w_out", p4_out, far_out)
    d_x, d_norm_g, d_scale, d_shift = rowwise_bwd(
        "prologue_bwd", f_prologue_res, S, ts_wide, 1, [wide(x2)], [norm_g, scale, shift],
        [wide(d_h), wide(dx_res)], [(0, "row", D_MODEL, 0, D_MODEL, F32)])

    parts = [d_shift, d_scale, d_gate, d_norm_g, d_final_g, d_conv_w.reshape(1, CONV_K * CONV_COLS), d_qn_g, d_kn_g,
             d_gdn_norm_g, d_pv, d_av, loss_acc]
    n_pay = sum(p.shape[1] for p in parts)
    pay_w = -(-n_pay // 1024) * 1024
    pay2 = jnp.concatenate(parts + [jnp.zeros((1, pay_w - n_pay), F32)], axis=1).reshape(8, pay_w // 8)
    g3 = allgather8("ag_small", pay2)
    tot = sum_devices("sum_small", g3).reshape(1, pay_w)
    dmod_all = g3.reshape(N_DEV, pay_w)[:, :3 * D_MODEL]
    o = 3 * D_MODEL
    g_b_ada = tot[:, :o]
    g_norm_g, g_final_g = tot[:, o:o + D_MODEL], tot[:, o + D_MODEL:o + 2 * D_MODEL]
    o += 2 * D_MODEL
    g_conv_full = tot[:, o:o + CONV_K * CONV_COLS].reshape(CONV_K, CONV_COLS)
    g_conv = lax.dynamic_slice_in_dim(g_conv_full, chip * conv_cols_chip, conv_cols_chip, axis=1)
    o += CONV_K * CONV_COLS
    g_qn_g, g_kn_g, g_gdn_ng = tot[:, o:o + 128], tot[:, o + 128:o + 256], tot[:, o + 256:o + 384]
    g_pv, g_av = tot[:, o + 384:o + 512], tot[:, o + 512:o + 640]
    loss = tot[0, o + 640]
    g_b_fgate, g_dt_bias, g_a_log = g_pv[:, :8], g_pv[:, 8:16], g_av[:, 8:16]

    small_g = [g_norm_g, g_b_ada, g_final_g, g_conv, g_qn_g, g_kn_g, g_gdn_ng, g_b_fgate, g_a_log, g_dt_bias]
    small_upd = adamw_small(
        "adamw_small",
        [norm_g, b_ada, final_g[None, :], gdn_conv_w[0], fox_qn_g, fox_kn_g, gdn_norm_g, b_fgate, gdn_A_log, gdn_dt_bias],
        small_g,
        [m_norm_g, m_b_ada, m_final_g[None, :], m_gdn_conv_w[0], m_fox_qn_g, m_fox_kn_g, m_gdn_norm_g, m_b_fgate,
         m_gdn_A_log, m_gdn_dt_bias],
        [v_norm_g, v_b_ada, v_final_g[None, :], v_gdn_conv_w[0], v_fox_qn_g, v_fox_kn_g, v_gdn_norm_g, v_b_fgate,
         v_gdn_A_log, v_gdn_dt_bias])

    def small_leaves(k):
        vals = small_g if k == 0 else [small_upd[3 * p + k - 1] for p in range(len(small_g))]
        return [v[0] if p == 2 else (v[None] if p == 3 else v) for p, v in enumerate(vals)]

    n_ada = w_ada.shape[2]
    g_w_ada = ada_grad(c_all, lax.dynamic_slice_in_dim(dmod_all, chip * n_ada, n_ada, axis=1))
    ada_out = adamw("adamw_w_ada", w_ada[0], [g_w_ada], m_w_ada[0], v_w_ada[0], 128)

    in_out = [jnp.transpose(o) for o in adamw("adamw_w_in", jnp.transpose(w_in[0]), [jnp.transpose(g_w_in_full)],
                                              jnp.transpose(m_w_in[0]), jnp.transpose(v_w_in[0]), None, tc=256)]
    out_out = adamw("adamw_w_out", w_out[0], [g_w_out_full], m_w_out[0], v_w_out[0], 128)

    res = [loss, d_x[None]]
    for k in range(4):
        s = small_leaves(k)
        big = [ada_out[k][None], in_out[k][None], out_out[k][None]]
        res += [s[0], big[0], s[1], big[1], s[7], s[4], s[5], s[3], s[8], s[9], s[6], big[2], s[2]]
    return tuple(res)
```

```python
import functools

import jax
import jax.numpy as jnp
from jax import lax
from jax.experimental import pallas as pl
from jax.experimental.pallas import tpu as pltpu

F32 = jnp.float32
MXU_DTYPE = jnp.bfloat16
HIGHEST = lax.Precision.HIGHEST
SOLVE_PRECISION = lax.Precision.HIGH
MESH_ID = pl.DeviceIdType.MESH

D_MODEL = 2048
N_HEADS = 8
D_HEAD = 128
WIDTH = N_HEADS * D_HEAD
CHUNK = 64
CONV_K = 4
EPS = 1e-6
NEG = -1e30
ATT_SCALE = D_HEAD ** -0.5
N_SMALL = 3 * N_HEADS
P_COLS = 8 * WIDTH + 128
CB_FQ, CB_FK, CB_FV, CB_FZ, CB_GQ, CB_GK, CB_GV, CB_GZ, CB_SMALL = 0, 8, 16, 24, 32, 40, 48, 56, 64
IN_WIDTH = 8 * WIDTH + N_SMALL
N_CHIPS = 4
N_DEV = 8

ADAM_LR, ADAM_B1, ADAM_B2, ADAM_EPS, ADAM_WD, ADAM_STEP = 0.001, 0.9, 0.999, 1e-08, 0.01, 10


def _dotg(a, b, dims):
    return lax.dot_general(a.astype(MXU_DTYPE), b.astype(MXU_DTYPE), (dims, ((), ())), preferred_element_type=F32)


@jax.custom_vjp
def mm_nn(a, b):
    return _dotg(a, b, ((1,), (0,)))


@jax.custom_vjp
def mm_nt(a, b):
    return _dotg(a, b, ((1,), (1,)))


@jax.custom_vjp
def mm_tn(a, b):
    return _dotg(a, b, ((0,), (0,)))


mm_nn.defvjp(lambda a, b: (mm_nn(a, b), (a, b)), lambda r, g: (mm_nt(g, r[1]), mm_tn(r[0], g)))
mm_nt.defvjp(lambda a, b: (mm_nt(a, b), (a, b)), lambda r, g: (mm_nn(g, r[1]), mm_tn(g, r[0])))
mm_tn.defvjp(lambda a, b: (mm_tn(a, b), (a, b)), lambda r, g: (mm_nt(r[1], g), mm_nn(r[0], g)))


def hdot(a, b):
    return lax.dot_general(a, b, (((1,), (0,)), ((), ())), precision=HIGHEST, preferred_element_type=F32)


def _sigmoid(x):
    return 0.5 * (jnp.tanh(0.5 * x) + 1.0)


def _silu(x):
    return x * _sigmoid(x)


def _softplus(x):
    return jnp.maximum(x, 0.0) + jnp.log(1.0 + jnp.exp(-jnp.abs(x)))


def _log_sigmoid(x):
    return jnp.minimum(x, 0.0) - jnp.log(1.0 + jnp.exp(-jnp.abs(x)))


def _rms(x, g):
    return x * lax.rsqrt(jnp.mean(x * x, axis=-1, keepdims=True) + EPS) * g


def _l2n(x):
    return x * lax.rsqrt(jnp.sum(x * x, axis=-1, keepdims=True) + EPS)


def _tile(n, pref):
    t = min(n, pref)
    assert n % t == 0, (n, pref)
    return t


def _bspec(kind, w, col0, ts):
    if kind == "row":
        return pl.BlockSpec((ts, w), lambda i, h: (i, col0))
    if kind == "head":
        return pl.BlockSpec((ts, w), lambda i, h: (i, col0 + h))
    assert kind == "stat"
    return pl.BlockSpec((1, ts // w, 1, w), lambda i, h: (h, i, 0, 0))


def _ld(ref, kind):
    return ref[...]


def _st(ref, kind, val):
    if kind == "stat":
        w = ref.shape[3]
        for n in range(ref.shape[1]):
            ref[0, n] = _col_to_row(val[n * w:(n + 1) * w, :])
    else:
        ref[...] = val.astype(ref.dtype)


def _full_spec(a):
    nd = a.ndim
    return pl.BlockSpec(a.shape, lambda i, h: (0,) * nd)


def _out_struct(kind, w, S, width_total, dtype, nh):
    if kind == "stat":
        return jax.ShapeDtypeStruct((nh, S // w, 1, w), dtype)
    return jax.ShapeDtypeStruct((S, width_total), dtype)


def rowwise_fwd(name, f, S, ts, nh, ins, params, outs):
    n_in, n_p = len(ins), len(params)
    into = {j: o[5] for j, o in enumerate(outs) if len(o) > 5}
    outs = [o[:5] for o in outs]
    n_al = len(into)

    def body(*refs):
        vals = [_ld(r, s[1]).astype(F32) for r, s in zip(refs[:n_in], ins)]
        pv = [r[...] for r in refs[n_in:n_in + n_p]]
        res = f(*vals, *pv)
        for r, s, o in zip(refs[n_in + n_p + n_al:], outs, res):
            _st(r, s[0], o)

    return pl.pallas_call(
        body, name=name, grid=(S // ts, nh),
        in_specs=[_bspec(k, w, c0, ts) for (_, k, w, c0) in ins] + [_full_spec(p) for p in params]
        + [pl.BlockSpec(memory_space=pl.ANY)] * n_al,
        out_specs=[_bspec(k, w, c0, ts) for (k, w, c0, _, _) in outs],
        out_shape=[jax.ShapeDtypeStruct(into[j].shape, into[j].dtype) if j in into else _out_struct(k, w, S, tw, dt, nh)
                   for j, (k, w, c0, tw, dt) in enumerate(outs)],
        input_output_aliases={n_in + n_p + n: j for n, j in enumerate(sorted(into))},
    )(*[a for (a, _, _, _) in ins], *params, *[into[j] for j in sorted(into)])


def rowwise_bwd(name, f, S, ts, nh, ins, params, cts, row_grads, extra=None, extra_outs=(), primal_sum=False):
    n_in, n_p, n_ct = len(ins), len(params), len(cts)
    n_rg, n_ex = len(row_grads), len(extra_outs)
    into = {j: rg[6] for j, rg in enumerate(row_grads) if len(rg) > 6}
    row_grads = [rg[:6] for rg in row_grads]
    n_al = len(into)

    def body(*refs):
        first = (pl.program_id(0) == 0) & (pl.program_id(1) == 0)
        in_refs = refs[:n_in]
        p_refs = refs[n_in:n_in + n_p]
        ct_refs = refs[n_in + n_p:n_in + n_p + n_ct]
        o = n_in + n_p + n_ct + n_al
        rg_refs = refs[o:o + n_rg]
        ex_refs = refs[o + n_rg:o + n_rg + n_ex]
        acc_refs = refs[o + n_rg + n_ex:]
        vals = [_ld(r, s[1]).astype(F32) for r, s in zip(in_refs, ins)]
        pv = [r[...] for r in p_refs]
        res, vjp = jax.vjp(f, *vals, *pv)
        if primal_sum:
            cv = [jnp.ones_like(res[0])] + [_ld(r, s[1]).astype(F32) for r, s in zip(ct_refs, cts)]
        else:
            cv = [_ld(r, s[1]).astype(F32) for r, s in zip(ct_refs, cts)]
        grads = vjp(tuple(cv))
        for r, s in zip(rg_refs, row_grads):
            _st(r, s[1], grads[s[0]])
        if extra is not None:
            for r, s, e in zip(ex_refs, extra_outs, extra(vals, cv, grads)):
                _st(r, s[0], e)

        @pl.when(first)
        def _():
            for r in acc_refs:
                r[...] = jnp.zeros_like(r)

        for r, g in zip(acc_refs[:n_p], grads[n_in:]):
            r[...] += g
        if primal_sum:
            acc_refs[n_p][...] += jnp.sum(res[0])

    acc_shapes = [jax.ShapeDtypeStruct(p.shape, F32) for p in params]
    acc_specs = [_full_spec(p) for p in params]
    if primal_sum:
        acc_shapes.append(jax.ShapeDtypeStruct((1, 128), F32))
        acc_specs.append(pl.BlockSpec((1, 128), lambda i, h: (0, 0)))
    rg_shapes = [jax.ShapeDtypeStruct(into[j].shape, into[j].dtype) if j in into else _out_struct(k, w, S, tw, dt, nh)
                 for j, (_, k, w, c0, tw, dt) in enumerate(row_grads)]
    first_alias = n_in + n_p + n_ct
    return pl.pallas_call(
        body, name=name, grid=(S // ts, nh),
        in_specs=[_bspec(k, w, c0, ts) for (_, k, w, c0) in ins] + [_full_spec(p) for p in params]
        + [_bspec(k, w, c0, ts) for (_, k, w, c0) in cts] + [pl.BlockSpec(memory_space=pl.ANY)] * n_al,
        out_specs=[_bspec(k, w, c0, ts) for (_, k, w, c0, _, _) in row_grads]
        + [_bspec(k, w, c0, ts) for (k, w, c0, _, _) in extra_outs] + acc_specs,
        out_shape=rg_shapes + [_out_struct(k, w, S, tw, dt, nh) for (k, w, c0, tw, dt) in extra_outs] + acc_shapes,
        input_output_aliases={first_alias + n: j for n, j in enumerate(sorted(into))},
    )(*[a for (a, _, _, _) in ins], *params, *[a for (a, _, _, _) in cts], *[into[j] for j in sorted(into)])


def f_prologue(x, ng, sc, sh):
    return (_rms(x, ng) * (1.0 + sc) + sh,)


def f_prologue_res(x, ng, sc, sh):
    return (_rms(x, ng) * (1.0 + sc) + sh, x)


def f_loss(x, mo, tgt, gate, fg):
    y = _rms(x + gate * mo, fg)
    return (0.5 * jnp.mean(jnp.square(y - tgt), axis=-1, keepdims=True),)


def f_foxpre(fq, fk, fv, qg, kg):
    return (_rms(fq, qg) * ATT_SCALE, _rms(fk, kg), fv)


def f_postf(o, z):
    return (o * _silu(z),)


def f_postg(o, z, ng):
    return (_rms(o, ng) * _silu(z),)


def fox_norm_bwd(p_all, d_qn, d_kn, qg, kg, dp, S, ts):
    def body(p_ref, dq_ref, dk_ref, qg_ref, kg_ref, dp_in, o_ref, dqg_ref, dkg_ref):
        @pl.when(pl.program_id(0) == 0)
        def _():
            dqg_ref[...] = jnp.zeros_like(dqg_ref)
            dkg_ref[...] = jnp.zeros_like(dkg_ref)

        for ct_ref, g_ref, dg_ref, col0 in ((dq_ref, qg_ref, dqg_ref, 0), (dk_ref, kg_ref, dkg_ref, WIDTH)):
            dg = jnp.zeros((1, D_HEAD), F32)
            for h in range(N_HEADS):
                _, vjp = jax.vjp(_rms, p_ref[:, col0 + h * D_HEAD:col0 + (h + 1) * D_HEAD], g_ref[...])
                dx, dg_h = vjp(ct_ref[:, h * D_HEAD:(h + 1) * D_HEAD])
                o_ref[:, col0 + h * D_HEAD:col0 + (h + 1) * D_HEAD] = dx.astype(o_ref.dtype)
                dg = dg + dg_h
            dg_ref[...] += dg

    gain = pl.BlockSpec((1, D_HEAD), lambda i: (0, 0))
    both = pl.BlockSpec((ts, 2 * WIDTH), lambda i: (i, 0))
    one = pl.BlockSpec((ts, WIDTH), lambda i: (i, 0))
    return pl.pallas_call(
        body, name="fox_norm_bwd", grid=(S // ts,),
        in_specs=[both, one, one, gain, gain, pl.BlockSpec(memory_space=pl.ANY)],
        out_specs=[both, gain, gain],
        out_shape=[jax.ShapeDtypeStruct(dp.shape, dp.dtype), jax.ShapeDtypeStruct((1, D_HEAD), F32),
                   jax.ShapeDtypeStruct((1, D_HEAD), F32)],
        input_output_aliases={5: 0},
    )(p_all, d_qn, d_kn, qg, kg, dp)


def _chip_exchange(kind, x_ref, out_ref, send_sems, recv_sems):
    x, y, c = lax.axis_index("x"), lax.axis_index("y"), lax.axis_index("c")
    sends, arrivals = [], []
    for r in range(1, N_CHIPS):
        px, py = _flip(x, r & 2), _flip(y, r & 1)
        if kind == "scatter":
            src, dst, landed = x_ref.at[2 * px + py], out_ref.at[r - 1], out_ref.at[r - 1]
        else:
            src, dst, landed = x_ref, out_ref.at[2 * x + y], out_ref.at[2 * px + py]
        mk = functools.partial(pltpu.make_async_remote_copy, send_sem=send_sems.at[r - 1], recv_sem=recv_sems.at[r - 1],
                               device_id=(px, py, c), device_id_type=MESH_ID)
        sends.append(mk(src_ref=src, dst_ref=dst))
        arrivals.append(mk(src_ref=src, dst_ref=landed))
    return sends, arrivals


def _exchange_shape(kind, x):
    return jax.ShapeDtypeStruct(((N_CHIPS - 1,) + x.shape[1:]) if kind == "scatter" else ((N_CHIPS,) + x.shape), x.dtype)


def matmul(name, a, b, out_dtype, tm=512, tn=512, tk=2048, trans_a=False, trans_b=False, rides=()):
    M, K = a.shape[::-1] if trans_a else a.shape
    N, K2 = b.shape if trans_b else b.shape[::-1]
    assert K == K2
    tm, tn, tk = _tile(M, tm), _tile(N, tn), _tile(K, tk)
    ni, nj, nk = M // tm, N // tn, K // tk
    a_spec = pl.BlockSpec((tk, tm), lambda i, j, k: (k, i)) if trans_a else pl.BlockSpec((tm, tk), lambda i, j, k: (i, k))
    b_spec = pl.BlockSpec((tn, tk), lambda i, j, k: (j, k)) if trans_b else pl.BlockSpec((tk, tn), lambda i, j, k: (k, j))
    n_ride = len(rides)

    def body(a_ref, b_ref, *rest):
        ride_in, o_ref, ride_out = rest[:n_ride], rest[n_ride], rest[n_ride + 1:2 * n_ride + 1]
        scr = rest[2 * n_ride + 1:]
        i, j, k = pl.program_id(0), pl.program_id(1), pl.program_id(2)
        exchanges = [_chip_exchange(kind, ride_in[n], ride_out[n], scr[2 * n], scr[2 * n + 1])
                     for n, (kind, _) in enumerate(rides)]

        @pl.when((i == 0) & (j == 0) & (k == 0))
        def _():
            for sends, _ in exchanges:
                for cp in sends:
                    cp.start()

        part = lax.dot_general(a_ref[...], b_ref[...], (((0 if trans_a else 1,), (1 if trans_b else 0,)), ((), ())),
                               preferred_element_type=F32)
        if nk == 1:
            o_ref[...] = part.astype(o_ref.dtype)
        else:
            acc = scr[2 * n_ride]

            @pl.when(k == 0)
            def _():
                acc[...] = part

            @pl.when(k > 0)
            def _():
                acc[...] += part

            @pl.when(k == nk - 1)
            def _():
                o_ref[...] = acc[...].astype(o_ref.dtype)

        @pl.when((i == ni - 1) & (j == nj - 1) & (k == nk - 1))
        def _():
            for sends, arrivals in exchanges:
                for cp in arrivals:
                    cp.wait_recv()
                for cp in sends:
                    cp.wait_send()

    sems = [pltpu.SemaphoreType.DMA((N_CHIPS - 1,))] * (2 * n_ride)
    any_spec = pl.BlockSpec(memory_space=pl.ANY)
    res = pl.pallas_call(
        body, name=name, grid=(ni, nj, nk),
        in_specs=[a_spec, b_spec] + [any_spec] * n_ride,
        out_specs=[pl.BlockSpec((tm, tn), lambda i, j, k: (i, j))] + [any_spec] * n_ride,
        out_shape=[jax.ShapeDtypeStruct((M, N), out_dtype)] + [_exchange_shape(kind, x) for kind, x in rides],
        scratch_shapes=sems + ([] if nk == 1 else [pltpu.VMEM((tm, tn), F32)]),
        compiler_params=pltpu.CompilerParams(
            dimension_semantics=("arbitrary",) * 3 if rides else ("parallel", "parallel", "arbitrary")),
    )(a, b, *[x for _, x in rides])
    return tuple(res) if rides else res[0]


def _small_f(z, pv, av):
    lane = lax.broadcasted_iota(jnp.int32, z.shape, 1)
    zz = z + pv
    logf = _log_sigmoid(zz)
    gval = -jnp.exp(av) * _softplus(zz)
    beta = _sigmoid(zz)
    return jnp.where(lane < 8, logf, jnp.where(lane < 16, gval, jnp.where(lane < 24, beta, 0.0)))


def _tri(ts, upper):
    r = lax.broadcasted_iota(jnp.int32, (ts, ts), 0)
    c = lax.broadcasted_iota(jnp.int32, (ts, ts), 1)
    keep = (r <= c) if upper else (r >= c)
    same_chunk = (r // CHUNK) == (c // CHUNK)
    return keep.astype(F32), (keep & same_chunk).astype(F32)


def smalls_fwd(p_all, pv, av, S, ts):
    nt = S // ts

    def body(z_ref, pv_ref, av_ref, o_ref, carry):
        i = pl.program_id(0)

        @pl.when(i == 0)
        def _():
            carry[...] = jnp.zeros_like(carry)

        e = _small_f(z_ref[...], pv_ref[...], av_ref[...])
        lane = lax.broadcasted_iota(jnp.int32, e.shape, 1)
        l_all, l_chunk = _tri(ts, upper=False)
        cum_all = hdot(l_all, e) + carry[...]
        cum_chunk = hdot(l_chunk, e)
        carry[...] = cum_all[ts - 1:ts, :]
        o_ref[...] = jnp.where(lane < 8, cum_all, jnp.where(lane < 16, cum_chunk, e))

    return pl.pallas_call(
        body, name="smalls_fwd", grid=(nt,),
        in_specs=[pl.BlockSpec((ts, 128), lambda i: (i, CB_SMALL)), pl.BlockSpec((1, 128), lambda i: (0, 0)),
                  pl.BlockSpec((1, 128), lambda i: (0, 0))],
        out_specs=pl.BlockSpec((ts, 128), lambda i: (i, 0)),
        out_shape=jax.ShapeDtypeStruct((S, 128), F32),
        scratch_shapes=[pltpu.VMEM((1, 128), F32)],
    )(p_all, pv, av)


def smalls_bwd(p_all, pv, av, dsm, dp, S, ts):
    nt = S // ts

    def body(z_ref, pv_ref, av_ref, d_ref, dp_in, dz_ref, dpv_ref, dav_ref, carry):
        i = pl.program_id(0)

        @pl.when(i == 0)
        def _():
            carry[...] = jnp.zeros_like(carry)
            dpv_ref[...] = jnp.zeros_like(dpv_ref)
            dav_ref[...] = jnp.zeros_like(dav_ref)

        d = d_ref[...]
        lane = lax.broadcasted_iota(jnp.int32, d.shape, 1)
        u_all, u_chunk = _tri(ts, upper=True)
        rev_all = hdot(u_all, d) + carry[...]
        rev_chunk = hdot(u_chunk, d)
        carry[...] = rev_all[0:1, :]
        de = jnp.where(lane < 8, rev_all, jnp.where(lane < 16, rev_chunk, d))
        _, vjp = jax.vjp(_small_f, z_ref[...], pv_ref[...], av_ref[...])
        dz, dpv, dav = vjp(de)
        dz_ref[...] = dz.astype(dz_ref.dtype)
        dpv_ref[...] += dpv
        dav_ref[...] += dav

    rev = lambda i: (nt - 1 - i, 0)
    small_cols = pl.BlockSpec((ts, 128), lambda i: (nt - 1 - i, CB_SMALL))
    return pl.pallas_call(
        body, name="smalls_bwd", grid=(nt,),
        in_specs=[small_cols, pl.BlockSpec((1, 128), lambda i: (0, 0)), pl.BlockSpec((1, 128), lambda i: (0, 0)),
                  pl.BlockSpec((ts, 128), rev), pl.BlockSpec(memory_space=pl.ANY)],
        out_specs=[small_cols, pl.BlockSpec((1, 128), lambda i: (0, 0)), pl.BlockSpec((1, 128), lambda i: (0, 0))],
        out_shape=[jax.ShapeDtypeStruct(dp.shape, dp.dtype), jax.ShapeDtypeStruct((1, 128), F32),
                   jax.ShapeDtypeStruct((1, 128), F32)],
        scratch_shapes=[pltpu.VMEM((1, 128), F32)],
        input_output_aliases={4: 0},
    )(p_all, pv, av, dsm, dp)


def _col_to_row(col):
    return jnp.transpose(jnp.broadcast_to(col, (col.shape[0], 128)))[0:1, :]


def _row_to_col(row):
    return jnp.transpose(jnp.broadcast_to(row, (128, row.shape[1])))[:, 0:1]


def _causal_mask(s, keys_first):
    r = lax.broadcasted_iota(jnp.int32, s.shape, 0)
    c = lax.broadcasted_iota(jnp.int32, s.shape, 1)
    return jnp.where((r <= c) if keys_first else (c <= r), s, NEG)


def flash_fwd(qs, kn, vb, frow, S, tq):
    nq = S // tq
    tk = tq
    hp = 2
    wide = hp * D_HEAD
    head = lambda ref, h, rows=slice(None): ref[rows, h * D_HEAD:(h + 1) * D_HEAD]

    def body(q_ref, k_ref, v_ref, fr_ref, o_ref, lse_ref):
        i = pl.program_id(1)
        qs_ = [head(q_ref, h) for h in range(hp)]
        f_base = [fr_ref[h, i][:, 0:1] for h in range(hp)]

        def tile(j, carry, diagonal):
            rows = pl.ds(pl.multiple_of(j * tk, tk), tk)
            out = []
            for h in range(hp):
                m, l, acc = carry[3 * h:3 * h + 3]
                s = mm_nt(qs_[h], head(k_ref, h, rows)) - (fr_ref[h, j] - f_base[h])
                if diagonal:
                    s = _causal_mask(s, keys_first=False)
                m_new = jnp.maximum(m, jnp.max(s, axis=-1, keepdims=True))
                a = jnp.exp(m - m_new)
                p = jnp.exp(s - m_new)
                out += [m_new, a * l + jnp.sum(p, axis=-1, keepdims=True), a * acc + mm_nn(p, head(v_ref, h, rows))]
            return tuple(out)

        init = (jnp.full((tq, 1), NEG, F32), jnp.zeros((tq, 1), F32), jnp.zeros((tq, D_HEAD), F32)) * hp
        res = tile(i, lax.fori_loop(0, i, lambda j, c: tile(j, c, False), init), True)
        for h in range(hp):
            m, l, acc = res[3 * h:3 * h + 3]
            o_ref[:, h * D_HEAD:(h + 1) * D_HEAD] = acc / l
            lse_ref[h, 0] = _col_to_row(m + jnp.log(l))

    return pl.pallas_call(
        body, name="flash_fwd", grid=(N_HEADS // hp, nq),
        in_specs=[pl.BlockSpec((tq, wide), lambda h, i: (i, h)), pl.BlockSpec((S, wide), lambda h, i: (0, h)),
                  pl.BlockSpec((S, wide), lambda h, i: (0, h)),
                  pl.BlockSpec((hp, nq, 1, tk), lambda h, i: (h, 0, 0, 0))],
        out_specs=[pl.BlockSpec((tq, wide), lambda h, i: (i, h)),
                   pl.BlockSpec((hp, 1, 1, tq), lambda h, i: (h, i, 0, 0))],
        out_shape=[jax.ShapeDtypeStruct((S, WIDTH), F32), jax.ShapeDtypeStruct((N_HEADS, nq, 1, tq), F32)],
    )(qs, kn, vb, frow)


def flash_bwd(qs, kn, vb, do, frow, lse_row, delta_row, dp, S, tq):
    nq = S // tq
    tk = tq
    hp = 2
    wide = hp * D_HEAD
    head = lambda ref, h, rows=slice(None): ref[rows, h * D_HEAD:(h + 1) * D_HEAD]

    def body(q_ref, k_ref, v_ref, do_ref, fr_ref, lse_ref, dl_ref, dp_in, dq_ref, dk_ref, dv_ref, dfq_ref, dfk_ref):
        j = pl.program_id(1)
        ks = [head(k_ref, h) for h in range(hp)]
        vs = [head(v_ref, h) for h in range(hp)]
        fks = [_row_to_col(fr_ref[h, j]) for h in range(hp)]

        @pl.when(j == 0)
        def _():
            dq_ref[...] = jnp.zeros_like(dq_ref)
            dfq_ref[...] = jnp.zeros_like(dfq_ref)

        def tile(i, carry):
            rows = pl.ds(pl.multiple_of(i * tq, tq), tq)
            out = []
            for h in range(hp):
                dk, dv, dfk = carry[3 * h:3 * h + 3]
                q = head(q_ref, h, rows)
                do_ = head(do_ref, h, rows)
                st = mm_nt(ks[h], q) - (fks[h] - fr_ref[h, i][:, 0:1])
                pt = jnp.exp(st - lse_ref[h, i])
                dst = pt * (mm_nt(vs[h], do_) - dl_ref[h, i])
                dq_ref[rows, h * D_HEAD:(h + 1) * D_HEAD] += mm_tn(dst, ks[h]) * ATT_SCALE
                dfq_ref[h, i] += jnp.sum(dst, axis=0, keepdims=True)
                out += [dk + mm_nn(dst, q), dv + mm_nn(pt, do_), dfk - jnp.sum(dst, axis=-1, keepdims=True)]
            return tuple(out)

        def diagonal_tile():
            half = tk // 2
            out = []
            for h in range(hp):
                cols = slice(h * D_HEAD, (h + 1) * D_HEAD)
                f0 = fr_ref[h, j][:, 0:1]
                parts = []
                for a in range(2):
                    keys = slice(a * half, (a + 1) * half)
                    k_a, v_a, fk_a = ks[h][keys], vs[h][keys], fks[h][keys]
                    dk = jnp.zeros((half, D_HEAD), F32)
                    dv = jnp.zeros((half, D_HEAD), F32)
                    dfk = jnp.zeros((half, 1), F32)
                    for b in range(a, 2):
                        rows = pl.ds(pl.multiple_of(j * tq + b * half, half), half)
                        lanes = slice(b * half, (b + 1) * half)
                        q, do_ = q_ref[rows, cols], do_ref[rows, cols]
                        st = mm_nt(k_a, q) - (fk_a - f0)
                        if a == b:
                            st = _causal_mask(st, keys_first=True)
                        pt = jnp.exp(st - lse_ref[h, j][:, lanes])
                        dst = pt * (mm_nt(v_a, do_) - dl_ref[h, j][:, lanes])
                        dq_ref[rows, cols] += mm_tn(dst, k_a) * ATT_SCALE
                        dfq_ref[h, j, :, lanes] += jnp.sum(dst, axis=0, keepdims=True)
                        dk, dv = dk + mm_nn(dst, q), dv + mm_nn(pt, do_)
                        dfk = dfk - jnp.sum(dst, axis=-1, keepdims=True)
                    parts.append((dk, dv, dfk))
                out += [jnp.concatenate([p[n] for p in parts], axis=0) for n in range(3)]
            return tuple(out)

        res = lax.fori_loop(j + 1, nq, tile, diagonal_tile())
        for h in range(hp):
            dk, dv, dfk = res[3 * h:3 * h + 3]
            dk_ref[:, h * D_HEAD:(h + 1) * D_HEAD] = dk
            dv_ref[:, h * D_HEAD:(h + 1) * D_HEAD] = dv.astype(dv_ref.dtype)
            dfk_ref[h, 0] = _col_to_row(dfk)

    blk = pl.BlockSpec((tk, wide), lambda h, j: (j, h))
    full = pl.BlockSpec((S, wide), lambda h, j: (0, h), pipeline_mode=pl.Buffered(1))
    rows = pl.BlockSpec((hp, nq, 1, tq), lambda h, j: (h, 0, 0, 0))
    stat = jax.ShapeDtypeStruct((N_HEADS, nq, 1, tq), F32)
    return pl.pallas_call(
        body, name="flash_bwd", grid=(N_HEADS // hp, nq),
        in_specs=[full, blk, blk, full, rows, rows, rows, pl.BlockSpec(memory_space=pl.ANY)],
        out_specs=[full, blk, pl.BlockSpec((tk, wide), lambda h, j: (j, CB_FV // hp + h)), rows,
                   pl.BlockSpec((hp, 1, 1, tk), lambda h, j: (h, j, 0, 0))],
        out_shape=[jax.ShapeDtypeStruct((S, WIDTH), F32), jax.ShapeDtypeStruct((S, WIDTH), F32),
                   jax.ShapeDtypeStruct(dp.shape, dp.dtype), stat, stat],
        input_output_aliases={7: 2},
    )(qs, kn, vb, do, frow, lse_row, delta_row, dp)


CONV_COLS = 3 * WIDTH
CONV_TC = 512


def _gdn_act(y, group):
    s = _silu(y)
    if group == 2:
        return s
    return _l2n(s) * ATT_SCALE if group == 0 else _l2n(s)


def _gdn_act_bwd(y, d, group):
    _, vjp = jax.vjp(functools.partial(_gdn_act, group=group), y)
    return vjp(d)[0]


def conv_act_fwd(p_all, conv_w, S, ts):
    nt, tc = S // ts, CONV_TC
    cb0 = CB_GQ * 128 // tc
    r8 = ts // 8
    tiles_per_group = WIDTH // tc

    def body(x_ref, halo_ref, w_ref, y_ref, a_ref, scr):
        c, i = pl.program_id(0), pl.program_id(1)
        scr[0:8, :] = jnp.where(i > 0, halo_ref[...], 0.0)
        scr[8:8 + ts, :] = x_ref[...]
        w = w_ref[...]
        y = w[3:4, :] * x_ref[...]
        for j in range(CONV_K - 1):
            y = y + w[j:j + 1, :] * scr[5 + j:5 + j + ts, :]
        y_ref[...] = y
        for group in range(3):
            @pl.when(c // tiles_per_group == group)
            def _(group=group):
                for h in range(tc // D_HEAD):
                    sl = slice(h * D_HEAD, (h + 1) * D_HEAD)
                    a_ref[:, sl] = _gdn_act(y_ref[:, sl], group)

    out = pl.BlockSpec((ts, tc), lambda c, i: (i, c))
    return pl.pallas_call(
        body, name="conv_act_fwd", grid=(CONV_COLS // tc, nt),
        in_specs=[pl.BlockSpec((ts, tc), lambda c, i: (i, cb0 + c)),
                  pl.BlockSpec((8, tc), lambda c, i: (jnp.maximum(i * r8 - 1, 0), cb0 + c)),
                  pl.BlockSpec((CONV_K, tc), lambda c, i: (0, c))],
        out_specs=[out, out],
        out_shape=[jax.ShapeDtypeStruct((S, CONV_COLS), F32)] * 2,
        scratch_shapes=[pltpu.VMEM((ts + 8, tc), F32)],
    )(p_all, p_all, conv_w)


def conv_act_bwd(p_all, y, dqkv, conv_w, dp, S, ts):
    nt, tc = S // ts, CONV_TC
    cb0 = CB_GQ * 128 // tc
    r8 = ts // 8
    tiles_per_group = WIDTH // tc

    def body(x_ref, xh_ref, y_ref, yh_ref, d_ref, dh_ref, w_ref, dp_in, dx_ref, dw_ref, xs, ds):
        c, i = pl.program_id(0), pl.program_id(1)
        xs[0:8, :] = jnp.where(i > 0, xh_ref[...], 0.0)
        xs[8:8 + ts, :] = x_ref[...]
        for group in range(3):
            @pl.when(c // tiles_per_group == group)
            def _(group=group):
                for h in range(tc // D_HEAD):
                    sl = slice(h * D_HEAD, (h + 1) * D_HEAD)
                    ds[0:ts, sl] = _gdn_act_bwd(y_ref[:, sl], d_ref[:, sl], group)
                    ds[ts:ts + 8, sl] = jnp.where(i < nt - 1, _gdn_act_bwd(yh_ref[:, sl], dh_ref[:, sl], group), 0.0)
        w = w_ref[...]
        d = ds[0:ts, :]
        dx = w[3:4, :] * d
        for j in range(CONV_K - 1):
            dx = dx + w[j:j + 1, :] * ds[3 - j:3 - j + ts, :]
        dx_ref[...] = dx.astype(dx_ref.dtype)

        @pl.when(i == 0)
        def _():
            dw_ref[...] = jnp.zeros_like(dw_ref)

        rows = [jnp.sum(d * xs[5 + j:5 + j + ts, :], axis=0, keepdims=True) for j in range(CONV_K - 1)]
        rows.append(jnp.sum(d * x_ref[...], axis=0, keepdims=True))
        dw_ref[...] += jnp.concatenate(rows, axis=0)

    tile = pl.BlockSpec((ts, tc), lambda c, i: (i, c))
    next8 = pl.BlockSpec((8, tc), lambda c, i: (jnp.minimum((i + 1) * r8, nt * r8 - 1), c))
    return pl.pallas_call(
        body, name="conv_act_bwd", grid=(CONV_COLS // tc, nt),
        in_specs=[pl.BlockSpec((ts, tc), lambda c, i: (i, cb0 + c)),
                  pl.BlockSpec((8, tc), lambda c, i: (jnp.maximum(i * r8 - 1, 0), cb0 + c)),
                  tile, next8, tile, next8,
                  pl.BlockSpec((CONV_K, tc), lambda c, i: (0, c)), pl.BlockSpec(memory_space=pl.ANY)],
        out_specs=[pl.BlockSpec((ts, tc), lambda c, i: (i, cb0 + c)), pl.BlockSpec((CONV_K, tc), lambda c, i: (0, c))],
        out_shape=[jax.ShapeDtypeStruct(dp.shape, dp.dtype), jax.ShapeDtypeStruct((CONV_K, CONV_COLS), F32)],
        scratch_shapes=[pltpu.VMEM((ts + 8, tc), F32), pltpu.VMEM((ts + 8, tc), F32)],
        input_output_aliases={7: 0},
    )(p_all, p_all, y, y, dqkv, dqkv, conv_w, dp)


def _bdot(a, b, ca, cb):
    return lax.dot_general(a.astype(MXU_DTYPE), b.astype(MXU_DTYPE), (((ca,), (cb,)), ((0,), (0,))),
                           preferred_element_type=F32)


@jax.custom_vjp
def bmm_nn(a, b):
    return _bdot(a, b, 2, 1)


@jax.custom_vjp
def bmm_nt(a, b):
    return _bdot(a, b, 2, 2)


@jax.custom_vjp
def bmm_tn(a, b):
    return _bdot(a, b, 1, 1)


bmm_nn.defvjp(lambda a, b: (bmm_nn(a, b), (a, b)), lambda r, g: (bmm_nt(g, r[1]), bmm_tn(r[0], g)))
bmm_nt.defvjp(lambda a, b: (bmm_nt(a, b), (a, b)), lambda r, g: (bmm_nn(g, r[1]), bmm_tn(g, r[0])))
bmm_tn.defvjp(lambda a, b: (bmm_tn(a, b), (a, b)), lambda r, g: (bmm_nt(r[1], g), bmm_nn(r[0], g)))


def bdot_hi(a, b, ca=2, cb=1):
    return lax.dot_general(a, b, (((ca,), (cb,)), ((0,), (0,))), precision=SOLVE_PRECISION,
                           preferred_element_type=F32)


def _tri_inv_product(m):
    ii = lax.broadcasted_iota(jnp.int32, m.shape, 1)
    jj = lax.broadcasted_iota(jnp.int32, m.shape, 2)
    t = (ii == jj).astype(F32) - m
    pw = bdot_hi(m, m)
    for _ in range(4):
        t = t + bdot_hi(t, pw)
        pw = bdot_hi(pw, pw)
    return t + bdot_hi(t, pw)


def _tri_inv_bwd(t, g):
    return -bdot_hi(bdot_hi(t, g, 1, 1), t, 2, 2)


@jax.custom_vjp
def tri_inv(m):
    return _tri_inv_product(m)


@jax.custom_vjp
def tri_inv_known(m, t):
    return t


tri_inv.defvjp(lambda m: (lambda t: (t, t))(_tri_inv_product(m)), lambda t, g: (_tri_inv_bwd(t, g),))
tri_inv_known.defvjp(lambda m, t: (t, t), lambda t, g: (_tri_inv_bwd(t, g), jnp.zeros_like(t)))


def chunk_fn(q, k, v, gc_col, gc_row, beta, state, t_known=None):
    shape = (N_HEADS, CHUNK, CHUNK)
    ii = lax.broadcasted_iota(jnp.int32, shape, 1)
    jj = lax.broadcasted_iota(jnp.int32, shape, 2)
    lower = ii >= jj
    strict = ii > jj
    decay = jnp.exp(jnp.where(lower, gc_col - gc_row, NEG))
    kb = k * beta
    vb = v * beta
    m = jnp.where(strict, bmm_nt(kb, k) * decay, 0.0)
    t = tri_inv(m) if t_known is None else tri_inv_known(m, t_known)
    u = bdot_hi(t, vb)
    w = bdot_hi(t, kb * jnp.exp(gc_col))
    attn = jnp.where(lower, bmm_nt(q, k) * decay, 0.0)
    v_new = u - bmm_nn(w, state)
    o = bmm_nn(q * jnp.exp(gc_col), state) + bmm_nn(attn, v_new)
    g_last = gc_col[:, CHUNK - 1:CHUNK, :]
    k_dec = k * jnp.exp(g_last - gc_col)
    new_state = state * jnp.exp(g_last) + bmm_tn(k_dec, v_new)
    return (o, new_state, t) if t_known is None else (o, new_state)


GDN_CHUNKS_PER_STEP = 4


def _heads(ref, rows):
    return jnp.stack([ref[rows, h * D_HEAD:(h + 1) * D_HEAD] for h in range(N_HEADS)], axis=0)


def _head_cols(ref, rows, lane0):
    return jnp.stack([ref[rows, lane0 + h:lane0 + h + 1] for h in range(N_HEADS)], axis=0)


def gdn_fwd(qkv, smalls, gc_row, S):
    n = S // CHUNK
    per = _tile(n, GDN_CHUNKS_PER_STEP)
    rows_per = per * CHUNK

    def body(q_ref, k_ref, v_ref, sm_ref, gr_ref, o_ref, st_ref, t_ref, state):
        @pl.when(pl.program_id(0) == 0)
        def _():
            state[...] = jnp.zeros_like(state)

        s = state[...]
        for b in range(per):
            rows = pl.ds(b * CHUNK, CHUNK)
            st_ref[b] = s
            o, s, t = chunk_fn(_heads(q_ref, rows), _heads(k_ref, rows), _heads(v_ref, rows),
                               _head_cols(sm_ref, rows, 8), gr_ref[b][:, None, :], _head_cols(sm_ref, rows, 16), s)
            for h in range(N_HEADS):
                o_ref[rows, h * D_HEAD:(h + 1) * D_HEAD] = o[h]
            t_ref[b] = t
        state[...] = s

    return pl.pallas_call(
        body, name="gdn_chunk_fwd", grid=(n // per,),
        in_specs=[pl.BlockSpec((rows_per, WIDTH), lambda i, g=g: (i, g)) for g in range(3)]
        + [pl.BlockSpec((rows_per, 128), lambda i: (i, 0)), pl.BlockSpec((per, N_HEADS, CHUNK), lambda i: (i, 0, 0))],
        out_specs=[pl.BlockSpec((rows_per, WIDTH), lambda i: (i, 0)),
                   pl.BlockSpec((per, N_HEADS, D_HEAD, D_HEAD), lambda i: (i, 0, 0, 0)),
                   pl.BlockSpec((per, N_HEADS, CHUNK, CHUNK), lambda i: (i, 0, 0, 0))],
        out_shape=[jax.ShapeDtypeStruct((S, WIDTH), F32), jax.ShapeDtypeStruct((n, N_HEADS, D_HEAD, D_HEAD), F32),
                   jax.ShapeDtypeStruct((n, N_HEADS, CHUNK, CHUNK), F32)],
        scratch_shapes=[pltpu.VMEM((N_HEADS, D_HEAD, D_HEAD), F32)],
    )(qkv, qkv, qkv, smalls, gc_row)


def gdn_bwd(qkv, smalls, gc_row, states, tinv, do, S):
    n = S // CHUNK
    per = _tile(n, GDN_CHUNKS_PER_STEP)
    rows_per = per * CHUNK
    nb = n // per

    def body(q_ref, k_ref, v_ref, sm_ref, gr_ref, st_ref, t_ref, do_ref, dqkv_ref, dsm_ref, dgr_ref, dstate):
        @pl.when(pl.program_id(0) == 0)
        def _():
            dstate[...] = jnp.zeros_like(dstate)

        ds = dstate[...]
        for b in reversed(range(per)):
            rows = pl.ds(b * CHUNK, CHUNK)
            t_saved = t_ref[b]
            _, vjp = jax.vjp(lambda *a: chunk_fn(*a, t_known=t_saved), _heads(q_ref, rows), _heads(k_ref, rows),
                             _heads(v_ref, rows), _head_cols(sm_ref, rows, 8), gr_ref[b][:, None, :],
                             _head_cols(sm_ref, rows, 16), st_ref[b])
            dq, dk, dv, dgc, dgr, dbt, ds = vjp((_heads(do_ref, rows), ds))
            for h in range(N_HEADS):
                dqkv_ref[rows, h * D_HEAD:(h + 1) * D_HEAD] = dq[h]
                dqkv_ref[rows, WIDTH + h * D_HEAD:WIDTH + (h + 1) * D_HEAD] = dk[h]
                dqkv_ref[rows, 2 * WIDTH + h * D_HEAD:2 * WIDTH + (h + 1) * D_HEAD] = dv[h]
            cols = [dgc[h] for h in range(N_HEADS)] + [dbt[h] for h in range(N_HEADS)]
            dsm_ref[rows, :] = jnp.concatenate([jnp.zeros((CHUNK, 8), F32)] + cols
                                               + [jnp.zeros((CHUNK, 128 - 24), F32)], axis=1)
            dgr_ref[b] = jnp.concatenate([dgr[h] for h in range(N_HEADS)], axis=0)
        dstate[...] = ds

    rev = lambda c: (lambda i: (nb - 1 - i, c))
    rev4 = lambda i: (nb - 1 - i, 0, 0, 0)
    return pl.pallas_call(
        body, name="gdn_chunk_bwd", grid=(nb,),
        in_specs=[pl.BlockSpec((rows_per, WIDTH), rev(g)) for g in range(3)]
        + [pl.BlockSpec((rows_per, 128), rev(0)), pl.BlockSpec((per, N_HEADS, CHUNK), lambda i: (nb - 1 - i, 0, 0)),
           pl.BlockSpec((per, N_HEADS, D_HEAD, D_HEAD), rev4), pl.BlockSpec((per, N_HEADS, CHUNK, CHUNK), rev4),
           pl.BlockSpec((rows_per, WIDTH), rev(0))],
        out_specs=[pl.BlockSpec((rows_per, 3 * WIDTH), rev(0)), pl.BlockSpec((rows_per, 128), rev(0)),
                   pl.BlockSpec((per, N_HEADS, CHUNK), lambda i: (nb - 1 - i, 0, 0))],
        out_shape=[jax.ShapeDtypeStruct((S, 3 * WIDTH), F32)] + [jax.ShapeDtypeStruct((S, 128), F32),
                                                                  jax.ShapeDtypeStruct((n, N_HEADS, CHUNK), F32)],
        scratch_shapes=[pltpu.VMEM((N_HEADS, D_HEAD, D_HEAD), F32)],
    )(qkv, qkv, qkv, smalls, gc_row, states, tinv, do)


def _flip(a, bit):
    return 1 - a if bit else a


def allgather8(name, buf):
    R, W = buf.shape

    def body(x_ref, out_ref, send_sems, recv_sems, local_sem):
        x, y, c = lax.axis_index("x"), lax.axis_index("y"), lax.axis_index("c")

        def slot(px, py, pc):
            return out_ref.at[4 * px + 2 * py + pc]

        mine = pltpu.make_async_copy(x_ref, slot(x, y, c), local_sem)
        mine.start()
        sends = []
        for r in range(1, N_DEV):
            peer = (_flip(x, r & 4), _flip(y, r & 2), _flip(c, r & 1))
            cp = pltpu.make_async_remote_copy(src_ref=x_ref, dst_ref=slot(x, y, c), send_sem=send_sems.at[r - 1],
                                              recv_sem=recv_sems.at[r - 1], device_id=peer, device_id_type=MESH_ID)
            cp.start()
            sends.append(cp)
        for r in range(1, N_DEV):
            peer = (_flip(x, r & 4), _flip(y, r & 2), _flip(c, r & 1))
            pltpu.make_async_remote_copy(src_ref=x_ref, dst_ref=slot(*peer), send_sem=send_sems.at[r - 1],
                                         recv_sem=recv_sems.at[r - 1], device_id=peer,
                                         device_id_type=MESH_ID).wait_recv()
        for cp in sends:
            cp.wait_send()
        mine.wait()

    return pl.pallas_call(
        body, name=name,
        out_shape=jax.ShapeDtypeStruct((N_DEV, R, W), buf.dtype),
        in_specs=[pl.BlockSpec(memory_space=pltpu.VMEM)],
        out_specs=pl.BlockSpec(memory_space=pltpu.VMEM),
        scratch_shapes=[pltpu.SemaphoreType.DMA((N_DEV - 1,)), pltpu.SemaphoreType.DMA((N_DEV - 1,)),
                        pltpu.SemaphoreType.DMA],
    )(buf)


def allgather_chips(name, blk):
    R, W = blk.shape
    half = R // 2
    n_far = N_CHIPS - 1

    def body(x_ref, out_ref, send_sems, recv_sems):
        x, y, c = lax.axis_index("x"), lax.axis_index("y"), lax.axis_index("c")
        sibling = (x, y, 1 - c)
        chips = [(_flip(x, r & 2), _flip(y, r & 1)) for r in range(1, N_CHIPS)]

        def rows(px, py, pc):
            return out_ref.at[2 * px + py, pc]

        def copy(k, dst, to, src):
            return pltpu.make_async_remote_copy(src_ref=src, dst_ref=dst, send_sem=send_sems.at[k],
                                                recv_sem=recv_sems.at[k], device_id=to, device_id_type=MESH_ID)

        my_half = x_ref.at[c]
        first = [copy(j, rows(x, y, c), (*chip, c), my_half) for j, chip in enumerate(chips)]
        for cp in first:
            cp.start()
        passed = [copy(n_far + j, rows(*chip, c), sibling, rows(*chip, c)) for j, chip in enumerate(chips)]
        for j, chip in enumerate(chips):
            copy(j, rows(*chip, c), (*chip, c), my_half).wait_recv()
            passed[j].start()
        for j, chip in enumerate(chips):
            copy(n_far + j, rows(*chip, 1 - c), sibling, my_half).wait_recv()
        for cp in first + passed:
            cp.wait_send()

    far = pl.pallas_call(
        body, name=name,
        out_shape=jax.ShapeDtypeStruct((N_CHIPS, 2, half, W), blk.dtype),
        in_specs=[pl.BlockSpec(memory_space=pl.ANY)],
        out_specs=pl.BlockSpec(memory_space=pl.ANY),
        scratch_shapes=[pltpu.SemaphoreType.DMA((2 * n_far,)), pltpu.SemaphoreType.DMA((2 * n_far,))],
    )(blk.reshape(2, half, W)).reshape(N_CHIPS, R, W)
    own_chip = 2 * lax.axis_index("x") + lax.axis_index("y")
    return lax.dynamic_update_index_in_dim(far, blk, own_chip, axis=0)


def sibling_send_other_half(name, g4):
    n, R, W = g4.shape
    half = R // 2

    def body(x_ref, out_ref, send_sem, recv_sem):
        x, y, c = lax.axis_index("x"), lax.axis_index("y"), lax.axis_index("c")
        cp = pltpu.make_async_remote_copy(src_ref=x_ref.at[:, pl.ds((1 - c) * half, half), :], dst_ref=out_ref,
                                          send_sem=send_sem, recv_sem=recv_sem, device_id=(x, y, 1 - c),
                                          device_id_type=MESH_ID)
        cp.start()
        cp.wait_recv()
        cp.wait_send()

    return pl.pallas_call(
        body, name=name,
        out_shape=jax.ShapeDtypeStruct((n, half, W), g4.dtype),
        in_specs=[pl.BlockSpec(memory_space=pl.ANY)],
        out_specs=pl.BlockSpec(memory_space=pl.ANY),
        scratch_shapes=[pltpu.SemaphoreType.DMA, pltpu.SemaphoreType.DMA],
    )(g4)


def sibling_merge(name, mine):
    def body(x_ref, out_ref, send_sem, recv_sem):
        x, y, c = lax.axis_index("x"), lax.axis_index("y"), lax.axis_index("c")
        cp = pltpu.make_async_remote_copy(src_ref=x_ref, dst_ref=out_ref, send_sem=send_sem, recv_sem=recv_sem,
                                          device_id=(x, y, 1 - c), device_id_type=MESH_ID)
        cp.start()
        cp.wait_recv()
        cp.wait_send()

    other = pl.pallas_call(
        body, name=name,
        out_shape=jax.ShapeDtypeStruct(mine.shape, mine.dtype),
        in_specs=[pl.BlockSpec(memory_space=pl.ANY)],
        out_specs=pl.BlockSpec(memory_space=pl.ANY),
        scratch_shapes=[pltpu.SemaphoreType.DMA, pltpu.SemaphoreType.DMA],
    )(mine)
    first = lax.axis_index("c") == 0
    return jnp.concatenate([jnp.where(first, mine, other), jnp.where(first, other, mine)], axis=0)


def sum_parts(name, own, recv, tr):
    R, W = own.shape
    tr = _tile(R, tr)

    def body(o_ref, r_ref, out_ref):
        acc = o_ref[...]
        for k in range(recv.shape[0]):
            acc = acc + r_ref[k].astype(F32)
        out_ref[...] = acc

    return pl.pallas_call(
        body, name=name, grid=(R // tr,),
        in_specs=[pl.BlockSpec((tr, W), lambda i: (i, 0)), pl.BlockSpec((recv.shape[0], tr, W), lambda i: (0, i, 0))],
        out_specs=pl.BlockSpec((tr, W), lambda i: (i, 0)),
        out_shape=jax.ShapeDtypeStruct((R, W), F32),
    )(own, recv)


def sum_own_half(name, g4, recv, c_idx, tr, rounded=True):
    n, R, W = g4.shape
    half = R // 2
    tr = _tile(half, tr)
    nb = half // tr

    def body(c_ref, g_ref, r_ref, o_ref, *ob_ref):
        s = g_ref[0] + r_ref[0]
        o_ref[0] = s
        if rounded:
            ob_ref[0][0] = s.astype(ob_ref[0].dtype)

    mine = pl.BlockSpec((1, tr, W), lambda j, i, c: (j, i, 0))
    shapes = [jax.ShapeDtypeStruct((n, half, W), F32)] + ([jax.ShapeDtypeStruct((n, half, W), MXU_DTYPE)] * rounded)
    return pl.pallas_call(
        body, name=name,
        grid_spec=pltpu.PrefetchScalarGridSpec(
            num_scalar_prefetch=1, grid=(n, nb),
            in_specs=[pl.BlockSpec((1, tr, W), lambda j, i, c: (j, c[0] * nb + i, 0)), mine],
            out_specs=[mine] * len(shapes)),
        out_shape=shapes,
    )(c_idx, g4, recv)


def sum_devices(name, g):
    def body(g_ref, o_ref):
        acc = g_ref[0]
        for d in range(1, N_DEV):
            acc = acc + g_ref[d]
        o_ref[...] = acc

    return pl.pallas_call(body, name=name, out_shape=jax.ShapeDtypeStruct(g.shape[1:], F32))(g)


def ada_fwd(c_all, w_ada):
    K, N = w_ada.shape
    tn = _tile(N, 512)

    def body(c_ref, w_ref, o_ref):
        o_ref[...] = mm_nn(_silu(c_ref[...]), w_ref[...])

    return pl.pallas_call(
        body, name="ada_fwd", grid=(N // tn,),
        in_specs=[pl.BlockSpec((N_DEV, K), lambda j: (0, 0)), pl.BlockSpec((K, tn), lambda j: (0, j))],
        out_specs=pl.BlockSpec((N_DEV, tn), lambda j: (0, j)),
        out_shape=jax.ShapeDtypeStruct((N_DEV, N), F32),
    )(c_all, w_ada)


def ada_grad(c_all, dmod):
    K = c_all.shape[1]
    N = dmod.shape[1]
    tn = _tile(N, 512)

    def body(c_ref, d_ref, o_ref):
        o_ref[...] = mm_tn(_silu(c_ref[...]), d_ref[...])

    return pl.pallas_call(
        body, name="ada_grad", grid=(N // tn,),
        in_specs=[pl.BlockSpec((N_DEV, K), lambda j: (0, 0)), pl.BlockSpec((N_DEV, tn), lambda j: (0, j))],
        out_specs=pl.BlockSpec((K, tn), lambda j: (0, j)),
        out_shape=jax.ShapeDtypeStruct((K, N), F32),
    )(c_all, dmod)


def adamw(name, w, gparts, m, v, tr, tc=None):
    R, W = w.shape
    tr = R if tc else _tile(R, tr)
    ng = len(gparts)

    def body(w_ref, *refs):
        g_refs, (m_ref, v_ref, g_out, d_out, m_out, v_out) = refs[:ng], refs[ng:]
        g = g_refs[0][...]
        for r in g_refs[1:]:
            g = g + r[...]
        g_out[...] = g
        d_out[...], m_out[...], v_out[...] = _adamw_update(w_ref[...], g, m_ref[...], v_ref[...])

    spec = pl.BlockSpec((R, tc), lambda i: (0, i)) if tc else pl.BlockSpec((tr, W), lambda i: (i, 0))
    return pl.pallas_call(
        body, name=name, grid=(W // tc if tc else R // tr,),
        in_specs=[spec] * (3 + ng), out_specs=[spec] * 4,
        out_shape=[jax.ShapeDtypeStruct((R, W), F32)] * 4,
    )(w, *gparts, m, v)


def _adamw_update(w, g, m, v):
    m1 = ADAM_B1 * m + (1.0 - ADAM_B1) * g
    v1 = ADAM_B2 * v + (1.0 - ADAM_B2) * jnp.square(g)
    m_hat = m1 / (1.0 - ADAM_B1 ** ADAM_STEP)
    v_hat = v1 / (1.0 - ADAM_B2 ** ADAM_STEP)
    return -ADAM_LR * (m_hat / (jnp.sqrt(v_hat) + ADAM_EPS) + ADAM_WD * w), m1, v1


def adamw_small(name, ws, gs, ms, vs):
    n = len(ws)

    def body(*refs):
        for k in range(n):
            w_ref, g_ref, m_ref, v_ref = (refs[j * n + k] for j in range(4))
            outs = refs[4 * n + 3 * k:4 * n + 3 * k + 3]
            for o_ref, val in zip(outs, _adamw_update(w_ref[...], g_ref[...], m_ref[...], v_ref[...])):
                o_ref[...] = val

    return pl.pallas_call(
        body, name=name, out_shape=[jax.ShapeDtypeStruct(w.shape, F32) for w in ws for _ in range(3)],
    )(*ws, *gs, *ms, *vs)


_REF_SEGMENTS = ((0, 4 * WIDTH, 0), (4 * WIDTH, 4 * WIDTH + 8, 4 * WIDTH), (4 * WIDTH + 8, 8 * WIDTH + 8, -8),
                 (8 * WIDTH + 8, IN_WIDTH, 0))


def _internal_from_blocks(blocks, n):
    a = 4 * WIDTH
    pieces = []
    for lo, hi in ((0, a), (a + 8, 2 * a + 8), (a, a + 8), (2 * a + 8, IN_WIDTH)):
        for j in range(N_CHIPS):
            s, e = max(lo, j * n), min(hi, (j + 1) * n)
            if s < e:
                pieces.append(blocks[j][:, s - j * n:e - j * n])
    pieces.append(jnp.zeros(blocks.shape[1:2] + (128 - N_SMALL,), blocks.dtype))
    return jnp.concatenate(pieces, axis=1)


def _block_from_internal(g, j, n):
    pieces = []
    for lo, hi, shift in _REF_SEGMENTS:
        s, e = max(lo, j * n), min(hi, (j + 1) * n)
        if s < e:
            pieces.append(g[:, s + shift:e + shift])
    return jnp.concatenate(pieces, axis=1)


def _pad_lanes(v, n=128):
    return jnp.pad(v, ((0, 0), (0, n - v.shape[1])))


def kernel(x, c, norm_g, w_ada, b_ada, w_in, b_fgate, fox_qn_g, fox_kn_g, gdn_conv_w, gdn_A_log, gdn_dt_bias, gdn_norm_g, w_out, final_g, loss_target, m_norm_g, m_w_ada, m_b_ada, m_w_in, m_b_fgate, m_fox_qn_g, m_fox_kn_g, m_gdn_conv_w, m_gdn_A_log, m_gdn_dt_bias, m_gdn_norm_g, m_w_out, m_final_g, v_norm_g, v_w_ada, v_b_ada, v_w_in, v_b_fgate, v_fox_qn_g, v_fox_kn_g, v_gdn_conv_w, v_gdn_A_log, v_gdn_dt_bias, v_gdn_norm_g, v_w_out, v_final_g):
    S = x.shape[1]
    H = N_HEADS
    ix, iy, ic = lax.axis_index("x"), lax.axis_index("y"), lax.axis_index("c")
    chip = 2 * ix + iy
    me = 2 * chip + ic
    x2, tgt = x[0], loss_target[0]
    ts_wide = _tile(S, 256)
    ts = _tile(S, 512)
    ts_conv = _tile(S, 1024)
    ts_head = _tile(S, 2048)
    tq = _tile(S, 1024)
    nq = S // tq
    nch = S // CHUNK
    conv_cols_chip = CONV_COLS // N_CHIPS

    pay = jnp.concatenate([c, _pad_lanes(gdn_conv_w[0], D_MODEL), jnp.zeros((3, D_MODEL), F32)], axis=0)
    g1 = allgather8("ag_c", pay)
    c_all = g1[:, 0, :]
    conv_w = jnp.concatenate([g1[2 * j, 1:1 + CONV_K, :conv_cols_chip] for j in range(N_CHIPS)], axis=1)
    g2 = allgather8("ag_mod", ada_fwd(c_all, w_ada[0]))
    mod_rows = jnp.concatenate([g2[2 * j] for j in range(N_CHIPS)], axis=1)
    mod = lax.dynamic_slice_in_dim(mod_rows, me, 1, axis=0) + b_ada
    shift, scale, gate = mod[:, :D_MODEL], mod[:, D_MODEL:2 * D_MODEL], mod[:, 2 * D_MODEL:]

    n_in = w_in.shape[2]
    win4 = allgather_chips("ag_w_in", w_in[0].astype(MXU_DTYPE))
    w_all = _internal_from_blocks(win4, n_in)
    w_out_b = w_out[0].astype(MXU_DTYPE)

    (h_b,) = rowwise_fwd("prologue", f_prologue, S, ts_wide, 1, [(x2, "row", D_MODEL, 0)], [norm_g, scale, shift],
                         [("row", D_MODEL, 0, D_MODEL, MXU_DTYPE)])
    p_all, wout_far = matmul("in_proj", h_b, w_all, F32, tm=1024, tn=1664, tk=2048, rides=[("gather", w_out_b)])
    w_out_full = lax.dynamic_update_index_in_dim(wout_far, w_out_b, chip, axis=0).reshape(2 * WIDTH, D_MODEL)
    pv = jnp.concatenate([b_fgate, gdn_dt_bias, jnp.zeros((1, 128 - 16), F32)], axis=1)
    av = jnp.concatenate([jnp.zeros((1, 8), F32), gdn_A_log, jnp.zeros((1, 128 - 16), F32)], axis=1)
    smalls = smalls_fwd(p_all, pv, av, S, ts)
    frow = jnp.transpose(smalls[:, :H]).reshape(H, nq, 1, tq)
    gc_row = jnp.transpose(smalls[:, 8:16].reshape(nch, CHUNK, H), (0, 2, 1))
    head_b = ("head", D_HEAD, 0, WIDTH, MXU_DTYPE)
    head_f = ("head", D_HEAD, 0, WIDTH, F32)
    pcol = lambda cb: (p_all, "head", D_HEAD, cb)
    qn, kn, vb = rowwise_fwd("fox_pre", f_foxpre, S, ts_head, H,[pcol(CB_FQ), pcol(CB_FK), pcol(CB_FV)],
                             [fox_qn_g, fox_kn_g], [head_b] * 3)
    o_fox, lse = flash_fwd(qn, kn, vb, frow, S, tq)
    y_conv, qkv = conv_act_fwd(p_all, conv_w, S, ts_conv)
    o_gdn, states, tinv = gdn_fwd(qkv, smalls, gc_row, S)
    mixed = lax.empty((S, 2 * WIDTH), MXU_DTYPE)
    (mixed,) = rowwise_fwd("post_fox", f_postf, S, ts_head, H,[(o_fox, "head", D_HEAD, 0), pcol(CB_FZ)], [],
                           [("head", D_HEAD, 0, 2 * WIDTH, MXU_DTYPE, mixed)])
    (mixed,) = rowwise_fwd("post_gdn", f_postg, S, ts_head, H,[(o_gdn, "head", D_HEAD, 0), pcol(CB_GZ)], [gdn_norm_g],
                           [("head", D_HEAD, H, 2 * WIDTH, MXU_DTYPE, mixed)])
    mo = matmul("out_proj", mixed, w_out_full, F32, tm=1024, tn=2048, tk=2048)

    wide = lambda a: (a, "row", D_MODEL, 0)
    dx_res, dmo, d_gate, d_final_g, loss_acc = rowwise_bwd(
        "loss", f_loss, S, ts_wide, 1, [wide(x2), wide(mo), wide(tgt)], [gate, final_g[None, :]], [],
        [(0, "row", D_MODEL, 0, D_MODEL, F32), (1, "row", D_MODEL, 0, D_MODEL, MXU_DTYPE)], primal_sum=True)

    d_mixed = matmul("d_mixed", dmo, w_out_full, F32, tm=1024, tn=2048, tk=2048, trans_b=True)
    g_w_out = matmul("g_w_out", mixed, dmo, F32, tm=1024, tn=2048, tk=2048, trans_a=True)
    dp = lax.empty((S, P_COLS), MXU_DTYPE)
    into_dp = lambda idx, cb: (idx, "head", D_HEAD, cb, P_COLS, MXU_DTYPE, dp)
    do_fox, dp, delta = rowwise_bwd(
        "post_fox_bwd", f_postf, S, ts_head, H,[(o_fox, "head", D_HEAD, 0), pcol(CB_FZ)], [],
        [(d_mixed, "head", D_HEAD, 0)], [(0,) + head_b, into_dp(1, CB_FZ)],
        extra=lambda vals, cv, grads: (jnp.sum(grads[0].astype(MXU_DTYPE).astype(F32) * vals[0], axis=-1,
                                               keepdims=True),),
        extra_outs=[("stat", tq, 0, None, F32)])
    do_gdn, dp, d_gdn_norm_g = rowwise_bwd(
        "post_gdn_bwd", f_postg, S, ts_head, H,[(o_gdn, "head", D_HEAD, 0), pcol(CB_GZ)], [gdn_norm_g],
        [(d_mixed, "head", D_HEAD, 8)], [(0,) + head_f, into_dp(1, CB_GZ)])
    d_qn, d_kn, dp, dfq_row, dfk_row = flash_bwd(qn, kn, vb, do_fox, frow, lse, delta, dp, S, tq)
    dp, d_qn_g, d_kn_g = fox_norm_bwd(p_all, d_qn, d_kn, fox_qn_g, fox_kn_g, dp, S, ts_wide)
    d_qkv, dsm_chunk, dgc_row = gdn_bwd(qkv, smalls, gc_row, states, tinv, do_gdn, S)
    dp, d_conv_w = conv_act_bwd(p_all, y_conv, d_qkv, conv_w, dp, S, ts_conv)
    d_f = jnp.transpose((dfq_row + dfk_row).reshape(H, S))
    d_gc = jnp.transpose(dgc_row, (0, 2, 1)).reshape(S, H)
    dsm = dsm_chunk + jnp.concatenate([d_f, d_gc, jnp.zeros((S, 128 - 16), F32)], axis=1)
    dp, d_pv, d_av = smalls_bwd(p_all, pv, av, dsm, dp, S, ts)
    g_w_all = matmul("g_w_in", h_b, dp, F32, tm=1024, tn=1664, tk=2048, trans_a=True)

    c_idx = jnp.reshape(ic, (1,)).astype(jnp.int32)

    def chip_sums(tag, g4):
        recv = sibling_send_other_half("d2d_" + tag, g4)
        return sum_own_half("sum2_" + tag, g4, recv, c_idx, 128)

    def finish(tag, p4, far):
        own = lax.dynamic_index_in_dim(p4, chip, axis=0, keepdims=False)
        return sibling_merge("merge_" + tag, sum_parts("sum_" + tag, own, far, 128))

    g_int = g_w_all[None]
    (p_int,) = sum_own_half("sum2_w_in", g_int, sibling_send_other_half("d2d_w_in", g_int), c_idx, 128, rounded=False)
    p4_in = jnp.stack([_block_from_internal(p_int[0], j, n_in) for j in range(N_CHIPS)])
    p4_in_sent = p4_in.astype(MXU_DTYPE)
    p4_out, p4_out_sent = chip_sums("w_out", g_w_out.reshape(N_CHIPS, w_out.shape[1], D_MODEL))
    d_h, far_in, far_out = matmul("d_h", dp, w_all, F32, tm=1024, tn=2048, tk=1664, trans_b=True,
                                  rides=[("scatter", p4_in_sent), ("scatter", p4_out_sent)])
    g_w_in_full = finish("w_in", p4_in, far_in)
    g_w_out_full = finish("w_out", p4_out, far_out)
    d_x, d_norm_g, d_scale, d_shift = rowwise_bwd(
        "prologue_bwd", f_prologue_res, S, ts_wide, 1, [wide(x2)], [norm_g, scale, shift],
        [wide(d_h), wide(dx_res)], [(0, "row", D_MODEL, 0, D_MODEL, F32)])

    parts = [d_shift, d_scale, d_gate, d_norm_g, d_final_g, d_conv_w.reshape(1, CONV_K * CONV_COLS), d_qn_g, d_kn_g,
             d_gdn_norm_g, d_pv, d_av, loss_acc]
    n_pay = sum(p.shape[1] for p in parts)
    pay_w = -(-n_pay // 1024) * 1024
    pay2 = jnp.concatenate(parts + [jnp.zeros((1, pay_w - n_pay), F32)], axis=1).reshape(8, pay_w // 8)
    g3 = allgather8("ag_small", pay2)
    tot = sum_devices("sum_small", g3).reshape(1, pay_w)
    dmod_all = g3.reshape(N_DEV, pay_w)[:, :3 * D_MODEL]
    o = 3 * D_MODEL
    g_b_ada = tot[:, :o]
    g_norm_g, g_final_g = tot[:, o:o + D_MODEL], tot[:, o + D_MODEL:o + 2 * D_MODEL]
    o += 2 * D_MODEL
    g_conv_full = tot[:, o:o + CONV_K * CONV_COLS].reshape(CONV_K, CONV_COLS)
    g_conv = lax.dynamic_slice_in_dim(g_conv_full, chip * conv_cols_chip, conv_cols_chip, axis=1)
    o += CONV_K * CONV_COLS
    g_qn_g, g_kn_g, g_gdn_ng = tot[:, o:o + 128], tot[:, o + 128:o + 256], tot[:, o + 256:o + 384]
    g_pv, g_av = tot[:, o + 384:o + 512], tot[:, o + 512:o + 640]
    loss = tot[0, o + 640]
    g_b_fgate, g_dt_bias, g_a_log = g_pv[:, :8], g_pv[:, 8:16], g_av[:, 8:16]

    small_g = [g_norm_g, g_b_ada, g_final_g, g_conv, g_qn_g, g_kn_g, g_gdn_ng, g_b_fgate, g_a_log, g_dt_bias]
    small_upd = adamw_small(
        "adamw_small",
        [norm_g, b_ada, final_g[None, :], gdn_conv_w[0], fox_qn_g, fox_kn_g, gdn_norm_g, b_fgate, gdn_A_log, gdn_dt_bias],
        small_g,
        [m_norm_g, m_b_ada, m_final_g[None, :], m_gdn_conv_w[0], m_fox_qn_g, m_fox_kn_g, m_gdn_norm_g, m_b_fgate,
         m_gdn_A_log, m_gdn_dt_bias],
        [v_norm_g, v_b_ada, v_final_g[None, :], v_gdn_conv_w[0], v_fox_qn_g, v_fox_kn_g, v_gdn_norm_g, v_b_fgate,
         v_gdn_A_log, v_gdn_dt_bias])

    def small_leaves(k):
        vals = small_g if k == 0 else [small_upd[3 * p + k - 1] for p in range(len(small_g))]
        return [v[0] if p == 2 else (v[None] if p == 3 else v) for p, v in enumerate(vals)]

    n_ada = w_ada.shape[2]
    g_w_ada = ada_grad(c_all, lax.dynamic_slice_in_dim(dmod_all, chip * n_ada, n_ada, axis=1))
    ada_out = adamw("adamw_w_ada", w_ada[0], [g_w_ada], m_w_ada[0], v_w_ada[0], 128)

    in_out = [jnp.transpose(o) for o in adamw("adamw_w_in", jnp.transpose(w_in[0]), [jnp.transpose(g_w_in_full)],
                                              jnp.transpose(m_w_in[0]), jnp.transpose(v_w_in[0]), None, tc=256)]
    out_out = adamw("adamw_w_out", w_out[0], [g_w_out_full], m_w_out[0], v_w_out[0], 128)

    res = [loss, d_x[None]]
    for k in range(4):
        s = small_leaves(k)
        big = [ada_out[k][None], in_out[k][None], out_out[k][None]]
        res += [s[0], big[0], s[1], big[1], s[7], s[4], s[5], s[3], s[8], s[9], s[6], big[2], s[2]]
    return tuple(res)
```

```python
import functools

import jax
import jax.numpy as jnp
from jax import lax
from jax.experimental import pallas as pl
from jax.experimental.pallas import tpu as pltpu

F32 = jnp.float32
MXU_DTYPE = jnp.bfloat16
HIGHEST = lax.Precision.HIGHEST
SOLVE_PRECISION = lax.Precision.HIGH
MESH_ID = pl.DeviceIdType.MESH

D_MODEL = 2048
N_HEADS = 8
D_HEAD = 128
WIDTH = N_HEADS * D_HEAD
CHUNK = 64
CONV_K = 4
EPS = 1e-6
NEG = -1e30
ATT_SCALE = D_HEAD ** -0.5
N_SMALL = 3 * N_HEADS
P_COLS = 8 * WIDTH + 128
CB_FQ, CB_FK, CB_FV, CB_FZ, CB_GQ, CB_GK, CB_GV, CB_GZ, CB_SMALL = 0, 8, 16, 24, 32, 40, 48, 56, 64
IN_WIDTH = 8 * WIDTH + N_SMALL
N_CHIPS = 4
N_DEV = 8

ADAM_LR, ADAM_B1, ADAM_B2, ADAM_EPS, ADAM_WD, ADAM_STEP = 0.001, 0.9, 0.999, 1e-08, 0.01, 10


def _dotg(a, b, dims):
    return lax.dot_general(a.astype(MXU_DTYPE), b.astype(MXU_DTYPE), (dims, ((), ())), preferred_element_type=F32)


@jax.custom_vjp
def mm_nn(a, b):
    return _dotg(a, b, ((1,), (0,)))


@jax.custom_vjp
def mm_nt(a, b):
    return _dotg(a, b, ((1,), (1,)))


@jax.custom_vjp
def mm_tn(a, b):
    return _dotg(a, b, ((0,), (0,)))


mm_nn.defvjp(lambda a, b: (mm_nn(a, b), (a, b)), lambda r, g: (mm_nt(g, r[1]), mm_tn(r[0], g)))
mm_nt.defvjp(lambda a, b: (mm_nt(a, b), (a, b)), lambda r, g: (mm_nn(g, r[1]), mm_tn(g, r[0])))
mm_tn.defvjp(lambda a, b: (mm_tn(a, b), (a, b)), lambda r, g: (mm_nt(r[1], g), mm_nn(r[0], g)))


def hdot(a, b):
    return lax.dot_general(a, b, (((1,), (0,)), ((), ())), precision=HIGHEST, preferred_element_type=F32)


def _sigmoid(x):
    return 0.5 * (jnp.tanh(0.5 * x) + 1.0)


def _silu(x):
    return x * _sigmoid(x)


def _softplus(x):
    return jnp.maximum(x, 0.0) + jnp.log(1.0 + jnp.exp(-jnp.abs(x)))


def _log_sigmoid(x):
    return jnp.minimum(x, 0.0) - jnp.log(1.0 + jnp.exp(-jnp.abs(x)))


def _rms(x, g):
    return x * lax.rsqrt(jnp.mean(x * x, axis=-1, keepdims=True) + EPS) * g


def _l2n(x):
    return x * lax.rsqrt(jnp.sum(x * x, axis=-1, keepdims=True) + EPS)


def _tile(n, pref):
    t = min(n, pref)
    assert n % t == 0, (n, pref)
    return t


def _bspec(kind, w, col0, ts):
    if kind == "row":
        return pl.BlockSpec((ts, w), lambda i, h: (i, col0))
    if kind == "head":
        return pl.BlockSpec((ts, w), lambda i, h: (i, col0 + h))
    assert kind == "stat"
    return pl.BlockSpec((1, ts // w, 1, w), lambda i, h: (h, i, 0, 0))


def _ld(ref, kind):
    return ref[...]


def _st(ref, kind, val):
    if kind == "stat":
        w = ref.shape[3]
        for n in range(ref.shape[1]):
            ref[0, n] = _col_to_row(val[n * w:(n + 1) * w, :])
    else:
        ref[...] = val.astype(ref.dtype)


def _full_spec(a):
    nd = a.ndim
    return pl.BlockSpec(a.shape, lambda i, h: (0,) * nd)


def _out_struct(kind, w, S, width_total, dtype, nh):
    if kind == "stat":
        return jax.ShapeDtypeStruct((nh, S // w, 1, w), dtype)
    return jax.ShapeDtypeStruct((S, width_total), dtype)


def rowwise_fwd(name, f, S, ts, nh, ins, params, outs):
    n_in, n_p = len(ins), len(params)
    into = {j: o[5] for j, o in enumerate(outs) if len(o) > 5}
    outs = [o[:5] for o in outs]
    n_al = len(into)

    def body(*refs):
        vals = [_ld(r, s[1]).astype(F32) for r, s in zip(refs[:n_in], ins)]
        pv = [r[...] for r in refs[n_in:n_in + n_p]]
        res = f(*vals, *pv)
        for r, s, o in zip(refs[n_in + n_p + n_al:], outs, res):
            _st(r, s[0], o)

    return pl.pallas_call(
        body, name=name, grid=(S // ts, nh),
        in_specs=[_bspec(k, w, c0, ts) for (_, k, w, c0) in ins] + [_full_spec(p) for p in params]
        + [pl.BlockSpec(memory_space=pl.ANY)] * n_al,
        out_specs=[_bspec(k, w, c0, ts) for (k, w, c0, _, _) in outs],
        out_shape=[jax.ShapeDtypeStruct(into[j].shape, into[j].dtype) if j in into else _out_struct(k, w, S, tw, dt, nh)
                   for j, (k, w, c0, tw, dt) in enumerate(outs)],
        input_output_aliases={n_in + n_p + n: j for n, j in enumerate(sorted(into))},
    )(*[a for (a, _, _, _) in ins], *params, *[into[j] for j in sorted(into)])


def rowwise_bwd(name, f, S, ts, nh, ins, params, cts, row_grads, extra=None, extra_outs=(), primal_sum=False):
    n_in, n_p, n_ct = len(ins), len(params), len(cts)
    n_rg, n_ex = len(row_grads), len(extra_outs)
    into = {j: rg[6] for j, rg in enumerate(row_grads) if len(rg) > 6}
    row_grads = [rg[:6] for rg in row_grads]
    n_al = len(into)

    def body(*refs):
        first = (pl.program_id(0) == 0) & (pl.program_id(1) == 0)
        in_refs = refs[:n_in]
        p_refs = refs[n_in:n_in + n_p]
        ct_refs = refs[n_in + n_p:n_in + n_p + n_ct]
        o = n_in + n_p + n_ct + n_al
        rg_refs = refs[o:o + n_rg]
        ex_refs = refs[o + n_rg:o + n_rg + n_ex]
        acc_refs = refs[o + n_rg + n_ex:]
        vals = [_ld(r, s[1]).astype(F32) for r, s in zip(in_refs, ins)]
        pv = [r[...] for r in p_refs]
        res, vjp = jax.vjp(f, *vals, *pv)
        if primal_sum:
            cv = [jnp.ones_like(res[0])] + [_ld(r, s[1]).astype(F32) for r, s in zip(ct_refs, cts)]
        else:
            cv = [_ld(r, s[1]).astype(F32) for r, s in zip(ct_refs, cts)]
        grads = vjp(tuple(cv))
        for r, s in zip(rg_refs, row_grads):
            _st(r, s[1], grads[s[0]])
        if extra is not None:
            for r, s, e in zip(ex_refs, extra_outs, extra(vals, cv, grads)):
                _st(r, s[0], e)

        @pl.when(first)
        def _():
            for r in acc_refs:
                r[...] = jnp.zeros_like(r)

        for r, g in zip(acc_refs[:n_p], grads[n_in:]):
            r[...] += g
        if primal_sum:
            acc_refs[n_p][...] += jnp.sum(res[0])

    acc_shapes = [jax.ShapeDtypeStruct(p.shape, F32) for p in params]
    acc_specs = [_full_spec(p) for p in params]
    if primal_sum:
        acc_shapes.append(jax.ShapeDtypeStruct((1, 128), F32))
        acc_specs.append(pl.BlockSpec((1, 128), lambda i, h: (0, 0)))
    rg_shapes = [jax.ShapeDtypeStruct(into[j].shape, into[j].dtype) if j in into else _out_struct(k, w, S, tw, dt, nh)
                 for j, (_, k, w, c0, tw, dt) in enumerate(row_grads)]
    first_alias = n_in + n_p + n_ct
    return pl.pallas_call(
        body, name=name, grid=(S // ts, nh),
        in_specs=[_bspec(k, w, c0, ts) for (_, k, w, c0) in ins] + [_full_spec(p) for p in params]
        + [_bspec(k, w, c0, ts) for (_, k, w, c0) in cts] + [pl.BlockSpec(memory_space=pl.ANY)] * n_al,
        out_specs=[_bspec(k, w, c0, ts) for (_, k, w, c0, _, _) in row_grads]
        + [_bspec(k, w, c0, ts) for (k, w, c0, _, _) in extra_outs] + acc_specs,
        out_shape=rg_shapes + [_out_struct(k, w, S, tw, dt, nh) for (k, w, c0, tw, dt) in extra_outs] + acc_shapes,
        input_output_aliases={first_alias + n: j for n, j in enumerate(sorted(into))},
    )(*[a for (a, _, _, _) in ins], *params, *[a for (a, _, _, _) in cts], *[into[j] for j in sorted(into)])


def f_prologue(x, ng, sc, sh):
    return (_rms(x, ng) * (1.0 + sc) + sh,)


def f_prologue_res(x, ng, sc, sh):
    return (_rms(x, ng) * (1.0 + sc) + sh, x)


def f_loss(x, mo, tgt, gate, fg):
    y = _rms(x + gate * mo, fg)
    return (0.5 * jnp.mean(jnp.square(y - tgt), axis=-1, keepdims=True),)


def f_foxpre(fq, fk, fv, qg, kg):
    return (_rms(fq, qg) * ATT_SCALE, _rms(fk, kg), fv)


def f_postf(o, z):
    return (o * _silu(z),)


def f_postg(o, z, ng):
    return (_rms(o, ng) * _silu(z),)


def fox_norm_bwd(p_all, d_qn, d_kn, qg, kg, dp, S, ts):
    def body(p_ref, dq_ref, dk_ref, qg_ref, kg_ref, dp_in, o_ref, dqg_ref, dkg_ref):
        @pl.when(pl.program_id(0) == 0)
        def _():
            dqg_ref[...] = jnp.zeros_like(dqg_ref)
            dkg_ref[...] = jnp.zeros_like(dkg_ref)

        for ct_ref, g_ref, dg_ref, col0 in ((dq_ref, qg_ref, dqg_ref, 0), (dk_ref, kg_ref, dkg_ref, WIDTH)):
            dg = jnp.zeros((1, D_HEAD), F32)
            for h in range(N_HEADS):
                _, vjp = jax.vjp(_rms, p_ref[:, col0 + h * D_HEAD:col0 + (h + 1) * D_HEAD], g_ref[...])
                dx, dg_h = vjp(ct_ref[:, h * D_HEAD:(h + 1) * D_HEAD])
                o_ref[:, col0 + h * D_HEAD:col0 + (h + 1) * D_HEAD] = dx.astype(o_ref.dtype)
                dg = dg + dg_h
            dg_ref[...] += dg

    gain = pl.BlockSpec((1, D_HEAD), lambda i: (0, 0))
    both = pl.BlockSpec((ts, 2 * WIDTH), lambda i: (i, 0))
    one = pl.BlockSpec((ts, WIDTH), lambda i: (i, 0))
    return pl.pallas_call(
        body, name="fox_norm_bwd", grid=(S // ts,),
        in_specs=[both, one, one, gain, gain, pl.BlockSpec(memory_space=pl.ANY)],
        out_specs=[both, gain, gain],
        out_shape=[jax.ShapeDtypeStruct(dp.shape, dp.dtype), jax.ShapeDtypeStruct((1, D_HEAD), F32),
                   jax.ShapeDtypeStruct((1, D_HEAD), F32)],
        input_output_aliases={5: 0},
    )(p_all, d_qn, d_kn, qg, kg, dp)


def _chip_exchange(kind, x_ref, out_ref, send_sems, recv_sems):
    x, y, c = lax.axis_index("x"), lax.axis_index("y"), lax.axis_index("c")
    sends, arrivals = [], []
    for r in range(1, N_CHIPS):
        px, py = _flip(x, r & 2), _flip(y, r & 1)
        if kind == "scatter":
            src, dst, landed = x_ref.at[2 * px + py], out_ref.at[r - 1], out_ref.at[r - 1]
        else:
            src, dst, landed = x_ref, out_ref.at[2 * x + y], out_ref.at[2 * px + py]
        mk = functools.partial(pltpu.make_async_remote_copy, send_sem=send_sems.at[r - 1], recv_sem=recv_sems.at[r - 1],
                               device_id=(px, py, c), device_id_type=MESH_ID)
        sends.append(mk(src_ref=src, dst_ref=dst))
        arrivals.append(mk(src_ref=src, dst_ref=landed))
    return sends, arrivals


def _exchange_shape(kind, x):
    return jax.ShapeDtypeStruct(((N_CHIPS - 1,) + x.shape[1:]) if kind == "scatter" else ((N_CHIPS,) + x.shape), x.dtype)


def matmul(name, a, b, out_dtype, tm=512, tn=512, tk=2048, trans_a=False, trans_b=False, rides=()):
    M, K = a.shape[::-1] if trans_a else a.shape
    N, K2 = b.shape if trans_b else b.shape[::-1]
    assert K == K2
    tm, tn, tk = _tile(M, tm), _tile(N, tn), _tile(K, tk)
    ni, nj, nk = M // tm, N // tn, K // tk
    a_spec = pl.BlockSpec((tk, tm), lambda i, j, k: (k, i)) if trans_a else pl.BlockSpec((tm, tk), lambda i, j, k: (i, k))
    b_spec = pl.BlockSpec((tn, tk), lambda i, j, k: (j, k)) if trans_b else pl.BlockSpec((tk, tn), lambda i, j, k: (k, j))
    n_ride = len(rides)

    def body(a_ref, b_ref, *rest):
        ride_in, o_ref, ride_out = rest[:n_ride], rest[n_ride], rest[n_ride + 1:2 * n_ride + 1]
        scr = rest[2 * n_ride + 1:]
        i, j, k = pl.program_id(0), pl.program_id(1), pl.program_id(2)
        exchanges = [_chip_exchange(kind, ride_in[n], ride_out[n], scr[2 * n], scr[2 * n + 1])
                     for n, (kind, _) in enumerate(rides)]

        @pl.when((i == 0) & (j == 0) & (k == 0))
        def _():
            for sends, _ in exchanges:
                for cp in sends:
                    cp.start()

        part = lax.dot_general(a_ref[...], b_ref[...], (((0 if trans_a else 1,), (1 if trans_b else 0,)), ((), ())),
                               preferred_element_type=F32)
        if nk == 1:
            o_ref[...] = part.astype(o_ref.dtype)
        else:
            acc = scr[2 * n_ride]

            @pl.when(k == 0)
            def _():
                acc[...] = part

            @pl.when(k > 0)
            def _():
                acc[...] += part

            @pl.when(k == nk - 1)
            def _():
                o_ref[...] = acc[...].astype(o_ref.dtype)

        @pl.when((i == ni - 1) & (j == nj - 1) & (k == nk - 1))
        def _():
            for sends, arrivals in exchanges:
                for cp in arrivals:
                    cp.wait_recv()
                for cp in sends:
                    cp.wait_send()

    sems = [pltpu.SemaphoreType.DMA((N_CHIPS - 1,))] * (2 * n_ride)
    any_spec = pl.BlockSpec(memory_space=pl.ANY)
    res = pl.pallas_call(
        body, name=name, grid=(ni, nj, nk),
        in_specs=[a_spec, b_spec] + [any_spec] * n_ride,
        out_specs=[pl.BlockSpec((tm, tn), lambda i, j, k: (i, j))] + [any_spec] * n_ride,
        out_shape=[jax.ShapeDtypeStruct((M, N), out_dtype)] + [_exchange_shape(kind, x) for kind, x in rides],
        scratch_shapes=sems + ([] if nk == 1 else [pltpu.VMEM((tm, tn), F32)]),
        compiler_params=pltpu.CompilerParams(
            dimension_semantics=("arbitrary",) * 3 if rides else ("parallel", "parallel", "arbitrary")),
    )(a, b, *[x for _, x in rides])
    return tuple(res) if rides else res[0]


def _small_f(z, pv, av):
    lane = lax.broadcasted_iota(jnp.int32, z.shape, 1)
    zz = z + pv
    logf = _log_sigmoid(zz)
    gval = -jnp.exp(av) * _softplus(zz)
    beta = _sigmoid(zz)
    return jnp.where(lane < 8, logf, jnp.where(lane < 16, gval, jnp.where(lane < 24, beta, 0.0)))


def _tri(ts, upper):
    r = lax.broadcasted_iota(jnp.int32, (ts, ts), 0)
    c = lax.broadcasted_iota(jnp.int32, (ts, ts), 1)
    keep = (r <= c) if upper else (r >= c)
    same_chunk = (r // CHUNK) == (c // CHUNK)
    return keep.astype(F32), (keep & same_chunk).astype(F32)


def smalls_fwd(p_all, pv, av, S, ts):
    nt = S // ts

    def body(z_ref, pv_ref, av_ref, o_ref, carry):
        i = pl.program_id(0)

        @pl.when(i == 0)
        def _():
            carry[...] = jnp.zeros_like(carry)

        e = _small_f(z_ref[...], pv_ref[...], av_ref[...])
        lane = lax.broadcasted_iota(jnp.int32, e.shape, 1)
        l_all, l_chunk = _tri(ts, upper=False)
        cum_all = hdot(l_all, e) + carry[...]
        cum_chunk = hdot(l_chunk, e)
        carry[...] = cum_all[ts - 1:ts, :]
        o_ref[...] = jnp.where(lane < 8, cum_all, jnp.where(lane < 16, cum_chunk, e))

    return pl.pallas_call(
        body, name="smalls_fwd", grid=(nt,),
        in_specs=[pl.BlockSpec((ts, 128), lambda i: (i, CB_SMALL)), pl.BlockSpec((1, 128), lambda i: (0, 0)),
                  pl.BlockSpec((1, 128), lambda i: (0, 0))],
        out_specs=pl.BlockSpec((ts, 128), lambda i: (i, 0)),
        out_shape=jax.ShapeDtypeStruct((S, 128), F32),
        scratch_shapes=[pltpu.VMEM((1, 128), F32)],
    )(p_all, pv, av)


def smalls_bwd(p_all, pv, av, dsm, dp, S, ts):
    nt = S // ts

    def body(z_ref, pv_ref, av_ref, d_ref, dp_in, dz_ref, dpv_ref, dav_ref, carry):
        i = pl.program_id(0)

        @pl.when(i == 0)
        def _():
            carry[...] = jnp.zeros_like(carry)
            dpv_ref[...] = jnp.zeros_like(dpv_ref)
            dav_ref[...] = jnp.zeros_like(dav_ref)

        d = d_ref[...]
        lane = lax.broadcasted_iota(jnp.int32, d.shape, 1)
        u_all, u_chunk = _tri(ts, upper=True)
        rev_all = hdot(u_all, d) + carry[...]
        rev_chunk = hdot(u_chunk, d)
        carry[...] = rev_all[0:1, :]
        de = jnp.where(lane < 8, rev_all, jnp.where(lane < 16, rev_chunk, d))
        _, vjp = jax.vjp(_small_f, z_ref[...], pv_ref[...], av_ref[...])
        dz, dpv, dav = vjp(de)
        dz_ref[...] = dz.astype(dz_ref.dtype)
        dpv_ref[...] += dpv
        dav_ref[...] += dav

    rev = lambda i: (nt - 1 - i, 0)
    small_cols = pl.BlockSpec((ts, 128), lambda i: (nt - 1 - i, CB_SMALL))
    return pl.pallas_call(
        body, name="smalls_bwd", grid=(nt,),
        in_specs=[small_cols, pl.BlockSpec((1, 128), lambda i: (0, 0)), pl.BlockSpec((1, 128), lambda i: (0, 0)),
                  pl.BlockSpec((ts, 128), rev), pl.BlockSpec(memory_space=pl.ANY)],
        out_specs=[small_cols, pl.BlockSpec((1, 128), lambda i: (0, 0)), pl.BlockSpec((1, 128), lambda i: (0, 0))],
        out_shape=[jax.ShapeDtypeStruct(dp.shape, dp.dtype), jax.ShapeDtypeStruct((1, 128), F32),
                   jax.ShapeDtypeStruct((1, 128), F32)],
        scratch_shapes=[pltpu.VMEM((1, 128), F32)],
        input_output_aliases={4: 0},
    )(p_all, pv, av, dsm, dp)


def _col_to_row(col):
    return jnp.transpose(jnp.broadcast_to(col, (col.shape[0], 128)))[0:1, :]


def _row_to_col(row):
    return jnp.transpose(jnp.broadcast_to(row, (128, row.shape[1])))[:, 0:1]


def _causal_mask(s, keys_first):
    r = lax.broadcasted_iota(jnp.int32, s.shape, 0)
    c = lax.broadcasted_iota(jnp.int32, s.shape, 1)
    return jnp.where((r <= c) if keys_first else (c <= r), s, NEG)


def flash_fwd(qs, kn, vb, frow, S, tq):
    nq = S // tq
    tk = tq
    hp = 2
    wide = hp * D_HEAD
    head = lambda ref, h, rows=slice(None): ref[rows, h * D_HEAD:(h + 1) * D_HEAD]

    def body(q_ref, k_ref, v_ref, fr_ref, o_ref, lse_ref):
        i = pl.program_id(1)
        qs_ = [head(q_ref, h) for h in range(hp)]
        f_base = [fr_ref[h, i][:, 0:1] for h in range(hp)]

        def tile(j, carry, diagonal):
            rows = pl.ds(pl.multiple_of(j * tk, tk), tk)
            out = []
            for h in range(hp):
                m, l, acc = carry[3 * h:3 * h + 3]
                s = mm_nt(qs_[h], head(k_ref, h, rows)) - (fr_ref[h, j] - f_base[h])
                if diagonal:
                    s = _causal_mask(s, keys_first=False)
                m_new = jnp.maximum(m, jnp.max(s, axis=-1, keepdims=True))
                a = jnp.exp(m - m_new)
                p = jnp.exp(s - m_new)
                out += [m_new, a * l + jnp.sum(p, axis=-1, keepdims=True), a * acc + mm_nn(p, head(v_ref, h, rows))]
            return tuple(out)

        init = (jnp.full((tq, 1), NEG, F32), jnp.zeros((tq, 1), F32), jnp.zeros((tq, D_HEAD), F32)) * hp
        res = tile(i, lax.fori_loop(0, i, lambda j, c: tile(j, c, False), init), True)
        for h in range(hp):
            m, l, acc = res[3 * h:3 * h + 3]
            o_ref[:, h * D_HEAD:(h + 1) * D_HEAD] = acc / l
            lse_ref[h, 0] = _col_to_row(m + jnp.log(l))

    return pl.pallas_call(
        body, name="flash_fwd", grid=(N_HEADS // hp, nq),
        in_specs=[pl.BlockSpec((tq, wide), lambda h, i: (i, h)), pl.BlockSpec((S, wide), lambda h, i: (0, h)),
                  pl.BlockSpec((S, wide), lambda h, i: (0, h)),
                  pl.BlockSpec((hp, nq, 1, tk), lambda h, i: (h, 0, 0, 0))],
        out_specs=[pl.BlockSpec((tq, wide), lambda h, i: (i, h)),
                   pl.BlockSpec((hp, 1, 1, tq), lambda h, i: (h, i, 0, 0))],
        out_shape=[jax.ShapeDtypeStruct((S, WIDTH), F32), jax.ShapeDtypeStruct((N_HEADS, nq, 1, tq), F32)],
    )(qs, kn, vb, frow)


def flash_bwd(qs, kn, vb, do, frow, lse_row, delta_row, dp, S, tq):
    nq = S // tq
    tk = tq
    hp = 2
    wide = hp * D_HEAD
    head = lambda ref, h, rows=slice(None): ref[rows, h * D_HEAD:(h + 1) * D_HEAD]

    def body(q_ref, k_ref, v_ref, do_ref, fr_ref, lse_ref, dl_ref, dp_in, dq_ref, dk_ref, dv_ref, dfq_ref, dfk_ref):
        j = pl.program_id(1)
        ks = [head(k_ref, h) for h in range(hp)]
        vs = [head(v_ref, h) for h in range(hp)]
        fks = [_row_to_col(fr_ref[h, j]) for h in range(hp)]

        @pl.when(j == 0)
        def _():
            dq_ref[...] = jnp.zeros_like(dq_ref)
            dfq_ref[...] = jnp.zeros_like(dfq_ref)

        def tile(i, carry):
            rows = pl.ds(pl.multiple_of(i * tq, tq), tq)
            out = []
            for h in range(hp):
                dk, dv, dfk = carry[3 * h:3 * h + 3]
                q = head(q_ref, h, rows)
                do_ = head(do_ref, h, rows)
                st = mm_nt(ks[h], q) - (fks[h] - fr_ref[h, i][:, 0:1])
                pt = jnp.exp(st - lse_ref[h, i])
                dst = pt * (mm_nt(vs[h], do_) - dl_ref[h, i])
                dq_ref[rows, h * D_HEAD:(h + 1) * D_HEAD] += mm_tn(dst, ks[h]) * ATT_SCALE
                dfq_ref[h, i] += jnp.sum(dst, axis=0, keepdims=True)
                out += [dk + mm_nn(dst, q), dv + mm_nn(pt, do_), dfk - jnp.sum(dst, axis=-1, keepdims=True)]
            return tuple(out)

        def diagonal_tile():
            half = tk // 2
            out = []
            for h in range(hp):
                cols = slice(h * D_HEAD, (h + 1) * D_HEAD)
                f0 = fr_ref[h, j][:, 0:1]
                parts = []
                for a in range(2):
                    keys = slice(a * half, (a + 1) * half)
                    k_a, v_a, fk_a = ks[h][keys], vs[h][keys], fks[h][keys]
                    dk = jnp.zeros((half, D_HEAD), F32)
                    dv = jnp.zeros((half, D_HEAD), F32)
                    dfk = jnp.zeros((half, 1), F32)
                    for b in range(a, 2):
                        rows = pl.ds(pl.multiple_of(j * tq + b * half, half), half)
                        lanes = slice(b * half, (b + 1) * half)
                        q, do_ = q_ref[rows, cols], do_ref[rows, cols]
                        st = mm_nt(k_a, q) - (fk_a - f0)
                        if a == b:
                            st = _causal_mask(st, keys_first=True)
                        pt = jnp.exp(st - lse_ref[h, j][:, lanes])
                        dst = pt * (mm_nt(v_a, do_) - dl_ref[h, j][:, lanes])
                        dq_ref[rows, cols] += mm_tn(dst, k_a) * ATT_SCALE
                        dfq_ref[h, j, :, lanes] += jnp.sum(dst, axis=0, keepdims=True)
                        dk, dv = dk + mm_nn(dst, q), dv + mm_nn(pt, do_)
                        dfk = dfk - jnp.sum(dst, axis=-1, keepdims=True)
                    parts.append((dk, dv, dfk))
                out += [jnp.concatenate([p[n] for p in parts], axis=0) for n in range(3)]
            return tuple(out)

        res = lax.fori_loop(j + 1, nq, tile, diagonal_tile())
        for h in range(hp):
            dk, dv, dfk = res[3 * h:3 * h + 3]
            dk_ref[:, h * D_HEAD:(h + 1) * D_HEAD] = dk
            dv_ref[:, h * D_HEAD:(h + 1) * D_HEAD] = dv.astype(dv_ref.dtype)
            dfk_ref[h, 0] = _col_to_row(dfk)

    blk = pl.BlockSpec((tk, wide), lambda h, j: (j, h))
    full = pl.BlockSpec((S, wide), lambda h, j: (0, h), pipeline_mode=pl.Buffered(1))
    rows = pl.BlockSpec((hp, nq, 1, tq), lambda h, j: (h, 0, 0, 0))
    stat = jax.ShapeDtypeStruct((N_HEADS, nq, 1, tq), F32)
    return pl.pallas_call(
        body, name="flash_bwd", grid=(N_HEADS // hp, nq),
        in_specs=[full, blk, blk, full, rows, rows, rows, pl.BlockSpec(memory_space=pl.ANY)],
        out_specs=[full, blk, pl.BlockSpec((tk, wide), lambda h, j: (j, CB_FV // hp + h)), rows,
                   pl.BlockSpec((hp, 1, 1, tk), lambda h, j: (h, j, 0, 0))],
        out_shape=[jax.ShapeDtypeStruct((S, WIDTH), F32), jax.ShapeDtypeStruct((S, WIDTH), F32),
                   jax.ShapeDtypeStruct(dp.shape, dp.dtype), stat, stat],
        input_output_aliases={7: 2},
    )(qs, kn, vb, do, frow, lse_row, delta_row, dp)


CONV_COLS = 3 * WIDTH
CONV_TC = 512


def _gdn_act(y, group):
    s = _silu(y)
    if group == 2:
        return s
    return _l2n(s) * ATT_SCALE if group == 0 else _l2n(s)


def _gdn_act_bwd(y, d, group):
    _, vjp = jax.vjp(functools.partial(_gdn_act, group=group), y)
    return vjp(d)[0]


def conv_act_fwd(p_all, conv_w, S, ts):
    nt, tc = S // ts, CONV_TC
    cb0 = CB_GQ * 128 // tc
    r8 = ts // 8
    tiles_per_group = WIDTH // tc

    def body(x_ref, halo_ref, w_ref, y_ref, a_ref, scr):
        c, i = pl.program_id(0), pl.program_id(1)
        scr[0:8, :] = jnp.where(i > 0, halo_ref[...], 0.0)
        scr[8:8 + ts, :] = x_ref[...]
        w = w_ref[...]
        y = w[3:4, :] * x_ref[...]
        for j in range(CONV_K - 1):
            y = y + w[j:j + 1, :] * scr[5 + j:5 + j + ts, :]
        y_ref[...] = y
        for group in range(3):
            @pl.when(c // tiles_per_group == group)
            def _(group=group):
                for h in range(tc // D_HEAD):
                    sl = slice(h * D_HEAD, (h + 1) * D_HEAD)
                    a_ref[:, sl] = _gdn_act(y_ref[:, sl], group)

    out = pl.BlockSpec((ts, tc), lambda c, i: (i, c))
    return pl.pallas_call(
        body, name="conv_act_fwd", grid=(CONV_COLS // tc, nt),
        in_specs=[pl.BlockSpec((ts, tc), lambda c, i: (i, cb0 + c)),
                  pl.BlockSpec((8, tc), lambda c, i: (jnp.maximum(i * r8 - 1, 0), cb0 + c)),
                  pl.BlockSpec((CONV_K, tc), lambda c, i: (0, c))],
        out_specs=[out, out],
        out_shape=[jax.ShapeDtypeStruct((S, CONV_COLS), F32)] * 2,
        scratch_shapes=[pltpu.VMEM((ts + 8, tc), F32)],
    )(p_all, p_all, conv_w)


def conv_act_bwd(p_all, y, dqkv, conv_w, dp, S, ts):
    nt, tc = S // ts, CONV_TC
    cb0 = CB_GQ * 128 // tc
    r8 = ts // 8
    tiles_per_group = WIDTH // tc

    def body(x_ref, xh_ref, y_ref, yh_ref, d_ref, dh_ref, w_ref, dp_in, dx_ref, dw_ref, xs, ds):
        c, i = pl.program_id(0), pl.program_id(1)
        xs[0:8, :] = jnp.where(i > 0, xh_ref[...], 0.0)
        xs[8:8 + ts, :] = x_ref[...]
        for group in range(3):
            @pl.when(c // tiles_per_group == group)
            def _(group=group):
                for h in range(tc // D_HEAD):
                    sl = slice(h * D_HEAD, (h + 1) * D_HEAD)
                    ds[0:ts, sl] = _gdn_act_bwd(y_ref[:, sl], d_ref[:, sl], group)
                    ds[ts:ts + 8, sl] = jnp.where(i < nt - 1, _gdn_act_bwd(yh_ref[:, sl], dh_ref[:, sl], group), 0.0)
        w = w_ref[...]
        d = ds[0:ts, :]
        dx = w[3:4, :] * d
        for j in range(CONV_K - 1):
            dx = dx + w[j:j + 1, :] * ds[3 - j:3 - j + ts, :]
        dx_ref[...] = dx.astype(dx_ref.dtype)

        @pl.when(i == 0)
        def _():
            dw_ref[...] = jnp.zeros_like(dw_ref)

        rows = [jnp.sum(d * xs[5 + j:5 + j + ts, :], axis=0, keepdims=True) for j in range(CONV_K - 1)]
        rows.append(jnp.sum(d * x_ref[...], axis=0, keepdims=True))
        dw_ref[...] += jnp.concatenate(rows, axis=0)

    tile = pl.BlockSpec((ts, tc), lambda c, i: (i, c))
    next8 = pl.BlockSpec((8, tc), lambda c, i: (jnp.minimum((i + 1) * r8, nt * r8 - 1), c))
    return pl.pallas_call(
        body, name="conv_act_bwd", grid=(CONV_COLS // tc, nt),
        in_specs=[pl.BlockSpec((ts, tc), lambda c, i: (i, cb0 + c)),
                  pl.BlockSpec((8, tc), lambda c, i: (jnp.maximum(i * r8 - 1, 0), cb0 + c)),
                  tile, next8, tile, next8,
                  pl.BlockSpec((CONV_K, tc), lambda c, i: (0, c)), pl.BlockSpec(memory_space=pl.ANY)],
        out_specs=[pl.BlockSpec((ts, tc), lambda c, i: (i, cb0 + c)), pl.BlockSpec((CONV_K, tc), lambda c, i: (0, c))],
        out_shape=[jax.ShapeDtypeStruct(dp.shape, dp.dtype), jax.ShapeDtypeStruct((CONV_K, CONV_COLS), F32)],
        scratch_shapes=[pltpu.VMEM((ts + 8, tc), F32), pltpu.VMEM((ts + 8, tc), F32)],
        input_output_aliases={7: 0},
    )(p_all, p_all, y, y, dqkv, dqkv, conv_w, dp)


def _bdot(a, b, ca, cb):
    return lax.dot_general(a.astype(MXU_DTYPE), b.astype(MXU_DTYPE), (((ca,), (cb,)), ((0,), (0,))),
                           preferred_element_type=F32)


@jax.custom_vjp
def bmm_nn(a, b):
    return _bdot(a, b, 2, 1)


@jax.custom_vjp
def bmm_nt(a, b):
    return _bdot(a, b, 2, 2)


@jax.custom_vjp
def bmm_tn(a, b):
    return _bdot(a, b, 1, 1)


bmm_nn.defvjp(lambda a, b: (bmm_nn(a, b), (a, b)), lambda r, g: (bmm_nt(g, r[1]), bmm_tn(r[0], g)))
bmm_nt.defvjp(lambda a, b: (bmm_nt(a, b), (a, b)), lambda r, g: (bmm_nn(g, r[1]), bmm_tn(g, r[0])))
bmm_tn.defvjp(lambda a, b: (bmm_tn(a, b), (a, b)), lambda r, g: (bmm_nt(r[1], g), bmm_nn(r[0], g)))


def bdot_hi(a, b, ca=2, cb=1):
    return lax.dot_general(a, b, (((ca,), (cb,)), ((0,), (0,))), precision=SOLVE_PRECISION,
                           preferred_element_type=F32)


def _tri_inv_product(m):
    ii = lax.broadcasted_iota(jnp.int32, m.shape, 1)
    jj = lax.broadcasted_iota(jnp.int32, m.shape, 2)
    t = (ii == jj).astype(F32) - m
    pw = bdot_hi(m, m)
    for _ in range(4):
        t = t + bdot_hi(t, pw)
        pw = bdot_hi(pw, pw)
    return t + bdot_hi(t, pw)


def _tri_inv_bwd(t, g):
    return -bdot_hi(bdot_hi(t, g, 1, 1), t, 2, 2)


@jax.custom_vjp
def tri_inv(m):
    return _tri_inv_product(m)


@jax.custom_vjp
def tri_inv_known(m, t):
    return t


tri_inv.defvjp(lambda m: (lambda t: (t, t))(_tri_inv_product(m)), lambda t, g: (_tri_inv_bwd(t, g),))
tri_inv_known.defvjp(lambda m, t: (t, t), lambda t, g: (_tri_inv_bwd(t, g), jnp.zeros_like(t)))


def chunk_fn(q, k, v, gc_col, gc_row, beta, state, t_known=None):
    shape = (N_HEADS, CHUNK, CHUNK)
    ii = lax.broadcasted_iota(jnp.int32, shape, 1)
    jj = lax.broadcasted_iota(jnp.int32, shape, 2)
    lower = ii >= jj
    strict = ii > jj
    decay = jnp.exp(jnp.where(lower, gc_col - gc_row, NEG))
    kb = k * beta
    vb = v * beta
    m = jnp.where(strict, bmm_nt(kb, k) * decay, 0.0)
    t = tri_inv(m) if t_known is None else tri_inv_known(m, t_known)
    u = bdot_hi(t, vb)
    w = bdot_hi(t, kb * jnp.exp(gc_col))
    attn = jnp.where(lower, bmm_nt(q, k) * decay, 0.0)
    v_new = u - bmm_nn(w, state)
    o = bmm_nn(q * jnp.exp(gc_col), state) + bmm_nn(attn, v_new)
    g_last = gc_col[:, CHUNK - 1:CHUNK, :]
    k_dec = k * jnp.exp(g_last - gc_col)
    new_state = state * jnp.exp(g_last) + bmm_tn(k_dec, v_new)
    return (o, new_state, t) if t_known is None else (o, new_state)


GDN_CHUNKS_PER_STEP = 4


def _heads(ref, rows):
    return jnp.stack([ref[rows, h * D_HEAD:(h + 1) * D_HEAD] for h in range(N_HEADS)], axis=0)


def _head_cols(ref, rows, lane0):
    return jnp.stack([ref[rows, lane0 + h:lane0 + h + 1] for h in range(N_HEADS)], axis=0)


def gdn_fwd(qkv, smalls, gc_row, S):
    n = S // CHUNK
    per = _tile(n, GDN_CHUNKS_PER_STEP)
    rows_per = per * CHUNK

    def body(q_ref, k_ref, v_ref, sm_ref, gr_ref, o_ref, st_ref, t_ref, state):
        @pl.when(pl.program_id(0) == 0)
        def _():
            state[...] = jnp.zeros_like(state)

        s = state[...]
        for b in range(per):
            rows = pl.ds(b * CHUNK, CHUNK)
            st_ref[b] = s
            o, s, t = chunk_fn(_heads(q_ref, rows), _heads(k_ref, rows), _heads(v_ref, rows),
                               _head_cols(sm_ref, rows, 8), gr_ref[b][:, None, :], _head_cols(sm_ref, rows, 16), s)
            for h in range(N_HEADS):
                o_ref[rows, h * D_HEAD:(h + 1) * D_HEAD] = o[h]
            t_ref[b] = t
        state[...] = s

    return pl.pallas_call(
        body, name="gdn_chunk_fwd", grid=(n // per,),
        in_specs=[pl.BlockSpec((rows_per, WIDTH), lambda i, g=g: (i, g)) for g in range(3)]
        + [pl.BlockSpec((rows_per, 128), lambda i: (i, 0)), pl.BlockSpec((per, N_HEADS, CHUNK), lambda i: (i, 0, 0))],
        out_specs=[pl.BlockSpec((rows_per, WIDTH), lambda i: (i, 0)),
                   pl.BlockSpec((per, N_HEADS, D_HEAD, D_HEAD), lambda i: (i, 0, 0, 0)),
                   pl.BlockSpec((per, N_HEADS, CHUNK, CHUNK), lambda i: (i, 0, 0, 0))],
        out_shape=[jax.ShapeDtypeStruct((S, WIDTH), F32), jax.ShapeDtypeStruct((n, N_HEADS, D_HEAD, D_HEAD), F32),
                   jax.ShapeDtypeStruct((n, N_HEADS, CHUNK, CHUNK), F32)],
        scratch_shapes=[pltpu.VMEM((N_HEADS, D_HEAD, D_HEAD), F32)],
    )(qkv, qkv, qkv, smalls, gc_row)


def gdn_bwd(qkv, smalls, gc_row, states, tinv, do, S):
    n = S // CHUNK
    per = _tile(n, GDN_CHUNKS_PER_STEP)
    rows_per = per * CHUNK
    nb = n // per

    def body(q_ref, k_ref, v_ref, sm_ref, gr_ref, st_ref, t_ref, do_ref, dqkv_ref, dsm_ref, dgr_ref, dstate):
        @pl.when(pl.program_id(0) == 0)
        def _():
            dstate[...] = jnp.zeros_like(dstate)

        ds = dstate[...]
        for b in reversed(range(per)):
            rows = pl.ds(b * CHUNK, CHUNK)
            t_saved = t_ref[b]
            _, vjp = jax.vjp(lambda *a: chunk_fn(*a, t_known=t_saved), _heads(q_ref, rows), _heads(k_ref, rows),
                             _heads(v_ref, rows), _head_cols(sm_ref, rows, 8), gr_ref[b][:, None, :],
                             _head_cols(sm_ref, rows, 16), st_ref[b])
            dq, dk, dv, dgc, dgr, dbt, ds = vjp((_heads(do_ref, rows), ds))
            for h in range(N_HEADS):
                dqkv_ref[rows, h * D_HEAD:(h + 1) * D_HEAD] = dq[h]
                dqkv_ref[rows, WIDTH + h * D_HEAD:WIDTH + (h + 1) * D_HEAD] = dk[h]
                dqkv_ref[rows, 2 * WIDTH + h * D_HEAD:2 * WIDTH + (h + 1) * D_HEAD] = dv[h]
            cols = [dgc[h] for h in range(N_HEADS)] + [dbt[h] for h in range(N_HEADS)]
            dsm_ref[rows, :] = jnp.concatenate([jnp.zeros((CHUNK, 8), F32)] + cols
                                               + [jnp.zeros((CHUNK, 128 - 24), F32)], axis=1)
            dgr_ref[b] = jnp.concatenate([dgr[h] for h in range(N_HEADS)], axis=0)
        dstate[...] = ds

    rev = lambda c: (lambda i: (nb - 1 - i, c))
    rev4 = lambda i: (nb - 1 - i, 0, 0, 0)
    return pl.pallas_call(
        body, name="gdn_chunk_bwd", grid=(nb,),
        in_specs=[pl.BlockSpec((rows_per, WIDTH), rev(g)) for g in range(3)]
        + [pl.BlockSpec((rows_per, 128), rev(0)), pl.BlockSpec((per, N_HEADS, CHUNK), lambda i: (nb - 1 - i, 0, 0)),
           pl.BlockSpec((per, N_HEADS, D_HEAD, D_HEAD), rev4), pl.BlockSpec((per, N_HEADS, CHUNK, CHUNK), rev4),
           pl.BlockSpec((rows_per, WIDTH), rev(0))],
        out_specs=[pl.BlockSpec((rows_per, 3 * WIDTH), rev(0)), pl.BlockSpec((rows_per, 128), rev(0)),
                   pl.BlockSpec((per, N_HEADS, CHUNK), lambda i: (nb - 1 - i, 0, 0))],
        out_shape=[jax.ShapeDtypeStruct((S, 3 * WIDTH), F32)] + [jax.ShapeDtypeStruct((S, 128), F32),
                                                                  jax.ShapeDtypeStruct((n, N_HEADS, CHUNK), F32)],
        scratch_shapes=[pltpu.VMEM((N_HEADS, D_HEAD, D_HEAD), F32)],
    )(qkv, qkv, qkv, smalls, gc_row, states, tinv, do)


def _flip(a, bit):
    return 1 - a if bit else a


def allgather8(name, buf):
    R, W = buf.shape

    def body(x_ref, out_ref, send_sems, recv_sems, local_sem):
        x, y, c = lax.axis_index("x"), lax.axis_index("y"), lax.axis_index("c")

        def slot(px, py, pc):
            return out_ref.at[4 * px + 2 * py + pc]

        mine = pltpu.make_async_copy(x_ref, slot(x, y, c), local_sem)
        mine.start()
        sends = []
        for r in range(1, N_DEV):
            peer = (_flip(x, r & 4), _flip(y, r & 2), _flip(c, r & 1))
            cp = pltpu.make_async_remote_copy(src_ref=x_ref, dst_ref=slot(x, y, c), send_sem=send_sems.at[r - 1],
                                              recv_sem=recv_sems.at[r - 1], device_id=peer, device_id_type=MESH_ID)
            cp.start()
            sends.append(cp)
        for r in range(1, N_DEV):
            peer = (_flip(x, r & 4), _flip(y, r & 2), _flip(c, r & 1))
            pltpu.make_async_remote_copy(src_ref=x_ref, dst_ref=slot(*peer), send_sem=send_sems.at[r - 1],
                                         recv_sem=recv_sems.at[r - 1], device_id=peer,
                                         device_id_type=MESH_ID).wait_recv()
        for cp in sends:
            cp.wait_send()
        mine.wait()

    return pl.pallas_call(
        body, name=name,
        out_shape=jax.ShapeDtypeStruct((N_DEV, R, W), buf.dtype),
        in_specs=[pl.BlockSpec(memory_space=pltpu.VMEM)],
        out_specs=pl.BlockSpec(memory_space=pltpu.VMEM),
        scratch_shapes=[pltpu.SemaphoreType.DMA((N_DEV - 1,)), pltpu.SemaphoreType.DMA((N_DEV - 1,)),
                        pltpu.SemaphoreType.DMA],
    )(buf)


def prologue_allgather(name, x2, ng, sc, sh, blk, S, ts):
    R, W = blk.shape
    half = R // 2
    n_far = N_CHIPS - 1
    n = S // ts
    assert n >= 3

    def body(x_ref, ng_ref, sc_ref, sh_ref, w_ref, h_ref, out_ref, send_sems, recv_sems):
        i = pl.program_id(0)
        x, y, c = lax.axis_index("x"), lax.axis_index("y"), lax.axis_index("c")
        sibling = (x, y, 1 - c)
        chips = [(_flip(x, r & 2), _flip(y, r & 1)) for r in range(1, N_CHIPS)]

        def rows(px, py, pc):
            return out_ref.at[2 * px + py, pc]

        def copy(k, dst, to, src):
            return pltpu.make_async_remote_copy(src_ref=src, dst_ref=dst, send_sem=send_sems.at[k],
                                                recv_sem=recv_sems.at[k], device_id=to, device_id_type=MESH_ID)

        my_half = w_ref.at[c]
        first = [copy(j, rows(x, y, c), (*chip, c), my_half) for j, chip in enumerate(chips)]
        passed = [copy(n_far + j, rows(*chip, c), sibling, rows(*chip, c)) for j, chip in enumerate(chips)]

        @pl.when(i == 0)
        def _():
            for cp in first:
                cp.start()

        h_ref[...] = f_prologue(x_ref[...], ng_ref[...], sc_ref[...], sh_ref[...])[0].astype(h_ref.dtype)

        @pl.when(i == n - 2)
        def _():
            for j, chip in enumerate(chips):
                copy(j, rows(*chip, c), (*chip, c), my_half).wait_recv()
                passed[j].start()

        @pl.when(i == n - 1)
        def _():
            for j, chip in enumerate(chips):
                copy(n_far + j, rows(*chip, 1 - c), sibling, my_half).wait_recv()
            for cp in first + passed:
                cp.wait_send()

    row = pl.BlockSpec((ts, D_MODEL), lambda i: (i, 0))
    par = pl.BlockSpec((1, D_MODEL), lambda i: (0, 0))
    h, far = pl.pallas_call(
        body, name=name, grid=(n,),
        in_specs=[row, par, par, par, pl.BlockSpec(memory_space=pl.ANY)],
        out_specs=[row, pl.BlockSpec(memory_space=pl.ANY)],
        out_shape=[jax.ShapeDtypeStruct((S, D_MODEL), MXU_DTYPE),
                   jax.ShapeDtypeStruct((N_CHIPS, 2, half, W), blk.dtype)],
        scratch_shapes=[pltpu.SemaphoreType.DMA((2 * n_far,)), pltpu.SemaphoreType.DMA((2 * n_far,))],
    )(x2, ng, sc, sh, blk.reshape(2, half, W))
    own_chip = 2 * lax.axis_index("x") + lax.axis_index("y")
    return h, lax.dynamic_update_index_in_dim(far.reshape(N_CHIPS, R, W), blk, own_chip, axis=0)


def sibling_send_other_half(name, g4):
    n, R, W = g4.shape
    half = R // 2

    def body(x_ref, out_ref, send_sem, recv_sem):
        x, y, c = lax.axis_index("x"), lax.axis_index("y"), lax.axis_index("c")
        cp = pltpu.make_async_remote_copy(src_ref=x_ref.at[:, pl.ds((1 - c) * half, half), :], dst_ref=out_ref,
                                          send_sem=send_sem, recv_sem=recv_sem, device_id=(x, y, 1 - c),
                                          device_id_type=MESH_ID)
        cp.start()
        cp.wait_recv()
        cp.wait_send()

    return pl.pallas_call(
        body, name=name,
        out_shape=jax.ShapeDtypeStruct((n, half, W), g4.dtype),
        in_specs=[pl.BlockSpec(memory_space=pl.ANY)],
        out_specs=pl.BlockSpec(memory_space=pl.ANY),
        scratch_shapes=[pltpu.SemaphoreType.DMA, pltpu.SemaphoreType.DMA],
    )(g4)


def sibling_merge(name, mine):
    def body(x_ref, out_ref, send_sem, recv_sem):
        x, y, c = lax.axis_index("x"), lax.axis_index("y"), lax.axis_index("c")
        cp = pltpu.make_async_remote_copy(src_ref=x_ref, dst_ref=out_ref, send_sem=send_sem, recv_sem=recv_sem,
                                          device_id=(x, y, 1 - c), device_id_type=MESH_ID)
        cp.start()
        cp.wait_recv()
        cp.wait_send()

    other = pl.pallas_call(
        body, name=name,
        out_shape=jax.ShapeDtypeStruct(mine.shape, mine.dtype),
        in_specs=[pl.BlockSpec(memory_space=pl.ANY)],
        out_specs=pl.BlockSpec(memory_space=pl.ANY),
        scratch_shapes=[pltpu.SemaphoreType.DMA, pltpu.SemaphoreType.DMA],
    )(mine)
    first = lax.axis_index("c") == 0
    return jnp.concatenate([jnp.where(first, mine, other), jnp.where(first, other, mine)], axis=0)


def sum_parts(name, own, recv, tr):
    R, W = own.shape
    tr = _tile(R, tr)

    def body(o_ref, r_ref, out_ref):
        acc = o_ref[...]
        for k in range(recv.shape[0]):
            acc = acc + r_ref[k].astype(F32)
        out_ref[...] = acc

    return pl.pallas_call(
        body, name=name, grid=(R // tr,),
        in_specs=[pl.BlockSpec((tr, W), lambda i: (i, 0)), pl.BlockSpec((recv.shape[0], tr, W), lambda i: (0, i, 0))],
        out_specs=pl.BlockSpec((tr, W), lambda i: (i, 0)),
        out_shape=jax.ShapeDtypeStruct((R, W), F32),
    )(own, recv)


def sum_own_half(name, g4, recv, c_idx, tr, rounded=True):
    n, R, W = g4.shape
    half = R // 2
    tr = _tile(half, tr)
    nb = half // tr

    def body(c_ref, g_ref, r_ref, o_ref, *ob_ref):
        s = g_ref[0] + r_ref[0]
        o_ref[0] = s
        if rounded:
            ob_ref[0][0] = s.astype(ob_ref[0].dtype)

    mine = pl.BlockSpec((1, tr, W), lambda j, i, c: (j, i, 0))
    shapes = [jax.ShapeDtypeStruct((n, half, W), F32)] + ([jax.ShapeDtypeStruct((n, half, W), MXU_DTYPE)] * rounded)
    return pl.pallas_call(
        body, name=name,
        grid_spec=pltpu.PrefetchScalarGridSpec(
            num_scalar_prefetch=1, grid=(n, nb),
            in_specs=[pl.BlockSpec((1, tr, W), lambda j, i, c: (j, c[0] * nb + i, 0)), mine],
            out_specs=[mine] * len(shapes)),
        out_shape=shapes,
    )(c_idx, g4, recv)


def sum_devices(name, g):
    def body(g_ref, o_ref):
        acc = g_ref[0]
        for d in range(1, N_DEV):
            acc = acc + g_ref[d]
        o_ref[...] = acc

    return pl.pallas_call(body, name=name, out_shape=jax.ShapeDtypeStruct(g.shape[1:], F32))(g)


def ada_fwd(c_all, w_ada):
    K, N = w_ada.shape
    tn = _tile(N, 512)

    def body(c_ref, w_ref, o_ref):
        o_ref[...] = mm_nn(_silu(c_ref[...]), w_ref[...])

    return pl.pallas_call(
        body, name="ada_fwd", grid=(N // tn,),
        in_specs=[pl.BlockSpec((N_DEV, K), lambda j: (0, 0)), pl.BlockSpec((K, tn), lambda j: (0, j))],
        out_specs=pl.BlockSpec((N_DEV, tn), lambda j: (0, j)),
        out_shape=jax.ShapeDtypeStruct((N_DEV, N), F32),
    )(c_all, w_ada)


def ada_grad(c_all, dmod):
    K = c_all.shape[1]
    N = dmod.shape[1]
    tn = _tile(N, 512)

    def body(c_ref, d_ref, o_ref):
        o_ref[...] = mm_tn(_silu(c_ref[...]), d_ref[...])

    return pl.pallas_call(
        body, name="ada_grad", grid=(N // tn,),
        in_specs=[pl.BlockSpec((N_DEV, K), lambda j: (0, 0)), pl.BlockSpec((N_DEV, tn), lambda j: (0, j))],
        out_specs=pl.BlockSpec((K, tn), lambda j: (0, j)),
        out_shape=jax.ShapeDtypeStruct((K, N), F32),
    )(c_all, dmod)


def adamw(name, w, gparts, m, v, tr, tc=None):
    R, W = w.shape
    tr = R if tc else _tile(R, tr)
    ng = len(gparts)

    def body(w_ref, *refs):
        g_refs, (m_ref, v_ref, g_out, d_out, m_out, v_out) = refs[:ng], refs[ng:]
        g = g_refs[0][...]
        for r in g_refs[1:]:
            g = g + r[...]
        g_out[...] = g
        d_out[...], m_out[...], v_out[...] = _adamw_update(w_ref[...], g, m_ref[...], v_ref[...])

    spec = pl.BlockSpec((R, tc), lambda i: (0, i)) if tc else pl.BlockSpec((tr, W), lambda i: (i, 0))
    return pl.pallas_call(
        body, name=name, grid=(W // tc if tc else R // tr,),
        in_specs=[spec] * (3 + ng), out_specs=[spec] * 4,
        out_shape=[jax.ShapeDtypeStruct((R, W), F32)] * 4,
    )(w, *gparts, m, v)


def _adamw_update(w, g, m, v):
    m1 = ADAM_B1 * m + (1.0 - ADAM_B1) * g
    v1 = ADAM_B2 * v + (1.0 - ADAM_B2) * jnp.square(g)
    m_hat = m1 / (1.0 - ADAM_B1 ** ADAM_STEP)
    v_hat = v1 / (1.0 - ADAM_B2 ** ADAM_STEP)
    return -ADAM_LR * (m_hat / (jnp.sqrt(v_hat) + ADAM_EPS) + ADAM_WD * w), m1, v1


def adamw_small(name, ws, gs, ms, vs):
    n = len(ws)

    def body(*refs):
        for k in range(n):
            w_ref, g_ref, m_ref, v_ref = (refs[j * n + k] for j in range(4))
            outs = refs[4 * n + 3 * k:4 * n + 3 * k + 3]
            for o_ref, val in zip(outs, _adamw_update(w_ref[...], g_ref[...], m_ref[...], v_ref[...])):
                o_ref[...] = val

    return pl.pallas_call(
        body, name=name, out_shape=[jax.ShapeDtypeStruct(w.shape, F32) for w in ws for _ in range(3)],
    )(*ws, *gs, *ms, *vs)


_REF_SEGMENTS = ((0, 4 * WIDTH, 0), (4 * WIDTH, 4 * WIDTH + 8, 4 * WIDTH), (4 * WIDTH + 8, 8 * WIDTH + 8, -8),
                 (8 * WIDTH + 8, IN_WIDTH, 0))


def _internal_from_blocks(blocks, n):
    a = 4 * WIDTH
    pieces = []
    for lo, hi in ((0, a), (a + 8, 2 * a + 8), (a, a + 8), (2 * a + 8, IN_WIDTH)):
        for j in range(N_CHIPS):
            s, e = max(lo, j * n), min(hi, (j + 1) * n)
            if s < e:
                pieces.append(blocks[j][:, s - j * n:e - j * n])
    pieces.append(jnp.zeros(blocks.shape[1:2] + (128 - N_SMALL,), blocks.dtype))
    return jnp.concatenate(pieces, axis=1)


def _block_from_internal(g, j, n):
    pieces = []
    for lo, hi, shift in _REF_SEGMENTS:
        s, e = max(lo, j * n), min(hi, (j + 1) * n)
        if s < e:
            pieces.append(g[:, s + shift:e + shift])
    return jnp.concatenate(pieces, axis=1)


def _pad_lanes(v, n=128):
    return jnp.pad(v, ((0, 0), (0, n - v.shape[1])))


def kernel(x, c, norm_g, w_ada, b_ada, w_in, b_fgate, fox_qn_g, fox_kn_g, gdn_conv_w, gdn_A_log, gdn_dt_bias, gdn_norm_g, w_out, final_g, loss_target, m_norm_g, m_w_ada, m_b_ada, m_w_in, m_b_fgate, m_fox_qn_g, m_fox_kn_g, m_gdn_conv_w, m_gdn_A_log, m_gdn_dt_bias, m_gdn_norm_g, m_w_out, m_final_g, v_norm_g, v_w_ada, v_b_ada, v_w_in, v_b_fgate, v_fox_qn_g, v_fox_kn_g, v_gdn_conv_w, v_gdn_A_log, v_gdn_dt_bias, v_gdn_norm_g, v_w_out, v_final_g):
    S = x.shape[1]
    H = N_HEADS
    ix, iy, ic = lax.axis_index("x"), lax.axis_index("y"), lax.axis_index("c")
    chip = 2 * ix + iy
    me = 2 * chip + ic
    x2, tgt = x[0], loss_target[0]
    ts_wide = _tile(S, 256)
    ts = _tile(S, 512)
    ts_conv = _tile(S, 1024)
    ts_head = _tile(S, 2048)
    tq = _tile(S, 1024)
    nq = S // tq
    nch = S // CHUNK
    conv_cols_chip = CONV_COLS // N_CHIPS

    pay = jnp.concatenate([c, _pad_lanes(gdn_conv_w[0], D_MODEL), jnp.zeros((3, D_MODEL), F32)], axis=0)
    g1 = allgather8("ag_c", pay)
    c_all = g1[:, 0, :]
    conv_w = jnp.concatenate([g1[2 * j, 1:1 + CONV_K, :conv_cols_chip] for j in range(N_CHIPS)], axis=1)
    g2 = allgather8("ag_mod", ada_fwd(c_all, w_ada[0]))
    mod_rows = jnp.concatenate([g2[2 * j] for j in range(N_CHIPS)], axis=1)
    mod = lax.dynamic_slice_in_dim(mod_rows, me, 1, axis=0) + b_ada
    shift, scale, gate = mod[:, :D_MODEL], mod[:, D_MODEL:2 * D_MODEL], mod[:, 2 * D_MODEL:]

    n_in = w_in.shape[2]
    h_b, win4 = prologue_allgather("prologue_ag_w_in", x2, norm_g, scale, shift, w_in[0].astype(MXU_DTYPE), S, ts_wide)
    w_all = _internal_from_blocks(win4, n_in)
    w_out_b = w_out[0].astype(MXU_DTYPE)
    p_all, wout_far = matmul("in_proj", h_b, w_all, F32, tm=1024, tn=1664, tk=2048, rides=[("gather", w_out_b)])
    w_out_full = lax.dynamic_update_index_in_dim(wout_far, w_out_b, chip, axis=0).reshape(2 * WIDTH, D_MODEL)
    pv = jnp.concatenate([b_fgate, gdn_dt_bias, jnp.zeros((1, 128 - 16), F32)], axis=1)
    av = jnp.concatenate([jnp.zeros((1, 8), F32), gdn_A_log, jnp.zeros((1, 128 - 16), F32)], axis=1)
    smalls = smalls_fwd(p_all, pv, av, S, ts)
    frow = jnp.transpose(smalls[:, :H]).reshape(H, nq, 1, tq)
    gc_row = jnp.transpose(smalls[:, 8:16].reshape(nch, CHUNK, H), (0, 2, 1))
    head_b = ("head", D_HEAD, 0, WIDTH, MXU_DTYPE)
    head_f = ("head", D_HEAD, 0, WIDTH, F32)
    pcol = lambda cb: (p_all, "head", D_HEAD, cb)
    qn, kn, vb = rowwise_fwd("fox_pre", f_foxpre, S, ts_head, H,[pcol(CB_FQ), pcol(CB_FK), pcol(CB_FV)],
                             [fox_qn_g, fox_kn_g], [head_b] * 3)
    o_fox, lse = flash_fwd(qn, kn, vb, frow, S, tq)
    y_conv, qkv = conv_act_fwd(p_all, conv_w, S, ts_conv)
    o_gdn, states, tinv = gdn_fwd(qkv, smalls, gc_row, S)
    mixed = lax.empty((S, 2 * WIDTH), MXU_DTYPE)
    (mixed,) = rowwise_fwd("post_fox", f_postf, S, ts_head, H,[(o_fox, "head", D_HEAD, 0), pcol(CB_FZ)], [],
                           [("head", D_HEAD, 0, 2 * WIDTH, MXU_DTYPE, mixed)])
    (mixed,) = rowwise_fwd("post_gdn", f_postg, S, ts_head, H,[(o_gdn, "head", D_HEAD, 0), pcol(CB_GZ)], [gdn_norm_g],
                           [("head", D_HEAD, H, 2 * WIDTH, MXU_DTYPE, mixed)])
    mo = matmul("out_proj", mixed, w_out_full, F32, tm=1024, tn=2048, tk=2048)

    wide = lambda a: (a, "row", D_MODEL, 0)
    dx_res, dmo, d_gate, d_final_g, loss_acc = rowwise_bwd(
        "loss", f_loss, S, ts_wide, 1, [wide(x2), wide(mo), wide(tgt)], [gate, final_g[None, :]], [],
        [(0, "row", D_MODEL, 0, D_MODEL, F32), (1, "row", D_MODEL, 0, D_MODEL, MXU_DTYPE)], primal_sum=True)

    d_mixed = matmul("d_mixed", dmo, w_out_full, F32, tm=1024, tn=2048, tk=2048, trans_b=True)
    g_w_out = matmul("g_w_out", mixed, dmo, F32, tm=1024, tn=2048, tk=2048, trans_a=True)
    dp = lax.empty((S, P_COLS), MXU_DTYPE)
    into_dp = lambda idx, cb: (idx, "head", D_HEAD, cb, P_COLS, MXU_DTYPE, dp)
    do_fox, dp, delta = rowwise_bwd(
        "post_fox_bwd", f_postf, S, ts_head, H,[(o_fox, "head", D_HEAD, 0), pcol(CB_FZ)], [],
        [(d_mixed, "head", D_HEAD, 0)], [(0,) + head_b, into_dp(1, CB_FZ)],
        extra=lambda vals, cv, grads: (jnp.sum(grads[0].astype(MXU_DTYPE).astype(F32) * vals[0], axis=-1,
                                               keepdims=True),),
        extra_outs=[("stat", tq, 0, None, F32)])
    do_gdn, dp, d_gdn_norm_g = rowwise_bwd(
        "post_gdn_bwd", f_postg, S, ts_head, H,[(o_gdn, "head", D_HEAD, 0), pcol(CB_GZ)], [gdn_norm_g],
        [(d_mixed, "head", D_HEAD, 8)], [(0,) + head_f, into_dp(1, CB_GZ)])
    d_qn, d_kn, dp, dfq_row, dfk_row = flash_bwd(qn, kn, vb, do_fox, frow, lse, delta, dp, S, tq)
    dp, d_qn_g, d_kn_g = fox_norm_bwd(p_all, d_qn, d_kn, fox_qn_g, fox_kn_g, dp, S, ts_wide)
    d_qkv, dsm_chunk, dgc_row = gdn_bwd(qkv, smalls, gc_row, states, tinv, do_gdn, S)
    dp, d_conv_w = conv_act_bwd(p_all, y_conv, d_qkv, conv_w, dp, S, ts_conv)
    d_f = jnp.transpose((dfq_row + dfk_row).reshape(H, S))
    d_gc = jnp.transpose(dgc_row, (0, 2, 1)).reshape(S, H)
    dsm = dsm_chunk + jnp.concatenate([d_f, d_gc, jnp.zeros((S, 128 - 16), F32)], axis=1)
    dp, d_pv, d_av = smalls_bwd(p_all, pv, av, dsm, dp, S, ts)
    g_w_all = matmul("g_w_in", h_b, dp, F32, tm=1024, tn=1664, tk=2048, trans_a=True)

    c_idx = jnp.reshape(ic, (1,)).astype(jnp.int32)

    def chip_sums(tag, g4):
        recv = sibling_send_other_half("d2d_" + tag, g4)
        return sum_own_half("sum2_" + tag, g4, recv, c_idx, 128)

    def finish(tag, p4, far):
        own = lax.dynamic_index_in_dim(p4, chip, axis=0, keepdims=False)
        return sibling_merge("merge_" + tag, sum_parts("sum_" + tag, own, far, 128))

    g_int = g_w_all[None]
    (p_int,) = sum_own_half("sum2_w_in", g_int, sibling_send_other_half("d2d_w_in", g_int), c_idx, 128, rounded=False)
    p4_in = jnp.stack([_block_from_internal(p_int[0], j, n_in) for j in range(N_CHIPS)])
    p4_in_sent = p4_in.astype(MXU_DTYPE)
    p4_out, p4_out_sent = chip_sums("w_out", g_w_out.reshape(N_CHIPS, w_out.shape[1], D_MODEL))
    d_h, far_in, far_out = matmul("d_h", dp, w_all, F32, tm=1024, tn=2048, tk=1664, trans_b=True,
                                  rides=[("scatter", p4_in_sent), ("scatter", p4_out_sent)])
    g_w_in_full = finish("w_in", p4_in, far_in)
    g_w_out_full = finish("w_out", p4_out, far_out)
    d_x, d_norm_g, d_scale, d_shift = rowwise_bwd(
        "prologue_bwd", f_prologue_res, S, ts_wide, 1, [wide(x2)], [norm_g, scale, shift],
        [wide(d_h), wide(dx_res)], [(0, "row", D_MODEL, 0, D_MODEL, F32)])

    parts = [d_shift, d_scale, d_gate, d_norm_g, d_final_g, d_conv_w.reshape(1, CONV_K * CONV_COLS), d_qn_g, d_kn_g,
             d_gdn_norm_g, d_pv, d_av, loss_acc]
    n_pay = sum(p.shape[1] for p in parts)
    pay_w = -(-n_pay // 1024) * 1024
    pay2 = jnp.concatenate(parts + [jnp.zeros((1, pay_w - n_pay), F32)], axis=1).reshape(8, pay_w // 8)
    g3 = allgather8("ag_small", pay2)
    tot = sum_devices("sum_small", g3).reshape(1, pay_w)
    dmod_all = g3.reshape(N_DEV, pay_w)[:, :3 * D_MODEL]
    o = 3 * D_MODEL
    g_b_ada = tot[:, :o]
    g_norm_g, g_final_g = tot[:, o:o + D_MODEL], tot[:, o + D_MODEL:o + 2 * D_MODEL]
    o += 2 * D_MODEL
    g_conv_full = tot[:, o:o + CONV_K * CONV_COLS].reshape(CONV_K, CONV_COLS)
    g_conv = lax.dynamic_slice_in_dim(g_conv_full, chip * conv_cols_chip, conv_cols_chip, axis=1)
    o += CONV_K * CONV_COLS
    g_qn_g, g_kn_g, g_gdn_ng = tot[:, o:o + 128], tot[:, o + 128:o + 256], tot[:, o + 256:o + 384]
    g_pv, g_av = tot[:, o + 384:o + 512], tot[:, o + 512:o + 640]
    loss = tot[0, o + 640]
    g_b_fgate, g_dt_bias, g_a_log = g_pv[:, :8], g_pv[:, 8:16], g_av[:, 8:16]

    small_g = [g_norm_g, g_b_ada, g_final_g, g_conv, g_qn_g, g_kn_g, g_gdn_ng, g_b_fgate, g_a_log, g_dt_bias]
    small_upd = adamw_small(
        "adamw_small",
        [norm_g, b_ada, final_g[None, :], gdn_conv_w[0], fox_qn_g, fox_kn_g, gdn_norm_g, b_fgate, gdn_A_log, gdn_dt_bias],
        small_g,
        [m_norm_g, m_b_ada, m_final_g[None, :], m_gdn_conv_w[0], m_fox_qn_g, m_fox_kn_g, m_gdn_norm_g, m_b_fgate,
         m_gdn_A_log, m_gdn_dt_bias],
        [v_norm_g, v_b_ada, v_final_g[None, :], v_gdn_conv_w[0], v_fox_qn_g, v_fox_kn_g, v_gdn_norm_g, v_b_fgate,
         v_gdn_A_log, v_gdn_dt_bias])

    def small_leaves(k):
        vals = small_g if k == 0 else [small_upd[3 * p + k - 1] for p in range(len(small_g))]
        return [v[0] if p == 2 else (v[None] if p == 3 else v) for p, v in enumerate(vals)]

    n_ada = w_ada.shape[2]
    g_w_ada = ada_grad(c_all, lax.dynamic_slice_in_dim(dmod_all, chip * n_ada, n_ada, axis=1))
    ada_out = adamw("adamw_w_ada", w_ada[0], [g_w_ada], m_w_ada[0], v_w_ada[0], 128)

    in_out = [jnp.transpose(o) for o in adamw("adamw_w_in", jnp.transpose(w_in[0]), [jnp.transpose(g_w_in_full)],
                                              jnp.transpose(m_w_in[0]), jnp.transpose(v_w_in[0]), None, tc=256)]
    out_out = adamw("adamw_w_out", w_out[0], [g_w_out_full], m_w_out[0], v_w_out[0], 128)

    res = [loss, d_x[None]]
    for k in range(4):
        s = small_leaves(k)
        big = [ada_out[k][None], in_out[k][None], out_out[k][None]]
        res += [s[0], big[0], s[1], big[1], s[7], s[4], s[5], s[3], s[8], s[9], s[6], big[2], s[2]]
    return tuple(res)
```

```python
import functools

import jax
import jax.numpy as jnp
from jax import lax
from jax.experimental import pallas as pl
from jax.experimental.pallas import tpu as pltpu

F32 = jnp.float32
MXU_DTYPE = jnp.bfloat16
HIGHEST = lax.Precision.HIGHEST
SOLVE_PRECISION = lax.Precision.HIGH
MESH_ID = pl.DeviceIdType.MESH

D_MODEL = 2048
N_HEADS = 8
D_HEAD = 128
WIDTH = N_HEADS * D_HEAD
CHUNK = 64
CONV_K = 4
EPS = 1e-6
NEG = -1e30
ATT_SCALE = D_HEAD ** -0.5
N_SMALL = 3 * N_HEADS
P_COLS = 8 * WIDTH + 128
CB_FQ, CB_FK, CB_FV, CB_FZ, CB_GQ, CB_GK, CB_GV, CB_GZ, CB_SMALL = 0, 8, 16, 24, 32, 40, 48, 56, 64
IN_WIDTH = 8 * WIDTH + N_SMALL
N_CHIPS = 4
N_DEV = 8

ADAM_LR, ADAM_B1, ADAM_B2, ADAM_EPS, ADAM_WD, ADAM_STEP = 0.001, 0.9, 0.999, 1e-08, 0.01, 10


def _dotg(a, b, dims):
    return lax.dot_general(a.astype(MXU_DTYPE), b.astype(MXU_DTYPE), (dims, ((), ())), preferred_element_type=F32)


@jax.custom_vjp
def mm_nn(a, b):
    return _dotg(a, b, ((1,), (0,)))


@jax.custom_vjp
def mm_nt(a, b):
    return _dotg(a, b, ((1,), (1,)))


@jax.custom_vjp
def mm_tn(a, b):
    return _dotg(a, b, ((0,), (0,)))


mm_nn.defvjp(lambda a, b: (mm_nn(a, b), (a, b)), lambda r, g: (mm_nt(g, r[1]), mm_tn(r[0], g)))
mm_nt.defvjp(lambda a, b: (mm_nt(a, b), (a, b)), lambda r, g: (mm_nn(g, r[1]), mm_tn(g, r[0])))
mm_tn.defvjp(lambda a, b: (mm_tn(a, b), (a, b)), lambda r, g: (mm_nt(r[1], g), mm_nn(r[0], g)))


def hdot(a, b):
    return lax.dot_general(a, b, (((1,), (0,)), ((), ())), precision=HIGHEST, preferred_element_type=F32)


def _sigmoid(x):
    return 0.5 * (jnp.tanh(0.5 * x) + 1.0)


def _silu(x):
    return x * _sigmoid(x)


def _softplus(x):
    return jnp.maximum(x, 0.0) + jnp.log(1.0 + jnp.exp(-jnp.abs(x)))


def _log_sigmoid(x):
    return jnp.minimum(x, 0.0) - jnp.log(1.0 + jnp.exp(-jnp.abs(x)))


def _rms(x, g):
    return x * lax.rsqrt(jnp.mean(x * x, axis=-1, keepdims=True) + EPS) * g


def _l2n(x):
    return x * lax.rsqrt(jnp.sum(x * x, axis=-1, keepdims=True) + EPS)


def _tile(n, pref):
    t = min(n, pref)
    assert n % t == 0, (n, pref)
    return t


def _bspec(kind, w, col0, ts):
    if kind == "row":
        return pl.BlockSpec((ts, w), lambda i, h: (i, col0))
    if kind == "head":
        return pl.BlockSpec((ts, w), lambda i, h: (i, col0 + h))
    assert kind == "stat"
    return pl.BlockSpec((1, ts // w, 1, w), lambda i, h: (h, i, 0, 0))


def _ld(ref, kind):
    return ref[...]


def _st(ref, kind, val):
    if kind == "stat":
        w = ref.shape[3]
        for n in range(ref.shape[1]):
            ref[0, n] = _col_to_row(val[n * w:(n + 1) * w, :])
    else:
        ref[...] = val.astype(ref.dtype)


def _full_spec(a):
    nd = a.ndim
    return pl.BlockSpec(a.shape, lambda i, h: (0,) * nd)


def _out_struct(kind, w, S, width_total, dtype, nh):
    if kind == "stat":
        return jax.ShapeDtypeStruct((nh, S // w, 1, w), dtype)
    return jax.ShapeDtypeStruct((S, width_total), dtype)


def rowwise_fwd(name, f, S, ts, nh, ins, params, outs):
    n_in, n_p = len(ins), len(params)
    into = {j: o[5] for j, o in enumerate(outs) if len(o) > 5}
    outs = [o[:5] for o in outs]
    n_al = len(into)

    def body(*refs):
        vals = [_ld(r, s[1]).astype(F32) for r, s in zip(refs[:n_in], ins)]
        pv = [r[...] for r in refs[n_in:n_in + n_p]]
        res = f(*vals, *pv)
        for r, s, o in zip(refs[n_in + n_p + n_al:], outs, res):
            _st(r, s[0], o)

    return pl.pallas_call(
        body, name=name, grid=(S // ts, nh),
        in_specs=[_bspec(k, w, c0, ts) for (_, k, w, c0) in ins] + [_full_spec(p) for p in params]
        + [pl.BlockSpec(memory_space=pl.ANY)] * n_al,
        out_specs=[_bspec(k, w, c0, ts) for (k, w, c0, _, _) in outs],
        out_shape=[jax.ShapeDtypeStruct(into[j].shape, into[j].dtype) if j in into else _out_struct(k, w, S, tw, dt, nh)
                   for j, (k, w, c0, tw, dt) in enumerate(outs)],
        input_output_aliases={n_in + n_p + n: j for n, j in enumerate(sorted(into))},
    )(*[a for (a, _, _, _) in ins], *params, *[into[j] for j in sorted(into)])


def rowwise_bwd(name, f, S, ts, nh, ins, params, cts, row_grads, extra=None, extra_outs=(), primal_sum=False):
    n_in, n_p, n_ct = len(ins), len(params), len(cts)
    n_rg, n_ex = len(row_grads), len(extra_outs)
    into = {j: rg[6] for j, rg in enumerate(row_grads) if len(rg) > 6}
    row_grads = [rg[:6] for rg in row_grads]
    n_al = len(into)

    def body(*refs):
        first = (pl.program_id(0) == 0) & (pl.program_id(1) == 0)
        in_refs = refs[:n_in]
        p_refs = refs[n_in:n_in + n_p]
        ct_refs = refs[n_in + n_p:n_in + n_p + n_ct]
        o = n_in + n_p + n_ct + n_al
        rg_refs = refs[o:o + n_rg]
        ex_refs = refs[o + n_rg:o + n_rg + n_ex]
        acc_refs = refs[o + n_rg + n_ex:]
        vals = [_ld(r, s[1]).astype(F32) for r, s in zip(in_refs, ins)]
        pv = [r[...] for r in p_refs]
        res, vjp = jax.vjp(f, *vals, *pv)
        if primal_sum:
            cv = [jnp.ones_like(res[0])] + [_ld(r, s[1]).astype(F32) for r, s in zip(ct_refs, cts)]
        else:
            cv = [_ld(r, s[1]).astype(F32) for r, s in zip(ct_refs, cts)]
        grads = vjp(tuple(cv))
        for r, s in zip(rg_refs, row_grads):
            _st(r, s[1], grads[s[0]])
        if extra is not None:
            for r, s, e in zip(ex_refs, extra_outs, extra(vals, cv, grads)):
                _st(r, s[0], e)

        @pl.when(first)
        def _():
            for r in acc_refs:
                r[...] = jnp.zeros_like(r)

        for r, g in zip(acc_refs[:n_p], grads[n_in:]):
            r[...] += g
        if primal_sum:
            acc_refs[n_p][...] += jnp.sum(res[0])

    acc_shapes = [jax.ShapeDtypeStruct(p.shape, F32) for p in params]
    acc_specs = [_full_spec(p) for p in params]
    if primal_sum:
        acc_shapes.append(jax.ShapeDtypeStruct((1, 128), F32))
        acc_specs.append(pl.BlockSpec((1, 128), lambda i, h: (0, 0)))
    rg_shapes = [jax.ShapeDtypeStruct(into[j].shape, into[j].dtype) if j in into else _out_struct(k, w, S, tw, dt, nh)
                 for j, (_, k, w, c0, tw, dt) in enumerate(row_grads)]
    first_alias = n_in + n_p + n_ct
    return pl.pallas_call(
        body, name=name, grid=(S // ts, nh),
        in_specs=[_bspec(k, w, c0, ts) for (_, k, w, c0) in ins] + [_full_spec(p) for p in params]
        + [_bspec(k, w, c0, ts) for (_, k, w, c0) in cts] + [pl.BlockSpec(memory_space=pl.ANY)] * n_al,
        out_specs=[_bspec(k, w, c0, ts) for (_, k, w, c0, _, _) in row_grads]
        + [_bspec(k, w, c0, ts) for (k, w, c0, _, _) in extra_outs] + acc_specs,
        out_shape=rg_shapes + [_out_struct(k, w, S, tw, dt, nh) for (k, w, c0, tw, dt) in extra_outs] + acc_shapes,
        input_output_aliases={first_alias + n: j for n, j in enumerate(sorted(into))},
    )(*[a for (a, _, _, _) in ins], *params, *[a for (a, _, _, _) in cts], *[into[j] for j in sorted(into)])


def f_prologue(x, ng, sc, sh):
    return (_rms(x, ng) * (1.0 + sc) + sh,)


def f_prologue_res(x, ng, sc, sh):
    return (_rms(x, ng) * (1.0 + sc) + sh, x)


def f_loss(x, mo, tgt, gate, fg):
    y = _rms(x + gate * mo, fg)
    return (0.5 * jnp.mean(jnp.square(y - tgt), axis=-1, keepdims=True),)


def f_foxpre(fq, fk, fv, qg, kg):
    return (_rms(fq, qg) * ATT_SCALE, _rms(fk, kg), fv)


def f_postf(o, z):
    return (o * _silu(z),)


def f_postg(o, z, ng):
    return (_rms(o, ng) * _silu(z),)


def fox_norm_bwd(p_all, d_qn, d_kn, qg, kg, dp, S, ts):
    def body(p_ref, dq_ref, dk_ref, qg_ref, kg_ref, dp_in, o_ref, dqg_ref, dkg_ref):
        @pl.when(pl.program_id(0) == 0)
        def _():
            dqg_ref[...] = jnp.zeros_like(dqg_ref)
            dkg_ref[...] = jnp.zeros_like(dkg_ref)

        for ct_ref, g_ref, dg_ref, col0 in ((dq_ref, qg_ref, dqg_ref, 0), (dk_ref, kg_ref, dkg_ref, WIDTH)):
            dg = jnp.zeros((1, D_HEAD), F32)
            for h in range(N_HEADS):
                _, vjp = jax.vjp(_rms, p_ref[:, col0 + h * D_HEAD:col0 + (h + 1) * D_HEAD], g_ref[...])
                dx, dg_h = vjp(ct_ref[:, h * D_HEAD:(h + 1) * D_HEAD])
                o_ref[:, col0 + h * D_HEAD:col0 + (h + 1) * D_HEAD] = dx.astype(o_ref.dtype)
                dg = dg + dg_h
            dg_ref[...] += dg

    gain = pl.BlockSpec((1, D_HEAD), lambda i: (0, 0))
    both = pl.BlockSpec((ts, 2 * WIDTH), lambda i: (i, 0))
    one = pl.BlockSpec((ts, WIDTH), lambda i: (i, 0))
    return pl.pallas_call(
        body, name="fox_norm_bwd", grid=(S // ts,),
        in_specs=[both, one, one, gain, gain, pl.BlockSpec(memory_space=pl.ANY)],
        out_specs=[both, gain, gain],
        out_shape=[jax.ShapeDtypeStruct(dp.shape, dp.dtype), jax.ShapeDtypeStruct((1, D_HEAD), F32),
                   jax.ShapeDtypeStruct((1, D_HEAD), F32)],
        input_output_aliases={5: 0},
    )(p_all, d_qn, d_kn, qg, kg, dp)


def _chip_exchange(kind, x_ref, out_ref, send_sems, recv_sems):
    x, y, c = lax.axis_index("x"), lax.axis_index("y"), lax.axis_index("c")
    if kind == "sibling_half":
        half = x_ref.shape[1] // 2
        cp = pltpu.make_async_remote_copy(src_ref=x_ref.at[:, pl.ds((1 - c) * half, half), :], dst_ref=out_ref,
                                          send_sem=send_sems.at[0], recv_sem=recv_sems.at[0],
                                          device_id=(x, y, 1 - c), device_id_type=MESH_ID)
        return [cp], [cp]
    sends, arrivals = [], []
    for r in range(1, N_CHIPS):
        px, py = _flip(x, r & 2), _flip(y, r & 1)
        if kind == "scatter":
            src, dst, landed = x_ref.at[2 * px + py], out_ref.at[r - 1], out_ref.at[r - 1]
        else:
            src, dst, landed = x_ref, out_ref.at[2 * x + y], out_ref.at[2 * px + py]
        mk = functools.partial(pltpu.make_async_remote_copy, send_sem=send_sems.at[r - 1], recv_sem=recv_sems.at[r - 1],
                               device_id=(px, py, c), device_id_type=MESH_ID)
        sends.append(mk(src_ref=src, dst_ref=dst))
        arrivals.append(mk(src_ref=src, dst_ref=landed))
    return sends, arrivals


def _exchange_shape(kind, x):
    if kind == "sibling_half":
        return jax.ShapeDtypeStruct((x.shape[0], x.shape[1] // 2, x.shape[2]), x.dtype)
    return jax.ShapeDtypeStruct(((N_CHIPS - 1,) + x.shape[1:]) if kind == "scatter" else ((N_CHIPS,) + x.shape), x.dtype)


def matmul(name, a, b, out_dtype, tm=512, tn=512, tk=2048, trans_a=False, trans_b=False, rides=()):
    M, K = a.shape[::-1] if trans_a else a.shape
    N, K2 = b.shape if trans_b else b.shape[::-1]
    assert K == K2
    tm, tn, tk = _tile(M, tm), _tile(N, tn), _tile(K, tk)
    ni, nj, nk = M // tm, N // tn, K // tk
    a_spec = pl.BlockSpec((tk, tm), lambda i, j, k: (k, i)) if trans_a else pl.BlockSpec((tm, tk), lambda i, j, k: (i, k))
    b_spec = pl.BlockSpec((tn, tk), lambda i, j, k: (j, k)) if trans_b else pl.BlockSpec((tk, tn), lambda i, j, k: (k, j))
    n_ride = len(rides)

    def body(a_ref, b_ref, *rest):
        ride_in, o_ref, ride_out = rest[:n_ride], rest[n_ride], rest[n_ride + 1:2 * n_ride + 1]
        scr = rest[2 * n_ride + 1:]
        i, j, k = pl.program_id(0), pl.program_id(1), pl.program_id(2)
        exchanges = [_chip_exchange(kind, ride_in[n], ride_out[n], scr[2 * n], scr[2 * n + 1])
                     for n, (kind, _) in enumerate(rides)]

        @pl.when((i == 0) & (j == 0) & (k == 0))
        def _():
            for sends, _ in exchanges:
                for cp in sends:
                    cp.start()

        part = lax.dot_general(a_ref[...], b_ref[...], (((0 if trans_a else 1,), (1 if trans_b else 0,)), ((), ())),
                               preferred_element_type=F32)
        if nk == 1:
            o_ref[...] = part.astype(o_ref.dtype)
        else:
            acc = scr[2 * n_ride]

            @pl.when(k == 0)
            def _():
                acc[...] = part

            @pl.when(k > 0)
            def _():
                acc[...] += part

            @pl.when(k == nk - 1)
            def _():
                o_ref[...] = acc[...].astype(o_ref.dtype)

        @pl.when((i == ni - 1) & (j == nj - 1) & (k == nk - 1))
        def _():
            for sends, arrivals in exchanges:
                for cp in arrivals:
                    cp.wait_recv()
                for cp in sends:
                    cp.wait_send()

    sems = [pltpu.SemaphoreType.DMA((N_CHIPS - 1,))] * (2 * n_ride)
    any_spec = pl.BlockSpec(memory_space=pl.ANY)
    res = pl.pallas_call(
        body, name=name, grid=(ni, nj, nk),
        in_specs=[a_spec, b_spec] + [any_spec] * n_ride,
        out_specs=[pl.BlockSpec((tm, tn), lambda i, j, k: (i, j))] + [any_spec] * n_ride,
        out_shape=[jax.ShapeDtypeStruct((M, N), out_dtype)] + [_exchange_shape(kind, x) for kind, x in rides],
        scratch_shapes=sems + ([] if nk == 1 else [pltpu.VMEM((tm, tn), F32)]),
        compiler_params=pltpu.CompilerParams(
            dimension_semantics=("arbitrary",) * 3 if rides else ("parallel", "parallel", "arbitrary")),
    )(a, b, *[x for _, x in rides])
    return tuple(res) if rides else res[0]


def _small_f(z, pv, av):
    lane = lax.broadcasted_iota(jnp.int32, z.shape, 1)
    zz = z + pv
    logf = _log_sigmoid(zz)
    gval = -jnp.exp(av) * _softplus(zz)
    beta = _sigmoid(zz)
    return jnp.where(lane < 8, logf, jnp.where(lane < 16, gval, jnp.where(lane < 24, beta, 0.0)))


def _tri(ts, upper):
    r = lax.broadcasted_iota(jnp.int32, (ts, ts), 0)
    c = lax.broadcasted_iota(jnp.int32, (ts, ts), 1)
    keep = (r <= c) if upper else (r >= c)
    same_chunk = (r // CHUNK) == (c // CHUNK)
    return keep.astype(F32), (keep & same_chunk).astype(F32)


def smalls_fwd(p_all, pv, av, S, ts):
    nt = S // ts

    def body(z_ref, pv_ref, av_ref, o_ref, carry):
        i = pl.program_id(0)

        @pl.when(i == 0)
        def _():
            carry[...] = jnp.zeros_like(carry)

        e = _small_f(z_ref[...], pv_ref[...], av_ref[...])
        lane = lax.broadcasted_iota(jnp.int32, e.shape, 1)
        l_all, l_chunk = _tri(ts, upper=False)
        cum_all = hdot(l_all, e) + carry[...]
        cum_chunk = hdot(l_chunk, e)
        carry[...] = cum_all[ts - 1:ts, :]
        o_ref[...] = jnp.where(lane < 8, cum_all, jnp.where(lane < 16, cum_chunk, e))

    return pl.pallas_call(
        body, name="smalls_fwd", grid=(nt,),
        in_specs=[pl.BlockSpec((ts, 128), lambda i: (i, CB_SMALL)), pl.BlockSpec((1, 128), lambda i: (0, 0)),
                  pl.BlockSpec((1, 128), lambda i: (0, 0))],
        out_specs=pl.BlockSpec((ts, 128), lambda i: (i, 0)),
        out_shape=jax.ShapeDtypeStruct((S, 128), F32),
        scratch_shapes=[pltpu.VMEM((1, 128), F32)],
    )(p_all, pv, av)


def smalls_bwd(p_all, pv, av, dsm, dp, S, ts):
    nt = S // ts

    def body(z_ref, pv_ref, av_ref, d_ref, dp_in, dz_ref, dpv_ref, dav_ref, carry):
        i = pl.program_id(0)

        @pl.when(i == 0)
        def _():
            carry[...] = jnp.zeros_like(carry)
            dpv_ref[...] = jnp.zeros_like(dpv_ref)
            dav_ref[...] = jnp.zeros_like(dav_ref)

        d = d_ref[...]
        lane = lax.broadcasted_iota(jnp.int32, d.shape, 1)
        u_all, u_chunk = _tri(ts, upper=True)
        rev_all = hdot(u_all, d) + carry[...]
        rev_chunk = hdot(u_chunk, d)
        carry[...] = rev_all[0:1, :]
        de = jnp.where(lane < 8, rev_all, jnp.where(lane < 16, rev_chunk, d))
        _, vjp = jax.vjp(_small_f, z_ref[...], pv_ref[...], av_ref[...])
        dz, dpv, dav = vjp(de)
        dz_ref[...] = dz.astype(dz_ref.dtype)
        dpv_ref[...] += dpv
        dav_ref[...] += dav

    rev = lambda i: (nt - 1 - i, 0)
    small_cols = pl.BlockSpec((ts, 128), lambda i: (nt - 1 - i, CB_SMALL))
    return pl.pallas_call(
        body, name="smalls_bwd", grid=(nt,),
        in_specs=[small_cols, pl.BlockSpec((1, 128), lambda i: (0, 0)), pl.BlockSpec((1, 128), lambda i: (0, 0)),
                  pl.BlockSpec((ts, 128), rev), pl.BlockSpec(memory_space=pl.ANY)],
        out_specs=[small_cols, pl.BlockSpec((1, 128), lambda i: (0, 0)), pl.BlockSpec((1, 128), lambda i: (0, 0))],
        out_shape=[jax.ShapeDtypeStruct(dp.shape, dp.dtype), jax.ShapeDtypeStruct((1, 128), F32),
                   jax.ShapeDtypeStruct((1, 128), F32)],
        scratch_shapes=[pltpu.VMEM((1, 128), F32)],
        input_output_aliases={4: 0},
    )(p_all, pv, av, dsm, dp)


def _col_to_row(col):
    return jnp.transpose(jnp.broadcast_to(col, (col.shape[0], 128)))[0:1, :]


def _row_to_col(row):
    return jnp.transpose(jnp.broadcast_to(row, (128, row.shape[1])))[:, 0:1]


def _causal_mask(s, keys_first):
    r = lax.broadcasted_iota(jnp.int32, s.shape, 0)
    c = lax.broadcasted_iota(jnp.int32, s.shape, 1)
    return jnp.where((r <= c) if keys_first else (c <= r), s, NEG)


def flash_fwd(qs, kn, vb, frow, S, tq):
    nq = S // tq
    tk = tq
    hp = 2
    wide = hp * D_HEAD
    head = lambda ref, h, rows=slice(None): ref[rows, h * D_HEAD:(h + 1) * D_HEAD]

    def body(q_ref, k_ref, v_ref, fr_ref, o_ref, lse_ref):
        i = pl.program_id(1)
        qs_ = [head(q_ref, h) for h in range(hp)]
        f_base = [fr_ref[h, i][:, 0:1] for h in range(hp)]

        def tile(j, carry, diagonal):
            rows = pl.ds(pl.multiple_of(j * tk, tk), tk)
            out = []
            for h in range(hp):
                m, l, acc = carry[3 * h:3 * h + 3]
                s = mm_nt(qs_[h], head(k_ref, h, rows)) - (fr_ref[h, j] - f_base[h])
                if diagonal:
                    s = _causal_mask(s, keys_first=False)
                m_new = jnp.maximum(m, jnp.max(s, axis=-1, keepdims=True))
                a = jnp.exp(m - m_new)
                p = jnp.exp(s - m_new)
                out += [m_new, a * l + jnp.sum(p, axis=-1, keepdims=True), a * acc + mm_nn(p, head(v_ref, h, rows))]
            return tuple(out)

        init = (jnp.full((tq, 1), NEG, F32), jnp.zeros((tq, 1), F32), jnp.zeros((tq, D_HEAD), F32)) * hp
        res = tile(i, lax.fori_loop(0, i, lambda j, c: tile(j, c, False), init), True)
        for h in range(hp):
            m, l, acc = res[3 * h:3 * h + 3]
            o_ref[:, h * D_HEAD:(h + 1) * D_HEAD] = acc / l
            lse_ref[h, 0] = _col_to_row(m + jnp.log(l))

    return pl.pallas_call(
        body, name="flash_fwd", grid=(N_HEADS // hp, nq),
        in_specs=[pl.BlockSpec((tq, wide), lambda h, i: (i, h)), pl.BlockSpec((S, wide), lambda h, i: (0, h)),
                  pl.BlockSpec((S, wide), lambda h, i: (0, h)),
                  pl.BlockSpec((hp, nq, 1, tk), lambda h, i: (h, 0, 0, 0))],
        out_specs=[pl.BlockSpec((tq, wide), lambda h, i: (i, h)),
                   pl.BlockSpec((hp, 1, 1, tq), lambda h, i: (h, i, 0, 0))],
        out_shape=[jax.ShapeDtypeStruct((S, WIDTH), F32), jax.ShapeDtypeStruct((N_HEADS, nq, 1, tq), F32)],
    )(qs, kn, vb, frow)


def flash_bwd(qs, kn, vb, do, frow, lse_row, delta_row, dp, S, tq):
    nq = S // tq
    tk = tq
    hp = 2
    wide = hp * D_HEAD
    head = lambda ref, h, rows=slice(None): ref[rows, h * D_HEAD:(h + 1) * D_HEAD]

    def body(q_ref, k_ref, v_ref, do_ref, fr_ref, lse_ref, dl_ref, dp_in, dq_ref, dk_ref, dv_ref, dfq_ref, dfk_ref):
        j = pl.program_id(1)
        ks = [head(k_ref, h) for h in range(hp)]
        vs = [head(v_ref, h) for h in range(hp)]
        fks = [_row_to_col(fr_ref[h, j]) for h in range(hp)]

        @pl.when(j == 0)
        def _():
            dq_ref[...] = jnp.zeros_like(dq_ref)
            dfq_ref[...] = jnp.zeros_like(dfq_ref)

        def tile(i, carry):
            rows = pl.ds(pl.multiple_of(i * tq, tq), tq)
            out = []
            for h in range(hp):
                dk, dv, dfk = carry[3 * h:3 * h + 3]
                q = head(q_ref, h, rows)
                do_ = head(do_ref, h, rows)
                st = mm_nt(ks[h], q) - (fks[h] - fr_ref[h, i][:, 0:1])
                pt = jnp.exp(st - lse_ref[h, i])
                dst = pt * (mm_nt(vs[h], do_) - dl_ref[h, i])
                dq_ref[rows, h * D_HEAD:(h + 1) * D_HEAD] += mm_tn(dst, ks[h]) * ATT_SCALE
                dfq_ref[h, i] += jnp.sum(dst, axis=0, keepdims=True)
                out += [dk + mm_nn(dst, q), dv + mm_nn(pt, do_), dfk - jnp.sum(dst, axis=-1, keepdims=True)]
            return tuple(out)

        def diagonal_tile():
            half = tk // 2
            out = []
            for h in range(hp):
                cols = slice(h * D_HEAD, (h + 1) * D_HEAD)
                f0 = fr_ref[h, j][:, 0:1]
                parts = []
                for a in range(2):
                    keys = slice(a * half, (a + 1) * half)
                    k_a, v_a, fk_a = ks[h][keys], vs[h][keys], fks[h][keys]
                    dk = jnp.zeros((half, D_HEAD), F32)
                    dv = jnp.zeros((half, D_HEAD), F32)
                    dfk = jnp.zeros((half, 1), F32)
                    for b in range(a, 2):
                        rows = pl.ds(pl.multiple_of(j * tq + b * half, half), half)
                        lanes = slice(b * half, (b + 1) * half)
                        q, do_ = q_ref[rows, cols], do_ref[rows, cols]
                        st = mm_nt(k_a, q) - (fk_a - f0)
                        if a == b:
                            st = _causal_mask(st, keys_first=True)
                        pt = jnp.exp(st - lse_ref[h, j][:, lanes])
                        dst = pt * (mm_nt(v_a, do_) - dl_ref[h, j][:, lanes])
                        dq_ref[rows, cols] += mm_tn(dst, k_a) * ATT_SCALE
                        dfq_ref[h, j, :, lanes] += jnp.sum(dst, axis=0, keepdims=True)
                        dk, dv = dk + mm_nn(dst, q), dv + mm_nn(pt, do_)
                        dfk = dfk - jnp.sum(dst, axis=-1, keepdims=True)
                    parts.append((dk, dv, dfk))
                out += [jnp.concatenate([p[n] for p in parts], axis=0) for n in range(3)]
            return tuple(out)

        res = lax.fori_loop(j + 1, nq, tile, diagonal_tile())
        for h in range(hp):
            dk, dv, dfk = res[3 * h:3 * h + 3]
            dk_ref[:, h * D_HEAD:(h + 1) * D_HEAD] = dk
            dv_ref[:, h * D_HEAD:(h + 1) * D_HEAD] = dv.astype(dv_ref.dtype)
            dfk_ref[h, 0] = _col_to_row(dfk)

    blk = pl.BlockSpec((tk, wide), lambda h, j: (j, h))
    full = pl.BlockSpec((S, wide), lambda h, j: (0, h), pipeline_mode=pl.Buffered(1))
    rows = pl.BlockSpec((hp, nq, 1, tq), lambda h, j: (h, 0, 0, 0))
    stat = jax.ShapeDtypeStruct((N_HEADS, nq, 1, tq), F32)
    return pl.pallas_call(
        body, name="flash_bwd", grid=(N_HEADS // hp, nq),
        in_specs=[full, blk, blk, full, rows, rows, rows, pl.BlockSpec(memory_space=pl.ANY)],
        out_specs=[full, blk, pl.BlockSpec((tk, wide), lambda h, j: (j, CB_FV // hp + h)), rows,
                   pl.BlockSpec((hp, 1, 1, tk), lambda h, j: (h, j, 0, 0))],
        out_shape=[jax.ShapeDtypeStruct((S, WIDTH), F32), jax.ShapeDtypeStruct((S, WIDTH), F32),
                   jax.ShapeDtypeStruct(dp.shape, dp.dtype), stat, stat],
        input_output_aliases={7: 2},
    )(qs, kn, vb, do, frow, lse_row, delta_row, dp)


CONV_COLS = 3 * WIDTH
CONV_TC = 512


def _gdn_act(y, group):
    s = _silu(y)
    if group == 2:
        return s
    return _l2n(s) * ATT_SCALE if group == 0 else _l2n(s)


def _gdn_act_bwd(y, d, group):
    _, vjp = jax.vjp(functools.partial(_gdn_act, group=group), y)
    return vjp(d)[0]


def conv_act_fwd(p_all, conv_w, S, ts):
    nt, tc = S // ts, CONV_TC
    cb0 = CB_GQ * 128 // tc
    r8 = ts // 8
    tiles_per_group = WIDTH // tc

    def body(x_ref, halo_ref, w_ref, y_ref, a_ref, scr):
        c, i = pl.program_id(0), pl.program_id(1)
        scr[0:8, :] = jnp.where(i > 0, halo_ref[...], 0.0)
        scr[8:8 + ts, :] = x_ref[...]
        w = w_ref[...]
        y = w[3:4, :] * x_ref[...]
        for j in range(CONV_K - 1):
            y = y + w[j:j + 1, :] * scr[5 + j:5 + j + ts, :]
        y_ref[...] = y
        for group in range(3):
            @pl.when(c // tiles_per_group == group)
            def _(group=group):
                for h in range(tc // D_HEAD):
                    sl = slice(h * D_HEAD, (h + 1) * D_HEAD)
                    a_ref[:, sl] = _gdn_act(y_ref[:, sl], group)

    out = pl.BlockSpec((ts, tc), lambda c, i: (i, c))
    return pl.pallas_call(
        body, name="conv_act_fwd", grid=(CONV_COLS // tc, nt),
        in_specs=[pl.BlockSpec((ts, tc), lambda c, i: (i, cb0 + c)),
                  pl.BlockSpec((8, tc), lambda c, i: (jnp.maximum(i * r8 - 1, 0), cb0 + c)),
                  pl.BlockSpec((CONV_K, tc), lambda c, i: (0, c))],
        out_specs=[out, out],
        out_shape=[jax.ShapeDtypeStruct((S, CONV_COLS), F32)] * 2,
        scratch_shapes=[pltpu.VMEM((ts + 8, tc), F32)],
    )(p_all, p_all, conv_w)


def conv_act_bwd(p_all, y, dqkv, conv_w, dp, S, ts):
    nt, tc = S // ts, CONV_TC
    cb0 = CB_GQ * 128 // tc
    r8 = ts // 8
    tiles_per_group = WIDTH // tc

    def body(x_ref, xh_ref, y_ref, yh_ref, d_ref, dh_ref, w_ref, dp_in, dx_ref, dw_ref, xs, ds):
        c, i = pl.program_id(0), pl.program_id(1)
        xs[0:8, :] = jnp.where(i > 0, xh_ref[...], 0.0)
        xs[8:8 + ts, :] = x_ref[...]
        for group in range(3):
            @pl.when(c // tiles_per_group == group)
            def _(group=group):
                for h in range(tc // D_HEAD):
                    sl = slice(h * D_HEAD, (h + 1) * D_HEAD)
                    ds[0:ts, sl] = _gdn_act_bwd(y_ref[:, sl], d_ref[:, sl], group)
                    ds[ts:ts + 8, sl] = jnp.where(i < nt - 1, _gdn_act_bwd(yh_ref[:, sl], dh_ref[:, sl], group), 0.0)
        w = w_ref[...]
        d = ds[0:ts, :]
        dx = w[3:4, :] * d
        for j in range(CONV_K - 1):
            dx = dx + w[j:j + 1, :] * ds[3 - j:3 - j + ts, :]
        dx_ref[...] = dx.astype(dx_ref.dtype)

        @pl.when(i == 0)
        def _():
            dw_ref[...] = jnp.zeros_like(dw_ref)

        rows = [jnp.sum(d * xs[5 + j:5 + j + ts, :], axis=0, keepdims=True) for j in range(CONV_K - 1)]
        rows.append(jnp.sum(d * x_ref[...], axis=0, keepdims=True))
        dw_ref[...] += jnp.concatenate(rows, axis=0)

    tile = pl.BlockSpec((ts, tc), lambda c, i: (i, c))
    next8 = pl.BlockSpec((8, tc), lambda c, i: (jnp.minimum((i + 1) * r8, nt * r8 - 1), c))
    return pl.pallas_call(
        body, name="conv_act_bwd", grid=(CONV_COLS // tc, nt),
        in_specs=[pl.BlockSpec((ts, tc), lambda c, i: (i, cb0 + c)),
                  pl.BlockSpec((8, tc), lambda c, i: (jnp.maximum(i * r8 - 1, 0), cb0 + c)),
                  tile, next8, tile, next8,
                  pl.BlockSpec((CONV_K, tc), lambda c, i: (0, c)), pl.BlockSpec(memory_space=pl.ANY)],
        out_specs=[pl.BlockSpec((ts, tc), lambda c, i: (i, cb0 + c)), pl.BlockSpec((CONV_K, tc), lambda c, i: (0, c))],
        out_shape=[jax.ShapeDtypeStruct(dp.shape, dp.dtype), jax.ShapeDtypeStruct((CONV_K, CONV_COLS), F32)],
        scratch_shapes=[pltpu.VMEM((ts + 8, tc), F32), pltpu.VMEM((ts + 8, tc), F32)],
        input_output_aliases={7: 0},
    )(p_all, p_all, y, y, dqkv, dqkv, conv_w, dp)


def _bdot(a, b, ca, cb):
    return lax.dot_general(a.astype(MXU_DTYPE), b.astype(MXU_DTYPE), (((ca,), (cb,)), ((0,), (0,))),
                           preferred_element_type=F32)


@jax.custom_vjp
def bmm_nn(a, b):
    return _bdot(a, b, 2, 1)


@jax.custom_vjp
def bmm_nt(a, b):
    return _bdot(a, b, 2, 2)


@jax.custom_vjp
def bmm_tn(a, b):
    return _bdot(a, b, 1, 1)


bmm_nn.defvjp(lambda a, b: (bmm_nn(a, b), (a, b)), lambda r, g: (bmm_nt(g, r[1]), bmm_tn(r[0], g)))
bmm_nt.defvjp(lambda a, b: (bmm_nt(a, b), (a, b)), lambda r, g: (bmm_nn(g, r[1]), bmm_tn(g, r[0])))
bmm_tn.defvjp(lambda a, b: (bmm_tn(a, b), (a, b)), lambda r, g: (bmm_nt(r[1], g), bmm_nn(r[0], g)))


def bdot_hi(a, b, ca=2, cb=1):
    return lax.dot_general(a, b, (((ca,), (cb,)), ((0,), (0,))), precision=SOLVE_PRECISION,
                           preferred_element_type=F32)


def _tri_inv_product(m):
    ii = lax.broadcasted_iota(jnp.int32, m.shape, 1)
    jj = lax.broadcasted_iota(jnp.int32, m.shape, 2)
    t = (ii == jj).astype(F32) - m
    pw = bdot_hi(m, m)
    for _ in range(4):
        t = t + bdot_hi(t, pw)
        pw = bdot_hi(pw, pw)
    return t + bdot_hi(t, pw)


def _tri_inv_bwd(t, g):
    return -bdot_hi(bdot_hi(t, g, 1, 1), t, 2, 2)


@jax.custom_vjp
def tri_inv(m):
    return _tri_inv_product(m)


@jax.custom_vjp
def tri_inv_known(m, t):
    return t


tri_inv.defvjp(lambda m: (lambda t: (t, t))(_tri_inv_product(m)), lambda t, g: (_tri_inv_bwd(t, g),))
tri_inv_known.defvjp(lambda m, t: (t, t), lambda t, g: (_tri_inv_bwd(t, g), jnp.zeros_like(t)))


def chunk_fn(q, k, v, gc_col, gc_row, beta, state, t_known=None):
    shape = (N_HEADS, CHUNK, CHUNK)
    ii = lax.broadcasted_iota(jnp.int32, shape, 1)
    jj = lax.broadcasted_iota(jnp.int32, shape, 2)
    lower = ii >= jj
    strict = ii > jj
    decay = jnp.exp(jnp.where(lower, gc_col - gc_row, NEG))
    kb = k * beta
    vb = v * beta
    m = jnp.where(strict, bmm_nt(kb, k) * decay, 0.0)
    t = tri_inv(m) if t_known is None else tri_inv_known(m, t_known)
    u = bdot_hi(t, vb)
    w = bdot_hi(t, kb * jnp.exp(gc_col))
    attn = jnp.where(lower, bmm_nt(q, k) * decay, 0.0)
    v_new = u - bmm_nn(w, state)
    o = bmm_nn(q * jnp.exp(gc_col), state) + bmm_nn(attn, v_new)
    g_last = gc_col[:, CHUNK - 1:CHUNK, :]
    k_dec = k * jnp.exp(g_last - gc_col)
    new_state = state * jnp.exp(g_last) + bmm_tn(k_dec, v_new)
    return (o, new_state, t) if t_known is None else (o, new_state)


GDN_CHUNKS_PER_STEP = 4


def _heads(ref, rows):
    return jnp.stack([ref[rows, h * D_HEAD:(h + 1) * D_HEAD] for h in range(N_HEADS)], axis=0)


def _head_cols(ref, rows, lane0):
    return jnp.stack([ref[rows, lane0 + h:lane0 + h + 1] for h in range(N_HEADS)], axis=0)


def gdn_fwd(qkv, smalls, gc_row, S):
    n = S // CHUNK
    per = _tile(n, GDN_CHUNKS_PER_STEP)
    rows_per = per * CHUNK

    def body(q_ref, k_ref, v_ref, sm_ref, gr_ref, o_ref, st_ref, t_ref, state):
        @pl.when(pl.program_id(0) == 0)
        def _():
            state[...] = jnp.zeros_like(state)

        s = state[...]
        for b in range(per):
            rows = pl.ds(b * CHUNK, CHUNK)
            st_ref[b] = s
            o, s, t = chunk_fn(_heads(q_ref, rows), _heads(k_ref, rows), _heads(v_ref, rows),
                               _head_cols(sm_ref, rows, 8), gr_ref[b][:, None, :], _head_cols(sm_ref, rows, 16), s)
            for h in range(N_HEADS):
                o_ref[rows, h * D_HEAD:(h + 1) * D_HEAD] = o[h]
            t_ref[b] = t
        state[...] = s

    return pl.pallas_call(
        body, name="gdn_chunk_fwd", grid=(n // per,),
        in_specs=[pl.BlockSpec((rows_per, WIDTH), lambda i, g=g: (i, g)) for g in range(3)]
        + [pl.BlockSpec((rows_per, 128), lambda i: (i, 0)), pl.BlockSpec((per, N_HEADS, CHUNK), lambda i: (i, 0, 0))],
        out_specs=[pl.BlockSpec((rows_per, WIDTH), lambda i: (i, 0)),
                   pl.BlockSpec((per, N_HEADS, D_HEAD, D_HEAD), lambda i: (i, 0, 0, 0)),
                   pl.BlockSpec((per, N_HEADS, CHUNK, CHUNK), lambda i: (i, 0, 0, 0))],
        out_shape=[jax.ShapeDtypeStruct((S, WIDTH), F32), jax.ShapeDtypeStruct((n, N_HEADS, D_HEAD, D_HEAD), F32),
                   jax.ShapeDtypeStruct((n, N_HEADS, CHUNK, CHUNK), F32)],
        scratch_shapes=[pltpu.VMEM((N_HEADS, D_HEAD, D_HEAD), F32)],
    )(qkv, qkv, qkv, smalls, gc_row)


def gdn_bwd(qkv, smalls, gc_row, states, tinv, do, S):
    n = S // CHUNK
    per = _tile(n, GDN_CHUNKS_PER_STEP)
    rows_per = per * CHUNK
    nb = n // per

    def body(q_ref, k_ref, v_ref, sm_ref, gr_ref, st_ref, t_ref, do_ref, dqkv_ref, dsm_ref, dgr_ref, dstate):
        @pl.when(pl.program_id(0) == 0)
        def _():
            dstate[...] = jnp.zeros_like(dstate)

        ds = dstate[...]
        for b in reversed(range(per)):
            rows = pl.ds(b * CHUNK, CHUNK)
            t_saved = t_ref[b]
            _, vjp = jax.vjp(lambda *a: chunk_fn(*a, t_known=t_saved), _heads(q_ref, rows), _heads(k_ref, rows),
                             _heads(v_ref, rows), _head_cols(sm_ref, rows, 8), gr_ref[b][:, None, :],
                             _head_cols(sm_ref, rows, 16), st_ref[b])
            dq, dk, dv, dgc, dgr, dbt, ds = vjp((_heads(do_ref, rows), ds))
            for h in range(N_HEADS):
                dqkv_ref[rows, h * D_HEAD:(h + 1) * D_HEAD] = dq[h]
                dqkv_ref[rows, WIDTH + h * D_HEAD:WIDTH + (h + 1) * D_HEAD] = dk[h]
                dqkv_ref[rows, 2 * WIDTH + h * D_HEAD:2 * WIDTH + (h + 1) * D_HEAD] = dv[h]
            cols = [dgc[h] for h in range(N_HEADS)] + [dbt[h] for h in range(N_HEADS)]
            dsm_ref[rows, :] = jnp.concatenate([jnp.zeros((CHUNK, 8), F32)] + cols
                                               + [jnp.zeros((CHUNK, 128 - 24), F32)], axis=1)
            dgr_ref[b] = jnp.concatenate([dgr[h] for h in range(N_HEADS)], axis=0)
        dstate[...] = ds

    rev = lambda c: (lambda i: (nb - 1 - i, c))
    rev4 = lambda i: (nb - 1 - i, 0, 0, 0)
    return pl.pallas_call(
        body, name="gdn_chunk_bwd", grid=(nb,),
        in_specs=[pl.BlockSpec((rows_per, WIDTH), rev(g)) for g in range(3)]
        + [pl.BlockSpec((rows_per, 128), rev(0)), pl.BlockSpec((per, N_HEADS, CHUNK), lambda i: (nb - 1 - i, 0, 0)),
           pl.BlockSpec((per, N_HEADS, D_HEAD, D_HEAD), rev4), pl.BlockSpec((per, N_HEADS, CHUNK, CHUNK), rev4),
           pl.BlockSpec((rows_per, WIDTH), rev(0))],
        out_specs=[pl.BlockSpec((rows_per, 3 * WIDTH), rev(0)), pl.BlockSpec((rows_per, 128), rev(0)),
                   pl.BlockSpec((per, N_HEADS, CHUNK), lambda i: (nb - 1 - i, 0, 0))],
        out_shape=[jax.ShapeDtypeStruct((S, 3 * WIDTH), F32)] + [jax.ShapeDtypeStruct((S, 128), F32),
                                                                  jax.ShapeDtypeStruct((n, N_HEADS, CHUNK), F32)],
        scratch_shapes=[pltpu.VMEM((N_HEADS, D_HEAD, D_HEAD), F32)],
    )(qkv, qkv, qkv, smalls, gc_row, states, tinv, do)


def _flip(a, bit):
    return 1 - a if bit else a


def allgather8(name, buf):
    R, W = buf.shape

    def body(x_ref, out_ref, send_sems, recv_sems, local_sem):
        x, y, c = lax.axis_index("x"), lax.axis_index("y"), lax.axis_index("c")

        def slot(px, py, pc):
            return out_ref.at[4 * px + 2 * py + pc]

        mine = pltpu.make_async_copy(x_ref, slot(x, y, c), local_sem)
        mine.start()
        sends = []
        for r in range(1, N_DEV):
            peer = (_flip(x, r & 4), _flip(y, r & 2), _flip(c, r & 1))
            cp = pltpu.make_async_remote_copy(src_ref=x_ref, dst_ref=slot(x, y, c), send_sem=send_sems.at[r - 1],
                                              recv_sem=recv_sems.at[r - 1], device_id=peer, device_id_type=MESH_ID)
            cp.start()
            sends.append(cp)
        for r in range(1, N_DEV):
            peer = (_flip(x, r & 4), _flip(y, r & 2), _flip(c, r & 1))
            pltpu.make_async_remote_copy(src_ref=x_ref, dst_ref=slot(*peer), send_sem=send_sems.at[r - 1],
                                         recv_sem=recv_sems.at[r - 1], device_id=peer,
                                         device_id_type=MESH_ID).wait_recv()
        for cp in sends:
            cp.wait_send()
        mine.wait()

    return pl.pallas_call(
        body, name=name,
        out_shape=jax.ShapeDtypeStruct((N_DEV, R, W), buf.dtype),
        in_specs=[pl.BlockSpec(memory_space=pltpu.VMEM)],
        out_specs=pl.BlockSpec(memory_space=pltpu.VMEM),
        scratch_shapes=[pltpu.SemaphoreType.DMA((N_DEV - 1,)), pltpu.SemaphoreType.DMA((N_DEV - 1,)),
                        pltpu.SemaphoreType.DMA],
    )(buf)


def prologue_allgather(name, x2, ng, sc, sh, blk, S, ts):
    R, W = blk.shape
    half = R // 2
    n_far = N_CHIPS - 1
    n = S // ts
    assert n >= 3

    def body(x_ref, ng_ref, sc_ref, sh_ref, w_ref, h_ref, out_ref, send_sems, recv_sems):
        i = pl.program_id(0)
        x, y, c = lax.axis_index("x"), lax.axis_index("y"), lax.axis_index("c")
        sibling = (x, y, 1 - c)
        chips = [(_flip(x, r & 2), _flip(y, r & 1)) for r in range(1, N_CHIPS)]

        def rows(px, py, pc):
            return out_ref.at[2 * px + py, pc]

        def copy(k, dst, to, src):
            return pltpu.make_async_remote_copy(src_ref=src, dst_ref=dst, send_sem=send_sems.at[k],
                                                recv_sem=recv_sems.at[k], device_id=to, device_id_type=MESH_ID)

        my_half = w_ref.at[c]
        first = [copy(j, rows(x, y, c), (*chip, c), my_half) for j, chip in enumerate(chips)]
        passed = [copy(n_far + j, rows(*chip, c), sibling, rows(*chip, c)) for j, chip in enumerate(chips)]

        @pl.when(i == 0)
        def _():
            for cp in first:
                cp.start()

        h_ref[...] = f_prologue(x_ref[...], ng_ref[...], sc_ref[...], sh_ref[...])[0].astype(h_ref.dtype)

        @pl.when(i == n - 2)
        def _():
            for j, chip in enumerate(chips):
                copy(j, rows(*chip, c), (*chip, c), my_half).wait_recv()
                passed[j].start()

        @pl.when(i == n - 1)
        def _():
            for j, chip in enumerate(chips):
                copy(n_far + j, rows(*chip, 1 - c), sibling, my_half).wait_recv()
            for cp in first + passed:
                cp.wait_send()

    row = pl.BlockSpec((ts, D_MODEL), lambda i: (i, 0))
    par = pl.BlockSpec((1, D_MODEL), lambda i: (0, 0))
    h, far = pl.pallas_call(
        body, name=name, grid=(n,),
        in_specs=[row, par, par, par, pl.BlockSpec(memory_space=pl.ANY)],
        out_specs=[row, pl.BlockSpec(memory_space=pl.ANY)],
        out_shape=[jax.ShapeDtypeStruct((S, D_MODEL), MXU_DTYPE),
                   jax.ShapeDtypeStruct((N_CHIPS, 2, half, W), blk.dtype)],
        scratch_shapes=[pltpu.SemaphoreType.DMA((2 * n_far,)), pltpu.SemaphoreType.DMA((2 * n_far,))],
    )(x2, ng, sc, sh, blk.reshape(2, half, W))
    own_chip = 2 * lax.axis_index("x") + lax.axis_index("y")
    return h, lax.dynamic_update_index_in_dim(far.reshape(N_CHIPS, R, W), blk, own_chip, axis=0)


def sibling_send_other_half(name, g4):
    n, R, W = g4.shape
    half = R // 2

    def body(x_ref, out_ref, send_sem, recv_sem):
        x, y, c = lax.axis_index("x"), lax.axis_index("y"), lax.axis_index("c")
        cp = pltpu.make_async_remote_copy(src_ref=x_ref.at[:, pl.ds((1 - c) * half, half), :], dst_ref=out_ref,
                                          send_sem=send_sem, recv_sem=recv_sem, device_id=(x, y, 1 - c),
                                          device_id_type=MESH_ID)
        cp.start()
        cp.wait_recv()
        cp.wait_send()

    return pl.pallas_call(
        body, name=name,
        out_shape=jax.ShapeDtypeStruct((n, half, W), g4.dtype),
        in_specs=[pl.BlockSpec(memory_space=pl.ANY)],
        out_specs=pl.BlockSpec(memory_space=pl.ANY),
        scratch_shapes=[pltpu.SemaphoreType.DMA, pltpu.SemaphoreType.DMA],
    )(g4)


def sibling_merge(name, mine):
    def body(x_ref, out_ref, send_sem, recv_sem):
        x, y, c = lax.axis_index("x"), lax.axis_index("y"), lax.axis_index("c")
        cp = pltpu.make_async_remote_copy(src_ref=x_ref, dst_ref=out_ref, send_sem=send_sem, recv_sem=recv_sem,
                                          device_id=(x, y, 1 - c), device_id_type=MESH_ID)
        cp.start()
        cp.wait_recv()
        cp.wait_send()

    other = pl.pallas_call(
        body, name=name,
        out_shape=jax.ShapeDtypeStruct(mine.shape, mine.dtype),
        in_specs=[pl.BlockSpec(memory_space=pl.ANY)],
        out_specs=pl.BlockSpec(memory_space=pl.ANY),
        scratch_shapes=[pltpu.SemaphoreType.DMA, pltpu.SemaphoreType.DMA],
    )(mine)
    first = lax.axis_index("c") == 0
    return jnp.concatenate([jnp.where(first, mine, other), jnp.where(first, other, mine)], axis=0)


def sum_parts(name, own, recv, tr):
    R, W = own.shape
    tr = _tile(R, tr)

    def body(o_ref, r_ref, out_ref):
        acc = o_ref[...]
        for k in range(recv.shape[0]):
            acc = acc + r_ref[k].astype(F32)
        out_ref[...] = acc

    return pl.pallas_call(
        body, name=name, grid=(R // tr,),
        in_specs=[pl.BlockSpec((tr, W), lambda i: (i, 0)), pl.BlockSpec((recv.shape[0], tr, W), lambda i: (0, i, 0))],
        out_specs=pl.BlockSpec((tr, W), lambda i: (i, 0)),
        out_shape=jax.ShapeDtypeStruct((R, W), F32),
    )(own, recv)


def sum_own_half(name, g4, recv, c_idx, tr, rounded=True):
    n, R, W = g4.shape
    half = R // 2
    tr = _tile(half, tr)
    nb = half // tr

    def body(c_ref, g_ref, r_ref, o_ref, *ob_ref):
        s = g_ref[0] + r_ref[0]
        o_ref[0] = s
        if rounded:
            ob_ref[0][0] = s.astype(ob_ref[0].dtype)

    mine = pl.BlockSpec((1, tr, W), lambda j, i, c: (j, i, 0))
    shapes = [jax.ShapeDtypeStruct((n, half, W), F32)] + ([jax.ShapeDtypeStruct((n, half, W), MXU_DTYPE)] * rounded)
    return pl.pallas_call(
        body, name=name,
        grid_spec=pltpu.PrefetchScalarGridSpec(
            num_scalar_prefetch=1, grid=(n, nb),
            in_specs=[pl.BlockSpec((1, tr, W), lambda j, i, c: (j, c[0] * nb + i, 0)), mine],
            out_specs=[mine] * len(shapes)),
        out_shape=shapes,
    )(c_idx, g4, recv)


def sum_devices(name, g):
    def body(g_ref, o_ref):
        acc = g_ref[0]
        for d in range(1, N_DEV):
            acc = acc + g_ref[d]
        o_ref[...] = acc

    return pl.pallas_call(body, name=name, out_shape=jax.ShapeDtypeStruct(g.shape[1:], F32))(g)


def ada_fwd(c_all, w_ada):
    K, N = w_ada.shape
    tn = _tile(N, 512)

    def body(c_ref, w_ref, o_ref):
        o_ref[...] = mm_nn(_silu(c_ref[...]), w_ref[...])

    return pl.pallas_call(
        body, name="ada_fwd", grid=(N // tn,),
        in_specs=[pl.BlockSpec((N_DEV, K), lambda j: (0, 0)), pl.BlockSpec((K, tn), lambda j: (0, j))],
        out_specs=pl.BlockSpec((N_DEV, tn), lambda j: (0, j)),
        out_shape=jax.ShapeDtypeStruct((N_DEV, N), F32),
    )(c_all, w_ada)


def ada_grad(c_all, dmod):
    K = c_all.shape[1]
    N = dmod.shape[1]
    tn = _tile(N, 512)

    def body(c_ref, d_ref, o_ref):
        o_ref[...] = mm_tn(_silu(c_ref[...]), d_ref[...])

    return pl.pallas_call(
        body, name="ada_grad", grid=(N // tn,),
        in_specs=[pl.BlockSpec((N_DEV, K), lambda j: (0, 0)), pl.BlockSpec((N_DEV, tn), lambda j: (0, j))],
        out_specs=pl.BlockSpec((K, tn), lambda j: (0, j)),
        out_shape=jax.ShapeDtypeStruct((K, N), F32),
    )(c_all, dmod)


def adamw(name, w, gparts, m, v, tr, tc=None):
    R, W = w.shape
    tr = R if tc else _tile(R, tr)
    ng = len(gparts)

    def body(w_ref, *refs):
        g_refs, (m_ref, v_ref, g_out, d_out, m_out, v_out) = refs[:ng], refs[ng:]
        g = g_refs[0][...]
        for r in g_refs[1:]:
            g = g + r[...]
        g_out[...] = g
        d_out[...], m_out[...], v_out[...] = _adamw_update(w_ref[...], g, m_ref[...], v_ref[...])

    spec = pl.BlockSpec((R, tc), lambda i: (0, i)) if tc else pl.BlockSpec((tr, W), lambda i: (i, 0))
    return pl.pallas_call(
        body, name=name, grid=(W // tc if tc else R // tr,),
        in_specs=[spec] * (3 + ng), out_specs=[spec] * 4,
        out_shape=[jax.ShapeDtypeStruct((R, W), F32)] * 4,
    )(w, *gparts, m, v)


def _adamw_update(w, g, m, v):
    m1 = ADAM_B1 * m + (1.0 - ADAM_B1) * g
    v1 = ADAM_B2 * v + (1.0 - ADAM_B2) * jnp.square(g)
    m_hat = m1 / (1.0 - ADAM_B1 ** ADAM_STEP)
    v_hat = v1 / (1.0 - ADAM_B2 ** ADAM_STEP)
    return -ADAM_LR * (m_hat / (jnp.sqrt(v_hat) + ADAM_EPS) + ADAM_WD * w), m1, v1


def adamw_small(name, ws, gs, ms, vs):
    n = len(ws)

    def body(*refs):
        for k in range(n):
            w_ref, g_ref, m_ref, v_ref = (refs[j * n + k] for j in range(4))
            outs = refs[4 * n + 3 * k:4 * n + 3 * k + 3]
            for o_ref, val in zip(outs, _adamw_update(w_ref[...], g_ref[...], m_ref[...], v_ref[...])):
                o_ref[...] = val

    return pl.pallas_call(
        body, name=name, out_shape=[jax.ShapeDtypeStruct(w.shape, F32) for w in ws for _ in range(3)],
    )(*ws, *gs, *ms, *vs)


_REF_SEGMENTS = ((0, 4 * WIDTH, 0), (4 * WIDTH, 4 * WIDTH + 8, 4 * WIDTH), (4 * WIDTH + 8, 8 * WIDTH + 8, -8),
                 (8 * WIDTH + 8, IN_WIDTH, 0))


def _internal_from_blocks(blocks, n):
    a = 4 * WIDTH
    pieces = []
    for lo, hi in ((0, a), (a + 8, 2 * a + 8), (a, a + 8), (2 * a + 8, IN_WIDTH)):
        for j in range(N_CHIPS):
            s, e = max(lo, j * n), min(hi, (j + 1) * n)
            if s < e:
                pieces.append(blocks[j][:, s - j * n:e - j * n])
    pieces.append(jnp.zeros(blocks.shape[1:2] + (128 - N_SMALL,), blocks.dtype))
    return jnp.concatenate(pieces, axis=1)


def _block_from_internal(g, j, n):
    pieces = []
    for lo, hi, shift in _REF_SEGMENTS:
        s, e = max(lo, j * n), min(hi, (j + 1) * n)
        if s < e:
            pieces.append(g[:, s + shift:e + shift])
    return jnp.concatenate(pieces, axis=1)


def _pad_lanes(v, n=128):
    return jnp.pad(v, ((0, 0), (0, n - v.shape[1])))


def kernel(x, c, norm_g, w_ada, b_ada, w_in, b_fgate, fox_qn_g, fox_kn_g, gdn_conv_w, gdn_A_log, gdn_dt_bias, gdn_norm_g, w_out, final_g, loss_target, m_norm_g, m_w_ada, m_b_ada, m_w_in, m_b_fgate, m_fox_qn_g, m_fox_kn_g, m_gdn_conv_w, m_gdn_A_log, m_gdn_dt_bias, m_gdn_norm_g, m_w_out, m_final_g, v_norm_g, v_w_ada, v_b_ada, v_w_in, v_b_fgate, v_fox_qn_g, v_fox_kn_g, v_gdn_conv_w, v_gdn_A_log, v_gdn_dt_bias, v_gdn_norm_g, v_w_out, v_final_g):
    S = x.shape[1]
    H = N_HEADS
    ix, iy, ic = lax.axis_index("x"), lax.axis_index("y"), lax.axis_index("c")
    chip = 2 * ix + iy
    me = 2 * chip + ic
    x2, tgt = x[0], loss_target[0]
    ts_wide = _tile(S, 256)
    ts = _tile(S, 512)
    ts_conv = _tile(S, 1024)
    ts_head = _tile(S, 2048)
    tq = _tile(S, 1024)
    nq = S // tq
    nch = S // CHUNK
    conv_cols_chip = CONV_COLS // N_CHIPS

    pay = jnp.concatenate([c, _pad_lanes(gdn_conv_w[0], D_MODEL), jnp.zeros((3, D_MODEL), F32)], axis=0)
    g1 = allgather8("ag_c", pay)
    c_all = g1[:, 0, :]
    conv_w = jnp.concatenate([g1[2 * j, 1:1 + CONV_K, :conv_cols_chip] for j in range(N_CHIPS)], axis=1)
    g2 = allgather8("ag_mod", ada_fwd(c_all, w_ada[0]))
    mod_rows = jnp.concatenate([g2[2 * j] for j in range(N_CHIPS)], axis=1)
    mod = lax.dynamic_slice_in_dim(mod_rows, me, 1, axis=0) + b_ada
    shift, scale, gate = mod[:, :D_MODEL], mod[:, D_MODEL:2 * D_MODEL], mod[:, 2 * D_MODEL:]

    n_in = w_in.shape[2]
    h_b, win4 = prologue_allgather("prologue_ag_w_in", x2, norm_g, scale, shift, w_in[0].astype(MXU_DTYPE), S, ts_wide)
    w_all = _internal_from_blocks(win4, n_in)
    w_out_b = w_out[0].astype(MXU_DTYPE)
    p_all, wout_far = matmul("in_proj", h_b, w_all, F32, tm=1024, tn=1664, tk=2048, rides=[("gather", w_out_b)])
    w_out_full = lax.dynamic_update_index_in_dim(wout_far, w_out_b, chip, axis=0).reshape(2 * WIDTH, D_MODEL)
    pv = jnp.concatenate([b_fgate, gdn_dt_bias, jnp.zeros((1, 128 - 16), F32)], axis=1)
    av = jnp.concatenate([jnp.zeros((1, 8), F32), gdn_A_log, jnp.zeros((1, 128 - 16), F32)], axis=1)
    smalls = smalls_fwd(p_all, pv, av, S, ts)
    frow = jnp.transpose(smalls[:, :H]).reshape(H, nq, 1, tq)
    gc_row = jnp.transpose(smalls[:, 8:16].reshape(nch, CHUNK, H), (0, 2, 1))
    head_b = ("head", D_HEAD, 0, WIDTH, MXU_DTYPE)
    head_f = ("head", D_HEAD, 0, WIDTH, F32)
    pcol = lambda cb: (p_all, "head", D_HEAD, cb)
    qn, kn, vb = rowwise_fwd("fox_pre", f_foxpre, S, ts_head, H,[pcol(CB_FQ), pcol(CB_FK), pcol(CB_FV)],
                             [fox_qn_g, fox_kn_g], [head_b] * 3)
    o_fox, lse = flash_fwd(qn, kn, vb, frow, S, tq)
    y_conv, qkv = conv_act_fwd(p_all, conv_w, S, ts_conv)
    o_gdn, states, tinv = gdn_fwd(qkv, smalls, gc_row, S)
    mixed = lax.empty((S, 2 * WIDTH), MXU_DTYPE)
    (mixed,) = rowwise_fwd("post_fox", f_postf, S, ts_head, H,[(o_fox, "head", D_HEAD, 0), pcol(CB_FZ)], [],
                           [("head", D_HEAD, 0, 2 * WIDTH, MXU_DTYPE, mixed)])
    (mixed,) = rowwise_fwd("post_gdn", f_postg, S, ts_head, H,[(o_gdn, "head", D_HEAD, 0), pcol(CB_GZ)], [gdn_norm_g],
                           [("head", D_HEAD, H, 2 * WIDTH, MXU_DTYPE, mixed)])
    mo = matmul("out_proj", mixed, w_out_full, F32, tm=1024, tn=2048, tk=2048)

    wide = lambda a: (a, "row", D_MODEL, 0)
    dx_res, dmo, d_gate, d_final_g, loss_acc = rowwise_bwd(
        "loss", f_loss, S, ts_wide, 1, [wide(x2), wide(mo), wide(tgt)], [gate, final_g[None, :]], [],
        [(0, "row", D_MODEL, 0, D_MODEL, F32), (1, "row", D_MODEL, 0, D_MODEL, MXU_DTYPE)], primal_sum=True)

    d_mixed = matmul("d_mixed", dmo, w_out_full, F32, tm=1024, tn=2048, tk=2048, trans_b=True)
    g_w_out = matmul("g_w_out", mixed, dmo, F32, tm=1024, tn=2048, tk=2048, trans_a=True)
    dp = lax.empty((S, P_COLS), MXU_DTYPE)
    into_dp = lambda idx, cb: (idx, "head", D_HEAD, cb, P_COLS, MXU_DTYPE, dp)
    do_fox, dp, delta = rowwise_bwd(
        "post_fox_bwd", f_postf, S, ts_head, H,[(o_fox, "head", D_HEAD, 0), pcol(CB_FZ)], [],
        [(d_mixed, "head", D_HEAD, 0)], [(0,) + head_b, into_dp(1, CB_FZ)],
        extra=lambda vals, cv, grads: (jnp.sum(grads[0].astype(MXU_DTYPE).astype(F32) * vals[0], axis=-1,
                                               keepdims=True),),
        extra_outs=[("stat", tq, 0, None, F32)])
    do_gdn, dp, d_gdn_norm_g = rowwise_bwd(
        "post_gdn_bwd", f_postg, S, ts_head, H,[(o_gdn, "head", D_HEAD, 0), pcol(CB_GZ)], [gdn_norm_g],
        [(d_mixed, "head", D_HEAD, 8)], [(0,) + head_f, into_dp(1, CB_GZ)])
    d_qn, d_kn, dp, dfq_row, dfk_row = flash_bwd(qn, kn, vb, do_fox, frow, lse, delta, dp, S, tq)
    dp, d_qn_g, d_kn_g = fox_norm_bwd(p_all, d_qn, d_kn, fox_qn_g, fox_kn_g, dp, S, ts_wide)
    d_qkv, dsm_chunk, dgc_row = gdn_bwd(qkv, smalls, gc_row, states, tinv, do_gdn, S)
    dp, d_conv_w = conv_act_bwd(p_all, y_conv, d_qkv, conv_w, dp, S, ts_conv)
    d_f = jnp.transpose((dfq_row + dfk_row).reshape(H, S))
    d_gc = jnp.transpose(dgc_row, (0, 2, 1)).reshape(S, H)
    dsm = dsm_chunk + jnp.concatenate([d_f, d_gc, jnp.zeros((S, 128 - 16), F32)], axis=1)
    dp, d_pv, d_av = smalls_bwd(p_all, pv, av, dsm, dp, S, ts)
    g4_out = g_w_out.reshape(N_CHIPS, w_out.shape[1], D_MODEL)
    g_w_all, recv_out = matmul("g_w_in", h_b, dp, F32, tm=1024, tn=1664, tk=2048, trans_a=True,
                               rides=[("sibling_half", g4_out)])

    c_idx = jnp.reshape(ic, (1,)).astype(jnp.int32)

    def finish(tag, p4, far):
        own = lax.dynamic_index_in_dim(p4, chip, axis=0, keepdims=False)
        return sibling_merge("merge_" + tag, sum_parts("sum_" + tag, own, far, 128))

    g_int = g_w_all[None]
    (p_int,) = sum_own_half("sum2_w_in", g_int, sibling_send_other_half("d2d_w_in", g_int), c_idx, 128, rounded=False)
    p4_in = jnp.stack([_block_from_internal(p_int[0], j, n_in) for j in range(N_CHIPS)])
    p4_in_sent = p4_in.astype(MXU_DTYPE)
    p4_out, p4_out_sent = sum_own_half("sum2_w_out", g4_out, recv_out, c_idx, 128)
    d_h, far_in, far_out = matmul("d_h", dp, w_all, F32, tm=1024, tn=2048, tk=1664, trans_b=True,
                                  rides=[("scatter", p4_in_sent), ("scatter", p4_out_sent)])
    g_w_in_full = finish("w_in", p4_in, far_in)
    g_w_out_full = finish("w_out", p4_out, far_out)
    d_x, d_norm_g, d_scale, d_shift = rowwise_bwd(
        "prologue_bwd", f_prologue_res, S, ts_wide, 1, [wide(x2)], [norm_g, scale, shift],
        [wide(d_h), wide(dx_res)], [(0, "row", D_MODEL, 0, D_MODEL, F32)])

    parts = [d_shift, d_scale, d_gate, d_norm_g, d_final_g, d_conv_w.reshape(1, CONV_K * CONV_COLS), d_qn_g, d_kn_g,
             d_gdn_norm_g, d_pv, d_av, loss_acc]
    n_pay = sum(p.shape[1] for p in parts)
    pay_w = -(-n_pay // 1024) * 1024
    pay2 = jnp.concatenate(parts + [jnp.zeros((1, pay_w - n_pay), F32)], axis=1).reshape(8, pay_w // 8)
    g3 = allgather8("ag_small", pay2)
    tot = sum_devices("sum_small", g3).reshape(1, pay_w)
    dmod_all = g3.reshape(N_DEV, pay_w)[:, :3 * D_MODEL]
    o = 3 * D_MODEL
    g_b_ada = tot[:, :o]
    g_norm_g, g_final_g = tot[:, o:o + D_MODEL], tot[:, o + D_MODEL:o + 2 * D_MODEL]
    o += 2 * D_MODEL
    g_conv_full = tot[:, o:o + CONV_K * CONV_COLS].reshape(CONV_K, CONV_COLS)
    g_conv = lax.dynamic_slice_in_dim(g_conv_full, chip * conv_cols_chip, conv_cols_chip, axis=1)
    o += CONV_K * CONV_COLS
    g_qn_g, g_kn_g, g_gdn_ng = tot[:, o:o + 128], tot[:, o + 128:o + 256], tot[:, o + 256:o + 384]
    g_pv, g_av = tot[:, o + 384:o + 512], tot[:, o + 512:o + 640]
    loss = tot[0, o + 640]
    g_b_fgate, g_dt_bias, g_a_log = g_pv[:, :8], g_pv[:, 8:16], g_av[:, 8:16]

    small_g = [g_norm_g, g_b_ada, g_final_g, g_conv, g_qn_g, g_kn_g, g_gdn_ng, g_b_fgate, g_a_log, g_dt_bias]
    small_upd = adamw_small(
        "adamw_small",
        [norm_g, b_ada, final_g[None, :], gdn_conv_w[0], fox_qn_g, fox_kn_g, gdn_norm_g, b_fgate, gdn_A_log, gdn_dt_bias],
        small_g,
        [m_norm_g, m_b_ada, m_final_g[None, :], m_gdn_conv_w[0], m_fox_qn_g, m_fox_kn_g, m_gdn_norm_g, m_b_fgate,
         m_gdn_A_log, m_gdn_dt_bias],
        [v_norm_g, v_b_ada, v_final_g[None, :], v_gdn_conv_w[0], v_fox_qn_g, v_fox_kn_g, v_gdn_norm_g, v_b_fgate,
         v_gdn_A_log, v_gdn_dt_bias])

    def small_leaves(k):
        vals = small_g if k == 0 else [small_upd[3 * p + k - 1] for p in range(len(small_g))]
        return [v[0] if p == 2 else (v[None] if p == 3 else v) for p, v in enumerate(vals)]

    n_ada = w_ada.shape[2]
    g_w_ada = ada_grad(c_all, lax.dynamic_slice_in_dim(dmod_all, chip * n_ada, n_ada, axis=1))
    ada_out = adamw("adamw_w_ada", w_ada[0], [g_w_ada], m_w_ada[0], v_w_ada[0], 128)

    in_out = [jnp.transpose(o) for o in adamw("adamw_w_in", jnp.transpose(w_in[0]), [jnp.transpose(g_w_in_full)],
                                              jnp.transpose(m_w_in[0]), jnp.transpose(v_w_in[0]), None, tc=256)]
    out_out = adamw("adamw_w_out", w_out[0], [g_w_out_full], m_w_out[0], v_w_out[0], 128)

    res = [loss, d_x[None]]
    for k in range(4):
        s = small_leaves(k)
        big = [ada_out[k][None], in_out[k][None], out_out[k][None]]
        res += [s[0], big[0], s[1], big[1], s[7], s[4], s[5], s[3], s[8], s[9], s[6], big[2], s[2]]
    return tuple(res)
```
